```python
import jax, jax.numpy as jnp
from jax import lax
import numpy as np

D_MODEL = 1024
BATCH = 8
SEQ = 8192
DEPTH = 1

CHUNK = 64
D_MIX = D_MODEL
D_CONV = D_MIX // 2
N_HEADS = 8
HEAD_DIM = (D_MIX - D_CONV) // N_HEADS
D_ATT = N_HEADS * HEAD_DIM
CONV_WIDTH = 31
Q_BLOCK = 128
N_GROUPS = 4
EXPERTS_PER_GROUP = 8
N_EXPERTS = N_GROUPS * EXPERTS_PER_GROUP
TOP_K = 2
D_EXPERT = D_MODEL // 4
ROW_BLOCK = 256
EPS = 1e-6
D_IN = 2 * D_CONV + 3 * D_ATT + N_HEADS

kernel_name = 'hymba_conformer_fox_hiermoe'


def rms_norm(x, g):
    xf = x.astype(jnp.float32)
    y = xf * lax.rsqrt(jnp.mean(xf * xf, axis=-1, keepdims=True) + EPS)
    return (y * g.astype(jnp.float32)).astype(x.dtype)


def layer_norm(x, g, b):
    xf = x.astype(jnp.float32)
    mu = jnp.mean(xf, axis=-1, keepdims=True)
    xc = xf - mu
    y = xc * lax.rsqrt(jnp.mean(xc * xc, axis=-1, keepdims=True) + EPS)
    return (y * g.astype(jnp.float32) + b.astype(jnp.float32)).astype(x.dtype)


def conformer_conv(u_val, u_gate, w_dw, b_dw, ln_g, ln_b):
    a = u_val * jax.nn.sigmoid(u_gate)
    c = a.shape[-1]
    a = lax.conv_general_dilated(
        a, w_dw[:, None, :].astype(a.dtype), window_strides=(1,),
        padding=[(CONV_WIDTH - 1, 0)],
        dimension_numbers=('NWC', 'WIO', 'NWC'),
        feature_group_count=c) + b_dw.astype(a.dtype)
    a = layer_norm(a, ln_g, ln_b)
    return jax.nn.silu(a)


def forgetting_attention(q, k, v, log_f):
    s_len = q.shape[1]
    cum = jnp.transpose(jnp.cumsum(log_f, axis=1), (0, 2, 1))
    scale = HEAD_DIM ** -0.5
    outs = []
    for i in range(s_len // Q_BLOCK):
        q0 = i * Q_BLOCK
        k_end = q0 + Q_BLOCK
        s = jnp.einsum('bqhd,bkhd->bhqk', q[:, q0:k_end], k[:, :k_end],
                       preferred_element_type=jnp.float32) * scale
        s = s + cum[:, :, q0:k_end, None] - cum[:, :, None, :k_end]
        mask = (q0 + jnp.arange(Q_BLOCK))[:, None] >= jnp.arange(k_end)[None, :]
        s = jnp.where(mask, s, -jnp.inf)
        p = jax.nn.softmax(s, axis=-1)
        outs.append(jnp.einsum('bhqk,bkhd->bqhd', p.astype(v.dtype), v[:, :k_end]))
    return jnp.concatenate(outs, axis=1)


def hier_route(h, w_r1, b_r1, w_r2, b_r2):
    lg1 = (h @ w_r1).astype(jnp.float32) + b_r1.astype(jnp.float32)
    p1 = jax.nn.softmax(lg1, axis=-1)
    grp = jnp.argmax(lg1, axis=-1)
    p1_sel = jnp.take_along_axis(p1, grp[:, None], axis=1)[:, 0]
    lg2_all = jnp.einsum('td,gde->tge', h, w_r2).astype(jnp.float32) + b_r2.astype(jnp.float32)
    lg2 = jnp.take_along_axis(lg2_all, grp[:, None, None], axis=1)[:, 0]
    top_v, top_j = lax.top_k(lg2, TOP_K)
    p2 = jax.nn.softmax(top_v, axis=-1)
    weights = p1_sel[:, None] * p2
    experts = grp[:, None] * EXPERTS_PER_GROUP + top_j
    return experts.astype(jnp.int32), weights


def hier_moe(h, w_r1, b_r1, w_r2, b_r2, w_gate, w_up, w_down):
    t_len, d = h.shape
    experts, weights = hier_route(h, w_r1, b_r1, w_r2, b_r2)
    n_assign = t_len * TOP_K
    flat_e = experts.reshape(-1)
    flat_tok = jnp.repeat(jnp.arange(t_len, dtype=jnp.int32), TOP_K)
    flat_w = weights.reshape(-1)
    order = jnp.argsort(flat_e)
    se, stok, sw = flat_e[order], flat_tok[order], flat_w[order]
    counts = jnp.bincount(flat_e, length=N_EXPERTS)
    starts = jnp.cumsum(counts) - counts
    padded_counts = ((counts + ROW_BLOCK - 1) // ROW_BLOCK) * ROW_BLOCK
    padded_ends = jnp.cumsum(padded_counts)
    padded_starts = padded_ends - padded_counts
    dest = padded_starts[se] + (jnp.arange(n_assign) - starts[se])
    n_blocks = -(-n_assign // ROW_BLOCK) + N_EXPERTS
    n_rows = n_blocks * ROW_BLOCK
    row_tok = jnp.full((n_rows,), t_len, jnp.int32).at[dest].set(stok)
    row_w = jnp.zeros((n_rows,), jnp.float32).at[dest].set(sw)
    block_start = jnp.arange(n_blocks) * ROW_BLOCK
    block_e = jnp.minimum(jnp.searchsorted(padded_ends, block_start, side='right'), N_EXPERTS - 1)
    h_pad = jnp.concatenate([h, jnp.zeros((1, d), h.dtype)], axis=0)
    xin = h_pad[row_tok].reshape(n_blocks, ROW_BLOCK, d)

    def expert_block(args):
        xb, e = args
        g = xb @ w_gate[e]
        u = xb @ w_up[e]
        return (jax.nn.silu(g) * u) @ w_down[e]

    yb = lax.map(expert_block, (xin, block_e))
    y = yb.reshape(n_rows, d) * row_w[:, None].astype(yb.dtype)
    return jax.ops.segment_sum(y, row_tok, num_segments=t_len + 1)[:t_len]


def setup_inputs(seed: int = 0) -> dict:
    key = jax.random.key(seed)
    ks = jax.random.split(key, 24)
    f32 = jnp.float32
    L = DEPTH
    nrm = lambda k, shape, s: jax.random.normal(k, shape, f32) * s
    gain = lambda k, shape: 1.0 + 0.05 * jax.random.normal(k, shape, f32)
    b_f = jnp.linspace(1.0, 6.0, N_HEADS, dtype=f32)[None, :] + 0.1 * jax.random.normal(ks[3], (L, N_HEADS), f32)
    return {
        'x': jax.random.normal(ks[0], (BATCH, SEQ, D_MODEL), f32),
        'norm1_g': gain(ks[1], (L, D_MODEL)),
        'w_in': nrm(ks[2], (L, D_MODEL, D_IN), D_MODEL ** -0.5),
        'b_f': b_f,
        'w_dw': nrm(ks[4], (L, CONV_WIDTH, D_CONV), CONV_WIDTH ** -0.5),
        'b_dw': nrm(ks[5], (L, D_CONV), 0.02),
        'conv_ln_g': gain(ks[6], (L, D_CONV)),
        'conv_ln_b': nrm(ks[7], (L, D_CONV), 0.02),
        'out_g_conv': gain(ks[8], (L, D_CONV)),
        'out_g_att': gain(ks[9], (L, D_ATT)),
        'w_out': nrm(ks[10], (L, D_MIX, D_MODEL), D_MIX ** -0.5),
        'norm2_g': gain(ks[11], (L, D_MODEL)),
        'w_r1': nrm(ks[12], (L, D_MODEL, N_GROUPS), D_MODEL ** -0.5),
        'b_r1': nrm(ks[13], (L, N_GROUPS), 0.01),
        'w_r2': nrm(ks[14], (L, N_GROUPS, D_MODEL, EXPERTS_PER_GROUP), D_MODEL ** -0.5),
        'b_r2': nrm(ks[15], (L, N_GROUPS, EXPERTS_PER_GROUP), 0.01),
        'w_gate': nrm(ks[16], (L, N_EXPERTS, D_MODEL, D_EXPERT), D_MODEL ** -0.5),
        'w_up': nrm(ks[17], (L, N_EXPERTS, D_MODEL, D_EXPERT), D_MODEL ** -0.5),
        'w_down': nrm(ks[18], (L, N_EXPERTS, D_EXPERT, D_MODEL), D_EXPERT ** -0.5),
        'final_g': gain(ks[19], (D_MODEL,)),
    }


def reference(x, norm1_g, w_in, b_f, w_dw, b_dw, conv_ln_g, conv_ln_b, out_g_conv, out_g_att,
              w_out, norm2_g, w_r1, b_r1, w_r2, b_r2, w_gate, w_up, w_down, final_g):
    b, s, d = x.shape
    o1 = D_CONV
    o2 = o1 + D_CONV
    o3 = o2 + D_ATT
    o4 = o3 + D_ATT
    o5 = o4 + D_ATT
    for l in range(DEPTH):
        h = rms_norm(x, norm1_g[l])
        z = h @ w_in[l]
        y_conv = conformer_conv(z[..., :o1], z[..., o1:o2], w_dw[l], b_dw[l], conv_ln_g[l], conv_ln_b[l])
        q = z[..., o2:o3].reshape(b, s, N_HEADS, HEAD_DIM)
        k = z[..., o3:o4].reshape(b, s, N_HEADS, HEAD_DIM)
        v = z[..., o4:o5].reshape(b, s, N_HEADS, HEAD_DIM)
        log_f = jax.nn.log_sigmoid(z[..., o5:].astype(jnp.float32) + b_f[l].astype(jnp.float32))
        y_att = forgetting_attention(q, k, v, log_f).reshape(b, s, D_ATT)
        mix = jnp.concatenate([rms_norm(y_conv, out_g_conv[l]), rms_norm(y_att, out_g_att[l])], axis=-1)
        x = x + mix @ w_out[l]
        h2 = rms_norm(x, norm2_g[l]).reshape(b * s, d)
        x = x + hier_moe(h2, w_r1[l], b_r1[l], w_r2[l], b_r2[l], w_gate[l], w_up[l], w_down[l]).reshape(b, s, d)
    return rms_norm(x, final_g)
```

```python
import functools

import jax
import jax.numpy as jnp
from jax import lax
from jax.experimental import pallas as pl
from jax.experimental.pallas import tpu as pltpu

D_MODEL = 1024
D_CONV = 512
N_HEADS = 8
HEAD_DIM = 64
D_ATT = N_HEADS * HEAD_DIM
CONV_WIDTH = 31
N_GROUPS = 4
EXPERTS_PER_GROUP = 8
N_EXPERTS = N_GROUPS * EXPERTS_PER_GROUP
TOP_K = 2
D_EXPERT = D_MODEL // 4
ROW_BLOCK = 256
EPS = 1e-6

LANES = 128
KAUG = 128
N_PIECES = 3
PIECE_STRIDE = 8
CONV_HALO = 32
VMEM_LIMIT = 56 * 1024 * 1024

F32 = jnp.float32
BF16 = jnp.bfloat16


def _dot(a, b):
    return jnp.dot(a, b, preferred_element_type=F32)


def _dot_nt(a, b):
    return lax.dot_general(a, b, (((1,), (1,)), ((), ())), preferred_element_type=F32)


def _split3(x):
    hi = x.astype(BF16)
    r1 = x - hi.astype(F32)
    mid = r1.astype(BF16)
    lo = (r1 - mid.astype(F32)).astype(BF16)
    return hi.astype(F32), mid.astype(F32), lo.astype(F32)


def _piece_lane_mask(lane, h):
    return (lane == h) | (lane == h + PIECE_STRIDE) | (lane == h + 2 * PIECE_STRIDE)


def _inproj_kernel(x_ref, g1_ref, wvg_ref, wk_ref, wf_ref, bf_ref, wqT_ref, wvT_ref, ltri_ref,
                   a_ref, kp_ref, qpT_ref, vT_ref, carry_ref, *, tm, tk):
    @pl.when(pl.program_id(1) == 0)
    def _():
        carry_ref[...] = jnp.zeros_like(carry_ref)

    x = x_ref[...]
    ms = jnp.mean(x * x, axis=-1, keepdims=True)
    hb = ((x * lax.rsqrt(ms + EPS)) * g1_ref[...]).astype(BF16)

    zvg = _dot(hb, wvg_ref[...])
    a_ref[...] = zvg[:, :D_CONV] * jax.nn.sigmoid(zvg[:, D_CONV:])

    kk = _dot(hb, wk_ref[...])

    zf = _dot(hb, wf_ref[...]) + bf_ref[...]
    lf = jnp.minimum(zf, 0.0) - jnp.log1p(jnp.exp(-jnp.abs(zf)))
    lane = lax.broadcasted_iota(jnp.int32, (tm, LANES), 1)
    hi, mid, lo = _split3(lf)
    lf3 = jnp.where(lane < PIECE_STRIDE, hi,
                    jnp.where(lane < 2 * PIECE_STRIDE, mid,
                              jnp.where(lane < 3 * PIECE_STRIDE, lo, 0.0))).astype(BF16)
    cs3 = _dot(ltri_ref[...], lf3)
    c = (cs3 + pltpu.roll(cs3, LANES - PIECE_STRIDE, 1)
         + pltpu.roll(cs3, LANES - 2 * PIECE_STRIDE, 1)) + carry_ref[...]
    carry_ref[...] = c[tm - 1:tm, :]

    nhi, nmid, nlo = _split3(-c)
    p3 = jnp.where(lane < PIECE_STRIDE, nhi,
                   jnp.where(lane < 2 * PIECE_STRIDE, pltpu.roll(nmid, PIECE_STRIDE, 1),
                             jnp.where(lane < 3 * PIECE_STRIDE, pltpu.roll(nlo, 2 * PIECE_STRIDE, 1), 0.0)))
    p3_hi = pltpu.roll(p3, HEAD_DIM, 1)

    qT = _dot_nt(wqT_ref[...], hb)
    vT = _dot_nt(wvT_ref[...], hb)
    row = lax.broadcasted_iota(jnp.int32, (HEAD_DIM, tm), 0)

    for h in range(N_HEADS):
        kcol = kk[:, (h // 2) * LANES:(h // 2 + 1) * LANES]
        ones_h = jnp.where(_piece_lane_mask(row, h), 1.0, 0.0)
        q_h = qT[h * HEAD_DIM:(h + 1) * HEAD_DIM, :]
        if h % 2 == 0:
            ext = jnp.where(_piece_lane_mask(lane, h + HEAD_DIM), p3_hi, 0.0)
            kp = jnp.where(lane < HEAD_DIM, kcol, ext)
            qp = jnp.concatenate([q_h, ones_h], axis=0)
        else:
            ext = jnp.where(_piece_lane_mask(lane, h), p3, 0.0)
            kp = jnp.where(lane >= HEAD_DIM, kcol, ext)
            qp = jnp.concatenate([ones_h, q_h], axis=0)
        qpT_ref[0, h] = qp.astype(BF16)
        for cidx in range(tm // tk):
            kp_ref[0, h, cidx] = kp[cidx * tk:(cidx + 1) * tk, :].astype(BF16)
            vT_ref[0, h, cidx] = vT[h * HEAD_DIM:(h + 1) * HEAD_DIM, cidx * tk:(cidx + 1) * tk].astype(BF16)


def _inproj(x2, g1, wvg, wk, wf, bf3, wqT, wvT, *, batch, seq, tm, tk):
    nt = seq // tm
    nk = seq // tk
    ltri = jnp.tril(jnp.ones((tm, tm), F32)).astype(BF16)
    const = lambda shape: pl.BlockSpec(shape, lambda b, t: (0,) * len(shape))
    return pl.pallas_call(
        functools.partial(_inproj_kernel, tm=tm, tk=tk),
        grid=(batch, nt),
        in_specs=[
            pl.BlockSpec((tm, D_MODEL), lambda b, t: (b * nt + t, 0)),
            const((1, D_MODEL)), const((D_MODEL, 2 * D_CONV)), const((D_MODEL, D_ATT)),
            const((D_MODEL, LANES)), const((1, LANES)), const((D_ATT, D_MODEL)), const((D_ATT, D_MODEL)),
            const((tm, tm)),
        ],
        out_specs=[
            pl.BlockSpec((tm, D_CONV), lambda b, t: (b * nt + t, 0)),
            pl.BlockSpec((1, N_HEADS, tm // tk, tk, KAUG), lambda b, t: (b, 0, t, 0, 0)),
            pl.BlockSpec((1, N_HEADS, KAUG, tm), lambda b, t: (b, 0, 0, t)),
            pl.BlockSpec((1, N_HEADS, tm // tk, HEAD_DIM, tk), lambda b, t: (b, 0, t, 0, 0)),
        ],
        out_shape=[
            jax.ShapeDtypeStruct((batch * seq, D_CONV), F32),
            jax.ShapeDtypeStruct((batch, N_HEADS, nk, tk, KAUG), BF16),
            jax.ShapeDtypeStruct((batch, N_HEADS, KAUG, seq), BF16),
            jax.ShapeDtypeStruct((batch, N_HEADS, nk, HEAD_DIM, tk), BF16),
        ],
        scratch_shapes=[pltpu.VMEM((1, LANES), F32)],
        compiler_params=pltpu.CompilerParams(
            dimension_semantics=("arbitrary", "arbitrary"), vmem_limit_bytes=VMEM_LIMIT),
        name="inproj",
    )(x2, g1, wvg, wk, wf, bf3, wqT, wvT, ltri)


def _conv_kernel(a_ref, w_ref, b_ref, lng_ref, lnb_ref, og_ref, out_ref, buf_ref, acc_ref, *, tc, rc):
    @pl.when(pl.program_id(1) == 0)
    def _():
        buf_ref[0:CONV_HALO, :] = jnp.zeros((CONV_HALO, D_CONV), F32)

    @pl.when(pl.program_id(1) > 0)
    def _():
        buf_ref[0:CONV_HALO, :] = buf_ref[tc:tc + CONV_HALO, :]

    buf_ref[CONV_HALO:CONV_HALO + tc, :] = a_ref[...]

    base = CONV_HALO - (CONV_WIDTH - 1)
    for cb in range(D_CONV // LANES):
        cols = slice(cb * LANES, (cb + 1) * LANES)
        for r0 in range(0, tc, rc):
            acc = jnp.zeros((rc, LANES), F32)
            for j in range(CONV_WIDTH):
                acc = acc + w_ref[j:j + 1, cols] * buf_ref[r0 + base + j:r0 + base + j + rc, cols]
            acc_ref[r0:r0 + rc, cols] = acc

    y = acc_ref[...] + b_ref[...]
    mu = jnp.mean(y, axis=-1, keepdims=True)
    yc = y - mu
    var = jnp.mean(yc * yc, axis=-1, keepdims=True)
    yn = yc * lax.rsqrt(var + EPS) * lng_ref[...] + lnb_ref[...]
    s = yn * jax.nn.sigmoid(yn)
    ms = jnp.mean(s * s, axis=-1, keepdims=True)
    out_ref[...] = (s * lax.rsqrt(ms + EPS) * og_ref[...]).astype(BF16)


def _conv(a, w_dw, b_dw, ln_g, ln_b, og, *, batch, seq, tc, rc):
    nt = seq // tc
    const = lambda shape: pl.BlockSpec(shape, lambda b, t: (0,) * len(shape))
    return pl.pallas_call(
        functools.partial(_conv_kernel, tc=tc, rc=rc),
        grid=(batch, nt),
        in_specs=[pl.BlockSpec((tc, D_CONV), lambda b, t: (b * nt + t, 0)),
                  const((CONV_WIDTH, D_CONV)), const((1, D_CONV)), const((1, D_CONV)),
                  const((1, D_CONV)), const((1, D_CONV))],
        out_specs=pl.BlockSpec((tc, D_CONV), lambda b, t: (b * nt + t, 0)),
        out_shape=jax.ShapeDtypeStruct((batch * seq, D_CONV), BF16),
        scratch_shapes=[pltpu.VMEM((tc + CONV_HALO, D_CONV), F32), pltpu.VMEM((tc, D_CONV), F32)],
        compiler_params=pltpu.CompilerParams(
            dimension_semantics=("arbitrary", "arbitrary"), vmem_limit_bytes=VMEM_LIMIT),
        name="conv",
    )(a, w_dw, b_dw, ln_g, ln_b, og)


def _attn_kernel(qT_ref, k_ref, v_ref, o_ref, *, tq, tk):
    i = pl.program_id(2)
    qT = qT_ref[0, 0]
    n_sub = tq // tk
    q_idx = i * tq + lax.broadcasted_iota(jnp.int32, (tk, tq), 1)
    k_loc = lax.broadcasted_iota(jnp.int32, (tk, tq), 0)

    def step(j, carry, masked):
        m, l, acc = carry
        s = _dot(k_ref[0, 0, j], qT)
        if masked:
            s = jnp.where(j * tk + k_loc <= q_idx, s, -jnp.inf)
        m_new = jnp.maximum(m, jnp.max(s, axis=0, keepdims=True))
        alpha = jnp.exp(m - m_new)
        p = jnp.exp(s - m_new)
        l = alpha * l + jnp.sum(p, axis=0, keepdims=True)
        acc = alpha * acc + _dot(v_ref[0, 0, j], p.astype(BF16))
        return m_new, l, acc

    init = (jnp.full((1, tq), -jnp.inf, F32), jnp.zeros((1, tq), F32), jnp.zeros((HEAD_DIM, tq), F32))
    carry = lax.fori_loop(0, i * n_sub, lambda j, c: step(j, c, False), init)
    for d in range(n_sub):
        carry = step(i * n_sub + d, carry, True)
    _, l, acc = carry
    o_ref[0] = acc / l


def _attention(qpT, kp, vT, *, batch, seq, tq, tk):
    nk = seq // tk
    return pl.pallas_call(
        functools.partial(_attn_kernel, tq=tq, tk=tk),
        grid=(batch, N_HEADS, seq // tq),
        in_specs=[
            pl.BlockSpec((1, 1, KAUG, tq), lambda b, h, i: (b, h, 0, i)),
            pl.BlockSpec((1, 1, nk, tk, KAUG), lambda b, h, i: (b, h, 0, 0, 0)),
            pl.BlockSpec((1, 1, nk, HEAD_DIM, tk), lambda b, h, i: (b, h, 0, 0, 0)),
        ],
        out_specs=pl.BlockSpec((1, HEAD_DIM, tq), lambda b, h, i: (b, h, i)),
        out_shape=jax.ShapeDtypeStruct((batch, D_ATT, seq), F32),
        compiler_params=pltpu.CompilerParams(
            dimension_semantics=("arbitrary", "arbitrary", "arbitrary"), vmem_limit_bytes=VMEM_LIMIT),
        name="attention",
    )(qpT, kp, vT)


def _outproj_kernel(x_ref, mc_ref, yT_ref, woc_ref, woa_ref, ga_ref, g2_ref, wr_ref, br_ref, ltri_ref,
                    x1_ref, h2_ref, route_ref, cnt_ref, carry_ref, *, tm):
    first = (pl.program_id(0) == 0) & (pl.program_id(1) == 0)

    @pl.when(first)
    def _():
        carry_ref[...] = jnp.zeros_like(carry_ref)

    yT = yT_ref[0]
    msa = jnp.mean(yT * yT, axis=0, keepdims=True)
    yn = (yT * lax.rsqrt(msa + EPS)).T * ga_ref[...]
    x1 = x_ref[...] + _dot(mc_ref[...], woc_ref[...]) + _dot(yn.astype(BF16), woa_ref[...])
    x1_ref[...] = x1
    ms = jnp.mean(x1 * x1, axis=-1, keepdims=True)
    h2 = (x1 * lax.rsqrt(ms + EPS)) * g2_ref[...]
    h2_ref[...] = h2

    lg = _dot(h2.astype(BF16), wr_ref[...]) + br_ref[...]
    lane = lax.broadcasted_iota(jnp.int32, (tm, LANES), 1)
    neg = -jnp.inf
    big = jnp.int32(LANES)

    lg1 = jnp.where(lane < N_GROUPS, lg, neg)
    m1 = jnp.max(lg1, axis=-1, keepdims=True)
    p1_sel = 1.0 / jnp.sum(jnp.exp(lg1 - m1), axis=-1, keepdims=True)
    grp = jnp.min(jnp.where(lg1 == m1, lane, big), axis=-1, keepdims=True)

    lo_lane = N_GROUPS + EXPERTS_PER_GROUP * grp
    v = jnp.where((lane >= lo_lane) & (lane < lo_lane + EXPERTS_PER_GROUP), lg, neg)
    v1 = jnp.max(v, axis=-1, keepdims=True)
    j1 = jnp.min(jnp.where(v == v1, lane, big), axis=-1, keepdims=True)
    vv = jnp.where(lane == j1, neg, v)
    v2 = jnp.max(vv, axis=-1, keepdims=True)
    j2 = jnp.min(jnp.where(vv == v2, lane, big), axis=-1, keepdims=True)
    e21 = jnp.exp(v2 - v1)
    w0 = p1_sel / (1.0 + e21)
    w1 = p1_sel * e21 / (1.0 + e21)
    e0 = j1 - N_GROUPS
    e1 = j2 - N_GROUPS

    oh0 = lane == e0
    oh1 = lane == e1
    cmat = jnp.where(oh0 | oh1, 1.0, 0.0)
    prefix = _dot(ltri_ref[...], cmat.astype(BF16)) + carry_ref[...]
    r0 = jnp.sum(jnp.where(oh0, prefix, 0.0), axis=-1, keepdims=True)
    r1 = jnp.sum(jnp.where(oh1, prefix, 0.0), axis=-1, keepdims=True)
    carry = carry_ref[...] + jnp.sum(cmat, axis=0, keepdims=True)
    carry_ref[...] = carry
    cnt_ref[...] = jnp.broadcast_to(carry, cnt_ref.shape)

    route = jnp.where(lane == 0, e0.astype(F32),
            jnp.where(lane == 1, e1.astype(F32),
            jnp.where(lane == 2, w0,
            jnp.where(lane == 3, w1,
            jnp.where(lane == 4, r0,
            jnp.where(lane == 5, r1, 0.0))))))
    route_ref[...] = route


def _outproj(x2, mc, yT, woc, woa, ga, g2, wr, br, *, batch, seq, tm):
    nt = seq // tm
    ltri = jnp.tril(jnp.ones((tm, tm), F32), -1).astype(BF16)
    const = lambda shape: pl.BlockSpec(shape, lambda b, t: (0,) * len(shape))
    row_spec = lambda w: pl.BlockSpec((tm, w), lambda b, t: (b * nt + t, 0))
    return pl.pallas_call(
        functools.partial(_outproj_kernel, tm=tm),
        grid=(batch, nt),
        in_specs=[row_spec(D_MODEL), row_spec(D_CONV),
                  pl.BlockSpec((1, D_ATT, tm), lambda b, t: (b, 0, t)),
                  const((D_CONV, D_MODEL)), const((D_ATT, D_MODEL)), const((1, D_ATT)), const((1, D_MODEL)),
                  const((D_MODEL, LANES)), const((1, LANES)), const((tm, tm))],
        out_specs=[row_spec(D_MODEL), row_spec(D_MODEL), row_spec(LANES), const((8, LANES))],
        out_shape=[jax.ShapeDtypeStruct((batch * seq, D_MODEL), F32),
                   jax.ShapeDtypeStruct((batch * seq, D_MODEL), F32),
                   jax.ShapeDtypeStruct((batch * seq, LANES), F32),
                   jax.ShapeDtypeStruct((8, LANES), F32)],
        scratch_shapes=[pltpu.VMEM((1, LANES), F32)],
        compiler_params=pltpu.CompilerParams(
            dimension_semantics=("arbitrary", "arbitrary"), vmem_limit_bytes=VMEM_LIMIT),
        name="outproj",
    )(x2, mc, yT, woc, woa, ga, g2, wr, br, ltri)


def _row_copies(src_at, dst_at, sem):
    return pltpu.make_async_copy(src_at, dst_at, sem)


def _dispatch_kernel(dest_ref, h2_ref, xs_in_ref, xs_ref, sem, *, tm):
    del xs_in_ref

    def body(r, carry):
        for k in range(TOP_K):
            d = dest_ref[TOP_K * r + k]
            _row_copies(h2_ref.at[pl.ds(r, 1)], xs_ref.at[pl.ds(d, 1)], sem).start()
        return carry

    lax.fori_loop(0, tm, body, 0, unroll=8)
    for _ in range(TOP_K):
        _row_copies(h2_ref, xs_ref.at[pl.ds(0, tm)], sem).wait()


def _dispatch(dest, h2, xs0, *, tm):
    t_len = h2.shape[0]
    return pl.pallas_call(
        functools.partial(_dispatch_kernel, tm=tm),
        grid=(t_len // tm,),
        in_specs=[pl.BlockSpec((TOP_K * tm,), lambda i: (i,), memory_space=pltpu.SMEM),
                  pl.BlockSpec((tm, D_MODEL), lambda i: (i, 0)),
                  pl.BlockSpec(memory_space=pl.ANY)],
        out_specs=pl.BlockSpec(memory_space=pl.ANY),
        out_shape=jax.ShapeDtypeStruct(xs0.shape, xs0.dtype),
        scratch_shapes=[pltpu.SemaphoreType.DMA(())],
        input_output_aliases={2: 0},
        compiler_params=pltpu.CompilerParams(dimension_semantics=("arbitrary",), vmem_limit_bytes=VMEM_LIMIT),
        name="dispatch",
    )(dest, h2, xs0)


def _expert_kernel(be_ref, x_ref, wg_ref, wu_ref, wd_ref, y_ref):
    del be_ref
    xb = x_ref[...].astype(BF16)
    g = _dot(xb, wg_ref[0])
    u = _dot(xb, wu_ref[0])
    hmid = (g * jax.nn.sigmoid(g)) * u
    y_ref[...] = _dot(hmid.astype(BF16), wd_ref[0])


def _experts(block_e, xs, wg, wu, wd):
    n_rows = xs.shape[0]
    n_blocks = n_rows // ROW_BLOCK
    grid_spec = pltpu.PrefetchScalarGridSpec(
        num_scalar_prefetch=1,
        grid=(n_blocks,),
        in_specs=[pl.BlockSpec((ROW_BLOCK, D_MODEL), lambda i, be: (i, 0)),
                  pl.BlockSpec((1, D_MODEL, D_EXPERT), lambda i, be: (be[i], 0, 0)),
                  pl.BlockSpec((1, D_MODEL, D_EXPERT), lambda i, be: (be[i], 0, 0)),
                  pl.BlockSpec((1, D_EXPERT, D_MODEL), lambda i, be: (be[i], 0, 0))],
        out_specs=pl.BlockSpec((ROW_BLOCK, D_MODEL), lambda i, be: (i, 0)),
    )
    return pl.pallas_call(
        _expert_kernel,
        grid_spec=grid_spec,
        out_shape=jax.ShapeDtypeStruct((n_rows, D_MODEL), F32),
        compiler_params=pltpu.CompilerParams(dimension_semantics=("arbitrary",), vmem_limit_bytes=VMEM_LIMIT),
        name="experts",
    )(block_e, xs, wg, wu, wd)


def _combine_kernel(dest_ref, x1_ref, route_ref, g_ref, ys_ref, out_ref, buf_ref, sem, *, tm):
    def body(r, carry):
        for k in range(TOP_K):
            d = dest_ref[TOP_K * r + k]
            _row_copies(ys_ref.at[pl.ds(d, 1)], buf_ref.at[k, pl.ds(r, 1)], sem).start()
        return carry

    lax.fori_loop(0, tm, body, 0, unroll=8)
    for k in range(TOP_K):
        _row_copies(ys_ref.at[pl.ds(0, tm)], buf_ref.at[k], sem).wait()

    route = route_ref[...]
    moe = route[:, 2:3] * buf_ref[0] + route[:, 3:4] * buf_ref[1]
    x2 = x1_ref[...] + moe
    ms = jnp.mean(x2 * x2, axis=-1, keepdims=True)
    out_ref[...] = (x2 * lax.rsqrt(ms + EPS)) * g_ref[...]


def _combine(dest, x1, route, gf, ys, *, tm):
    t_len = x1.shape[0]
    return pl.pallas_call(
        functools.partial(_combine_kernel, tm=tm),
        grid=(t_len // tm,),
        in_specs=[pl.BlockSpec((TOP_K * tm,), lambda i: (i,), memory_space=pltpu.SMEM),
                  pl.BlockSpec((tm, D_MODEL), lambda i: (i, 0)),
                  pl.BlockSpec((tm, LANES), lambda i: (i, 0)),
                  pl.BlockSpec((1, D_MODEL), lambda i: (0, 0)),
                  pl.BlockSpec(memory_space=pl.ANY)],
        out_specs=pl.BlockSpec((tm, D_MODEL), lambda i: (i, 0)),
        out_shape=jax.ShapeDtypeStruct((t_len, D_MODEL), F32),
        scratch_shapes=[pltpu.VMEM((TOP_K, tm, D_MODEL), F32), pltpu.SemaphoreType.DMA(())],
        compiler_params=pltpu.CompilerParams(dimension_semantics=("arbitrary",), vmem_limit_bytes=VMEM_LIMIT),
        name="combine",
    )(dest, x1, route, gf, ys)


def _layer(x, norm1_g, w_in, b_f, w_dw, b_dw, conv_ln_g, conv_ln_b, out_g_conv, out_g_att, w_out,
           norm2_g, w_r1, b_r1, w_r2, b_r2, w_gate, w_up, w_down):
    batch, seq, d = x.shape
    t_len = batch * seq
    tm = min(512, seq)
    tk = min(256, seq)
    tq = min(512, seq)
    o1, o2 = D_CONV, 2 * D_CONV
    o3, o4, o5 = o2 + D_ATT, o2 + 2 * D_ATT, o2 + 3 * D_ATT

    wvg = w_in[:, :o2].astype(BF16)
    wk = w_in[:, o3:o4].astype(BF16)
    wqT = (w_in[:, o2:o3] * (HEAD_DIM ** -0.5)).T.astype(BF16)
    wvT = w_in[:, o4:o5].T.astype(BF16)
    wf = jnp.zeros((d, LANES), F32)
    bf3 = jnp.zeros((1, LANES), F32)
    for p in range(N_PIECES):
        wf = wf.at[:, p * PIECE_STRIDE:p * PIECE_STRIDE + N_HEADS].set(w_in[:, o5:])
        bf3 = bf3.at[0, p * PIECE_STRIDE:p * PIECE_STRIDE + N_HEADS].set(b_f.astype(F32))
    wf = wf.astype(BF16)

    x2 = x.reshape(t_len, d)
    a, kp, qpT, vT = _inproj(x2, norm1_g.reshape(1, d), wvg, wk, wf, bf3, wqT, wvT,
                             batch=batch, seq=seq, tm=tm, tk=tk)
    mc = _conv(a, w_dw, b_dw.reshape(1, -1), conv_ln_g.reshape(1, -1), conv_ln_b.reshape(1, -1),
               out_g_conv.reshape(1, -1), batch=batch, seq=seq, tc=min(256, seq), rc=32)
    yT = _attention(qpT, kp, vT, batch=batch, seq=seq, tq=tq, tk=tk)

    wr = jnp.zeros((d, LANES), F32)
    wr = wr.at[:, :N_GROUPS].set(w_r1)
    wr = wr.at[:, N_GROUPS:N_GROUPS + N_EXPERTS].set(jnp.transpose(w_r2, (1, 0, 2)).reshape(d, N_EXPERTS))
    br = jnp.zeros((1, LANES), F32)
    br = br.at[0, :N_GROUPS].set(b_r1.astype(F32))
    br = br.at[0, N_GROUPS:N_GROUPS + N_EXPERTS].set(b_r2.reshape(-1).astype(F32))
    x1, h2, route, cnt = _outproj(x2, mc, yT, w_out[:D_CONV].astype(BF16), w_out[D_CONV:].astype(BF16),
                                  out_g_att.reshape(1, -1), norm2_g.reshape(1, d), wr.astype(BF16), br,
                                  batch=batch, seq=seq, tm=tm)

    counts = cnt[0, :N_EXPERTS].astype(jnp.int32)
    padded = ((counts + ROW_BLOCK - 1) // ROW_BLOCK) * ROW_BLOCK
    ends = jnp.cumsum(padded)
    starts = ends - padded
    n_blocks = -(-(t_len * TOP_K) // ROW_BLOCK) + N_EXPERTS
    block_e = jnp.minimum(jnp.searchsorted(ends, jnp.arange(n_blocks) * ROW_BLOCK, side='right'),
                          N_EXPERTS - 1).astype(jnp.int32)
    experts = route[:, 0:TOP_K].astype(jnp.int32)
    ranks = route[:, 4:4 + TOP_K].astype(jnp.int32)
    dest = (starts[experts] + ranks).reshape(-1).astype(jnp.int32)

    tr = min(256, t_len)
    xs = _dispatch(dest, h2, jnp.zeros((n_blocks * ROW_BLOCK, d), F32), tm=tr)
    ys = _experts(block_e, xs, w_gate.astype(BF16), w_up.astype(BF16), w_down.astype(BF16))
    return dest, x1, route, ys, tr


def kernel(x, norm1_g, w_in, b_f, w_dw, b_dw, conv_ln_g, conv_ln_b, out_g_conv, out_g_att, w_out, norm2_g,
           w_r1, b_r1, w_r2, b_r2, w_gate, w_up, w_down, final_g):
    assert norm1_g.shape[0] == 1, "single-layer stack"
    batch, seq, d = x.shape
    dest, x1, route, ys, tr = _layer(
        x, norm1_g[0], w_in[0], b_f[0], w_dw[0], b_dw[0], conv_ln_g[0], conv_ln_b[0], out_g_conv[0],
        out_g_att[0], w_out[0], norm2_g[0], w_r1[0], b_r1[0], w_r2[0], b_r2[0], w_gate[0], w_up[0], w_down[0])
    out = _combine(dest, x1, route, final_g.reshape(1, d), ys, tm=tr)
    return out.reshape(batch, seq, d)
```

```python
import functools

import jax
import jax.numpy as jnp
from jax import lax
from jax.experimental import pallas as pl
from jax.experimental.pallas import tpu as pltpu

D_MODEL = 1024
D_CONV = 512
N_HEADS = 8
HEAD_DIM = 64
D_ATT = N_HEADS * HEAD_DIM
CONV_WIDTH = 31
N_GROUPS = 4
EXPERTS_PER_GROUP = 8
N_EXPERTS = N_GROUPS * EXPERTS_PER_GROUP
TOP_K = 2
D_EXPERT = D_MODEL // 4
ROW_BLOCK = 256
EPS = 1e-6

LANES = 128
KAUG = 128
N_PIECES = 3
PIECE_STRIDE = 8
CONV_HALO = 32
V_ROWS = 80
LOG2E = 1.4426950408889634
VMEM_LIMIT = 56 * 1024 * 1024

F32 = jnp.float32
BF16 = jnp.bfloat16


def _dot(a, b):
    return jnp.dot(a, b, preferred_element_type=F32)


def _dot_nt(a, b):
    return lax.dot_general(a, b, (((1,), (1,)), ((), ())), preferred_element_type=F32)


def _split3(x):
    hi = x.astype(BF16)
    r1 = x - hi.astype(F32)
    mid = r1.astype(BF16)
    lo = (r1 - mid.astype(F32)).astype(BF16)
    return hi.astype(F32), mid.astype(F32), lo.astype(F32)


def _piece_lane_mask(lane, h):
    return (lane == h) | (lane == h + PIECE_STRIDE) | (lane == h + 2 * PIECE_STRIDE)


def _inproj_kernel(x_ref, g1_ref, wvg_ref, wk_ref, wf_ref, bf_ref, wqT_ref, wvT_ref, ltri_ref,
                   a_ref, kp_ref, qpT_ref, vT_ref, carry_ref, *, tm, tk):
    @pl.when(pl.program_id(1) == 0)
    def _():
        carry_ref[...] = jnp.zeros_like(carry_ref)

    x = x_ref[...]
    ms = jnp.mean(x * x, axis=-1, keepdims=True)
    hb = ((x * lax.rsqrt(ms + EPS)) * g1_ref[...]).astype(BF16)

    zvg = _dot(hb, wvg_ref[...])
    a_ref[...] = zvg[:, :D_CONV] * jax.nn.sigmoid(zvg[:, D_CONV:])

    kk = _dot(hb, wk_ref[...])

    zf = _dot(hb, wf_ref[...]) + bf_ref[...]
    lf = jnp.minimum(zf, 0.0) - jnp.log1p(jnp.exp(-jnp.abs(zf)))
    lane = lax.broadcasted_iota(jnp.int32, (tm, LANES), 1)
    hi, mid, lo = _split3(lf)
    lf3 = jnp.where(lane < PIECE_STRIDE, hi,
                    jnp.where(lane < 2 * PIECE_STRIDE, mid,
                              jnp.where(lane < 3 * PIECE_STRIDE, lo, 0.0))).astype(BF16)
    cs3 = _dot(ltri_ref[...], lf3)
    c = (cs3 + pltpu.roll(cs3, LANES - PIECE_STRIDE, 1)
         + pltpu.roll(cs3, LANES - 2 * PIECE_STRIDE, 1)) + carry_ref[...]
    carry_ref[...] = c[tm - 1:tm, :]

    nhi, nmid, nlo = _split3(c * (-LOG2E))
    p3 = jnp.where(lane < PIECE_STRIDE, nhi,
                   jnp.where(lane < 2 * PIECE_STRIDE, pltpu.roll(nmid, PIECE_STRIDE, 1),
                             jnp.where(lane < 3 * PIECE_STRIDE, pltpu.roll(nlo, 2 * PIECE_STRIDE, 1), 0.0)))
    p3_hi = pltpu.roll(p3, HEAD_DIM, 1)

    qT = _dot_nt(wqT_ref[...], hb) * LOG2E
    vT = _dot_nt(wvT_ref[...], hb)
    row = lax.broadcasted_iota(jnp.int32, (HEAD_DIM, tm), 0)
    vrow = lax.broadcasted_iota(jnp.int32, (V_ROWS - HEAD_DIM, tk), 0)
    v_tail = jnp.where(vrow == 0, 1.0, 0.0).astype(BF16)

    for h in range(N_HEADS):
        kcol = kk[:, (h // 2) * LANES:(h // 2 + 1) * LANES]
        ones_h = jnp.where(_piece_lane_mask(row, h), 1.0, 0.0)
        q_h = qT[h * HEAD_DIM:(h + 1) * HEAD_DIM, :]
        if h % 2 == 0:
            ext = jnp.where(_piece_lane_mask(lane, h + HEAD_DIM), p3_hi, 0.0)
            kp = jnp.where(lane < HEAD_DIM, kcol, ext)
            qp = jnp.concatenate([q_h, ones_h], axis=0)
        else:
            ext = jnp.where(_piece_lane_mask(lane, h), p3, 0.0)
            kp = jnp.where(lane >= HEAD_DIM, kcol, ext)
            qp = jnp.concatenate([ones_h, q_h], axis=0)
        qpT_ref[0, h] = qp.astype(BF16)
        for cidx in range(tm // tk):
            kp_ref[0, h, cidx] = kp[cidx * tk:(cidx + 1) * tk, :].astype(BF16)
            v_h = vT[h * HEAD_DIM:(h + 1) * HEAD_DIM, cidx * tk:(cidx + 1) * tk].astype(BF16)
            vT_ref[0, h, cidx] = jnp.concatenate([v_h, v_tail], axis=0)


def _inproj(x2, g1, wvg, wk, wf, bf3, wqT, wvT, *, batch, seq, tm, tk):
    nt = seq // tm
    nk = seq // tk
    ltri = jnp.tril(jnp.ones((tm, tm), F32)).astype(BF16)
    const = lambda shape: pl.BlockSpec(shape, lambda b, t: (0,) * len(shape))
    return pl.pallas_call(
        functools.partial(_inproj_kernel, tm=tm, tk=tk),
        grid=(batch, nt),
        in_specs=[
            pl.BlockSpec((tm, D_MODEL), lambda b, t: (b * nt + t, 0)),
            const((1, D_MODEL)), const((D_MODEL, 2 * D_CONV)), const((D_MODEL, D_ATT)),
            const((D_MODEL, LANES)), const((1, LANES)), const((D_ATT, D_MODEL)), const((D_ATT, D_MODEL)),
            const((tm, tm)),
        ],
        out_specs=[
            pl.BlockSpec((tm, D_CONV), lambda b, t: (b * nt + t, 0)),
            pl.BlockSpec((1, N_HEADS, tm // tk, tk, KAUG), lambda b, t: (b, 0, t, 0, 0)),
            pl.BlockSpec((1, N_HEADS, KAUG, tm), lambda b, t: (b, 0, 0, t)),
            pl.BlockSpec((1, N_HEADS, tm // tk, V_ROWS, tk), lambda b, t: (b, 0, t, 0, 0)),
        ],
        out_shape=[
            jax.ShapeDtypeStruct((batch * seq, D_CONV), F32),
            jax.ShapeDtypeStruct((batch, N_HEADS, nk, tk, KAUG), BF16),
            jax.ShapeDtypeStruct((batch, N_HEADS, KAUG, seq), BF16),
            jax.ShapeDtypeStruct((batch, N_HEADS, nk, V_ROWS, tk), BF16),
        ],
        scratch_shapes=[pltpu.VMEM((1, LANES), F32)],
        compiler_params=pltpu.CompilerParams(
            dimension_semantics=("arbitrary", "arbitrary"), vmem_limit_bytes=VMEM_LIMIT),
        name="inproj",
    )(x2, g1, wvg, wk, wf, bf3, wqT, wvT, ltri)


def _conv_kernel(a_ref, w_ref, b_ref, lng_ref, lnb_ref, og_ref, out_ref, buf_ref, acc_ref, *, tc, rc):
    @pl.when(pl.program_id(1) == 0)
    def _():
        buf_ref[0:CONV_HALO, :] = jnp.zeros((CONV_HALO, D_CONV), F32)

    @pl.when(pl.program_id(1) > 0)
    def _():
        buf_ref[0:CONV_HALO, :] = buf_ref[tc:tc + CONV_HALO, :]

    buf_ref[CONV_HALO:CONV_HALO + tc, :] = a_ref[...]

    base = CONV_HALO - (CONV_WIDTH - 1)
    for cb in range(D_CONV // LANES):
        cols = slice(cb * LANES, (cb + 1) * LANES)
        for r0 in range(0, tc, rc):
            acc = jnp.zeros((rc, LANES), F32)
            for j in range(CONV_WIDTH):
                acc = acc + w_ref[j:j + 1, cols] * buf_ref[r0 + base + j:r0 + base + j + rc, cols]
            acc_ref[r0:r0 + rc, cols] = acc

    y = acc_ref[...] + b_ref[...]
    mu = jnp.mean(y, axis=-1, keepdims=True)
    yc = y - mu
    var = jnp.mean(yc * yc, axis=-1, keepdims=True)
    yn = yc * lax.rsqrt(var + EPS) * lng_ref[...] + lnb_ref[...]
    s = yn * jax.nn.sigmoid(yn)
    ms = jnp.mean(s * s, axis=-1, keepdims=True)
    out_ref[...] = (s * lax.rsqrt(ms + EPS) * og_ref[...]).astype(BF16)


def _conv(a, w_dw, b_dw, ln_g, ln_b, og, *, batch, seq, tc, rc):
    nt = seq // tc
    const = lambda shape: pl.BlockSpec(shape, lambda b, t: (0,) * len(shape))
    return pl.pallas_call(
        functools.partial(_conv_kernel, tc=tc, rc=rc),
        grid=(batch, nt),
        in_specs=[pl.BlockSpec((tc, D_CONV), lambda b, t: (b * nt + t, 0)),
                  const((CONV_WIDTH, D_CONV)), const((1, D_CONV)), const((1, D_CONV)),
                  const((1, D_CONV)), const((1, D_CONV))],
        out_specs=pl.BlockSpec((tc, D_CONV), lambda b, t: (b * nt + t, 0)),
        out_shape=jax.ShapeDtypeStruct((batch * seq, D_CONV), BF16),
        scratch_shapes=[pltpu.VMEM((tc + CONV_HALO, D_CONV), F32), pltpu.VMEM((tc, D_CONV), F32)],
        compiler_params=pltpu.CompilerParams(
            dimension_semantics=("arbitrary", "arbitrary"), vmem_limit_bytes=VMEM_LIMIT),
        name="conv",
    )(a, w_dw, b_dw, ln_g, ln_b, og)


def _attn_kernel(qT_ref, k_ref, v_ref, o_ref, *, tq, tk):
    i = pl.program_id(2)
    qT = qT_ref[0, 0]
    n_sub = tq // tk

    def update(m, acc, s, cmax, v):
        m_new = jnp.maximum(m, cmax)
        p = jnp.exp2(s - m_new).astype(BF16)
        return m_new, jnp.exp2(m - m_new) * acc + _dot(v, p)

    def group(g, carry):
        m, acc = carry
        base = g * n_sub
        scores = [_dot(k_ref[0, 0, base + d], qT) for d in range(n_sub)]
        cmaxes = [jnp.max(s, axis=0, keepdims=True) for s in scores]
        for d in range(n_sub):
            m, acc = update(m, acc, scores[d], cmaxes[d], v_ref[0, 0, base + d])
        return m, acc

    init = (jnp.full((1, tq), -jnp.inf, F32), jnp.zeros((V_ROWS, tq), F32))
    m, acc = lax.fori_loop(0, i, group, init)

    key = lax.broadcasted_iota(jnp.int32, (tk, tk), 0)
    qry = lax.broadcasted_iota(jnp.int32, (tk, tk), 1)
    for d in range(n_sub):
        lo = d * tk
        s = _dot(k_ref[0, 0, i * n_sub + d], qT[:, lo:])
        s_diag = jnp.where(key <= qry, s[:, :tk], -jnp.inf)
        s = s_diag if lo + tk == tq else jnp.concatenate([s_diag, s[:, tk:]], axis=1)
        m_d, acc_d = update(m[:, lo:], acc[:, lo:], s, jnp.max(s, axis=0, keepdims=True),
                            v_ref[0, 0, i * n_sub + d])
        if lo == 0:
            m, acc = m_d, acc_d
        else:
            m = jnp.concatenate([m[:, :lo], m_d], axis=1)
            acc = jnp.concatenate([acc[:, :lo], acc_d], axis=1)
    o_ref[0] = acc[:HEAD_DIM] / acc[HEAD_DIM:HEAD_DIM + 1]


def _attention(qpT, kp, vT, *, batch, seq, tq, tk):
    nk = seq // tk
    return pl.pallas_call(
        functools.partial(_attn_kernel, tq=tq, tk=tk),
        grid=(batch, N_HEADS, seq // tq),
        in_specs=[
            pl.BlockSpec((1, 1, KAUG, tq), lambda b, h, i: (b, h, 0, i)),
            pl.BlockSpec((1, 1, nk, tk, KAUG), lambda b, h, i: (b, h, 0, 0, 0)),
            pl.BlockSpec((1, 1, nk, V_ROWS, tk), lambda b, h, i: (b, h, 0, 0, 0)),
        ],
        out_specs=pl.BlockSpec((1, HEAD_DIM, tq), lambda b, h, i: (b, h, i)),
        out_shape=jax.ShapeDtypeStruct((batch, D_ATT, seq), F32),
        compiler_params=pltpu.CompilerParams(
            dimension_semantics=("arbitrary", "arbitrary", "arbitrary"), vmem_limit_bytes=VMEM_LIMIT),
        name="attention",
    )(qpT, kp, vT)


def _outproj_kernel(x_ref, mc_ref, yT_ref, woc_ref, woa_ref, ga_ref, g2_ref, wr_ref, br_ref, ltri_ref,
                    x1_ref, h2_ref, route_ref, cnt_ref, carry_ref, *, tm):
    first = (pl.program_id(0) == 0) & (pl.program_id(1) == 0)

    @pl.when(first)
    def _():
        carry_ref[...] = jnp.zeros_like(carry_ref)

    yT = yT_ref[0]
    msa = jnp.mean(yT * yT, axis=0, keepdims=True)
    yn = (yT * lax.rsqrt(msa + EPS)).T * ga_ref[...]
    x1 = x_ref[...] + _dot(mc_ref[...], woc_ref[...]) + _dot(yn.astype(BF16), woa_ref[...])
    x1_ref[...] = x1
    ms = jnp.mean(x1 * x1, axis=-1, keepdims=True)
    h2 = (x1 * lax.rsqrt(ms + EPS)) * g2_ref[...]
    h2_ref[...] = h2

    lg = _dot(h2.astype(BF16), wr_ref[...]) + br_ref[...]
    lane = lax.broadcasted_iota(jnp.int32, (tm, LANES), 1)
    neg = -jnp.inf
    big = jnp.int32(LANES)

    lg1 = jnp.where(lane < N_GROUPS, lg, neg)
    m1 = jnp.max(lg1, axis=-1, keepdims=True)
    p1_sel = 1.0 / jnp.sum(jnp.exp(lg1 - m1), axis=-1, keepdims=True)
    grp = jnp.min(jnp.where(lg1 == m1, lane, big), axis=-1, keepdims=True)

    lo_lane = N_GROUPS + EXPERTS_PER_GROUP * grp
    v = jnp.where((lane >= lo_lane) & (lane < lo_lane + EXPERTS_PER_GROUP), lg, neg)
    v1 = jnp.max(v, axis=-1, keepdims=True)
    j1 = jnp.min(jnp.where(v == v1, lane, big), axis=-1, keepdims=True)
    vv = jnp.where(lane == j1, neg, v)
    v2 = jnp.max(vv, axis=-1, keepdims=True)
    j2 = jnp.min(jnp.where(vv == v2, lane, big), axis=-1, keepdims=True)
    e21 = jnp.exp(v2 - v1)
    w0 = p1_sel / (1.0 + e21)
    w1 = p1_sel * e21 / (1.0 + e21)
    e0 = j1 - N_GROUPS
    e1 = j2 - N_GROUPS

    oh0 = lane == e0
    oh1 = lane == e1
    cmat = jnp.where(oh0 | oh1, 1.0, 0.0)
    prefix = _dot(ltri_ref[...], cmat.astype(BF16)) + carry_ref[...]
    r0 = jnp.sum(jnp.where(oh0, prefix, 0.0), axis=-1, keepdims=True)
    r1 = jnp.sum(jnp.where(oh1, prefix, 0.0), axis=-1, keepdims=True)
    carry = carry_ref[...] + jnp.sum(cmat, axis=0, keepdims=True)
    carry_ref[...] = carry
    cnt_ref[...] = jnp.broadcast_to(carry, cnt_ref.shape)

    route = jnp.where(lane == 0, e0.astype(F32),
            jnp.where(lane == 1, e1.astype(F32),
            jnp.where(lane == 2, w0,
            jnp.where(lane == 3, w1,
            jnp.where(lane == 4, r0,
            jnp.where(lane == 5, r1, 0.0))))))
    route_ref[...] = route


def _outproj(x2, mc, yT, woc, woa, ga, g2, wr, br, *, batch, seq, tm):
    nt = seq // tm
    ltri = jnp.tril(jnp.ones((tm, tm), F32), -1).astype(BF16)
    const = lambda shape: pl.BlockSpec(shape, lambda b, t: (0,) * len(shape))
    row_spec = lambda w: pl.BlockSpec((tm, w), lambda b, t: (b * nt + t, 0))
    return pl.pallas_call(
        functools.partial(_outproj_kernel, tm=tm),
        grid=(batch, nt),
        in_specs=[row_spec(D_MODEL), row_spec(D_CONV),
                  pl.BlockSpec((1, D_ATT, tm), lambda b, t: (b, 0, t)),
                  const((D_CONV, D_MODEL)), const((D_ATT, D_MODEL)), const((1, D_ATT)), const((1, D_MODEL)),
                  const((D_MODEL, LANES)), const((1, LANES)), const((tm, tm))],
        out_specs=[row_spec(D_MODEL), row_spec(D_MODEL), row_spec(LANES), const((8, LANES))],
        out_shape=[jax.ShapeDtypeStruct((batch * seq, D_MODEL), F32),
                   jax.ShapeDtypeStruct((batch * seq, D_MODEL), F32),
                   jax.ShapeDtypeStruct((batch * seq, LANES), F32),
                   jax.ShapeDtypeStruct((8, LANES), F32)],
        scratch_shapes=[pltpu.VMEM((1, LANES), F32)],
        compiler_params=pltpu.CompilerParams(
            dimension_semantics=("arbitrary", "arbitrary"), vmem_limit_bytes=VMEM_LIMIT),
        name="outproj",
    )(x2, mc, yT, woc, woa, ga, g2, wr, br, ltri)


def _row_copies(src_at, dst_at, sem):
    return pltpu.make_async_copy(src_at, dst_at, sem)


def _dispatch_kernel(dest_ref, h2_ref, xs_in_ref, xs_ref, sem, *, tm):
    del xs_in_ref

    def body(r, carry):
        for k in range(TOP_K):
            d = dest_ref[TOP_K * r + k]
            _row_copies(h2_ref.at[pl.ds(r, 1)], xs_ref.at[pl.ds(d, 1)], sem).start()
        return carry

    lax.fori_loop(0, tm, body, 0, unroll=8)
    for _ in range(TOP_K):
        _row_copies(h2_ref, xs_ref.at[pl.ds(0, tm)], sem).wait()


def _dispatch(dest, h2, xs0, *, tm):
    t_len = h2.shape[0]
    return pl.pallas_call(
        functools.partial(_dispatch_kernel, tm=tm),
        grid=(t_len // tm,),
        in_specs=[pl.BlockSpec((TOP_K * tm,), lambda i: (i,), memory_space=pltpu.SMEM),
                  pl.BlockSpec((tm, D_MODEL), lambda i: (i, 0)),
                  pl.BlockSpec(memory_space=pl.ANY)],
        out_specs=pl.BlockSpec(memory_space=pl.ANY),
        out_shape=jax.ShapeDtypeStruct(xs0.shape, xs0.dtype),
        scratch_shapes=[pltpu.SemaphoreType.DMA(())],
        input_output_aliases={2: 0},
        compiler_params=pltpu.CompilerParams(dimension_semantics=("arbitrary",), vmem_limit_bytes=VMEM_LIMIT),
        name="dispatch",
    )(dest, h2, xs0)


def _expert_kernel(be_ref, x_ref, wg_ref, wu_ref, wd_ref, y_ref):
    del be_ref
    xb = x_ref[...].astype(BF16)
    g = _dot(xb, wg_ref[0])
    u = _dot(xb, wu_ref[0])
    hmid = (g * jax.nn.sigmoid(g)) * u
    y_ref[...] = _dot(hmid.astype(BF16), wd_ref[0])


def _experts(block_e, xs, wg, wu, wd):
    n_rows = xs.shape[0]
    n_blocks = n_rows // ROW_BLOCK
    grid_spec = pltpu.PrefetchScalarGridSpec(
        num_scalar_prefetch=1,
        grid=(n_blocks,),
        in_specs=[pl.BlockSpec((ROW_BLOCK, D_MODEL), lambda i, be: (i, 0)),
                  pl.BlockSpec((1, D_MODEL, D_EXPERT), lambda i, be: (be[i], 0, 0)),
                  pl.BlockSpec((1, D_MODEL, D_EXPERT), lambda i, be: (be[i], 0, 0)),
                  pl.BlockSpec((1, D_EXPERT, D_MODEL), lambda i, be: (be[i], 0, 0))],
        out_specs=pl.BlockSpec((ROW_BLOCK, D_MODEL), lambda i, be: (i, 0)),
    )
    return pl.pallas_call(
        _expert_kernel,
        grid_spec=grid_spec,
        out_shape=jax.ShapeDtypeStruct((n_rows, D_MODEL), F32),
        compiler_params=pltpu.CompilerParams(dimension_semantics=("arbitrary",), vmem_limit_bytes=VMEM_LIMIT),
        name="experts",
    )(block_e, xs, wg, wu, wd)


def _combine_kernel(dest_ref, x1_ref, route_ref, g_ref, ys_ref, out_ref, buf_ref, sem, *, tm):
    def body(r, carry):
        for k in range(TOP_K):
            d = dest_ref[TOP_K * r + k]
            _row_copies(ys_ref.at[pl.ds(d, 1)], buf_ref.at[k, pl.ds(r, 1)], sem).start()
        return carry

    lax.fori_loop(0, tm, body, 0, unroll=8)
    for k in range(TOP_K):
        _row_copies(ys_ref.at[pl.ds(0, tm)], buf_ref.at[k], sem).wait()

    route = route_ref[...]
    moe = route[:, 2:3] * buf_ref[0] + route[:, 3:4] * buf_ref[1]
    x2 = x1_ref[...] + moe
    ms = jnp.mean(x2 * x2, axis=-1, keepdims=True)
    out_ref[...] = (x2 * lax.rsqrt(ms + EPS)) * g_ref[...]


def _combine(dest, x1, route, gf, ys, *, tm):
    t_len = x1.shape[0]
    return pl.pallas_call(
        functools.partial(_combine_kernel, tm=tm),
        grid=(t_len // tm,),
        in_specs=[pl.BlockSpec((TOP_K * tm,), lambda i: (i,), memory_space=pltpu.SMEM),
                  pl.BlockSpec((tm, D_MODEL), lambda i: (i, 0)),
                  pl.BlockSpec((tm, LANES), lambda i: (i, 0)),
                  pl.BlockSpec((1, D_MODEL), lambda i: (0, 0)),
                  pl.BlockSpec(memory_space=pl.ANY)],
        out_specs=pl.BlockSpec((tm, D_MODEL), lambda i: (i, 0)),
        out_shape=jax.ShapeDtypeStruct((t_len, D_MODEL), F32),
        scratch_shapes=[pltpu.VMEM((TOP_K, tm, D_MODEL), F32), pltpu.SemaphoreType.DMA(())],
        compiler_params=pltpu.CompilerParams(dimension_semantics=("arbitrary",), vmem_limit_bytes=VMEM_LIMIT),
        name="combine",
    )(dest, x1, route, gf, ys)


def _layer(x, norm1_g, w_in, b_f, w_dw, b_dw, conv_ln_g, conv_ln_b, out_g_conv, out_g_att, w_out,
           norm2_g, w_r1, b_r1, w_r2, b_r2, w_gate, w_up, w_down):
    batch, seq, d = x.shape
    t_len = batch * seq
    tm = min(512, seq)
    tk = min(256, seq)
    tq = min(1024, seq)
    o1, o2 = D_CONV, 2 * D_CONV
    o3, o4, o5 = o2 + D_ATT, o2 + 2 * D_ATT, o2 + 3 * D_ATT

    wvg = w_in[:, :o2].astype(BF16)
    wk = w_in[:, o3:o4].astype(BF16)
    wqT = (w_in[:, o2:o3] * (HEAD_DIM ** -0.5)).T.astype(BF16)
    wvT = w_in[:, o4:o5].T.astype(BF16)
    assert PIECE_STRIDE == N_HEADS
    pad = LANES - N_PIECES * PIECE_STRIDE
    wf = jnp.pad(jnp.tile(w_in[:, o5:], (1, N_PIECES)), ((0, 0), (0, pad))).astype(BF16)
    bf3 = jnp.pad(jnp.tile(b_f.astype(F32).reshape(1, N_HEADS), (1, N_PIECES)), ((0, 0), (0, pad)))

    x2 = x.reshape(t_len, d)
    a, kp, qpT, vT = _inproj(x2, norm1_g.reshape(1, d), wvg, wk, wf, bf3, wqT, wvT,
                             batch=batch, seq=seq, tm=tm, tk=tk)
    mc = _conv(a, w_dw, b_dw.reshape(1, -1), conv_ln_g.reshape(1, -1), conv_ln_b.reshape(1, -1),
               out_g_conv.reshape(1, -1), batch=batch, seq=seq, tc=min(256, seq), rc=32)
    yT = _attention(qpT, kp, vT, batch=batch, seq=seq, tq=tq, tk=tk)

    rpad = LANES - N_GROUPS - N_EXPERTS
    wr = jnp.pad(jnp.concatenate([w_r1, jnp.transpose(w_r2, (1, 0, 2)).reshape(d, N_EXPERTS)], axis=1),
                 ((0, 0), (0, rpad)))
    br = jnp.pad(jnp.concatenate([b_r1.astype(F32), b_r2.reshape(-1).astype(F32)]).reshape(1, -1),
                 ((0, 0), (0, rpad)))
    x1, h2, route, cnt = _outproj(x2, mc, yT, w_out[:D_CONV].astype(BF16), w_out[D_CONV:].astype(BF16),
                                  out_g_att.reshape(1, -1), norm2_g.reshape(1, d), wr.astype(BF16), br,
                                  batch=batch, seq=seq, tm=tm)

    counts = cnt[0, :N_EXPERTS].astype(jnp.int32)
    padded = ((counts + ROW_BLOCK - 1) // ROW_BLOCK) * ROW_BLOCK
    ends = jnp.cumsum(padded)
    starts = ends - padded
    n_blocks = -(-(t_len * TOP_K) // ROW_BLOCK) + N_EXPERTS
    block_start = jnp.arange(n_blocks, dtype=jnp.int32) * ROW_BLOCK
    block_e = jnp.minimum(jnp.sum(ends[None, :] <= block_start[:, None], axis=1), N_EXPERTS - 1).astype(jnp.int32)
    experts = route[:, 0:TOP_K].astype(jnp.int32)
    ranks = route[:, 4:4 + TOP_K].astype(jnp.int32)
    onehot = experts[:, :, None] == jnp.arange(N_EXPERTS, dtype=jnp.int32)
    dest = (jnp.sum(jnp.where(onehot, starts, 0), axis=-1) + ranks).reshape(-1).astype(jnp.int32)

    tr = min(256, t_len)
    xs = _dispatch(dest, h2, jnp.zeros((n_blocks * ROW_BLOCK, d), F32), tm=tr)
    ys = _experts(block_e, xs, w_gate.astype(BF16), w_up.astype(BF16), w_down.astype(BF16))
    return dest, x1, route, ys, tr


def kernel(x, norm1_g, w_in, b_f, w_dw, b_dw, conv_ln_g, conv_ln_b, out_g_conv, out_g_att, w_out, norm2_g,
           w_r1, b_r1, w_r2, b_r2, w_gate, w_up, w_down, final_g):
    assert norm1_g.shape[0] == 1, "single-layer stack"
    batch, seq, d = x.shape
    dest, x1, route, ys, tr = _layer(
        x, norm1_g[0], w_in[0], b_f[0], w_dw[0], b_dw[0], conv_ln_g[0], conv_ln_b[0], out_g_conv[0],
        out_g_att[0], w_out[0], norm2_g[0], w_r1[0], b_r1[0], w_r2[0], b_r2[0], w_gate[0], w_up[0], w_down[0])
    out = _combine(dest, x1, route, final_g.reshape(1, d), ys, tm=tr)
    return out.reshape(batch, seq, d)
```

```python
import functools

import jax
import jax.numpy as jnp
from jax import lax
from jax.experimental import pallas as pl
from jax.experimental.pallas import tpu as pltpu

D_MODEL = 1024
D_CONV = 512
N_HEADS = 8
HEAD_DIM = 64
D_ATT = N_HEADS * HEAD_DIM
CONV_WIDTH = 31
N_GROUPS = 4
EXPERTS_PER_GROUP = 8
N_EXPERTS = N_GROUPS * EXPERTS_PER_GROUP
TOP_K = 2
D_EXPERT = D_MODEL // 4
ROW_BLOCK = 256
EPS = 1e-6

LANES = 128
KAUG = 128
N_PIECES = 3
PIECE_STRIDE = 8
CONV_HALO = 32
PAIRS_PER_GROUP = EXPERTS_PER_GROUP * (EXPERTS_PER_GROUP - 1) // 2
N_CLASSES = N_GROUPS * PAIRS_PER_GROUP
ROW_W = D_MODEL + LANES
ISSUE_UNROLL = 8
V_ROWS = 80
LOG2E = 1.4426950408889634
VMEM_LIMIT = 56 * 1024 * 1024

F32 = jnp.float32
BF16 = jnp.bfloat16


def _dot(a, b):
    return jnp.dot(a, b, preferred_element_type=F32)


def _dot_nt(a, b):
    return lax.dot_general(a, b, (((1,), (1,)), ((), ())), preferred_element_type=F32)


def _split3(x):
    hi = x.astype(BF16)
    r1 = x - hi.astype(F32)
    mid = r1.astype(BF16)
    lo = (r1 - mid.astype(F32)).astype(BF16)
    return hi.astype(F32), mid.astype(F32), lo.astype(F32)


def _piece_lane_mask(lane, h):
    return (lane == h) | (lane == h + PIECE_STRIDE) | (lane == h + 2 * PIECE_STRIDE)


def _inproj_kernel(x_ref, g1_ref, wvg_ref, wk_ref, wf_ref, bf_ref, wqT_ref, wvT_ref, ltri_ref,
                   a_ref, kp_ref, qpT_ref, vT_ref, carry_ref, *, tm, tk):
    @pl.when(pl.program_id(1) == 0)
    def _():
        carry_ref[...] = jnp.zeros_like(carry_ref)

    x = x_ref[...]
    ms = jnp.mean(x * x, axis=-1, keepdims=True)
    hb = ((x * lax.rsqrt(ms + EPS)) * g1_ref[...]).astype(BF16)

    zvg = _dot(hb, wvg_ref[...])
    a_ref[...] = zvg[:, :D_CONV] * jax.nn.sigmoid(zvg[:, D_CONV:])

    kk = _dot(hb, wk_ref[...])

    zf = _dot(hb, wf_ref[...]) + bf_ref[...]
    lf = jnp.minimum(zf, 0.0) - jnp.log1p(jnp.exp(-jnp.abs(zf)))
    lane = lax.broadcasted_iota(jnp.int32, (tm, LANES), 1)
    hi, mid, lo = _split3(lf)
    lf3 = jnp.where(lane < PIECE_STRIDE, hi,
                    jnp.where(lane < 2 * PIECE_STRIDE, mid,
                              jnp.where(lane < 3 * PIECE_STRIDE, lo, 0.0))).astype(BF16)
    cs3 = _dot(ltri_ref[...], lf3)
    c = (cs3 + pltpu.roll(cs3, LANES - PIECE_STRIDE, 1)
         + pltpu.roll(cs3, LANES - 2 * PIECE_STRIDE, 1)) + carry_ref[...]
    carry_ref[...] = c[tm - 1:tm, :]

    nhi, nmid, nlo = _split3(c * (-LOG2E))
    p3 = jnp.where(lane < PIECE_STRIDE, nhi,
                   jnp.where(lane < 2 * PIECE_STRIDE, pltpu.roll(nmid, PIECE_STRIDE, 1),
                             jnp.where(lane < 3 * PIECE_STRIDE, pltpu.roll(nlo, 2 * PIECE_STRIDE, 1), 0.0)))
    p3_hi = pltpu.roll(p3, HEAD_DIM, 1)

    qT = _dot_nt(wqT_ref[...], hb) * LOG2E
    vT = _dot_nt(wvT_ref[...], hb)
    row = lax.broadcasted_iota(jnp.int32, (HEAD_DIM, tm), 0)
    vrow = lax.broadcasted_iota(jnp.int32, (V_ROWS - HEAD_DIM, tk), 0)
    v_tail = jnp.where(vrow == 0, 1.0, 0.0).astype(BF16)

    for h in range(N_HEADS):
        kcol = kk[:, (h // 2) * LANES:(h // 2 + 1) * LANES]
        ones_h = jnp.where(_piece_lane_mask(row, h), 1.0, 0.0)
        q_h = qT[h * HEAD_DIM:(h + 1) * HEAD_DIM, :]
        if h % 2 == 0:
            ext = jnp.where(_piece_lane_mask(lane, h + HEAD_DIM), p3_hi, 0.0)
            kp = jnp.where(lane < HEAD_DIM, kcol, ext)
            qp = jnp.concatenate([q_h, ones_h], axis=0)
        else:
            ext = jnp.where(_piece_lane_mask(lane, h), p3, 0.0)
            kp = jnp.where(lane >= HEAD_DIM, kcol, ext)
            qp = jnp.concatenate([ones_h, q_h], axis=0)
        qpT_ref[0, h] = qp.astype(BF16)
        for cidx in range(tm // tk):
            kp_ref[0, h, cidx] = kp[cidx * tk:(cidx + 1) * tk, :].astype(BF16)
            v_h = vT[h * HEAD_DIM:(h + 1) * HEAD_DIM, cidx * tk:(cidx + 1) * tk].astype(BF16)
            vT_ref[0, h, cidx] = jnp.concatenate([v_h, v_tail], axis=0)


def _inproj(x2, g1, wvg, wk, wf, bf3, wqT, wvT, *, batch, seq, tm, tk):
    nt = seq // tm
    nk = seq // tk
    ltri = jnp.tril(jnp.ones((tm, tm), F32)).astype(BF16)
    const = lambda shape: pl.BlockSpec(shape, lambda b, t: (0,) * len(shape))
    return pl.pallas_call(
        functools.partial(_inproj_kernel, tm=tm, tk=tk),
        grid=(batch, nt),
        in_specs=[
            pl.BlockSpec((tm, D_MODEL), lambda b, t: (b * nt + t, 0)),
            const((1, D_MODEL)), const((D_MODEL, 2 * D_CONV)), const((D_MODEL, D_ATT)),
            const((D_MODEL, LANES)), const((1, LANES)), const((D_ATT, D_MODEL)), const((D_ATT, D_MODEL)),
            const((tm, tm)),
        ],
        out_specs=[
            pl.BlockSpec((tm, D_CONV), lambda b, t: (b * nt + t, 0)),
            pl.BlockSpec((1, N_HEADS, tm // tk, tk, KAUG), lambda b, t: (b, 0, t, 0, 0)),
            pl.BlockSpec((1, N_HEADS, KAUG, tm), lambda b, t: (b, 0, 0, t)),
            pl.BlockSpec((1, N_HEADS, tm // tk, V_ROWS, tk), lambda b, t: (b, 0, t, 0, 0)),
        ],
        out_shape=[
            jax.ShapeDtypeStruct((batch * seq, D_CONV), F32),
            jax.ShapeDtypeStruct((batch, N_HEADS, nk, tk, KAUG), BF16),
            jax.ShapeDtypeStruct((batch, N_HEADS, KAUG, seq), BF16),
            jax.ShapeDtypeStruct((batch, N_HEADS, nk, V_ROWS, tk), BF16),
        ],
        scratch_shapes=[pltpu.VMEM((1, LANES), F32)],
        compiler_params=pltpu.CompilerParams(
            dimension_semantics=("arbitrary", "arbitrary"), vmem_limit_bytes=VMEM_LIMIT),
        name="inproj",
    )(x2, g1, wvg, wk, wf, bf3, wqT, wvT, ltri)


def _conv_kernel(a_ref, w_ref, b_ref, lng_ref, lnb_ref, og_ref, out_ref, buf_ref, acc_ref, *, tc, rc):
    @pl.when(pl.program_id(1) == 0)
    def _():
        buf_ref[0:CONV_HALO, :] = jnp.zeros((CONV_HALO, D_CONV), F32)

    @pl.when(pl.program_id(1) > 0)
    def _():
        buf_ref[0:CONV_HALO, :] = buf_ref[tc:tc + CONV_HALO, :]

    buf_ref[CONV_HALO:CONV_HALO + tc, :] = a_ref[...]

    base = CONV_HALO - (CONV_WIDTH - 1)
    for cb in range(D_CONV // LANES):
        cols = slice(cb * LANES, (cb + 1) * LANES)
        for r0 in range(0, tc, rc):
            acc = jnp.zeros((rc, LANES), F32)
            for j in range(CONV_WIDTH):
                acc = acc + w_ref[j:j + 1, cols] * buf_ref[r0 + base + j:r0 + base + j + rc, cols]
            acc_ref[r0:r0 + rc, cols] = acc

    y = acc_ref[...] + b_ref[...]
    mu = jnp.mean(y, axis=-1, keepdims=True)
    yc = y - mu
    var = jnp.mean(yc * yc, axis=-1, keepdims=True)
    yn = yc * lax.rsqrt(var + EPS) * lng_ref[...] + lnb_ref[...]
    s = yn * jax.nn.sigmoid(yn)
    ms = jnp.mean(s * s, axis=-1, keepdims=True)
    out_ref[...] = (s * lax.rsqrt(ms + EPS) * og_ref[...]).astype(BF16)


def _conv(a, w_dw, b_dw, ln_g, ln_b, og, *, batch, seq, tc, rc):
    nt = seq // tc
    const = lambda shape: pl.BlockSpec(shape, lambda b, t: (0,) * len(shape))
    return pl.pallas_call(
        functools.partial(_conv_kernel, tc=tc, rc=rc),
        grid=(batch, nt),
        in_specs=[pl.BlockSpec((tc, D_CONV), lambda b, t: (b * nt + t, 0)),
                  const((CONV_WIDTH, D_CONV)), const((1, D_CONV)), const((1, D_CONV)),
                  const((1, D_CONV)), const((1, D_CONV))],
        out_specs=pl.BlockSpec((tc, D_CONV), lambda b, t: (b * nt + t, 0)),
        out_shape=jax.ShapeDtypeStruct((batch * seq, D_CONV), BF16),
        scratch_shapes=[pltpu.VMEM((tc + CONV_HALO, D_CONV), F32), pltpu.VMEM((tc, D_CONV), F32)],
        compiler_params=pltpu.CompilerParams(
            dimension_semantics=("arbitrary", "arbitrary"), vmem_limit_bytes=VMEM_LIMIT),
        name="conv",
    )(a, w_dw, b_dw, ln_g, ln_b, og)


def _attn_kernel(qT_ref, k_ref, v_ref, o_ref, *, tq, tk):
    i = pl.program_id(2)
    qT = qT_ref[0, 0]
    n_sub = tq // tk

    def update(m, acc, s, cmax, v):
        m_new = jnp.maximum(m, cmax)
        p = jnp.exp2(s - m_new).astype(BF16)
        return m_new, jnp.exp2(m - m_new) * acc + _dot(v, p)

    def group(g, carry):
        m, acc = carry
        base = g * n_sub
        scores = [_dot(k_ref[0, 0, base + d], qT) for d in range(n_sub)]
        cmaxes = [jnp.max(s, axis=0, keepdims=True) for s in scores]
        for d in range(n_sub):
            m, acc = update(m, acc, scores[d], cmaxes[d], v_ref[0, 0, base + d])
        return m, acc

    init = (jnp.full((1, tq), -jnp.inf, F32), jnp.zeros((V_ROWS, tq), F32))
    m, acc = lax.fori_loop(0, i, group, init)

    key = lax.broadcasted_iota(jnp.int32, (tk, tk), 0)
    qry = lax.broadcasted_iota(jnp.int32, (tk, tk), 1)
    for d in range(n_sub):
        lo = d * tk
        s = _dot(k_ref[0, 0, i * n_sub + d], qT[:, lo:])
        s_diag = jnp.where(key <= qry, s[:, :tk], -jnp.inf)
        s = s_diag if lo + tk == tq else jnp.concatenate([s_diag, s[:, tk:]], axis=1)
        m_d, acc_d = update(m[:, lo:], acc[:, lo:], s, jnp.max(s, axis=0, keepdims=True),
                            v_ref[0, 0, i * n_sub + d])
        if lo == 0:
            m, acc = m_d, acc_d
        else:
            m = jnp.concatenate([m[:, :lo], m_d], axis=1)
            acc = jnp.concatenate([acc[:, :lo], acc_d], axis=1)
    o_ref[0] = acc[:HEAD_DIM] / acc[HEAD_DIM:HEAD_DIM + 1]


def _attention(qpT, kp, vT, *, batch, seq, tq, tk):
    nk = seq // tk
    return pl.pallas_call(
        functools.partial(_attn_kernel, tq=tq, tk=tk),
        grid=(batch, N_HEADS, seq // tq),
        in_specs=[
            pl.BlockSpec((1, 1, KAUG, tq), lambda b, h, i: (b, h, 0, i)),
            pl.BlockSpec((1, 1, nk, tk, KAUG), lambda b, h, i: (b, h, 0, 0, 0)),
            pl.BlockSpec((1, 1, nk, V_ROWS, tk), lambda b, h, i: (b, h, 0, 0, 0)),
        ],
        out_specs=pl.BlockSpec((1, HEAD_DIM, tq), lambda b, h, i: (b, h, i)),
        out_shape=jax.ShapeDtypeStruct((batch, D_ATT, seq), F32),
        compiler_params=pltpu.CompilerParams(
            dimension_semantics=("arbitrary", "arbitrary", "arbitrary"), vmem_limit_bytes=VMEM_LIMIT),
        name="attention",
    )(qpT, kp, vT)


def _outproj_kernel(x_ref, mc_ref, yT_ref, woc_ref, woa_ref, ga_ref, g2_ref, wr_ref, br_ref, ltri_ref,
                    x1_ref, h2_ref, route_ref, cnt_ref, carry_ref, *, tm):
    first = (pl.program_id(0) == 0) & (pl.program_id(1) == 0)

    @pl.when(first)
    def _():
        carry_ref[...] = jnp.zeros_like(carry_ref)

    yT = yT_ref[0]
    msa = jnp.mean(yT * yT, axis=0, keepdims=True)
    yn = (yT * lax.rsqrt(msa + EPS)).T * ga_ref[...]
    x1 = x_ref[...] + _dot(mc_ref[...], woc_ref[...]) + _dot(yn.astype(BF16), woa_ref[...])
    x1_ref[...] = x1
    ms = jnp.mean(x1 * x1, axis=-1, keepdims=True)
    h2 = (x1 * lax.rsqrt(ms + EPS)) * g2_ref[...]
    h2_ref[:, :D_MODEL] = h2

    lg = _dot(h2.astype(BF16), wr_ref[...]) + br_ref[...]
    lane = lax.broadcasted_iota(jnp.int32, (tm, LANES), 1)
    neg = -jnp.inf
    big = jnp.int32(LANES)

    lg1 = jnp.where(lane < N_GROUPS, lg, neg)
    m1 = jnp.max(lg1, axis=-1, keepdims=True)
    p1_sel = 1.0 / jnp.sum(jnp.exp(lg1 - m1), axis=-1, keepdims=True)
    grp = jnp.min(jnp.where(lg1 == m1, lane, big), axis=-1, keepdims=True)

    lo_lane = N_GROUPS + EXPERTS_PER_GROUP * grp
    v = jnp.where((lane >= lo_lane) & (lane < lo_lane + EXPERTS_PER_GROUP), lg, neg)
    v1 = jnp.max(v, axis=-1, keepdims=True)
    j1 = jnp.min(jnp.where(v == v1, lane, big), axis=-1, keepdims=True)
    vv = jnp.where(lane == j1, neg, v)
    v2 = jnp.max(vv, axis=-1, keepdims=True)
    j2 = jnp.min(jnp.where(vv == v2, lane, big), axis=-1, keepdims=True)
    e21 = jnp.exp(v2 - v1)
    w0 = p1_sel / (1.0 + e21)
    w1 = p1_sel * e21 / (1.0 + e21)
    e0 = j1 - N_GROUPS
    e1 = j2 - N_GROUPS

    swap = e1 < e0
    ea = jnp.where(swap, e1, e0)
    eb = jnp.where(swap, e0, e1)
    wa = jnp.where(swap, w1, w0)
    wb = jnp.where(swap, w0, w1)
    al = ea - EXPERTS_PER_GROUP * grp
    bl = eb - EXPERTS_PER_GROUP * grp
    cid = PAIRS_PER_GROUP * grp + ((al * (2 * EXPERTS_PER_GROUP - 1 - al)) >> 1) + (bl - al - 1)

    oh = lane == cid
    cmat = jnp.where(oh, 1.0, 0.0)
    prefix = _dot(ltri_ref[...], cmat.astype(BF16)) + carry_ref[...]
    rank = jnp.sum(jnp.where(oh, prefix, 0.0), axis=-1, keepdims=True)
    carry = carry_ref[...] + jnp.sum(cmat, axis=0, keepdims=True)
    carry_ref[...] = carry
    cnt_ref[...] = jnp.broadcast_to(carry, cnt_ref.shape)

    h2_ref[:, D_MODEL:] = jnp.where(lane == 0, wa, jnp.where(lane == 1, wb, 0.0))
    route_ref[...] = jnp.where(lane == 0, cid.astype(F32), jnp.where(lane == 1, rank, 0.0))


def _outproj(x2, mc, yT, woc, woa, ga, g2, wr, br, *, batch, seq, tm):
    nt = seq // tm
    ltri = jnp.tril(jnp.ones((tm, tm), F32), -1).astype(BF16)
    const = lambda shape: pl.BlockSpec(shape, lambda b, t: (0,) * len(shape))
    row_spec = lambda w: pl.BlockSpec((tm, w), lambda b, t: (b * nt + t, 0))
    return pl.pallas_call(
        functools.partial(_outproj_kernel, tm=tm),
        grid=(batch, nt),
        in_specs=[row_spec(D_MODEL), row_spec(D_CONV),
                  pl.BlockSpec((1, D_ATT, tm), lambda b, t: (b, 0, t)),
                  const((D_CONV, D_MODEL)), const((D_ATT, D_MODEL)), const((1, D_ATT)), const((1, D_MODEL)),
                  const((D_MODEL, LANES)), const((1, LANES)), const((tm, tm))],
        out_specs=[row_spec(D_MODEL), row_spec(ROW_W), row_spec(LANES), const((8, LANES))],
        out_shape=[jax.ShapeDtypeStruct((batch * seq, D_MODEL), F32),
                   jax.ShapeDtypeStruct((batch * seq, ROW_W), F32),
                   jax.ShapeDtypeStruct((batch * seq, LANES), F32),
                   jax.ShapeDtypeStruct((8, LANES), F32)],
        scratch_shapes=[pltpu.VMEM((1, LANES), F32)],
        compiler_params=pltpu.CompilerParams(
            dimension_semantics=("arbitrary", "arbitrary"), vmem_limit_bytes=VMEM_LIMIT),
        name="outproj",
    )(x2, mc, yT, woc, woa, ga, g2, wr, br, ltri)


def _row_copies(src_at, dst_at, sem):
    return pltpu.make_async_copy(src_at, dst_at, sem)


def _dispatch_kernel(dest_ref, h2_ref, xs_ref, sem, *, tm):
    def body(g, carry):
        for u in range(ISSUE_UNROLL):
            r = g * ISSUE_UNROLL + u
            _row_copies(h2_ref.at[pl.ds(r, 1)], xs_ref.at[pl.ds(dest_ref[r], 1)], sem).start(priority=u % 2)
        return carry

    lax.fori_loop(0, tm // ISSUE_UNROLL, body, 0)
    _row_copies(h2_ref, xs_ref.at[pl.ds(0, tm)], sem).wait()


def _dispatch(dest, h2, *, tm):
    t_len = h2.shape[0]
    return pl.pallas_call(
        functools.partial(_dispatch_kernel, tm=tm),
        grid=(t_len // tm,),
        in_specs=[pl.BlockSpec((tm,), lambda i: (i,), memory_space=pltpu.SMEM),
                  pl.BlockSpec((tm, ROW_W), lambda i: (i, 0))],
        out_specs=pl.BlockSpec(memory_space=pl.ANY),
        out_shape=jax.ShapeDtypeStruct((t_len, ROW_W), F32),
        scratch_shapes=[pltpu.SemaphoreType.DMA(())],
        compiler_params=pltpu.CompilerParams(dimension_semantics=("arbitrary",), vmem_limit_bytes=VMEM_LIMIT),
        name="dispatch",
    )(dest, h2)


def _expert_kernel(blk_ref, ea_ref, eb_ref, lo_ref, hi_ref, x_ref, wga_ref, wua_ref, wda_ref,
                   wgb_ref, wub_ref, wdb_ref, y_ref):
    del blk_ref, ea_ref, eb_ref
    i = pl.program_id(0)
    lo = lo_ref[i]
    hi = hi_ref[i]

    def mlp(xb, wg_ref, wu_ref, wd_ref):
        g = _dot(xb, wg_ref[0])
        u = _dot(xb, wu_ref[0])
        return _dot(((g * jax.nn.sigmoid(g)) * u).astype(BF16), wd_ref[0])

    @pl.when(hi > lo)
    def _():
        xb = x_ref[:, :D_MODEL].astype(BF16)
        wts = x_ref[:, D_MODEL:]
        y = wts[:, 0:1] * mlp(xb, wga_ref, wua_ref, wda_ref) + wts[:, 1:2] * mlp(xb, wgb_ref, wub_ref, wdb_ref)
        row = lax.broadcasted_iota(jnp.int32, (ROW_BLOCK, 1), 0)
        mine = (row >= lo) & (row < hi)

        @pl.when(lo == 0)
        def _():
            y_ref[...] = jnp.where(mine, y, 0.0)

        @pl.when(lo > 0)
        def _():
            y_ref[...] = jnp.where(mine, y, y_ref[...])


def _experts(items, xs, wg, wu, wd):
    n_rows = xs.shape[0]
    n_items = items[0].shape[0]
    wspec = lambda shape, which: pl.BlockSpec(
        shape, lambda i, blk, ea, eb, lo, hi: ((ea, eb)[which][i], 0, 0))
    grid_spec = pltpu.PrefetchScalarGridSpec(
        num_scalar_prefetch=5,
        grid=(n_items,),
        in_specs=[pl.BlockSpec((ROW_BLOCK, ROW_W), lambda i, blk, ea, eb, lo, hi: (blk[i], 0)),
                  wspec((1, D_MODEL, D_EXPERT), 0), wspec((1, D_MODEL, D_EXPERT), 0), wspec((1, D_EXPERT, D_MODEL), 0),
                  wspec((1, D_MODEL, D_EXPERT), 1), wspec((1, D_MODEL, D_EXPERT), 1), wspec((1, D_EXPERT, D_MODEL), 1)],
        out_specs=pl.BlockSpec((ROW_BLOCK, D_MODEL), lambda i, blk, ea, eb, lo, hi: (blk[i], 0)),
    )
    return pl.pallas_call(
        _expert_kernel,
        grid_spec=grid_spec,
        out_shape=jax.ShapeDtypeStruct((n_rows, D_MODEL), F32),
        compiler_params=pltpu.CompilerParams(dimension_semantics=("arbitrary",), vmem_limit_bytes=VMEM_LIMIT),
        name="experts",
    )(*items, xs, wg, wu, wd, wg, wu, wd)


def _combine_kernel(dcur_ref, dnxt_ref, x1_ref, g_ref, ys_ref, out_ref, buf_ref, sems, *, tm):
    i = pl.program_id(0)
    slot = i % 2

    def issue(d_ref, s):
        def body(g, carry):
            for u in range(ISSUE_UNROLL):
                r = g * ISSUE_UNROLL + u
                _row_copies(ys_ref.at[pl.ds(d_ref[r], 1)], buf_ref.at[s, pl.ds(r, 1)],
                            sems.at[s]).start(priority=u % 2)
            return carry

        lax.fori_loop(0, tm // ISSUE_UNROLL, body, 0)

    @pl.when(i == 0)
    def _():
        issue(dcur_ref, 0)

    @pl.when(i + 1 < pl.num_programs(0))
    def _():
        issue(dnxt_ref, 1 - slot)

    _row_copies(ys_ref.at[pl.ds(0, tm)], buf_ref.at[slot], sems.at[slot]).wait()
    x2 = x1_ref[...] + buf_ref[slot]
    ms = jnp.mean(x2 * x2, axis=-1, keepdims=True)
    out_ref[...] = (x2 * lax.rsqrt(ms + EPS)) * g_ref[...]


def _combine(dest, x1, gf, ys, *, tm):
    t_len = x1.shape[0]
    n = t_len // tm
    return pl.pallas_call(
        functools.partial(_combine_kernel, tm=tm),
        grid=(n,),
        in_specs=[pl.BlockSpec((tm,), lambda i: (i,), memory_space=pltpu.SMEM),
                  pl.BlockSpec((tm,), lambda i: (jnp.minimum(i + 1, n - 1),), memory_space=pltpu.SMEM),
                  pl.BlockSpec((tm, D_MODEL), lambda i: (i, 0)),
                  pl.BlockSpec((1, D_MODEL), lambda i: (0, 0)),
                  pl.BlockSpec(memory_space=pl.ANY)],
        out_specs=pl.BlockSpec((tm, D_MODEL), lambda i: (i, 0)),
        out_shape=jax.ShapeDtypeStruct((t_len, D_MODEL), F32),
        scratch_shapes=[pltpu.VMEM((2, tm, D_MODEL), F32), pltpu.SemaphoreType.DMA((2,))],
        compiler_params=pltpu.CompilerParams(dimension_semantics=("arbitrary",), vmem_limit_bytes=VMEM_LIMIT),
        name="combine",
    )(dest, dest, x1, gf, ys)


def _layer(x, norm1_g, w_in, b_f, w_dw, b_dw, conv_ln_g, conv_ln_b, out_g_conv, out_g_att, w_out,
           norm2_g, w_r1, b_r1, w_r2, b_r2, w_gate, w_up, w_down):
    batch, seq, d = x.shape
    t_len = batch * seq
    tm = min(512, seq)
    tk = min(256, seq)
    tq = min(1024, seq)
    o1, o2 = D_CONV, 2 * D_CONV
    o3, o4, o5 = o2 + D_ATT, o2 + 2 * D_ATT, o2 + 3 * D_ATT

    wvg = w_in[:, :o2].astype(BF16)
    wk = w_in[:, o3:o4].astype(BF16)
    wqT = (w_in[:, o2:o3] * (HEAD_DIM ** -0.5)).T.astype(BF16)
    wvT = w_in[:, o4:o5].T.astype(BF16)
    assert PIECE_STRIDE == N_HEADS
    pad = LANES - N_PIECES * PIECE_STRIDE
    wf = jnp.pad(jnp.tile(w_in[:, o5:], (1, N_PIECES)), ((0, 0), (0, pad))).astype(BF16)
    bf3 = jnp.pad(jnp.tile(b_f.astype(F32).reshape(1, N_HEADS), (1, N_PIECES)), ((0, 0), (0, pad)))

    x2 = x.reshape(t_len, d)
    a, kp, qpT, vT = _inproj(x2, norm1_g.reshape(1, d), wvg, wk, wf, bf3, wqT, wvT,
                             batch=batch, seq=seq, tm=tm, tk=tk)
    mc = _conv(a, w_dw, b_dw.reshape(1, -1), conv_ln_g.reshape(1, -1), conv_ln_b.reshape(1, -1),
               out_g_conv.reshape(1, -1), batch=batch, seq=seq, tc=min(256, seq), rc=32)
    yT = _attention(qpT, kp, vT, batch=batch, seq=seq, tq=tq, tk=tk)

    rpad = LANES - N_GROUPS - N_EXPERTS
    wr = jnp.pad(jnp.concatenate([w_r1, jnp.transpose(w_r2, (1, 0, 2)).reshape(d, N_EXPERTS)], axis=1),
                 ((0, 0), (0, rpad)))
    br = jnp.pad(jnp.concatenate([b_r1.astype(F32), b_r2.reshape(-1).astype(F32)]).reshape(1, -1),
                 ((0, 0), (0, rpad)))
    x1, h2, route, cnt = _outproj(x2, mc, yT, w_out[:D_CONV].astype(BF16), w_out[D_CONV:].astype(BF16),
                                  out_g_att.reshape(1, -1), norm2_g.reshape(1, d), wr.astype(BF16), br,
                                  batch=batch, seq=seq, tm=tm)

    i32 = jnp.int32
    lanes = jnp.arange(LANES, dtype=i32)
    pick = lambda table, idx: jnp.sum(jnp.where(idx[:, None] == lanes, table[None, :], 0), axis=1).astype(i32)
    counts = cnt[0].astype(i32)
    ends = jnp.cumsum(counts).astype(i32)
    starts = ends - counts
    dest = pick(starts, route[:, 0].astype(i32)) + route[:, 1].astype(i32)

    n_blocks = t_len // ROW_BLOCK
    b_lo = starts // ROW_BLOCK
    n_it = jnp.where(counts > 0, (ends - 1) // ROW_BLOCK - b_lo + 1, 0)
    it_end = jnp.cumsum(n_it).astype(i32)
    it_start = it_end - n_it
    idx = jnp.arange(n_blocks + N_CLASSES, dtype=i32)
    valid = idx < it_end[-1]
    last_cls = jnp.max(jnp.where(counts > 0, lanes, 0))
    cls = jnp.where(valid, jnp.sum(it_end[None, :] <= idx[:, None], axis=1).astype(i32), last_cls)
    blk = jnp.where(valid, pick(b_lo, cls) + idx - pick(it_start, cls), n_blocks - 1)
    row0 = blk * ROW_BLOCK
    lo = jnp.where(valid, jnp.maximum(pick(starts, cls), row0) - row0, 0)
    hi = jnp.where(valid, jnp.minimum(pick(ends, cls), row0 + ROW_BLOCK) - row0, 0)
    pair_a, pair_b = [], []
    for g in range(N_GROUPS):
        for a_loc in range(EXPERTS_PER_GROUP):
            for b_loc in range(a_loc + 1, EXPERTS_PER_GROUP):
                pair_a.append(g * EXPERTS_PER_GROUP + a_loc)
                pair_b.append(g * EXPERTS_PER_GROUP + b_loc)
    cpad = [0] * (LANES - N_CLASSES)
    items = (blk.astype(i32), pick(jnp.array(pair_a + cpad, i32), cls), pick(jnp.array(pair_b + cpad, i32), cls),
             lo.astype(i32), hi.astype(i32))

    xs = _dispatch(dest, h2, tm=min(1024, t_len))
    ys = _experts(items, xs, w_gate.astype(BF16), w_up.astype(BF16), w_down.astype(BF16))
    return dest, x1, ys


def kernel(x, norm1_g, w_in, b_f, w_dw, b_dw, conv_ln_g, conv_ln_b, out_g_conv, out_g_att, w_out, norm2_g,
           w_r1, b_r1, w_r2, b_r2, w_gate, w_up, w_down, final_g):
    assert norm1_g.shape[0] == 1, "single-layer stack"
    batch, seq, d = x.shape
    dest, x1, ys = _layer(
        x, norm1_g[0], w_in[0], b_f[0], w_dw[0], b_dw[0], conv_ln_g[0], conv_ln_b[0], out_g_conv[0],
        out_g_att[0], w_out[0], norm2_g[0], w_r1[0], b_r1[0], w_r2[0], b_r2[0], w_gate[0], w_up[0], w_down[0])
    out = _combine(dest, x1, final_g.reshape(1, d), ys, tm=min(512, batch * seq))
    return out.reshape(batch, seq, d)
```

```python
import functools

import jax
import jax.numpy as jnp
from jax import lax
from jax.experimental import pallas as pl
from jax.experimental.pallas import tpu as pltpu

D_MODEL = 1024
D_CONV = 512
N_HEADS = 8
HEAD_DIM = 64
D_ATT = N_HEADS * HEAD_DIM
CONV_WIDTH = 31
N_GROUPS = 4
EXPERTS_PER_GROUP = 8
N_EXPERTS = N_GROUPS * EXPERTS_PER_GROUP
TOP_K = 2
D_EXPERT = D_MODEL // 4
ROW_BLOCK = 256
EPS = 1e-6

LANES = 128
SUBLANES = 8
KAUG = 128
N_PIECES = 3
PIECE_STRIDE = 8
CONV_HALO = 32
PAIRS_PER_GROUP = EXPERTS_PER_GROUP * (EXPERTS_PER_GROUP - 1) // 2
N_CLASSES = N_GROUPS * PAIRS_PER_GROUP
ROW_W = D_MODEL + LANES
ISSUE_UNROLL = 8
V_ROWS = 80
LOG2E = 1.4426950408889634
VMEM_LIMIT = 56 * 1024 * 1024

F32 = jnp.float32
BF16 = jnp.bfloat16


def _dot(a, b):
    return jnp.dot(a, b, preferred_element_type=F32)


def _dot_nt(a, b):
    return lax.dot_general(a, b, (((1,), (1,)), ((), ())), preferred_element_type=F32)


def _split3(x):
    hi = x.astype(BF16)
    r1 = x - hi.astype(F32)
    mid = r1.astype(BF16)
    lo = (r1 - mid.astype(F32)).astype(BF16)
    return hi.astype(F32), mid.astype(F32), lo.astype(F32)


def _piece_lane_mask(lane, h):
    return (lane == h) | (lane == h + PIECE_STRIDE) | (lane == h + 2 * PIECE_STRIDE)


def _inproj_kernel(x_ref, g1_ref, wvg_ref, wk_ref, wf_ref, bf_ref, wqT_ref, wvT_ref, ltri_ref,
                   a_ref, kp_ref, qpT_ref, vT_ref, carry_ref, *, tm, tk):
    @pl.when(pl.program_id(1) == 0)
    def _():
        carry_ref[...] = jnp.zeros_like(carry_ref)

    x = x_ref[...]
    ms = jnp.mean(x * x, axis=-1, keepdims=True)
    hb = ((x * lax.rsqrt(ms + EPS)) * g1_ref[...]).astype(BF16)

    zvg = _dot(hb, wvg_ref[...])
    a_ref[...] = zvg[:, :D_CONV] * jax.nn.sigmoid(zvg[:, D_CONV:])

    kk = _dot(hb, wk_ref[...])

    zf = _dot(hb, wf_ref[...]) + bf_ref[...]
    lf = jnp.minimum(zf, 0.0) - jnp.log1p(jnp.exp(-jnp.abs(zf)))
    lane = lax.broadcasted_iota(jnp.int32, (tm, LANES), 1)
    hi, mid, lo = _split3(lf)
    lf3 = jnp.where(lane < PIECE_STRIDE, hi,
                    jnp.where(lane < 2 * PIECE_STRIDE, mid,
                              jnp.where(lane < 3 * PIECE_STRIDE, lo, 0.0))).astype(BF16)
    cs3 = _dot(ltri_ref[...], lf3)
    c = (cs3 + pltpu.roll(cs3, LANES - PIECE_STRIDE, 1)
         + pltpu.roll(cs3, LANES - 2 * PIECE_STRIDE, 1)) + carry_ref[...]
    carry_ref[...] = c[tm - 1:tm, :]

    nhi, nmid, nlo = _split3(c * (-LOG2E))
    p3 = jnp.where(lane < PIECE_STRIDE, nhi,
                   jnp.where(lane < 2 * PIECE_STRIDE, pltpu.roll(nmid, PIECE_STRIDE, 1),
                             jnp.where(lane < 3 * PIECE_STRIDE, pltpu.roll(nlo, 2 * PIECE_STRIDE, 1), 0.0)))
    p3_hi = pltpu.roll(p3, HEAD_DIM, 1)

    qT = _dot_nt(wqT_ref[...], hb) * LOG2E
    vT = _dot_nt(wvT_ref[...], hb)
    row = lax.broadcasted_iota(jnp.int32, (HEAD_DIM, tm), 0)
    vrow = lax.broadcasted_iota(jnp.int32, (V_ROWS - HEAD_DIM, tk), 0)
    v_tail = jnp.where(vrow == 0, 1.0, 0.0).astype(BF16)

    for h in range(N_HEADS):
        kcol = kk[:, (h // 2) * LANES:(h // 2 + 1) * LANES]
        ones_h = jnp.where(_piece_lane_mask(row, h), 1.0, 0.0)
        q_h = qT[h * HEAD_DIM:(h + 1) * HEAD_DIM, :]
        if h % 2 == 0:
            ext = jnp.where(_piece_lane_mask(lane, h + HEAD_DIM), p3_hi, 0.0)
            kp = jnp.where(lane < HEAD_DIM, kcol, ext)
            qp = jnp.concatenate([q_h, ones_h], axis=0)
        else:
            ext = jnp.where(_piece_lane_mask(lane, h), p3, 0.0)
            kp = jnp.where(lane >= HEAD_DIM, kcol, ext)
            qp = jnp.concatenate([ones_h, q_h], axis=0)
        qpT_ref[0, h] = qp.astype(BF16)
        for cidx in range(tm // tk):
            kp_ref[0, h, cidx] = kp[cidx * tk:(cidx + 1) * tk, :].astype(BF16)
            v_h = vT[h * HEAD_DIM:(h + 1) * HEAD_DIM, cidx * tk:(cidx + 1) * tk].astype(BF16)
            vT_ref[0, h, cidx] = jnp.concatenate([v_h, v_tail], axis=0)


def _inproj(x2, g1, wvg, wk, wf, bf3, wqT, wvT, *, batch, seq, tm, tk):
    nt = seq // tm
    nk = seq // tk
    ltri = jnp.tril(jnp.ones((tm, tm), F32)).astype(BF16)
    const = lambda shape: pl.BlockSpec(shape, lambda b, t: (0,) * len(shape))
    return pl.pallas_call(
        functools.partial(_inproj_kernel, tm=tm, tk=tk),
        grid=(batch, nt),
        in_specs=[
            pl.BlockSpec((tm, D_MODEL), lambda b, t: (b * nt + t, 0)),
            const((1, D_MODEL)), const((D_MODEL, 2 * D_CONV)), const((D_MODEL, D_ATT)),
            const((D_MODEL, LANES)), const((1, LANES)), const((D_ATT, D_MODEL)), const((D_ATT, D_MODEL)),
            const((tm, tm)),
        ],
        out_specs=[
            pl.BlockSpec((tm, D_CONV), lambda b, t: (b * nt + t, 0)),
            pl.BlockSpec((1, N_HEADS, tm // tk, tk, KAUG), lambda b, t: (b, 0, t, 0, 0)),
            pl.BlockSpec((1, N_HEADS, KAUG, tm), lambda b, t: (b, 0, 0, t)),
            pl.BlockSpec((1, N_HEADS, tm // tk, V_ROWS, tk), lambda b, t: (b, 0, t, 0, 0)),
        ],
        out_shape=[
            jax.ShapeDtypeStruct((batch * seq, D_CONV), F32),
            jax.ShapeDtypeStruct((batch, N_HEADS, nk, tk, KAUG), BF16),
            jax.ShapeDtypeStruct((batch, N_HEADS, KAUG, seq), BF16),
            jax.ShapeDtypeStruct((batch, N_HEADS, nk, V_ROWS, tk), BF16),
        ],
        scratch_shapes=[pltpu.VMEM((1, LANES), F32)],
        compiler_params=pltpu.CompilerParams(
            dimension_semantics=("arbitrary", "arbitrary"), vmem_limit_bytes=VMEM_LIMIT),
        name="inproj",
    )(x2, g1, wvg, wk, wf, bf3, wqT, wvT, ltri)


def _conv_kernel(a_ref, w_ref, b_ref, lng_ref, lnb_ref, og_ref, out_ref, sh_ref, acc_ref, *, tc, rc):
    @pl.when(pl.program_id(1) == 0)
    def _():
        sh_ref[0, 0:CONV_HALO, :] = jnp.zeros((CONV_HALO, D_CONV), F32)

    @pl.when(pl.program_id(1) > 0)
    def _():
        sh_ref[0, 0:CONV_HALO, :] = sh_ref[0, tc:tc + CONV_HALO, :]

    sh_ref[0, CONV_HALO:CONV_HALO + tc, :] = a_ref[...]
    n_sh = tc + CONV_HALO - SUBLANES
    for f in range(1, SUBLANES):
        sh_ref[f, 0:n_sh, :] = sh_ref[0, f:f + n_sh, :]

    base = CONV_HALO - (CONV_WIDTH - 1)
    for cb in range(D_CONV // LANES):
        cols = slice(cb * LANES, (cb + 1) * LANES)

        def chunk(c, carry, cols=cols):
            r0 = pl.multiple_of(c * rc, rc)
            acc = jnp.zeros((rc, LANES), F32)
            for j in range(CONV_WIDTH):
                f = (base + j) % SUBLANES
                acc = acc + w_ref[j:j + 1, cols] * sh_ref[f, pl.ds(r0 + (base + j - f), rc), cols]
            acc_ref[pl.ds(r0, rc), cols] = acc
            return carry

        lax.fori_loop(0, tc // rc, chunk, 0)

    y = acc_ref[...] + b_ref[...]
    mu = jnp.mean(y, axis=-1, keepdims=True)
    yc = y - mu
    var = jnp.mean(yc * yc, axis=-1, keepdims=True)
    yn = yc * lax.rsqrt(var + EPS) * lng_ref[...] + lnb_ref[...]
    s = yn * jax.nn.sigmoid(yn)
    ms = jnp.mean(s * s, axis=-1, keepdims=True)
    out_ref[...] = (s * lax.rsqrt(ms + EPS) * og_ref[...]).astype(BF16)


def _conv(a, w_dw, b_dw, ln_g, ln_b, og, *, batch, seq, tc, rc):
    nt = seq // tc
    const = lambda shape: pl.BlockSpec(shape, lambda b, t: (0,) * len(shape))
    return pl.pallas_call(
        functools.partial(_conv_kernel, tc=tc, rc=rc),
        grid=(batch, nt),
        in_specs=[pl.BlockSpec((tc, D_CONV), lambda b, t: (b * nt + t, 0)),
                  const((CONV_WIDTH, D_CONV)), const((1, D_CONV)), const((1, D_CONV)),
                  const((1, D_CONV)), const((1, D_CONV))],
        out_specs=pl.BlockSpec((tc, D_CONV), lambda b, t: (b * nt + t, 0)),
        out_shape=jax.ShapeDtypeStruct((batch * seq, D_CONV), BF16),
        scratch_shapes=[pltpu.VMEM((SUBLANES, tc + CONV_HALO, D_CONV), F32), pltpu.VMEM((tc, D_CONV), F32)],
        compiler_params=pltpu.CompilerParams(
            dimension_semantics=("arbitrary", "arbitrary"), vmem_limit_bytes=VMEM_LIMIT),
        name="conv",
    )(a, w_dw, b_dw, ln_g, ln_b, og)


def _attn_kernel(qT_ref, k_ref, v_ref, o_ref, sa_ref, sb_ref, cma_ref, cmb_ref, *, tq, tk):
    i = pl.program_id(2)
    qT = qT_ref[0, 0]
    n_sub = tq // tk

    def scores(g, s_ref, cm_ref):
        for d in range(n_sub):
            s = _dot(k_ref[0, 0, g * n_sub + d], qT)
            s_ref[d] = s
            cm_ref[d] = jnp.max(s, axis=0, keepdims=True)

    def update(m, acc, s, cmax, v):
        m_new = jnp.maximum(m, cmax)
        p = jnp.exp2(s - m_new).astype(BF16)
        return m_new, jnp.exp2(m - m_new) * acc + _dot(v, p)

    def consume(g, s_ref, cm_ref, carry, nxt=None):
        m, acc = carry
        m_new = m
        for d in range(n_sub):
            m_new = jnp.maximum(m_new, cm_ref[d])
        ps = []
        for d in range(n_sub):
            if nxt is not None:
                g_n, sn_ref, cmn_ref = nxt
                s = _dot(k_ref[0, 0, g_n * n_sub + d], qT)
                sn_ref[d] = s
                cmn_ref[d] = jnp.max(s, axis=0, keepdims=True)
            ps.append(jnp.exp2(s_ref[d] - m_new).astype(BF16))
        p = jnp.concatenate(ps, axis=0)
        v = jnp.concatenate([v_ref[0, 0, g * n_sub + d] for d in range(n_sub)], axis=1)
        return m_new, jnp.exp2(m - m_new) * acc + _dot(v, p)

    key = lax.broadcasted_iota(jnp.int32, (tk, tk), 0)
    qry = lax.broadcasted_iota(jnp.int32, (tk, tk), 1)

    def consume_diag(s_ref, carry):
        m, acc = carry
        for d in range(n_sub):
            lo = d * tk
            s_diag = jnp.where(key <= qry, s_ref[d, :, lo:lo + tk], -jnp.inf)
            s = s_diag if lo + tk == tq else jnp.concatenate([s_diag, s_ref[d, :, lo + tk:]], axis=1)
            m_d, acc_d = update(m[:, lo:], acc[:, lo:], s, jnp.max(s, axis=0, keepdims=True),
                                v_ref[0, 0, i * n_sub + d])
            if lo == 0:
                m, acc = m_d, acc_d
            else:
                m = jnp.concatenate([m[:, :lo], m_d], axis=1)
                acc = jnp.concatenate([acc[:, :lo], acc_d], axis=1)
        o_ref[0] = acc[:HEAD_DIM] / acc[HEAD_DIM:HEAD_DIM + 1]

    def pair(p, carry):
        g = 2 * p
        carry = consume(g, sa_ref, cma_ref, carry, nxt=(g + 1, sb_ref, cmb_ref))
        return consume(g + 1, sb_ref, cmb_ref, carry, nxt=(g + 2, sa_ref, cma_ref))

    scores(0, sa_ref, cma_ref)
    init = (jnp.full((1, tq), -jnp.inf, F32), jnp.zeros((V_ROWS, tq), F32))
    carry = lax.fori_loop(0, i // 2, pair, init)

    @pl.when(i % 2 == 1)
    def _():
        consume_diag(sb_ref, consume(i - 1, sa_ref, cma_ref, carry, nxt=(i, sb_ref, cmb_ref)))

    @pl.when(i % 2 == 0)
    def _():
        consume_diag(sa_ref, carry)


def _attention(qpT, kp, vT, *, batch, seq, tq, tk):
    nk = seq // tk
    return pl.pallas_call(
        functools.partial(_attn_kernel, tq=tq, tk=tk),
        grid=(batch, N_HEADS, seq // tq),
        in_specs=[
            pl.BlockSpec((1, 1, KAUG, tq), lambda b, h, i: (b, h, 0, i)),
            pl.BlockSpec((1, 1, nk, tk, KAUG), lambda b, h, i: (b, h, 0, 0, 0)),
            pl.BlockSpec((1, 1, nk, V_ROWS, tk), lambda b, h, i: (b, h, 0, 0, 0)),
        ],
        out_specs=pl.BlockSpec((1, HEAD_DIM, tq), lambda b, h, i: (b, h, i)),
        out_shape=jax.ShapeDtypeStruct((batch, D_ATT, seq), F32),
        scratch_shapes=[pltpu.VMEM((tq // tk, tk, tq), F32), pltpu.VMEM((tq // tk, tk, tq), F32),
                        pltpu.VMEM((tq // tk, 1, tq), F32), pltpu.VMEM((tq // tk, 1, tq), F32)],
        compiler_params=pltpu.CompilerParams(
            dimension_semantics=("arbitrary", "arbitrary", "arbitrary"), vmem_limit_bytes=VMEM_LIMIT),
        name="attention",
    )(qpT, kp, vT)


def _outproj_kernel(x_ref, mc_ref, yT_ref, woc_ref, woa_ref, ga_ref, g2_ref, wr_ref, br_ref, ltri_ref,
                    x1_ref, h2_ref, route_ref, cnt_ref, carry_ref, *, tm):
    first = (pl.program_id(0) == 0) & (pl.program_id(1) == 0)

    @pl.when(first)
    def _():
        carry_ref[...] = jnp.zeros_like(carry_ref)

    yT = yT_ref[0]
    msa = jnp.mean(yT * yT, axis=0, keepdims=True)
    yn = (yT * lax.rsqrt(msa + EPS)).T * ga_ref[...]
    x1 = x_ref[...] + _dot(mc_ref[...], woc_ref[...]) + _dot(yn.astype(BF16), woa_ref[...])
    x1_ref[...] = x1
    ms = jnp.mean(x1 * x1, axis=-1, keepdims=True)
    h2 = (x1 * lax.rsqrt(ms + EPS)) * g2_ref[...]
    h2_ref[:, :D_MODEL] = h2

    lg = _dot(h2.astype(BF16), wr_ref[...]) + br_ref[...]
    lane = lax.broadcasted_iota(jnp.int32, (tm, LANES), 1)
    neg = -jnp.inf
    big = jnp.int32(LANES)

    lg1 = jnp.where(lane < N_GROUPS, lg, neg)
    m1 = jnp.max(lg1, axis=-1, keepdims=True)
    p1_sel = 1.0 / jnp.sum(jnp.exp(lg1 - m1), axis=-1, keepdims=True)
    grp = jnp.min(jnp.where(lg1 == m1, lane, big), axis=-1, keepdims=True)

    lo_lane = N_GROUPS + EXPERTS_PER_GROUP * grp
    v = jnp.where((lane >= lo_lane) & (lane < lo_lane + EXPERTS_PER_GROUP), lg, neg)
    v1 = jnp.max(v, axis=-1, keepdims=True)
    j1 = jnp.min(jnp.where(v == v1, lane, big), axis=-1, keepdims=True)
    vv = jnp.where(lane == j1, neg, v)
    v2 = jnp.max(vv, axis=-1, keepdims=True)
    j2 = jnp.min(jnp.where(vv == v2, lane, big), axis=-1, keepdims=True)
    e21 = jnp.exp(v2 - v1)
    w0 = p1_sel / (1.0 + e21)
    w1 = p1_sel * e21 / (1.0 + e21)
    e0 = j1 - N_GROUPS
    e1 = j2 - N_GROUPS

    swap = e1 < e0
    ea = jnp.where(swap, e1, e0)
    eb = jnp.where(swap, e0, e1)
    wa = jnp.where(swap, w1, w0)
    wb = jnp.where(swap, w0, w1)
    al = ea - EXPERTS_PER_GROUP * grp
    bl = eb - EXPERTS_PER_GROUP * grp
    cid = PAIRS_PER_GROUP * grp + ((al * (2 * EXPERTS_PER_GROUP - 1 - al)) >> 1) + (bl - al - 1)

    oh = lane == cid
    cmat = jnp.where(oh, 1.0, 0.0)
    prefix = _dot(ltri_ref[...], cmat.astype(BF16)) + carry_ref[...]
    rank = jnp.sum(jnp.where(oh, prefix, 0.0), axis=-1, keepdims=True)
    carry = carry_ref[...] + jnp.sum(cmat, axis=0, keepdims=True)
    carry_ref[...] = carry
    cnt_ref[...] = jnp.broadcast_to(carry, cnt_ref.shape)

    h2_ref[:, D_MODEL:] = jnp.where(lane == 0, wa, jnp.where(lane == 1, wb, 0.0))
    route_ref[...] = jnp.where(lane == 0, cid.astype(F32), jnp.where(lane == 1, rank, 0.0))


def _outproj(x2, mc, yT, woc, woa, ga, g2, wr, br, *, batch, seq, tm):
    nt = seq // tm
    ltri = jnp.tril(jnp.ones((tm, tm), F32), -1).astype(BF16)
    const = lambda shape: pl.BlockSpec(shape, lambda b, t: (0,) * len(shape))
    row_spec = lambda w: pl.BlockSpec((tm, w), lambda b, t: (b * nt + t, 0))
    return pl.pallas_call(
        functools.partial(_outproj_kernel, tm=tm),
        grid=(batch, nt),
        in_specs=[row_spec(D_MODEL), row_spec(D_CONV),
                  pl.BlockSpec((1, D_ATT, tm), lambda b, t: (b, 0, t)),
                  const((D_CONV, D_MODEL)), const((D_ATT, D_MODEL)), const((1, D_ATT)), const((1, D_MODEL)),
                  const((D_MODEL, LANES)), const((1, LANES)), const((tm, tm))],
        out_specs=[row_spec(D_MODEL), row_spec(ROW_W), row_spec(LANES), const((8, LANES))],
        out_shape=[jax.ShapeDtypeStruct((batch * seq, D_MODEL), F32),
                   jax.ShapeDtypeStruct((batch * seq, ROW_W), F32),
                   jax.ShapeDtypeStruct((batch * seq, LANES), F32),
                   jax.ShapeDtypeStruct((8, LANES), F32)],
        scratch_shapes=[pltpu.VMEM((1, LANES), F32)],
        compiler_params=pltpu.CompilerParams(
            dimension_semantics=("arbitrary", "arbitrary"), vmem_limit_bytes=VMEM_LIMIT),
        name="outproj",
    )(x2, mc, yT, woc, woa, ga, g2, wr, br, ltri)


def _row_copies(src_at, dst_at, sem):
    return pltpu.make_async_copy(src_at, dst_at, sem)


def _dispatch_kernel(dest_ref, h2_ref, xs_ref, sem, *, tm):
    def body(g, carry):
        for u in range(ISSUE_UNROLL):
            r = g * ISSUE_UNROLL + u
            _row_copies(h2_ref.at[pl.ds(r, 1)], xs_ref.at[pl.ds(dest_ref[r], 1)], sem).start(priority=u % 2)
        return carry

    lax.fori_loop(0, tm // ISSUE_UNROLL, body, 0)
    _row_copies(h2_ref, xs_ref.at[pl.ds(0, tm)], sem).wait()


def _dispatch(dest, h2, *, tm):
    t_len = h2.shape[0]
    return pl.pallas_call(
        functools.partial(_dispatch_kernel, tm=tm),
        grid=(t_len // tm,),
        in_specs=[pl.BlockSpec((tm,), lambda i: (i,), memory_space=pltpu.SMEM),
                  pl.BlockSpec((tm, ROW_W), lambda i: (i, 0))],
        out_specs=pl.BlockSpec(memory_space=pl.ANY),
        out_shape=jax.ShapeDtypeStruct((t_len, ROW_W), F32),
        scratch_shapes=[pltpu.SemaphoreType.DMA(())],
        compiler_params=pltpu.CompilerParams(dimension_semantics=("arbitrary",), vmem_limit_bytes=VMEM_LIMIT),
        name="dispatch",
    )(dest, h2)


def _expert_kernel(blk_ref, ea_ref, eb_ref, lo_ref, hi_ref, x_ref, wga_ref, wua_ref, wda_ref,
                   wgb_ref, wub_ref, wdb_ref, y_ref):
    del blk_ref, ea_ref, eb_ref
    i = pl.program_id(0)
    lo = lo_ref[i]
    hi = hi_ref[i]

    def mlp(xb, wg_ref, wu_ref, wd_ref):
        g = _dot(xb, wg_ref[0])
        u = _dot(xb, wu_ref[0])
        return _dot(((g * jax.nn.sigmoid(g)) * u).astype(BF16), wd_ref[0])

    @pl.when(hi > lo)
    def _():
        xb = x_ref[:, :D_MODEL].astype(BF16)
        wts = x_ref[:, D_MODEL:]
        y = wts[:, 0:1] * mlp(xb, wga_ref, wua_ref, wda_ref) + wts[:, 1:2] * mlp(xb, wgb_ref, wub_ref, wdb_ref)
        row = lax.broadcasted_iota(jnp.int32, (ROW_BLOCK, 1), 0)
        mine = (row >= lo) & (row < hi)

        @pl.when(lo == 0)
        def _():
            y_ref[...] = jnp.where(mine, y, 0.0)

        @pl.when(lo > 0)
        def _():
            y_ref[...] = jnp.where(mine, y, y_ref[...])


def _experts(items, xs, wg, wu, wd):
    n_rows = xs.shape[0]
    n_items = items[0].shape[0]
    wspec = lambda shape, which: pl.BlockSpec(
        shape, lambda i, blk, ea, eb, lo, hi: ((ea, eb)[which][i], 0, 0))
    grid_spec = pltpu.PrefetchScalarGridSpec(
        num_scalar_prefetch=5,
        grid=(n_items,),
        in_specs=[pl.BlockSpec((ROW_BLOCK, ROW_W), lambda i, blk, ea, eb, lo, hi: (blk[i], 0)),
                  wspec((1, D_MODEL, D_EXPERT), 0), wspec((1, D_MODEL, D_EXPERT), 0), wspec((1, D_EXPERT, D_MODEL), 0),
                  wspec((1, D_MODEL, D_EXPERT), 1), wspec((1, D_MODEL, D_EXPERT), 1), wspec((1, D_EXPERT, D_MODEL), 1)],
        out_specs=pl.BlockSpec((ROW_BLOCK, D_MODEL), lambda i, blk, ea, eb, lo, hi: (blk[i], 0)),
    )
    return pl.pallas_call(
        _expert_kernel,
        grid_spec=grid_spec,
        out_shape=jax.ShapeDtypeStruct((n_rows, D_MODEL), F32),
        compiler_params=pltpu.CompilerParams(dimension_semantics=("arbitrary",), vmem_limit_bytes=VMEM_LIMIT),
        name="experts",
    )(*items, xs, wg, wu, wd, wg, wu, wd)


def _combine_kernel(dcur_ref, dnxt_ref, x1_ref, g_ref, ys_ref, out_ref, buf_ref, sems, *, tm):
    i = pl.program_id(0)
    slot = i % 2

    def issue(d_ref, s):
        def body(g, carry):
            for u in range(ISSUE_UNROLL):
                r = g * ISSUE_UNROLL + u
                _row_copies(ys_ref.at[pl.ds(d_ref[r], 1)], buf_ref.at[s, pl.ds(r, 1)],
                            sems.at[s]).start(priority=u % 2)
            return carry

        lax.fori_loop(0, tm // ISSUE_UNROLL, body, 0)

    @pl.when(i == 0)
    def _():
        issue(dcur_ref, 0)

    @pl.when(i + 1 < pl.num_programs(0))
    def _():
        issue(dnxt_ref, 1 - slot)

    _row_copies(ys_ref.at[pl.ds(0, tm)], buf_ref.at[slot], sems.at[slot]).wait()
    x2 = x1_ref[...] + buf_ref[slot]
    ms = jnp.mean(x2 * x2, axis=-1, keepdims=True)
    out_ref[...] = (x2 * lax.rsqrt(ms + EPS)) * g_ref[...]


def _combine(dest, x1, gf, ys, *, tm):
    t_len = x1.shape[0]
    n = t_len // tm
    return pl.pallas_call(
        functools.partial(_combine_kernel, tm=tm),
        grid=(n,),
        in_specs=[pl.BlockSpec((tm,), lambda i: (i,), memory_space=pltpu.SMEM),
                  pl.BlockSpec((tm,), lambda i: (jnp.minimum(i + 1, n - 1),), memory_space=pltpu.SMEM),
                  pl.BlockSpec((tm, D_MODEL), lambda i: (i, 0)),
                  pl.BlockSpec((1, D_MODEL), lambda i: (0, 0)),
                  pl.BlockSpec(memory_space=pl.ANY)],
        out_specs=pl.BlockSpec((tm, D_MODEL), lambda i: (i, 0)),
        out_shape=jax.ShapeDtypeStruct((t_len, D_MODEL), F32),
        scratch_shapes=[pltpu.VMEM((2, tm, D_MODEL), F32), pltpu.SemaphoreType.DMA((2,))],
        compiler_params=pltpu.CompilerParams(dimension_semantics=("arbitrary",), vmem_limit_bytes=VMEM_LIMIT),
        name="combine",
    )(dest, dest, x1, gf, ys)


def _layer(x, norm1_g, w_in, b_f, w_dw, b_dw, conv_ln_g, conv_ln_b, out_g_conv, out_g_att, w_out,
           norm2_g, w_r1, b_r1, w_r2, b_r2, w_gate, w_up, w_down):
    batch, seq, d = x.shape
    t_len = batch * seq
    tm = min(512, seq)
    tk = min(256, seq)
    tq = min(1024, seq)
    o1, o2 = D_CONV, 2 * D_CONV
    o3, o4, o5 = o2 + D_ATT, o2 + 2 * D_ATT, o2 + 3 * D_ATT

    wvg = w_in[:, :o2].astype(BF16)
    wk = w_in[:, o3:o4].astype(BF16)
    wqT = (w_in[:, o2:o3] * (HEAD_DIM ** -0.5)).T.astype(BF16)
    wvT = w_in[:, o4:o5].T.astype(BF16)
    assert PIECE_STRIDE == N_HEADS
    pad = LANES - N_PIECES * PIECE_STRIDE
    wf = jnp.pad(jnp.tile(w_in[:, o5:], (1, N_PIECES)), ((0, 0), (0, pad))).astype(BF16)
    bf3 = jnp.pad(jnp.tile(b_f.astype(F32).reshape(1, N_HEADS), (1, N_PIECES)), ((0, 0), (0, pad)))

    x2 = x.reshape(t_len, d)
    a, kp, qpT, vT = _inproj(x2, norm1_g.reshape(1, d), wvg, wk, wf, bf3, wqT, wvT,
                             batch=batch, seq=seq, tm=tm, tk=tk)
    mc = _conv(a, w_dw, b_dw.reshape(1, -1), conv_ln_g.reshape(1, -1), conv_ln_b.reshape(1, -1),
               out_g_conv.reshape(1, -1), batch=batch, seq=seq, tc=min(256, seq), rc=64)
    yT = _attention(qpT, kp, vT, batch=batch, seq=seq, tq=tq, tk=tk)

    rpad = LANES - N_GROUPS - N_EXPERTS
    wr = jnp.pad(jnp.concatenate([w_r1, jnp.transpose(w_r2, (1, 0, 2)).reshape(d, N_EXPERTS)], axis=1),
                 ((0, 0), (0, rpad)))
    br = jnp.pad(jnp.concatenate([b_r1.astype(F32), b_r2.reshape(-1).astype(F32)]).reshape(1, -1),
                 ((0, 0), (0, rpad)))
    x1, h2, route, cnt = _outproj(x2, mc, yT, w_out[:D_CONV].astype(BF16), w_out[D_CONV:].astype(BF16),
                                  out_g_att.reshape(1, -1), norm2_g.reshape(1, d), wr.astype(BF16), br,
                                  batch=batch, seq=seq, tm=tm)

    i32 = jnp.int32
    lanes = jnp.arange(LANES, dtype=i32)
    pick = lambda table, idx: jnp.sum(jnp.where(idx[:, None] == lanes, table[None, :], 0), axis=1).astype(i32)
    counts = cnt[0].astype(i32)
    ends = jnp.cumsum(counts).astype(i32)
    starts = ends - counts
    dest = pick(starts, route[:, 0].astype(i32)) + route[:, 1].astype(i32)

    n_blocks = t_len // ROW_BLOCK
    b_lo = starts // ROW_BLOCK
    n_it = jnp.where(counts > 0, (ends - 1) // ROW_BLOCK - b_lo + 1, 0)
    it_end = jnp.cumsum(n_it).astype(i32)
    it_start = it_end - n_it
    idx = jnp.arange(n_blocks + N_CLASSES, dtype=i32)
    valid = idx < it_end[-1]
    last_cls = jnp.max(jnp.where(counts > 0, lanes, 0))
    cls = jnp.where(valid, jnp.sum(it_end[None, :] <= idx[:, None], axis=1).astype(i32), last_cls)
    blk = jnp.where(valid, pick(b_lo, cls) + idx - pick(it_start, cls), n_blocks - 1)
    row0 = blk * ROW_BLOCK
    lo = jnp.where(valid, jnp.maximum(pick(starts, cls), row0) - row0, 0)
    hi = jnp.where(valid, jnp.minimum(pick(ends, cls), row0 + ROW_BLOCK) - row0, 0)
    pair_a, pair_b = [], []
    for g in range(N_GROUPS):
        for a_loc in range(EXPERTS_PER_GROUP):
            for b_loc in range(a_loc + 1, EXPERTS_PER_GROUP):
                pair_a.append(g * EXPERTS_PER_GROUP + a_loc)
                pair_b.append(g * EXPERTS_PER_GROUP + b_loc)
    cpad = [0] * (LANES - N_CLASSES)
    items = (blk.astype(i32), pick(jnp.array(pair_a + cpad, i32), cls), pick(jnp.array(pair_b + cpad, i32), cls),
             lo.astype(i32), hi.astype(i32))

    xs = _dispatch(dest, h2, tm=min(1024, t_len))
    ys = _experts(items, xs, w_gate.astype(BF16), w_up.astype(BF16), w_down.astype(BF16))
    return dest, x1, ys


def kernel(x, norm1_g, w_in, b_f, w_dw, b_dw, conv_ln_g, conv_ln_b, out_g_conv, out_g_att, w_out, norm2_g,
           w_r1, b_r1, w_r2, b_r2, w_gate, w_up, w_down, final_g):
    assert norm1_g.shape[0] == 1, "single-layer stack"
    batch, seq, d = x.shape
    dest, x1, ys = _layer(
        x, norm1_g[0], w_in[0], b_f[0], w_dw[0], b_dw[0], conv_ln_g[0], conv_ln_b[0], out_g_conv[0],
        out_g_att[0], w_out[0], norm2_g[0], w_r1[0], b_r1[0], w_r2[0], b_r2[0], w_gate[0], w_up[0], w_down[0])
    out = _combine(dest, x1, final_g.reshape(1, d), ys, tm=min(512, batch * seq))
    return out.reshape(batch, seq, d)
```

```python
import functools

import jax
import jax.numpy as jnp
from jax import lax
from jax.experimental import pallas as pl
from jax.experimental.pallas import tpu as pltpu

D_MODEL = 1024
D_CONV = 512
N_HEADS = 8
HEAD_DIM = 64
D_ATT = N_HEADS * HEAD_DIM
CONV_WIDTH = 31
N_GROUPS = 4
EXPERTS_PER_GROUP = 8
N_EXPERTS = N_GROUPS * EXPERTS_PER_GROUP
TOP_K = 2
D_EXPERT = D_MODEL // 4
ROW_BLOCK = 256
EPS = 1e-6

LANES = 128
SUBLANES = 8
KAUG = 128
N_PIECES = 3
PIECE_STRIDE = 8
CONV_HALO = 32
PAIRS_PER_GROUP = EXPERTS_PER_GROUP * (EXPERTS_PER_GROUP - 1) // 2
N_CLASSES = N_GROUPS * PAIRS_PER_GROUP
ROW_W = D_MODEL + LANES
SKIP_LOG2 = 160.0
NORM_SLACK = 1.02
ISSUE_UNROLL = 8
V_ROWS = 80
LOG2E = 1.4426950408889634
VMEM_LIMIT = 56 * 1024 * 1024

F32 = jnp.float32
BF16 = jnp.bfloat16


def _dot(a, b):
    return jnp.dot(a, b, preferred_element_type=F32)


def _dot_nt(a, b):
    return lax.dot_general(a, b, (((1,), (1,)), ((), ())), preferred_element_type=F32)


def _split3(x):
    hi = x.astype(BF16)
    r1 = x - hi.astype(F32)
    mid = r1.astype(BF16)
    lo = (r1 - mid.astype(F32)).astype(BF16)
    return hi.astype(F32), mid.astype(F32), lo.astype(F32)


def _piece_lane_mask(lane, h):
    return (lane == h) | (lane == h + PIECE_STRIDE) | (lane == h + 2 * PIECE_STRIDE)


def _inproj_kernel(x_ref, g1_ref, wvg_ref, wk_ref, wf_ref, bf_ref, wqT_ref, wvT_ref, ltri_ref, hsel_ref,
                   a_ref, kp_ref, qpT_ref, vT_ref, kstat_ref, qstat_ref, carry_ref, *, tm, tk):
    @pl.when(pl.program_id(1) == 0)
    def _():
        carry_ref[...] = jnp.zeros_like(carry_ref)

    x = x_ref[...]
    ms = jnp.mean(x * x, axis=-1, keepdims=True)
    hb = ((x * lax.rsqrt(ms + EPS)) * g1_ref[...]).astype(BF16)

    zvg = _dot(hb, wvg_ref[...])
    a_ref[...] = zvg[:, :D_CONV] * jax.nn.sigmoid(zvg[:, D_CONV:])

    kk = _dot(hb, wk_ref[...])

    zf = _dot(hb, wf_ref[...]) + bf_ref[...]
    lf = jnp.minimum(zf, 0.0) - jnp.log1p(jnp.exp(-jnp.abs(zf)))
    lane = lax.broadcasted_iota(jnp.int32, (tm, LANES), 1)
    hi, mid, lo = _split3(lf)
    lf3 = jnp.where(lane < PIECE_STRIDE, hi,
                    jnp.where(lane < 2 * PIECE_STRIDE, mid,
                              jnp.where(lane < 3 * PIECE_STRIDE, lo, 0.0))).astype(BF16)
    cs3 = _dot(ltri_ref[...], lf3)
    c = (cs3 + pltpu.roll(cs3, LANES - PIECE_STRIDE, 1)
         + pltpu.roll(cs3, LANES - 2 * PIECE_STRIDE, 1)) + carry_ref[...]
    carry_ref[...] = c[tm - 1:tm, :]

    nhi, nmid, nlo = _split3(c * (-LOG2E))
    p3 = jnp.where(lane < PIECE_STRIDE, nhi,
                   jnp.where(lane < 2 * PIECE_STRIDE, pltpu.roll(nmid, PIECE_STRIDE, 1),
                             jnp.where(lane < 3 * PIECE_STRIDE, pltpu.roll(nlo, 2 * PIECE_STRIDE, 1), 0.0)))
    p3_hi = pltpu.roll(p3, HEAD_DIM, 1)

    qT = _dot_nt(wqT_ref[...], hb) * LOG2E
    vT = _dot_nt(wvT_ref[...], hb)
    row = lax.broadcasted_iota(jnp.int32, (HEAD_DIM, tm), 0)
    vrow = lax.broadcasted_iota(jnp.int32, (V_ROWS - HEAD_DIM, tk), 0)
    v_tail = jnp.where(vrow == 0, 1.0, 0.0).astype(BF16)

    for h in range(N_HEADS):
        kcol = kk[:, (h // 2) * LANES:(h // 2 + 1) * LANES]
        ones_h = jnp.where(_piece_lane_mask(row, h), 1.0, 0.0)
        q_h = qT[h * HEAD_DIM:(h + 1) * HEAD_DIM, :]
        if h % 2 == 0:
            ext = jnp.where(_piece_lane_mask(lane, h + HEAD_DIM), p3_hi, 0.0)
            kp = jnp.where(lane < HEAD_DIM, kcol, ext)
            qp = jnp.concatenate([q_h, ones_h], axis=0)
        else:
            ext = jnp.where(_piece_lane_mask(lane, h), p3, 0.0)
            kp = jnp.where(lane >= HEAD_DIM, kcol, ext)
            qp = jnp.concatenate([ones_h, q_h], axis=0)
        qpT_ref[0, h] = qp.astype(BF16)
        for cidx in range(tm // tk):
            kp_ref[0, h, cidx] = kp[cidx * tk:(cidx + 1) * tk, :].astype(BF16)
            v_h = vT[h * HEAD_DIM:(h + 1) * HEAD_DIM, cidx * tk:(cidx + 1) * tk].astype(BF16)
            vT_ref[0, h, cidx] = jnp.concatenate([v_h, v_tail], axis=0)

    kb = kk.astype(BF16).astype(F32)
    kn2 = jnp.max(_dot((kb * kb).astype(BF16), hsel_ref[...]), axis=0, keepdims=True)
    srow = lax.broadcasted_iota(jnp.int32, (SUBLANES, LANES), 0)
    kstat_ref[0, 0] = jnp.where(srow == 0, c[0:1, :], jnp.where(srow == 1, c[tm - 1:tm, :],
                                                                 jnp.where(srow == 2, kn2, 0.0)))
    qb = qT.astype(BF16).astype(F32)
    q2 = qb * qb
    qn2 = jnp.concatenate([jnp.sum(q2[h * HEAD_DIM:(h + 1) * HEAD_DIM, :], axis=0, keepdims=True)
                           for h in range(N_HEADS)], axis=0)
    qstat_ref[0, 0] = jnp.broadcast_to(jnp.max(qn2, axis=1, keepdims=True), (N_HEADS, LANES))


def _inproj(x2, g1, wvg, wk, wf, bf3, wqT, wvT, *, batch, seq, tm, tk):
    nt = seq // tm
    nk = seq // tk
    ltri = jnp.tril(jnp.ones((tm, tm), F32)).astype(BF16)
    hsel = (jnp.arange(D_ATT)[:, None] // HEAD_DIM == jnp.arange(LANES)[None, :]).astype(BF16)
    const = lambda shape: pl.BlockSpec(shape, lambda b, t: (0,) * len(shape))
    stat_spec = pl.BlockSpec((1, 1, SUBLANES, LANES), lambda b, t: (b, t, 0, 0))
    stat_shape = jax.ShapeDtypeStruct((batch, nt, SUBLANES, LANES), F32)
    return pl.pallas_call(
        functools.partial(_inproj_kernel, tm=tm, tk=tk),
        grid=(batch, nt),
        in_specs=[
            pl.BlockSpec((tm, D_MODEL), lambda b, t: (b * nt + t, 0)),
            const((1, D_MODEL)), const((D_MODEL, 2 * D_CONV)), const((D_MODEL, D_ATT)),
            const((D_MODEL, LANES)), const((1, LANES)), const((D_ATT, D_MODEL)), const((D_ATT, D_MODEL)),
            const((tm, tm)), const((D_ATT, LANES)),
        ],
        out_specs=[
            pl.BlockSpec((tm, D_CONV), lambda b, t: (b * nt + t, 0)),
            pl.BlockSpec((1, N_HEADS, tm // tk, tk, KAUG), lambda b, t: (b, 0, t, 0, 0)),
            pl.BlockSpec((1, N_HEADS, KAUG, tm), lambda b, t: (b, 0, 0, t)),
            pl.BlockSpec((1, N_HEADS, tm // tk, V_ROWS, tk), lambda b, t: (b, 0, t, 0, 0)),
            stat_spec, stat_spec,
        ],
        out_shape=[
            jax.ShapeDtypeStruct((batch * seq, D_CONV), F32),
            jax.ShapeDtypeStruct((batch, N_HEADS, nk, tk, KAUG), BF16),
            jax.ShapeDtypeStruct((batch, N_HEADS, KAUG, seq), BF16),
            jax.ShapeDtypeStruct((batch, N_HEADS, nk, V_ROWS, tk), BF16),
            stat_shape, stat_shape,
        ],
        scratch_shapes=[pltpu.VMEM((1, LANES), F32)],
        compiler_params=pltpu.CompilerParams(
            dimension_semantics=("arbitrary", "arbitrary"), vmem_limit_bytes=VMEM_LIMIT),
        name="inproj",
    )(x2, g1, wvg, wk, wf, bf3, wqT, wvT, ltri, hsel)


def _conv_kernel(a_ref, w_ref, b_ref, lng_ref, lnb_ref, og_ref, out_ref, sh_ref, acc_ref, *, tc, rc):
    n_cb = D_CONV // LANES

    @pl.when(pl.program_id(1) == 0)
    def _():
        for cb in range(n_cb):
            sh_ref[0, cb, 0:CONV_HALO, :] = jnp.zeros((CONV_HALO, LANES), F32)

    @pl.when(pl.program_id(1) > 0)
    def _():
        for cb in range(n_cb):
            sh_ref[0, cb, 0:CONV_HALO, :] = sh_ref[0, cb, tc:tc + CONV_HALO, :]

    n_sh = tc + CONV_HALO - SUBLANES
    for cb in range(n_cb):
        sh_ref[0, cb, CONV_HALO:CONV_HALO + tc, :] = a_ref[:, cb * LANES:(cb + 1) * LANES]
        for f in range(1, SUBLANES):
            sh_ref[f, cb, 0:n_sh, :] = sh_ref[0, cb, f:f + n_sh, :]

    base = CONV_HALO - (CONV_WIDTH - 1)
    for cb in range(n_cb):
        cols = slice(cb * LANES, (cb + 1) * LANES)

        def chunk(c, carry, cb=cb, cols=cols):
            r0 = pl.multiple_of(c * rc, rc)
            acc = jnp.zeros((rc, LANES), F32)
            for j in range(CONV_WIDTH):
                f = (base + j) % SUBLANES
                acc = acc + w_ref[j:j + 1, cols] * sh_ref[f, cb, pl.ds(r0 + (base + j - f), rc), :]
            acc_ref[pl.ds(r0, rc), cols] = acc
            return carry

        lax.fori_loop(0, tc // rc, chunk, 0)

    y = acc_ref[...] + b_ref[...]
    mu = jnp.mean(y, axis=-1, keepdims=True)
    yc = y - mu
    var = jnp.mean(yc * yc, axis=-1, keepdims=True)
    yn = yc * lax.rsqrt(var + EPS) * lng_ref[...] + lnb_ref[...]
    s = yn * jax.nn.sigmoid(yn)
    ms = jnp.mean(s * s, axis=-1, keepdims=True)
    out_ref[...] = (s * lax.rsqrt(ms + EPS) * og_ref[...]).astype(BF16)


def _conv(a, w_dw, b_dw, ln_g, ln_b, og, *, batch, seq, tc, rc):
    nt = seq // tc
    const = lambda shape: pl.BlockSpec(shape, lambda b, t: (0,) * len(shape))
    return pl.pallas_call(
        functools.partial(_conv_kernel, tc=tc, rc=rc),
        grid=(batch, nt),
        in_specs=[pl.BlockSpec((tc, D_CONV), lambda b, t: (b * nt + t, 0)),
                  const((CONV_WIDTH, D_CONV)), const((1, D_CONV)), const((1, D_CONV)),
                  const((1, D_CONV)), const((1, D_CONV))],
        out_specs=pl.BlockSpec((tc, D_CONV), lambda b, t: (b * nt + t, 0)),
        out_shape=jax.ShapeDtypeStruct((batch * seq, D_CONV), BF16),
        scratch_shapes=[pltpu.VMEM((SUBLANES, D_CONV // LANES, tc + CONV_HALO, LANES), F32),
                        pltpu.VMEM((tc, D_CONV), F32)],
        compiler_params=pltpu.CompilerParams(
            dimension_semantics=("arbitrary", "arbitrary"), vmem_limit_bytes=VMEM_LIMIT),
        name="conv",
    )(a, w_dw, b_dw, ln_g, ln_b, og)


def _attn_kernel(g0_ref, qT_ref, k_ref, v_ref, o_ref, sa_ref, sb_ref, cma_ref, cmb_ref, *, tq, tk):
    i = pl.program_id(2)
    qT = qT_ref[0, 0]
    n_sub = tq // tk

    def scores(g, s_ref, cm_ref):
        for d in range(n_sub):
            s = _dot(k_ref[0, 0, g * n_sub + d], qT)
            s_ref[d] = s
            cm_ref[d] = jnp.max(s, axis=0, keepdims=True)

    def update(m, acc, s, cmax, v):
        m_new = jnp.maximum(m, cmax)
        p = jnp.exp2(s - m_new).astype(BF16)
        return m_new, jnp.exp2(m - m_new) * acc + _dot(v, p)

    def consume(g, s_ref, cm_ref, carry, nxt=None):
        m, acc = carry
        m_new = m
        for d in range(n_sub):
            m_new = jnp.maximum(m_new, cm_ref[d])
        ps = []
        for d in range(n_sub):
            if nxt is not None:
                g_n, sn_ref, cmn_ref = nxt
                s = _dot(k_ref[0, 0, g_n * n_sub + d], qT)
                sn_ref[d] = s
                cmn_ref[d] = jnp.max(s, axis=0, keepdims=True)
            ps.append(jnp.exp2(s_ref[d] - m_new).astype(BF16))
        p = jnp.concatenate(ps, axis=0)
        v = jnp.concatenate([v_ref[0, 0, g * n_sub + d] for d in range(n_sub)], axis=1)
        return m_new, jnp.exp2(m - m_new) * acc + _dot(v, p)

    key = lax.broadcasted_iota(jnp.int32, (tk, tk), 0)
    qry = lax.broadcasted_iota(jnp.int32, (tk, tk), 1)

    def consume_diag(s_ref, carry):
        m, acc = carry
        for d in range(n_sub):
            lo = d * tk
            s_diag = jnp.where(key <= qry, s_ref[d, :, lo:lo + tk], -jnp.inf)
            s = s_diag if lo + tk == tq else jnp.concatenate([s_diag, s_ref[d, :, lo + tk:]], axis=1)
            m_d, acc_d = update(m[:, lo:], acc[:, lo:], s, jnp.max(s, axis=0, keepdims=True),
                                v_ref[0, 0, i * n_sub + d])
            if lo == 0:
                m, acc = m_d, acc_d
            else:
                m = jnp.concatenate([m[:, :lo], m_d], axis=1)
                acc = jnp.concatenate([acc[:, :lo], acc_d], axis=1)
        o_ref[0] = acc[:HEAD_DIM] / acc[HEAD_DIM:HEAD_DIM + 1]

    g0 = g0_ref[(pl.program_id(0) * N_HEADS + pl.program_id(1)) * pl.num_programs(2) + i]
    n_full = i - g0

    def pair(p, carry):
        g = g0 + 2 * p
        carry = consume(g, sa_ref, cma_ref, carry, nxt=(g + 1, sb_ref, cmb_ref))
        return consume(g + 1, sb_ref, cmb_ref, carry, nxt=(g + 2, sa_ref, cma_ref))

    scores(g0, sa_ref, cma_ref)
    init = (jnp.full((1, tq), -jnp.inf, F32), jnp.zeros((V_ROWS, tq), F32))
    carry = lax.fori_loop(0, n_full // 2, pair, init)

    @pl.when(n_full % 2 == 1)
    def _():
        consume_diag(sb_ref, consume(i - 1, sa_ref, cma_ref, carry, nxt=(i, sb_ref, cmb_ref)))

    @pl.when(n_full % 2 == 0)
    def _():
        consume_diag(sa_ref, carry)


def _first_groups(kstat, qstat, *, tm, tq):
    per = tq // tm
    bound = jnp.sqrt(jnp.max(kstat[:, :, 2, :N_HEADS], axis=1) * jnp.max(qstat[:, :, :, 0], axis=1))
    bound = bound * NORM_SLACK
    cq = kstat[:, ::per, 0, :N_HEADS]
    ck = kstat[:, per - 1::per, 1, :N_HEADS]
    d = (cq[:, :, None, :] - ck[:, None, :, :]) * LOG2E
    nq = cq.shape[1]
    earlier = jnp.arange(nq)[None, :, None, None] > jnp.arange(nq)[None, None, :, None]
    skip = earlier & (2.0 * bound[:, None, None, :] + d < -SKIP_LOG2)
    g0 = jnp.sum(jnp.cumprod(skip.astype(jnp.int32), axis=2), axis=2)
    return jnp.transpose(g0, (0, 2, 1)).reshape(-1).astype(jnp.int32)


def _attention(g0, qpT, kp, vT, *, batch, seq, tq, tk):
    nk = seq // tk
    grid_spec = pltpu.PrefetchScalarGridSpec(
        num_scalar_prefetch=1,
        grid=(batch, N_HEADS, seq // tq),
        in_specs=[
            pl.BlockSpec((1, 1, KAUG, tq), lambda b, h, i, g0: (b, h, 0, i)),
            pl.BlockSpec((1, 1, nk, tk, KAUG), lambda b, h, i, g0: (b, h, 0, 0, 0)),
            pl.BlockSpec((1, 1, nk, V_ROWS, tk), lambda b, h, i, g0: (b, h, 0, 0, 0)),
        ],
        out_specs=pl.BlockSpec((1, HEAD_DIM, tq), lambda b, h, i, g0: (b, h, i)),
        scratch_shapes=[pltpu.VMEM((tq // tk, tk, tq), F32), pltpu.VMEM((tq // tk, tk, tq), F32),
                        pltpu.VMEM((tq // tk, 1, tq), F32), pltpu.VMEM((tq // tk, 1, tq), F32)],
    )
    return pl.pallas_call(
        functools.partial(_attn_kernel, tq=tq, tk=tk),
        grid_spec=grid_spec,
        out_shape=jax.ShapeDtypeStruct((batch, D_ATT, seq), F32),
        compiler_params=pltpu.CompilerParams(
            dimension_semantics=("arbitrary", "arbitrary", "arbitrary"), vmem_limit_bytes=VMEM_LIMIT),
        name="attention",
    )(g0, qpT, kp, vT)


def _outproj_kernel(x_ref, mc_ref, yT_ref, woc_ref, woa_ref, ga_ref, g2_ref, wr_ref, br_ref, ltri_ref,
                    x1_ref, h2_ref, route_ref, cnt_ref, carry_ref, *, tm):
    first = (pl.program_id(0) == 0) & (pl.program_id(1) == 0)

    @pl.when(first)
    def _():
        carry_ref[...] = jnp.zeros_like(carry_ref)

    yT = yT_ref[0]
    msa = jnp.mean(yT * yT, axis=0, keepdims=True)
    yn = (yT * lax.rsqrt(msa + EPS)).T * ga_ref[...]
    x1 = x_ref[...] + _dot(mc_ref[...], woc_ref[...]) + _dot(yn.astype(BF16), woa_ref[...])
    x1_ref[...] = x1
    ms = jnp.mean(x1 * x1, axis=-1, keepdims=True)
    h2 = (x1 * lax.rsqrt(ms + EPS)) * g2_ref[...]
    h2_ref[:, :D_MODEL] = h2

    lg = _dot(h2.astype(BF16), wr_ref[...]) + br_ref[...]
    lane = lax.broadcasted_iota(jnp.int32, (tm, LANES), 1)
    neg = -jnp.inf
    big = jnp.int32(LANES)

    lg1 = jnp.where(lane < N_GROUPS, lg, neg)
    m1 = jnp.max(lg1, axis=-1, keepdims=True)
    p1_sel = 1.0 / jnp.sum(jnp.exp(lg1 - m1), axis=-1, keepdims=True)
    grp = jnp.min(jnp.where(lg1 == m1, lane, big), axis=-1, keepdims=True)

    lo_lane = N_GROUPS + EXPERTS_PER_GROUP * grp
    v = jnp.where((lane >= lo_lane) & (lane < lo_lane + EXPERTS_PER_GROUP), lg, neg)
    v1 = jnp.max(v, axis=-1, keepdims=True)
    j1 = jnp.min(jnp.where(v == v1, lane, big), axis=-1, keepdims=True)
    vv = jnp.where(lane == j1, neg, v)
    v2 = jnp.max(vv, axis=-1, keepdims=True)
    j2 = jnp.min(jnp.where(vv == v2, lane, big), axis=-1, keepdims=True)
    e21 = jnp.exp(v2 - v1)
    w0 = p1_sel / (1.0 + e21)
    w1 = p1_sel * e21 / (1.0 + e21)
    e0 = j1 - N_GROUPS
    e1 = j2 - N_GROUPS

    swap = e1 < e0
    ea = jnp.where(swap, e1, e0)
    eb = jnp.where(swap, e0, e1)
    wa = jnp.where(swap, w1, w0)
    wb = jnp.where(swap, w0, w1)
    al = ea - EXPERTS_PER_GROUP * grp
    bl = eb - EXPERTS_PER_GROUP * grp
    cid = PAIRS_PER_GROUP * grp + ((al * (2 * EXPERTS_PER_GROUP - 1 - al)) >> 1) + (bl - al - 1)

    oh = lane == cid
    cmat = jnp.where(oh, 1.0, 0.0)
    prefix = _dot(ltri_ref[...], cmat.astype(BF16)) + carry_ref[...]
    rank = jnp.sum(jnp.where(oh, prefix, 0.0), axis=-1, keepdims=True)
    carry = carry_ref[...] + jnp.sum(cmat, axis=0, keepdims=True)
    carry_ref[...] = carry
    cnt_ref[...] = jnp.broadcast_to(carry, cnt_ref.shape)

    h2_ref[:, D_MODEL:] = jnp.where(lane == 0, wa, jnp.where(lane == 1, wb, 0.0))
    route_ref[...] = jnp.where(lane == 0, cid.astype(F32), jnp.where(lane == 1, rank, 0.0))


def _outproj(x2, mc, yT, woc, woa, ga, g2, wr, br, *, batch, seq, tm):
    nt = seq // tm
    ltri = jnp.tril(jnp.ones((tm, tm), F32), -1).astype(BF16)
    const = lambda shape: pl.BlockSpec(shape, lambda b, t: (0,) * len(shape))
    row_spec = lambda w: pl.BlockSpec((tm, w), lambda b, t: (b * nt + t, 0))
    return pl.pallas_call(
        functools.partial(_outproj_kernel, tm=tm),
        grid=(batch, nt),
        in_specs=[row_spec(D_MODEL), row_spec(D_CONV),
                  pl.BlockSpec((1, D_ATT, tm), lambda b, t: (b, 0, t)),
                  const((D_CONV, D_MODEL)), const((D_ATT, D_MODEL)), const((1, D_ATT)), const((1, D_MODEL)),
                  const((D_MODEL, LANES)), const((1, LANES)), const((tm, tm))],
        out_specs=[row_spec(D_MODEL), row_spec(ROW_W), row_spec(LANES), const((8, LANES))],
        out_shape=[jax.ShapeDtypeStruct((batch * seq, D_MODEL), F32),
                   jax.ShapeDtypeStruct((batch * seq, ROW_W), F32),
                   jax.ShapeDtypeStruct((batch * seq, LANES), F32),
                   jax.ShapeDtypeStruct((8, LANES), F32)],
        scratch_shapes=[pltpu.VMEM((1, LANES), F32)],
        compiler_params=pltpu.CompilerParams(
            dimension_semantics=("arbitrary", "arbitrary"), vmem_limit_bytes=VMEM_LIMIT),
        name="outproj",
    )(x2, mc, yT, woc, woa, ga, g2, wr, br, ltri)


def _row_copies(src_at, dst_at, sem):
    return pltpu.make_async_copy(src_at, dst_at, sem)


def _dispatch_kernel(dest_ref, h2_ref, xs_ref, sem, *, tm):
    def body(g, carry):
        for u in range(ISSUE_UNROLL):
            r = g * ISSUE_UNROLL + u
            _row_copies(h2_ref.at[pl.ds(r, 1)], xs_ref.at[pl.ds(dest_ref[r], 1)], sem).start(priority=u % 2)
        return carry

    lax.fori_loop(0, tm // ISSUE_UNROLL, body, 0)
    _row_copies(h2_ref, xs_ref.at[pl.ds(0, tm)], sem).wait()


def _dispatch(dest, h2, *, tm):
    t_len = h2.shape[0]
    return pl.pallas_call(
        functools.partial(_dispatch_kernel, tm=tm),
        grid=(t_len // tm,),
        in_specs=[pl.BlockSpec((tm,), lambda i: (i,), memory_space=pltpu.SMEM),
                  pl.BlockSpec((tm, ROW_W), lambda i: (i, 0))],
        out_specs=pl.BlockSpec(memory_space=pl.ANY),
        out_shape=jax.ShapeDtypeStruct((t_len, ROW_W), F32),
        scratch_shapes=[pltpu.SemaphoreType.DMA(())],
        compiler_params=pltpu.CompilerParams(dimension_semantics=("arbitrary",), vmem_limit_bytes=VMEM_LIMIT),
        name="dispatch",
    )(dest, h2)


def _expert_kernel(blk_ref, ea_ref, eb_ref, lo_ref, hi_ref, x_ref, wga_ref, wua_ref, wda_ref,
                   wgb_ref, wub_ref, wdb_ref, y_ref):
    del blk_ref, ea_ref, eb_ref
    i = pl.program_id(0)
    lo = lo_ref[i]
    hi = hi_ref[i]

    def mlp(xb, wg_ref, wu_ref, wd_ref):
        g = _dot(xb, wg_ref[0])
        u = _dot(xb, wu_ref[0])
        return _dot(((g * jax.nn.sigmoid(g)) * u).astype(BF16), wd_ref[0])

    @pl.when(hi > lo)
    def _():
        xb = x_ref[:, :D_MODEL].astype(BF16)
        wts = x_ref[:, D_MODEL:]
        y = wts[:, 0:1] * mlp(xb, wga_ref, wua_ref, wda_ref) + wts[:, 1:2] * mlp(xb, wgb_ref, wub_ref, wdb_ref)
        row = lax.broadcasted_iota(jnp.int32, (ROW_BLOCK, 1), 0)
        mine = (row >= lo) & (row < hi)

        @pl.when(lo == 0)
        def _():
            y_ref[...] = jnp.where(mine, y, 0.0)

        @pl.when(lo > 0)
        def _():
            y_ref[...] = jnp.where(mine, y, y_ref[...])


def _experts(items, xs, wg, wu, wd):
    n_rows = xs.shape[0]
    n_items = items[0].shape[0]
    wspec = lambda shape, which: pl.BlockSpec(
        shape, lambda i, blk, ea, eb, lo, hi: ((ea, eb)[which][i], 0, 0))
    grid_spec = pltpu.PrefetchScalarGridSpec(
        num_scalar_prefetch=5,
        grid=(n_items,),
        in_specs=[pl.BlockSpec((ROW_BLOCK, ROW_W), lambda i, blk, ea, eb, lo, hi: (blk[i], 0)),
                  wspec((1, D_MODEL, D_EXPERT), 0), wspec((1, D_MODEL, D_EXPERT), 0), wspec((1, D_EXPERT, D_MODEL), 0),
                  wspec((1, D_MODEL, D_EXPERT), 1), wspec((1, D_MODEL, D_EXPERT), 1), wspec((1, D_EXPERT, D_MODEL), 1)],
        out_specs=pl.BlockSpec((ROW_BLOCK, D_MODEL), lambda i, blk, ea, eb, lo, hi: (blk[i], 0)),
    )
    return pl.pallas_call(
        _expert_kernel,
        grid_spec=grid_spec,
        out_shape=jax.ShapeDtypeStruct((n_rows, D_MODEL), F32),
        compiler_params=pltpu.CompilerParams(dimension_semantics=("arbitrary",), vmem_limit_bytes=VMEM_LIMIT),
        name="experts",
    )(*items, xs, wg, wu, wd, wg, wu, wd)


def _combine_kernel(dcur_ref, dnxt_ref, x1_ref, g_ref, ys_ref, out_ref, buf_ref, sems, *, tm):
    i = pl.program_id(0)
    slot = i % 2

    def issue(d_ref, s):
        def body(g, carry):
            for u in range(ISSUE_UNROLL):
                r = g * ISSUE_UNROLL + u
                _row_copies(ys_ref.at[pl.ds(d_ref[r], 1)], buf_ref.at[s, pl.ds(r, 1)],
                            sems.at[s]).start(priority=u % 2)
            return carry

        lax.fori_loop(0, tm // ISSUE_UNROLL, body, 0)

    @pl.when(i == 0)
    def _():
        issue(dcur_ref, 0)

    @pl.when(i + 1 < pl.num_programs(0))
    def _():
        issue(dnxt_ref, 1 - slot)

    _row_copies(ys_ref.at[pl.ds(0, tm)], buf_ref.at[slot], sems.at[slot]).wait()
    x2 = x1_ref[...] + buf_ref[slot]
    ms = jnp.mean(x2 * x2, axis=-1, keepdims=True)
    out_ref[...] = (x2 * lax.rsqrt(ms + EPS)) * g_ref[...]


def _combine(dest, x1, gf, ys, *, tm):
    t_len = x1.shape[0]
    n = t_len // tm
    return pl.pallas_call(
        functools.partial(_combine_kernel, tm=tm),
        grid=(n,),
        in_specs=[pl.BlockSpec((tm,), lambda i: (i,), memory_space=pltpu.SMEM),
                  pl.BlockSpec((tm,), lambda i: (jnp.minimum(i + 1, n - 1),), memory_space=pltpu.SMEM),
                  pl.BlockSpec((tm, D_MODEL), lambda i: (i, 0)),
                  pl.BlockSpec((1, D_MODEL), lambda i: (0, 0)),
                  pl.BlockSpec(memory_space=pl.ANY)],
        out_specs=pl.BlockSpec((tm, D_MODEL), lambda i: (i, 0)),
        out_shape=jax.ShapeDtypeStruct((t_len, D_MODEL), F32),
        scratch_shapes=[pltpu.VMEM((2, tm, D_MODEL), F32), pltpu.SemaphoreType.DMA((2,))],
        compiler_params=pltpu.CompilerParams(dimension_semantics=("arbitrary",), vmem_limit_bytes=VMEM_LIMIT),
        name="combine",
    )(dest, dest, x1, gf, ys)


def _layer(x, norm1_g, w_in, b_f, w_dw, b_dw, conv_ln_g, conv_ln_b, out_g_conv, out_g_att, w_out,
           norm2_g, w_r1, b_r1, w_r2, b_r2, w_gate, w_up, w_down):
    batch, seq, d = x.shape
    t_len = batch * seq
    tm = min(512, seq)
    tk = min(256, seq)
    tq = min(1024, seq)
    o1, o2 = D_CONV, 2 * D_CONV
    o3, o4, o5 = o2 + D_ATT, o2 + 2 * D_ATT, o2 + 3 * D_ATT

    wvg = w_in[:, :o2].astype(BF16)
    wk = w_in[:, o3:o4].astype(BF16)
    wqT = (w_in[:, o2:o3] * (HEAD_DIM ** -0.5)).T.astype(BF16)
    wvT = w_in[:, o4:o5].T.astype(BF16)
    assert PIECE_STRIDE == N_HEADS
    pad = LANES - N_PIECES * PIECE_STRIDE
    wf = jnp.pad(jnp.tile(w_in[:, o5:], (1, N_PIECES)), ((0, 0), (0, pad))).astype(BF16)
    bf3 = jnp.pad(jnp.tile(b_f.astype(F32).reshape(1, N_HEADS), (1, N_PIECES)), ((0, 0), (0, pad)))

    x2 = x.reshape(t_len, d)
    a, kp, qpT, vT, kstat, qstat = _inproj(x2, norm1_g.reshape(1, d), wvg, wk, wf, bf3, wqT, wvT,
                                           batch=batch, seq=seq, tm=tm, tk=tk)
    mc = _conv(a, w_dw, b_dw.reshape(1, -1), conv_ln_g.reshape(1, -1), conv_ln_b.reshape(1, -1),
               out_g_conv.reshape(1, -1), batch=batch, seq=seq, tc=min(256, seq), rc=64)
    yT = _attention(_first_groups(kstat, qstat, tm=tm, tq=tq), qpT, kp, vT, batch=batch, seq=seq, tq=tq, tk=tk)

    rpad = LANES - N_GROUPS - N_EXPERTS
    wr = jnp.pad(jnp.concatenate([w_r1, jnp.transpose(w_r2, (1, 0, 2)).reshape(d, N_EXPERTS)], axis=1),
                 ((0, 0), (0, rpad)))
    br = jnp.pad(jnp.concatenate([b_r1.astype(F32), b_r2.reshape(-1).astype(F32)]).reshape(1, -1),
                 ((0, 0), (0, rpad)))
    x1, h2, route, cnt = _outproj(x2, mc, yT, w_out[:D_CONV].astype(BF16), w_out[D_CONV:].astype(BF16),
                                  out_g_att.reshape(1, -1), norm2_g.reshape(1, d), wr.astype(BF16), br,
                                  batch=batch, seq=seq, tm=tm)

    i32 = jnp.int32
    lanes = jnp.arange(LANES, dtype=i32)
    pick = lambda table, idx: jnp.sum(jnp.where(idx[:, None] == lanes, table[None, :], 0), axis=1).astype(i32)
    counts = cnt[0].astype(i32)
    ends = jnp.cumsum(counts).astype(i32)
    starts = ends - counts
    dest = pick(starts, route[:, 0].astype(i32)) + route[:, 1].astype(i32)

    n_blocks = t_len // ROW_BLOCK
    b_lo = starts // ROW_BLOCK
    n_it = jnp.where(counts > 0, (ends - 1) // ROW_BLOCK - b_lo + 1, 0)
    it_end = jnp.cumsum(n_it).astype(i32)
    it_start = it_end - n_it
    idx = jnp.arange(n_blocks + N_CLASSES, dtype=i32)
    valid = idx < it_end[-1]
    last_cls = jnp.max(jnp.where(counts > 0, lanes, 0))
    cls = jnp.where(valid, jnp.sum(it_end[None, :] <= idx[:, None], axis=1).astype(i32), last_cls)
    blk = jnp.where(valid, pick(b_lo, cls) + idx - pick(it_start, cls), n_blocks - 1)
    row0 = blk * ROW_BLOCK
    lo = jnp.where(valid, jnp.maximum(pick(starts, cls), row0) - row0, 0)
    hi = jnp.where(valid, jnp.minimum(pick(ends, cls), row0 + ROW_BLOCK) - row0, 0)
    pair_a, pair_b = [], []
    for g in range(N_GROUPS):
        for a_loc in range(EXPERTS_PER_GROUP):
            for b_loc in range(a_loc + 1, EXPERTS_PER_GROUP):
                pair_a.append(g * EXPERTS_PER_GROUP + a_loc)
                pair_b.append(g * EXPERTS_PER_GROUP + b_loc)
    cpad = [0] * (LANES - N_CLASSES)
    items = (blk.astype(i32), pick(jnp.array(pair_a + cpad, i32), cls), pick(jnp.array(pair_b + cpad, i32), cls),
             lo.astype(i32), hi.astype(i32))

    xs = _dispatch(dest, h2, tm=min(1024, t_len))
    ys = _experts(items, xs, w_gate.astype(BF16), w_up.astype(BF16), w_down.astype(BF16))
    return dest, x1, ys


def kernel(x, norm1_g, w_in, b_f, w_dw, b_dw, conv_ln_g, conv_ln_b, out_g_conv, out_g_att, w_out, norm2_g,
           w_r1, b_r1, w_r2, b_r2, w_gate, w_up, w_down, final_g):
    assert norm1_g.shape[0] == 1, "single-layer stack"
    batch, seq, d = x.shape
    dest, x1, ys = _layer(
        x, norm1_g[0], w_in[0], b_f[0], w_dw[0], b_dw[0], conv_ln_g[0], conv_ln_b[0], out_g_conv[0],
        out_g_att[0], w_out[0], norm2_g[0], w_r1[0], b_r1[0], w_r2[0], b_r2[0], w_gate[0], w_up[0], w_down[0])
    out = _combine(dest, x1, final_g.reshape(1, d), ys, tm=min(512, batch * seq))
    return out.reshape(batch, seq, d)
```

```python
import functools

import jax
import jax.numpy as jnp
from jax import lax
from jax.experimental import pallas as pl
from jax.experimental.pallas import tpu as pltpu

D_MODEL = 1024
D_CONV = 512
N_HEADS = 8
HEAD_DIM = 64
D_ATT = N_HEADS * HEAD_DIM
CONV_WIDTH = 31
N_GROUPS = 4
EXPERTS_PER_GROUP = 8
N_EXPERTS = N_GROUPS * EXPERTS_PER_GROUP
TOP_K = 2
D_EXPERT = D_MODEL // 4
ROW_BLOCK = 256
EPS = 1e-6

LANES = 128
SUBLANES = 8
KAUG = 128
N_PIECES = 3
PIECE_STRIDE = 8
CONV_HALO = 32
PAIRS_PER_GROUP = EXPERTS_PER_GROUP * (EXPERTS_PER_GROUP - 1) // 2
N_CLASSES = N_GROUPS * PAIRS_PER_GROUP
ROW_W = D_MODEL + LANES
SKIP_LOG2 = 160.0
NORM_SLACK = 1.02
V_ROWS = 80
LOG2E = 1.4426950408889634
VMEM_LIMIT = 56 * 1024 * 1024

F32 = jnp.float32
BF16 = jnp.bfloat16


def _dot(a, b):
    return jnp.dot(a, b, preferred_element_type=F32)


def _dot_nt(a, b):
    return lax.dot_general(a, b, (((1,), (1,)), ((), ())), preferred_element_type=F32)


def _split3(x):
    hi = x.astype(BF16)
    r1 = x - hi.astype(F32)
    mid = r1.astype(BF16)
    lo = (r1 - mid.astype(F32)).astype(BF16)
    return hi.astype(F32), mid.astype(F32), lo.astype(F32)


def _piece_lane_mask(lane, h):
    return (lane == h) | (lane == h + PIECE_STRIDE) | (lane == h + 2 * PIECE_STRIDE)


def _inproj_kernel(x_ref, g1_ref, wvg_ref, wk_ref, wf_ref, bf_ref, wqT_ref, wvT_ref, ltri_ref, hsel_ref,
                   a_ref, kp_ref, qpT_ref, vT_ref, kstat_ref, qstat_ref, carry_ref, *, tm, tk):
    @pl.when(pl.program_id(1) == 0)
    def _():
        carry_ref[...] = jnp.zeros_like(carry_ref)

    x = x_ref[...]
    ms = jnp.mean(x * x, axis=-1, keepdims=True)
    hb = ((x * lax.rsqrt(ms + EPS)) * g1_ref[...]).astype(BF16)

    zvg = _dot(hb, wvg_ref[...])
    a_ref[...] = zvg[:, :D_CONV] * jax.nn.sigmoid(zvg[:, D_CONV:])

    kk = _dot(hb, wk_ref[...])

    zf = _dot(hb, wf_ref[...]) + bf_ref[...]
    lf = jnp.minimum(zf, 0.0) - jnp.log1p(jnp.exp(-jnp.abs(zf)))
    lane = lax.broadcasted_iota(jnp.int32, (tm, LANES), 1)
    hi, mid, lo = _split3(lf)
    lf3 = jnp.where(lane < PIECE_STRIDE, hi,
                    jnp.where(lane < 2 * PIECE_STRIDE, mid,
                              jnp.where(lane < 3 * PIECE_STRIDE, lo, 0.0))).astype(BF16)
    cs3 = _dot(ltri_ref[...], lf3)
    c = (cs3 + pltpu.roll(cs3, LANES - PIECE_STRIDE, 1)
         + pltpu.roll(cs3, LANES - 2 * PIECE_STRIDE, 1)) + carry_ref[...]
    carry_ref[...] = c[tm - 1:tm, :]

    nhi, nmid, nlo = _split3(c * (-LOG2E))
    p3 = jnp.where(lane < PIECE_STRIDE, nhi,
                   jnp.where(lane < 2 * PIECE_STRIDE, pltpu.roll(nmid, PIECE_STRIDE, 1),
                             jnp.where(lane < 3 * PIECE_STRIDE, pltpu.roll(nlo, 2 * PIECE_STRIDE, 1), 0.0)))
    p3_hi = pltpu.roll(p3, HEAD_DIM, 1)

    qT = _dot_nt(wqT_ref[...], hb) * LOG2E
    vT = _dot_nt(wvT_ref[...], hb)
    row = lax.broadcasted_iota(jnp.int32, (HEAD_DIM, tm), 0)
    vrow = lax.broadcasted_iota(jnp.int32, (V_ROWS - HEAD_DIM, tk), 0)
    v_tail = jnp.where(vrow == 0, 1.0, 0.0).astype(BF16)

    for h in range(N_HEADS):
        kcol = kk[:, (h // 2) * LANES:(h // 2 + 1) * LANES]
        ones_h = jnp.where(_piece_lane_mask(row, h), 1.0, 0.0)
        q_h = qT[h * HEAD_DIM:(h + 1) * HEAD_DIM, :]
        if h % 2 == 0:
            ext = jnp.where(_piece_lane_mask(lane, h + HEAD_DIM), p3_hi, 0.0)
            kp = jnp.where(lane < HEAD_DIM, kcol, ext)
            qp = jnp.concatenate([q_h, ones_h], axis=0)
        else:
            ext = jnp.where(_piece_lane_mask(lane, h), p3, 0.0)
            kp = jnp.where(lane >= HEAD_DIM, kcol, ext)
            qp = jnp.concatenate([ones_h, q_h], axis=0)
        qpT_ref[0, h] = qp.astype(BF16)
        for cidx in range(tm // tk):
            kp_ref[0, h, cidx] = kp[cidx * tk:(cidx + 1) * tk, :].astype(BF16)
            v_h = vT[h * HEAD_DIM:(h + 1) * HEAD_DIM, cidx * tk:(cidx + 1) * tk].astype(BF16)
            vT_ref[0, h, cidx] = jnp.concatenate([v_h, v_tail], axis=0)

    kn2 = jnp.max(_dot((kk * kk).astype(BF16), hsel_ref[...]), axis=0, keepdims=True)
    srow = lax.broadcasted_iota(jnp.int32, (SUBLANES, LANES), 0)
    kstat_ref[0, 0] = jnp.where(srow == 0, c[0:1, :], jnp.where(srow == 1, c[tm - 1:tm, :],
                                                                 jnp.where(srow == 2, kn2, 0.0)))
    q2 = qT * qT
    qn2 = jnp.concatenate([jnp.sum(q2[h * HEAD_DIM:(h + 1) * HEAD_DIM, :], axis=0, keepdims=True)
                           for h in range(N_HEADS)], axis=0)
    qstat_ref[0, 0] = jnp.broadcast_to(jnp.max(qn2, axis=1, keepdims=True), (N_HEADS, LANES))


def _inproj(x2, g1, wvg, wk, wf, bf3, wqT, wvT, *, batch, seq, tm, tk):
    nt = seq // tm
    nk = seq // tk
    ltri = jnp.tril(jnp.ones((tm, tm), F32)).astype(BF16)
    hsel = (jnp.arange(D_ATT)[:, None] // HEAD_DIM == jnp.arange(LANES)[None, :]).astype(BF16)
    const = lambda shape: pl.BlockSpec(shape, lambda b, t: (0,) * len(shape))
    stat_spec = pl.BlockSpec((1, 1, SUBLANES, LANES), lambda b, t: (b, t, 0, 0))
    stat_shape = jax.ShapeDtypeStruct((batch, nt, SUBLANES, LANES), F32)
    return pl.pallas_call(
        functools.partial(_inproj_kernel, tm=tm, tk=tk),
        grid=(batch, nt),
        in_specs=[
            pl.BlockSpec((tm, D_MODEL), lambda b, t: (b * nt + t, 0)),
            const((1, D_MODEL)), const((D_MODEL, 2 * D_CONV)), const((D_MODEL, D_ATT)),
            const((D_MODEL, LANES)), const((1, LANES)), const((D_ATT, D_MODEL)), const((D_ATT, D_MODEL)),
            const((tm, tm)), const((D_ATT, LANES)),
        ],
        out_specs=[
            pl.BlockSpec((tm, D_CONV), lambda b, t: (b * nt + t, 0)),
            pl.BlockSpec((1, N_HEADS, tm // tk, tk, KAUG), lambda b, t: (b, 0, t, 0, 0)),
            pl.BlockSpec((1, N_HEADS, KAUG, tm), lambda b, t: (b, 0, 0, t)),
            pl.BlockSpec((1, N_HEADS, tm // tk, V_ROWS, tk), lambda b, t: (b, 0, t, 0, 0)),
            stat_spec, stat_spec,
        ],
        out_shape=[
            jax.ShapeDtypeStruct((batch * seq, D_CONV), F32),
            jax.ShapeDtypeStruct((batch, N_HEADS, nk, tk, KAUG), BF16),
            jax.ShapeDtypeStruct((batch, N_HEADS, KAUG, seq), BF16),
            jax.ShapeDtypeStruct((batch, N_HEADS, nk, V_ROWS, tk), BF16),
            stat_shape, stat_shape,
        ],
        scratch_shapes=[pltpu.VMEM((1, LANES), F32)],
        compiler_params=pltpu.CompilerParams(
            dimension_semantics=("arbitrary", "arbitrary"), vmem_limit_bytes=VMEM_LIMIT),
        name="inproj",
    )(x2, g1, wvg, wk, wf, bf3, wqT, wvT, ltri, hsel)


def _conv_kernel(a_ref, w_ref, b_ref, lng_ref, lnb_ref, og_ref, out_ref, sh_ref, acc_ref, *, tc, rc):
    n_cb = D_CONV // LANES

    @pl.when(pl.program_id(1) == 0)
    def _():
        for cb in range(n_cb):
            sh_ref[0, cb, 0:CONV_HALO, :] = jnp.zeros((CONV_HALO, LANES), F32)

    @pl.when(pl.program_id(1) > 0)
    def _():
        for cb in range(n_cb):
            sh_ref[0, cb, 0:CONV_HALO, :] = sh_ref[0, cb, tc:tc + CONV_HALO, :]

    n_sh = tc + CONV_HALO - SUBLANES
    for cb in range(n_cb):
        sh_ref[0, cb, CONV_HALO:CONV_HALO + tc, :] = a_ref[:, cb * LANES:(cb + 1) * LANES]
        for f in range(1, SUBLANES):
            sh_ref[f, cb, 0:n_sh, :] = sh_ref[0, cb, f:f + n_sh, :]

    base = CONV_HALO - (CONV_WIDTH - 1)
    for cb in range(n_cb):
        cols = slice(cb * LANES, (cb + 1) * LANES)

        def chunk(c, carry, cb=cb, cols=cols):
            r0 = pl.multiple_of(c * rc, rc)
            acc = jnp.zeros((rc, LANES), F32)
            for j in range(CONV_WIDTH):
                f = (base + j) % SUBLANES
                acc = acc + w_ref[j:j + 1, cols] * sh_ref[f, cb, pl.ds(r0 + (base + j - f), rc), :]
            acc_ref[pl.ds(r0, rc), cols] = acc
            return carry

        lax.fori_loop(0, tc // rc, chunk, 0)

    y = acc_ref[...] + b_ref[...]
    mu = jnp.mean(y, axis=-1, keepdims=True)
    yc = y - mu
    var = jnp.mean(yc * yc, axis=-1, keepdims=True)
    yn = yc * lax.rsqrt(var + EPS) * lng_ref[...] + lnb_ref[...]
    s = yn * jax.nn.sigmoid(yn)
    ms = jnp.mean(s * s, axis=-1, keepdims=True)
    out_ref[...] = (s * lax.rsqrt(ms + EPS) * og_ref[...]).astype(BF16)


def _conv(a, w_dw, b_dw, ln_g, ln_b, og, *, batch, seq, tc, rc):
    nt = seq // tc
    const = lambda shape: pl.BlockSpec(shape, lambda b, t: (0,) * len(shape))
    return pl.pallas_call(
        functools.partial(_conv_kernel, tc=tc, rc=rc),
        grid=(batch, nt),
        in_specs=[pl.BlockSpec((tc, D_CONV), lambda b, t: (b * nt + t, 0)),
                  const((CONV_WIDTH, D_CONV)), const((1, D_CONV)), const((1, D_CONV)),
                  const((1, D_CONV)), const((1, D_CONV))],
        out_specs=pl.BlockSpec((tc, D_CONV), lambda b, t: (b * nt + t, 0)),
        out_shape=jax.ShapeDtypeStruct((batch * seq, D_CONV), BF16),
        scratch_shapes=[pltpu.VMEM((SUBLANES, D_CONV // LANES, tc + CONV_HALO, LANES), F32),
                        pltpu.VMEM((tc, D_CONV), F32)],
        compiler_params=pltpu.CompilerParams(
            dimension_semantics=("arbitrary", "arbitrary"), vmem_limit_bytes=VMEM_LIMIT),
        name="conv",
    )(a, w_dw, b_dw, ln_g, ln_b, og)


def _attn_kernel(g0_ref, qT_ref, k_ref, v_ref, o_ref, sa_ref, sb_ref, cma_ref, cmb_ref, *, tq, tk):
    i = pl.program_id(2)
    qT = qT_ref[0, 0]
    n_sub = tq // tk

    def scores(g, s_ref, cm_ref):
        for d in range(n_sub):
            s = _dot(k_ref[0, 0, g * n_sub + d], qT)
            s_ref[d] = s
            cm_ref[d] = jnp.max(s, axis=0, keepdims=True)

    def update(m, acc, s, cmax, v):
        m_new = jnp.maximum(m, cmax)
        p = jnp.exp2(s - m_new).astype(BF16)
        return m_new, jnp.exp2(m - m_new) * acc + _dot(v, p)

    def consume(g, s_ref, cm_ref, carry, nxt=None):
        m, acc = carry
        m_new = m
        for d in range(n_sub):
            m_new = jnp.maximum(m_new, cm_ref[d])
        ps = []
        for d in range(n_sub):
            if nxt is not None:
                g_n, sn_ref, cmn_ref = nxt
                s = _dot(k_ref[0, 0, g_n * n_sub + d], qT)
                sn_ref[d] = s
                cmn_ref[d] = jnp.max(s, axis=0, keepdims=True)
            ps.append(jnp.exp2(s_ref[d] - m_new).astype(BF16))
        p = jnp.concatenate(ps, axis=0)
        v = jnp.concatenate([v_ref[0, 0, g * n_sub + d] for d in range(n_sub)], axis=1)
        return m_new, jnp.exp2(m - m_new) * acc + _dot(v, p)

    key = lax.broadcasted_iota(jnp.int32, (tk, tk), 0)
    qry = lax.broadcasted_iota(jnp.int32, (tk, tk), 1)

    def consume_diag(s_ref, carry):
        m, acc = carry
        for d in range(n_sub):
            lo = d * tk
            s_diag = jnp.where(key <= qry, s_ref[d, :, lo:lo + tk], -jnp.inf)
            s = s_diag if lo + tk == tq else jnp.concatenate([s_diag, s_ref[d, :, lo + tk:]], axis=1)
            m_d, acc_d = update(m[:, lo:], acc[:, lo:], s, jnp.max(s, axis=0, keepdims=True),
                                v_ref[0, 0, i * n_sub + d])
            if lo == 0:
                m, acc = m_d, acc_d
            else:
                m = jnp.concatenate([m[:, :lo], m_d], axis=1)
                acc = jnp.concatenate([acc[:, :lo], acc_d], axis=1)
        o_ref[0] = acc[:HEAD_DIM] / acc[HEAD_DIM:HEAD_DIM + 1]

    g0 = g0_ref[(pl.program_id(0) * N_HEADS + pl.program_id(1)) * pl.num_programs(2) + i]
    n_full = i - g0

    def pair(p, carry):
        g = g0 + 2 * p
        carry = consume(g, sa_ref, cma_ref, carry, nxt=(g + 1, sb_ref, cmb_ref))
        return consume(g + 1, sb_ref, cmb_ref, carry, nxt=(g + 2, sa_ref, cma_ref))

    scores(g0, sa_ref, cma_ref)
    init = (jnp.full((1, tq), -jnp.inf, F32), jnp.zeros((V_ROWS, tq), F32))
    carry = lax.fori_loop(0, n_full // 2, pair, init)

    @pl.when(n_full % 2 == 1)
    def _():
        consume_diag(sb_ref, consume(i - 1, sa_ref, cma_ref, carry, nxt=(i, sb_ref, cmb_ref)))

    @pl.when(n_full % 2 == 0)
    def _():
        consume_diag(sa_ref, carry)


def _first_groups(kstat, qstat, *, tm, tq):
    per = tq // tm
    bound = jnp.sqrt(jnp.max(kstat[:, :, 2, :N_HEADS], axis=1) * jnp.max(qstat[:, :, :, 0], axis=1))
    bound = bound * NORM_SLACK
    cq = kstat[:, ::per, 0, :N_HEADS]
    ck = kstat[:, per - 1::per, 1, :N_HEADS]
    d = (cq[:, :, None, :] - ck[:, None, :, :]) * LOG2E
    nq = cq.shape[1]
    earlier = jnp.arange(nq)[None, :, None, None] > jnp.arange(nq)[None, None, :, None]
    skip = earlier & (2.0 * bound[:, None, None, :] + d < -SKIP_LOG2)
    g0 = jnp.sum(jnp.cumprod(skip.astype(jnp.int32), axis=2), axis=2)
    return jnp.transpose(g0, (0, 2, 1)).reshape(-1).astype(jnp.int32)


def _attention(g0, qpT, kp, vT, *, batch, seq, tq, tk):
    nk = seq // tk
    grid_spec = pltpu.PrefetchScalarGridSpec(
        num_scalar_prefetch=1,
        grid=(batch, N_HEADS, seq // tq),
        in_specs=[
            pl.BlockSpec((1, 1, KAUG, tq), lambda b, h, i, g0: (b, h, 0, i)),
            pl.BlockSpec((1, 1, nk, tk, KAUG), lambda b, h, i, g0: (b, h, 0, 0, 0)),
            pl.BlockSpec((1, 1, nk, V_ROWS, tk), lambda b, h, i, g0: (b, h, 0, 0, 0)),
        ],
        out_specs=pl.BlockSpec((1, HEAD_DIM, tq), lambda b, h, i, g0: (b, h, i)),
        scratch_shapes=[pltpu.VMEM((tq // tk, tk, tq), F32), pltpu.VMEM((tq // tk, tk, tq), F32),
                        pltpu.VMEM((tq // tk, 1, tq), F32), pltpu.VMEM((tq // tk, 1, tq), F32)],
    )
    return pl.pallas_call(
        functools.partial(_attn_kernel, tq=tq, tk=tk),
        grid_spec=grid_spec,
        out_shape=jax.ShapeDtypeStruct((batch, D_ATT, seq), F32),
        compiler_params=pltpu.CompilerParams(
            dimension_semantics=("arbitrary", "arbitrary", "arbitrary"), vmem_limit_bytes=VMEM_LIMIT),
        name="attention",
    )(g0, qpT, kp, vT)


def _outproj_kernel(x_ref, mc_ref, yT_ref, woc_ref, woa_ref, ga_ref, g2_ref, wrT_ref, brT_ref, utri_ref,
                    x1_ref, h2_ref, route_ref, cnt_ref, carry_ref, *, tm):
    first = (pl.program_id(0) == 0) & (pl.program_id(1) == 0)

    @pl.when(first)
    def _():
        carry_ref[...] = jnp.zeros_like(carry_ref)

    yT = yT_ref[0]
    msa = jnp.mean(yT * yT, axis=0, keepdims=True)
    yn = (yT * lax.rsqrt(msa + EPS)).T * ga_ref[...]
    x1 = x_ref[...] + _dot(mc_ref[...], woc_ref[...]) + _dot(yn.astype(BF16), woa_ref[...])
    x1_ref[...] = x1
    ms = jnp.mean(x1 * x1, axis=-1, keepdims=True)
    h2 = (x1 * lax.rsqrt(ms + EPS)) * g2_ref[...]
    h2_ref[:, :D_MODEL] = h2

    lg = _dot_nt(wrT_ref[...], h2.astype(BF16)) + brT_ref[...]
    neg = -jnp.inf
    sub = lax.broadcasted_iota(jnp.int32, (SUBLANES, tm), 0)
    big = jnp.int32(SUBLANES)
    first_of = lambda hit: jnp.min(jnp.where(hit, sub, big), axis=0, keepdims=True)

    lg1 = lg[0:SUBLANES]
    m1 = jnp.max(lg1, axis=0, keepdims=True)
    p1_sel = 1.0 / jnp.sum(jnp.exp(lg1 - m1), axis=0, keepdims=True)
    grp = first_of(lg1 == m1)

    slab = lambda g: lg[SUBLANES * (g + 1):SUBLANES * (g + 2)]
    v = slab(N_GROUPS - 1)
    for g in range(N_GROUPS - 2, -1, -1):
        v = jnp.where(grp == g, slab(g), v)
    v1 = jnp.max(v, axis=0, keepdims=True)
    j1 = first_of(v == v1)
    vv = jnp.where(sub == j1, neg, v)
    v2 = jnp.max(vv, axis=0, keepdims=True)
    j2 = first_of(vv == v2)
    e21 = jnp.exp(v2 - v1)
    w0 = p1_sel / (1.0 + e21)
    w1 = p1_sel * e21 / (1.0 + e21)

    swap = j2 < j1
    al = jnp.where(swap, j2, j1)
    bl = jnp.where(swap, j1, j2)
    wa = jnp.where(swap, w1, w0)
    wb = jnp.where(swap, w0, w1)
    cid = PAIRS_PER_GROUP * grp + ((al * (2 * EXPERTS_PER_GROUP - 1 - al)) >> 1) + (bl - al - 1)

    cls = lax.broadcasted_iota(jnp.int32, (LANES, tm), 0)
    oh = cls == cid
    cmat = jnp.where(oh, 1.0, 0.0)
    prefix = _dot(cmat.astype(BF16), utri_ref[...]) + carry_ref[...]
    rank = jnp.sum(jnp.where(oh, prefix, 0.0), axis=0, keepdims=True)
    counts = prefix[:, tm - 1:tm] + cmat[:, tm - 1:tm]
    carry_ref[...] = counts
    cnt_ref[...] = counts

    h2_ref[:, D_MODEL:] = jnp.where(cls == 0, wa, jnp.where(cls == 1, wb, 0.0)).T
    route_ref[...] = jnp.where(sub == 0, cid.astype(F32), jnp.where(sub == 1, rank, 0.0))


def _outproj(x2, mc, yT, woc, woa, ga, g2, wrT, brT, *, batch, seq, tm):
    nt = seq // tm
    utri = jnp.triu(jnp.ones((tm, tm), F32), 1).astype(BF16)
    const = lambda shape: pl.BlockSpec(shape, lambda b, t: (0,) * len(shape))
    row_spec = lambda w: pl.BlockSpec((tm, w), lambda b, t: (b * nt + t, 0))
    return pl.pallas_call(
        functools.partial(_outproj_kernel, tm=tm),
        grid=(batch, nt),
        in_specs=[row_spec(D_MODEL), row_spec(D_CONV),
                  pl.BlockSpec((1, D_ATT, tm), lambda b, t: (b, 0, t)),
                  const((D_CONV, D_MODEL)), const((D_ATT, D_MODEL)), const((1, D_ATT)), const((1, D_MODEL)),
                  const((LANES, D_MODEL)), const((LANES, 1)), const((tm, tm))],
        out_specs=[row_spec(D_MODEL), row_spec(ROW_W),
                   pl.BlockSpec((SUBLANES, tm), lambda b, t: (0, b * nt + t)), const((LANES, 1))],
        out_shape=[jax.ShapeDtypeStruct((batch * seq, D_MODEL), F32),
                   jax.ShapeDtypeStruct((batch * seq, ROW_W), F32),
                   jax.ShapeDtypeStruct((SUBLANES, batch * seq), F32),
                   jax.ShapeDtypeStruct((LANES, 1), F32)],
        scratch_shapes=[pltpu.VMEM((LANES, 1), F32)],
        compiler_params=pltpu.CompilerParams(
            dimension_semantics=("arbitrary", "arbitrary"), vmem_limit_bytes=VMEM_LIMIT),
        name="outproj",
    )(x2, mc, yT, woc, woa, ga, g2, wrT, brT, utri)


def _row_copies(src_at, dst_at, sem):
    return pltpu.make_async_copy(src_at, dst_at, sem)


def _dispatch_kernel(dest_ref, h2_ref, xs_ref, sem, *, tm):
    def body(g, carry):
        for u in range(SUBLANES):
            _row_copies(h2_ref.at[g, pl.ds(u, 1)], xs_ref.at[pl.ds(dest_ref[g * SUBLANES + u], 1)],
                        sem).start(priority=u % 2)
        return carry

    lax.fori_loop(0, tm // SUBLANES, body, 0)
    done = xs_ref.at[pl.ds(0, tm)]
    _row_copies(done, done, sem).wait()


def _dispatch(dest, h2, *, tm):
    t_len = h2.shape[0]
    return pl.pallas_call(
        functools.partial(_dispatch_kernel, tm=tm),
        grid=(t_len // tm,),
        in_specs=[pl.BlockSpec((tm,), lambda i: (i,), memory_space=pltpu.SMEM),
                  pl.BlockSpec((tm // SUBLANES, SUBLANES, ROW_W), lambda i: (i, 0, 0))],
        out_specs=pl.BlockSpec(memory_space=pl.ANY),
        out_shape=jax.ShapeDtypeStruct((t_len, ROW_W), F32),
        scratch_shapes=[pltpu.SemaphoreType.DMA(())],
        compiler_params=pltpu.CompilerParams(dimension_semantics=("arbitrary",), vmem_limit_bytes=VMEM_LIMIT),
        name="dispatch",
    )(dest, h2.reshape(t_len // SUBLANES, SUBLANES, ROW_W))


def _expert_kernel(blk_ref, ea_ref, eb_ref, lo_ref, hi_ref, x_ref, wga_ref, wua_ref, wda_ref,
                   wgb_ref, wub_ref, wdb_ref, y_ref):
    del blk_ref, ea_ref, eb_ref
    i = pl.program_id(0)
    lo = lo_ref[i]
    hi = hi_ref[i]

    def mlp(xb, wg_ref, wu_ref, wd_ref):
        g = _dot(xb, wg_ref[0])
        u = _dot(xb, wu_ref[0])
        return _dot(((g * jax.nn.sigmoid(g)) * u).astype(BF16), wd_ref[0])

    @pl.when(hi > lo)
    def _():
        xb = x_ref[:, :D_MODEL].astype(BF16)
        wts = x_ref[:, D_MODEL:]
        y = wts[:, 0:1] * mlp(xb, wga_ref, wua_ref, wda_ref) + wts[:, 1:2] * mlp(xb, wgb_ref, wub_ref, wdb_ref)
        row = lax.broadcasted_iota(jnp.int32, (ROW_BLOCK, 1), 0)
        mine = (row >= lo) & (row < hi)

        @pl.when(lo == 0)
        def _():
            y_ref[...] = jnp.where(mine, y, 0.0)

        @pl.when(lo > 0)
        def _():
            y_ref[...] = jnp.where(mine, y, y_ref[...])


def _experts(items, xs, wg, wu, wd):
    n_rows = xs.shape[0]
    n_items = items[0].shape[0]
    wspec = lambda shape, which: pl.BlockSpec(
        shape, lambda i, blk, ea, eb, lo, hi: ((ea, eb)[which][i], 0, 0))
    grid_spec = pltpu.PrefetchScalarGridSpec(
        num_scalar_prefetch=5,
        grid=(n_items,),
        in_specs=[pl.BlockSpec((ROW_BLOCK, ROW_W), lambda i, blk, ea, eb, lo, hi: (blk[i], 0)),
                  wspec((1, D_MODEL, D_EXPERT), 0), wspec((1, D_MODEL, D_EXPERT), 0), wspec((1, D_EXPERT, D_MODEL), 0),
                  wspec((1, D_MODEL, D_EXPERT), 1), wspec((1, D_MODEL, D_EXPERT), 1), wspec((1, D_EXPERT, D_MODEL), 1)],
        out_specs=pl.BlockSpec((ROW_BLOCK, D_MODEL), lambda i, blk, ea, eb, lo, hi: (blk[i], 0)),
    )
    return pl.pallas_call(
        _expert_kernel,
        grid_spec=grid_spec,
        out_shape=jax.ShapeDtypeStruct((n_rows, D_MODEL), F32),
        compiler_params=pltpu.CompilerParams(dimension_semantics=("arbitrary",), vmem_limit_bytes=VMEM_LIMIT),
        name="experts",
    )(*items, xs, wg, wu, wd, wg, wu, wd)


def _combine_kernel(dcur_ref, dnxt_ref, x1_ref, g_ref, ys_ref, out_ref, buf_ref, sems, *, tm):
    i = pl.program_id(0)
    slot = i % 2

    def issue(d_ref, s):
        def body(g, carry):
            for u in range(SUBLANES):
                _row_copies(ys_ref.at[pl.ds(d_ref[g * SUBLANES + u], 1)], buf_ref.at[s, g, pl.ds(u, 1)],
                            sems.at[s]).start(priority=u % 2)
            return carry

        lax.fori_loop(0, tm // SUBLANES, body, 0)

    @pl.when(i == 0)
    def _():
        issue(dcur_ref, 0)

    @pl.when(i + 1 < pl.num_programs(0))
    def _():
        issue(dnxt_ref, 1 - slot)

    _row_copies(buf_ref.at[slot], buf_ref.at[slot], sems.at[slot]).wait()
    x2 = x1_ref[...] + buf_ref[slot]
    ms = jnp.mean(x2 * x2, axis=-1, keepdims=True)
    out_ref[...] = (x2 * lax.rsqrt(ms + EPS)) * g_ref[...]


def _combine(dest, x1, gf, ys, *, tm):
    t_len = x1.shape[0]
    n = t_len // tm
    return pl.pallas_call(
        functools.partial(_combine_kernel, tm=tm),
        grid=(n,),
        in_specs=[pl.BlockSpec((tm,), lambda i: (i,), memory_space=pltpu.SMEM),
                  pl.BlockSpec((tm,), lambda i: (jnp.minimum(i + 1, n - 1),), memory_space=pltpu.SMEM),
                  pl.BlockSpec((tm // SUBLANES, SUBLANES, D_MODEL), lambda i: (i, 0, 0)),
                  pl.BlockSpec((1, D_MODEL), lambda i: (0, 0)),
                  pl.BlockSpec(memory_space=pl.ANY)],
        out_specs=pl.BlockSpec((tm // SUBLANES, SUBLANES, D_MODEL), lambda i: (i, 0, 0)),
        out_shape=jax.ShapeDtypeStruct((t_len // SUBLANES, SUBLANES, D_MODEL), F32),
        scratch_shapes=[pltpu.VMEM((2, tm // SUBLANES, SUBLANES, D_MODEL), F32), pltpu.SemaphoreType.DMA((2,))],
        compiler_params=pltpu.CompilerParams(dimension_semantics=("arbitrary",), vmem_limit_bytes=VMEM_LIMIT),
        name="combine",
    )(dest, dest, x1.reshape(t_len // SUBLANES, SUBLANES, D_MODEL), gf, ys)


def _layer(x, norm1_g, w_in, b_f, w_dw, b_dw, conv_ln_g, conv_ln_b, out_g_conv, out_g_att, w_out,
           norm2_g, w_r1, b_r1, w_r2, b_r2, w_gate, w_up, w_down):
    batch, seq, d = x.shape
    t_len = batch * seq
    tm = min(512, seq)
    tk = min(256, seq)
    tq = min(1024, seq)
    o1, o2 = D_CONV, 2 * D_CONV
    o3, o4, o5 = o2 + D_ATT, o2 + 2 * D_ATT, o2 + 3 * D_ATT

    wvg = w_in[:, :o2].astype(BF16)
    wk = w_in[:, o3:o4].astype(BF16)
    wqT = (w_in[:, o2:o3] * (HEAD_DIM ** -0.5)).T.astype(BF16)
    wvT = w_in[:, o4:o5].T.astype(BF16)
    assert PIECE_STRIDE == N_HEADS
    pad = LANES - N_PIECES * PIECE_STRIDE
    wf = jnp.pad(jnp.tile(w_in[:, o5:], (1, N_PIECES)), ((0, 0), (0, pad))).astype(BF16)
    bf3 = jnp.pad(jnp.tile(b_f.astype(F32).reshape(1, N_HEADS), (1, N_PIECES)), ((0, 0), (0, pad)))

    x2 = x.reshape(t_len, d)
    a, kp, qpT, vT, kstat, qstat = _inproj(x2, norm1_g.reshape(1, d), wvg, wk, wf, bf3, wqT, wvT,
                                           batch=batch, seq=seq, tm=tm, tk=tk)
    mc = _conv(a, w_dw, b_dw.reshape(1, -1), conv_ln_g.reshape(1, -1), conv_ln_b.reshape(1, -1),
               out_g_conv.reshape(1, -1), batch=batch, seq=seq, tc=min(256, seq), rc=64)
    yT = _attention(_first_groups(kstat, qstat, tm=tm, tq=tq), qpT, kp, vT, batch=batch, seq=seq, tq=tq, tk=tk)

    gpad = SUBLANES - N_GROUPS
    rpad = LANES - SUBLANES - N_EXPERTS
    wrT = jnp.concatenate([w_r1.T, jnp.zeros((gpad, d), F32),
                           jnp.transpose(w_r2, (0, 2, 1)).reshape(N_EXPERTS, d), jnp.zeros((rpad, d), F32)], axis=0)
    brT = jnp.concatenate([b_r1.astype(F32), jnp.full((gpad,), -jnp.inf, F32), b_r2.reshape(-1).astype(F32),
                           jnp.zeros((rpad,), F32)]).reshape(LANES, 1)
    x1, h2, route, cnt = _outproj(x2, mc, yT, w_out[:D_CONV].astype(BF16), w_out[D_CONV:].astype(BF16),
                                  out_g_att.reshape(1, -1), norm2_g.reshape(1, d), wrT.astype(BF16), brT,
                                  batch=batch, seq=seq, tm=tm)

    i32 = jnp.int32
    lanes = jnp.arange(LANES, dtype=i32)
    pick = lambda table, idx: jnp.sum(jnp.where(idx[:, None] == lanes, table[None, :], 0), axis=1).astype(i32)
    counts = cnt[:, 0].astype(i32)
    ends = jnp.cumsum(counts).astype(i32)
    starts = ends - counts
    dest = pick(starts, route[0].astype(i32)) + route[1].astype(i32)

    n_blocks = t_len // ROW_BLOCK
    b_lo = starts // ROW_BLOCK
    n_it = jnp.where(counts > 0, (ends - 1) // ROW_BLOCK - b_lo + 1, 0)
    it_end = jnp.cumsum(n_it).astype(i32)
    it_start = it_end - n_it
    idx = jnp.arange(n_blocks + N_CLASSES, dtype=i32)
    valid = idx < it_end[-1]
    last_cls = jnp.max(jnp.where(counts > 0, lanes, 0))
    cls = jnp.where(valid, jnp.sum(it_end[None, :] <= idx[:, None], axis=1).astype(i32), last_cls)
    blk = jnp.where(valid, pick(b_lo, cls) + idx - pick(it_start, cls), n_blocks - 1)
    row0 = blk * ROW_BLOCK
    lo = jnp.where(valid, jnp.maximum(pick(starts, cls), row0) - row0, 0)
    hi = jnp.where(valid, jnp.minimum(pick(ends, cls), row0 + ROW_BLOCK) - row0, 0)
    pair_a, pair_b = [], []
    for g in range(N_GROUPS):
        for a_loc in range(EXPERTS_PER_GROUP):
            for b_loc in range(a_loc + 1, EXPERTS_PER_GROUP):
                pair_a.append(g * EXPERTS_PER_GROUP + a_loc)
                pair_b.append(g * EXPERTS_PER_GROUP + b_loc)
    cpad = [0] * (LANES - N_CLASSES)
    items = (blk.astype(i32), pick(jnp.array(pair_a + cpad, i32), cls), pick(jnp.array(pair_b + cpad, i32), cls),
             lo.astype(i32), hi.astype(i32))

    xs = _dispatch(dest, h2, tm=min(1024, t_len))
    ys = _experts(items, xs, w_gate.astype(BF16), w_up.astype(BF16), w_down.astype(BF16))
    return dest, x1, ys


def kernel(x, norm1_g, w_in, b_f, w_dw, b_dw, conv_ln_g, conv_ln_b, out_g_conv, out_g_att, w_out, norm2_g,
           w_r1, b_r1, w_r2, b_r2, w_gate, w_up, w_down, final_g):
    assert norm1_g.shape[0] == 1, "single-layer stack"
    batch, seq, d = x.shape
    dest, x1, ys = _layer(
        x, norm1_g[0], w_in[0], b_f[0], w_dw[0], b_dw[0], conv_ln_g[0], conv_ln_b[0], out_g_conv[0],
        out_g_att[0], w_out[0], norm2_g[0], w_r1[0], b_r1[0], w_r2[0], b_r2[0], w_gate[0], w_up[0], w_down[0])
    out = _combine(dest, x1, final_g.reshape(1, d), ys, tm=min(512, batch * seq))
    return out.reshape(batch, seq, d)
```

```python
import functools

import jax
import jax.numpy as jnp
from jax import lax
from jax.experimental import pallas as pl
from jax.experimental.pallas import tpu as pltpu

D_MODEL = 1024
D_CONV = 512
N_HEADS = 8
HEAD_DIM = 64
D_ATT = N_HEADS * HEAD_DIM
CONV_WIDTH = 31
N_GROUPS = 4
EXPERTS_PER_GROUP = 8
N_EXPERTS = N_GROUPS * EXPERTS_PER_GROUP
TOP_K = 2
D_EXPERT = D_MODEL // 4
ROW_BLOCK = 256
EPS = 1e-6

LANES = 128
SUBLANES = 8
KAUG = 128
N_PIECES = 3
PIECE_STRIDE = 8
CONV_HALO = 32
PAIRS_PER_GROUP = EXPERTS_PER_GROUP * (EXPERTS_PER_GROUP - 1) // 2
N_CLASSES = N_GROUPS * PAIRS_PER_GROUP
ROW_W = D_MODEL + LANES
SKIP_LOG2 = 160.0
NORM_SLACK = 1.02
V_ROWS = 80
LOG2E = 1.4426950408889634
VMEM_LIMIT = 56 * 1024 * 1024

F32 = jnp.float32
BF16 = jnp.bfloat16


def _dot(a, b):
    return jnp.dot(a, b, preferred_element_type=F32)


def _dot_nt(a, b):
    return lax.dot_general(a, b, (((1,), (1,)), ((), ())), preferred_element_type=F32)


def _split3(x):
    hi = x.astype(BF16)
    r1 = x - hi.astype(F32)
    mid = r1.astype(BF16)
    lo = (r1 - mid.astype(F32)).astype(BF16)
    return hi.astype(F32), mid.astype(F32), lo.astype(F32)


def _piece_lane_mask(lane, h):
    return (lane == h) | (lane == h + PIECE_STRIDE) | (lane == h + 2 * PIECE_STRIDE)


def _inproj_kernel(x_ref, g1_ref, wvg_ref, wk_ref, wf_ref, bf_ref, wqT_ref, wvT_ref, ltri_ref, hsel_ref,
                   a_ref, kp_ref, qpT_ref, vT_ref, kstat_ref, qstat_ref, carry_ref, *, tm, tk):
    @pl.when(pl.program_id(1) == 0)
    def _():
        carry_ref[...] = jnp.zeros_like(carry_ref)

    x = x_ref[...]
    ms = jnp.mean(x * x, axis=-1, keepdims=True)
    hb = ((x * lax.rsqrt(ms + EPS)) * g1_ref[...]).astype(BF16)

    zvg = _dot(hb, wvg_ref[...])
    a_ref[...] = zvg[:, :D_CONV] * jax.nn.sigmoid(zvg[:, D_CONV:])

    kk = _dot(hb, wk_ref[...])

    zf = _dot(hb, wf_ref[...]) + bf_ref[...]
    lf = jnp.minimum(zf, 0.0) - jnp.log1p(jnp.exp(-jnp.abs(zf)))
    lane = lax.broadcasted_iota(jnp.int32, (tm, LANES), 1)
    hi, mid, lo = _split3(lf)
    lf3 = jnp.where(lane < PIECE_STRIDE, hi,
                    jnp.where(lane < 2 * PIECE_STRIDE, mid,
                              jnp.where(lane < 3 * PIECE_STRIDE, lo, 0.0))).astype(BF16)
    cs3 = _dot(ltri_ref[...], lf3)
    c = (cs3 + pltpu.roll(cs3, LANES - PIECE_STRIDE, 1)
         + pltpu.roll(cs3, LANES - 2 * PIECE_STRIDE, 1)) + carry_ref[...]
    carry_ref[...] = c[tm - 1:tm, :]

    nhi, nmid, nlo = _split3(c * (-LOG2E))
    p3 = jnp.where(lane < PIECE_STRIDE, nhi,
                   jnp.where(lane < 2 * PIECE_STRIDE, pltpu.roll(nmid, PIECE_STRIDE, 1),
                             jnp.where(lane < 3 * PIECE_STRIDE, pltpu.roll(nlo, 2 * PIECE_STRIDE, 1), 0.0)))
    p3_hi = pltpu.roll(p3, HEAD_DIM, 1)

    qT = _dot_nt(wqT_ref[...], hb) * LOG2E
    vT = _dot_nt(wvT_ref[...], hb)
    row = lax.broadcasted_iota(jnp.int32, (HEAD_DIM, tm), 0)
    vrow = lax.broadcasted_iota(jnp.int32, (V_ROWS - HEAD_DIM, tk), 0)
    v_tail = jnp.where(vrow == 0, 1.0, 0.0).astype(BF16)

    for h in range(N_HEADS):
        kcol = kk[:, (h // 2) * LANES:(h // 2 + 1) * LANES]
        ones_h = jnp.where(_piece_lane_mask(row, h), 1.0, 0.0)
        q_h = qT[h * HEAD_DIM:(h + 1) * HEAD_DIM, :]
        if h % 2 == 0:
            ext = jnp.where(_piece_lane_mask(lane, h + HEAD_DIM), p3_hi, 0.0)
            kp = jnp.where(lane < HEAD_DIM, kcol, ext)
            qp = jnp.concatenate([q_h, ones_h], axis=0)
        else:
            ext = jnp.where(_piece_lane_mask(lane, h), p3, 0.0)
            kp = jnp.where(lane >= HEAD_DIM, kcol, ext)
            qp = jnp.concatenate([ones_h, q_h], axis=0)
        qpT_ref[0, h] = qp.astype(BF16)
        for cidx in range(tm // tk):
            kp_ref[0, h, cidx] = kp[cidx * tk:(cidx + 1) * tk, :].astype(BF16)
            v_h = vT[h * HEAD_DIM:(h + 1) * HEAD_DIM, cidx * tk:(cidx + 1) * tk].astype(BF16)
            vT_ref[0, h, cidx] = jnp.concatenate([v_h, v_tail], axis=0)

    kn2 = jnp.max(_dot((kk * kk).astype(BF16), hsel_ref[...]), axis=0, keepdims=True)
    srow = lax.broadcasted_iota(jnp.int32, (SUBLANES, LANES), 0)
    kstat_ref[0, 0] = jnp.where(srow == 0, c[0:1, :], jnp.where(srow == 1, c[tm - 1:tm, :],
                                                                 jnp.where(srow == 2, kn2, 0.0)))
    q2 = qT * qT
    qn2 = jnp.concatenate([jnp.sum(q2[h * HEAD_DIM:(h + 1) * HEAD_DIM, :], axis=0, keepdims=True)
                           for h in range(N_HEADS)], axis=0)
    qstat_ref[0, 0] = jnp.broadcast_to(jnp.max(qn2, axis=1, keepdims=True), (N_HEADS, LANES))


def _inproj(x2, g1, wvg, wk, wf, bf3, wqT, wvT, *, batch, seq, tm, tk):
    nt = seq // tm
    nk = seq // tk
    ltri = jnp.tril(jnp.ones((tm, tm), F32)).astype(BF16)
    hsel = (jnp.arange(D_ATT)[:, None] // HEAD_DIM == jnp.arange(LANES)[None, :]).astype(BF16)
    const = lambda shape: pl.BlockSpec(shape, lambda b, t: (0,) * len(shape))
    stat_spec = pl.BlockSpec((1, 1, SUBLANES, LANES), lambda b, t: (b, t, 0, 0))
    stat_shape = jax.ShapeDtypeStruct((batch, nt, SUBLANES, LANES), F32)
    return pl.pallas_call(
        functools.partial(_inproj_kernel, tm=tm, tk=tk),
        grid=(batch, nt),
        in_specs=[
            pl.BlockSpec((tm, D_MODEL), lambda b, t: (b * nt + t, 0)),
            const((1, D_MODEL)), const((D_MODEL, 2 * D_CONV)), const((D_MODEL, D_ATT)),
            const((D_MODEL, LANES)), const((1, LANES)), const((D_ATT, D_MODEL)), const((D_ATT, D_MODEL)),
            const((tm, tm)), const((D_ATT, LANES)),
        ],
        out_specs=[
            pl.BlockSpec((tm, D_CONV), lambda b, t: (b * nt + t, 0)),
            pl.BlockSpec((1, N_HEADS, tm // tk, tk, KAUG), lambda b, t: (b, 0, t, 0, 0)),
            pl.BlockSpec((1, N_HEADS, KAUG, tm), lambda b, t: (b, 0, 0, t)),
            pl.BlockSpec((1, N_HEADS, tm // tk, V_ROWS, tk), lambda b, t: (b, 0, t, 0, 0)),
            stat_spec, stat_spec,
        ],
        out_shape=[
            jax.ShapeDtypeStruct((batch * seq, D_CONV), F32),
            jax.ShapeDtypeStruct((batch, N_HEADS, nk, tk, KAUG), BF16),
            jax.ShapeDtypeStruct((batch, N_HEADS, KAUG, seq), BF16),
            jax.ShapeDtypeStruct((batch, N_HEADS, nk, V_ROWS, tk), BF16),
            stat_shape, stat_shape,
        ],
        scratch_shapes=[pltpu.VMEM((1, LANES), F32)],
        compiler_params=pltpu.CompilerParams(
            dimension_semantics=("arbitrary", "arbitrary"), vmem_limit_bytes=VMEM_LIMIT),
        name="inproj",
    )(x2, g1, wvg, wk, wf, bf3, wqT, wvT, ltri, hsel)


def _conv_tile(a_ref, w_ref, b_ref, lng_ref, lnb_ref, og_ref, sh_ref, acc_ref, *, tc, rc):
    n_cb = D_CONV // LANES

    @pl.when(pl.program_id(1) == 0)
    def _():
        for cb in range(n_cb):
            sh_ref[0, cb, 0:CONV_HALO, :] = jnp.zeros((CONV_HALO, LANES), F32)

    @pl.when(pl.program_id(1) > 0)
    def _():
        for cb in range(n_cb):
            sh_ref[0, cb, 0:CONV_HALO, :] = sh_ref[0, cb, tc:tc + CONV_HALO, :]

    n_sh = tc + CONV_HALO - SUBLANES
    for cb in range(n_cb):
        sh_ref[0, cb, CONV_HALO:CONV_HALO + tc, :] = a_ref[:, cb * LANES:(cb + 1) * LANES]
        for f in range(1, SUBLANES):
            sh_ref[f, cb, 0:n_sh, :] = sh_ref[0, cb, f:f + n_sh, :]

    base = CONV_HALO - (CONV_WIDTH - 1)
    for cb in range(n_cb):
        cols = slice(cb * LANES, (cb + 1) * LANES)

        def chunk(c, carry, cb=cb, cols=cols):
            r0 = pl.multiple_of(c * rc, rc)
            acc = jnp.zeros((rc, LANES), F32)
            for j in range(CONV_WIDTH):
                f = (base + j) % SUBLANES
                acc = acc + w_ref[j:j + 1, cols] * sh_ref[f, cb, pl.ds(r0 + (base + j - f), rc), :]
            acc_ref[pl.ds(r0, rc), cols] = acc
            return carry

        lax.fori_loop(0, tc // rc, chunk, 0)

    y = acc_ref[...] + b_ref[...]
    mu = jnp.mean(y, axis=-1, keepdims=True)
    yc = y - mu
    var = jnp.mean(yc * yc, axis=-1, keepdims=True)
    yn = yc * lax.rsqrt(var + EPS) * lng_ref[...] + lnb_ref[...]
    s = yn * jax.nn.sigmoid(yn)
    ms = jnp.mean(s * s, axis=-1, keepdims=True)
    return (s * lax.rsqrt(ms + EPS) * og_ref[...]).astype(BF16)


def _attn_kernel(g0_ref, qT_ref, k_ref, v_ref, o_ref, sa_ref, sb_ref, cma_ref, cmb_ref, *, tq, tk):
    i = pl.program_id(2)
    qT = qT_ref[0, 0]
    n_sub = tq // tk

    def scores(g, s_ref, cm_ref):
        for d in range(n_sub):
            s = _dot(k_ref[0, 0, g * n_sub + d], qT)
            s_ref[d] = s
            cm_ref[d] = jnp.max(s, axis=0, keepdims=True)

    def consume(g, s_ref, cm_ref, carry, nxt=None):
        m, acc = carry
        m_new = m
        for d in range(n_sub):
            m_new = jnp.maximum(m_new, cm_ref[d])
        ps = []
        for d in range(n_sub):
            if nxt is not None:
                g_n, sn_ref, cmn_ref = nxt
                s = _dot(k_ref[0, 0, g_n * n_sub + d], qT)
                sn_ref[d] = s
                cmn_ref[d] = jnp.max(s, axis=0, keepdims=True)
            ps.append(jnp.exp2(s_ref[d] - m_new).astype(BF16))
        p = jnp.concatenate(ps, axis=0)
        v = jnp.concatenate([v_ref[0, 0, g * n_sub + d] for d in range(n_sub)], axis=1)
        return m_new, jnp.exp2(m - m_new) * acc + _dot(v, p)

    key = lax.broadcasted_iota(jnp.int32, (tk, tk), 0)
    qry = lax.broadcasted_iota(jnp.int32, (tk, tk), 1)

    def consume_diag(s_ref, cm_ref, carry):
        m, acc = carry
        lanes = lambda j: slice(j * tk, (j + 1) * tk)
        causal = [jnp.where(key <= qry, s_ref[d, :, lanes(d)], -jnp.inf) for d in range(n_sub)]
        m_tiles = []
        for j in range(n_sub):
            m_j = jnp.maximum(m[:, lanes(j)], jnp.max(causal[j], axis=0, keepdims=True))
            for d in range(j):
                m_j = jnp.maximum(m_j, cm_ref[d, :, lanes(j)])
            m_tiles.append(m_j)
        m_new = jnp.concatenate(m_tiles, axis=1)
        rows = []
        for d in range(n_sub):
            tiles = [jnp.zeros((tk, d * tk), BF16)] if d else []
            tiles.append(jnp.exp2(causal[d] - m_tiles[d]).astype(BF16))
            if d + 1 < n_sub:
                rest = slice((d + 1) * tk, tq)
                tiles.append(jnp.exp2(s_ref[d, :, rest] - m_new[:, rest]).astype(BF16))
            rows.append(jnp.concatenate(tiles, axis=1))
        p = jnp.concatenate(rows, axis=0)
        v = jnp.concatenate([v_ref[0, 0, i * n_sub + d] for d in range(n_sub)], axis=1)
        acc = jnp.exp2(m - m_new) * acc + _dot(v, p)
        o_ref[0] = acc[:HEAD_DIM] / acc[HEAD_DIM:HEAD_DIM + 1]

    g0 = g0_ref[(pl.program_id(0) * N_HEADS + pl.program_id(1)) * pl.num_programs(2) + i]
    n_full = i - g0

    def pair(p, carry):
        g = g0 + 2 * p
        carry = consume(g, sa_ref, cma_ref, carry, nxt=(g + 1, sb_ref, cmb_ref))
        return consume(g + 1, sb_ref, cmb_ref, carry, nxt=(g + 2, sa_ref, cma_ref))

    scores(g0, sa_ref, cma_ref)
    init = (jnp.full((1, tq), -jnp.inf, F32), jnp.zeros((V_ROWS, tq), F32))
    carry = lax.fori_loop(0, n_full // 2, pair, init)

    @pl.when(n_full % 2 == 1)
    def _():
        consume_diag(sb_ref, cmb_ref, consume(i - 1, sa_ref, cma_ref, carry, nxt=(i, sb_ref, cmb_ref)))

    @pl.when(n_full % 2 == 0)
    def _():
        consume_diag(sa_ref, cma_ref, carry)


def _first_groups(kstat, qstat, *, tm, tq):
    per = tq // tm
    bound = jnp.sqrt(jnp.max(kstat[:, :, 2, :N_HEADS], axis=1) * jnp.max(qstat[:, :, :, 0], axis=1))
    bound = bound * NORM_SLACK
    cq = kstat[:, ::per, 0, :N_HEADS]
    ck = kstat[:, per - 1::per, 1, :N_HEADS]
    d = (cq[:, :, None, :] - ck[:, None, :, :]) * LOG2E
    nq = cq.shape[1]
    earlier = jnp.arange(nq)[None, :, None, None] > jnp.arange(nq)[None, None, :, None]
    skip = earlier & (2.0 * bound[:, None, None, :] + d < -SKIP_LOG2)
    g0 = jnp.sum(jnp.cumprod(skip.astype(jnp.int32), axis=2), axis=2)
    return jnp.transpose(g0, (0, 2, 1)).reshape(-1).astype(jnp.int32)


def _attention(g0, qpT, kp, vT, *, batch, seq, tq, tk):
    nk = seq // tk
    grid_spec = pltpu.PrefetchScalarGridSpec(
        num_scalar_prefetch=1,
        grid=(batch, N_HEADS, seq // tq),
        in_specs=[
            pl.BlockSpec((1, 1, KAUG, tq), lambda b, h, i, g0: (b, h, 0, i)),
            pl.BlockSpec((1, 1, nk, tk, KAUG), lambda b, h, i, g0: (b, h, 0, 0, 0)),
            pl.BlockSpec((1, 1, nk, V_ROWS, tk), lambda b, h, i, g0: (b, h, 0, 0, 0)),
        ],
        out_specs=pl.BlockSpec((1, HEAD_DIM, tq), lambda b, h, i, g0: (b, h, i)),
        scratch_shapes=[pltpu.VMEM((tq // tk, tk, tq), F32), pltpu.VMEM((tq // tk, tk, tq), F32),
                        pltpu.VMEM((tq // tk, 1, tq), F32), pltpu.VMEM((tq // tk, 1, tq), F32)],
    )
    return pl.pallas_call(
        functools.partial(_attn_kernel, tq=tq, tk=tk),
        grid_spec=grid_spec,
        out_shape=jax.ShapeDtypeStruct((batch, D_ATT, seq), F32),
        compiler_params=pltpu.CompilerParams(
            dimension_semantics=("arbitrary", "arbitrary", "arbitrary"), vmem_limit_bytes=VMEM_LIMIT),
        name="attention",
    )(g0, qpT, kp, vT)


def _outproj_kernel(x_ref, a_ref, yT_ref, wdw_ref, bdw_ref, lng_ref, lnb_ref, gc_ref, woc_ref, woa_ref, ga_ref,
                    g2_ref, wrT_ref, brT_ref, utri_ref, x1_ref, h2_ref, route_ref, cnt_ref,
                    carry_ref, sh_ref, acc_ref, *, tm, rc):
    first = (pl.program_id(0) == 0) & (pl.program_id(1) == 0)

    @pl.when(first)
    def _():
        carry_ref[...] = jnp.zeros_like(carry_ref)

    mc = _conv_tile(a_ref, wdw_ref, bdw_ref, lng_ref, lnb_ref, gc_ref, sh_ref, acc_ref, tc=tm, rc=rc)
    yT = yT_ref[0]
    msa = jnp.mean(yT * yT, axis=0, keepdims=True)
    yn = (yT * lax.rsqrt(msa + EPS)).T * ga_ref[...]
    x1 = x_ref[...] + _dot(mc, woc_ref[...]) + _dot(yn.astype(BF16), woa_ref[...])
    x1_ref[...] = x1
    ms = jnp.mean(x1 * x1, axis=-1, keepdims=True)
    h2 = (x1 * lax.rsqrt(ms + EPS)) * g2_ref[...]
    h2_ref[:, :D_MODEL] = h2

    lg = _dot_nt(wrT_ref[...], h2.astype(BF16)) + brT_ref[...]
    neg = -jnp.inf
    sub = lax.broadcasted_iota(jnp.int32, (SUBLANES, tm), 0)
    big = jnp.int32(SUBLANES)
    first_of = lambda hit: jnp.min(jnp.where(hit, sub, big), axis=0, keepdims=True)

    lg1 = lg[0:SUBLANES]
    m1 = jnp.max(lg1, axis=0, keepdims=True)
    p1_sel = 1.0 / jnp.sum(jnp.exp(lg1 - m1), axis=0, keepdims=True)
    grp = first_of(lg1 == m1)

    slab = lambda g: lg[SUBLANES * (g + 1):SUBLANES * (g + 2)]
    v = slab(N_GROUPS - 1)
    for g in range(N_GROUPS - 2, -1, -1):
        v = jnp.where(grp == g, slab(g), v)
    v1 = jnp.max(v, axis=0, keepdims=True)
    j1 = first_of(v == v1)
    vv = jnp.where(sub == j1, neg, v)
    v2 = jnp.max(vv, axis=0, keepdims=True)
    j2 = first_of(vv == v2)
    e21 = jnp.exp(v2 - v1)
    w0 = p1_sel / (1.0 + e21)
    w1 = p1_sel * e21 / (1.0 + e21)

    swap = j2 < j1
    al = jnp.where(swap, j2, j1)
    bl = jnp.where(swap, j1, j2)
    wa = jnp.where(swap, w1, w0)
    wb = jnp.where(swap, w0, w1)
    cid = PAIRS_PER_GROUP * grp + ((al * (2 * EXPERTS_PER_GROUP - 1 - al)) >> 1) + (bl - al - 1)

    cls = lax.broadcasted_iota(jnp.int32, (LANES, tm), 0)
    oh = cls == cid
    cmat = jnp.where(oh, 1.0, 0.0)
    prefix = _dot(cmat.astype(BF16), utri_ref[...]) + carry_ref[...]
    rank = jnp.sum(jnp.where(oh, prefix, 0.0), axis=0, keepdims=True)
    counts = prefix[:, tm - 1:tm] + cmat[:, tm - 1:tm]
    carry_ref[...] = counts
    cnt_ref[...] = counts

    h2_ref[:, D_MODEL:] = jnp.where(cls == 0, wa, jnp.where(cls == 1, wb, 0.0)).T
    route_ref[...] = jnp.where(sub == 0, cid.astype(F32), jnp.where(sub == 1, rank, 0.0))


def _outproj(x2, a, yT, conv_params, woc, woa, ga, g2, wrT, brT, *, batch, seq, tm, rc):
    nt = seq // tm
    utri = jnp.triu(jnp.ones((tm, tm), F32), 1).astype(BF16)
    const = lambda shape: pl.BlockSpec(shape, lambda b, t: (0,) * len(shape))
    row_spec = lambda w: pl.BlockSpec((tm, w), lambda b, t: (b * nt + t, 0))
    return pl.pallas_call(
        functools.partial(_outproj_kernel, tm=tm, rc=rc),
        grid=(batch, nt),
        in_specs=[row_spec(D_MODEL), row_spec(D_CONV),
                  pl.BlockSpec((1, D_ATT, tm), lambda b, t: (b, 0, t)),
                  const((CONV_WIDTH, D_CONV)), const((1, D_CONV)), const((1, D_CONV)), const((1, D_CONV)),
                  const((1, D_CONV)),
                  const((D_CONV, D_MODEL)), const((D_ATT, D_MODEL)), const((1, D_ATT)), const((1, D_MODEL)),
                  const((LANES, D_MODEL)), const((LANES, 1)), const((tm, tm))],
        out_specs=[row_spec(D_MODEL), row_spec(ROW_W),
                   pl.BlockSpec((SUBLANES, tm), lambda b, t: (0, b * nt + t)), const((LANES, 1))],
        out_shape=[jax.ShapeDtypeStruct((batch * seq, D_MODEL), F32),
                   jax.ShapeDtypeStruct((batch * seq, ROW_W), F32),
                   jax.ShapeDtypeStruct((SUBLANES, batch * seq), F32),
                   jax.ShapeDtypeStruct((LANES, 1), F32)],
        scratch_shapes=[pltpu.VMEM((LANES, 1), F32),
                        pltpu.VMEM((SUBLANES, D_CONV // LANES, tm + CONV_HALO, LANES), F32),
                        pltpu.VMEM((tm, D_CONV), F32)],
        compiler_params=pltpu.CompilerParams(
            dimension_semantics=("arbitrary", "arbitrary"), vmem_limit_bytes=VMEM_LIMIT),
        name="outproj",
    )(x2, a, yT, *conv_params, woc, woa, ga, g2, wrT, brT, utri)


def _row_copies(src_at, dst_at, sem):
    return pltpu.make_async_copy(src_at, dst_at, sem)


def _dispatch_kernel(dest_ref, h2_ref, xs_ref, sem, *, tm):
    def body(g, carry):
        for u in range(SUBLANES):
            _row_copies(h2_ref.at[g, pl.ds(u, 1)], xs_ref.at[pl.ds(dest_ref[g * SUBLANES + u], 1)],
                        sem).start(priority=u % 2)
        return carry

    lax.fori_loop(0, tm // SUBLANES, body, 0)
    done = xs_ref.at[pl.ds(0, tm)]
    _row_copies(done, done, sem).wait()


def _dispatch(dest, h2, *, tm):
    t_len = h2.shape[0]
    return pl.pallas_call(
        functools.partial(_dispatch_kernel, tm=tm),
        grid=(t_len // tm,),
        in_specs=[pl.BlockSpec((tm,), lambda i: (i,), memory_space=pltpu.SMEM),
                  pl.BlockSpec((tm // SUBLANES, SUBLANES, ROW_W), lambda i: (i, 0, 0))],
        out_specs=pl.BlockSpec(memory_space=pl.ANY),
        out_shape=jax.ShapeDtypeStruct((t_len, ROW_W), F32),
        scratch_shapes=[pltpu.SemaphoreType.DMA(())],
        compiler_params=pltpu.CompilerParams(dimension_semantics=("arbitrary",), vmem_limit_bytes=VMEM_LIMIT),
        name="dispatch",
    )(dest, h2.reshape(t_len // SUBLANES, SUBLANES, ROW_W))


def _expert_kernel(blk_ref, ea_ref, eb_ref, lo_ref, hi_ref, x_ref, wga_ref, wua_ref, wda_ref,
                   wgb_ref, wub_ref, wdb_ref, y_ref):
    del blk_ref, ea_ref, eb_ref
    i = pl.program_id(0)
    lo = lo_ref[i]
    hi = hi_ref[i]

    def mlp(xb, wg_ref, wu_ref, wd_ref):
        g = _dot(xb, wg_ref[0])
        u = _dot(xb, wu_ref[0])
        return _dot(((g * jax.nn.sigmoid(g)) * u).astype(BF16), wd_ref[0])

    @pl.when(hi > lo)
    def _():
        xb = x_ref[:, :D_MODEL].astype(BF16)
        wts = x_ref[:, D_MODEL:]
        y = wts[:, 0:1] * mlp(xb, wga_ref, wua_ref, wda_ref) + wts[:, 1:2] * mlp(xb, wgb_ref, wub_ref, wdb_ref)
        row = lax.broadcasted_iota(jnp.int32, (ROW_BLOCK, 1), 0)
        mine = (row >= lo) & (row < hi)

        @pl.when(lo == 0)
        def _():
            y_ref[...] = jnp.where(mine, y, 0.0)

        @pl.when(lo > 0)
        def _():
            y_ref[...] = jnp.where(mine, y, y_ref[...])


def _experts(items, xs, wg, wu, wd):
    n_rows = xs.shape[0]
    n_items = items[0].shape[0]
    wspec = lambda shape, which: pl.BlockSpec(
        shape, lambda i, blk, ea, eb, lo, hi: ((ea, eb)[which][i], 0, 0))
    grid_spec = pltpu.PrefetchScalarGridSpec(
        num_scalar_prefetch=5,
        grid=(n_items,),
        in_specs=[pl.BlockSpec((ROW_BLOCK, ROW_W), lambda i, blk, ea, eb, lo, hi: (blk[i], 0)),
                  wspec((1, D_MODEL, D_EXPERT), 0), wspec((1, D_MODEL, D_EXPERT), 0), wspec((1, D_EXPERT, D_MODEL), 0),
                  wspec((1, D_MODEL, D_EXPERT), 1), wspec((1, D_MODEL, D_EXPERT), 1), wspec((1, D_EXPERT, D_MODEL), 1)],
        out_specs=pl.BlockSpec((ROW_BLOCK, D_MODEL), lambda i, blk, ea, eb, lo, hi: (blk[i], 0)),
    )
    return pl.pallas_call(
        _expert_kernel,
        grid_spec=grid_spec,
        out_shape=jax.ShapeDtypeStruct((n_rows, D_MODEL), F32),
        compiler_params=pltpu.CompilerParams(dimension_semantics=("arbitrary",), vmem_limit_bytes=VMEM_LIMIT),
        name="experts",
    )(*items, xs, wg, wu, wd, wg, wu, wd)


def _combine_kernel(dcur_ref, dnxt_ref, x1_ref, g_ref, ys_ref, out_ref, buf_ref, sems, *, tm):
    i = pl.program_id(0)
    slot = i % 2

    def issue(d_ref, s):
        def body(g, carry):
            for u in range(SUBLANES):
                _row_copies(ys_ref.at[pl.ds(d_ref[g * SUBLANES + u], 1)], buf_ref.at[s, g, pl.ds(u, 1)],
                            sems.at[s]).start(priority=u % 2)
            return carry

        lax.fori_loop(0, tm // SUBLANES, body, 0)

    @pl.when(i == 0)
    def _():
        issue(dcur_ref, 0)

    @pl.when(i + 1 < pl.num_programs(0))
    def _():
        issue(dnxt_ref, 1 - slot)

    _row_copies(buf_ref.at[slot], buf_ref.at[slot], sems.at[slot]).wait()
    x2 = x1_ref[...] + buf_ref[slot]
    ms = jnp.mean(x2 * x2, axis=-1, keepdims=True)
    out_ref[...] = (x2 * lax.rsqrt(ms + EPS)) * g_ref[...]


def _combine(dest, x1, gf, ys, *, tm):
    t_len = x1.shape[0]
    n = t_len // tm
    return pl.pallas_call(
        functools.partial(_combine_kernel, tm=tm),
        grid=(n,),
        in_specs=[pl.BlockSpec((tm,), lambda i: (i,), memory_space=pltpu.SMEM),
                  pl.BlockSpec((tm,), lambda i: (jnp.minimum(i + 1, n - 1),), memory_space=pltpu.SMEM),
                  pl.BlockSpec((tm // SUBLANES, SUBLANES, D_MODEL), lambda i: (i, 0, 0)),
                  pl.BlockSpec((1, D_MODEL), lambda i: (0, 0)),
                  pl.BlockSpec(memory_space=pl.ANY)],
        out_specs=pl.BlockSpec((tm // SUBLANES, SUBLANES, D_MODEL), lambda i: (i, 0, 0)),
        out_shape=jax.ShapeDtypeStruct((t_len // SUBLANES, SUBLANES, D_MODEL), F32),
        scratch_shapes=[pltpu.VMEM((2, tm // SUBLANES, SUBLANES, D_MODEL), F32), pltpu.SemaphoreType.DMA((2,))],
        compiler_params=pltpu.CompilerParams(dimension_semantics=("arbitrary",), vmem_limit_bytes=VMEM_LIMIT),
        name="combine",
    )(dest, dest, x1.reshape(t_len // SUBLANES, SUBLANES, D_MODEL), gf, ys)


def _layer(x, norm1_g, w_in, b_f, w_dw, b_dw, conv_ln_g, conv_ln_b, out_g_conv, out_g_att, w_out,
           norm2_g, w_r1, b_r1, w_r2, b_r2, w_gate, w_up, w_down):
    batch, seq, d = x.shape
    t_len = batch * seq
    tm = min(512, seq)
    tk = min(256, seq)
    tq = min(1024, seq)
    o1, o2 = D_CONV, 2 * D_CONV
    o3, o4, o5 = o2 + D_ATT, o2 + 2 * D_ATT, o2 + 3 * D_ATT

    wvg = w_in[:, :o2].astype(BF16)
    wk = w_in[:, o3:o4].astype(BF16)
    wqT = (w_in[:, o2:o3] * (HEAD_DIM ** -0.5)).T.astype(BF16)
    wvT = w_in[:, o4:o5].T.astype(BF16)
    assert PIECE_STRIDE == N_HEADS
    pad = LANES - N_PIECES * PIECE_STRIDE
    wf = jnp.pad(jnp.tile(w_in[:, o5:], (1, N_PIECES)), ((0, 0), (0, pad))).astype(BF16)
    bf3 = jnp.pad(jnp.tile(b_f.astype(F32).reshape(1, N_HEADS), (1, N_PIECES)), ((0, 0), (0, pad)))

    x2 = x.reshape(t_len, d)
    a, kp, qpT, vT, kstat, qstat = _inproj(x2, norm1_g.reshape(1, d), wvg, wk, wf, bf3, wqT, wvT,
                                           batch=batch, seq=seq, tm=tm, tk=tk)
    conv_params = (w_dw, b_dw.reshape(1, -1), conv_ln_g.reshape(1, -1), conv_ln_b.reshape(1, -1),
                   out_g_conv.reshape(1, -1))
    yT = _attention(_first_groups(kstat, qstat, tm=tm, tq=tq), qpT, kp, vT, batch=batch, seq=seq, tq=tq, tk=tk)

    gpad = SUBLANES - N_GROUPS
    rpad = LANES - SUBLANES - N_EXPERTS
    wrT = jnp.concatenate([w_r1.T, jnp.zeros((gpad, d), F32),
                           jnp.transpose(w_r2, (0, 2, 1)).reshape(N_EXPERTS, d), jnp.zeros((rpad, d), F32)], axis=0)
    brT = jnp.concatenate([b_r1.astype(F32), jnp.full((gpad,), -jnp.inf, F32), b_r2.reshape(-1).astype(F32),
                           jnp.zeros((rpad,), F32)]).reshape(LANES, 1)
    x1, h2, route, cnt = _outproj(x2, a, yT, conv_params, w_out[:D_CONV].astype(BF16),
                                  w_out[D_CONV:].astype(BF16), out_g_att.reshape(1, -1), norm2_g.reshape(1, d),
                                  wrT.astype(BF16), brT, batch=batch, seq=seq, tm=tm, rc=min(128, seq))

    i32 = jnp.int32
    lanes = jnp.arange(LANES, dtype=i32)
    pick = lambda table, idx: jnp.sum(jnp.where(idx[:, None] == lanes, table[None, :], 0), axis=1).astype(i32)
    counts = cnt[:, 0].astype(i32)
    ends = jnp.cumsum(counts).astype(i32)
    starts = ends - counts
    dest = pick(starts, route[0].astype(i32)) + route[1].astype(i32)

    n_blocks = t_len // ROW_BLOCK
    b_lo = starts // ROW_BLOCK
    n_it = jnp.where(counts > 0, (ends - 1) // ROW_BLOCK - b_lo + 1, 0)
    it_end = jnp.cumsum(n_it).astype(i32)
    it_start = it_end - n_it
    idx = jnp.arange(n_blocks + N_CLASSES, dtype=i32)
    valid = idx < it_end[-1]
    last_cls = jnp.max(jnp.where(counts > 0, lanes, 0))
    cls = jnp.where(valid, jnp.sum(it_end[None, :] <= idx[:, None], axis=1).astype(i32), last_cls)
    blk = jnp.where(valid, pick(b_lo, cls) + idx - pick(it_start, cls), n_blocks - 1)
    row0 = blk * ROW_BLOCK
    lo = jnp.where(valid, jnp.maximum(pick(starts, cls), row0) - row0, 0)
    hi = jnp.where(valid, jnp.minimum(pick(ends, cls), row0 + ROW_BLOCK) - row0, 0)
    pair_a, pair_b = [], []
    for g in range(N_GROUPS):
        for a_loc in range(EXPERTS_PER_GROUP):
            for b_loc in range(a_loc + 1, EXPERTS_PER_GROUP):
                pair_a.append(g * EXPERTS_PER_GROUP + a_loc)
                pair_b.append(g * EXPERTS_PER_GROUP + b_loc)
    cpad = [0] * (LANES - N_CLASSES)
    items = (blk.astype(i32), pick(jnp.array(pair_a + cpad, i32), cls), pick(jnp.array(pair_b + cpad, i32), cls),
             lo.astype(i32), hi.astype(i32))

    xs = _dispatch(dest, h2, tm=min(1024, t_len))
    ys = _experts(items, xs, w_gate.astype(BF16), w_up.astype(BF16), w_down.astype(BF16))
    return dest, x1, ys


def kernel(x, norm1_g, w_in, b_f, w_dw, b_dw, conv_ln_g, conv_ln_b, out_g_conv, out_g_att, w_out, norm2_g,
           w_r1, b_r1, w_r2, b_r2, w_gate, w_up, w_down, final_g):
    assert norm1_g.shape[0] == 1, "single-layer stack"
    batch, seq, d = x.shape
    dest, x1, ys = _layer(
        x, norm1_g[0], w_in[0], b_f[0], w_dw[0], b_dw[0], conv_ln_g[0], conv_ln_b[0], out_g_conv[0],
        out_g_att[0], w_out[0], norm2_g[0], w_r1[0], b_r1[0], w_r2[0], b_r2[0], w_gate[0], w_up[0], w_down[0])
    out = _combine(dest, x1, final_g.reshape(1, d), ys, tm=min(512, batch * seq))
    return out.reshape(batch, seq, d)
```

```python
import functools

import jax
import jax.numpy as jnp
from jax import lax
from jax.experimental import pallas as pl
from jax.experimental.pallas import tpu as pltpu

D_MODEL = 1024
D_CONV = 512
N_HEADS = 8
HEAD_DIM = 64
D_ATT = N_HEADS * HEAD_DIM
CONV_WIDTH = 31
N_GROUPS = 4
EXPERTS_PER_GROUP = 8
N_EXPERTS = N_GROUPS * EXPERTS_PER_GROUP
TOP_K = 2
D_EXPERT = D_MODEL // 4
ROW_BLOCK = 256
EPS = 1e-6

LANES = 128
SUBLANES = 8
KAUG = 128
N_PIECES = 3
PIECE_STRIDE = 8
CONV_HALO = 32
PAIRS_PER_GROUP = EXPERTS_PER_GROUP * (EXPERTS_PER_GROUP - 1) // 2
N_CLASSES = N_GROUPS * PAIRS_PER_GROUP
ROW_W = D_MODEL + LANES
SKIP_LOG2 = 160.0
FIXED_REF_MAX_BOUND = 40.0
NORM_SLACK = 1.02
V_ROWS = 80
LOG2E = 1.4426950408889634
VMEM_LIMIT = 56 * 1024 * 1024

F32 = jnp.float32
BF16 = jnp.bfloat16


def _dot(a, b):
    return jnp.dot(a, b, preferred_element_type=F32)


def _dot_nt(a, b):
    return lax.dot_general(a, b, (((1,), (1,)), ((), ())), preferred_element_type=F32)


def _split3(x):
    hi = x.astype(BF16)
    r1 = x - hi.astype(F32)
    mid = r1.astype(BF16)
    lo = (r1 - mid.astype(F32)).astype(BF16)
    return hi.astype(F32), mid.astype(F32), lo.astype(F32)


def _piece_lane_mask(lane, h):
    return (lane == h) | (lane == h + PIECE_STRIDE) | (lane == h + 2 * PIECE_STRIDE)


def _inproj_kernel(x_ref, g1_ref, wvg_ref, wk_ref, wf_ref, bf_ref, wqT_ref, wvT_ref, ltri_ref, hsel_ref,
                   a_ref, kp_ref, qpT_ref, vT_ref, kstat_ref, qstat_ref, carry_ref, *, tm, tk):
    @pl.when(pl.program_id(1) == 0)
    def _():
        carry_ref[...] = jnp.zeros_like(carry_ref)

    x = x_ref[...]
    ms = jnp.mean(x * x, axis=-1, keepdims=True)
    hb = ((x * lax.rsqrt(ms + EPS)) * g1_ref[...]).astype(BF16)

    zvg = _dot(hb, wvg_ref[...])
    a_ref[...] = zvg[:, :D_CONV] * jax.nn.sigmoid(zvg[:, D_CONV:])

    kk = _dot(hb, wk_ref[...])

    zf = _dot(hb, wf_ref[...]) + bf_ref[...]
    lf = jnp.minimum(zf, 0.0) - jnp.log1p(jnp.exp(-jnp.abs(zf)))
    lane = lax.broadcasted_iota(jnp.int32, (tm, LANES), 1)
    hi, mid, lo = _split3(lf)
    lf3 = jnp.where(lane < PIECE_STRIDE, hi,
                    jnp.where(lane < 2 * PIECE_STRIDE, mid,
                              jnp.where(lane < 3 * PIECE_STRIDE, lo, 0.0))).astype(BF16)
    cs3 = _dot(ltri_ref[...], lf3)
    c = (cs3 + pltpu.roll(cs3, LANES - PIECE_STRIDE, 1)
         + pltpu.roll(cs3, LANES - 2 * PIECE_STRIDE, 1)) + carry_ref[...]
    carry_ref[...] = c[tm - 1:tm, :]

    nhi, nmid, nlo = _split3(c * (-LOG2E))
    p3 = jnp.where(lane < PIECE_STRIDE, nhi,
                   jnp.where(lane < 2 * PIECE_STRIDE, pltpu.roll(nmid, PIECE_STRIDE, 1),
                             jnp.where(lane < 3 * PIECE_STRIDE, pltpu.roll(nlo, 2 * PIECE_STRIDE, 1), 0.0)))
    p3_hi = pltpu.roll(p3, HEAD_DIM, 1)
    q_pieces = (-p3).T[0:HEAD_DIM]
    q_shift = N_PIECES * PIECE_STRIDE

    qT = _dot_nt(wqT_ref[...], hb) * LOG2E
    vT = _dot_nt(wvT_ref[...], hb)
    row = lax.broadcasted_iota(jnp.int32, (HEAD_DIM, tm), 0)
    vrow = lax.broadcasted_iota(jnp.int32, (V_ROWS - HEAD_DIM, tk), 0)
    v_tail = jnp.where(vrow == 0, 1.0, 0.0).astype(BF16)

    for h in range(N_HEADS):
        kcol = kk[:, (h // 2) * LANES:(h // 2 + 1) * LANES]
        f_h = jnp.where(_piece_lane_mask(row, h), q_pieces, 0.0)
        aug_q = jnp.where(_piece_lane_mask(row, h), 1.0, 0.0) + jnp.concatenate(
            [jnp.zeros((q_shift, tm), F32), f_h[0:HEAD_DIM - q_shift]], axis=0)
        q_h = qT[h * HEAD_DIM:(h + 1) * HEAD_DIM, :]
        if h % 2 == 0:
            ext = jnp.where(_piece_lane_mask(lane, h + HEAD_DIM), p3_hi,
                            jnp.where(_piece_lane_mask(lane, h + HEAD_DIM + q_shift), 1.0, 0.0))
            kp = jnp.where(lane < HEAD_DIM, kcol, ext)
            qp = jnp.concatenate([q_h, aug_q], axis=0)
        else:
            ext = jnp.where(_piece_lane_mask(lane, h), p3,
                            jnp.where(_piece_lane_mask(lane, h + q_shift), 1.0, 0.0))
            kp = jnp.where(lane >= HEAD_DIM, kcol, ext)
            qp = jnp.concatenate([aug_q, q_h], axis=0)
        qpT_ref[0, h] = qp.astype(BF16)
        for cidx in range(tm // tk):
            kp_ref[0, h, cidx] = kp[cidx * tk:(cidx + 1) * tk, :].astype(BF16)
            v_h = vT[h * HEAD_DIM:(h + 1) * HEAD_DIM, cidx * tk:(cidx + 1) * tk].astype(BF16)
            vT_ref[0, h, cidx] = jnp.concatenate([v_h, v_tail], axis=0)

    kn2 = jnp.max(_dot((kk * kk).astype(BF16), hsel_ref[...]), axis=0, keepdims=True)
    srow = lax.broadcasted_iota(jnp.int32, (SUBLANES, LANES), 0)
    kstat_ref[0, 0] = jnp.where(srow == 0, c[0:1, :], jnp.where(srow == 1, c[tm - 1:tm, :],
                                                                 jnp.where(srow == 2, kn2, 0.0)))
    q2 = qT * qT
    qn2 = jnp.concatenate([jnp.sum(q2[h * HEAD_DIM:(h + 1) * HEAD_DIM, :], axis=0, keepdims=True)
                           for h in range(N_HEADS)], axis=0)
    qstat_ref[0, 0] = jnp.broadcast_to(jnp.max(qn2, axis=1, keepdims=True), (N_HEADS, LANES))


def _inproj(x2, g1, wvg, wk, wf, bf3, wqT, wvT, *, batch, seq, tm, tk):
    nt = seq // tm
    nk = seq // tk
    ltri = jnp.tril(jnp.ones((tm, tm), F32)).astype(BF16)
    hsel = (jnp.arange(D_ATT)[:, None] // HEAD_DIM == jnp.arange(LANES)[None, :]).astype(BF16)
    const = lambda shape: pl.BlockSpec(shape, lambda b, t: (0,) * len(shape))
    stat_spec = pl.BlockSpec((1, 1, SUBLANES, LANES), lambda b, t: (b, t, 0, 0))
    stat_shape = jax.ShapeDtypeStruct((batch, nt, SUBLANES, LANES), F32)
    return pl.pallas_call(
        functools.partial(_inproj_kernel, tm=tm, tk=tk),
        grid=(batch, nt),
        in_specs=[
            pl.BlockSpec((tm, D_MODEL), lambda b, t: (b * nt + t, 0)),
            const((1, D_MODEL)), const((D_MODEL, 2 * D_CONV)), const((D_MODEL, D_ATT)),
            const((D_MODEL, LANES)), const((1, LANES)), const((D_ATT, D_MODEL)), const((D_ATT, D_MODEL)),
            const((tm, tm)), const((D_ATT, LANES)),
        ],
        out_specs=[
            pl.BlockSpec((tm, D_CONV), lambda b, t: (b * nt + t, 0)),
            pl.BlockSpec((1, N_HEADS, tm // tk, tk, KAUG), lambda b, t: (b, 0, t, 0, 0)),
            pl.BlockSpec((1, N_HEADS, KAUG, tm), lambda b, t: (b, 0, 0, t)),
            pl.BlockSpec((1, N_HEADS, tm // tk, V_ROWS, tk), lambda b, t: (b, 0, t, 0, 0)),
            stat_spec, stat_spec,
        ],
        out_shape=[
            jax.ShapeDtypeStruct((batch * seq, D_CONV), F32),
            jax.ShapeDtypeStruct((batch, N_HEADS, nk, tk, KAUG), BF16),
            jax.ShapeDtypeStruct((batch, N_HEADS, KAUG, seq), BF16),
            jax.ShapeDtypeStruct((batch, N_HEADS, nk, V_ROWS, tk), BF16),
            stat_shape, stat_shape,
        ],
        scratch_shapes=[pltpu.VMEM((1, LANES), F32)],
        compiler_params=pltpu.CompilerParams(
            dimension_semantics=("arbitrary", "arbitrary"), vmem_limit_bytes=VMEM_LIMIT),
        name="inproj",
    )(x2, g1, wvg, wk, wf, bf3, wqT, wvT, ltri, hsel)


def _conv_tile(a_ref, w_ref, b_ref, lng_ref, lnb_ref, og_ref, sh_ref, acc_ref, *, tc, rc):
    n_cb = D_CONV // LANES

    @pl.when(pl.program_id(1) == 0)
    def _():
        for cb in range(n_cb):
            sh_ref[0, cb, 0:CONV_HALO, :] = jnp.zeros((CONV_HALO, LANES), F32)

    @pl.when(pl.program_id(1) > 0)
    def _():
        for cb in range(n_cb):
            sh_ref[0, cb, 0:CONV_HALO, :] = sh_ref[0, cb, tc:tc + CONV_HALO, :]

    n_sh = tc + CONV_HALO - SUBLANES
    for cb in range(n_cb):
        sh_ref[0, cb, CONV_HALO:CONV_HALO + tc, :] = a_ref[:, cb * LANES:(cb + 1) * LANES]
        for f in range(1, SUBLANES):
            sh_ref[f, cb, 0:n_sh, :] = sh_ref[0, cb, f:f + n_sh, :]

    base = CONV_HALO - (CONV_WIDTH - 1)
    for cb in range(n_cb):
        cols = slice(cb * LANES, (cb + 1) * LANES)

        def chunk(c, carry, cb=cb, cols=cols):
            r0 = pl.multiple_of(c * rc, rc)
            acc = jnp.zeros((rc, LANES), F32)
            for j in range(CONV_WIDTH):
                f = (base + j) % SUBLANES
                acc = acc + w_ref[j:j + 1, cols] * sh_ref[f, cb, pl.ds(r0 + (base + j - f), rc), :]
            acc_ref[pl.ds(r0, rc), cols] = acc
            return carry

        lax.fori_loop(0, tc // rc, chunk, 0)

    y = acc_ref[...] + b_ref[...]
    mu = jnp.mean(y, axis=-1, keepdims=True)
    yc = y - mu
    var = jnp.mean(yc * yc, axis=-1, keepdims=True)
    yn = yc * lax.rsqrt(var + EPS) * lng_ref[...] + lnb_ref[...]
    s = yn * jax.nn.sigmoid(yn)
    ms = jnp.mean(s * s, axis=-1, keepdims=True)
    return (s * lax.rsqrt(ms + EPS) * og_ref[...]).astype(BF16)


def _attn_fixed_ref_kernel(g0_ref, bnd_ref, qT_ref, k_ref, v_ref, o_ref, acc_ref, *, tq, tk):
    i = pl.program_id(2)
    qT = qT_ref[0, 0]
    n_sub = tq // tk
    bnd = bnd_ref[0, 0, 0:1, 0:1]
    g0 = g0_ref[(pl.program_id(0) * N_HEADS + pl.program_id(1)) * pl.num_programs(2) + i]

    def values(g):
        return jnp.concatenate([v_ref[0, 0, g * n_sub + d] for d in range(n_sub)], axis=1)

    def probs(g):
        return jnp.concatenate([jnp.exp2(_dot(k_ref[0, 0, g * n_sub + d], qT) - bnd).astype(BF16)
                                for d in range(n_sub)], axis=0)

    def two_groups(p, acc):
        g = g0 + 2 * p
        acc = acc + _dot(values(g), probs(g))
        return acc + _dot(values(g + 1), probs(g + 1))

    n_full = i - g0
    acc_ref[...] = lax.fori_loop(0, n_full // 2, two_groups, jnp.zeros((V_ROWS, tq), F32))

    @pl.when(n_full % 2 == 1)
    def _():
        acc_ref[...] += _dot(values(i - 1), probs(i - 1))

    key = lax.broadcasted_iota(jnp.int32, (tk, tk), 0)
    qry = lax.broadcasted_iota(jnp.int32, (tk, tk), 1)
    rows = []
    for d in range(n_sub):
        s = _dot(k_ref[0, 0, i * n_sub + d], qT[:, d * tk:])
        tiles = [jnp.zeros((tk, d * tk), BF16)] if d else []
        tiles.append(jnp.exp2(jnp.where(key <= qry, s[:, :tk], -jnp.inf) - bnd).astype(BF16))
        if d + 1 < n_sub:
            tiles.append(jnp.exp2(s[:, tk:] - bnd).astype(BF16))
        rows.append(jnp.concatenate(tiles, axis=1))
    acc = acc_ref[...] + _dot(values(i), jnp.concatenate(rows, axis=0))
    o_ref[0] = acc[:HEAD_DIM] / acc[HEAD_DIM:HEAD_DIM + 1]


def _attn_kernel(g0_ref, qT_ref, k_ref, v_ref, o_ref, sa_ref, sb_ref, cma_ref, cmb_ref, *, tq, tk):
    i = pl.program_id(2)
    qT = qT_ref[0, 0]
    n_sub = tq // tk

    def scores(g, s_ref, cm_ref):
        for d in range(n_sub):
            s = _dot(k_ref[0, 0, g * n_sub + d], qT)
            s_ref[d] = s
            cm_ref[d] = jnp.max(s, axis=0, keepdims=True)

    def consume(g, s_ref, cm_ref, carry, nxt=None):
        m, acc = carry
        m_new = m
        for d in range(n_sub):
            m_new = jnp.maximum(m_new, cm_ref[d])
        ps = []
        for d in range(n_sub):
            if nxt is not None:
                g_n, sn_ref, cmn_ref = nxt
                s = _dot(k_ref[0, 0, g_n * n_sub + d], qT)
                sn_ref[d] = s
                cmn_ref[d] = jnp.max(s, axis=0, keepdims=True)
            ps.append(jnp.exp2(s_ref[d] - m_new).astype(BF16))
        p = jnp.concatenate(ps, axis=0)
        v = jnp.concatenate([v_ref[0, 0, g * n_sub + d] for d in range(n_sub)], axis=1)
        return m_new, jnp.exp2(m - m_new) * acc + _dot(v, p)

    key = lax.broadcasted_iota(jnp.int32, (tk, tk), 0)
    qry = lax.broadcasted_iota(jnp.int32, (tk, tk), 1)

    def consume_diag(s_ref, cm_ref, carry):
        m, acc = carry
        lanes = lambda j: slice(j * tk, (j + 1) * tk)
        causal = [jnp.where(key <= qry, s_ref[d, :, lanes(d)], -jnp.inf) for d in range(n_sub)]
        m_tiles = []
        for j in range(n_sub):
            m_j = jnp.maximum(m[:, lanes(j)], jnp.max(causal[j], axis=0, keepdims=True))
            for d in range(j):
                m_j = jnp.maximum(m_j, cm_ref[d, :, lanes(j)])
            m_tiles.append(m_j)
        m_new = jnp.concatenate(m_tiles, axis=1)
        rows = []
        for d in range(n_sub):
            tiles = [jnp.zeros((tk, d * tk), BF16)] if d else []
            tiles.append(jnp.exp2(causal[d] - m_tiles[d]).astype(BF16))
            if d + 1 < n_sub:
                rest = slice((d + 1) * tk, tq)
                tiles.append(jnp.exp2(s_ref[d, :, rest] - m_new[:, rest]).astype(BF16))
            rows.append(jnp.concatenate(tiles, axis=1))
        p = jnp.concatenate(rows, axis=0)
        v = jnp.concatenate([v_ref[0, 0, i * n_sub + d] for d in range(n_sub)], axis=1)
        acc = jnp.exp2(m - m_new) * acc + _dot(v, p)
        o_ref[0] = acc[:HEAD_DIM] / acc[HEAD_DIM:HEAD_DIM + 1]

    g0 = g0_ref[(pl.program_id(0) * N_HEADS + pl.program_id(1)) * pl.num_programs(2) + i]
    n_full = i - g0

    def pair(p, carry):
        g = g0 + 2 * p
        carry = consume(g, sa_ref, cma_ref, carry, nxt=(g + 1, sb_ref, cmb_ref))
        return consume(g + 1, sb_ref, cmb_ref, carry, nxt=(g + 2, sa_ref, cma_ref))

    scores(g0, sa_ref, cma_ref)
    init = (jnp.full((1, tq), -jnp.inf, F32), jnp.zeros((V_ROWS, tq), F32))
    carry = lax.fori_loop(0, n_full // 2, pair, init)

    @pl.when(n_full % 2 == 1)
    def _():
        consume_diag(sb_ref, cmb_ref, consume(i - 1, sa_ref, cma_ref, carry, nxt=(i, sb_ref, cmb_ref)))

    @pl.when(n_full % 2 == 0)
    def _():
        consume_diag(sa_ref, cma_ref, carry)


def _first_groups(kstat, qstat, *, tm, tq):
    per = tq // tm
    bound = jnp.sqrt(jnp.max(kstat[:, :, 2, :N_HEADS], axis=1) * jnp.max(qstat[:, :, :, 0], axis=1))
    bound = bound * NORM_SLACK
    cq = kstat[:, ::per, 0, :N_HEADS]
    ck = kstat[:, per - 1::per, 1, :N_HEADS]
    d = (cq[:, :, None, :] - ck[:, None, :, :]) * LOG2E
    nq = cq.shape[1]
    earlier = jnp.arange(nq)[None, :, None, None] > jnp.arange(nq)[None, None, :, None]
    skip = earlier & (2.0 * bound[:, None, None, :] + d < -SKIP_LOG2)
    g0 = jnp.sum(jnp.cumprod(skip.astype(jnp.int32), axis=2), axis=2)
    return jnp.transpose(g0, (0, 2, 1)).reshape(-1).astype(jnp.int32), bound


def _attention(g0, bound, qpT, kp, vT, *, batch, seq, tq, tk):
    nk = seq // tk
    params = pltpu.CompilerParams(
        dimension_semantics=("arbitrary", "arbitrary", "arbitrary"), vmem_limit_bytes=VMEM_LIMIT)
    out_shape = jax.ShapeDtypeStruct((batch, D_ATT, seq), F32)
    qkv_specs = [
        pl.BlockSpec((1, 1, KAUG, tq), lambda b, h, i, g0: (b, h, 0, i)),
        pl.BlockSpec((1, 1, nk, tk, KAUG), lambda b, h, i, g0: (b, h, 0, 0, 0)),
        pl.BlockSpec((1, 1, nk, V_ROWS, tk), lambda b, h, i, g0: (b, h, 0, 0, 0)),
    ]
    out_spec = pl.BlockSpec((1, HEAD_DIM, tq), lambda b, h, i, g0: (b, h, i))

    def fixed_ref(g0, bnd, qpT, kp, vT):
        grid_spec = pltpu.PrefetchScalarGridSpec(
            num_scalar_prefetch=1, grid=(batch, N_HEADS, seq // tq),
            in_specs=[pl.BlockSpec((1, 1, SUBLANES, LANES), lambda b, h, i, g0: (b, h, 0, 0))] + qkv_specs,
            out_specs=out_spec, scratch_shapes=[pltpu.VMEM((V_ROWS, tq), F32)])
        return pl.pallas_call(functools.partial(_attn_fixed_ref_kernel, tq=tq, tk=tk), grid_spec=grid_spec,
                              out_shape=out_shape, compiler_params=params, name="attention_fixed_ref",
                              )(g0, bnd, qpT, kp, vT)

    def running_max(g0, bnd, qpT, kp, vT):
        del bnd
        grid_spec = pltpu.PrefetchScalarGridSpec(
            num_scalar_prefetch=1, grid=(batch, N_HEADS, seq // tq), in_specs=qkv_specs, out_specs=out_spec,
            scratch_shapes=[pltpu.VMEM((tq // tk, tk, tq), F32), pltpu.VMEM((tq // tk, tk, tq), F32),
                            pltpu.VMEM((tq // tk, 1, tq), F32), pltpu.VMEM((tq // tk, 1, tq), F32)])
        return pl.pallas_call(functools.partial(_attn_kernel, tq=tq, tk=tk), grid_spec=grid_spec,
                              out_shape=out_shape, compiler_params=params, name="attention",
                              )(g0, qpT, kp, vT)

    bnd = jnp.broadcast_to(bound[:, :, None, None], (batch, N_HEADS, SUBLANES, LANES))
    return lax.cond(jnp.max(bound) <= FIXED_REF_MAX_BOUND, fixed_ref, running_max, g0, bnd, qpT, kp, vT)


def _outproj_kernel(x_ref, a_ref, yT_ref, wdw_ref, bdw_ref, lng_ref, lnb_ref, gc_ref, woc_ref, woa_ref, ga_ref,
                    g2_ref, wrT_ref, brT_ref, utri_ref, x1_ref, h2_ref, route_ref, cnt_ref,
                    carry_ref, sh_ref, acc_ref, *, tm, rc):
    first = (pl.program_id(0) == 0) & (pl.program_id(1) == 0)

    @pl.when(first)
    def _():
        carry_ref[...] = jnp.zeros_like(carry_ref)

    mc = _conv_tile(a_ref, wdw_ref, bdw_ref, lng_ref, lnb_ref, gc_ref, sh_ref, acc_ref, tc=tm, rc=rc)
    yT = yT_ref[0]
    msa = jnp.mean(yT * yT, axis=0, keepdims=True)
    yn = (yT * lax.rsqrt(msa + EPS)).T * ga_ref[...]
    x1 = x_ref[...] + _dot(mc, woc_ref[...]) + _dot(yn.astype(BF16), woa_ref[...])
    x1_ref[...] = x1
    ms = jnp.mean(x1 * x1, axis=-1, keepdims=True)
    h2 = (x1 * lax.rsqrt(ms + EPS)) * g2_ref[...]
    h2_ref[:, :D_MODEL] = h2

    lg = _dot_nt(wrT_ref[...], h2.astype(BF16)) + brT_ref[...]
    neg = -jnp.inf
    sub = lax.broadcasted_iota(jnp.int32, (SUBLANES, tm), 0)
    big = jnp.int32(SUBLANES)
    first_of = lambda hit: jnp.min(jnp.where(hit, sub, big), axis=0, keepdims=True)

    lg1 = lg[0:SUBLANES]
    m1 = jnp.max(lg1, axis=0, keepdims=True)
    p1_sel = 1.0 / jnp.sum(jnp.exp(lg1 - m1), axis=0, keepdims=True)
    grp = first_of(lg1 == m1)

    slab = lambda g: lg[SUBLANES * (g + 1):SUBLANES * (g + 2)]
    v = slab(N_GROUPS - 1)
    for g in range(N_GROUPS - 2, -1, -1):
        v = jnp.where(grp == g, slab(g), v)
    v1 = jnp.max(v, axis=0, keepdims=True)
    j1 = first_of(v == v1)
    vv = jnp.where(sub == j1, neg, v)
    v2 = jnp.max(vv, axis=0, keepdims=True)
    j2 = first_of(vv == v2)
    e21 = jnp.exp(v2 - v1)
    w0 = p1_sel / (1.0 + e21)
    w1 = p1_sel * e21 / (1.0 + e21)

    swap = j2 < j1
    al = jnp.where(swap, j2, j1)
    bl = jnp.where(swap, j1, j2)
    wa = jnp.where(swap, w1, w0)
    wb = jnp.where(swap, w0, w1)
    cid = PAIRS_PER_GROUP * grp + ((al * (2 * EXPERTS_PER_GROUP - 1 - al)) >> 1) + (bl - al - 1)

    cls = lax.broadcasted_iota(jnp.int32, (LANES, tm), 0)
    oh = cls == cid
    cmat = jnp.where(oh, 1.0, 0.0)
    prefix = _dot(cmat.astype(BF16), utri_ref[...]) + carry_ref[...]
    rank = jnp.sum(jnp.where(oh, prefix, 0.0), axis=0, keepdims=True)
    counts = prefix[:, tm - 1:tm] + cmat[:, tm - 1:tm]
    carry_ref[...] = counts
    cnt_ref[...] = counts

    h2_ref[:, D_MODEL:] = jnp.where(cls == 0, wa, jnp.where(cls == 1, wb, 0.0)).T
    route_ref[...] = jnp.where(sub == 0, cid.astype(F32), jnp.where(sub == 1, rank, 0.0))


def _outproj(x2, a, yT, conv_params, woc, woa, ga, g2, wrT, brT, *, batch, seq, tm, rc):
    nt = seq // tm
    utri = jnp.triu(jnp.ones((tm, tm), F32), 1).astype(BF16)
    const = lambda shape: pl.BlockSpec(shape, lambda b, t: (0,) * len(shape))
    row_spec = lambda w: pl.BlockSpec((tm, w), lambda b, t: (b * nt + t, 0))
    return pl.pallas_call(
        functools.partial(_outproj_kernel, tm=tm, rc=rc),
        grid=(batch, nt),
        in_specs=[row_spec(D_MODEL), row_spec(D_CONV),
                  pl.BlockSpec((1, D_ATT, tm), lambda b, t: (b, 0, t)),
                  const((CONV_WIDTH, D_CONV)), const((1, D_CONV)), const((1, D_CONV)), const((1, D_CONV)),
                  const((1, D_CONV)),
                  const((D_CONV, D_MODEL)), const((D_ATT, D_MODEL)), const((1, D_ATT)), const((1, D_MODEL)),
                  const((LANES, D_MODEL)), const((LANES, 1)), const((tm, tm))],
        out_specs=[row_spec(D_MODEL), row_spec(ROW_W),
                   pl.BlockSpec((SUBLANES, tm), lambda b, t: (0, b * nt + t)), const((LANES, 1))],
        out_shape=[jax.ShapeDtypeStruct((batch * seq, D_MODEL), F32),
                   jax.ShapeDtypeStruct((batch * seq, ROW_W), F32),
                   jax.ShapeDtypeStruct((SUBLANES, batch * seq), F32),
                   jax.ShapeDtypeStruct((LANES, 1), F32)],
        scratch_shapes=[pltpu.VMEM((LANES, 1), F32),
                        pltpu.VMEM((SUBLANES, D_CONV // LANES, tm + CONV_HALO, LANES), F32),
                        pltpu.VMEM((tm, D_CONV), F32)],
        compiler_params=pltpu.CompilerParams(
            dimension_semantics=("arbitrary", "arbitrary"), vmem_limit_bytes=VMEM_LIMIT),
        name="outproj",
    )(x2, a, yT, *conv_params, woc, woa, ga, g2, wrT, brT, utri)


def _row_copies(src_at, dst_at, sem):
    return pltpu.make_async_copy(src_at, dst_at, sem)


def _dispatch_kernel(dest_ref, h2_ref, xs_ref, sem, *, tm):
    def body(g, carry):
        for u in range(SUBLANES):
            _row_copies(h2_ref.at[g, pl.ds(u, 1)], xs_ref.at[pl.ds(dest_ref[g * SUBLANES + u], 1)],
                        sem).start(priority=u % 2)
        return carry

    lax.fori_loop(0, tm // SUBLANES, body, 0)
    done = xs_ref.at[pl.ds(0, tm)]
    _row_copies(done, done, sem).wait()


def _dispatch(dest, h2, *, tm):
    t_len = h2.shape[0]
    return pl.pallas_call(
        functools.partial(_dispatch_kernel, tm=tm),
        grid=(t_len // tm,),
        in_specs=[pl.BlockSpec((tm,), lambda i: (i,), memory_space=pltpu.SMEM),
                  pl.BlockSpec((tm // SUBLANES, SUBLANES, ROW_W), lambda i: (i, 0, 0))],
        out_specs=pl.BlockSpec(memory_space=pl.ANY),
        out_shape=jax.ShapeDtypeStruct((t_len, ROW_W), F32),
        scratch_shapes=[pltpu.SemaphoreType.DMA(())],
        compiler_params=pltpu.CompilerParams(dimension_semantics=("arbitrary",), vmem_limit_bytes=VMEM_LIMIT),
        name="dispatch",
    )(dest, h2.reshape(t_len // SUBLANES, SUBLANES, ROW_W))


def _expert_kernel(blk_ref, ea_ref, eb_ref, lo_ref, hi_ref, x_ref, wga_ref, wua_ref, wda_ref,
                   wgb_ref, wub_ref, wdb_ref, y_ref):
    del blk_ref, ea_ref, eb_ref
    i = pl.program_id(0)
    lo = lo_ref[i]
    hi = hi_ref[i]

    def mlp(xb, wg_ref, wu_ref, wd_ref):
        g = _dot(xb, wg_ref[0])
        u = _dot(xb, wu_ref[0])
        return _dot(((g * jax.nn.sigmoid(g)) * u).astype(BF16), wd_ref[0])

    @pl.when(hi > lo)
    def _():
        xb = x_ref[:, :D_MODEL].astype(BF16)
        wts = x_ref[:, D_MODEL:]
        y = wts[:, 0:1] * mlp(xb, wga_ref, wua_ref, wda_ref) + wts[:, 1:2] * mlp(xb, wgb_ref, wub_ref, wdb_ref)
        row = lax.broadcasted_iota(jnp.int32, (ROW_BLOCK, 1), 0)
        mine = (row >= lo) & (row < hi)

        @pl.when(lo == 0)
        def _():
            y_ref[...] = jnp.where(mine, y, 0.0)

        @pl.when(lo > 0)
        def _():
            y_ref[...] = jnp.where(mine, y, y_ref[...])


def _experts(items, xs, wg, wu, wd):
    n_rows = xs.shape[0]
    n_items = items[0].shape[0]
    wspec = lambda shape, which: pl.BlockSpec(
        shape, lambda i, blk, ea, eb, lo, hi: ((ea, eb)[which][i], 0, 0))
    grid_spec = pltpu.PrefetchScalarGridSpec(
        num_scalar_prefetch=5,
        grid=(n_items,),
        in_specs=[pl.BlockSpec((ROW_BLOCK, ROW_W), lambda i, blk, ea, eb, lo, hi: (blk[i], 0)),
                  wspec((1, D_MODEL, D_EXPERT), 0), wspec((1, D_MODEL, D_EXPERT), 0), wspec((1, D_EXPERT, D_MODEL), 0),
                  wspec((1, D_MODEL, D_EXPERT), 1), wspec((1, D_MODEL, D_EXPERT), 1), wspec((1, D_EXPERT, D_MODEL), 1)],
        out_specs=pl.BlockSpec((ROW_BLOCK, D_MODEL), lambda i, blk, ea, eb, lo, hi: (blk[i], 0)),
    )
    return pl.pallas_call(
        _expert_kernel,
        grid_spec=grid_spec,
        out_shape=jax.ShapeDtypeStruct((n_rows, D_MODEL), F32),
        compiler_params=pltpu.CompilerParams(dimension_semantics=("arbitrary",), vmem_limit_bytes=VMEM_LIMIT),
        name="experts",
    )(*items, xs, wg, wu, wd, wg, wu, wd)


def _combine_kernel(dcur_ref, dnxt_ref, x1_ref, g_ref, ys_ref, out_ref, buf_ref, sems, *, tm):
    i = pl.program_id(0)
    slot = i % 2

    def issue(d_ref, s):
        def body(g, carry):
            for u in range(SUBLANES):
                _row_copies(ys_ref.at[pl.ds(d_ref[g * SUBLANES + u], 1)], buf_ref.at[s, g, pl.ds(u, 1)],
                            sems.at[s]).start(priority=u % 2)
            return carry

        lax.fori_loop(0, tm // SUBLANES, body, 0)

    @pl.when(i == 0)
    def _():
        issue(dcur_ref, 0)

    @pl.when(i + 1 < pl.num_programs(0))
    def _():
        issue(dnxt_ref, 1 - slot)

    _row_copies(buf_ref.at[slot], buf_ref.at[slot], sems.at[slot]).wait()
    x2 = x1_ref[...] + buf_ref[slot]
    ms = jnp.mean(x2 * x2, axis=-1, keepdims=True)
    out_ref[...] = (x2 * lax.rsqrt(ms + EPS)) * g_ref[...]


def _combine(dest, x1, gf, ys, *, tm):
    t_len = x1.shape[0]
    n = t_len // tm
    return pl.pallas_call(
        functools.partial(_combine_kernel, tm=tm),
        grid=(n,),
        in_specs=[pl.BlockSpec((tm,), lambda i: (i,), memory_space=pltpu.SMEM),
                  pl.BlockSpec((tm,), lambda i: (jnp.minimum(i + 1, n - 1),), memory_space=pltpu.SMEM),
                  pl.BlockSpec((tm // SUBLANES, SUBLANES, D_MODEL), lambda i: (i, 0, 0)),
                  pl.BlockSpec((1, D_MODEL), lambda i: (0, 0)),
                  pl.BlockSpec(memory_space=pl.ANY)],
        out_specs=pl.BlockSpec((tm // SUBLANES, SUBLANES, D_MODEL), lambda i: (i, 0, 0)),
        out_shape=jax.ShapeDtypeStruct((t_len // SUBLANES, SUBLANES, D_MODEL), F32),
        scratch_shapes=[pltpu.VMEM((2, tm // SUBLANES, SUBLANES, D_MODEL), F32), pltpu.SemaphoreType.DMA((2,))],
        compiler_params=pltpu.CompilerParams(dimension_semantics=("arbitrary",), vmem_limit_bytes=VMEM_LIMIT),
        name="combine",
    )(dest, dest, x1.reshape(t_len // SUBLANES, SUBLANES, D_MODEL), gf, ys)


def _layer(x, norm1_g, w_in, b_f, w_dw, b_dw, conv_ln_g, conv_ln_b, out_g_conv, out_g_att, w_out,
           norm2_g, w_r1, b_r1, w_r2, b_r2, w_gate, w_up, w_down):
    batch, seq, d = x.shape
    t_len = batch * seq
    tm = min(512, seq)
    tk = min(256, seq)
    tq = min(1024, seq)
    o1, o2 = D_CONV, 2 * D_CONV
    o3, o4, o5 = o2 + D_ATT, o2 + 2 * D_ATT, o2 + 3 * D_ATT

    wvg = w_in[:, :o2].astype(BF16)
    wk = w_in[:, o3:o4].astype(BF16)
    wqT = (w_in[:, o2:o3] * (HEAD_DIM ** -0.5)).T.astype(BF16)
    wvT = w_in[:, o4:o5].T.astype(BF16)
    assert PIECE_STRIDE == N_HEADS
    pad = LANES - N_PIECES * PIECE_STRIDE
    wf = jnp.pad(jnp.tile(w_in[:, o5:], (1, N_PIECES)), ((0, 0), (0, pad))).astype(BF16)
    bf3 = jnp.pad(jnp.tile(b_f.astype(F32).reshape(1, N_HEADS), (1, N_PIECES)), ((0, 0), (0, pad)))

    x2 = x.reshape(t_len, d)
    a, kp, qpT, vT, kstat, qstat = _inproj(x2, norm1_g.reshape(1, d), wvg, wk, wf, bf3, wqT, wvT,
                                           batch=batch, seq=seq, tm=tm, tk=tk)
    conv_params = (w_dw, b_dw.reshape(1, -1), conv_ln_g.reshape(1, -1), conv_ln_b.reshape(1, -1),
                   out_g_conv.reshape(1, -1))
    g0, bound = _first_groups(kstat, qstat, tm=tm, tq=tq)
    yT = _attention(g0, bound, qpT, kp, vT, batch=batch, seq=seq, tq=tq, tk=tk)

    gpad = SUBLANES - N_GROUPS
    rpad = LANES - SUBLANES - N_EXPERTS
    wrT = jnp.concatenate([w_r1.T, jnp.zeros((gpad, d), F32),
                           jnp.transpose(w_r2, (0, 2, 1)).reshape(N_EXPERTS, d), jnp.zeros((rpad, d), F32)], axis=0)
    brT = jnp.concatenate([b_r1.astype(F32), jnp.full((gpad,), -jnp.inf, F32), b_r2.reshape(-1).astype(F32),
                           jnp.zeros((rpad,), F32)]).reshape(LANES, 1)
    x1, h2, route, cnt = _outproj(x2, a, yT, conv_params, w_out[:D_CONV].astype(BF16),
                                  w_out[D_CONV:].astype(BF16), out_g_att.reshape(1, -1), norm2_g.reshape(1, d),
                                  wrT.astype(BF16), brT, batch=batch, seq=seq, tm=tm, rc=min(128, seq))

    i32 = jnp.int32
    lanes = jnp.arange(LANES, dtype=i32)
    pick = lambda table, idx: jnp.sum(jnp.where(idx[:, None] == lanes, table[None, :], 0), axis=1).astype(i32)
    counts = cnt[:, 0].astype(i32)
    ends = jnp.cumsum(counts).astype(i32)
    starts = ends - counts
    dest = pick(starts, route[0].astype(i32)) + route[1].astype(i32)

    n_blocks = t_len // ROW_BLOCK
    b_lo = starts // ROW_BLOCK
    n_it = jnp.where(counts > 0, (ends - 1) // ROW_BLOCK - b_lo + 1, 0)
    it_end = jnp.cumsum(n_it).astype(i32)
    it_start = it_end - n_it
    idx = jnp.arange(n_blocks + N_CLASSES, dtype=i32)
    valid = idx < it_end[-1]
    last_cls = jnp.max(jnp.where(counts > 0, lanes, 0))
    cls = jnp.where(valid, jnp.sum(it_end[None, :] <= idx[:, None], axis=1).astype(i32), last_cls)
    blk = jnp.where(valid, pick(b_lo, cls) + idx - pick(it_start, cls), n_blocks - 1)
    row0 = blk * ROW_BLOCK
    lo = jnp.where(valid, jnp.maximum(pick(starts, cls), row0) - row0, 0)
    hi = jnp.where(valid, jnp.minimum(pick(ends, cls), row0 + ROW_BLOCK) - row0, 0)
    pair_a, pair_b = [], []
    for g in range(N_GROUPS):
        for a_loc in range(EXPERTS_PER_GROUP):
            for b_loc in range(a_loc + 1, EXPERTS_PER_GROUP):
                pair_a.append(g * EXPERTS_PER_GROUP + a_loc)
                pair_b.append(g * EXPERTS_PER_GROUP + b_loc)
    cpad = [0] * (LANES - N_CLASSES)
    items = (blk.astype(i32), pick(jnp.array(pair_a + cpad, i32), cls), pick(jnp.array(pair_b + cpad, i32), cls),
             lo.astype(i32), hi.astype(i32))

    xs = _dispatch(dest, h2, tm=min(1024, t_len))
    ys = _experts(items, xs, w_gate.astype(BF16), w_up.astype(BF16), w_down.astype(BF16))
    return dest, x1, ys


def kernel(x, norm1_g, w_in, b_f, w_dw, b_dw, conv_ln_g, conv_ln_b, out_g_conv, out_g_att, w_out, norm2_g,
           w_r1, b_r1, w_r2, b_r2, w_gate, w_up, w_down, final_g):
    assert norm1_g.shape[0] == 1, "single-layer stack"
    batch, seq, d = x.shape
    dest, x1, ys = _layer(
        x, norm1_g[0], w_in[0], b_f[0], w_dw[0], b_dw[0], conv_ln_g[0], conv_ln_b[0], out_g_conv[0],
        out_g_att[0], w_out[0], norm2_g[0], w_r1[0], b_r1[0], w_r2[0], b_r2[0], w_gate[0], w_up[0], w_down[0])
    out = _combine(dest, x1, final_g.reshape(1, d), ys, tm=min(512, batch * seq))
    return out.reshape(batch, seq, d)
```

```python
import functools

import jax
import jax.numpy as jnp
from jax import lax
from jax.experimental import pallas as pl
from jax.experimental.pallas import tpu as pltpu

D_MODEL = 1024
D_CONV = 512
N_HEADS = 8
HEAD_DIM = 64
D_ATT = N_HEADS * HEAD_DIM
CONV_WIDTH = 31
N_GROUPS = 4
EXPERTS_PER_GROUP = 8
N_EXPERTS = N_GROUPS * EXPERTS_PER_GROUP
TOP_K = 2
D_EXPERT = D_MODEL // 4
ROW_BLOCK = 256
EPS = 1e-6

LANES = 128
SUBLANES = 8
KAUG = 128
N_PIECES = 3
PIECE_STRIDE = 8
CONV_HALO = 32
PAIRS_PER_GROUP = EXPERTS_PER_GROUP * (EXPERTS_PER_GROUP - 1) // 2
N_CLASSES = N_GROUPS * PAIRS_PER_GROUP
ROW_W = D_MODEL + LANES
SKIP_LOG2 = 160.0
FIXED_REF_MAX_BOUND = 40.0
NORM_SLACK = 1.02
V_ROWS = 80
LOG2E = 1.4426950408889634
VMEM_LIMIT = 56 * 1024 * 1024

F32 = jnp.float32
BF16 = jnp.bfloat16


def _dot(a, b):
    return jnp.dot(a, b, preferred_element_type=F32)


def _dot_nt(a, b):
    return lax.dot_general(a, b, (((1,), (1,)), ((), ())), preferred_element_type=F32)


def _split3(x):
    hi = x.astype(BF16)
    r1 = x - hi.astype(F32)
    mid = r1.astype(BF16)
    lo = (r1 - mid.astype(F32)).astype(BF16)
    return hi.astype(F32), mid.astype(F32), lo.astype(F32)


def _piece_lane_mask(lane, h):
    return (lane == h) | (lane == h + PIECE_STRIDE) | (lane == h + 2 * PIECE_STRIDE)


def _inproj_kernel(x_ref, g1_ref, wvg_ref, wk_ref, wf_ref, bf_ref, wqT_ref, wvT_ref, ltri_ref, hsel_ref,
                   a_ref, kp_ref, qpT_ref, vT_ref, kstat_ref, qstat_ref, carry_ref, *, tm, tk):
    @pl.when(pl.program_id(1) == 0)
    def _():
        carry_ref[...] = jnp.zeros_like(carry_ref)

    x = x_ref[...]
    ms = jnp.mean(x * x, axis=-1, keepdims=True)
    hb = ((x * lax.rsqrt(ms + EPS)) * g1_ref[...]).astype(BF16)

    zvg = _dot(hb, wvg_ref[...])
    a_ref[...] = zvg[:, :D_CONV] * jax.nn.sigmoid(zvg[:, D_CONV:])

    kk = _dot(hb, wk_ref[...])

    zf = _dot(hb, wf_ref[...]) + bf_ref[...]
    lf = jnp.minimum(zf, 0.0) - jnp.log1p(jnp.exp(-jnp.abs(zf)))
    lane = lax.broadcasted_iota(jnp.int32, (tm, LANES), 1)
    hi, mid, lo = _split3(lf)
    lf3 = jnp.where(lane < PIECE_STRIDE, hi,
                    jnp.where(lane < 2 * PIECE_STRIDE, mid,
                              jnp.where(lane < 3 * PIECE_STRIDE, lo, 0.0))).astype(BF16)
    cs3 = _dot(ltri_ref[...], lf3)
    c = (cs3 + pltpu.roll(cs3, LANES - PIECE_STRIDE, 1)
         + pltpu.roll(cs3, LANES - 2 * PIECE_STRIDE, 1)) + carry_ref[...]
    carry_ref[...] = c[tm - 1:tm, :]

    nhi, nmid, nlo = _split3(c * (-LOG2E))
    p3 = jnp.where(lane < PIECE_STRIDE, nhi,
                   jnp.where(lane < 2 * PIECE_STRIDE, pltpu.roll(nmid, PIECE_STRIDE, 1),
                             jnp.where(lane < 3 * PIECE_STRIDE, pltpu.roll(nlo, 2 * PIECE_STRIDE, 1), 0.0)))
    p3_hi = pltpu.roll(p3, HEAD_DIM, 1)
    q_pieces = (-p3).T[0:HEAD_DIM]
    q_shift = N_PIECES * PIECE_STRIDE

    qT = _dot_nt(wqT_ref[...], hb) * LOG2E
    vT = _dot_nt(wvT_ref[...], hb)
    row = lax.broadcasted_iota(jnp.int32, (HEAD_DIM, tm), 0)
    vrow = lax.broadcasted_iota(jnp.int32, (V_ROWS - HEAD_DIM, tk), 0)
    v_tail = jnp.where(vrow == 0, 1.0, 0.0).astype(BF16)

    for h in range(N_HEADS):
        kcol = kk[:, (h // 2) * LANES:(h // 2 + 1) * LANES]
        f_h = jnp.where(_piece_lane_mask(row, h), q_pieces, 0.0)
        aug_q = jnp.where(_piece_lane_mask(row, h), 1.0, 0.0) + jnp.concatenate(
            [jnp.zeros((q_shift, tm), F32), f_h[0:HEAD_DIM - q_shift]], axis=0)
        q_h = qT[h * HEAD_DIM:(h + 1) * HEAD_DIM, :]
        if h % 2 == 0:
            ext = jnp.where(_piece_lane_mask(lane, h + HEAD_DIM), p3_hi,
                            jnp.where(_piece_lane_mask(lane, h + HEAD_DIM + q_shift), 1.0, 0.0))
            kp = jnp.where(lane < HEAD_DIM, kcol, ext)
            qp = jnp.concatenate([q_h, aug_q], axis=0)
        else:
            ext = jnp.where(_piece_lane_mask(lane, h), p3,
                            jnp.where(_piece_lane_mask(lane, h + q_shift), 1.0, 0.0))
            kp = jnp.where(lane >= HEAD_DIM, kcol, ext)
            qp = jnp.concatenate([aug_q, q_h], axis=0)
        qpT_ref[0, h] = qp.astype(BF16)
        for cidx in range(tm // tk):
            kp_ref[0, h, cidx] = kp[cidx * tk:(cidx + 1) * tk, :].astype(BF16)
            v_h = vT[h * HEAD_DIM:(h + 1) * HEAD_DIM, cidx * tk:(cidx + 1) * tk].astype(BF16)
            vT_ref[0, h, cidx] = jnp.concatenate([v_h, v_tail], axis=0)

    kn2 = jnp.max(_dot((kk * kk).astype(BF16), hsel_ref[...]), axis=0, keepdims=True)
    srow = lax.broadcasted_iota(jnp.int32, (SUBLANES, LANES), 0)
    kstat_ref[0, 0] = jnp.where(srow == 0, c[0:1, :], jnp.where(srow == 1, c[tm - 1:tm, :],
                                                                 jnp.where(srow == 2, kn2, 0.0)))
    q2 = qT * qT
    qn2 = jnp.concatenate([jnp.sum(q2[h * HEAD_DIM:(h + 1) * HEAD_DIM, :], axis=0, keepdims=True)
                           for h in range(N_HEADS)], axis=0)
    qstat_ref[0, 0] = jnp.broadcast_to(jnp.max(qn2, axis=1, keepdims=True), (N_HEADS, LANES))


def _inproj(x2, g1, wvg, wk, wf, bf3, wqT, wvT, *, batch, seq, tm, tk):
    nt = seq // tm
    nk = seq // tk
    ltri = jnp.tril(jnp.ones((tm, tm), F32)).astype(BF16)
    hsel = (jnp.arange(D_ATT)[:, None] // HEAD_DIM == jnp.arange(LANES)[None, :]).astype(BF16)
    const = lambda shape: pl.BlockSpec(shape, lambda b, t: (0,) * len(shape))
    stat_spec = pl.BlockSpec((1, 1, SUBLANES, LANES), lambda b, t: (b, t, 0, 0))
    stat_shape = jax.ShapeDtypeStruct((batch, nt, SUBLANES, LANES), F32)
    return pl.pallas_call(
        functools.partial(_inproj_kernel, tm=tm, tk=tk),
        grid=(batch, nt),
        in_specs=[
            pl.BlockSpec((tm, D_MODEL), lambda b, t: (b * nt + t, 0)),
            const((1, D_MODEL)), const((D_MODEL, 2 * D_CONV)), const((D_MODEL, D_ATT)),
            const((D_MODEL, LANES)), const((1, LANES)), const((D_ATT, D_MODEL)), const((D_ATT, D_MODEL)),
            const((tm, tm)), const((D_ATT, LANES)),
        ],
        out_specs=[
            pl.BlockSpec((tm, D_CONV), lambda b, t: (b * nt + t, 0)),
            pl.BlockSpec((1, N_HEADS, tm // tk, tk, KAUG), lambda b, t: (b, 0, t, 0, 0)),
            pl.BlockSpec((1, N_HEADS, KAUG, tm), lambda b, t: (b, 0, 0, t)),
            pl.BlockSpec((1, N_HEADS, tm // tk, V_ROWS, tk), lambda b, t: (b, 0, t, 0, 0)),
            stat_spec, stat_spec,
        ],
        out_shape=[
            jax.ShapeDtypeStruct((batch * seq, D_CONV), F32),
            jax.ShapeDtypeStruct((batch, N_HEADS, nk, tk, KAUG), BF16),
            jax.ShapeDtypeStruct((batch, N_HEADS, KAUG, seq), BF16),
            jax.ShapeDtypeStruct((batch, N_HEADS, nk, V_ROWS, tk), BF16),
            stat_shape, stat_shape,
        ],
        scratch_shapes=[pltpu.VMEM((1, LANES), F32)],
        compiler_params=pltpu.CompilerParams(
            dimension_semantics=("arbitrary", "arbitrary"), vmem_limit_bytes=VMEM_LIMIT),
        name="inproj",
    )(x2, g1, wvg, wk, wf, bf3, wqT, wvT, ltri, hsel)


def _conv_tile(a_ref, w_ref, b_ref, lng_ref, lnb_ref, og_ref, sh_ref, acc_ref, *, tc, rc):
    n_cb = D_CONV // LANES

    @pl.when(pl.program_id(1) == 0)
    def _():
        for cb in range(n_cb):
            sh_ref[0, cb, 0:CONV_HALO, :] = jnp.zeros((CONV_HALO, LANES), F32)

    @pl.when(pl.program_id(1) > 0)
    def _():
        for cb in range(n_cb):
            sh_ref[0, cb, 0:CONV_HALO, :] = sh_ref[0, cb, tc:tc + CONV_HALO, :]

    n_sh = tc + CONV_HALO - SUBLANES
    for cb in range(n_cb):
        sh_ref[0, cb, CONV_HALO:CONV_HALO + tc, :] = a_ref[:, cb * LANES:(cb + 1) * LANES]
        for f in range(1, SUBLANES):
            sh_ref[f, cb, 0:n_sh, :] = sh_ref[0, cb, f:f + n_sh, :]

    base = CONV_HALO - (CONV_WIDTH - 1)
    for cb in range(n_cb):
        cols = slice(cb * LANES, (cb + 1) * LANES)

        def chunk(c, carry, cb=cb, cols=cols):
            r0 = pl.multiple_of(c * rc, rc)
            acc = jnp.zeros((rc, LANES), F32)
            for j in range(CONV_WIDTH):
                f = (base + j) % SUBLANES
                acc = acc + w_ref[j:j + 1, cols] * sh_ref[f, cb, pl.ds(r0 + (base + j - f), rc), :]
            acc_ref[pl.ds(r0, rc), cols] = acc
            return carry

        lax.fori_loop(0, tc // rc, chunk, 0)

    y = acc_ref[...] + b_ref[...]
    mu = jnp.mean(y, axis=-1, keepdims=True)
    yc = y - mu
    var = jnp.mean(yc * yc, axis=-1, keepdims=True)
    yn = yc * lax.rsqrt(var + EPS) * lng_ref[...] + lnb_ref[...]
    s = yn * jax.nn.sigmoid(yn)
    ms = jnp.mean(s * s, axis=-1, keepdims=True)
    return (s * lax.rsqrt(ms + EPS) * og_ref[...]).astype(BF16)


def _attn_fixed_ref_kernel(g0_ref, bnd_ref, qT_ref, k_ref, v_ref, o_ref, *, tq, tk):
    i = pl.program_id(2)
    qT = qT_ref[0, 0]
    n_sub = tq // tk
    bnd = bnd_ref[0, 0, 0:1, 0:1]
    g0 = g0_ref[(pl.program_id(0) * N_HEADS + pl.program_id(1)) * pl.num_programs(2) + i]

    def values(g):
        return jnp.concatenate([v_ref[0, 0, g * n_sub + d] for d in range(n_sub)], axis=1)

    def probs(g):
        return jnp.concatenate([jnp.exp2(_dot(k_ref[0, 0, g * n_sub + d], qT) - bnd).astype(BF16)
                                for d in range(n_sub)], axis=0)

    def two_groups(p, acc):
        g = g0 + 2 * p
        acc = acc + _dot(values(g), probs(g))
        return acc + _dot(values(g + 1), probs(g + 1))

    key = lax.broadcasted_iota(jnp.int32, (tk, tk), 0)
    qry = lax.broadcasted_iota(jnp.int32, (tk, tk), 1)

    def diagonal(acc):
        rows = []
        for d in range(n_sub):
            s = _dot(k_ref[0, 0, i * n_sub + d], qT[:, d * tk:])
            tiles = [jnp.zeros((tk, d * tk), BF16)] if d else []
            tiles.append(jnp.exp2(jnp.where(key <= qry, s[:, :tk], -jnp.inf) - bnd).astype(BF16))
            if d + 1 < n_sub:
                tiles.append(jnp.exp2(s[:, tk:] - bnd).astype(BF16))
            rows.append(jnp.concatenate(tiles, axis=1))
        acc = acc + _dot(values(i), jnp.concatenate(rows, axis=0))
        o_ref[0] = acc[:HEAD_DIM] / acc[HEAD_DIM:HEAD_DIM + 1]

    n_full = i - g0
    acc = lax.fori_loop(0, n_full // 2, two_groups, jnp.zeros((V_ROWS, tq), F32))

    @pl.when(n_full % 2 == 1)
    def _():
        diagonal(acc + _dot(values(i - 1), probs(i - 1)))

    @pl.when(n_full % 2 == 0)
    def _():
        diagonal(acc)


def _attn_kernel(g0_ref, qT_ref, k_ref, v_ref, o_ref, sa_ref, sb_ref, cma_ref, cmb_ref, *, tq, tk):
    i = pl.program_id(2)
    qT = qT_ref[0, 0]
    n_sub = tq // tk

    def scores(g, s_ref, cm_ref):
        for d in range(n_sub):
            s = _dot(k_ref[0, 0, g * n_sub + d], qT)
            s_ref[d] = s
            cm_ref[d] = jnp.max(s, axis=0, keepdims=True)

    def consume(g, s_ref, cm_ref, carry, nxt=None):
        m, acc = carry
        m_new = m
        for d in range(n_sub):
            m_new = jnp.maximum(m_new, cm_ref[d])
        ps = []
        for d in range(n_sub):
            if nxt is not None:
                g_n, sn_ref, cmn_ref = nxt
                s = _dot(k_ref[0, 0, g_n * n_sub + d], qT)
                sn_ref[d] = s
                cmn_ref[d] = jnp.max(s, axis=0, keepdims=True)
            ps.append(jnp.exp2(s_ref[d] - m_new).astype(BF16))
        p = jnp.concatenate(ps, axis=0)
        v = jnp.concatenate([v_ref[0, 0, g * n_sub + d] for d in range(n_sub)], axis=1)
        return m_new, jnp.exp2(m - m_new) * acc + _dot(v, p)

    key = lax.broadcasted_iota(jnp.int32, (tk, tk), 0)
    qry = lax.broadcasted_iota(jnp.int32, (tk, tk), 1)

    def consume_diag(s_ref, cm_ref, carry):
        m, acc = carry
        lanes = lambda j: slice(j * tk, (j + 1) * tk)
        causal = [jnp.where(key <= qry, s_ref[d, :, lanes(d)], -jnp.inf) for d in range(n_sub)]
        m_tiles = []
        for j in range(n_sub):
            m_j = jnp.maximum(m[:, lanes(j)], jnp.max(causal[j], axis=0, keepdims=True))
            for d in range(j):
                m_j = jnp.maximum(m_j, cm_ref[d, :, lanes(j)])
            m_tiles.append(m_j)
        m_new = jnp.concatenate(m_tiles, axis=1)
        rows = []
        for d in range(n_sub):
            tiles = [jnp.zeros((tk, d * tk), BF16)] if d else []
            tiles.append(jnp.exp2(causal[d] - m_tiles[d]).astype(BF16))
            if d + 1 < n_sub:
                rest = slice((d + 1) * tk, tq)
                tiles.append(jnp.exp2(s_ref[d, :, rest] - m_new[:, rest]).astype(BF16))
            rows.append(jnp.concatenate(tiles, axis=1))
        p = jnp.concatenate(rows, axis=0)
        v = jnp.concatenate([v_ref[0, 0, i * n_sub + d] for d in range(n_sub)], axis=1)
        acc = jnp.exp2(m - m_new) * acc + _dot(v, p)
        o_ref[0] = acc[:HEAD_DIM] / acc[HEAD_DIM:HEAD_DIM + 1]

    g0 = g0_ref[(pl.program_id(0) * N_HEADS + pl.program_id(1)) * pl.num_programs(2) + i]
    n_full = i - g0

    def pair(p, carry):
        g = g0 + 2 * p
        carry = consume(g, sa_ref, cma_ref, carry, nxt=(g + 1, sb_ref, cmb_ref))
        return consume(g + 1, sb_ref, cmb_ref, carry, nxt=(g + 2, sa_ref, cma_ref))

    scores(g0, sa_ref, cma_ref)
    init = (jnp.full((1, tq), -jnp.inf, F32), jnp.zeros((V_ROWS, tq), F32))
    carry = lax.fori_loop(0, n_full // 2, pair, init)

    @pl.when(n_full % 2 == 1)
    def _():
        consume_diag(sb_ref, cmb_ref, consume(i - 1, sa_ref, cma_ref, carry, nxt=(i, sb_ref, cmb_ref)))

    @pl.when(n_full % 2 == 0)
    def _():
        consume_diag(sa_ref, cma_ref, carry)


def _first_groups(kstat, qstat, *, tm, tq):
    per = tq // tm
    bound = jnp.sqrt(jnp.max(kstat[:, :, 2, :N_HEADS], axis=1) * jnp.max(qstat[:, :, :, 0], axis=1))
    bound = bound * NORM_SLACK
    cq = kstat[:, ::per, 0, :N_HEADS]
    ck = kstat[:, per - 1::per, 1, :N_HEADS]
    d = (cq[:, :, None, :] - ck[:, None, :, :]) * LOG2E
    nq = cq.shape[1]
    earlier = jnp.arange(nq)[None, :, None, None] > jnp.arange(nq)[None, None, :, None]
    skip = earlier & (2.0 * bound[:, None, None, :] + d < -SKIP_LOG2)
    g0 = jnp.sum(jnp.cumprod(skip.astype(jnp.int32), axis=2), axis=2)
    return jnp.transpose(g0, (0, 2, 1)).reshape(-1).astype(jnp.int32), bound


def _attention(g0, bound, qpT, kp, vT, *, batch, seq, tq, tk):
    nk = seq // tk
    params = pltpu.CompilerParams(
        dimension_semantics=("arbitrary", "arbitrary", "arbitrary"), vmem_limit_bytes=VMEM_LIMIT)
    out_shape = jax.ShapeDtypeStruct((batch, D_ATT, seq), F32)
    qkv_specs = [
        pl.BlockSpec((1, 1, KAUG, tq), lambda b, h, i, g0: (b, h, 0, i)),
        pl.BlockSpec((1, 1, nk, tk, KAUG), lambda b, h, i, g0: (b, h, 0, 0, 0)),
        pl.BlockSpec((1, 1, nk, V_ROWS, tk), lambda b, h, i, g0: (b, h, 0, 0, 0)),
    ]
    out_spec = pl.BlockSpec((1, HEAD_DIM, tq), lambda b, h, i, g0: (b, h, i))

    def fixed_ref(g0, bnd, qpT, kp, vT):
        grid_spec = pltpu.PrefetchScalarGridSpec(
            num_scalar_prefetch=1, grid=(batch, N_HEADS, seq // tq),
            in_specs=[pl.BlockSpec((1, 1, SUBLANES, LANES), lambda b, h, i, g0: (b, h, 0, 0))] + qkv_specs,
            out_specs=out_spec)
        return pl.pallas_call(functools.partial(_attn_fixed_ref_kernel, tq=tq, tk=tk), grid_spec=grid_spec,
                              out_shape=out_shape, compiler_params=params, name="attention_fixed_ref",
                              )(g0, bnd, qpT, kp, vT)

    def running_max(g0, bnd, qpT, kp, vT):
        del bnd
        grid_spec = pltpu.PrefetchScalarGridSpec(
            num_scalar_prefetch=1, grid=(batch, N_HEADS, seq // tq), in_specs=qkv_specs, out_specs=out_spec,
            scratch_shapes=[pltpu.VMEM((tq // tk, tk, tq), F32), pltpu.VMEM((tq // tk, tk, tq), F32),
                            pltpu.VMEM((tq // tk, 1, tq), F32), pltpu.VMEM((tq // tk, 1, tq), F32)])
        return pl.pallas_call(functools.partial(_attn_kernel, tq=tq, tk=tk), grid_spec=grid_spec,
                              out_shape=out_shape, compiler_params=params, name="attention",
                              )(g0, qpT, kp, vT)

    bnd = jnp.broadcast_to(bound[:, :, None, None], (batch, N_HEADS, SUBLANES, LANES))
    return lax.cond(jnp.max(bound) <= FIXED_REF_MAX_BOUND, fixed_ref, running_max, g0, bnd, qpT, kp, vT)


def _outproj_kernel(x_ref, a_ref, yT_ref, wdw_ref, bdw_ref, lng_ref, lnb_ref, gc_ref, woc_ref, woa_ref, ga_ref,
                    g2_ref, wrT_ref, brT_ref, utri_ref, x1_ref, h2_ref, route_ref, cnt_ref,
                    carry_ref, sh_ref, acc_ref, *, tm, rc):
    first = (pl.program_id(0) == 0) & (pl.program_id(1) == 0)

    @pl.when(first)
    def _():
        carry_ref[...] = jnp.zeros_like(carry_ref)

    mc = _conv_tile(a_ref, wdw_ref, bdw_ref, lng_ref, lnb_ref, gc_ref, sh_ref, acc_ref, tc=tm, rc=rc)
    yT = yT_ref[0]
    msa = jnp.mean(yT * yT, axis=0, keepdims=True)
    yn = (yT * lax.rsqrt(msa + EPS)).T * ga_ref[...]
    x1 = x_ref[...] + _dot(mc, woc_ref[...]) + _dot(yn.astype(BF16), woa_ref[...])
    x1_ref[...] = x1
    ms = jnp.mean(x1 * x1, axis=-1, keepdims=True)
    h2 = (x1 * lax.rsqrt(ms + EPS)) * g2_ref[...]
    h2_ref[:, :D_MODEL] = h2

    lg = _dot_nt(wrT_ref[...], h2.astype(BF16)) + brT_ref[...]
    neg = -jnp.inf
    sub = lax.broadcasted_iota(jnp.int32, (SUBLANES, tm), 0)
    big = jnp.int32(SUBLANES)
    first_of = lambda hit: jnp.min(jnp.where(hit, sub, big), axis=0, keepdims=True)

    lg1 = lg[0:SUBLANES]
    m1 = jnp.max(lg1, axis=0, keepdims=True)
    p1_sel = 1.0 / jnp.sum(jnp.exp(lg1 - m1), axis=0, keepdims=True)
    grp = first_of(lg1 == m1)

    slab = lambda g: lg[SUBLANES * (g + 1):SUBLANES * (g + 2)]
    v = slab(N_GROUPS - 1)
    for g in range(N_GROUPS - 2, -1, -1):
        v = jnp.where(grp == g, slab(g), v)
    v1 = jnp.max(v, axis=0, keepdims=True)
    j1 = first_of(v == v1)
    vv = jnp.where(sub == j1, neg, v)
    v2 = jnp.max(vv, axis=0, keepdims=True)
    j2 = first_of(vv == v2)
    e21 = jnp.exp(v2 - v1)
    w0 = p1_sel / (1.0 + e21)
    w1 = p1_sel * e21 / (1.0 + e21)

    swap = j2 < j1
    al = jnp.where(swap, j2, j1)
    bl = jnp.where(swap, j1, j2)
    wa = jnp.where(swap, w1, w0)
    wb = jnp.where(swap, w0, w1)
    cid = PAIRS_PER_GROUP * grp + ((al * (2 * EXPERTS_PER_GROUP - 1 - al)) >> 1) + (bl - al - 1)

    cls = lax.broadcasted_iota(jnp.int32, (LANES, tm), 0)
    oh = cls == cid
    cmat = jnp.where(oh, 1.0, 0.0)
    prefix = _dot(cmat.astype(BF16), utri_ref[...]) + carry_ref[...]
    rank = jnp.sum(jnp.where(oh, prefix, 0.0), axis=0, keepdims=True)
    counts = prefix[:, tm - 1:tm] + cmat[:, tm - 1:tm]
    carry_ref[...] = counts
    cnt_ref[...] = counts

    h2_ref[:, D_MODEL:] = jnp.where(cls == 0, wa, jnp.where(cls == 1, wb, 0.0)).T
    route_ref[...] = jnp.where(sub == 0, cid.astype(F32), jnp.where(sub == 1, rank, 0.0))


def _outproj(x2, a, yT, conv_params, woc, woa, ga, g2, wrT, brT, *, batch, seq, tm, rc):
    nt = seq // tm
    utri = jnp.triu(jnp.ones((tm, tm), F32), 1).astype(BF16)
    const = lambda shape: pl.BlockSpec(shape, lambda b, t: (0,) * len(shape))
    row_spec = lambda w: pl.BlockSpec((tm, w), lambda b, t: (b * nt + t, 0))
    return pl.pallas_call(
        functools.partial(_outproj_kernel, tm=tm, rc=rc),
        grid=(batch, nt),
        in_specs=[row_spec(D_MODEL), row_spec(D_CONV),
                  pl.BlockSpec((1, D_ATT, tm), lambda b, t: (b, 0, t)),
                  const((CONV_WIDTH, D_CONV)), const((1, D_CONV)), const((1, D_CONV)), const((1, D_CONV)),
                  const((1, D_CONV)),
                  const((D_CONV, D_MODEL)), const((D_ATT, D_MODEL)), const((1, D_ATT)), const((1, D_MODEL)),
                  const((LANES, D_MODEL)), const((LANES, 1)), const((tm, tm))],
        out_specs=[row_spec(D_MODEL), row_spec(ROW_W),
                   pl.BlockSpec((SUBLANES, tm), lambda b, t: (0, b * nt + t)), const((LANES, 1))],
        out_shape=[jax.ShapeDtypeStruct((batch * seq, D_MODEL), F32),
                   jax.ShapeDtypeStruct((batch * seq, ROW_W), F32),
                   jax.ShapeDtypeStruct((SUBLANES, batch * seq), F32),
                   jax.ShapeDtypeStruct((LANES, 1), F32)],
        scratch_shapes=[pltpu.VMEM((LANES, 1), F32),
                        pltpu.VMEM((SUBLANES, D_CONV // LANES, tm + CONV_HALO, LANES), F32),
                        pltpu.VMEM((tm, D_CONV), F32)],
        compiler_params=pltpu.CompilerParams(
            dimension_semantics=("arbitrary", "arbitrary"), vmem_limit_bytes=VMEM_LIMIT),
        name="outproj",
    )(x2, a, yT, *conv_params, woc, woa, ga, g2, wrT, brT, utri)


def _row_copies(src_at, dst_at, sem):
    return pltpu.make_async_copy(src_at, dst_at, sem)


def _dispatch_kernel(dest_ref, h2_ref, xs_ref, sem, *, tm):
    def body(g, carry):
        for u in range(SUBLANES):
            _row_copies(h2_ref.at[g, pl.ds(u, 1)], xs_ref.at[pl.ds(dest_ref[g * SUBLANES + u], 1)],
                        sem).start(priority=u % 2)
        return carry

    lax.fori_loop(0, tm // SUBLANES, body, 0)
    done = xs_ref.at[pl.ds(0, tm)]
    _row_copies(done, done, sem).wait()


def _dispatch(dest, h2, *, tm):
    t_len = h2.shape[0]
    return pl.pallas_call(
        functools.partial(_dispatch_kernel, tm=tm),
        grid=(t_len // tm,),
        in_specs=[pl.BlockSpec((tm,), lambda i: (i,), memory_space=pltpu.SMEM),
                  pl.BlockSpec((tm // SUBLANES, SUBLANES, ROW_W), lambda i: (i, 0, 0))],
        out_specs=pl.BlockSpec(memory_space=pl.ANY),
        out_shape=jax.ShapeDtypeStruct((t_len, ROW_W), F32),
        scratch_shapes=[pltpu.SemaphoreType.DMA(())],
        compiler_params=pltpu.CompilerParams(dimension_semantics=("arbitrary",), vmem_limit_bytes=VMEM_LIMIT),
        name="dispatch",
    )(dest, h2.reshape(t_len // SUBLANES, SUBLANES, ROW_W))


def _expert_kernel(blk_ref, ea_ref, eb_ref, lo_ref, hi_ref, x_ref, wgua_ref, wda_ref, wgub_ref, wdb_ref, y_ref):
    del blk_ref, ea_ref, eb_ref
    i = pl.program_id(0)
    lo = lo_ref[i]
    hi = hi_ref[i]

    def mlp(xb, wgu_ref, wd_ref):
        gu = _dot(xb, wgu_ref[0])
        g = gu[:, :D_EXPERT]
        return _dot(((g * jax.nn.sigmoid(g)) * gu[:, D_EXPERT:]).astype(BF16), wd_ref[0])

    @pl.when(hi > lo)
    def _():
        xb = x_ref[:, :D_MODEL].astype(BF16)
        wts = x_ref[:, D_MODEL:]
        y = wts[:, 0:1] * mlp(xb, wgua_ref, wda_ref) + wts[:, 1:2] * mlp(xb, wgub_ref, wdb_ref)
        row = lax.broadcasted_iota(jnp.int32, (ROW_BLOCK, 1), 0)
        mine = (row >= lo) & (row < hi)

        @pl.when(lo == 0)
        def _():
            y_ref[...] = jnp.where(mine, y, 0.0)

        @pl.when(lo > 0)
        def _():
            y_ref[...] = jnp.where(mine, y, y_ref[...])


def _experts(items, xs, wgu, wd):
    n_rows = xs.shape[0]
    n_items = items[0].shape[0]
    wspec = lambda shape, which: pl.BlockSpec(
        shape, lambda i, blk, ea, eb, lo, hi: ((ea, eb)[which][i], 0, 0))
    grid_spec = pltpu.PrefetchScalarGridSpec(
        num_scalar_prefetch=5,
        grid=(n_items,),
        in_specs=[pl.BlockSpec((ROW_BLOCK, ROW_W), lambda i, blk, ea, eb, lo, hi: (blk[i], 0)),
                  wspec((1, D_MODEL, 2 * D_EXPERT), 0), wspec((1, D_EXPERT, D_MODEL), 0),
                  wspec((1, D_MODEL, 2 * D_EXPERT), 1), wspec((1, D_EXPERT, D_MODEL), 1)],
        out_specs=pl.BlockSpec((ROW_BLOCK, D_MODEL), lambda i, blk, ea, eb, lo, hi: (blk[i], 0)),
    )
    return pl.pallas_call(
        _expert_kernel,
        grid_spec=grid_spec,
        out_shape=jax.ShapeDtypeStruct((n_rows, D_MODEL), F32),
        compiler_params=pltpu.CompilerParams(dimension_semantics=("arbitrary",), vmem_limit_bytes=VMEM_LIMIT),
        name="experts",
    )(*items, xs, wgu, wd, wgu, wd)


def _combine_kernel(dcur_ref, dnxt_ref, x1_ref, g_ref, ys_ref, out_ref, buf_ref, sems, *, tm):
    i = pl.program_id(0)
    slot = i % 2

    def issue(d_ref, s):
        def body(g, carry):
            for u in range(SUBLANES):
                _row_copies(ys_ref.at[pl.ds(d_ref[g * SUBLANES + u], 1)], buf_ref.at[s, g, pl.ds(u, 1)],
                            sems.at[s]).start(priority=u % 2)
            return carry

        lax.fori_loop(0, tm // SUBLANES, body, 0)

    @pl.when(i == 0)
    def _():
        issue(dcur_ref, 0)

    @pl.when(i + 1 < pl.num_programs(0))
    def _():
        issue(dnxt_ref, 1 - slot)

    _row_copies(buf_ref.at[slot], buf_ref.at[slot], sems.at[slot]).wait()
    x2 = x1_ref[...] + buf_ref[slot]
    ms = jnp.mean(x2 * x2, axis=-1, keepdims=True)
    out_ref[...] = (x2 * lax.rsqrt(ms + EPS)) * g_ref[...]


def _combine(dest, x1, gf, ys, *, tm):
    t_len = x1.shape[0]
    n = t_len // tm
    return pl.pallas_call(
        functools.partial(_combine_kernel, tm=tm),
        grid=(n,),
        in_specs=[pl.BlockSpec((tm,), lambda i: (i,), memory_space=pltpu.SMEM),
                  pl.BlockSpec((tm,), lambda i: (jnp.minimum(i + 1, n - 1),), memory_space=pltpu.SMEM),
                  pl.BlockSpec((tm // SUBLANES, SUBLANES, D_MODEL), lambda i: (i, 0, 0)),
                  pl.BlockSpec((1, D_MODEL), lambda i: (0, 0)),
                  pl.BlockSpec(memory_space=pl.ANY)],
        out_specs=pl.BlockSpec((tm // SUBLANES, SUBLANES, D_MODEL), lambda i: (i, 0, 0)),
        out_shape=jax.ShapeDtypeStruct((t_len // SUBLANES, SUBLANES, D_MODEL), F32),
        scratch_shapes=[pltpu.VMEM((2, tm // SUBLANES, SUBLANES, D_MODEL), F32), pltpu.SemaphoreType.DMA((2,))],
        compiler_params=pltpu.CompilerParams(dimension_semantics=("arbitrary",), vmem_limit_bytes=VMEM_LIMIT),
        name="combine",
    )(dest, dest, x1.reshape(t_len // SUBLANES, SUBLANES, D_MODEL), gf, ys)


def _layer(x, norm1_g, w_in, b_f, w_dw, b_dw, conv_ln_g, conv_ln_b, out_g_conv, out_g_att, w_out,
           norm2_g, w_r1, b_r1, w_r2, b_r2, w_gate, w_up, w_down):
    batch, seq, d = x.shape
    t_len = batch * seq
    tm = min(512, seq)
    tk = min(256, seq)
    tq = min(1024, seq)
    o1, o2 = D_CONV, 2 * D_CONV
    o3, o4, o5 = o2 + D_ATT, o2 + 2 * D_ATT, o2 + 3 * D_ATT

    wvg = w_in[:, :o2].astype(BF16)
    wk = w_in[:, o3:o4].astype(BF16)
    wqT = (w_in[:, o2:o3] * (HEAD_DIM ** -0.5)).T.astype(BF16)
    wvT = w_in[:, o4:o5].T.astype(BF16)
    assert PIECE_STRIDE == N_HEADS
    pad = LANES - N_PIECES * PIECE_STRIDE
    wf = jnp.pad(jnp.tile(w_in[:, o5:], (1, N_PIECES)), ((0, 0), (0, pad))).astype(BF16)
    bf3 = jnp.pad(jnp.tile(b_f.astype(F32).reshape(1, N_HEADS), (1, N_PIECES)), ((0, 0), (0, pad)))

    x2 = x.reshape(t_len, d)
    a, kp, qpT, vT, kstat, qstat = _inproj(x2, norm1_g.reshape(1, d), wvg, wk, wf, bf3, wqT, wvT,
                                           batch=batch, seq=seq, tm=tm, tk=tk)
    conv_params = (w_dw, b_dw.reshape(1, -1), conv_ln_g.reshape(1, -1), conv_ln_b.reshape(1, -1),
                   out_g_conv.reshape(1, -1))
    g0, bound = _first_groups(kstat, qstat, tm=tm, tq=tq)
    yT = _attention(g0, bound, qpT, kp, vT, batch=batch, seq=seq, tq=tq, tk=tk)

    gpad = SUBLANES - N_GROUPS
    rpad = LANES - SUBLANES - N_EXPERTS
    wrT = jnp.concatenate([w_r1.T, jnp.zeros((gpad, d), F32),
                           jnp.transpose(w_r2, (0, 2, 1)).reshape(N_EXPERTS, d), jnp.zeros((rpad, d), F32)], axis=0)
    brT = jnp.concatenate([b_r1.astype(F32), jnp.full((gpad,), -jnp.inf, F32), b_r2.reshape(-1).astype(F32),
                           jnp.zeros((rpad,), F32)]).reshape(LANES, 1)
    x1, h2, route, cnt = _outproj(x2, a, yT, conv_params, w_out[:D_CONV].astype(BF16),
                                  w_out[D_CONV:].astype(BF16), out_g_att.reshape(1, -1), norm2_g.reshape(1, d),
                                  wrT.astype(BF16), brT, batch=batch, seq=seq, tm=tm, rc=min(128, seq))

    i32 = jnp.int32
    lanes = jnp.arange(LANES, dtype=i32)
    pick = lambda table, idx: jnp.sum(jnp.where(idx[:, None] == lanes, table[None, :], 0), axis=1).astype(i32)
    counts = cnt[:, 0].astype(i32)
    ends = jnp.cumsum(counts).astype(i32)
    starts = ends - counts
    dest = pick(starts, route[0].astype(i32)) + route[1].astype(i32)

    n_blocks = t_len // ROW_BLOCK
    b_lo = starts // ROW_BLOCK
    n_it = jnp.where(counts > 0, (ends - 1) // ROW_BLOCK - b_lo + 1, 0)
    it_end = jnp.cumsum(n_it).astype(i32)
    it_start = it_end - n_it
    idx = jnp.arange(n_blocks + N_CLASSES, dtype=i32)
    valid = idx < it_end[-1]
    last_cls = jnp.max(jnp.where(counts > 0, lanes, 0))
    cls = jnp.where(valid, jnp.sum(it_end[None, :] <= idx[:, None], axis=1).astype(i32), last_cls)
    blk = jnp.where(valid, pick(b_lo, cls) + idx - pick(it_start, cls), n_blocks - 1)
    row0 = blk * ROW_BLOCK
    lo = jnp.where(valid, jnp.maximum(pick(starts, cls), row0) - row0, 0)
    hi = jnp.where(valid, jnp.minimum(pick(ends, cls), row0 + ROW_BLOCK) - row0, 0)
    pair_a, pair_b = [], []
    for g in range(N_GROUPS):
        for a_loc in range(EXPERTS_PER_GROUP):
            for b_loc in range(a_loc + 1, EXPERTS_PER_GROUP):
                pair_a.append(g * EXPERTS_PER_GROUP + a_loc)
                pair_b.append(g * EXPERTS_PER_GROUP + b_loc)
    cpad = [0] * (LANES - N_CLASSES)
    items = (blk.astype(i32), pick(jnp.array(pair_a + cpad, i32), cls), pick(jnp.array(pair_b + cpad, i32), cls),
             lo.astype(i32), hi.astype(i32))

    xs = _dispatch(dest, h2, tm=min(1024, t_len))
    ys = _experts(items, xs, jnp.concatenate([w_gate, w_up], axis=-1).astype(BF16), w_down.astype(BF16))
    return dest, x1, ys


def kernel(x, norm1_g, w_in, b_f, w_dw, b_dw, conv_ln_g, conv_ln_b, out_g_conv, out_g_att, w_out, norm2_g,
           w_r1, b_r1, w_r2, b_r2, w_gate, w_up, w_down, final_g):
    assert norm1_g.shape[0] == 1, "single-layer stack"
    batch, seq, d = x.shape
    dest, x1, ys = _layer(
        x, norm1_g[0], w_in[0], b_f[0], w_dw[0], b_dw[0], conv_ln_g[0], conv_ln_b[0], out_g_conv[0],
        out_g_att[0], w_out[0], norm2_g[0], w_r1[0], b_r1[0], w_r2[0], b_r2[0], w_gate[0], w_up[0], w_down[0])
    out = _combine(dest, x1, final_g.reshape(1, d), ys, tm=min(512, batch * seq))
    return out.reshape(batch, seq, d)
```

```python
import functools

import jax
import jax.numpy as jnp
from jax import lax
from jax.experimental import pallas as pl
from jax.experimental.pallas import tpu as pltpu

D_MODEL = 1024
D_CONV = 512
N_HEADS = 8
HEAD_DIM = 64
D_ATT = N_HEADS * HEAD_DIM
CONV_WIDTH = 31
N_GROUPS = 4
EXPERTS_PER_GROUP = 8
N_EXPERTS = N_GROUPS * EXPERTS_PER_GROUP
TOP_K = 2
D_EXPERT = D_MODEL // 4
ROW_BLOCK = 256
EPS = 1e-6

LANES = 128
SUBLANES = 8
KAUG = 128
N_PIECES = 3
PIECE_STRIDE = 8
CONV_HALO = 32
PAIRS_PER_GROUP = EXPERTS_PER_GROUP * (EXPERTS_PER_GROUP - 1) // 2
N_CLASSES = N_GROUPS * PAIRS_PER_GROUP
ROW_W = D_MODEL + LANES
SKIP_LOG2 = 160.0
FIXED_REF_MAX_BOUND = 40.0
NORM_SLACK = 1.02
V_ROWS = 80
LOG2E = 1.4426950408889634
VMEM_LIMIT = 56 * 1024 * 1024

F32 = jnp.float32
BF16 = jnp.bfloat16


def _dot(a, b):
    return jnp.dot(a, b, preferred_element_type=F32)


def _dot_nt(a, b):
    return lax.dot_general(a, b, (((1,), (1,)), ((), ())), preferred_element_type=F32)


def _split3(x):
    hi = x.astype(BF16)
    r1 = x - hi.astype(F32)
    mid = r1.astype(BF16)
    lo = (r1 - mid.astype(F32)).astype(BF16)
    return hi.astype(F32), mid.astype(F32), lo.astype(F32)


def _piece_lane_mask(lane, h):
    return (lane == h) | (lane == h + PIECE_STRIDE) | (lane == h + 2 * PIECE_STRIDE)


def _inproj_kernel(x_ref, g1_ref, wvg_ref, wk_ref, wf_ref, bf_ref, wqT_ref, wvT_ref, ltri_ref, hsel_ref,
                   a_ref, kp_ref, qpT_ref, vT_ref, kstat_ref, qstat_ref, carry_ref, *, tm, tk):
    @pl.when(pl.program_id(1) == 0)
    def _():
        carry_ref[...] = jnp.zeros_like(carry_ref)

    x = x_ref[...]
    ms = jnp.mean(x * x, axis=-1, keepdims=True)
    hb = ((x * lax.rsqrt(ms + EPS)) * g1_ref[...]).astype(BF16)

    zvg = _dot(hb, wvg_ref[...])
    a_ref[...] = zvg[:, :D_CONV] * jax.nn.sigmoid(zvg[:, D_CONV:])

    kk = _dot(hb, wk_ref[...])

    zf = _dot(hb, wf_ref[...]) + bf_ref[...]
    lf = jnp.minimum(zf, 0.0) - jnp.log1p(jnp.exp(-jnp.abs(zf)))
    lane = lax.broadcasted_iota(jnp.int32, (tm, LANES), 1)
    hi, mid, lo = _split3(lf)
    lf3 = jnp.where(lane < PIECE_STRIDE, hi,
                    jnp.where(lane < 2 * PIECE_STRIDE, mid,
                              jnp.where(lane < 3 * PIECE_STRIDE, lo, 0.0))).astype(BF16)
    cs3 = _dot(ltri_ref[...], lf3)
    c = (cs3 + pltpu.roll(cs3, LANES - PIECE_STRIDE, 1)
         + pltpu.roll(cs3, LANES - 2 * PIECE_STRIDE, 1)) + carry_ref[...]
    carry_ref[...] = c[tm - 1:tm, :]

    nhi, nmid, nlo = _split3(c * (-LOG2E))
    p3 = jnp.where(lane < PIECE_STRIDE, nhi,
                   jnp.where(lane < 2 * PIECE_STRIDE, pltpu.roll(nmid, PIECE_STRIDE, 1),
                             jnp.where(lane < 3 * PIECE_STRIDE, pltpu.roll(nlo, 2 * PIECE_STRIDE, 1), 0.0)))
    p3_hi = pltpu.roll(p3, HEAD_DIM, 1)
    q_pieces = (-p3).T[0:HEAD_DIM]
    q_shift = N_PIECES * PIECE_STRIDE

    qT = _dot_nt(wqT_ref[...], hb) * LOG2E
    vT = _dot_nt(wvT_ref[...], hb)
    row = lax.broadcasted_iota(jnp.int32, (HEAD_DIM, tm), 0)
    vrow = lax.broadcasted_iota(jnp.int32, (V_ROWS - HEAD_DIM, tk), 0)
    v_tail = jnp.where(vrow == 0, 1.0, 0.0).astype(BF16)

    for h in range(N_HEADS):
        kcol = kk[:, (h // 2) * LANES:(h // 2 + 1) * LANES]
        f_h = jnp.where(_piece_lane_mask(row, h), q_pieces, 0.0)
        aug_q = jnp.where(_piece_lane_mask(row, h), 1.0, 0.0) + jnp.concatenate(
            [jnp.zeros((q_shift, tm), F32), f_h[0:HEAD_DIM - q_shift]], axis=0)
        q_h = qT[h * HEAD_DIM:(h + 1) * HEAD_DIM, :]
        if h % 2 == 0:
            ext = jnp.where(_piece_lane_mask(lane, h + HEAD_DIM), p3_hi,
                            jnp.where(_piece_lane_mask(lane, h + HEAD_DIM + q_shift), 1.0, 0.0))
            kp = jnp.where(lane < HEAD_DIM, kcol, ext)
            qp = jnp.concatenate([q_h, aug_q], axis=0)
        else:
            ext = jnp.where(_piece_lane_mask(lane, h), p3,
                            jnp.where(_piece_lane_mask(lane, h + q_shift), 1.0, 0.0))
            kp = jnp.where(lane >= HEAD_DIM, kcol, ext)
            qp = jnp.concatenate([aug_q, q_h], axis=0)
        qpT_ref[0, h] = qp.astype(BF16)
        for cidx in range(tm // tk):
            kp_ref[0, h, cidx] = kp[cidx * tk:(cidx + 1) * tk, :].astype(BF16)
            v_h = vT[h * HEAD_DIM:(h + 1) * HEAD_DIM, cidx * tk:(cidx + 1) * tk].astype(BF16)
            vT_ref[0, h, cidx] = jnp.concatenate([v_h, v_tail], axis=0)

    kn2 = jnp.max(_dot((kk * kk).astype(BF16), hsel_ref[...]), axis=0, keepdims=True)
    srow = lax.broadcasted_iota(jnp.int32, (SUBLANES, LANES), 0)
    kstat_ref[0, 0] = jnp.where(srow == 0, c[0:1, :], jnp.where(srow == 1, c[tm - 1:tm, :],
                                                                 jnp.where(srow == 2, kn2, 0.0)))
    q2 = qT * qT
    qn2 = jnp.concatenate([jnp.sum(q2[h * HEAD_DIM:(h + 1) * HEAD_DIM, :], axis=0, keepdims=True)
                           for h in range(N_HEADS)], axis=0)
    qstat_ref[0, 0] = jnp.broadcast_to(jnp.max(qn2, axis=1, keepdims=True), (N_HEADS, LANES))


def _inproj(x2, g1, wvg, wk, wf, bf3, wqT, wvT, *, batch, seq, tm, tk):
    nt = seq // tm
    nk = seq // tk
    ltri = jnp.tril(jnp.ones((tm, tm), F32)).astype(BF16)
    hsel = (jnp.arange(D_ATT)[:, None] // HEAD_DIM == jnp.arange(LANES)[None, :]).astype(BF16)
    const = lambda shape: pl.BlockSpec(shape, lambda b, t: (0,) * len(shape))
    stat_spec = pl.BlockSpec((1, 1, SUBLANES, LANES), lambda b, t: (b, t, 0, 0))
    stat_shape = jax.ShapeDtypeStruct((batch, nt, SUBLANES, LANES), F32)
    return pl.pallas_call(
        functools.partial(_inproj_kernel, tm=tm, tk=tk),
        grid=(batch, nt),
        in_specs=[
            pl.BlockSpec((tm, D_MODEL), lambda b, t: (b * nt + t, 0)),
            const((1, D_MODEL)), const((D_MODEL, 2 * D_CONV)), const((D_MODEL, D_ATT)),
            const((D_MODEL, LANES)), const((1, LANES)), const((D_ATT, D_MODEL)), const((D_ATT, D_MODEL)),
            const((tm, tm)), const((D_ATT, LANES)),
        ],
        out_specs=[
            pl.BlockSpec((tm, D_CONV), lambda b, t: (b * nt + t, 0)),
            pl.BlockSpec((1, N_HEADS, tm // tk, tk, KAUG), lambda b, t: (b, 0, t, 0, 0)),
            pl.BlockSpec((1, N_HEADS, KAUG, tm), lambda b, t: (b, 0, 0, t)),
            pl.BlockSpec((1, N_HEADS, tm // tk, V_ROWS, tk), lambda b, t: (b, 0, t, 0, 0)),
            stat_spec, stat_spec,
        ],
        out_shape=[
            jax.ShapeDtypeStruct((batch * seq, D_CONV), F32),
            jax.ShapeDtypeStruct((batch, N_HEADS, nk, tk, KAUG), BF16),
            jax.ShapeDtypeStruct((batch, N_HEADS, KAUG, seq), BF16),
            jax.ShapeDtypeStruct((batch, N_HEADS, nk, V_ROWS, tk), BF16),
            stat_shape, stat_shape,
        ],
        scratch_shapes=[pltpu.VMEM((1, LANES), F32)],
        compiler_params=pltpu.CompilerParams(
            dimension_semantics=("arbitrary", "arbitrary"), vmem_limit_bytes=VMEM_LIMIT),
        name="inproj",
    )(x2, g1, wvg, wk, wf, bf3, wqT, wvT, ltri, hsel)


def _conv_tile(a_ref, w_ref, b_ref, lng_ref, lnb_ref, og_ref, sh_ref, acc_ref, *, tc, rc):
    n_cb = D_CONV // LANES

    @pl.when(pl.program_id(1) == 0)
    def _():
        for cb in range(n_cb):
            sh_ref[0, cb, 0:CONV_HALO, :] = jnp.zeros((CONV_HALO, LANES), F32)

    @pl.when(pl.program_id(1) > 0)
    def _():
        for cb in range(n_cb):
            sh_ref[0, cb, 0:CONV_HALO, :] = sh_ref[0, cb, tc:tc + CONV_HALO, :]

    n_sh = tc + CONV_HALO - SUBLANES
    for cb in range(n_cb):
        sh_ref[0, cb, CONV_HALO:CONV_HALO + tc, :] = a_ref[:, cb * LANES:(cb + 1) * LANES]
        for f in range(1, SUBLANES):
            sh_ref[f, cb, 0:n_sh, :] = sh_ref[0, cb, f:f + n_sh, :]

    base = CONV_HALO - (CONV_WIDTH - 1)
    for cb in range(n_cb):
        cols = slice(cb * LANES, (cb + 1) * LANES)

        def chunk(c, carry, cb=cb, cols=cols):
            r0 = pl.multiple_of(c * rc, rc)
            acc = jnp.zeros((rc, LANES), F32)
            for j in range(CONV_WIDTH):
                f = (base + j) % SUBLANES
                acc = acc + w_ref[j:j + 1, cols] * sh_ref[f, cb, pl.ds(r0 + (base + j - f), rc), :]
            acc_ref[pl.ds(r0, rc), cols] = acc
            return carry

        lax.fori_loop(0, tc // rc, chunk, 0)

    y = acc_ref[...] + b_ref[...]
    mu = jnp.mean(y, axis=-1, keepdims=True)
    yc = y - mu
    var = jnp.mean(yc * yc, axis=-1, keepdims=True)
    yn = yc * lax.rsqrt(var + EPS) * lng_ref[...] + lnb_ref[...]
    s = yn * jax.nn.sigmoid(yn)
    ms = jnp.mean(s * s, axis=-1, keepdims=True)
    return (s * lax.rsqrt(ms + EPS) * og_ref[...]).astype(BF16)


def _attn_fixed_ref_kernel(g0_ref, bnd_ref, qT_ref, k_ref, v_ref, o_ref, *, tq, tk):
    i = pl.program_id(2)
    qT = qT_ref[0, 0]
    n_sub = tq // tk
    bnd = bnd_ref[0, 0, 0:1, 0:1]
    g0 = g0_ref[(pl.program_id(0) * N_HEADS + pl.program_id(1)) * pl.num_programs(2) + i]

    def values(g):
        return jnp.concatenate([v_ref[0, 0, g * n_sub + d] for d in range(n_sub)], axis=1)

    def probs(g):
        return jnp.concatenate([jnp.exp2(_dot(k_ref[0, 0, g * n_sub + d], qT) - bnd).astype(BF16)
                                for d in range(n_sub)], axis=0)

    def two_groups(p, acc):
        g = g0 + 2 * p
        acc = acc + _dot(values(g), probs(g))
        return acc + _dot(values(g + 1), probs(g + 1))

    key = lax.broadcasted_iota(jnp.int32, (tk, tk), 0)
    qry = lax.broadcasted_iota(jnp.int32, (tk, tk), 1)

    def diagonal(acc):
        rows = []
        for d in range(n_sub):
            s = _dot(k_ref[0, 0, i * n_sub + d], qT[:, d * tk:])
            tiles = [jnp.zeros((tk, d * tk), BF16)] if d else []
            tiles.append(jnp.exp2(jnp.where(key <= qry, s[:, :tk], -jnp.inf) - bnd).astype(BF16))
            if d + 1 < n_sub:
                tiles.append(jnp.exp2(s[:, tk:] - bnd).astype(BF16))
            rows.append(jnp.concatenate(tiles, axis=1))
        acc = acc + _dot(values(i), jnp.concatenate(rows, axis=0))
        o_ref[0] = acc[:HEAD_DIM] / acc[HEAD_DIM:HEAD_DIM + 1]

    n_full = i - g0
    acc = lax.fori_loop(0, n_full // 2, two_groups, jnp.zeros((V_ROWS, tq), F32))

    @pl.when(n_full % 2 == 1)
    def _():
        diagonal(acc + _dot(values(i - 1), probs(i - 1)))

    @pl.when(n_full % 2 == 0)
    def _():
        diagonal(acc)


def _attn_kernel(g0_ref, qT_ref, k_ref, v_ref, o_ref, sa_ref, sb_ref, cma_ref, cmb_ref, *, tq, tk):
    i = pl.program_id(2)
    qT = qT_ref[0, 0]
    n_sub = tq // tk

    def scores(g, s_ref, cm_ref):
        for d in range(n_sub):
            s = _dot(k_ref[0, 0, g * n_sub + d], qT)
            s_ref[d] = s
            cm_ref[d] = jnp.max(s, axis=0, keepdims=True)

    def consume(g, s_ref, cm_ref, carry, nxt=None):
        m, acc = carry
        m_new = m
        for d in range(n_sub):
            m_new = jnp.maximum(m_new, cm_ref[d])
        ps = []
        for d in range(n_sub):
            if nxt is not None:
                g_n, sn_ref, cmn_ref = nxt
                s = _dot(k_ref[0, 0, g_n * n_sub + d], qT)
                sn_ref[d] = s
                cmn_ref[d] = jnp.max(s, axis=0, keepdims=True)
            ps.append(jnp.exp2(s_ref[d] - m_new).astype(BF16))
        p = jnp.concatenate(ps, axis=0)
        v = jnp.concatenate([v_ref[0, 0, g * n_sub + d] for d in range(n_sub)], axis=1)
        return m_new, jnp.exp2(m - m_new) * acc + _dot(v, p)

    key = lax.broadcasted_iota(jnp.int32, (tk, tk), 0)
    qry = lax.broadcasted_iota(jnp.int32, (tk, tk), 1)

    def consume_diag(s_ref, cm_ref, carry):
        m, acc = carry
        lanes = lambda j: slice(j * tk, (j + 1) * tk)
        causal = [jnp.where(key <= qry, s_ref[d, :, lanes(d)], -jnp.inf) for d in range(n_sub)]
        m_tiles = []
        for j in range(n_sub):
            m_j = jnp.maximum(m[:, lanes(j)], jnp.max(causal[j], axis=0, keepdims=True))
            for d in range(j):
                m_j = jnp.maximum(m_j, cm_ref[d, :, lanes(j)])
            m_tiles.append(m_j)
        m_new = jnp.concatenate(m_tiles, axis=1)
        rows = []
        for d in range(n_sub):
            tiles = [jnp.zeros((tk, d * tk), BF16)] if d else []
            tiles.append(jnp.exp2(causal[d] - m_tiles[d]).astype(BF16))
            if d + 1 < n_sub:
                rest = slice((d + 1) * tk, tq)
                tiles.append(jnp.exp2(s_ref[d, :, rest] - m_new[:, rest]).astype(BF16))
            rows.append(jnp.concatenate(tiles, axis=1))
        p = jnp.concatenate(rows, axis=0)
        v = jnp.concatenate([v_ref[0, 0, i * n_sub + d] for d in range(n_sub)], axis=1)
        acc = jnp.exp2(m - m_new) * acc + _dot(v, p)
        o_ref[0] = acc[:HEAD_DIM] / acc[HEAD_DIM:HEAD_DIM + 1]

    g0 = g0_ref[(pl.program_id(0) * N_HEADS + pl.program_id(1)) * pl.num_programs(2) + i]
    n_full = i - g0

    def pair(p, carry):
        g = g0 + 2 * p
        carry = consume(g, sa_ref, cma_ref, carry, nxt=(g + 1, sb_ref, cmb_ref))
        return consume(g + 1, sb_ref, cmb_ref, carry, nxt=(g + 2, sa_ref, cma_ref))

    scores(g0, sa_ref, cma_ref)
    init = (jnp.full((1, tq), -jnp.inf, F32), jnp.zeros((V_ROWS, tq), F32))
    carry = lax.fori_loop(0, n_full // 2, pair, init)

    @pl.when(n_full % 2 == 1)
    def _():
        consume_diag(sb_ref, cmb_ref, consume(i - 1, sa_ref, cma_ref, carry, nxt=(i, sb_ref, cmb_ref)))

    @pl.when(n_full % 2 == 0)
    def _():
        consume_diag(sa_ref, cma_ref, carry)


def _first_groups(kstat, qstat, *, tm, tq):
    per = tq // tm
    bound = jnp.sqrt(jnp.max(kstat[:, :, 2, :N_HEADS], axis=1) * jnp.max(qstat[:, :, :, 0], axis=1))
    bound = bound * NORM_SLACK
    cq = kstat[:, ::per, 0, :N_HEADS]
    ck = kstat[:, per - 1::per, 1, :N_HEADS]
    d = (cq[:, :, None, :] - ck[:, None, :, :]) * LOG2E
    nq = cq.shape[1]
    earlier = jnp.arange(nq)[None, :, None, None] > jnp.arange(nq)[None, None, :, None]
    skip = earlier & (2.0 * bound[:, None, None, :] + d < -SKIP_LOG2)
    g0 = jnp.sum(jnp.cumprod(skip.astype(jnp.int32), axis=2), axis=2)
    return jnp.transpose(g0, (0, 2, 1)).reshape(-1).astype(jnp.int32), bound


def _attention(g0, bound, qpT, kp, vT, *, batch, seq, tq, tk):
    nk = seq // tk
    params = pltpu.CompilerParams(
        dimension_semantics=("arbitrary", "arbitrary", "arbitrary"), vmem_limit_bytes=VMEM_LIMIT)
    out_shape = jax.ShapeDtypeStruct((batch, D_ATT, seq), F32)
    qkv_specs = [
        pl.BlockSpec((1, 1, KAUG, tq), lambda b, h, i, g0: (b, h, 0, i)),
        pl.BlockSpec((1, 1, nk, tk, KAUG), lambda b, h, i, g0: (b, h, 0, 0, 0)),
        pl.BlockSpec((1, 1, nk, V_ROWS, tk), lambda b, h, i, g0: (b, h, 0, 0, 0)),
    ]
    out_spec = pl.BlockSpec((1, HEAD_DIM, tq), lambda b, h, i, g0: (b, h, i))

    def fixed_ref(g0, bnd, qpT, kp, vT):
        grid_spec = pltpu.PrefetchScalarGridSpec(
            num_scalar_prefetch=1, grid=(batch, N_HEADS, seq // tq),
            in_specs=[pl.BlockSpec((1, 1, SUBLANES, LANES), lambda b, h, i, g0: (b, h, 0, 0))] + qkv_specs,
            out_specs=out_spec)
        return pl.pallas_call(functools.partial(_attn_fixed_ref_kernel, tq=tq, tk=tk), grid_spec=grid_spec,
                              out_shape=out_shape, compiler_params=params, name="attention_fixed_ref",
                              )(g0, bnd, qpT, kp, vT)

    def running_max(g0, bnd, qpT, kp, vT):
        del bnd
        grid_spec = pltpu.PrefetchScalarGridSpec(
            num_scalar_prefetch=1, grid=(batch, N_HEADS, seq // tq), in_specs=qkv_specs, out_specs=out_spec,
            scratch_shapes=[pltpu.VMEM((tq // tk, tk, tq), F32), pltpu.VMEM((tq // tk, tk, tq), F32),
                            pltpu.VMEM((tq // tk, 1, tq), F32), pltpu.VMEM((tq // tk, 1, tq), F32)])
        return pl.pallas_call(functools.partial(_attn_kernel, tq=tq, tk=tk), grid_spec=grid_spec,
                              out_shape=out_shape, compiler_params=params, name="attention",
                              )(g0, qpT, kp, vT)

    bnd = jnp.broadcast_to(bound[:, :, None, None], (batch, N_HEADS, SUBLANES, LANES))
    return lax.cond(jnp.max(bound) <= FIXED_REF_MAX_BOUND, fixed_ref, running_max, g0, bnd, qpT, kp, vT)


def _outproj_kernel(x_ref, a_ref, yT_ref, wdw_ref, bdw_ref, lng_ref, lnb_ref, gc_ref, woc_ref, woa_ref, ga_ref,
                    g2_ref, wrT_ref, brT_ref, utri_ref, x1_ref, h2_ref, route_ref, cnt_ref,
                    carry_ref, sh_ref, acc_ref, *, tm, rc):
    first = (pl.program_id(0) == 0) & (pl.program_id(1) == 0)

    @pl.when(first)
    def _():
        carry_ref[...] = jnp.zeros_like(carry_ref)

    mc = _conv_tile(a_ref, wdw_ref, bdw_ref, lng_ref, lnb_ref, gc_ref, sh_ref, acc_ref, tc=tm, rc=rc)
    yT = yT_ref[0]
    msa = jnp.mean(yT * yT, axis=0, keepdims=True)
    yn = (yT * lax.rsqrt(msa + EPS)).T * ga_ref[...]
    x1 = x_ref[...] + _dot(mc, woc_ref[...]) + _dot(yn.astype(BF16), woa_ref[...])
    x1_ref[...] = x1
    ms = jnp.mean(x1 * x1, axis=-1, keepdims=True)
    h2 = (x1 * lax.rsqrt(ms + EPS)) * g2_ref[...]
    h2_ref[:, :D_MODEL] = h2

    lg = _dot_nt(wrT_ref[...], h2.astype(BF16)) + brT_ref[...]
    neg = -jnp.inf
    sub = lax.broadcasted_iota(jnp.int32, (SUBLANES, tm), 0)
    big = jnp.int32(SUBLANES)
    first_of = lambda hit: jnp.min(jnp.where(hit, sub, big), axis=0, keepdims=True)

    lg1 = lg[0:SUBLANES]
    m1 = jnp.max(lg1, axis=0, keepdims=True)
    p1_sel = 1.0 / jnp.sum(jnp.exp(lg1 - m1), axis=0, keepdims=True)
    grp = first_of(lg1 == m1)

    slab = lambda g: lg[SUBLANES * (g + 1):SUBLANES * (g + 2)]
    v = slab(N_GROUPS - 1)
    for g in range(N_GROUPS - 2, -1, -1):
        v = jnp.where(grp == g, slab(g), v)
    v1 = jnp.max(v, axis=0, keepdims=True)
    j1 = first_of(v == v1)
    vv = jnp.where(sub == j1, neg, v)
    v2 = jnp.max(vv, axis=0, keepdims=True)
    j2 = first_of(vv == v2)
    e21 = jnp.exp(v2 - v1)
    w0 = p1_sel / (1.0 + e21)
    w1 = p1_sel * e21 / (1.0 + e21)

    swap = j2 < j1
    al = jnp.where(swap, j2, j1)
    bl = jnp.where(swap, j1, j2)
    wa = jnp.where(swap, w1, w0)
    wb = jnp.where(swap, w0, w1)
    cid = PAIRS_PER_GROUP * grp + ((al * (2 * EXPERTS_PER_GROUP - 1 - al)) >> 1) + (bl - al - 1)

    cls = lax.broadcasted_iota(jnp.int32, (LANES, tm), 0)
    oh = cls == cid
    cmat = jnp.where(oh, 1.0, 0.0)
    prefix = _dot(cmat.astype(BF16), utri_ref[...]) + carry_ref[...]
    rank = jnp.sum(jnp.where(oh, prefix, 0.0), axis=0, keepdims=True)
    counts = prefix[:, tm - 1:tm] + cmat[:, tm - 1:tm]
    carry_ref[...] = counts
    cnt_ref[...] = counts

    h2_ref[:, D_MODEL:] = jnp.where(cls == 0, wa, jnp.where(cls == 1, wb, 0.0)).T
    route_ref[...] = jnp.where(sub == 0, cid.astype(F32), jnp.where(sub == 1, rank, 0.0))


def _outproj(x2, a, yT, conv_params, woc, woa, ga, g2, wrT, brT, *, batch, seq, tm, rc):
    nt = seq // tm
    utri = jnp.triu(jnp.ones((tm, tm), F32), 1).astype(BF16)
    const = lambda shape: pl.BlockSpec(shape, lambda b, t: (0,) * len(shape))
    row_spec = lambda w: pl.BlockSpec((tm, w), lambda b, t: (b * nt + t, 0))
    return pl.pallas_call(
        functools.partial(_outproj_kernel, tm=tm, rc=rc),
        grid=(batch, nt),
        in_specs=[row_spec(D_MODEL), row_spec(D_CONV),
                  pl.BlockSpec((1, D_ATT, tm), lambda b, t: (b, 0, t)),
                  const((CONV_WIDTH, D_CONV)), const((1, D_CONV)), const((1, D_CONV)), const((1, D_CONV)),
                  const((1, D_CONV)),
                  const((D_CONV, D_MODEL)), const((D_ATT, D_MODEL)), const((1, D_ATT)), const((1, D_MODEL)),
                  const((LANES, D_MODEL)), const((LANES, 1)), const((tm, tm))],
        out_specs=[row_spec(D_MODEL), row_spec(ROW_W),
                   pl.BlockSpec((SUBLANES, tm), lambda b, t: (0, b * nt + t)), const((LANES, 1))],
        out_shape=[jax.ShapeDtypeStruct((batch * seq, D_MODEL), F32),
                   jax.ShapeDtypeStruct((batch * seq, ROW_W), F32),
                   jax.ShapeDtypeStruct((SUBLANES, batch * seq), F32),
                   jax.ShapeDtypeStruct((LANES, 1), F32)],
        scratch_shapes=[pltpu.VMEM((LANES, 1), F32),
                        pltpu.VMEM((SUBLANES, D_CONV // LANES, tm + CONV_HALO, LANES), F32),
                        pltpu.VMEM((tm, D_CONV), F32)],
        compiler_params=pltpu.CompilerParams(
            dimension_semantics=("arbitrary", "arbitrary"), vmem_limit_bytes=VMEM_LIMIT),
        name="outproj",
    )(x2, a, yT, *conv_params, woc, woa, ga, g2, wrT, brT, utri)


def _row_copies(src_at, dst_at, sem):
    return pltpu.make_async_copy(src_at, dst_at, sem)


def _dispatch_kernel(dest_ref, h2_ref, xs_ref, sem, *, tm):
    def body(g, carry):
        for u in range(SUBLANES):
            _row_copies(h2_ref.at[g, pl.ds(u, 1)], xs_ref.at[pl.ds(dest_ref[g * SUBLANES + u], 1)],
                        sem).start(priority=u % 2)
        return carry

    lax.fori_loop(0, tm // SUBLANES, body, 0)
    done = xs_ref.at[pl.ds(0, tm)]
    _row_copies(done, done, sem).wait()


def _dispatch(dest, h2, *, tm):
    t_len = h2.shape[0]
    return pl.pallas_call(
        functools.partial(_dispatch_kernel, tm=tm),
        grid=(t_len // tm,),
        in_specs=[pl.BlockSpec((tm,), lambda i: (i,), memory_space=pltpu.SMEM),
                  pl.BlockSpec((tm // SUBLANES, SUBLANES, ROW_W), lambda i: (i, 0, 0))],
        out_specs=pl.BlockSpec(memory_space=pl.ANY),
        out_shape=jax.ShapeDtypeStruct((t_len, ROW_W), F32),
        scratch_shapes=[pltpu.SemaphoreType.DMA(())],
        compiler_params=pltpu.CompilerParams(dimension_semantics=("arbitrary",), vmem_limit_bytes=VMEM_LIMIT),
        name="dispatch",
    )(dest, h2.reshape(t_len // SUBLANES, SUBLANES, ROW_W))


def _expert_kernel(blk_ref, ea_ref, eb_ref, lo_ref, hi_ref, x_ref, wgua_ref, wda_ref, wgub_ref, wdb_ref, y_ref):
    del blk_ref, ea_ref, eb_ref
    i = pl.program_id(0)
    lo = lo_ref[i]
    hi = hi_ref[i]
    half = ROW_BLOCK // 2

    def mlp(xb, wgu_ref, wd_ref):
        gu = _dot(xb, wgu_ref[0])
        g = gu[:, :D_EXPERT]
        return _dot(((g * jax.nn.sigmoid(g)) * gu[:, D_EXPERT:]).astype(BF16), wd_ref[0])

    def run(rows):
        n = rows.stop - rows.start
        xb = x_ref[rows, :D_MODEL].astype(BF16)
        wts = x_ref[rows, D_MODEL:]
        y = wts[:, 0:1] * mlp(xb, wgua_ref, wda_ref) + wts[:, 1:2] * mlp(xb, wgub_ref, wdb_ref)
        row = rows.start + lax.broadcasted_iota(jnp.int32, (n, 1), 0)
        mine = (row >= lo) & (row < hi)

        @pl.when(lo == 0)
        def _():
            y_ref[rows, :] = jnp.where(mine, y, 0.0)
            if n < ROW_BLOCK:
                y_ref[half:, :] = jnp.zeros((half, D_MODEL), F32)

        @pl.when(lo > 0)
        def _():
            y_ref[rows, :] = jnp.where(mine, y, y_ref[rows, :])

    @pl.when((hi > lo) & (lo < half) & (hi > half))
    def _():
        run(slice(0, ROW_BLOCK))

    @pl.when((hi > lo) & (hi <= half))
    def _():
        run(slice(0, half))

    @pl.when((hi > lo) & (lo >= half))
    def _():
        run(slice(half, ROW_BLOCK))


def _experts(items, xs, wgu, wd):
    n_rows = xs.shape[0]
    n_items = items[0].shape[0]
    wspec = lambda shape, which: pl.BlockSpec(
        shape, lambda i, blk, ea, eb, lo, hi: ((ea, eb)[which][i], 0, 0))
    grid_spec = pltpu.PrefetchScalarGridSpec(
        num_scalar_prefetch=5,
        grid=(n_items,),
        in_specs=[pl.BlockSpec((ROW_BLOCK, ROW_W), lambda i, blk, ea, eb, lo, hi: (blk[i], 0)),
                  wspec((1, D_MODEL, 2 * D_EXPERT), 0), wspec((1, D_EXPERT, D_MODEL), 0),
                  wspec((1, D_MODEL, 2 * D_EXPERT), 1), wspec((1, D_EXPERT, D_MODEL), 1)],
        out_specs=pl.BlockSpec((ROW_BLOCK, D_MODEL), lambda i, blk, ea, eb, lo, hi: (blk[i], 0)),
    )
    return pl.pallas_call(
        _expert_kernel,
        grid_spec=grid_spec,
        out_shape=jax.ShapeDtypeStruct((n_rows, D_MODEL), F32),
        compiler_params=pltpu.CompilerParams(dimension_semantics=("arbitrary",), vmem_limit_bytes=VMEM_LIMIT),
        name="experts",
    )(*items, xs, wgu, wd, wgu, wd)


def _combine_kernel(dcur_ref, dnxt_ref, x1_ref, g_ref, ys_ref, out_ref, buf_ref, sems, *, tm):
    i = pl.program_id(0)
    slot = i % 2

    def issue(d_ref, s):
        def body(g, carry):
            for u in range(SUBLANES):
                _row_copies(ys_ref.at[pl.ds(d_ref[g * SUBLANES + u], 1)], buf_ref.at[s, g, pl.ds(u, 1)],
                            sems.at[s]).start(priority=u % 2)
            return carry

        lax.fori_loop(0, tm // SUBLANES, body, 0)

    @pl.when(i == 0)
    def _():
        issue(dcur_ref, 0)

    @pl.when(i + 1 < pl.num_programs(0))
    def _():
        issue(dnxt_ref, 1 - slot)

    _row_copies(buf_ref.at[slot], buf_ref.at[slot], sems.at[slot]).wait()
    x2 = x1_ref[...] + buf_ref[slot]
    ms = jnp.mean(x2 * x2, axis=-1, keepdims=True)
    out_ref[...] = (x2 * lax.rsqrt(ms + EPS)) * g_ref[...]


def _combine(dest, x1, gf, ys, *, tm):
    t_len = x1.shape[0]
    n = t_len // tm
    return pl.pallas_call(
        functools.partial(_combine_kernel, tm=tm),
        grid=(n,),
        in_specs=[pl.BlockSpec((tm,), lambda i: (i,), memory_space=pltpu.SMEM),
                  pl.BlockSpec((tm,), lambda i: (jnp.minimum(i + 1, n - 1),), memory_space=pltpu.SMEM),
                  pl.BlockSpec((tm // SUBLANES, SUBLANES, D_MODEL), lambda i: (i, 0, 0)),
                  pl.BlockSpec((1, D_MODEL), lambda i: (0, 0)),
                  pl.BlockSpec(memory_space=pl.ANY)],
        out_specs=pl.BlockSpec((tm // SUBLANES, SUBLANES, D_MODEL), lambda i: (i, 0, 0)),
        out_shape=jax.ShapeDtypeStruct((t_len // SUBLANES, SUBLANES, D_MODEL), F32),
        scratch_shapes=[pltpu.VMEM((2, tm // SUBLANES, SUBLANES, D_MODEL), F32), pltpu.SemaphoreType.DMA((2,))],
        compiler_params=pltpu.CompilerParams(dimension_semantics=("arbitrary",), vmem_limit_bytes=VMEM_LIMIT),
        name="combine",
    )(dest, dest, x1.reshape(t_len // SUBLANES, SUBLANES, D_MODEL), gf, ys)


def _layer(x, norm1_g, w_in, b_f, w_dw, b_dw, conv_ln_g, conv_ln_b, out_g_conv, out_g_att, w_out,
           norm2_g, w_r1, b_r1, w_r2, b_r2, w_gate, w_up, w_down):
    batch, seq, d = x.shape
    t_len = batch * seq
    tm = min(512, seq)
    tk = min(256, seq)
    tq = min(1024, seq)
    o1, o2 = D_CONV, 2 * D_CONV
    o3, o4, o5 = o2 + D_ATT, o2 + 2 * D_ATT, o2 + 3 * D_ATT

    wvg = w_in[:, :o2].astype(BF16)
    wk = w_in[:, o3:o4].astype(BF16)
    wqT = (w_in[:, o2:o3] * (HEAD_DIM ** -0.5)).T.astype(BF16)
    wvT = w_in[:, o4:o5].T.astype(BF16)
    assert PIECE_STRIDE == N_HEADS
    pad = LANES - N_PIECES * PIECE_STRIDE
    wf = jnp.pad(jnp.tile(w_in[:, o5:], (1, N_PIECES)), ((0, 0), (0, pad))).astype(BF16)
    bf3 = jnp.pad(jnp.tile(b_f.astype(F32).reshape(1, N_HEADS), (1, N_PIECES)), ((0, 0), (0, pad)))

    x2 = x.reshape(t_len, d)
    a, kp, qpT, vT, kstat, qstat = _inproj(x2, norm1_g.reshape(1, d), wvg, wk, wf, bf3, wqT, wvT,
                                           batch=batch, seq=seq, tm=tm, tk=tk)
    conv_params = (w_dw, b_dw.reshape(1, -1), conv_ln_g.reshape(1, -1), conv_ln_b.reshape(1, -1),
                   out_g_conv.reshape(1, -1))
    g0, bound = _first_groups(kstat, qstat, tm=tm, tq=tq)
    yT = _attention(g0, bound, qpT, kp, vT, batch=batch, seq=seq, tq=tq, tk=tk)

    gpad = SUBLANES - N_GROUPS
    rpad = LANES - SUBLANES - N_EXPERTS
    wrT = jnp.concatenate([w_r1.T, jnp.zeros((gpad, d), F32),
                           jnp.transpose(w_r2, (0, 2, 1)).reshape(N_EXPERTS, d), jnp.zeros((rpad, d), F32)], axis=0)
    brT = jnp.concatenate([b_r1.astype(F32), jnp.full((gpad,), -jnp.inf, F32), b_r2.reshape(-1).astype(F32),
                           jnp.zeros((rpad,), F32)]).reshape(LANES, 1)
    x1, h2, route, cnt = _outproj(x2, a, yT, conv_params, w_out[:D_CONV].astype(BF16),
                                  w_out[D_CONV:].astype(BF16), out_g_att.reshape(1, -1), norm2_g.reshape(1, d),
                                  wrT.astype(BF16), brT, batch=batch, seq=seq, tm=tm, rc=min(128, seq))

    i32 = jnp.int32
    lanes = jnp.arange(LANES, dtype=i32)
    pick = lambda table, idx: jnp.sum(jnp.where(idx[:, None] == lanes, table[None, :], 0), axis=1).astype(i32)
    counts = cnt[:, 0].astype(i32)
    ends = jnp.cumsum(counts).astype(i32)
    starts = ends - counts
    dest = pick(starts, route[0].astype(i32)) + route[1].astype(i32)

    n_blocks = t_len // ROW_BLOCK
    b_lo = starts // ROW_BLOCK
    n_it = jnp.where(counts > 0, (ends - 1) // ROW_BLOCK - b_lo + 1, 0)
    it_end = jnp.cumsum(n_it).astype(i32)
    it_start = it_end - n_it
    idx = jnp.arange(n_blocks + N_CLASSES, dtype=i32)
    valid = idx < it_end[-1]
    last_cls = jnp.max(jnp.where(counts > 0, lanes, 0))
    cls = jnp.where(valid, jnp.sum(it_end[None, :] <= idx[:, None], axis=1).astype(i32), last_cls)
    blk = jnp.where(valid, pick(b_lo, cls) + idx - pick(it_start, cls), n_blocks - 1)
    row0 = blk * ROW_BLOCK
    lo = jnp.where(valid, jnp.maximum(pick(starts, cls), row0) - row0, 0)
    hi = jnp.where(valid, jnp.minimum(pick(ends, cls), row0 + ROW_BLOCK) - row0, 0)
    pair_a, pair_b = [], []
    for g in range(N_GROUPS):
        for a_loc in range(EXPERTS_PER_GROUP):
            for b_loc in range(a_loc + 1, EXPERTS_PER_GROUP):
                pair_a.append(g * EXPERTS_PER_GROUP + a_loc)
                pair_b.append(g * EXPERTS_PER_GROUP + b_loc)
    cpad = [0] * (LANES - N_CLASSES)
    items = (blk.astype(i32), pick(jnp.array(pair_a + cpad, i32), cls), pick(jnp.array(pair_b + cpad, i32), cls),
             lo.astype(i32), hi.astype(i32))

    xs = _dispatch(dest, h2, tm=min(1024, t_len))
    ys = _experts(items, xs, jnp.concatenate([w_gate, w_up], axis=-1).astype(BF16), w_down.astype(BF16))
    return dest, x1, ys


def kernel(x, norm1_g, w_in, b_f, w_dw, b_dw, conv_ln_g, conv_ln_b, out_g_conv, out_g_att, w_out, norm2_g,
           w_r1, b_r1, w_r2, b_r2, w_gate, w_up, w_down, final_g):
    assert norm1_g.shape[0] == 1, "single-layer stack"
    batch, seq, d = x.shape
    dest, x1, ys = _layer(
        x, norm1_g[0], w_in[0], b_f[0], w_dw[0], b_dw[0], conv_ln_g[0], conv_ln_b[0], out_g_conv[0],
        out_g_att[0], w_out[0], norm2_g[0], w_r1[0], b_r1[0], w_r2[0], b_r2[0], w_gate[0], w_up[0], w_down[0])
    out = _combine(dest, x1, final_g.reshape(1, d), ys, tm=min(512, batch * seq))
    return out.reshape(batch, seq, d)
```

```python
import functools

import jax
import jax.numpy as jnp
from jax import lax
from jax.experimental import pallas as pl
from jax.experimental.pallas import tpu as pltpu

D_MODEL = 1024
D_CONV = 512
N_HEADS = 8
HEAD_DIM = 64
D_ATT = N_HEADS * HEAD_DIM
CONV_WIDTH = 31
N_GROUPS = 4
EXPERTS_PER_GROUP = 8
N_EXPERTS = N_GROUPS * EXPERTS_PER_GROUP
TOP_K = 2
D_EXPERT = D_MODEL // 4
ROW_BLOCK = 256
EPS = 1e-6

LANES = 128
SUBLANES = 8
KAUG = 128
N_PIECES = 3
PIECE_STRIDE = 8
CONV_HALO = 32
PAIRS_PER_GROUP = EXPERTS_PER_GROUP * (EXPERTS_PER_GROUP - 1) // 2
N_CLASSES = N_GROUPS * PAIRS_PER_GROUP
ROW_W = D_MODEL + LANES
SKIP_LOG2 = 160.0
FIXED_REF_MAX_BOUND = 40.0
NORM_SLACK = 1.02
V_ROWS = 80
LOG2E = 1.4426950408889634
VMEM_LIMIT = 56 * 1024 * 1024

F32 = jnp.float32
BF16 = jnp.bfloat16


def _dot(a, b):
    return jnp.dot(a, b, preferred_element_type=F32)


def _dot_nt(a, b):
    return lax.dot_general(a, b, (((1,), (1,)), ((), ())), preferred_element_type=F32)


def _split3(x):
    hi = x.astype(BF16)
    r1 = x - hi.astype(F32)
    mid = r1.astype(BF16)
    lo = (r1 - mid.astype(F32)).astype(BF16)
    return hi.astype(F32), mid.astype(F32), lo.astype(F32)


def _piece_lane_mask(lane, h):
    return (lane == h) | (lane == h + PIECE_STRIDE) | (lane == h + 2 * PIECE_STRIDE)


def _inproj_kernel(x_ref, g1_ref, wvg_ref, wk_ref, wf_ref, bf_ref, wqT_ref, wvT_ref, ltri_ref, hsel_ref,
                   a_ref, kp_ref, qpT_ref, vT_ref, kstat_ref, qstat_ref, carry_ref, *, tm, tk):
    @pl.when(pl.program_id(1) == 0)
    def _():
        carry_ref[...] = jnp.zeros_like(carry_ref)

    x = x_ref[...]
    ms = jnp.mean(x * x, axis=-1, keepdims=True)
    hb = ((x * lax.rsqrt(ms + EPS)) * g1_ref[...]).astype(BF16)

    zvg = _dot(hb, wvg_ref[...])
    a_ref[...] = zvg[:, :D_CONV] * jax.nn.sigmoid(zvg[:, D_CONV:])

    kk = _dot(hb, wk_ref[...])

    zf = _dot(hb, wf_ref[...]) + bf_ref[...]
    lf = jnp.minimum(zf, 0.0) - jnp.log1p(jnp.exp(-jnp.abs(zf)))
    lane = lax.broadcasted_iota(jnp.int32, (tm, LANES), 1)
    hi, mid, lo = _split3(lf)
    lf3 = jnp.where(lane < PIECE_STRIDE, hi,
                    jnp.where(lane < 2 * PIECE_STRIDE, mid,
                              jnp.where(lane < 3 * PIECE_STRIDE, lo, 0.0))).astype(BF16)
    cs3 = _dot(ltri_ref[...], lf3)
    c = (cs3 + pltpu.roll(cs3, LANES - PIECE_STRIDE, 1)
         + pltpu.roll(cs3, LANES - 2 * PIECE_STRIDE, 1)) + carry_ref[...]
    carry_ref[...] = c[tm - 1:tm, :]

    nhi, nmid, nlo = _split3(c * (-LOG2E))
    p3 = jnp.where(lane < PIECE_STRIDE, nhi,
                   jnp.where(lane < 2 * PIECE_STRIDE, pltpu.roll(nmid, PIECE_STRIDE, 1),
                             jnp.where(lane < 3 * PIECE_STRIDE, pltpu.roll(nlo, 2 * PIECE_STRIDE, 1), 0.0)))
    p3_hi = pltpu.roll(p3, HEAD_DIM, 1)
    q_pieces = (-p3).T[0:HEAD_DIM]
    q_shift = N_PIECES * PIECE_STRIDE

    qT = _dot_nt(wqT_ref[...], hb) * LOG2E
    vT = _dot_nt(wvT_ref[...], hb)
    row = lax.broadcasted_iota(jnp.int32, (HEAD_DIM, tm), 0)
    vrow = lax.broadcasted_iota(jnp.int32, (V_ROWS - HEAD_DIM, tk), 0)
    v_tail = jnp.where(vrow == 0, 1.0, 0.0).astype(BF16)

    for h in range(N_HEADS):
        kcol = kk[:, (h // 2) * LANES:(h // 2 + 1) * LANES]
        f_h = jnp.where(_piece_lane_mask(row, h), q_pieces, 0.0)
        aug_q = jnp.where(_piece_lane_mask(row, h), 1.0, 0.0) + jnp.concatenate(
            [jnp.zeros((q_shift, tm), F32), f_h[0:HEAD_DIM - q_shift]], axis=0)
        q_h = qT[h * HEAD_DIM:(h + 1) * HEAD_DIM, :]
        if h % 2 == 0:
            ext = jnp.where(_piece_lane_mask(lane, h + HEAD_DIM), p3_hi,
                            jnp.where(_piece_lane_mask(lane, h + HEAD_DIM + q_shift), 1.0, 0.0))
            kp = jnp.where(lane < HEAD_DIM, kcol, ext)
            qp = jnp.concatenate([q_h, aug_q], axis=0)
        else:
            ext = jnp.where(_piece_lane_mask(lane, h), p3,
                            jnp.where(_piece_lane_mask(lane, h + q_shift), 1.0, 0.0))
            kp = jnp.where(lane >= HEAD_DIM, kcol, ext)
            qp = jnp.concatenate([aug_q, q_h], axis=0)
        qpT_ref[0, h] = qp.astype(BF16)
        for cidx in range(tm // tk):
            kp_ref[0, h, cidx] = kp[cidx * tk:(cidx + 1) * tk, :].astype(BF16)
            v_h = vT[h * HEAD_DIM:(h + 1) * HEAD_DIM, cidx * tk:(cidx + 1) * tk].astype(BF16)
            vT_ref[0, h, cidx] = jnp.concatenate([v_h, v_tail], axis=0)

    kn2 = jnp.max(_dot((kk * kk).astype(BF16), hsel_ref[...]), axis=0, keepdims=True)
    srow = lax.broadcasted_iota(jnp.int32, (SUBLANES, LANES), 0)
    kstat_ref[0, 0] = jnp.where(srow == 0, c[0:1, :], jnp.where(srow == 1, c[tm - 1:tm, :],
                                                                 jnp.where(srow == 2, kn2, 0.0)))
    q2 = qT * qT
    qn2 = jnp.concatenate([jnp.sum(q2[h * HEAD_DIM:(h + 1) * HEAD_DIM, :], axis=0, keepdims=True)
                           for h in range(N_HEADS)], axis=0)
    qstat_ref[0, 0] = jnp.broadcast_to(jnp.max(qn2, axis=1, keepdims=True), (N_HEADS, LANES))


def _inproj(x2, g1, wvg, wk, wf, bf3, wqT, wvT, *, batch, seq, tm, tk):
    nt = seq // tm
    nk = seq // tk
    ltri = jnp.tril(jnp.ones((tm, tm), F32)).astype(BF16)
    hsel = (jnp.arange(D_ATT)[:, None] // HEAD_DIM == jnp.arange(LANES)[None, :]).astype(BF16)
    const = lambda shape: pl.BlockSpec(shape, lambda b, t: (0,) * len(shape))
    stat_spec = pl.BlockSpec((1, 1, SUBLANES, LANES), lambda b, t: (b, t, 0, 0))
    stat_shape = jax.ShapeDtypeStruct((batch, nt, SUBLANES, LANES), F32)
    return pl.pallas_call(
        functools.partial(_inproj_kernel, tm=tm, tk=tk),
        grid=(batch, nt),
        in_specs=[
            pl.BlockSpec((tm, D_MODEL), lambda b, t: (b * nt + t, 0)),
            const((1, D_MODEL)), const((D_MODEL, 2 * D_CONV)), const((D_MODEL, D_ATT)),
            const((D_MODEL, LANES)), const((1, LANES)), const((D_ATT, D_MODEL)), const((D_ATT, D_MODEL)),
            const((tm, tm)), const((D_ATT, LANES)),
        ],
        out_specs=[
            pl.BlockSpec((tm, D_CONV), lambda b, t: (b * nt + t, 0)),
            pl.BlockSpec((1, N_HEADS, tm // tk, tk, KAUG), lambda b, t: (b, 0, t, 0, 0)),
            pl.BlockSpec((1, N_HEADS, KAUG, tm), lambda b, t: (b, 0, 0, t)),
            pl.BlockSpec((1, N_HEADS, tm // tk, V_ROWS, tk), lambda b, t: (b, 0, t, 0, 0)),
            stat_spec, stat_spec,
        ],
        out_shape=[
            jax.ShapeDtypeStruct((batch * seq, D_CONV), F32),
            jax.ShapeDtypeStruct((batch, N_HEADS, nk, tk, KAUG), BF16),
            jax.ShapeDtypeStruct((batch, N_HEADS, KAUG, seq), BF16),
            jax.ShapeDtypeStruct((batch, N_HEADS, nk, V_ROWS, tk), BF16),
            stat_shape, stat_shape,
        ],
        scratch_shapes=[pltpu.VMEM((1, LANES), F32)],
        compiler_params=pltpu.CompilerParams(
            dimension_semantics=("arbitrary", "arbitrary"), vmem_limit_bytes=VMEM_LIMIT),
        name="inproj",
    )(x2, g1, wvg, wk, wf, bf3, wqT, wvT, ltri, hsel)


def _conv_tile(a_ref, w_ref, b_ref, lng_ref, lnb_ref, og_ref, sh_ref, acc_ref, *, tc, rc):
    n_cb = D_CONV // LANES

    @pl.when(pl.program_id(1) == 0)
    def _():
        for cb in range(n_cb):
            sh_ref[0, cb, 0:CONV_HALO, :] = jnp.zeros((CONV_HALO, LANES), F32)

    @pl.when(pl.program_id(1) > 0)
    def _():
        for cb in range(n_cb):
            sh_ref[0, cb, 0:CONV_HALO, :] = sh_ref[0, cb, tc:tc + CONV_HALO, :]

    n_sh = tc + CONV_HALO - SUBLANES
    for cb in range(n_cb):
        sh_ref[0, cb, CONV_HALO:CONV_HALO + tc, :] = a_ref[:, cb * LANES:(cb + 1) * LANES]
        for f in range(1, SUBLANES):
            sh_ref[f, cb, 0:n_sh, :] = sh_ref[0, cb, f:f + n_sh, :]

    base = CONV_HALO - (CONV_WIDTH - 1)
    for cb in range(n_cb):
        cols = slice(cb * LANES, (cb + 1) * LANES)

        def chunk(c, carry, cb=cb, cols=cols):
            r0 = pl.multiple_of(c * rc, rc)
            acc = jnp.zeros((rc, LANES), F32)
            for j in range(CONV_WIDTH):
                f = (base + j) % SUBLANES
                acc = acc + w_ref[j:j + 1, cols] * sh_ref[f, cb, pl.ds(r0 + (base + j - f), rc), :]
            acc_ref[pl.ds(r0, rc), cols] = acc
            return carry

        lax.fori_loop(0, tc // rc, chunk, 0)

    y = acc_ref[...] + b_ref[...]
    mu = jnp.mean(y, axis=-1, keepdims=True)
    yc = y - mu
    var = jnp.mean(yc * yc, axis=-1, keepdims=True)
    yn = yc * lax.rsqrt(var + EPS) * lng_ref[...] + lnb_ref[...]
    s = yn * jax.nn.sigmoid(yn)
    ms = jnp.mean(s * s, axis=-1, keepdims=True)
    return (s * lax.rsqrt(ms + EPS) * og_ref[...]).astype(BF16)


def _attn_fixed_ref_kernel(g0_ref, bnd_ref, qT_ref, k_ref, v_ref, o_ref, *, tq, tk, unit):
    i = pl.program_id(2)
    qT = qT_ref[0, 0]
    n_sub = tq // tk
    bnd = bnd_ref[0, 0, 0:1, 0:1]
    sb0 = unit * g0_ref[(pl.program_id(0) * N_HEADS + pl.program_id(1)) * pl.num_programs(2) + i]

    def values(sb, n):
        return jnp.concatenate([v_ref[0, 0, sb + d] for d in range(n)], axis=1)

    def probs(sb, n):
        return jnp.concatenate([jnp.exp2(_dot(k_ref[0, 0, sb + d], qT) - bnd).astype(BF16)
                                for d in range(n)], axis=0)

    def two_groups(p, acc):
        sb = sb0 + 2 * n_sub * p
        acc = acc + _dot(values(sb, n_sub), probs(sb, n_sub))
        return acc + _dot(values(sb + n_sub, n_sub), probs(sb + n_sub, n_sub))

    key = lax.broadcasted_iota(jnp.int32, (tk, tk), 0)
    qry = lax.broadcasted_iota(jnp.int32, (tk, tk), 1)

    def diagonal(acc):
        rows = []
        for d in range(n_sub):
            s = _dot(k_ref[0, 0, i * n_sub + d], qT[:, d * tk:])
            tiles = [jnp.zeros((tk, d * tk), BF16)] if d else []
            tiles.append(jnp.exp2(jnp.where(key <= qry, s[:, :tk], -jnp.inf) - bnd).astype(BF16))
            if d + 1 < n_sub:
                tiles.append(jnp.exp2(s[:, tk:] - bnd).astype(BF16))
            rows.append(jnp.concatenate(tiles, axis=1))
        acc = acc + _dot(values(i * n_sub, n_sub), jnp.concatenate(rows, axis=0))
        o_ref[0] = acc[:HEAD_DIM] / acc[HEAD_DIM:HEAD_DIM + 1]

    n_before = i * n_sub - sb0
    acc = lax.fori_loop(0, n_before // (2 * n_sub), two_groups, jnp.zeros((V_ROWS, tq), F32))
    left = n_before % (2 * n_sub)

    for n in range(0, 2 * n_sub, unit):
        @pl.when(left == n)
        def _(n=n):
            diagonal(acc + _dot(values(i * n_sub - n, n), probs(i * n_sub - n, n)) if n else acc)


def _attn_kernel(g0_ref, qT_ref, k_ref, v_ref, o_ref, sa_ref, sb_ref, cma_ref, cmb_ref, *, tq, tk):
    i = pl.program_id(2)
    qT = qT_ref[0, 0]
    n_sub = tq // tk

    def scores(g, s_ref, cm_ref):
        for d in range(n_sub):
            s = _dot(k_ref[0, 0, g * n_sub + d], qT)
            s_ref[d] = s
            cm_ref[d] = jnp.max(s, axis=0, keepdims=True)

    def consume(g, s_ref, cm_ref, carry, nxt=None):
        m, acc = carry
        m_new = m
        for d in range(n_sub):
            m_new = jnp.maximum(m_new, cm_ref[d])
        ps = []
        for d in range(n_sub):
            if nxt is not None:
                g_n, sn_ref, cmn_ref = nxt
                s = _dot(k_ref[0, 0, g_n * n_sub + d], qT)
                sn_ref[d] = s
                cmn_ref[d] = jnp.max(s, axis=0, keepdims=True)
            ps.append(jnp.exp2(s_ref[d] - m_new).astype(BF16))
        p = jnp.concatenate(ps, axis=0)
        v = jnp.concatenate([v_ref[0, 0, g * n_sub + d] for d in range(n_sub)], axis=1)
        return m_new, jnp.exp2(m - m_new) * acc + _dot(v, p)

    key = lax.broadcasted_iota(jnp.int32, (tk, tk), 0)
    qry = lax.broadcasted_iota(jnp.int32, (tk, tk), 1)

    def consume_diag(s_ref, cm_ref, carry):
        m, acc = carry
        lanes = lambda j: slice(j * tk, (j + 1) * tk)
        causal = [jnp.where(key <= qry, s_ref[d, :, lanes(d)], -jnp.inf) for d in range(n_sub)]
        m_tiles = []
        for j in range(n_sub):
            m_j = jnp.maximum(m[:, lanes(j)], jnp.max(causal[j], axis=0, keepdims=True))
            for d in range(j):
                m_j = jnp.maximum(m_j, cm_ref[d, :, lanes(j)])
            m_tiles.append(m_j)
        m_new = jnp.concatenate(m_tiles, axis=1)
        rows = []
        for d in range(n_sub):
            tiles = [jnp.zeros((tk, d * tk), BF16)] if d else []
            tiles.append(jnp.exp2(causal[d] - m_tiles[d]).astype(BF16))
            if d + 1 < n_sub:
                rest = slice((d + 1) * tk, tq)
                tiles.append(jnp.exp2(s_ref[d, :, rest] - m_new[:, rest]).astype(BF16))
            rows.append(jnp.concatenate(tiles, axis=1))
        p = jnp.concatenate(rows, axis=0)
        v = jnp.concatenate([v_ref[0, 0, i * n_sub + d] for d in range(n_sub)], axis=1)
        acc = jnp.exp2(m - m_new) * acc + _dot(v, p)
        o_ref[0] = acc[:HEAD_DIM] / acc[HEAD_DIM:HEAD_DIM + 1]

    g0 = g0_ref[(pl.program_id(0) * N_HEADS + pl.program_id(1)) * pl.num_programs(2) + i]
    n_full = i - g0

    def pair(p, carry):
        g = g0 + 2 * p
        carry = consume(g, sa_ref, cma_ref, carry, nxt=(g + 1, sb_ref, cmb_ref))
        return consume(g + 1, sb_ref, cmb_ref, carry, nxt=(g + 2, sa_ref, cma_ref))

    scores(g0, sa_ref, cma_ref)
    init = (jnp.full((1, tq), -jnp.inf, F32), jnp.zeros((V_ROWS, tq), F32))
    carry = lax.fori_loop(0, n_full // 2, pair, init)

    @pl.when(n_full % 2 == 1)
    def _():
        consume_diag(sb_ref, cmb_ref, consume(i - 1, sa_ref, cma_ref, carry, nxt=(i, sb_ref, cmb_ref)))

    @pl.when(n_full % 2 == 0)
    def _():
        consume_diag(sa_ref, cma_ref, carry)


def _first_groups(kstat, qstat, *, tm, tq):
    per = tq // tm
    bound = jnp.sqrt(jnp.max(kstat[:, :, 2, :N_HEADS], axis=1) * jnp.max(qstat[:, :, :, 0], axis=1))
    bound = bound * NORM_SLACK
    cq = kstat[:, ::per, 0, :N_HEADS]
    ck = kstat[:, :, 1, :N_HEADS]
    d = (cq[:, :, None, :] - ck[:, None, :, :]) * LOG2E
    nq, ng = cq.shape[1], ck.shape[1]
    earlier = per * jnp.arange(nq)[None, :, None, None] > jnp.arange(ng)[None, None, :, None]
    skip = earlier & (2.0 * bound[:, None, None, :] + d < -SKIP_LOG2)
    g0 = jnp.sum(jnp.cumprod(skip.astype(jnp.int32), axis=2), axis=2)
    return jnp.transpose(g0, (0, 2, 1)).reshape(-1).astype(jnp.int32), bound


def _attention(g0, bound, qpT, kp, vT, *, batch, seq, tm, tq, tk):
    nk = seq // tk
    params = pltpu.CompilerParams(
        dimension_semantics=("arbitrary", "arbitrary", "arbitrary"), vmem_limit_bytes=VMEM_LIMIT)
    out_shape = jax.ShapeDtypeStruct((batch, D_ATT, seq), F32)
    qkv_specs = [
        pl.BlockSpec((1, 1, KAUG, tq), lambda b, h, i, g0: (b, h, 0, i)),
        pl.BlockSpec((1, 1, nk, tk, KAUG), lambda b, h, i, g0: (b, h, 0, 0, 0)),
        pl.BlockSpec((1, 1, nk, V_ROWS, tk), lambda b, h, i, g0: (b, h, 0, 0, 0)),
    ]
    out_spec = pl.BlockSpec((1, HEAD_DIM, tq), lambda b, h, i, g0: (b, h, i))

    def fixed_ref(g0, bnd, qpT, kp, vT):
        grid_spec = pltpu.PrefetchScalarGridSpec(
            num_scalar_prefetch=1, grid=(batch, N_HEADS, seq // tq),
            in_specs=[pl.BlockSpec((1, 1, SUBLANES, LANES), lambda b, h, i, g0: (b, h, 0, 0))] + qkv_specs,
            out_specs=out_spec)
        return pl.pallas_call(functools.partial(_attn_fixed_ref_kernel, tq=tq, tk=tk, unit=tm // tk),
                              grid_spec=grid_spec, out_shape=out_shape, compiler_params=params,
                              name="attention_fixed_ref")(g0, bnd, qpT, kp, vT)

    def running_max(g0, bnd, qpT, kp, vT):
        del bnd
        grid_spec = pltpu.PrefetchScalarGridSpec(
            num_scalar_prefetch=1, grid=(batch, N_HEADS, seq // tq), in_specs=qkv_specs, out_specs=out_spec,
            scratch_shapes=[pltpu.VMEM((tq // tk, tk, tq), F32), pltpu.VMEM((tq // tk, tk, tq), F32),
                            pltpu.VMEM((tq // tk, 1, tq), F32), pltpu.VMEM((tq // tk, 1, tq), F32)])
        return pl.pallas_call(functools.partial(_attn_kernel, tq=tq, tk=tk), grid_spec=grid_spec,
                              out_shape=out_shape, compiler_params=params, name="attention",
                              )(g0 // (tq // tm), qpT, kp, vT)

    bnd = jnp.broadcast_to(bound[:, :, None, None], (batch, N_HEADS, SUBLANES, LANES))
    return lax.cond(jnp.max(bound) <= FIXED_REF_MAX_BOUND, fixed_ref, running_max, g0, bnd, qpT, kp, vT)


def _outproj_kernel(x_ref, a_ref, yT_ref, wdw_ref, bdw_ref, lng_ref, lnb_ref, gc_ref, woc_ref, woa_ref, ga_ref,
                    g2_ref, wrT_ref, brT_ref, utri_ref, x1_ref, h2_ref, route_ref, cnt_ref,
                    carry_ref, sh_ref, acc_ref, *, tm, rc):
    first = (pl.program_id(0) == 0) & (pl.program_id(1) == 0)

    @pl.when(first)
    def _():
        carry_ref[...] = jnp.zeros_like(carry_ref)

    mc = _conv_tile(a_ref, wdw_ref, bdw_ref, lng_ref, lnb_ref, gc_ref, sh_ref, acc_ref, tc=tm, rc=rc)
    yT = yT_ref[0]
    msa = jnp.mean(yT * yT, axis=0, keepdims=True)
    yn = (yT * lax.rsqrt(msa + EPS)).T * ga_ref[...]
    x1 = x_ref[...] + _dot(mc, woc_ref[...]) + _dot(yn.astype(BF16), woa_ref[...])
    x1_ref[...] = x1
    ms = jnp.mean(x1 * x1, axis=-1, keepdims=True)
    h2 = (x1 * lax.rsqrt(ms + EPS)) * g2_ref[...]
    h2_ref[:, :D_MODEL] = h2

    lg = _dot_nt(wrT_ref[...], h2.astype(BF16)) + brT_ref[...]
    neg = -jnp.inf
    sub = lax.broadcasted_iota(jnp.int32, (SUBLANES, tm), 0)
    big = jnp.int32(SUBLANES)
    first_of = lambda hit: jnp.min(jnp.where(hit, sub, big), axis=0, keepdims=True)

    lg1 = lg[0:SUBLANES]
    m1 = jnp.max(lg1, axis=0, keepdims=True)
    p1_sel = 1.0 / jnp.sum(jnp.exp(lg1 - m1), axis=0, keepdims=True)
    grp = first_of(lg1 == m1)

    slab = lambda g: lg[SUBLANES * (g + 1):SUBLANES * (g + 2)]
    v = slab(N_GROUPS - 1)
    for g in range(N_GROUPS - 2, -1, -1):
        v = jnp.where(grp == g, slab(g), v)
    v1 = jnp.max(v, axis=0, keepdims=True)
    j1 = first_of(v == v1)
    vv = jnp.where(sub == j1, neg, v)
    v2 = jnp.max(vv, axis=0, keepdims=True)
    j2 = first_of(vv == v2)
    e21 = jnp.exp(v2 - v1)
    w0 = p1_sel / (1.0 + e21)
    w1 = p1_sel * e21 / (1.0 + e21)

    swap = j2 < j1
    al = jnp.where(swap, j2, j1)
    bl = jnp.where(swap, j1, j2)
    wa = jnp.where(swap, w1, w0)
    wb = jnp.where(swap, w0, w1)
    cid = PAIRS_PER_GROUP * grp + ((al * (2 * EXPERTS_PER_GROUP - 1 - al)) >> 1) + (bl - al - 1)

    cls = lax.broadcasted_iota(jnp.int32, (LANES, tm), 0)
    oh = cls == cid
    cmat = jnp.where(oh, 1.0, 0.0)
    prefix = _dot(cmat.astype(BF16), utri_ref[...]) + carry_ref[...]
    rank = jnp.sum(jnp.where(oh, prefix, 0.0), axis=0, keepdims=True)
    counts = prefix[:, tm - 1:tm] + cmat[:, tm - 1:tm]
    carry_ref[...] = counts
    cnt_ref[...] = counts

    h2_ref[:, D_MODEL:] = jnp.where(cls == 0, wa, jnp.where(cls == 1, wb, 0.0)).T
    route_ref[...] = jnp.where(sub == 0, cid.astype(F32), jnp.where(sub == 1, rank, 0.0))


def _outproj(x2, a, yT, conv_params, woc, woa, ga, g2, wrT, brT, *, batch, seq, tm, rc):
    nt = seq // tm
    utri = jnp.triu(jnp.ones((tm, tm), F32), 1).astype(BF16)
    const = lambda shape: pl.BlockSpec(shape, lambda b, t: (0,) * len(shape))
    row_spec = lambda w: pl.BlockSpec((tm, w), lambda b, t: (b * nt + t, 0))
    return pl.pallas_call(
        functools.partial(_outproj_kernel, tm=tm, rc=rc),
        grid=(batch, nt),
        in_specs=[row_spec(D_MODEL), row_spec(D_CONV),
                  pl.BlockSpec((1, D_ATT, tm), lambda b, t: (b, 0, t)),
                  const((CONV_WIDTH, D_CONV)), const((1, D_CONV)), const((1, D_CONV)), const((1, D_CONV)),
                  const((1, D_CONV)),
                  const((D_CONV, D_MODEL)), const((D_ATT, D_MODEL)), const((1, D_ATT)), const((1, D_MODEL)),
                  const((LANES, D_MODEL)), const((LANES, 1)), const((tm, tm))],
        out_specs=[row_spec(D_MODEL), row_spec(ROW_W),
                   pl.BlockSpec((SUBLANES, tm), lambda b, t: (0, b * nt + t)), const((LANES, 1))],
        out_shape=[jax.ShapeDtypeStruct((batch * seq, D_MODEL), F32),
                   jax.ShapeDtypeStruct((batch * seq, ROW_W), F32),
                   jax.ShapeDtypeStruct((SUBLANES, batch * seq), F32),
                   jax.ShapeDtypeStruct((LANES, 1), F32)],
        scratch_shapes=[pltpu.VMEM((LANES, 1), F32),
                        pltpu.VMEM((SUBLANES, D_CONV // LANES, tm + CONV_HALO, LANES), F32),
                        pltpu.VMEM((tm, D_CONV), F32)],
        compiler_params=pltpu.CompilerParams(
            dimension_semantics=("arbitrary", "arbitrary"), vmem_limit_bytes=VMEM_LIMIT),
        name="outproj",
    )(x2, a, yT, *conv_params, woc, woa, ga, g2, wrT, brT, utri)


def _row_copies(src_at, dst_at, sem):
    return pltpu.make_async_copy(src_at, dst_at, sem)


def _dispatch_kernel(dest_ref, h2_ref, xs_ref, sem, *, tm):
    def body(g, carry):
        for u in range(SUBLANES):
            _row_copies(h2_ref.at[g, pl.ds(u, 1)], xs_ref.at[pl.ds(dest_ref[g * SUBLANES + u], 1)],
                        sem).start(priority=u % 2)
        return carry

    lax.fori_loop(0, tm // SUBLANES, body, 0)
    done = xs_ref.at[pl.ds(0, tm)]
    _row_copies(done, done, sem).wait()


def _dispatch(dest, h2, *, tm):
    t_len = h2.shape[0]
    return pl.pallas_call(
        functools.partial(_dispatch_kernel, tm=tm),
        grid=(t_len // tm,),
        in_specs=[pl.BlockSpec((tm,), lambda i: (i,), memory_space=pltpu.SMEM),
                  pl.BlockSpec((tm // SUBLANES, SUBLANES, ROW_W), lambda i: (i, 0, 0))],
        out_specs=pl.BlockSpec(memory_space=pl.ANY),
        out_shape=jax.ShapeDtypeStruct((t_len, ROW_W), F32),
        scratch_shapes=[pltpu.SemaphoreType.DMA(())],
        compiler_params=pltpu.CompilerParams(dimension_semantics=("arbitrary",), vmem_limit_bytes=VMEM_LIMIT),
        name="dispatch",
    )(dest, h2.reshape(t_len // SUBLANES, SUBLANES, ROW_W))


def _expert_kernel(blk_ref, ea_ref, eb_ref, lo_ref, hi_ref, x_ref, wgua_ref, wda_ref, wgub_ref, wdb_ref, y_ref):
    del blk_ref, ea_ref, eb_ref
    i = pl.program_id(0)
    lo = lo_ref[i]
    hi = hi_ref[i]
    half = ROW_BLOCK // 2

    def mlp(xb, wgu_ref, wd_ref):
        gu = _dot(xb, wgu_ref[0])
        g = gu[:, :D_EXPERT]
        return _dot(((g * jax.nn.sigmoid(g)) * gu[:, D_EXPERT:]).astype(BF16), wd_ref[0])

    def run(rows):
        n = rows.stop - rows.start
        xb = x_ref[rows, :D_MODEL].astype(BF16)
        wts = x_ref[rows, D_MODEL:]
        y = wts[:, 0:1] * mlp(xb, wgua_ref, wda_ref) + wts[:, 1:2] * mlp(xb, wgub_ref, wdb_ref)
        row = rows.start + lax.broadcasted_iota(jnp.int32, (n, 1), 0)
        mine = (row >= lo) & (row < hi)

        @pl.when(lo == 0)
        def _():
            y_ref[rows, :] = jnp.where(mine, y, 0.0)
            if n < ROW_BLOCK:
                y_ref[half:, :] = jnp.zeros((half, D_MODEL), F32)

        @pl.when(lo > 0)
        def _():
            y_ref[rows, :] = jnp.where(mine, y, y_ref[rows, :])

    @pl.when((hi > lo) & (lo < half) & (hi > half))
    def _():
        run(slice(0, ROW_BLOCK))

    @pl.when((hi > lo) & (hi <= half))
    def _():
        run(slice(0, half))

    @pl.when((hi > lo) & (lo >= half))
    def _():
        run(slice(half, ROW_BLOCK))


def _experts(items, xs, wgu, wd):
    n_rows = xs.shape[0]
    n_items = items[0].shape[0]
    wspec = lambda shape, which: pl.BlockSpec(
        shape, lambda i, blk, ea, eb, lo, hi: ((ea, eb)[which][i], 0, 0))
    grid_spec = pltpu.PrefetchScalarGridSpec(
        num_scalar_prefetch=5,
        grid=(n_items,),
        in_specs=[pl.BlockSpec((ROW_BLOCK, ROW_W), lambda i, blk, ea, eb, lo, hi: (blk[i], 0)),
                  wspec((1, D_MODEL, 2 * D_EXPERT), 0), wspec((1, D_EXPERT, D_MODEL), 0),
                  wspec((1, D_MODEL, 2 * D_EXPERT), 1), wspec((1, D_EXPERT, D_MODEL), 1)],
        out_specs=pl.BlockSpec((ROW_BLOCK, D_MODEL), lambda i, blk, ea, eb, lo, hi: (blk[i], 0)),
    )
    return pl.pallas_call(
        _expert_kernel,
        grid_spec=grid_spec,
        out_shape=jax.ShapeDtypeStruct((n_rows, D_MODEL), F32),
        compiler_params=pltpu.CompilerParams(dimension_semantics=("arbitrary",), vmem_limit_bytes=VMEM_LIMIT),
        name="experts",
    )(*items, xs, wgu, wd, wgu, wd)


def _combine_kernel(dcur_ref, dnxt_ref, x1_ref, g_ref, ys_ref, out_ref, buf_ref, sems, *, tm):
    i = pl.program_id(0)
    slot = i % 2

    def issue(d_ref, s):
        def body(g, carry):
            for u in range(SUBLANES):
                _row_copies(ys_ref.at[pl.ds(d_ref[g * SUBLANES + u], 1)], buf_ref.at[s, g, pl.ds(u, 1)],
                            sems.at[s]).start(priority=u % 2)
            return carry

        lax.fori_loop(0, tm // SUBLANES, body, 0)

    @pl.when(i == 0)
    def _():
        issue(dcur_ref, 0)

    @pl.when(i + 1 < pl.num_programs(0))
    def _():
        issue(dnxt_ref, 1 - slot)

    _row_copies(buf_ref.at[slot], buf_ref.at[slot], sems.at[slot]).wait()
    x2 = x1_ref[...] + buf_ref[slot]
    ms = jnp.mean(x2 * x2, axis=-1, keepdims=True)
    out_ref[...] = (x2 * lax.rsqrt(ms + EPS)) * g_ref[...]


def _combine(dest, x1, gf, ys, *, tm):
    t_len = x1.shape[0]
    n = t_len // tm
    return pl.pallas_call(
        functools.partial(_combine_kernel, tm=tm),
        grid=(n,),
        in_specs=[pl.BlockSpec((tm,), lambda i: (i,), memory_space=pltpu.SMEM),
                  pl.BlockSpec((tm,), lambda i: (jnp.minimum(i + 1, n - 1),), memory_space=pltpu.SMEM),
                  pl.BlockSpec((tm // SUBLANES, SUBLANES, D_MODEL), lambda i: (i, 0, 0)),
                  pl.BlockSpec((1, D_MODEL), lambda i: (0, 0)),
                  pl.BlockSpec(memory_space=pl.ANY)],
        out_specs=pl.BlockSpec((tm // SUBLANES, SUBLANES, D_MODEL), lambda i: (i, 0, 0)),
        out_shape=jax.ShapeDtypeStruct((t_len // SUBLANES, SUBLANES, D_MODEL), F32),
        scratch_shapes=[pltpu.VMEM((2, tm // SUBLANES, SUBLANES, D_MODEL), F32), pltpu.SemaphoreType.DMA((2,))],
        compiler_params=pltpu.CompilerParams(dimension_semantics=("arbitrary",), vmem_limit_bytes=VMEM_LIMIT),
        name="combine",
    )(dest, dest, x1.reshape(t_len // SUBLANES, SUBLANES, D_MODEL), gf, ys)


def _layer(x, norm1_g, w_in, b_f, w_dw, b_dw, conv_ln_g, conv_ln_b, out_g_conv, out_g_att, w_out,
           norm2_g, w_r1, b_r1, w_r2, b_r2, w_gate, w_up, w_down):
    batch, seq, d = x.shape
    t_len = batch * seq
    tm = min(512, seq)
    tk = min(256, seq)
    tq = min(1024, seq)
    o1, o2 = D_CONV, 2 * D_CONV
    o3, o4, o5 = o2 + D_ATT, o2 + 2 * D_ATT, o2 + 3 * D_ATT

    wvg = w_in[:, :o2].astype(BF16)
    wk = w_in[:, o3:o4].astype(BF16)
    wqT = (w_in[:, o2:o3] * (HEAD_DIM ** -0.5)).T.astype(BF16)
    wvT = w_in[:, o4:o5].T.astype(BF16)
    assert PIECE_STRIDE == N_HEADS
    pad = LANES - N_PIECES * PIECE_STRIDE
    wf = jnp.pad(jnp.tile(w_in[:, o5:], (1, N_PIECES)), ((0, 0), (0, pad))).astype(BF16)
    bf3 = jnp.pad(jnp.tile(b_f.astype(F32).reshape(1, N_HEADS), (1, N_PIECES)), ((0, 0), (0, pad)))

    x2 = x.reshape(t_len, d)
    a, kp, qpT, vT, kstat, qstat = _inproj(x2, norm1_g.reshape(1, d), wvg, wk, wf, bf3, wqT, wvT,
                                           batch=batch, seq=seq, tm=tm, tk=tk)
    conv_params = (w_dw, b_dw.reshape(1, -1), conv_ln_g.reshape(1, -1), conv_ln_b.reshape(1, -1),
                   out_g_conv.reshape(1, -1))
    g0, bound = _first_groups(kstat, qstat, tm=tm, tq=tq)
    yT = _attention(g0, bound, qpT, kp, vT, batch=batch, seq=seq, tm=tm, tq=tq, tk=tk)

    gpad = SUBLANES - N_GROUPS
    rpad = LANES - SUBLANES - N_EXPERTS
    wrT = jnp.concatenate([w_r1.T, jnp.zeros((gpad, d), F32),
                           jnp.transpose(w_r2, (0, 2, 1)).reshape(N_EXPERTS, d), jnp.zeros((rpad, d), F32)], axis=0)
    brT = jnp.concatenate([b_r1.astype(F32), jnp.full((gpad,), -jnp.inf, F32), b_r2.reshape(-1).astype(F32),
                           jnp.zeros((rpad,), F32)]).reshape(LANES, 1)
    x1, h2, route, cnt = _outproj(x2, a, yT, conv_params, w_out[:D_CONV].astype(BF16),
                                  w_out[D_CONV:].astype(BF16), out_g_att.reshape(1, -1), norm2_g.reshape(1, d),
                                  wrT.astype(BF16), brT, batch=batch, seq=seq, tm=tm, rc=min(128, seq))

    i32 = jnp.int32
    lanes = jnp.arange(LANES, dtype=i32)
    pick = lambda table, idx: jnp.sum(jnp.where(idx[:, None] == lanes, table[None, :], 0), axis=1).astype(i32)
    counts = cnt[:, 0].astype(i32)
    ends = jnp.cumsum(counts).astype(i32)
    starts = ends - counts
    dest = pick(starts, route[0].astype(i32)) + route[1].astype(i32)

    n_blocks = t_len // ROW_BLOCK
    b_lo = starts // ROW_BLOCK
    n_it = jnp.where(counts > 0, (ends - 1) // ROW_BLOCK - b_lo + 1, 0)
    it_end = jnp.cumsum(n_it).astype(i32)
    it_start = it_end - n_it
    idx = jnp.arange(n_blocks + N_CLASSES, dtype=i32)
    valid = idx < it_end[-1]
    last_cls = jnp.max(jnp.where(counts > 0, lanes, 0))
    cls = jnp.where(valid, jnp.sum(it_end[None, :] <= idx[:, None], axis=1).astype(i32), last_cls)
    blk = jnp.where(valid, pick(b_lo, cls) + idx - pick(it_start, cls), n_blocks - 1)
    row0 = blk * ROW_BLOCK
    lo = jnp.where(valid, jnp.maximum(pick(starts, cls), row0) - row0, 0)
    hi = jnp.where(valid, jnp.minimum(pick(ends, cls), row0 + ROW_BLOCK) - row0, 0)
    pair_a, pair_b = [], []
    for g in range(N_GROUPS):
        for a_loc in range(EXPERTS_PER_GROUP):
            for b_loc in range(a_loc + 1, EXPERTS_PER_GROUP):
                pair_a.append(g * EXPERTS_PER_GROUP + a_loc)
                pair_b.append(g * EXPERTS_PER_GROUP + b_loc)
    cpad = [0] * (LANES - N_CLASSES)
    items = (blk.astype(i32), pick(jnp.array(pair_a + cpad, i32), cls), pick(jnp.array(pair_b + cpad, i32), cls),
             lo.astype(i32), hi.astype(i32))

    xs = _dispatch(dest, h2, tm=min(1024, t_len))
    ys = _experts(items, xs, jnp.concatenate([w_gate, w_up], axis=-1).astype(BF16), w_down.astype(BF16))
    return dest, x1, ys


def kernel(x, norm1_g, w_in, b_f, w_dw, b_dw, conv_ln_g, conv_ln_b, out_g_conv, out_g_att, w_out, norm2_g,
           w_r1, b_r1, w_r2, b_r2, w_gate, w_up, w_down, final_g):
    assert norm1_g.shape[0] == 1, "single-layer stack"
    batch, seq, d = x.shape
    dest, x1, ys = _layer(
        x, norm1_g[0], w_in[0], b_f[0], w_dw[0], b_dw[0], conv_ln_g[0], conv_ln_b[0], out_g_conv[0],
        out_g_att[0], w_out[0], norm2_g[0], w_r1[0], b_r1[0], w_r2[0], b_r2[0], w_gate[0], w_up[0], w_down[0])
    out = _combine(dest, x1, final_g.reshape(1, d), ys, tm=min(512, batch * seq))
    return out.reshape(batch, seq, d)
```

```python
import functools

import jax
import jax.numpy as jnp
from jax import lax
from jax.experimental import pallas as pl
from jax.experimental.pallas import tpu as pltpu

D_MODEL = 1024
D_CONV = 512
N_HEADS = 8
HEAD_DIM = 64
D_ATT = N_HEADS * HEAD_DIM
CONV_WIDTH = 31
N_GROUPS = 4
EXPERTS_PER_GROUP = 8
N_EXPERTS = N_GROUPS * EXPERTS_PER_GROUP
D_EXPERT = D_MODEL // 4
ROW_BLOCK = 256
EPS = 1e-6

LANES = 128
SUBLANES = 8
KAUG = 128
N_PIECES = 3
PIECE_STRIDE = 8
CONV_HALO = 32
PAIRS_PER_GROUP = EXPERTS_PER_GROUP * (EXPERTS_PER_GROUP - 1) // 2
N_CLASSES = N_GROUPS * PAIRS_PER_GROUP
ROW_W = D_MODEL + LANES
SKIP_LOG2 = 160.0
FIXED_REF_MAX_BOUND = 40.0
NORM_SLACK = 1.02
V_ROWS = 80
LOG2E = 1.4426950408889634
VMEM_LIMIT = 56 * 1024 * 1024

F32 = jnp.float32
BF16 = jnp.bfloat16


def _dot(a, b):
    return jnp.dot(a, b, preferred_element_type=F32)


def _dot_nt(a, b):
    return lax.dot_general(a, b, (((1,), (1,)), ((), ())), preferred_element_type=F32)


def _split3(x):
    hi = x.astype(BF16)
    r1 = x - hi.astype(F32)
    mid = r1.astype(BF16)
    lo = (r1 - mid.astype(F32)).astype(BF16)
    return hi.astype(F32), mid.astype(F32), lo.astype(F32)


def _piece_lane_mask(lane, h):
    return (lane == h) | (lane == h + PIECE_STRIDE) | (lane == h + 2 * PIECE_STRIDE)


def _inproj_kernel(x_ref, g1_ref, wrow_ref, bf_ref, wqvT_ref, ltri_ref, hsel_ref,
                   a_ref, kp_ref, qpT_ref, vT_ref, kstat_ref, qstat_ref, carry_ref, *, tm, tk):
    @pl.when(pl.program_id(1) == 0)
    def _():
        carry_ref[...] = jnp.zeros_like(carry_ref)

    x = x_ref[...]
    ms = jnp.mean(x * x, axis=-1, keepdims=True)
    hb = ((x * lax.rsqrt(ms + EPS)) * g1_ref[...]).astype(BF16)

    z = _dot(hb, wrow_ref[...])
    a_ref[...] = z[:, :D_CONV] * jax.nn.sigmoid(z[:, D_CONV:2 * D_CONV])

    kk = z[:, 2 * D_CONV:2 * D_CONV + D_ATT]

    zf = z[:, 2 * D_CONV + D_ATT:] + bf_ref[...]
    lf = jnp.minimum(zf, 0.0) - jnp.log1p(jnp.exp(-jnp.abs(zf)))
    lane = lax.broadcasted_iota(jnp.int32, (tm, LANES), 1)
    hi, mid, lo = _split3(lf)
    lf3 = jnp.where(lane < PIECE_STRIDE, hi,
                    jnp.where(lane < 2 * PIECE_STRIDE, mid,
                              jnp.where(lane < 3 * PIECE_STRIDE, lo, 0.0))).astype(BF16)
    cs3 = _dot(ltri_ref[...], lf3)
    c = (cs3 + pltpu.roll(cs3, LANES - PIECE_STRIDE, 1)
         + pltpu.roll(cs3, LANES - 2 * PIECE_STRIDE, 1)) + carry_ref[...]
    carry_ref[...] = c[tm - 1:tm, :]

    nhi, nmid, nlo = _split3(c * (-LOG2E))
    p3 = jnp.where(lane < PIECE_STRIDE, nhi,
                   jnp.where(lane < 2 * PIECE_STRIDE, pltpu.roll(nmid, PIECE_STRIDE, 1),
                             jnp.where(lane < 3 * PIECE_STRIDE, pltpu.roll(nlo, 2 * PIECE_STRIDE, 1), 0.0)))
    p3_hi = pltpu.roll(p3, HEAD_DIM, 1)
    q_pieces = (-p3).T[0:HEAD_DIM]
    q_shift = N_PIECES * PIECE_STRIDE

    qvT = _dot_nt(wqvT_ref[...], hb)
    qT = qvT[:D_ATT] * LOG2E
    vT = qvT[D_ATT:]
    row = lax.broadcasted_iota(jnp.int32, (HEAD_DIM, tm), 0)
    vrow = lax.broadcasted_iota(jnp.int32, (V_ROWS - HEAD_DIM, tk), 0)
    v_tail = jnp.where(vrow == 0, 1.0, 0.0).astype(BF16)

    for h in range(N_HEADS):
        kcol = kk[:, (h // 2) * LANES:(h // 2 + 1) * LANES]
        f_h = jnp.where(_piece_lane_mask(row, h), q_pieces, 0.0)
        aug_q = jnp.where(_piece_lane_mask(row, h), 1.0, 0.0) + jnp.concatenate(
            [jnp.zeros((q_shift, tm), F32), f_h[0:HEAD_DIM - q_shift]], axis=0)
        q_h = qT[h * HEAD_DIM:(h + 1) * HEAD_DIM, :]
        if h % 2 == 0:
            ext = jnp.where(_piece_lane_mask(lane, h + HEAD_DIM), p3_hi,
                            jnp.where(_piece_lane_mask(lane, h + HEAD_DIM + q_shift), 1.0, 0.0))
            kp = jnp.where(lane < HEAD_DIM, kcol, ext)
            qp = jnp.concatenate([q_h, aug_q], axis=0)
        else:
            ext = jnp.where(_piece_lane_mask(lane, h), p3,
                            jnp.where(_piece_lane_mask(lane, h + q_shift), 1.0, 0.0))
            kp = jnp.where(lane >= HEAD_DIM, kcol, ext)
            qp = jnp.concatenate([aug_q, q_h], axis=0)
        qpT_ref[0, h] = qp.astype(BF16)
        for cidx in range(tm // tk):
            kp_ref[0, h, cidx] = kp[cidx * tk:(cidx + 1) * tk, :].astype(BF16)
            v_h = vT[h * HEAD_DIM:(h + 1) * HEAD_DIM, cidx * tk:(cidx + 1) * tk].astype(BF16)
            vT_ref[0, h, cidx] = jnp.concatenate([v_h, v_tail], axis=0)

    kn2 = jnp.max(_dot((kk * kk).astype(BF16), hsel_ref[...]), axis=0, keepdims=True)
    srow = lax.broadcasted_iota(jnp.int32, (SUBLANES, LANES), 0)
    kstat_ref[0, 0] = jnp.where(srow == 0, c[0:1, :], jnp.where(srow == 1, c[tm - 1:tm, :],
                                                                 jnp.where(srow == 2, kn2, 0.0)))
    q2 = qT * qT
    qn2 = jnp.concatenate([jnp.sum(q2[h * HEAD_DIM:(h + 1) * HEAD_DIM, :], axis=0, keepdims=True)
                           for h in range(N_HEADS)], axis=0)
    qstat_ref[0, 0] = jnp.broadcast_to(jnp.max(qn2, axis=1, keepdims=True), (N_HEADS, LANES))


def _inproj(x2, g1, wrow, bf3, wqvT, *, batch, seq, tm, tk):
    nt = seq // tm
    nk = seq // tk
    ltri = jnp.tril(jnp.ones((tm, tm), F32)).astype(BF16)
    hsel = (jnp.arange(D_ATT)[:, None] // HEAD_DIM == jnp.arange(LANES)[None, :]).astype(BF16)
    const = lambda shape: pl.BlockSpec(shape, lambda b, t: (0,) * len(shape))
    stat_spec = pl.BlockSpec((1, 1, SUBLANES, LANES), lambda b, t: (b, t, 0, 0))
    stat_shape = jax.ShapeDtypeStruct((batch, nt, SUBLANES, LANES), F32)
    return pl.pallas_call(
        functools.partial(_inproj_kernel, tm=tm, tk=tk),
        grid=(batch, nt),
        in_specs=[
            pl.BlockSpec((tm, D_MODEL), lambda b, t: (b * nt + t, 0)),
            const((1, D_MODEL)), const((D_MODEL, 2 * D_CONV + D_ATT + LANES)), const((1, LANES)),
            const((2 * D_ATT, D_MODEL)), const((tm, tm)), const((D_ATT, LANES)),
        ],
        out_specs=[
            pl.BlockSpec((tm, D_CONV), lambda b, t: (b * nt + t, 0)),
            pl.BlockSpec((1, N_HEADS, tm // tk, tk, KAUG), lambda b, t: (b, 0, t, 0, 0)),
            pl.BlockSpec((1, N_HEADS, KAUG, tm), lambda b, t: (b, 0, 0, t)),
            pl.BlockSpec((1, N_HEADS, tm // tk, V_ROWS, tk), lambda b, t: (b, 0, t, 0, 0)),
            stat_spec, stat_spec,
        ],
        out_shape=[
            jax.ShapeDtypeStruct((batch * seq, D_CONV), F32),
            jax.ShapeDtypeStruct((batch, N_HEADS, nk, tk, KAUG), BF16),
            jax.ShapeDtypeStruct((batch, N_HEADS, KAUG, seq), BF16),
            jax.ShapeDtypeStruct((batch, N_HEADS, nk, V_ROWS, tk), BF16),
            stat_shape, stat_shape,
        ],
        scratch_shapes=[pltpu.VMEM((1, LANES), F32)],
        compiler_params=pltpu.CompilerParams(
            dimension_semantics=("arbitrary", "arbitrary"), vmem_limit_bytes=VMEM_LIMIT),
        name="inproj",
    )(x2, g1, wrow, bf3, wqvT, ltri, hsel)


def _conv_tile(a_ref, w_ref, b_ref, lng_ref, lnb_ref, og_ref, sh_ref, acc_ref, *, tc, rc):
    n_cb = D_CONV // LANES

    @pl.when(pl.program_id(1) == 0)
    def _():
        for cb in range(n_cb):
            sh_ref[0, cb, 0:CONV_HALO, :] = jnp.zeros((CONV_HALO, LANES), F32)

    @pl.when(pl.program_id(1) > 0)
    def _():
        for cb in range(n_cb):
            sh_ref[0, cb, 0:CONV_HALO, :] = sh_ref[0, cb, tc:tc + CONV_HALO, :]

    n_sh = tc + CONV_HALO - SUBLANES
    for cb in range(n_cb):
        sh_ref[0, cb, CONV_HALO:CONV_HALO + tc, :] = a_ref[:, cb * LANES:(cb + 1) * LANES]
        for f in range(1, SUBLANES):
            sh_ref[f, cb, 0:n_sh, :] = sh_ref[0, cb, f:f + n_sh, :]

    base = CONV_HALO - (CONV_WIDTH - 1)
    for cb in range(n_cb):
        cols = slice(cb * LANES, (cb + 1) * LANES)

        def chunk(c, carry, cb=cb, cols=cols):
            r0 = pl.multiple_of(c * rc, rc)
            acc = jnp.zeros((rc, LANES), F32)
            for j in range(CONV_WIDTH):
                f = (base + j) % SUBLANES
                acc = acc + w_ref[j:j + 1, cols] * sh_ref[f, cb, pl.ds(r0 + (base + j - f), rc), :]
            acc_ref[pl.ds(r0, rc), cols] = acc
            return carry

        lax.fori_loop(0, tc // rc, chunk, 0)

    y = acc_ref[...] + b_ref[...]
    mu = jnp.mean(y, axis=-1, keepdims=True)
    yc = y - mu
    var = jnp.mean(yc * yc, axis=-1, keepdims=True)
    yn = yc * lax.rsqrt(var + EPS) * lng_ref[...] + lnb_ref[...]
    s = yn * jax.nn.sigmoid(yn)
    ms = jnp.mean(s * s, axis=-1, keepdims=True)
    return (s * lax.rsqrt(ms + EPS) * og_ref[...]).astype(BF16)


def _attn_fixed_ref_kernel(g0_ref, bnd_ref, qT_ref, k_ref, v_ref, o_ref, *, tq, tk, unit):
    i = pl.program_id(2)
    qT = qT_ref[0, 0]
    n_sub = tq // tk
    bnd = bnd_ref[0, 0, 0:1, 0:1]
    sb0 = unit * g0_ref[(pl.program_id(0) * N_HEADS + pl.program_id(1)) * pl.num_programs(2) + i]

    def values(sb, n):
        return jnp.concatenate([v_ref[0, 0, sb + d] for d in range(n)], axis=1)

    def probs(sb, n):
        return jnp.concatenate([jnp.exp2(_dot(k_ref[0, 0, sb + d], qT) - bnd).astype(BF16)
                                for d in range(n)], axis=0)

    def two_groups(p, acc):
        sb = sb0 + 2 * n_sub * p
        acc = acc + _dot(values(sb, n_sub), probs(sb, n_sub))
        return acc + _dot(values(sb + n_sub, n_sub), probs(sb + n_sub, n_sub))

    key = lax.broadcasted_iota(jnp.int32, (tk, tk), 0)
    qry = lax.broadcasted_iota(jnp.int32, (tk, tk), 1)

    def diagonal(acc):
        rows = []
        for d in range(n_sub):
            s = _dot(k_ref[0, 0, i * n_sub + d], qT[:, d * tk:])
            tiles = [jnp.zeros((tk, d * tk), BF16)] if d else []
            tiles.append(jnp.exp2(jnp.where(key <= qry, s[:, :tk], -jnp.inf) - bnd).astype(BF16))
            if d + 1 < n_sub:
                tiles.append(jnp.exp2(s[:, tk:] - bnd).astype(BF16))
            rows.append(jnp.concatenate(tiles, axis=1))
        acc = acc + _dot(values(i * n_sub, n_sub), jnp.concatenate(rows, axis=0))
        o_ref[0] = acc[:HEAD_DIM] / acc[HEAD_DIM:HEAD_DIM + 1]

    n_before = i * n_sub - sb0
    acc = lax.fori_loop(0, n_before // (2 * n_sub), two_groups, jnp.zeros((V_ROWS, tq), F32))
    left = n_before % (2 * n_sub)

    for n in range(0, 2 * n_sub, unit):
        @pl.when(left == n)
        def _(n=n):
            diagonal(acc + _dot(values(i * n_sub - n, n), probs(i * n_sub - n, n)) if n else acc)


def _attn_kernel(g0_ref, qT_ref, k_ref, v_ref, o_ref, sa_ref, sb_ref, cma_ref, cmb_ref, *, tq, tk):
    i = pl.program_id(2)
    qT = qT_ref[0, 0]
    n_sub = tq // tk

    def scores(g, s_ref, cm_ref):
        for d in range(n_sub):
            s = _dot(k_ref[0, 0, g * n_sub + d], qT)
            s_ref[d] = s
            cm_ref[d] = jnp.max(s, axis=0, keepdims=True)

    def consume(g, s_ref, cm_ref, carry, nxt=None):
        m, acc = carry
        m_new = m
        for d in range(n_sub):
            m_new = jnp.maximum(m_new, cm_ref[d])
        ps = []
        for d in range(n_sub):
            if nxt is not None:
                g_n, sn_ref, cmn_ref = nxt
                s = _dot(k_ref[0, 0, g_n * n_sub + d], qT)
                sn_ref[d] = s
                cmn_ref[d] = jnp.max(s, axis=0, keepdims=True)
            ps.append(jnp.exp2(s_ref[d] - m_new).astype(BF16))
        p = jnp.concatenate(ps, axis=0)
        v = jnp.concatenate([v_ref[0, 0, g * n_sub + d] for d in range(n_sub)], axis=1)
        return m_new, jnp.exp2(m - m_new) * acc + _dot(v, p)

    key = lax.broadcasted_iota(jnp.int32, (tk, tk), 0)
    qry = lax.broadcasted_iota(jnp.int32, (tk, tk), 1)

    def consume_diag(s_ref, cm_ref, carry):
        m, acc = carry
        lanes = lambda j: slice(j * tk, (j + 1) * tk)
        causal = [jnp.where(key <= qry, s_ref[d, :, lanes(d)], -jnp.inf) for d in range(n_sub)]
        m_tiles = []
        for j in range(n_sub):
            m_j = jnp.maximum(m[:, lanes(j)], jnp.max(causal[j], axis=0, keepdims=True))
            for d in range(j):
                m_j = jnp.maximum(m_j, cm_ref[d, :, lanes(j)])
            m_tiles.append(m_j)
        m_new = jnp.concatenate(m_tiles, axis=1)
        rows = []
        for d in range(n_sub):
            tiles = [jnp.zeros((tk, d * tk), BF16)] if d else []
            tiles.append(jnp.exp2(causal[d] - m_tiles[d]).astype(BF16))
            if d + 1 < n_sub:
                rest = slice((d + 1) * tk, tq)
                tiles.append(jnp.exp2(s_ref[d, :, rest] - m_new[:, rest]).astype(BF16))
            rows.append(jnp.concatenate(tiles, axis=1))
        p = jnp.concatenate(rows, axis=0)
        v = jnp.concatenate([v_ref[0, 0, i * n_sub + d] for d in range(n_sub)], axis=1)
        acc = jnp.exp2(m - m_new) * acc + _dot(v, p)
        o_ref[0] = acc[:HEAD_DIM] / acc[HEAD_DIM:HEAD_DIM + 1]

    g0 = g0_ref[(pl.program_id(0) * N_HEADS + pl.program_id(1)) * pl.num_programs(2) + i]
    n_full = i - g0

    def pair(p, carry):
        g = g0 + 2 * p
        carry = consume(g, sa_ref, cma_ref, carry, nxt=(g + 1, sb_ref, cmb_ref))
        return consume(g + 1, sb_ref, cmb_ref, carry, nxt=(g + 2, sa_ref, cma_ref))

    scores(g0, sa_ref, cma_ref)
    init = (jnp.full((1, tq), -jnp.inf, F32), jnp.zeros((V_ROWS, tq), F32))
    carry = lax.fori_loop(0, n_full // 2, pair, init)

    @pl.when(n_full % 2 == 1)
    def _():
        consume_diag(sb_ref, cmb_ref, consume(i - 1, sa_ref, cma_ref, carry, nxt=(i, sb_ref, cmb_ref)))

    @pl.when(n_full % 2 == 0)
    def _():
        consume_diag(sa_ref, cma_ref, carry)


def _first_groups(kstat, qstat, *, tm, tq):
    per = tq // tm
    bound = jnp.sqrt(jnp.max(kstat[:, :, 2, :N_HEADS], axis=1) * jnp.max(qstat[:, :, :, 0], axis=1))
    bound = bound * NORM_SLACK
    cq = kstat[:, ::per, 0, :N_HEADS]
    ck = kstat[:, :, 1, :N_HEADS]
    d = (cq[:, :, None, :] - ck[:, None, :, :]) * LOG2E
    nq, ng = cq.shape[1], ck.shape[1]
    earlier = per * jnp.arange(nq)[None, :, None, None] > jnp.arange(ng)[None, None, :, None]
    skip = earlier & (2.0 * bound[:, None, None, :] + d < -SKIP_LOG2)
    g0 = jnp.sum(jnp.cumprod(skip.astype(jnp.int32), axis=2), axis=2)
    return jnp.transpose(g0, (0, 2, 1)).reshape(-1).astype(jnp.int32), bound


def _attention(g0, bound, qpT, kp, vT, *, batch, seq, tm, tq, tk):
    nk = seq // tk
    params = pltpu.CompilerParams(
        dimension_semantics=("arbitrary", "arbitrary", "arbitrary"), vmem_limit_bytes=VMEM_LIMIT)
    out_shape = jax.ShapeDtypeStruct((batch, D_ATT, seq), F32)
    qkv_specs = [
        pl.BlockSpec((1, 1, KAUG, tq), lambda b, h, i, g0: (b, h, 0, i)),
        pl.BlockSpec((1, 1, nk, tk, KAUG), lambda b, h, i, g0: (b, h, 0, 0, 0)),
        pl.BlockSpec((1, 1, nk, V_ROWS, tk), lambda b, h, i, g0: (b, h, 0, 0, 0)),
    ]
    out_spec = pl.BlockSpec((1, HEAD_DIM, tq), lambda b, h, i, g0: (b, h, i))

    def fixed_ref(g0, bnd, qpT, kp, vT):
        grid_spec = pltpu.PrefetchScalarGridSpec(
            num_scalar_prefetch=1, grid=(batch, N_HEADS, seq // tq),
            in_specs=[pl.BlockSpec((1, 1, SUBLANES, LANES), lambda b, h, i, g0: (b, h, 0, 0))] + qkv_specs,
            out_specs=out_spec)
        return pl.pallas_call(functools.partial(_attn_fixed_ref_kernel, tq=tq, tk=tk, unit=tm // tk),
                              grid_spec=grid_spec, out_shape=out_shape, compiler_params=params,
                              name="attention_fixed_ref")(g0, bnd, qpT, kp, vT)

    def running_max(g0, bnd, qpT, kp, vT):
        del bnd
        grid_spec = pltpu.PrefetchScalarGridSpec(
            num_scalar_prefetch=1, grid=(batch, N_HEADS, seq // tq), in_specs=qkv_specs, out_specs=out_spec,
            scratch_shapes=[pltpu.VMEM((tq // tk, tk, tq), F32), pltpu.VMEM((tq // tk, tk, tq), F32),
                            pltpu.VMEM((tq // tk, 1, tq), F32), pltpu.VMEM((tq // tk, 1, tq), F32)])
        return pl.pallas_call(functools.partial(_attn_kernel, tq=tq, tk=tk), grid_spec=grid_spec,
                              out_shape=out_shape, compiler_params=params, name="attention",
                              )(g0 // (tq // tm), qpT, kp, vT)

    bnd = jnp.broadcast_to(bound[:, :, None, None], (batch, N_HEADS, SUBLANES, LANES))
    return lax.cond(jnp.max(bound) <= FIXED_REF_MAX_BOUND, fixed_ref, running_max, g0, bnd, qpT, kp, vT)


def _outproj_kernel(x_ref, a_ref, yT_ref, wdw_ref, bdw_ref, lng_ref, lnb_ref, gc_ref, wo_ref, ga_ref,
                    g2_ref, wrT_ref, brT_ref, utri_ref, x1_ref, h2_ref, route_ref, cnt_ref,
                    carry_ref, sh_ref, acc_ref, *, tm, rc):
    first = (pl.program_id(0) == 0) & (pl.program_id(1) == 0)

    @pl.when(first)
    def _():
        carry_ref[...] = jnp.zeros_like(carry_ref)

    mc = _conv_tile(a_ref, wdw_ref, bdw_ref, lng_ref, lnb_ref, gc_ref, sh_ref, acc_ref, tc=tm, rc=rc)
    yT = yT_ref[0]
    msa = jnp.mean(yT * yT, axis=0, keepdims=True)
    yn = (yT * lax.rsqrt(msa + EPS)).T * ga_ref[...]
    x1 = x_ref[...] + _dot(jnp.concatenate([mc, yn.astype(BF16)], axis=1), wo_ref[...])
    x1_ref[...] = x1
    ms = jnp.mean(x1 * x1, axis=-1, keepdims=True)
    h2 = (x1 * lax.rsqrt(ms + EPS)) * g2_ref[...]
    h2_ref[:, :D_MODEL] = h2

    lg = _dot_nt(wrT_ref[...], h2.astype(BF16)) + brT_ref[...]
    neg = -jnp.inf
    sub = lax.broadcasted_iota(jnp.int32, (SUBLANES, tm), 0)
    big = jnp.int32(SUBLANES)
    first_of = lambda hit: jnp.min(jnp.where(hit, sub, big), axis=0, keepdims=True)

    lg1 = lg[0:SUBLANES]
    m1 = jnp.max(lg1, axis=0, keepdims=True)
    p1_sel = 1.0 / jnp.sum(jnp.exp(lg1 - m1), axis=0, keepdims=True)
    grp = first_of(lg1 == m1)

    slab = lambda g: lg[SUBLANES * (g + 1):SUBLANES * (g + 2)]
    v = slab(N_GROUPS - 1)
    for g in range(N_GROUPS - 2, -1, -1):
        v = jnp.where(grp == g, slab(g), v)
    v1 = jnp.max(v, axis=0, keepdims=True)
    j1 = first_of(v == v1)
    vv = jnp.where(sub == j1, neg, v)
    v2 = jnp.max(vv, axis=0, keepdims=True)
    j2 = first_of(vv == v2)
    e21 = jnp.exp(v2 - v1)
    w0 = p1_sel / (1.0 + e21)
    w1 = p1_sel * e21 / (1.0 + e21)

    swap = j2 < j1
    al = jnp.where(swap, j2, j1)
    bl = jnp.where(swap, j1, j2)
    wa = jnp.where(swap, w1, w0)
    wb = jnp.where(swap, w0, w1)
    cid = PAIRS_PER_GROUP * grp + ((al * (2 * EXPERTS_PER_GROUP - 1 - al)) >> 1) + (bl - al - 1)

    cls = lax.broadcasted_iota(jnp.int32, (LANES, tm), 0)
    oh = cls == cid
    cmat = jnp.where(oh, 1.0, 0.0)
    prefix = _dot(cmat.astype(BF16), utri_ref[...]) + carry_ref[...]
    rank = jnp.sum(jnp.where(oh, prefix, 0.0), axis=0, keepdims=True)
    counts = prefix[:, tm - 1:tm] + cmat[:, tm - 1:tm]
    carry_ref[...] = counts
    cnt_ref[...] = counts

    h2_ref[:, D_MODEL:] = jnp.where(cls == 0, wa, jnp.where(cls == 1, wb, 0.0)).T
    route_ref[...] = jnp.where(sub == 0, cid.astype(F32), jnp.where(sub == 1, rank, 0.0))


def _outproj(x2, a, yT, conv_params, wo, ga, g2, wrT, brT, *, batch, seq, tm, rc):
    nt = seq // tm
    utri = jnp.triu(jnp.ones((tm, tm), F32), 1).astype(BF16)
    const = lambda shape: pl.BlockSpec(shape, lambda b, t: (0,) * len(shape))
    row_spec = lambda w: pl.BlockSpec((tm, w), lambda b, t: (b * nt + t, 0))
    return pl.pallas_call(
        functools.partial(_outproj_kernel, tm=tm, rc=rc),
        grid=(batch, nt),
        in_specs=[row_spec(D_MODEL), row_spec(D_CONV),
                  pl.BlockSpec((1, D_ATT, tm), lambda b, t: (b, 0, t)),
                  const((CONV_WIDTH, D_CONV)), const((1, D_CONV)), const((1, D_CONV)), const((1, D_CONV)),
                  const((1, D_CONV)),
                  const((D_CONV + D_ATT, D_MODEL)), const((1, D_ATT)), const((1, D_MODEL)),
                  const((LANES, D_MODEL)), const((LANES, 1)), const((tm, tm))],
        out_specs=[row_spec(D_MODEL), row_spec(ROW_W),
                   pl.BlockSpec((SUBLANES, tm), lambda b, t: (0, b * nt + t)), const((LANES, 1))],
        out_shape=[jax.ShapeDtypeStruct((batch * seq, D_MODEL), F32),
                   jax.ShapeDtypeStruct((batch * seq, ROW_W), F32),
                   jax.ShapeDtypeStruct((SUBLANES, batch * seq), F32),
                   jax.ShapeDtypeStruct((LANES, 1), F32)],
        scratch_shapes=[pltpu.VMEM((LANES, 1), F32),
                        pltpu.VMEM((SUBLANES, D_CONV // LANES, tm + CONV_HALO, LANES), F32),
                        pltpu.VMEM((tm, D_CONV), F32)],
        compiler_params=pltpu.CompilerParams(
            dimension_semantics=("arbitrary", "arbitrary"), vmem_limit_bytes=VMEM_LIMIT),
        name="outproj",
    )(x2, a, yT, *conv_params, wo, ga, g2, wrT, brT, utri)


def _row_copies(src_at, dst_at, sem):
    return pltpu.make_async_copy(src_at, dst_at, sem)


def _dispatch_kernel(dest_ref, h2_ref, xs_ref, sem, *, tm):
    def body(g, carry):
        for u in range(SUBLANES):
            _row_copies(h2_ref.at[g, pl.ds(u, 1)], xs_ref.at[pl.ds(dest_ref[g * SUBLANES + u], 1)],
                        sem).start(priority=u % 2)
        return carry

    lax.fori_loop(0, tm // SUBLANES, body, 0)
    done = xs_ref.at[pl.ds(0, tm)]
    _row_copies(done, done, sem).wait()


def _dispatch(dest, h2, *, tm):
    t_len = h2.shape[0]
    return pl.pallas_call(
        functools.partial(_dispatch_kernel, tm=tm),
        grid=(t_len // tm,),
        in_specs=[pl.BlockSpec((tm,), lambda i: (i,), memory_space=pltpu.SMEM),
                  pl.BlockSpec((tm // SUBLANES, SUBLANES, ROW_W), lambda i: (i, 0, 0))],
        out_specs=pl.BlockSpec(memory_space=pl.ANY),
        out_shape=jax.ShapeDtypeStruct((t_len, ROW_W), F32),
        scratch_shapes=[pltpu.SemaphoreType.DMA(())],
        compiler_params=pltpu.CompilerParams(dimension_semantics=("arbitrary",), vmem_limit_bytes=VMEM_LIMIT),
        name="dispatch",
    )(dest, h2.reshape(t_len // SUBLANES, SUBLANES, ROW_W))


def _expert_kernel(blk_ref, ea_ref, eb_ref, lo_ref, hi_ref, x_ref, wgua_ref, wda_ref, wgub_ref, wdb_ref, y_ref):
    del blk_ref, ea_ref, eb_ref
    i = pl.program_id(0)
    lo = lo_ref[i]
    hi = hi_ref[i]
    half = ROW_BLOCK // 2

    def mlp(xb, wgu_ref, wd_ref):
        gu = _dot(xb, wgu_ref[0])
        g = gu[:, :D_EXPERT]
        return _dot(((g * jax.nn.sigmoid(g)) * gu[:, D_EXPERT:]).astype(BF16), wd_ref[0])

    def run(rows):
        n = rows.stop - rows.start
        xb = x_ref[rows, :D_MODEL].astype(BF16)
        wts = x_ref[rows, D_MODEL:]
        y = wts[:, 0:1] * mlp(xb, wgua_ref, wda_ref) + wts[:, 1:2] * mlp(xb, wgub_ref, wdb_ref)
        row = rows.start + lax.broadcasted_iota(jnp.int32, (n, 1), 0)
        mine = (row >= lo) & (row < hi)

        @pl.when(lo == 0)
        def _():
            y_ref[rows, :] = jnp.where(mine, y, 0.0)
            if n < ROW_BLOCK:
                y_ref[half:, :] = jnp.zeros((half, D_MODEL), F32)

        @pl.when(lo > 0)
        def _():
            y_ref[rows, :] = jnp.where(mine, y, y_ref[rows, :])

    @pl.when((hi > lo) & (lo < half) & (hi > half))
    def _():
        run(slice(0, ROW_BLOCK))

    @pl.when((hi > lo) & (hi <= half))
    def _():
        run(slice(0, half))

    @pl.when((hi > lo) & (lo >= half))
    def _():
        run(slice(half, ROW_BLOCK))


def _experts(items, xs, wgu, wd):
    n_rows = xs.shape[0]
    n_items = items[0].shape[0]
    wspec = lambda shape, which: pl.BlockSpec(
        shape, lambda i, blk, ea, eb, lo, hi: ((ea, eb)[which][i], 0, 0))
    grid_spec = pltpu.PrefetchScalarGridSpec(
        num_scalar_prefetch=5,
        grid=(n_items,),
        in_specs=[pl.BlockSpec((ROW_BLOCK, ROW_W), lambda i, blk, ea, eb, lo, hi: (blk[i], 0)),
                  wspec((1, D_MODEL, 2 * D_EXPERT), 0), wspec((1, D_EXPERT, D_MODEL), 0),
                  wspec((1, D_MODEL, 2 * D_EXPERT), 1), wspec((1, D_EXPERT, D_MODEL), 1)],
        out_specs=pl.BlockSpec((ROW_BLOCK, D_MODEL), lambda i, blk, ea, eb, lo, hi: (blk[i], 0)),
    )
    return pl.pallas_call(
        _expert_kernel,
        grid_spec=grid_spec,
        out_shape=jax.ShapeDtypeStruct((n_rows, D_MODEL), F32),
        compiler_params=pltpu.CompilerParams(dimension_semantics=("arbitrary",), vmem_limit_bytes=VMEM_LIMIT),
        name="experts",
    )(*items, xs, wgu, wd, wgu, wd)


def _combine_kernel(dcur_ref, dnxt_ref, x1_ref, g_ref, ys_ref, out_ref, buf_ref, sems, *, tm):
    i = pl.program_id(0)
    slot = i % 2

    def issue(d_ref, s):
        def body(g, carry):
            for u in range(SUBLANES):
                _row_copies(ys_ref.at[pl.ds(d_ref[g * SUBLANES + u], 1)], buf_ref.at[s, g, pl.ds(u, 1)],
                            sems.at[s]).start(priority=u % 2)
            return carry

        lax.fori_loop(0, tm // SUBLANES, body, 0)

    @pl.when(i == 0)
    def _():
        issue(dcur_ref, 0)

    @pl.when(i + 1 < pl.num_programs(0))
    def _():
        issue(dnxt_ref, 1 - slot)

    _row_copies(buf_ref.at[slot], buf_ref.at[slot], sems.at[slot]).wait()
    x2 = x1_ref[...] + buf_ref[slot]
    ms = jnp.mean(x2 * x2, axis=-1, keepdims=True)
    out_ref[...] = (x2 * lax.rsqrt(ms + EPS)) * g_ref[...]


def _combine(dest, x1, gf, ys, *, tm):
    t_len = x1.shape[0]
    n = t_len // tm
    return pl.pallas_call(
        functools.partial(_combine_kernel, tm=tm),
        grid=(n,),
        in_specs=[pl.BlockSpec((tm,), lambda i: (i,), memory_space=pltpu.SMEM),
                  pl.BlockSpec((tm,), lambda i: (jnp.minimum(i + 1, n - 1),), memory_space=pltpu.SMEM),
                  pl.BlockSpec((tm // SUBLANES, SUBLANES, D_MODEL), lambda i: (i, 0, 0)),
                  pl.BlockSpec((1, D_MODEL), lambda i: (0, 0)),
                  pl.BlockSpec(memory_space=pl.ANY)],
        out_specs=pl.BlockSpec((tm // SUBLANES, SUBLANES, D_MODEL), lambda i: (i, 0, 0)),
        out_shape=jax.ShapeDtypeStruct((t_len // SUBLANES, SUBLANES, D_MODEL), F32),
        scratch_shapes=[pltpu.VMEM((2, tm // SUBLANES, SUBLANES, D_MODEL), F32), pltpu.SemaphoreType.DMA((2,))],
        compiler_params=pltpu.CompilerParams(dimension_semantics=("arbitrary",), vmem_limit_bytes=VMEM_LIMIT),
        name="combine",
    )(dest, dest, x1.reshape(t_len // SUBLANES, SUBLANES, D_MODEL), gf, ys)


def _layer(x, norm1_g, w_in, b_f, w_dw, b_dw, conv_ln_g, conv_ln_b, out_g_conv, out_g_att, w_out,
           norm2_g, w_r1, b_r1, w_r2, b_r2, w_gate, w_up, w_down):
    batch, seq, d = x.shape
    t_len = batch * seq
    tm = min(512, seq)
    tk = min(256, seq)
    tq = min(1024, seq)
    o2 = 2 * D_CONV
    o3, o4, o5 = o2 + D_ATT, o2 + 2 * D_ATT, o2 + 3 * D_ATT

    assert PIECE_STRIDE == N_HEADS
    pad = LANES - N_PIECES * PIECE_STRIDE
    wf = jnp.pad(jnp.tile(w_in[:, o5:], (1, N_PIECES)), ((0, 0), (0, pad)))
    wrow = jnp.concatenate([w_in[:, :o2], w_in[:, o3:o4], wf], axis=1).astype(BF16)
    wqvT = jnp.concatenate([w_in[:, o2:o3] * (HEAD_DIM ** -0.5), w_in[:, o4:o5]], axis=1).T.astype(BF16)
    bf3 = jnp.pad(jnp.tile(b_f.astype(F32).reshape(1, N_HEADS), (1, N_PIECES)), ((0, 0), (0, pad)))

    x2 = x.reshape(t_len, d)
    a, kp, qpT, vT, kstat, qstat = _inproj(x2, norm1_g.reshape(1, d), wrow, bf3, wqvT,
                                           batch=batch, seq=seq, tm=tm, tk=tk)
    conv_params = (w_dw, b_dw.reshape(1, -1), conv_ln_g.reshape(1, -1), conv_ln_b.reshape(1, -1),
                   out_g_conv.reshape(1, -1))
    g0, bound = _first_groups(kstat, qstat, tm=tm, tq=tq)
    yT = _attention(g0, bound, qpT, kp, vT, batch=batch, seq=seq, tm=tm, tq=tq, tk=tk)

    gpad = SUBLANES - N_GROUPS
    rpad = LANES - SUBLANES - N_EXPERTS
    wrT = jnp.concatenate([w_r1.T, jnp.zeros((gpad, d), F32),
                           jnp.transpose(w_r2, (0, 2, 1)).reshape(N_EXPERTS, d), jnp.zeros((rpad, d), F32)], axis=0)
    brT = jnp.concatenate([b_r1.astype(F32), jnp.full((gpad,), -jnp.inf, F32), b_r2.reshape(-1).astype(F32),
                           jnp.zeros((rpad,), F32)]).reshape(LANES, 1)
    x1, h2, route, cnt = _outproj(x2, a, yT, conv_params, w_out.astype(BF16),
                                  out_g_att.reshape(1, -1), norm2_g.reshape(1, d),
                                  wrT.astype(BF16), brT, batch=batch, seq=seq, tm=tm, rc=min(128, seq))

    i32 = jnp.int32
    lanes = jnp.arange(LANES, dtype=i32)
    pick = lambda table, idx: jnp.sum(jnp.where(idx[:, None] == lanes, table[None, :], 0), axis=1).astype(i32)
    counts = cnt[:, 0].astype(i32)
    ends = jnp.cumsum(counts).astype(i32)
    starts = ends - counts
    dest = pick(starts, route[0].astype(i32)) + route[1].astype(i32)

    n_blocks = t_len // ROW_BLOCK
    b_lo = starts // ROW_BLOCK
    n_it = jnp.where(counts > 0, (ends - 1) // ROW_BLOCK - b_lo + 1, 0)
    it_end = jnp.cumsum(n_it).astype(i32)
    it_start = it_end - n_it
    idx = jnp.arange(n_blocks + N_CLASSES, dtype=i32)
    valid = idx < it_end[-1]
    last_cls = jnp.max(jnp.where(counts > 0, lanes, 0))
    cls = jnp.where(valid, jnp.sum(it_end[None, :] <= idx[:, None], axis=1).astype(i32), last_cls)
    blk = jnp.where(valid, pick(b_lo, cls) + idx - pick(it_start, cls), n_blocks - 1)
    row0 = blk * ROW_BLOCK
    lo = jnp.where(valid, jnp.maximum(pick(starts, cls), row0) - row0, 0)
    hi = jnp.where(valid, jnp.minimum(pick(ends, cls), row0 + ROW_BLOCK) - row0, 0)
    pair_a, pair_b = [], []
    for g in range(N_GROUPS):
        for a_loc in range(EXPERTS_PER_GROUP):
            for b_loc in range(a_loc + 1, EXPERTS_PER_GROUP):
                pair_a.append(g * EXPERTS_PER_GROUP + a_loc)
                pair_b.append(g * EXPERTS_PER_GROUP + b_loc)
    cpad = [0] * (LANES - N_CLASSES)
    items = (blk.astype(i32), pick(jnp.array(pair_a + cpad, i32), cls), pick(jnp.array(pair_b + cpad, i32), cls),
             lo.astype(i32), hi.astype(i32))

    xs = _dispatch(dest, h2, tm=min(1024, t_len))
    ys = _experts(items, xs, jnp.concatenate([w_gate, w_up], axis=-1).astype(BF16), w_down.astype(BF16))
    return dest, x1, ys


def kernel(x, norm1_g, w_in, b_f, w_dw, b_dw, conv_ln_g, conv_ln_b, out_g_conv, out_g_att, w_out, norm2_g,
           w_r1, b_r1, w_r2, b_r2, w_gate, w_up, w_down, final_g):
    assert norm1_g.shape[0] == 1, "single-layer stack"
    batch, seq, d = x.shape
    dest, x1, ys = _layer(
        x, norm1_g[0], w_in[0], b_f[0], w_dw[0], b_dw[0], conv_ln_g[0], conv_ln_b[0], out_g_conv[0],
        out_g_att[0], w_out[0], norm2_g[0], w_r1[0], b_r1[0], w_r2[0], b_r2[0], w_gate[0], w_up[0], w_down[0])
    out = _combine(dest, x1, final_g.reshape(1, d), ys, tm=min(512, batch * seq))
    return out.reshape(batch, seq, d)
```

```python
import functools

import jax
import jax.numpy as jnp
from jax import lax
from jax.experimental import pallas as pl
from jax.experimental.pallas import tpu as pltpu

D_MODEL = 1024
D_CONV = 512
N_HEADS = 8
HEAD_DIM = 64
D_ATT = N_HEADS * HEAD_DIM
CONV_WIDTH = 31
N_GROUPS = 4
EXPERTS_PER_GROUP = 8
N_EXPERTS = N_GROUPS * EXPERTS_PER_GROUP
D_EXPERT = D_MODEL // 4
ROW_BLOCK = 256
EPS = 1e-6

LANES = 128
SUBLANES = 8
KAUG = 128
N_PIECES = 3
PIECE_STRIDE = 8
CONV_HALO = 32
PAIRS_PER_GROUP = EXPERTS_PER_GROUP * (EXPERTS_PER_GROUP - 1) // 2
N_CLASSES = N_GROUPS * PAIRS_PER_GROUP
ROW_W = D_MODEL + LANES
SKIP_LOG2 = 160.0
FIXED_REF_MAX_BOUND = 40.0
NORM_SLACK = 1.02
V_ROWS = 80
LOG2E = 1.4426950408889634
VMEM_LIMIT = 56 * 1024 * 1024

F32 = jnp.float32
BF16 = jnp.bfloat16


def _dot(a, b):
    return jnp.dot(a, b, preferred_element_type=F32)


def _dot_nt(a, b):
    return lax.dot_general(a, b, (((1,), (1,)), ((), ())), preferred_element_type=F32)


def _split3(x):
    hi = x.astype(BF16)
    r1 = x - hi.astype(F32)
    mid = r1.astype(BF16)
    lo = (r1 - mid.astype(F32)).astype(BF16)
    return hi.astype(F32), mid.astype(F32), lo.astype(F32)


def _piece_lane_mask(lane, h):
    return (lane == h) | (lane == h + PIECE_STRIDE) | (lane == h + 2 * PIECE_STRIDE)


def _inproj_kernel(x_ref, g1_ref, wrow_ref, bf_ref, wqvT_ref, ltri_ref, hsel_ref,
                   a_ref, kp_ref, qpT_ref, vT_ref, kstat_ref, qstat_ref, carry_ref, *, tm, tk):
    @pl.when(pl.program_id(1) == 0)
    def _():
        carry_ref[...] = jnp.zeros_like(carry_ref)

    x = x_ref[...]
    ms = jnp.mean(x * x, axis=-1, keepdims=True)
    hb = ((x * lax.rsqrt(ms + EPS)) * g1_ref[...]).astype(BF16)

    z = _dot(hb, wrow_ref[...])
    a_ref[...] = z[:, :D_CONV] * jax.nn.sigmoid(z[:, D_CONV:2 * D_CONV])

    kk = z[:, 2 * D_CONV:2 * D_CONV + D_ATT]

    zf = z[:, 2 * D_CONV + D_ATT:] + bf_ref[...]
    lf = jnp.minimum(zf, 0.0) - jnp.log1p(jnp.exp(-jnp.abs(zf)))
    lane = lax.broadcasted_iota(jnp.int32, (tm, LANES), 1)
    hi, mid, lo = _split3(lf)
    lf3 = jnp.where(lane < PIECE_STRIDE, hi,
                    jnp.where(lane < 2 * PIECE_STRIDE, mid,
                              jnp.where(lane < 3 * PIECE_STRIDE, lo, 0.0))).astype(BF16)
    cs3 = _dot(ltri_ref[...], lf3)
    c = (cs3 + pltpu.roll(cs3, LANES - PIECE_STRIDE, 1)
         + pltpu.roll(cs3, LANES - 2 * PIECE_STRIDE, 1)) + carry_ref[...]
    carry_ref[...] = c[tm - 1:tm, :]

    nhi, nmid, nlo = _split3(c * (-LOG2E))
    p3 = jnp.where(lane < PIECE_STRIDE, nhi,
                   jnp.where(lane < 2 * PIECE_STRIDE, pltpu.roll(nmid, PIECE_STRIDE, 1),
                             jnp.where(lane < 3 * PIECE_STRIDE, pltpu.roll(nlo, 2 * PIECE_STRIDE, 1), 0.0)))
    p3_hi = pltpu.roll(p3, HEAD_DIM, 1)
    q_pieces = (-p3).T[0:HEAD_DIM]
    q_shift = N_PIECES * PIECE_STRIDE

    qvT = _dot_nt(wqvT_ref[...], hb)
    qT = qvT[:D_ATT] * LOG2E
    vT = qvT[D_ATT:]
    row = lax.broadcasted_iota(jnp.int32, (HEAD_DIM, tm), 0)
    vrow = lax.broadcasted_iota(jnp.int32, (V_ROWS - HEAD_DIM, tk), 0)
    v_tail = jnp.where(vrow == 0, 1.0, 0.0).astype(BF16)

    for h in range(N_HEADS):
        kcol = kk[:, (h // 2) * LANES:(h // 2 + 1) * LANES]
        f_h = jnp.where(_piece_lane_mask(row, h), q_pieces, 0.0)
        aug_q = jnp.where(_piece_lane_mask(row, h), 1.0, 0.0) + jnp.concatenate(
            [jnp.zeros((q_shift, tm), F32), f_h[0:HEAD_DIM - q_shift]], axis=0)
        q_h = qT[h * HEAD_DIM:(h + 1) * HEAD_DIM, :]
        if h % 2 == 0:
            ext = jnp.where(_piece_lane_mask(lane, h + HEAD_DIM), p3_hi,
                            jnp.where(_piece_lane_mask(lane, h + HEAD_DIM + q_shift), 1.0, 0.0))
            kp = jnp.where(lane < HEAD_DIM, kcol, ext)
            qp = jnp.concatenate([q_h, aug_q], axis=0)
        else:
            ext = jnp.where(_piece_lane_mask(lane, h), p3,
                            jnp.where(_piece_lane_mask(lane, h + q_shift), 1.0, 0.0))
            kp = jnp.where(lane >= HEAD_DIM, kcol, ext)
            qp = jnp.concatenate([aug_q, q_h], axis=0)
        qpT_ref[0, h] = qp.astype(BF16)
        for cidx in range(tm // tk):
            kp_ref[0, h, cidx] = kp[cidx * tk:(cidx + 1) * tk, :].astype(BF16)
            v_h = vT[h * HEAD_DIM:(h + 1) * HEAD_DIM, cidx * tk:(cidx + 1) * tk].astype(BF16)
            vT_ref[0, h, cidx] = jnp.concatenate([v_h, v_tail], axis=0)

    kn2 = jnp.max(_dot((kk * kk).astype(BF16), hsel_ref[...]), axis=0, keepdims=True)
    srow = lax.broadcasted_iota(jnp.int32, (SUBLANES, LANES), 0)
    kstat_ref[0, 0] = jnp.where(srow == 0, c[0:1, :], jnp.where(srow == 1, c[tm - 1:tm, :],
                                                                 jnp.where(srow == 2, kn2, 0.0)))
    q2 = qT * qT
    qn2 = jnp.concatenate([jnp.sum(q2[h * HEAD_DIM:(h + 1) * HEAD_DIM, :], axis=0, keepdims=True)
                           for h in range(N_HEADS)], axis=0)
    qstat_ref[0, 0] = jnp.broadcast_to(jnp.max(qn2, axis=1, keepdims=True), (N_HEADS, LANES))


def _inproj(x2, g1, wrow, bf3, wqvT, *, batch, seq, tm, tk):
    nt = seq // tm
    nk = seq // tk
    ltri = jnp.tril(jnp.ones((tm, tm), F32)).astype(BF16)
    hsel = (jnp.arange(D_ATT)[:, None] // HEAD_DIM == jnp.arange(LANES)[None, :]).astype(BF16)
    const = lambda shape: pl.BlockSpec(shape, lambda b, t: (0,) * len(shape))
    stat_spec = pl.BlockSpec((1, 1, SUBLANES, LANES), lambda b, t: (b, t, 0, 0))
    stat_shape = jax.ShapeDtypeStruct((batch, nt, SUBLANES, LANES), F32)
    return pl.pallas_call(
        functools.partial(_inproj_kernel, tm=tm, tk=tk),
        grid=(batch, nt),
        in_specs=[
            pl.BlockSpec((tm, D_MODEL), lambda b, t: (b * nt + t, 0)),
            const((1, D_MODEL)), const((D_MODEL, 2 * D_CONV + D_ATT + LANES)), const((1, LANES)),
            const((2 * D_ATT, D_MODEL)), const((tm, tm)), const((D_ATT, LANES)),
        ],
        out_specs=[
            pl.BlockSpec((tm, D_CONV), lambda b, t: (b * nt + t, 0)),
            pl.BlockSpec((1, N_HEADS, tm // tk, tk, KAUG), lambda b, t: (b, 0, t, 0, 0)),
            pl.BlockSpec((1, N_HEADS, KAUG, tm), lambda b, t: (b, 0, 0, t)),
            pl.BlockSpec((1, N_HEADS, tm // tk, V_ROWS, tk), lambda b, t: (b, 0, t, 0, 0)),
            stat_spec, stat_spec,
        ],
        out_shape=[
            jax.ShapeDtypeStruct((batch * seq, D_CONV), F32),
            jax.ShapeDtypeStruct((batch, N_HEADS, nk, tk, KAUG), BF16),
            jax.ShapeDtypeStruct((batch, N_HEADS, KAUG, seq), BF16),
            jax.ShapeDtypeStruct((batch, N_HEADS, nk, V_ROWS, tk), BF16),
            stat_shape, stat_shape,
        ],
        scratch_shapes=[pltpu.VMEM((1, LANES), F32)],
        compiler_params=pltpu.CompilerParams(
            dimension_semantics=("arbitrary", "arbitrary"), vmem_limit_bytes=VMEM_LIMIT),
        name="inproj",
    )(x2, g1, wrow, bf3, wqvT, ltri, hsel)


def _conv_tile(a_ref, w_ref, b_ref, lng_ref, lnb_ref, og_ref, sh_ref, acc_ref, *, tc, rc):
    n_cb = D_CONV // LANES

    @pl.when(pl.program_id(1) == 0)
    def _():
        for cb in range(n_cb):
            sh_ref[0, cb, 0:CONV_HALO, :] = jnp.zeros((CONV_HALO, LANES), F32)

    @pl.when(pl.program_id(1) > 0)
    def _():
        for cb in range(n_cb):
            sh_ref[0, cb, 0:CONV_HALO, :] = sh_ref[0, cb, tc:tc + CONV_HALO, :]

    n_sh = tc + CONV_HALO - SUBLANES
    for cb in range(n_cb):
        sh_ref[0, cb, CONV_HALO:CONV_HALO + tc, :] = a_ref[:, cb * LANES:(cb + 1) * LANES]
        for f in range(1, SUBLANES):
            sh_ref[f, cb, 0:n_sh, :] = sh_ref[0, cb, f:f + n_sh, :]

    base = CONV_HALO - (CONV_WIDTH - 1)
    for cb in range(n_cb):
        cols = slice(cb * LANES, (cb + 1) * LANES)

        def chunk(c, carry, cb=cb, cols=cols):
            r0 = pl.multiple_of(c * rc, rc)
            acc = jnp.zeros((rc, LANES), F32)
            for j in range(CONV_WIDTH):
                f = (base + j) % SUBLANES
                acc = acc + w_ref[j:j + 1, cols] * sh_ref[f, cb, pl.ds(r0 + (base + j - f), rc), :]
            acc_ref[pl.ds(r0, rc), cols] = acc
            return carry

        lax.fori_loop(0, tc // rc, chunk, 0)

    y = acc_ref[...] + b_ref[...]
    mu = jnp.mean(y, axis=-1, keepdims=True)
    yc = y - mu
    var = jnp.mean(yc * yc, axis=-1, keepdims=True)
    yn = yc * lax.rsqrt(var + EPS) * lng_ref[...] + lnb_ref[...]
    s = yn * jax.nn.sigmoid(yn)
    ms = jnp.mean(s * s, axis=-1, keepdims=True)
    return (s * lax.rsqrt(ms + EPS) * og_ref[...]).astype(BF16)


def _attn_fixed_ref_kernel(g0_ref, bnd_ref, qT_ref, k_ref, v_ref, o_ref, *, tq, tk, unit):
    i = pl.program_id(2)
    qT = qT_ref[0, 0]
    n_sub = tq // tk
    bnd = bnd_ref[0, 0, 0:1, 0:1]
    sb0 = unit * g0_ref[(pl.program_id(0) * N_HEADS + pl.program_id(1)) * pl.num_programs(2) + i]

    def values(sb, n):
        return jnp.concatenate([v_ref[0, 0, sb + d] for d in range(n)], axis=1)

    def probs(sb, n):
        return jnp.concatenate([jnp.exp2(_dot(k_ref[0, 0, sb + d], qT) - bnd).astype(BF16)
                                for d in range(n)], axis=0)

    def two_groups(p, acc):
        sb = sb0 + 2 * n_sub * p
        acc = acc + _dot(values(sb, n_sub), probs(sb, n_sub))
        return acc + _dot(values(sb + n_sub, n_sub), probs(sb + n_sub, n_sub))

    key = lax.broadcasted_iota(jnp.int32, (tk, tk), 0)
    qry = lax.broadcasted_iota(jnp.int32, (tk, tk), 1)

    def diagonal(acc):
        rows = []
        for d in range(n_sub):
            s = _dot(k_ref[0, 0, i * n_sub + d], qT[:, d * tk:])
            tiles = [jnp.zeros((tk, d * tk), BF16)] if d else []
            tiles.append(jnp.exp2(jnp.where(key <= qry, s[:, :tk], -jnp.inf) - bnd).astype(BF16))
            if d + 1 < n_sub:
                tiles.append(jnp.exp2(s[:, tk:] - bnd).astype(BF16))
            rows.append(jnp.concatenate(tiles, axis=1))
        acc = acc + _dot(values(i * n_sub, n_sub), jnp.concatenate(rows, axis=0))
        o_ref[0] = acc[:HEAD_DIM] / acc[HEAD_DIM:HEAD_DIM + 1]

    n_before = i * n_sub - sb0
    acc = lax.fori_loop(0, n_before // (2 * n_sub), two_groups, jnp.zeros((V_ROWS, tq), F32))
    left = n_before % (2 * n_sub)

    for n in range(0, 2 * n_sub, unit):
        @pl.when(left == n)
        def _(n=n):
            diagonal(acc + _dot(values(i * n_sub - n, n), probs(i * n_sub - n, n)) if n else acc)


def _attn_kernel(g0_ref, qT_ref, k_ref, v_ref, o_ref, sa_ref, sb_ref, cma_ref, cmb_ref, *, tq, tk):
    i = pl.program_id(2)
    qT = qT_ref[0, 0]
    n_sub = tq // tk

    def scores(g, s_ref, cm_ref):
        for d in range(n_sub):
            s = _dot(k_ref[0, 0, g * n_sub + d], qT)
            s_ref[d] = s
            cm_ref[d] = jnp.max(s, axis=0, keepdims=True)

    def consume(g, s_ref, cm_ref, carry, nxt=None):
        m, acc = carry
        m_new = m
        for d in range(n_sub):
            m_new = jnp.maximum(m_new, cm_ref[d])
        ps = []
        for d in range(n_sub):
            if nxt is not None:
                g_n, sn_ref, cmn_ref = nxt
                s = _dot(k_ref[0, 0, g_n * n_sub + d], qT)
                sn_ref[d] = s
                cmn_ref[d] = jnp.max(s, axis=0, keepdims=True)
            ps.append(jnp.exp2(s_ref[d] - m_new).astype(BF16))
        p = jnp.concatenate(ps, axis=0)
        v = jnp.concatenate([v_ref[0, 0, g * n_sub + d] for d in range(n_sub)], axis=1)
        return m_new, jnp.exp2(m - m_new) * acc + _dot(v, p)

    key = lax.broadcasted_iota(jnp.int32, (tk, tk), 0)
    qry = lax.broadcasted_iota(jnp.int32, (tk, tk), 1)

    def consume_diag(s_ref, cm_ref, carry):
        m, acc = carry
        lanes = lambda j: slice(j * tk, (j + 1) * tk)
        causal = [jnp.where(key <= qry, s_ref[d, :, lanes(d)], -jnp.inf) for d in range(n_sub)]
        m_tiles = []
        for j in range(n_sub):
            m_j = jnp.maximum(m[:, lanes(j)], jnp.max(causal[j], axis=0, keepdims=True))
            for d in range(j):
                m_j = jnp.maximum(m_j, cm_ref[d, :, lanes(j)])
            m_tiles.append(m_j)
        m_new = jnp.concatenate(m_tiles, axis=1)
        rows = []
        for d in range(n_sub):
            tiles = [jnp.zeros((tk, d * tk), BF16)] if d else []
            tiles.append(jnp.exp2(causal[d] - m_tiles[d]).astype(BF16))
            if d + 1 < n_sub:
                rest = slice((d + 1) * tk, tq)
                tiles.append(jnp.exp2(s_ref[d, :, rest] - m_new[:, rest]).astype(BF16))
            rows.append(jnp.concatenate(tiles, axis=1))
        p = jnp.concatenate(rows, axis=0)
        v = jnp.concatenate([v_ref[0, 0, i * n_sub + d] for d in range(n_sub)], axis=1)
        acc = jnp.exp2(m - m_new) * acc + _dot(v, p)
        o_ref[0] = acc[:HEAD_DIM] / acc[HEAD_DIM:HEAD_DIM + 1]

    g0 = g0_ref[(pl.program_id(0) * N_HEADS + pl.program_id(1)) * pl.num_programs(2) + i]
    n_full = i - g0

    def pair(p, carry):
        g = g0 + 2 * p
        carry = consume(g, sa_ref, cma_ref, carry, nxt=(g + 1, sb_ref, cmb_ref))
        return consume(g + 1, sb_ref, cmb_ref, carry, nxt=(g + 2, sa_ref, cma_ref))

    scores(g0, sa_ref, cma_ref)
    init = (jnp.full((1, tq), -jnp.inf, F32), jnp.zeros((V_ROWS, tq), F32))
    carry = lax.fori_loop(0, n_full // 2, pair, init)

    @pl.when(n_full % 2 == 1)
    def _():
        consume_diag(sb_ref, cmb_ref, consume(i - 1, sa_ref, cma_ref, carry, nxt=(i, sb_ref, cmb_ref)))

    @pl.when(n_full % 2 == 0)
    def _():
        consume_diag(sa_ref, cma_ref, carry)


def _first_groups(kstat, qstat, *, tm, tq):
    per = tq // tm
    bound = jnp.sqrt(jnp.max(kstat[:, :, 2, :N_HEADS], axis=1) * jnp.max(qstat[:, :, :, 0], axis=1))
    bound = bound * NORM_SLACK
    cq = kstat[:, ::per, 0, :N_HEADS]
    ck = kstat[:, :, 1, :N_HEADS]
    d = (cq[:, :, None, :] - ck[:, None, :, :]) * LOG2E
    nq, ng = cq.shape[1], ck.shape[1]
    earlier = per * jnp.arange(nq)[None, :, None, None] > jnp.arange(ng)[None, None, :, None]
    skip = earlier & (2.0 * bound[:, None, None, :] + d < -SKIP_LOG2)
    g0 = jnp.sum(jnp.cumprod(skip.astype(jnp.int32), axis=2), axis=2)
    return jnp.transpose(g0, (0, 2, 1)).reshape(-1).astype(jnp.int32), bound


def _attention(g0, bound, qpT, kp, vT, *, batch, seq, tm, tq, tk):
    nk = seq // tk
    params = pltpu.CompilerParams(
        dimension_semantics=("arbitrary", "arbitrary", "arbitrary"), vmem_limit_bytes=VMEM_LIMIT)
    out_shape = jax.ShapeDtypeStruct((batch, D_ATT, seq), F32)
    qkv_specs = [
        pl.BlockSpec((1, 1, KAUG, tq), lambda b, h, i, g0: (b, h, 0, i)),
        pl.BlockSpec((1, 1, nk, tk, KAUG), lambda b, h, i, g0: (b, h, 0, 0, 0)),
        pl.BlockSpec((1, 1, nk, V_ROWS, tk), lambda b, h, i, g0: (b, h, 0, 0, 0)),
    ]
    out_spec = pl.BlockSpec((1, HEAD_DIM, tq), lambda b, h, i, g0: (b, h, i))

    def fixed_ref(g0, bnd, qpT, kp, vT):
        grid_spec = pltpu.PrefetchScalarGridSpec(
            num_scalar_prefetch=1, grid=(batch, N_HEADS, seq // tq),
            in_specs=[pl.BlockSpec((1, 1, SUBLANES, LANES), lambda b, h, i, g0: (b, h, 0, 0))] + qkv_specs,
            out_specs=out_spec)
        return pl.pallas_call(functools.partial(_attn_fixed_ref_kernel, tq=tq, tk=tk, unit=tm // tk),
                              grid_spec=grid_spec, out_shape=out_shape, compiler_params=params,
                              name="attention_fixed_ref")(g0, bnd, qpT, kp, vT)

    def running_max(g0, bnd, qpT, kp, vT):
        del bnd
        grid_spec = pltpu.PrefetchScalarGridSpec(
            num_scalar_prefetch=1, grid=(batch, N_HEADS, seq // tq), in_specs=qkv_specs, out_specs=out_spec,
            scratch_shapes=[pltpu.VMEM((tq // tk, tk, tq), F32), pltpu.VMEM((tq // tk, tk, tq), F32),
                            pltpu.VMEM((tq // tk, 1, tq), F32), pltpu.VMEM((tq // tk, 1, tq), F32)])
        return pl.pallas_call(functools.partial(_attn_kernel, tq=tq, tk=tk), grid_spec=grid_spec,
                              out_shape=out_shape, compiler_params=params, name="attention",
                              )(g0 // (tq // tm), qpT, kp, vT)

    bnd = jnp.broadcast_to(bound[:, :, None, None], (batch, N_HEADS, SUBLANES, LANES))
    return lax.cond(jnp.max(bound) <= FIXED_REF_MAX_BOUND, fixed_ref, running_max, g0, bnd, qpT, kp, vT)


def _outproj_kernel(x_ref, a_ref, yT_ref, wdw_ref, bdw_ref, lng_ref, lnb_ref, gc_ref, wo_ref, ga_ref,
                    g2_ref, wrT_ref, brT_ref, utri_ref, x1_ref, h2_ref, route_ref, cnt_ref,
                    carry_ref, sh_ref, acc_ref, *, tm, rc):
    first = (pl.program_id(0) == 0) & (pl.program_id(1) == 0)

    @pl.when(first)
    def _():
        carry_ref[...] = jnp.zeros_like(carry_ref)

    mc = _conv_tile(a_ref, wdw_ref, bdw_ref, lng_ref, lnb_ref, gc_ref, sh_ref, acc_ref, tc=tm, rc=rc)
    yT = yT_ref[0]
    msa = jnp.mean(yT * yT, axis=0, keepdims=True)
    yn = (yT * lax.rsqrt(msa + EPS)).T * ga_ref[...]
    x1 = x_ref[...] + _dot(jnp.concatenate([mc, yn.astype(BF16)], axis=1), wo_ref[...])
    x1_ref[...] = x1
    ms = jnp.mean(x1 * x1, axis=-1, keepdims=True)
    h2 = (x1 * lax.rsqrt(ms + EPS)) * g2_ref[...]
    h2_ref[:, :D_MODEL] = h2

    lg = _dot_nt(wrT_ref[...], h2.astype(BF16)) + brT_ref[...]
    neg = -jnp.inf
    sub = lax.broadcasted_iota(jnp.int32, (SUBLANES, tm), 0)
    big = jnp.int32(SUBLANES)
    first_of = lambda hit: jnp.min(jnp.where(hit, sub, big), axis=0, keepdims=True)

    lg1 = lg[0:SUBLANES]
    m1 = jnp.max(lg1, axis=0, keepdims=True)
    p1_sel = 1.0 / jnp.sum(jnp.exp(lg1 - m1), axis=0, keepdims=True)
    grp = first_of(lg1 == m1)

    slab = lambda g: lg[SUBLANES * (g + 1):SUBLANES * (g + 2)]
    v = slab(N_GROUPS - 1)
    for g in range(N_GROUPS - 2, -1, -1):
        v = jnp.where(grp == g, slab(g), v)
    v1 = jnp.max(v, axis=0, keepdims=True)
    j1 = first_of(v == v1)
    vv = jnp.where(sub == j1, neg, v)
    v2 = jnp.max(vv, axis=0, keepdims=True)
    j2 = first_of(vv == v2)
    e21 = jnp.exp(v2 - v1)
    w0 = p1_sel / (1.0 + e21)
    w1 = p1_sel * e21 / (1.0 + e21)

    swap = j2 < j1
    al = jnp.where(swap, j2, j1)
    bl = jnp.where(swap, j1, j2)
    wa = jnp.where(swap, w1, w0)
    wb = jnp.where(swap, w0, w1)
    cid = PAIRS_PER_GROUP * grp + ((al * (2 * EXPERTS_PER_GROUP - 1 - al)) >> 1) + (bl - al - 1)

    cls = lax.broadcasted_iota(jnp.int32, (LANES, tm), 0)
    oh = cls == cid
    cmat = jnp.where(oh, 1.0, 0.0)
    prefix = _dot(cmat.astype(BF16), utri_ref[...]) + carry_ref[...]
    rank = jnp.sum(jnp.where(oh, prefix, 0.0), axis=0, keepdims=True)
    counts = prefix[:, tm - 1:tm] + cmat[:, tm - 1:tm]
    carry_ref[...] = counts
    cnt_ref[...] = counts

    h2_ref[:, D_MODEL:] = jnp.where(cls == 0, wa, jnp.where(cls == 1, wb, 0.0)).T
    route_ref[...] = jnp.where(sub == 0, cid.astype(F32), jnp.where(sub == 1, rank, 0.0))


def _outproj(x2, a, yT, conv_params, wo, ga, g2, wrT, brT, *, batch, seq, tm, rc):
    nt = seq // tm
    utri = jnp.triu(jnp.ones((tm, tm), F32), 1).astype(BF16)
    const = lambda shape: pl.BlockSpec(shape, lambda b, t: (0,) * len(shape))
    row_spec = lambda w: pl.BlockSpec((tm, w), lambda b, t: (b * nt + t, 0))
    return pl.pallas_call(
        functools.partial(_outproj_kernel, tm=tm, rc=rc),
        grid=(batch, nt),
        in_specs=[row_spec(D_MODEL), row_spec(D_CONV),
                  pl.BlockSpec((1, D_ATT, tm), lambda b, t: (b, 0, t)),
                  const((CONV_WIDTH, D_CONV)), const((1, D_CONV)), const((1, D_CONV)), const((1, D_CONV)),
                  const((1, D_CONV)),
                  const((D_CONV + D_ATT, D_MODEL)), const((1, D_ATT)), const((1, D_MODEL)),
                  const((LANES, D_MODEL)), const((LANES, 1)), const((tm, tm))],
        out_specs=[row_spec(D_MODEL), row_spec(ROW_W),
                   pl.BlockSpec((SUBLANES, tm), lambda b, t: (0, b * nt + t)), const((LANES, 1))],
        out_shape=[jax.ShapeDtypeStruct((batch * seq, D_MODEL), F32),
                   jax.ShapeDtypeStruct((batch * seq, ROW_W), F32),
                   jax.ShapeDtypeStruct((SUBLANES, batch * seq), F32),
                   jax.ShapeDtypeStruct((LANES, 1), F32)],
        scratch_shapes=[pltpu.VMEM((LANES, 1), F32),
                        pltpu.VMEM((SUBLANES, D_CONV // LANES, tm + CONV_HALO, LANES), F32),
                        pltpu.VMEM((tm, D_CONV), F32)],
        compiler_params=pltpu.CompilerParams(
            dimension_semantics=("arbitrary", "arbitrary"), vmem_limit_bytes=VMEM_LIMIT),
        name="outproj",
    )(x2, a, yT, *conv_params, wo, ga, g2, wrT, brT, utri)


def _row_copies(src_at, dst_at, sem):
    return pltpu.make_async_copy(src_at, dst_at, sem)


def _dispatch_kernel(dest_ref, h2_ref, xs_ref, sem, *, tm):
    def body(g, carry):
        for u in range(SUBLANES):
            _row_copies(h2_ref.at[g, pl.ds(u, 1)], xs_ref.at[pl.ds(dest_ref[g * SUBLANES + u], 1)],
                        sem).start(priority=u % 2)
        return carry

    lax.fori_loop(0, tm // SUBLANES, body, 0)
    done = xs_ref.at[pl.ds(0, tm)]
    _row_copies(done, done, sem).wait()


def _dispatch(dest, h2, *, tm):
    t_len = h2.shape[0]
    return pl.pallas_call(
        functools.partial(_dispatch_kernel, tm=tm),
        grid=(t_len // tm,),
        in_specs=[pl.BlockSpec((tm,), lambda i: (i,), memory_space=pltpu.SMEM),
                  pl.BlockSpec((tm // SUBLANES, SUBLANES, ROW_W), lambda i: (i, 0, 0))],
        out_specs=pl.BlockSpec(memory_space=pl.ANY),
        out_shape=jax.ShapeDtypeStruct((t_len, ROW_W), F32),
        scratch_shapes=[pltpu.SemaphoreType.DMA(())],
        compiler_params=pltpu.CompilerParams(dimension_semantics=("arbitrary",), vmem_limit_bytes=VMEM_LIMIT),
        name="dispatch",
    )(dest, h2.reshape(t_len // SUBLANES, SUBLANES, ROW_W))


def _expert_kernel(blk_ref, ea_ref, eb_ref, lo_ref, hi_ref, x_ref, wgua_ref, wda_ref, wgub_ref, wdb_ref, y_ref):
    del blk_ref, ea_ref, eb_ref
    i = pl.program_id(0)
    lo = lo_ref[i]
    hi = hi_ref[i]
    half = ROW_BLOCK // 2

    def mlp(xb, wgu_ref, wd_ref):
        gu = _dot(xb, wgu_ref[0])
        g = gu[:, :D_EXPERT]
        return _dot(((g * jax.nn.sigmoid(g)) * gu[:, D_EXPERT:]).astype(BF16), wd_ref[0])

    def run(rows):
        n = rows.stop - rows.start
        xb = x_ref[rows, :D_MODEL].astype(BF16)
        wts = x_ref[rows, D_MODEL:]
        y = wts[:, 0:1] * mlp(xb, wgua_ref, wda_ref) + wts[:, 1:2] * mlp(xb, wgub_ref, wdb_ref)
        row = rows.start + lax.broadcasted_iota(jnp.int32, (n, 1), 0)
        mine = (row >= lo) & (row < hi)

        @pl.when(lo == 0)
        def _():
            y_ref[rows, :] = jnp.where(mine, y, 0.0)
            if n < ROW_BLOCK:
                y_ref[half:, :] = jnp.zeros((half, D_MODEL), F32)

        @pl.when(lo > 0)
        def _():
            y_ref[rows, :] = jnp.where(mine, y, y_ref[rows, :])

    @pl.when((hi > lo) & (lo < half) & (hi > half))
    def _():
        run(slice(0, ROW_BLOCK))

    @pl.when((hi > lo) & (hi <= half))
    def _():
        run(slice(0, half))

    @pl.when((hi > lo) & (lo >= half))
    def _():
        run(slice(half, ROW_BLOCK))


def _experts(items, xs, wgu, wd):
    n_rows = xs.shape[0]
    n_items = items[0].shape[0]
    wspec = lambda shape, which: pl.BlockSpec(
        shape, lambda i, blk, ea, eb, lo, hi: ((ea, eb)[which][i], 0, 0))
    grid_spec = pltpu.PrefetchScalarGridSpec(
        num_scalar_prefetch=5,
        grid=(n_items,),
        in_specs=[pl.BlockSpec((ROW_BLOCK, ROW_W), lambda i, blk, ea, eb, lo, hi: (blk[i], 0)),
                  wspec((1, D_MODEL, 2 * D_EXPERT), 0), wspec((1, D_EXPERT, D_MODEL), 0),
                  wspec((1, D_MODEL, 2 * D_EXPERT), 1), wspec((1, D_EXPERT, D_MODEL), 1)],
        out_specs=pl.BlockSpec((ROW_BLOCK, D_MODEL), lambda i, blk, ea, eb, lo, hi: (blk[i], 0)),
    )
    return pl.pallas_call(
        _expert_kernel,
        grid_spec=grid_spec,
        out_shape=jax.ShapeDtypeStruct((n_rows, D_MODEL), F32),
        compiler_params=pltpu.CompilerParams(dimension_semantics=("arbitrary",), vmem_limit_bytes=VMEM_LIMIT),
        name="experts",
    )(*items, xs, wgu, wd, wgu, wd)


def _combine_kernel(dcur_ref, dnxt_ref, x1_ref, g_ref, ys_ref, out_ref, buf_ref, sems, *, tm):
    i = pl.program_id(0)
    slot = i % 2

    def issue(d_ref, s):
        def body(g, carry):
            for u in range(SUBLANES):
                _row_copies(ys_ref.at[pl.ds(d_ref[g * SUBLANES + u], 1)], buf_ref.at[s, g, pl.ds(u, 1)],
                            sems.at[s]).start(priority=u % 2)
            return carry

        lax.fori_loop(0, tm // SUBLANES, body, 0)

    @pl.when(i == 0)
    def _():
        issue(dcur_ref, 0)

    @pl.when(i + 1 < pl.num_programs(0))
    def _():
        issue(dnxt_ref, 1 - slot)

    _row_copies(buf_ref.at[slot], buf_ref.at[slot], sems.at[slot]).wait()
    x2 = x1_ref[...] + buf_ref[slot]
    ms = jnp.mean(x2 * x2, axis=-1, keepdims=True)
    out_ref[...] = (x2 * lax.rsqrt(ms + EPS)) * g_ref[...]


def _combine(dest, x1, gf, ys, *, tm):
    t_len = x1.shape[0]
    n = t_len // tm
    return pl.pallas_call(
        functools.partial(_combine_kernel, tm=tm),
        grid=(n,),
        in_specs=[pl.BlockSpec((tm,), lambda i: (i,), memory_space=pltpu.SMEM),
                  pl.BlockSpec((tm,), lambda i: (jnp.minimum(i + 1, n - 1),), memory_space=pltpu.SMEM),
                  pl.BlockSpec((tm // SUBLANES, SUBLANES, D_MODEL), lambda i: (i, 0, 0)),
                  pl.BlockSpec((1, D_MODEL), lambda i: (0, 0)),
                  pl.BlockSpec(memory_space=pl.ANY)],
        out_specs=pl.BlockSpec((tm // SUBLANES, SUBLANES, D_MODEL), lambda i: (i, 0, 0)),
        out_shape=jax.ShapeDtypeStruct((t_len // SUBLANES, SUBLANES, D_MODEL), F32),
        scratch_shapes=[pltpu.VMEM((2, tm // SUBLANES, SUBLANES, D_MODEL), F32), pltpu.SemaphoreType.DMA((2,))],
        compiler_params=pltpu.CompilerParams(dimension_semantics=("arbitrary",), vmem_limit_bytes=VMEM_LIMIT),
        name="combine",
    )(dest, dest, x1.reshape(t_len // SUBLANES, SUBLANES, D_MODEL), gf, ys)


def _layer(x, norm1_g, w_in, b_f, w_dw, b_dw, conv_ln_g, conv_ln_b, out_g_conv, out_g_att, w_out,
           norm2_g, w_r1, b_r1, w_r2, b_r2, w_gate, w_up, w_down):
    batch, seq, d = x.shape
    t_len = batch * seq
    tm = min(512, seq)
    tk = min(256, seq)
    tq = min(1024, seq)
    o2 = 2 * D_CONV
    o3, o4, o5 = o2 + D_ATT, o2 + 2 * D_ATT, o2 + 3 * D_ATT

    assert PIECE_STRIDE == N_HEADS
    pad = LANES - N_PIECES * PIECE_STRIDE
    wf = jnp.pad(jnp.tile(w_in[:, o5:], (1, N_PIECES)), ((0, 0), (0, pad)))
    wrow = jnp.concatenate([w_in[:, :o2], w_in[:, o3:o4], wf], axis=1).astype(BF16)
    wqvT = jnp.concatenate([w_in[:, o2:o3] * (HEAD_DIM ** -0.5), w_in[:, o4:o5]], axis=1).T.astype(BF16)
    bf3 = jnp.pad(jnp.tile(b_f.astype(F32).reshape(1, N_HEADS), (1, N_PIECES)), ((0, 0), (0, pad)))

    x2 = x.reshape(t_len, d)
    a, kp, qpT, vT, kstat, qstat = _inproj(x2, norm1_g.reshape(1, d), wrow, bf3, wqvT,
                                           batch=batch, seq=seq, tm=tm, tk=tk)
    conv_params = (w_dw, b_dw.reshape(1, -1), conv_ln_g.reshape(1, -1), conv_ln_b.reshape(1, -1),
                   out_g_conv.reshape(1, -1))
    g0, bound = _first_groups(kstat, qstat, tm=tm, tq=tq)
    yT = _attention(g0, bound, qpT, kp, vT, batch=batch, seq=seq, tm=tm, tq=tq, tk=tk)

    gpad = SUBLANES - N_GROUPS
    rpad = LANES - SUBLANES - N_EXPERTS
    wrT = jnp.concatenate([w_r1.T, jnp.zeros((gpad, d), F32),
                           jnp.transpose(w_r2, (0, 2, 1)).reshape(N_EXPERTS, d), jnp.zeros((rpad, d), F32)], axis=0)
    brT = jnp.concatenate([b_r1.astype(F32), jnp.full((gpad,), -jnp.inf, F32), b_r2.reshape(-1).astype(F32),
                           jnp.zeros((rpad,), F32)]).reshape(LANES, 1)
    x1, h2, route, cnt = _outproj(x2, a, yT, conv_params, w_out.astype(BF16),
                                  out_g_att.reshape(1, -1), norm2_g.reshape(1, d),
                                  wrT.astype(BF16), brT, batch=batch, seq=seq, tm=tm, rc=min(256, seq))

    i32 = jnp.int32
    lanes = jnp.arange(LANES, dtype=i32)
    pick = lambda table, idx: jnp.sum(jnp.where(idx[:, None] == lanes, table[None, :], 0), axis=1).astype(i32)
    counts = cnt[:, 0].astype(i32)
    ends = jnp.cumsum(counts).astype(i32)
    starts = ends - counts
    dest = pick(starts, route[0].astype(i32)) + route[1].astype(i32)

    n_blocks = t_len // ROW_BLOCK
    b_lo = starts // ROW_BLOCK
    n_it = jnp.where(counts > 0, (ends - 1) // ROW_BLOCK - b_lo + 1, 0)
    it_end = jnp.cumsum(n_it).astype(i32)
    it_start = it_end - n_it
    idx = jnp.arange(n_blocks + N_CLASSES, dtype=i32)
    valid = idx < it_end[-1]
    last_cls = jnp.max(jnp.where(counts > 0, lanes, 0))
    cls = jnp.where(valid, jnp.sum(it_end[None, :] <= idx[:, None], axis=1).astype(i32), last_cls)
    blk = jnp.where(valid, pick(b_lo, cls) + idx - pick(it_start, cls), n_blocks - 1)
    row0 = blk * ROW_BLOCK
    lo = jnp.where(valid, jnp.maximum(pick(starts, cls), row0) - row0, 0)
    hi = jnp.where(valid, jnp.minimum(pick(ends, cls), row0 + ROW_BLOCK) - row0, 0)
    pair_a, pair_b = [], []
    for g in range(N_GROUPS):
        for a_loc in range(EXPERTS_PER_GROUP):
            for b_loc in range(a_loc + 1, EXPERTS_PER_GROUP):
                pair_a.append(g * EXPERTS_PER_GROUP + a_loc)
                pair_b.append(g * EXPERTS_PER_GROUP + b_loc)
    cpad = [0] * (LANES - N_CLASSES)
    items = (blk.astype(i32), pick(jnp.array(pair_a + cpad, i32), cls), pick(jnp.array(pair_b + cpad, i32), cls),
             lo.astype(i32), hi.astype(i32))

    xs = _dispatch(dest, h2, tm=min(2048, t_len))
    ys = _experts(items, xs, jnp.concatenate([w_gate, w_up], axis=-1).astype(BF16), w_down.astype(BF16))
    return dest, x1, ys


def kernel(x, norm1_g, w_in, b_f, w_dw, b_dw, conv_ln_g, conv_ln_b, out_g_conv, out_g_att, w_out, norm2_g,
           w_r1, b_r1, w_r2, b_r2, w_gate, w_up, w_down, final_g):
    assert norm1_g.shape[0] == 1, "single-layer stack"
    batch, seq, d = x.shape
    dest, x1, ys = _layer(
        x, norm1_g[0], w_in[0], b_f[0], w_dw[0], b_dw[0], conv_ln_g[0], conv_ln_b[0], out_g_conv[0],
        out_g_att[0], w_out[0], norm2_g[0], w_r1[0], b_r1[0], w_r2[0], b_r2[0], w_gate[0], w_up[0], w_down[0])
    out = _combine(dest, x1, final_g.reshape(1, d), ys, tm=min(1024, batch * seq))
    return out.reshape(batch, seq, d)
```

```python
import functools

import jax
import jax.numpy as jnp
from jax import lax
from jax.experimental import pallas as pl
from jax.experimental.pallas import tpu as pltpu

D_MODEL = 1024
D_CONV = 512
N_HEADS = 8
HEAD_DIM = 64
D_ATT = N_HEADS * HEAD_DIM
CONV_WIDTH = 31
N_GROUPS = 4
EXPERTS_PER_GROUP = 8
N_EXPERTS = N_GROUPS * EXPERTS_PER_GROUP
D_EXPERT = D_MODEL // 4
ROW_BLOCK = 256
EPS = 1e-6

LANES = 128
SUBLANES = 8
KAUG = 128
N_PIECES = 3
PIECE_STRIDE = 8
CONV_HALO = 32
PAIRS_PER_GROUP = EXPERTS_PER_GROUP * (EXPERTS_PER_GROUP - 1) // 2
N_CLASSES = N_GROUPS * PAIRS_PER_GROUP
ROW_W = D_MODEL + LANES
SKIP_LOG2 = 160.0
FIXED_REF_MAX_BOUND = 40.0
NORM_SLACK = 1.02
V_ROWS = 80
LOG2E = 1.4426950408889634
VMEM_LIMIT = 56 * 1024 * 1024

F32 = jnp.float32
BF16 = jnp.bfloat16


def _dot(a, b):
    return jnp.dot(a, b, preferred_element_type=F32)


def _dot_nt(a, b):
    return lax.dot_general(a, b, (((1,), (1,)), ((), ())), preferred_element_type=F32)


def _split3(x):
    hi = x.astype(BF16)
    r1 = x - hi.astype(F32)
    mid = r1.astype(BF16)
    lo = (r1 - mid.astype(F32)).astype(BF16)
    return hi.astype(F32), mid.astype(F32), lo.astype(F32)


def _piece_lane_mask(lane, h):
    return (lane == h) | (lane == h + PIECE_STRIDE) | (lane == h + 2 * PIECE_STRIDE)


def _inproj_kernel(x_ref, g1_ref, wrow_ref, bf_ref, wqvT_ref, ltri_ref, hsel_ref,
                   a_ref, kp_ref, qpT_ref, vT_ref, kstat_ref, qstat_ref, carry_ref, *, tm, tk):
    @pl.when(pl.program_id(1) == 0)
    def _():
        carry_ref[...] = jnp.zeros_like(carry_ref)

    x = x_ref[...]
    ms = jnp.mean(x * x, axis=-1, keepdims=True)
    hb = ((x * lax.rsqrt(ms + EPS)) * g1_ref[...]).astype(BF16)

    z = _dot(hb, wrow_ref[...])
    a_ref[...] = z[:, :D_CONV] * jax.nn.sigmoid(z[:, D_CONV:2 * D_CONV])

    kk = z[:, 2 * D_CONV:2 * D_CONV + D_ATT]

    zf = z[:, 2 * D_CONV + D_ATT:] + bf_ref[...]
    lf = jnp.minimum(zf, 0.0) - jnp.log1p(jnp.exp(-jnp.abs(zf)))
    lane = lax.broadcasted_iota(jnp.int32, (tm, LANES), 1)
    hi, mid, lo = _split3(lf)
    lf3 = jnp.where(lane < PIECE_STRIDE, hi,
                    jnp.where(lane < 2 * PIECE_STRIDE, mid,
                              jnp.where(lane < 3 * PIECE_STRIDE, lo, 0.0))).astype(BF16)
    cs3 = _dot(ltri_ref[...], lf3)
    c = (cs3 + pltpu.roll(cs3, LANES - PIECE_STRIDE, 1)
         + pltpu.roll(cs3, LANES - 2 * PIECE_STRIDE, 1)) + carry_ref[...]
    carry_ref[...] = c[tm - 1:tm, :]

    nhi, nmid, nlo = _split3(c * (-LOG2E))
    p3 = jnp.where(lane < PIECE_STRIDE, nhi,
                   jnp.where(lane < 2 * PIECE_STRIDE, pltpu.roll(nmid, PIECE_STRIDE, 1),
                             jnp.where(lane < 3 * PIECE_STRIDE, pltpu.roll(nlo, 2 * PIECE_STRIDE, 1), 0.0)))
    p3_hi = pltpu.roll(p3, HEAD_DIM, 1)
    q_pieces = (-p3).T[0:HEAD_DIM]
    q_shift = N_PIECES * PIECE_STRIDE

    qvT = _dot_nt(wqvT_ref[...], hb)
    qT = qvT[:D_ATT] * LOG2E
    vT = qvT[D_ATT:]
    row = lax.broadcasted_iota(jnp.int32, (HEAD_DIM, tm), 0)
    vrow = lax.broadcasted_iota(jnp.int32, (V_ROWS - HEAD_DIM, tk), 0)
    v_tail = jnp.where(vrow == 0, 1.0, 0.0).astype(BF16)

    for h in range(N_HEADS):
        kcol = kk[:, (h // 2) * LANES:(h // 2 + 1) * LANES]
        f_h = jnp.where(_piece_lane_mask(row, h), q_pieces, 0.0)
        aug_q = jnp.where(_piece_lane_mask(row, h), 1.0, 0.0) + jnp.concatenate(
            [jnp.zeros((q_shift, tm), F32), f_h[0:HEAD_DIM - q_shift]], axis=0)
        q_h = qT[h * HEAD_DIM:(h + 1) * HEAD_DIM, :]
        if h % 2 == 0:
            ext = jnp.where(_piece_lane_mask(lane, h + HEAD_DIM), p3_hi,
                            jnp.where(_piece_lane_mask(lane, h + HEAD_DIM + q_shift), 1.0, 0.0))
            kp = jnp.where(lane < HEAD_DIM, kcol, ext)
            qp = jnp.concatenate([q_h, aug_q], axis=0)
        else:
            ext = jnp.where(_piece_lane_mask(lane, h), p3,
                            jnp.where(_piece_lane_mask(lane, h + q_shift), 1.0, 0.0))
            kp = jnp.where(lane >= HEAD_DIM, kcol, ext)
            qp = jnp.concatenate([aug_q, q_h], axis=0)
        qpT_ref[0, h] = qp.astype(BF16)
        for cidx in range(tm // tk):
            kp_ref[0, h, cidx] = kp[cidx * tk:(cidx + 1) * tk, :].astype(BF16)
            v_h = vT[h * HEAD_DIM:(h + 1) * HEAD_DIM, cidx * tk:(cidx + 1) * tk].astype(BF16)
            vT_ref[0, h, cidx] = jnp.concatenate([v_h, v_tail], axis=0)

    kn2 = jnp.max(_dot((kk * kk).astype(BF16), hsel_ref[...]), axis=0, keepdims=True)
    srow = lax.broadcasted_iota(jnp.int32, (SUBLANES, LANES), 0)
    kstat_ref[0, 0] = jnp.where(srow == 0, c[0:1, :], jnp.where(srow == 1, c[tm - 1:tm, :],
                                                                 jnp.where(srow == 2, kn2, 0.0)))
    q2 = qT * qT
    qn2 = jnp.concatenate([jnp.sum(q2[h * HEAD_DIM:(h + 1) * HEAD_DIM, :], axis=0, keepdims=True)
                           for h in range(N_HEADS)], axis=0)
    qstat_ref[0, 0] = jnp.broadcast_to(jnp.max(qn2, axis=1, keepdims=True), (N_HEADS, LANES))


def _inproj(x2, g1, wrow, bf3, wqvT, *, batch, seq, tm, tk):
    nt = seq // tm
    nk = seq // tk
    ltri = jnp.tril(jnp.ones((tm, tm), F32)).astype(BF16)
    hsel = (jnp.arange(D_ATT)[:, None] // HEAD_DIM == jnp.arange(LANES)[None, :]).astype(BF16)
    const = lambda shape: pl.BlockSpec(shape, lambda b, t: (0,) * len(shape))
    stat_spec = pl.BlockSpec((1, 1, SUBLANES, LANES), lambda b, t: (b, t, 0, 0))
    stat_shape = jax.ShapeDtypeStruct((batch, nt, SUBLANES, LANES), F32)
    return pl.pallas_call(
        functools.partial(_inproj_kernel, tm=tm, tk=tk),
        grid=(batch, nt),
        in_specs=[
            pl.BlockSpec((tm, D_MODEL), lambda b, t: (b * nt + t, 0)),
            const((1, D_MODEL)), const((D_MODEL, 2 * D_CONV + D_ATT + LANES)), const((1, LANES)),
            const((2 * D_ATT, D_MODEL)), const((tm, tm)), const((D_ATT, LANES)),
        ],
        out_specs=[
            pl.BlockSpec((tm, D_CONV), lambda b, t: (b * nt + t, 0)),
            pl.BlockSpec((1, N_HEADS, tm // tk, tk, KAUG), lambda b, t: (b, 0, t, 0, 0)),
            pl.BlockSpec((1, N_HEADS, KAUG, tm), lambda b, t: (b, 0, 0, t)),
            pl.BlockSpec((1, N_HEADS, tm // tk, V_ROWS, tk), lambda b, t: (b, 0, t, 0, 0)),
            stat_spec, stat_spec,
        ],
        out_shape=[
            jax.ShapeDtypeStruct((batch * seq, D_CONV), F32),
            jax.ShapeDtypeStruct((batch, N_HEADS, nk, tk, KAUG), BF16),
            jax.ShapeDtypeStruct((batch, N_HEADS, KAUG, seq), BF16),
            jax.ShapeDtypeStruct((batch, N_HEADS, nk, V_ROWS, tk), BF16),
            stat_shape, stat_shape,
        ],
        scratch_shapes=[pltpu.VMEM((1, LANES), F32)],
        compiler_params=pltpu.CompilerParams(
            dimension_semantics=("arbitrary", "arbitrary"), vmem_limit_bytes=VMEM_LIMIT),
        name="inproj",
    )(x2, g1, wrow, bf3, wqvT, ltri, hsel)


def _conv_tile(a_ref, w_ref, b_ref, lng_ref, lnb_ref, og_ref, sh_ref, acc_ref, *, tc, rc):
    n_cb = D_CONV // LANES

    @pl.when(pl.program_id(1) == 0)
    def _():
        for cb in range(n_cb):
            sh_ref[0, cb, 0:CONV_HALO, :] = jnp.zeros((CONV_HALO, LANES), F32)

    @pl.when(pl.program_id(1) > 0)
    def _():
        for cb in range(n_cb):
            sh_ref[0, cb, 0:CONV_HALO, :] = sh_ref[0, cb, tc:tc + CONV_HALO, :]

    n_sh = tc + CONV_HALO - SUBLANES
    for cb in range(n_cb):
        sh_ref[0, cb, CONV_HALO:CONV_HALO + tc, :] = a_ref[:, cb * LANES:(cb + 1) * LANES]
        for f in range(1, SUBLANES):
            sh_ref[f, cb, 0:n_sh, :] = sh_ref[0, cb, f:f + n_sh, :]

    base = CONV_HALO - (CONV_WIDTH - 1)
    for cb in range(n_cb):
        cols = slice(cb * LANES, (cb + 1) * LANES)

        def chunk(c, carry, cb=cb, cols=cols):
            r0 = pl.multiple_of(c * rc, rc)
            acc = jnp.zeros((rc, LANES), F32)
            for j in range(CONV_WIDTH):
                f = (base + j) % SUBLANES
                acc = acc + w_ref[j:j + 1, cols] * sh_ref[f, cb, pl.ds(r0 + (base + j - f), rc), :]
            acc_ref[pl.ds(r0, rc), cols] = acc
            return carry

        lax.fori_loop(0, tc // rc, chunk, 0)

    y = acc_ref[...] + b_ref[...]
    mu = jnp.mean(y, axis=-1, keepdims=True)
    yc = y - mu
    var = jnp.mean(yc * yc, axis=-1, keepdims=True)
    yn = yc * lax.rsqrt(var + EPS) * lng_ref[...] + lnb_ref[...]
    s = yn * jax.nn.sigmoid(yn)
    ms = jnp.mean(s * s, axis=-1, keepdims=True)
    return (s * lax.rsqrt(ms + EPS) * og_ref[...]).astype(BF16)


def _attn_fixed_ref_kernel(g0_ref, bnd_ref, qT_ref, k_ref, v_ref, o_ref, *, tq, tk, unit):
    i = pl.program_id(2)
    qT = qT_ref[0, 0]
    n_sub = tq // tk
    bnd = bnd_ref[0, 0, 0:1, 0:1]
    sb0 = unit * g0_ref[(pl.program_id(0) * N_HEADS + pl.program_id(1)) * pl.num_programs(2) + i]

    def values(sb, n):
        return jnp.concatenate([v_ref[0, 0, sb + d] for d in range(n)], axis=1)

    def probs(sb, n):
        return jnp.concatenate([jnp.exp2(_dot(k_ref[0, 0, sb + d], qT) - bnd).astype(BF16)
                                for d in range(n)], axis=0)

    def two_groups(p, acc):
        sb = sb0 + 2 * n_sub * p
        acc = acc + _dot(values(sb, n_sub), probs(sb, n_sub))
        return acc + _dot(values(sb + n_sub, n_sub), probs(sb + n_sub, n_sub))

    key = lax.broadcasted_iota(jnp.int32, (tk, tk), 0)
    qry = lax.broadcasted_iota(jnp.int32, (tk, tk), 1)

    def diagonal(acc):
        rows = []
        for d in range(n_sub):
            s = _dot(k_ref[0, 0, i * n_sub + d], qT[:, d * tk:])
            tiles = [jnp.zeros((tk, d * tk), BF16)] if d else []
            tiles.append(jnp.exp2(jnp.where(key <= qry, s[:, :tk], -jnp.inf) - bnd).astype(BF16))
            if d + 1 < n_sub:
                tiles.append(jnp.exp2(s[:, tk:] - bnd).astype(BF16))
            rows.append(jnp.concatenate(tiles, axis=1))
        acc = acc + _dot(values(i * n_sub, n_sub), jnp.concatenate(rows, axis=0))
        o_ref[0] = acc[:HEAD_DIM] / acc[HEAD_DIM:HEAD_DIM + 1]

    n_before = i * n_sub - sb0
    acc = lax.fori_loop(0, n_before // (2 * n_sub), two_groups, jnp.zeros((V_ROWS, tq), F32))
    left = n_before % (2 * n_sub)

    for n in range(0, 2 * n_sub, unit):
        @pl.when(left == n)
        def _(n=n):
            diagonal(acc + _dot(values(i * n_sub - n, n), probs(i * n_sub - n, n)) if n else acc)


def _attn_kernel(g0_ref, qT_ref, k_ref, v_ref, o_ref, sa_ref, sb_ref, cma_ref, cmb_ref, *, tq, tk):
    i = pl.program_id(2)
    qT = qT_ref[0, 0]
    n_sub = tq // tk

    def scores(g, s_ref, cm_ref):
        for d in range(n_sub):
            s = _dot(k_ref[0, 0, g * n_sub + d], qT)
            s_ref[d] = s
            cm_ref[d] = jnp.max(s, axis=0, keepdims=True)

    def consume(g, s_ref, cm_ref, carry, nxt=None):
        m, acc = carry
        m_new = m
        for d in range(n_sub):
            m_new = jnp.maximum(m_new, cm_ref[d])
        ps = []
        for d in range(n_sub):
            if nxt is not None:
                g_n, sn_ref, cmn_ref = nxt
                s = _dot(k_ref[0, 0, g_n * n_sub + d], qT)
                sn_ref[d] = s
                cmn_ref[d] = jnp.max(s, axis=0, keepdims=True)
            ps.append(jnp.exp2(s_ref[d] - m_new).astype(BF16))
        p = jnp.concatenate(ps, axis=0)
        v = jnp.concatenate([v_ref[0, 0, g * n_sub + d] for d in range(n_sub)], axis=1)
        return m_new, jnp.exp2(m - m_new) * acc + _dot(v, p)

    key = lax.broadcasted_iota(jnp.int32, (tk, tk), 0)
    qry = lax.broadcasted_iota(jnp.int32, (tk, tk), 1)

    def consume_diag(s_ref, cm_ref, carry):
        m, acc = carry
        lanes = lambda j: slice(j * tk, (j + 1) * tk)
        causal = [jnp.where(key <= qry, s_ref[d, :, lanes(d)], -jnp.inf) for d in range(n_sub)]
        m_tiles = []
        for j in range(n_sub):
            m_j = jnp.maximum(m[:, lanes(j)], jnp.max(causal[j], axis=0, keepdims=True))
            for d in range(j):
                m_j = jnp.maximum(m_j, cm_ref[d, :, lanes(j)])
            m_tiles.append(m_j)
        m_new = jnp.concatenate(m_tiles, axis=1)
        rows = []
        for d in range(n_sub):
            tiles = [jnp.zeros((tk, d * tk), BF16)] if d else []
            tiles.append(jnp.exp2(causal[d] - m_tiles[d]).astype(BF16))
            if d + 1 < n_sub:
                rest = slice((d + 1) * tk, tq)
                tiles.append(jnp.exp2(s_ref[d, :, rest] - m_new[:, rest]).astype(BF16))
            rows.append(jnp.concatenate(tiles, axis=1))
        p = jnp.concatenate(rows, axis=0)
        v = jnp.concatenate([v_ref[0, 0, i * n_sub + d] for d in range(n_sub)], axis=1)
        acc = jnp.exp2(m - m_new) * acc + _dot(v, p)
        o_ref[0] = acc[:HEAD_DIM] / acc[HEAD_DIM:HEAD_DIM + 1]

    g0 = g0_ref[(pl.program_id(0) * N_HEADS + pl.program_id(1)) * pl.num_programs(2) + i]
    n_full = i - g0

    def pair(p, carry):
        g = g0 + 2 * p
        carry = consume(g, sa_ref, cma_ref, carry, nxt=(g + 1, sb_ref, cmb_ref))
        return consume(g + 1, sb_ref, cmb_ref, carry, nxt=(g + 2, sa_ref, cma_ref))

    scores(g0, sa_ref, cma_ref)
    init = (jnp.full((1, tq), -jnp.inf, F32), jnp.zeros((V_ROWS, tq), F32))
    carry = lax.fori_loop(0, n_full // 2, pair, init)

    @pl.when(n_full % 2 == 1)
    def _():
        consume_diag(sb_ref, cmb_ref, consume(i - 1, sa_ref, cma_ref, carry, nxt=(i, sb_ref, cmb_ref)))

    @pl.when(n_full % 2 == 0)
    def _():
        consume_diag(sa_ref, cma_ref, carry)


def _first_groups(kstat, qstat, *, tm, tq):
    per = tq // tm
    bound = jnp.sqrt(jnp.max(kstat[:, :, 2, :N_HEADS], axis=1) * jnp.max(qstat[:, :, :, 0], axis=1))
    bound = bound * NORM_SLACK
    cq = kstat[:, ::per, 0, :N_HEADS]
    ck = kstat[:, :, 1, :N_HEADS]
    d = (cq[:, :, None, :] - ck[:, None, :, :]) * LOG2E
    nq, ng = cq.shape[1], ck.shape[1]
    earlier = per * jnp.arange(nq)[None, :, None, None] > jnp.arange(ng)[None, None, :, None]
    skip = earlier & (2.0 * bound[:, None, None, :] + d < -SKIP_LOG2)
    g0 = jnp.sum(jnp.cumprod(skip.astype(jnp.int32), axis=2), axis=2)
    return jnp.transpose(g0, (0, 2, 1)).reshape(-1).astype(jnp.int32), bound


def _attention(g0, bound, qpT, kp, vT, *, batch, seq, tm, tq, tk):
    nk = seq // tk
    params = pltpu.CompilerParams(
        dimension_semantics=("arbitrary", "arbitrary", "arbitrary"), vmem_limit_bytes=VMEM_LIMIT)
    out_shape = jax.ShapeDtypeStruct((batch, D_ATT, seq), F32)
    qkv_specs = [
        pl.BlockSpec((1, 1, KAUG, tq), lambda b, h, i, g0: (b, h, 0, i)),
        pl.BlockSpec((1, 1, nk, tk, KAUG), lambda b, h, i, g0: (b, h, 0, 0, 0)),
        pl.BlockSpec((1, 1, nk, V_ROWS, tk), lambda b, h, i, g0: (b, h, 0, 0, 0)),
    ]
    out_spec = pl.BlockSpec((1, HEAD_DIM, tq), lambda b, h, i, g0: (b, h, i))

    def fixed_ref(g0, bnd, qpT, kp, vT):
        grid_spec = pltpu.PrefetchScalarGridSpec(
            num_scalar_prefetch=1, grid=(batch, N_HEADS, seq // tq),
            in_specs=[pl.BlockSpec((1, 1, SUBLANES, LANES), lambda b, h, i, g0: (b, h, 0, 0))] + qkv_specs,
            out_specs=out_spec)
        return pl.pallas_call(functools.partial(_attn_fixed_ref_kernel, tq=tq, tk=tk, unit=tm // tk),
                              grid_spec=grid_spec, out_shape=out_shape, compiler_params=params,
                              name="attention_fixed_ref")(g0, bnd, qpT, kp, vT)

    def running_max(g0, bnd, qpT, kp, vT):
        del bnd
        grid_spec = pltpu.PrefetchScalarGridSpec(
            num_scalar_prefetch=1, grid=(batch, N_HEADS, seq // tq), in_specs=qkv_specs, out_specs=out_spec,
            scratch_shapes=[pltpu.VMEM((tq // tk, tk, tq), F32), pltpu.VMEM((tq // tk, tk, tq), F32),
                            pltpu.VMEM((tq // tk, 1, tq), F32), pltpu.VMEM((tq // tk, 1, tq), F32)])
        return pl.pallas_call(functools.partial(_attn_kernel, tq=tq, tk=tk), grid_spec=grid_spec,
                              out_shape=out_shape, compiler_params=params, name="attention",
                              )(g0 // (tq // tm), qpT, kp, vT)

    bnd = jnp.broadcast_to(bound[:, :, None, None], (batch, N_HEADS, SUBLANES, LANES))
    return lax.cond(jnp.max(bound) <= FIXED_REF_MAX_BOUND, fixed_ref, running_max, g0, bnd, qpT, kp, vT)


def _outproj_kernel(x_ref, a_ref, yT_ref, wdw_ref, bdw_ref, lng_ref, lnb_ref, gc_ref, wo_ref, ga_ref,
                    g2_ref, wrT_ref, brT_ref, utri_ref, x1_ref, h2_ref, route_ref, cnt_ref,
                    carry_ref, sh_ref, acc_ref, *, tm, rc):
    first = (pl.program_id(0) == 0) & (pl.program_id(1) == 0)

    @pl.when(first)
    def _():
        carry_ref[...] = jnp.zeros_like(carry_ref)

    mc = _conv_tile(a_ref, wdw_ref, bdw_ref, lng_ref, lnb_ref, gc_ref, sh_ref, acc_ref, tc=tm, rc=rc)
    yT = yT_ref[0]
    msa = jnp.mean(yT * yT, axis=0, keepdims=True)
    yn = (yT * lax.rsqrt(msa + EPS)).T * ga_ref[...]
    x1 = x_ref[...] + _dot(jnp.concatenate([mc, yn.astype(BF16)], axis=1), wo_ref[...])
    x1_ref[...] = x1
    ms = jnp.mean(x1 * x1, axis=-1, keepdims=True)
    h2 = (x1 * lax.rsqrt(ms + EPS)) * g2_ref[...]
    h2_ref[:, :D_MODEL] = h2

    lg = _dot_nt(wrT_ref[...], h2.astype(BF16)) + brT_ref[...]
    neg = -jnp.inf
    sub = lax.broadcasted_iota(jnp.int32, (SUBLANES, tm), 0)
    big = jnp.int32(SUBLANES)
    first_of = lambda hit: jnp.min(jnp.where(hit, sub, big), axis=0, keepdims=True)

    lg1 = lg[0:SUBLANES]
    m1 = jnp.max(lg1, axis=0, keepdims=True)
    p1_sel = 1.0 / jnp.sum(jnp.exp(lg1 - m1), axis=0, keepdims=True)
    grp = first_of(lg1 == m1)

    slab = lambda g: lg[SUBLANES * (g + 1):SUBLANES * (g + 2)]
    v = slab(N_GROUPS - 1)
    for g in range(N_GROUPS - 2, -1, -1):
        v = jnp.where(grp == g, slab(g), v)
    v1 = jnp.max(v, axis=0, keepdims=True)
    j1 = first_of(v == v1)
    vv = jnp.where(sub == j1, neg, v)
    v2 = jnp.max(vv, axis=0, keepdims=True)
    j2 = first_of(vv == v2)
    e21 = jnp.exp(v2 - v1)
    w0 = p1_sel / (1.0 + e21)
    w1 = p1_sel * e21 / (1.0 + e21)

    swap = j2 < j1
    al = jnp.where(swap, j2, j1)
    bl = jnp.where(swap, j1, j2)
    wa = jnp.where(swap, w1, w0)
    wb = jnp.where(swap, w0, w1)
    cid = PAIRS_PER_GROUP * grp + ((al * (2 * EXPERTS_PER_GROUP - 1 - al)) >> 1) + (bl - al - 1)

    cls = lax.broadcasted_iota(jnp.int32, (LANES, tm), 0)
    oh = cls == cid
    cmat = jnp.where(oh, 1.0, 0.0)
    prefix = _dot(cmat.astype(BF16), utri_ref[...]) + carry_ref[...]
    rank = jnp.sum(jnp.where(oh, prefix, 0.0), axis=0, keepdims=True)
    counts = prefix[:, tm - 1:tm] + cmat[:, tm - 1:tm]
    carry_ref[...] = counts
    cnt_ref[...] = counts

    h2_ref[:, D_MODEL:] = jnp.where(cls == 0, wa, jnp.where(cls == 1, wb, 0.0)).T
    route_ref[...] = jnp.where(sub == 0, cid.astype(F32), jnp.where(sub == 1, rank, 0.0))


def _outproj(x2, a, yT, conv_params, wo, ga, g2, wrT, brT, *, batch, seq, tm, rc):
    nt = seq // tm
    utri = jnp.triu(jnp.ones((tm, tm), F32), 1).astype(BF16)
    const = lambda shape: pl.BlockSpec(shape, lambda b, t: (0,) * len(shape))
    row_spec = lambda w: pl.BlockSpec((tm, w), lambda b, t: (b * nt + t, 0))
    return pl.pallas_call(
        functools.partial(_outproj_kernel, tm=tm, rc=rc),
        grid=(batch, nt),
        in_specs=[row_spec(D_MODEL), row_spec(D_CONV),
                  pl.BlockSpec((1, D_ATT, tm), lambda b, t: (b, 0, t)),
                  const((CONV_WIDTH, D_CONV)), const((1, D_CONV)), const((1, D_CONV)), const((1, D_CONV)),
                  const((1, D_CONV)),
                  const((D_CONV + D_ATT, D_MODEL)), const((1, D_ATT)), const((1, D_MODEL)),
                  const((LANES, D_MODEL)), const((LANES, 1)), const((tm, tm))],
        out_specs=[row_spec(D_MODEL), row_spec(ROW_W),
                   pl.BlockSpec((SUBLANES, tm), lambda b, t: (0, b * nt + t)), const((LANES, 1))],
        out_shape=[jax.ShapeDtypeStruct((batch * seq, D_MODEL), F32),
                   jax.ShapeDtypeStruct((batch * seq, ROW_W), F32),
                   jax.ShapeDtypeStruct((SUBLANES, batch * seq), F32),
                   jax.ShapeDtypeStruct((LANES, 1), F32)],
        scratch_shapes=[pltpu.VMEM((LANES, 1), F32),
                        pltpu.VMEM((SUBLANES, D_CONV // LANES, tm + CONV_HALO, LANES), F32),
                        pltpu.VMEM((tm, D_CONV), F32)],
        compiler_params=pltpu.CompilerParams(
            dimension_semantics=("arbitrary", "arbitrary"), vmem_limit_bytes=VMEM_LIMIT),
        name="outproj",
    )(x2, a, yT, *conv_params, wo, ga, g2, wrT, brT, utri)


def _row_copies(src_at, dst_at, sem):
    return pltpu.make_async_copy(src_at, dst_at, sem)


def _dispatch_kernel(dest_ref, h2_ref, xs_ref, sem, *, tm):
    def body(g, carry):
        for u in range(SUBLANES):
            _row_copies(h2_ref.at[g, pl.ds(u, 1)], xs_ref.at[pl.ds(dest_ref[g * SUBLANES + u], 1)],
                        sem).start(priority=u % 2)
        return carry

    lax.fori_loop(0, tm // SUBLANES, body, 0)
    done = xs_ref.at[pl.ds(0, tm)]
    _row_copies(done, done, sem).wait()


def _dispatch(dest, h2, *, tm):
    t_len = h2.shape[0]
    return pl.pallas_call(
        functools.partial(_dispatch_kernel, tm=tm),
        grid=(t_len // tm,),
        in_specs=[pl.BlockSpec((tm,), lambda i: (i,), memory_space=pltpu.SMEM),
                  pl.BlockSpec((tm // SUBLANES, SUBLANES, ROW_W), lambda i: (i, 0, 0))],
        out_specs=pl.BlockSpec(memory_space=pl.ANY),
        out_shape=jax.ShapeDtypeStruct((t_len, ROW_W), F32),
        scratch_shapes=[pltpu.SemaphoreType.DMA(())],
        compiler_params=pltpu.CompilerParams(dimension_semantics=("arbitrary",), vmem_limit_bytes=VMEM_LIMIT),
        name="dispatch",
    )(dest, h2.reshape(t_len // SUBLANES, SUBLANES, ROW_W))


def _expert_kernel(blk_ref, ea_ref, eb_ref, lo_ref, hi_ref, x_ref, wgua_ref, wda_ref, wgub_ref, wdb_ref, y_ref):
    del blk_ref, ea_ref, eb_ref
    i = pl.program_id(0)
    lo = lo_ref[i]
    hi = hi_ref[i]
    half = ROW_BLOCK // 2

    def mlp(xb, wgu_ref, wd_ref):
        gu = _dot(xb, wgu_ref[0])
        g = gu[:, :D_EXPERT]
        return _dot(((g * jax.nn.sigmoid(g)) * gu[:, D_EXPERT:]).astype(BF16), wd_ref[0])

    def run(rows):
        n = rows.stop - rows.start
        xb = x_ref[rows, :D_MODEL].astype(BF16)
        wts = x_ref[rows, D_MODEL:]
        y = wts[:, 0:1] * mlp(xb, wgua_ref, wda_ref) + wts[:, 1:2] * mlp(xb, wgub_ref, wdb_ref)
        row = rows.start + lax.broadcasted_iota(jnp.int32, (n, 1), 0)
        mine = (row >= lo) & (row < hi)

        @pl.when(lo == 0)
        def _():
            y_ref[rows, :] = jnp.where(mine, y, 0.0)
            if n < ROW_BLOCK:
                y_ref[half:, :] = jnp.zeros((half, D_MODEL), F32)

        @pl.when(lo > 0)
        def _():
            y_ref[rows, :] = jnp.where(mine, y, y_ref[rows, :])

    @pl.when((hi > lo) & (lo < half) & (hi > half))
    def _():
        run(slice(0, ROW_BLOCK))

    @pl.when((hi > lo) & (hi <= half))
    def _():
        run(slice(0, half))

    @pl.when((hi > lo) & (lo >= half))
    def _():
        run(slice(half, ROW_BLOCK))


def _experts(items, xs, wgu, wd):
    n_rows = xs.shape[0]
    n_items = items[0].shape[0]
    wspec = lambda shape, which: pl.BlockSpec(
        shape, lambda i, blk, ea, eb, lo, hi: ((ea, eb)[which][i], 0, 0))
    grid_spec = pltpu.PrefetchScalarGridSpec(
        num_scalar_prefetch=5,
        grid=(n_items,),
        in_specs=[pl.BlockSpec((ROW_BLOCK, ROW_W), lambda i, blk, ea, eb, lo, hi: (blk[i], 0)),
                  wspec((1, D_MODEL, 2 * D_EXPERT), 0), wspec((1, D_EXPERT, D_MODEL), 0),
                  wspec((1, D_MODEL, 2 * D_EXPERT), 1), wspec((1, D_EXPERT, D_MODEL), 1)],
        out_specs=pl.BlockSpec((ROW_BLOCK, D_MODEL), lambda i, blk, ea, eb, lo, hi: (blk[i], 0)),
    )
    return pl.pallas_call(
        _expert_kernel,
        grid_spec=grid_spec,
        out_shape=jax.ShapeDtypeStruct((n_rows, D_MODEL), F32),
        compiler_params=pltpu.CompilerParams(dimension_semantics=("arbitrary",), vmem_limit_bytes=VMEM_LIMIT),
        name="experts",
    )(*items, xs, wgu, wd, wgu, wd)


def _combine_kernel(dcur_ref, dnxt_ref, x1_ref, g_ref, ys_ref, out_ref, buf_ref, sems, *, tm):
    i = pl.program_id(0)
    slot = i % 2

    def issue(d_ref, s):
        def body(g, carry):
            for u in range(SUBLANES):
                _row_copies(ys_ref.at[pl.ds(d_ref[g * SUBLANES + u], 1)], buf_ref.at[s, g, pl.ds(u, 1)],
                            sems.at[s]).start(priority=u % 2)
            return carry

        lax.fori_loop(0, tm // SUBLANES, body, 0)

    @pl.when(i == 0)
    def _():
        issue(dcur_ref, 0)

    @pl.when(i + 1 < pl.num_programs(0))
    def _():
        issue(dnxt_ref, 1 - slot)

    _row_copies(buf_ref.at[slot], buf_ref.at[slot], sems.at[slot]).wait()
    x2 = x1_ref[...] + buf_ref[slot]
    ms = jnp.mean(x2 * x2, axis=-1, keepdims=True)
    out_ref[...] = (x2 * lax.rsqrt(ms + EPS)) * g_ref[...]


def _combine(dest, x1, gf, ys, *, tm):
    t_len = x1.shape[0]
    n = t_len // tm
    return pl.pallas_call(
        functools.partial(_combine_kernel, tm=tm),
        grid=(n,),
        in_specs=[pl.BlockSpec((tm,), lambda i: (i,), memory_space=pltpu.SMEM),
                  pl.BlockSpec((tm,), lambda i: (jnp.minimum(i + 1, n - 1),), memory_space=pltpu.SMEM),
                  pl.BlockSpec((tm // SUBLANES, SUBLANES, D_MODEL), lambda i: (i, 0, 0)),
                  pl.BlockSpec((1, D_MODEL), lambda i: (0, 0)),
                  pl.BlockSpec(memory_space=pl.ANY)],
        out_specs=pl.BlockSpec((tm // SUBLANES, SUBLANES, D_MODEL), lambda i: (i, 0, 0)),
        out_shape=jax.ShapeDtypeStruct((t_len // SUBLANES, SUBLANES, D_MODEL), F32),
        scratch_shapes=[pltpu.VMEM((2, tm // SUBLANES, SUBLANES, D_MODEL), F32), pltpu.SemaphoreType.DMA((2,))],
        compiler_params=pltpu.CompilerParams(dimension_semantics=("arbitrary",), vmem_limit_bytes=VMEM_LIMIT),
        name="combine",
    )(dest, dest, x1.reshape(t_len // SUBLANES, SUBLANES, D_MODEL), gf, ys)


def _layer(x, norm1_g, w_in, b_f, w_dw, b_dw, conv_ln_g, conv_ln_b, out_g_conv, out_g_att, w_out,
           norm2_g, w_r1, b_r1, w_r2, b_r2, w_gate, w_up, w_down):
    batch, seq, d = x.shape
    t_len = batch * seq
    tm = min(512, seq)
    tk = min(256, seq)
    tq = min(1024, seq)
    o2 = 2 * D_CONV
    o3, o4, o5 = o2 + D_ATT, o2 + 2 * D_ATT, o2 + 3 * D_ATT

    assert PIECE_STRIDE == N_HEADS
    pad = LANES - N_PIECES * PIECE_STRIDE
    wf = jnp.pad(jnp.tile(w_in[:, o5:], (1, N_PIECES)), ((0, 0), (0, pad)))
    wrow = jnp.concatenate([w_in[:, :o2], w_in[:, o3:o4], wf], axis=1).astype(BF16)
    wqvT = jnp.concatenate([w_in[:, o2:o3] * (HEAD_DIM ** -0.5), w_in[:, o4:o5]], axis=1).T.astype(BF16)
    bf3 = jnp.pad(jnp.tile(b_f.astype(F32).reshape(1, N_HEADS), (1, N_PIECES)), ((0, 0), (0, pad)))

    x2 = x.reshape(t_len, d)
    a, kp, qpT, vT, kstat, qstat = _inproj(x2, norm1_g.reshape(1, d), wrow, bf3, wqvT,
                                           batch=batch, seq=seq, tm=tm, tk=tk)
    conv_params = (w_dw, b_dw.reshape(1, -1), conv_ln_g.reshape(1, -1), conv_ln_b.reshape(1, -1),
                   out_g_conv.reshape(1, -1))
    g0, bound = _first_groups(kstat, qstat, tm=tm, tq=tq)
    yT = _attention(g0, bound, qpT, kp, vT, batch=batch, seq=seq, tm=tm, tq=tq, tk=tk)

    gpad = SUBLANES - N_GROUPS
    rpad = LANES - SUBLANES - N_EXPERTS
    wrT = jnp.concatenate([w_r1.T, jnp.zeros((gpad, d), F32),
                           jnp.transpose(w_r2, (0, 2, 1)).reshape(N_EXPERTS, d), jnp.zeros((rpad, d), F32)], axis=0)
    brT = jnp.concatenate([b_r1.astype(F32), jnp.full((gpad,), -jnp.inf, F32), b_r2.reshape(-1).astype(F32),
                           jnp.zeros((rpad,), F32)]).reshape(LANES, 1)
    x1, h2, route, cnt = _outproj(x2, a, yT, conv_params, w_out.astype(BF16),
                                  out_g_att.reshape(1, -1), norm2_g.reshape(1, d),
                                  wrT.astype(BF16), brT, batch=batch, seq=seq, tm=tm, rc=min(256, seq))

    i32 = jnp.int32
    lanes = jnp.arange(LANES, dtype=i32)
    pick = lambda table, idx: jnp.sum(jnp.where(idx[:, None] == lanes, table[None, :], 0), axis=1).astype(i32)
    counts = cnt[:, 0].astype(i32)
    ends = jnp.cumsum(counts).astype(i32)
    starts = ends - counts
    dest = pick(starts, route[0].astype(i32)) + route[1].astype(i32)

    n_blocks = t_len // ROW_BLOCK
    b_lo = starts // ROW_BLOCK
    n_it = jnp.where(counts > 0, (ends - 1) // ROW_BLOCK - b_lo + 1, 0)
    it_end = jnp.cumsum(n_it).astype(i32)
    it_start = it_end - n_it
    idx = jnp.arange(n_blocks + N_CLASSES, dtype=i32)
    valid = idx < it_end[-1]
    last_cls = jnp.max(jnp.where(counts > 0, lanes, 0))
    cls = jnp.where(valid, jnp.sum(it_end[None, :] <= idx[:, None], axis=1).astype(i32), last_cls)
    blk = jnp.where(valid, pick(b_lo, cls) + idx - pick(it_start, cls), n_blocks - 1)
    row0 = blk * ROW_BLOCK
    lo = jnp.where(valid, jnp.maximum(pick(starts, cls), row0) - row0, 0)
    hi = jnp.where(valid, jnp.minimum(pick(ends, cls), row0 + ROW_BLOCK) - row0, 0)
    pair_a, pair_b = [], []
    for g in range(N_GROUPS):
        for a_loc in range(EXPERTS_PER_GROUP):
            for b_loc in range(a_loc + 1, EXPERTS_PER_GROUP):
                pair_a.append(g * EXPERTS_PER_GROUP + a_loc)
                pair_b.append(g * EXPERTS_PER_GROUP + b_loc)
    cpad = [0] * (LANES - N_CLASSES)
    items = (blk.astype(i32), pick(jnp.array(pair_a + cpad, i32), cls), pick(jnp.array(pair_b + cpad, i32), cls),
             lo.astype(i32), hi.astype(i32))

    xs = _dispatch(dest, h2, tm=min(4096, t_len))
    ys = _experts(items, xs, jnp.concatenate([w_gate, w_up], axis=-1).astype(BF16), w_down.astype(BF16))
    return dest, x1, ys


def kernel(x, norm1_g, w_in, b_f, w_dw, b_dw, conv_ln_g, conv_ln_b, out_g_conv, out_g_att, w_out, norm2_g,
           w_r1, b_r1, w_r2, b_r2, w_gate, w_up, w_down, final_g):
    assert norm1_g.shape[0] == 1, "single-layer stack"
    batch, seq, d = x.shape
    dest, x1, ys = _layer(
        x, norm1_g[0], w_in[0], b_f[0], w_dw[0], b_dw[0], conv_ln_g[0], conv_ln_b[0], out_g_conv[0],
        out_g_att[0], w_out[0], norm2_g[0], w_r1[0], b_r1[0], w_r2[0], b_r2[0], w_gate[0], w_up[0], w_down[0])
    out = _combine(dest, x1, final_g.reshape(1, d), ys, tm=min(512, batch * seq))
    return out.reshape(batch, seq, d)
```

```python
import functools

import jax
import jax.numpy as jnp
from jax import lax
from jax.experimental import pallas as pl
from jax.experimental.pallas import tpu as pltpu

D_MODEL = 1024
D_CONV = 512
N_HEADS = 8
HEAD_DIM = 64
D_ATT = N_HEADS * HEAD_DIM
CONV_WIDTH = 31
N_GROUPS = 4
EXPERTS_PER_GROUP = 8
N_EXPERTS = N_GROUPS * EXPERTS_PER_GROUP
D_EXPERT = D_MODEL // 4
ROW_BLOCK = 256
EPS = 1e-6

LANES = 128
SUBLANES = 8
KAUG = 128
N_PIECES = 3
PIECE_STRIDE = 8
CONV_HALO = 32
PAIRS_PER_GROUP = EXPERTS_PER_GROUP * (EXPERTS_PER_GROUP - 1) // 2
N_CLASSES = N_GROUPS * PAIRS_PER_GROUP
ROW_W = D_MODEL + LANES
SKIP_LOG2 = 160.0
FIXED_REF_MAX_BOUND = 40.0
GROUPS_PER_TRIP = 4
NORM_SLACK = 1.02
V_ROWS = 80
LOG2E = 1.4426950408889634
VMEM_LIMIT = 56 * 1024 * 1024

F32 = jnp.float32
BF16 = jnp.bfloat16


def _dot(a, b):
    return jnp.dot(a, b, preferred_element_type=F32)


def _dot_nt(a, b):
    return lax.dot_general(a, b, (((1,), (1,)), ((), ())), preferred_element_type=F32)


def _split3(x):
    hi = x.astype(BF16)
    r1 = x - hi.astype(F32)
    mid = r1.astype(BF16)
    lo = (r1 - mid.astype(F32)).astype(BF16)
    return hi.astype(F32), mid.astype(F32), lo.astype(F32)


def _piece_lane_mask(lane, h):
    return (lane == h) | (lane == h + PIECE_STRIDE) | (lane == h + 2 * PIECE_STRIDE)


def _inproj_kernel(x_ref, g1_ref, wrow_ref, bf_ref, wqvT_ref, ltri_ref, hsel_ref,
                   a_ref, kp_ref, qpT_ref, vT_ref, kstat_ref, qstat_ref, carry_ref, *, tm, tk):
    @pl.when(pl.program_id(1) == 0)
    def _():
        carry_ref[...] = jnp.zeros_like(carry_ref)

    x = x_ref[...]
    ms = jnp.mean(x * x, axis=-1, keepdims=True)
    hb = ((x * lax.rsqrt(ms + EPS)) * g1_ref[...]).astype(BF16)

    z = _dot(hb, wrow_ref[...])
    a_ref[...] = z[:, :D_CONV] * jax.nn.sigmoid(z[:, D_CONV:2 * D_CONV])

    kk = z[:, 2 * D_CONV:2 * D_CONV + D_ATT]

    zf = z[:, 2 * D_CONV + D_ATT:] + bf_ref[...]
    lf = jnp.minimum(zf, 0.0) - jnp.log1p(jnp.exp(-jnp.abs(zf)))
    lane = lax.broadcasted_iota(jnp.int32, (tm, LANES), 1)
    hi, mid, lo = _split3(lf)
    lf3 = jnp.where(lane < PIECE_STRIDE, hi,
                    jnp.where(lane < 2 * PIECE_STRIDE, mid,
                              jnp.where(lane < 3 * PIECE_STRIDE, lo, 0.0))).astype(BF16)
    cs3 = _dot(ltri_ref[...], lf3)
    c = (cs3 + pltpu.roll(cs3, LANES - PIECE_STRIDE, 1)
         + pltpu.roll(cs3, LANES - 2 * PIECE_STRIDE, 1)) + carry_ref[...]
    carry_ref[...] = c[tm - 1:tm, :]

    nhi, nmid, nlo = _split3(c * (-LOG2E))
    p3 = jnp.where(lane < PIECE_STRIDE, nhi,
                   jnp.where(lane < 2 * PIECE_STRIDE, pltpu.roll(nmid, PIECE_STRIDE, 1),
                             jnp.where(lane < 3 * PIECE_STRIDE, pltpu.roll(nlo, 2 * PIECE_STRIDE, 1), 0.0)))
    p3_hi = pltpu.roll(p3, HEAD_DIM, 1)
    q_pieces = (-p3).T[0:HEAD_DIM]
    q_shift = N_PIECES * PIECE_STRIDE

    qvT = _dot_nt(wqvT_ref[...], hb)
    qT = qvT[:D_ATT] * LOG2E
    vT = qvT[D_ATT:]
    row = lax.broadcasted_iota(jnp.int32, (HEAD_DIM, tm), 0)
    vrow = lax.broadcasted_iota(jnp.int32, (V_ROWS - HEAD_DIM, tk), 0)
    v_tail = jnp.where(vrow == 0, 1.0, 0.0).astype(BF16)

    for h in range(N_HEADS):
        kcol = kk[:, (h // 2) * LANES:(h // 2 + 1) * LANES]
        f_h = jnp.where(_piece_lane_mask(row, h), q_pieces, 0.0)
        aug_q = jnp.where(_piece_lane_mask(row, h), 1.0, 0.0) + jnp.concatenate(
            [jnp.zeros((q_shift, tm), F32), f_h[0:HEAD_DIM - q_shift]], axis=0)
        q_h = qT[h * HEAD_DIM:(h + 1) * HEAD_DIM, :]
        if h % 2 == 0:
            ext = jnp.where(_piece_lane_mask(lane, h + HEAD_DIM), p3_hi,
                            jnp.where(_piece_lane_mask(lane, h + HEAD_DIM + q_shift), 1.0, 0.0))
            kp = jnp.where(lane < HEAD_DIM, kcol, ext)
            qp = jnp.concatenate([q_h, aug_q], axis=0)
        else:
            ext = jnp.where(_piece_lane_mask(lane, h), p3,
                            jnp.where(_piece_lane_mask(lane, h + q_shift), 1.0, 0.0))
            kp = jnp.where(lane >= HEAD_DIM, kcol, ext)
            qp = jnp.concatenate([aug_q, q_h], axis=0)
        qpT_ref[0, h] = qp.astype(BF16)
        for cidx in range(tm // tk):
            kp_ref[0, h, cidx] = kp[cidx * tk:(cidx + 1) * tk, :].astype(BF16)
            v_h = vT[h * HEAD_DIM:(h + 1) * HEAD_DIM, cidx * tk:(cidx + 1) * tk].astype(BF16)
            vT_ref[0, h, cidx] = jnp.concatenate([v_h, v_tail], axis=0)

    kn2 = jnp.max(_dot((kk * kk).astype(BF16), hsel_ref[...]), axis=0, keepdims=True)
    srow = lax.broadcasted_iota(jnp.int32, (SUBLANES, LANES), 0)
    kstat_ref[0, 0] = jnp.where(srow == 0, c[0:1, :], jnp.where(srow == 1, c[tm - 1:tm, :],
                                                                 jnp.where(srow == 2, kn2, 0.0)))
    q2 = qT * qT
    qn2 = jnp.concatenate([jnp.sum(q2[h * HEAD_DIM:(h + 1) * HEAD_DIM, :], axis=0, keepdims=True)
                           for h in range(N_HEADS)], axis=0)
    qstat_ref[0, 0] = jnp.broadcast_to(jnp.max(qn2, axis=1, keepdims=True), (N_HEADS, LANES))


def _inproj(x2, g1, wrow, bf3, wqvT, *, batch, seq, tm, tk):
    nt = seq // tm
    nk = seq // tk
    ltri = jnp.tril(jnp.ones((tm, tm), F32)).astype(BF16)
    hsel = (jnp.arange(D_ATT)[:, None] // HEAD_DIM == jnp.arange(LANES)[None, :]).astype(BF16)
    const = lambda shape: pl.BlockSpec(shape, lambda b, t: (0,) * len(shape))
    stat_spec = pl.BlockSpec((1, 1, SUBLANES, LANES), lambda b, t: (b, t, 0, 0))
    stat_shape = jax.ShapeDtypeStruct((batch, nt, SUBLANES, LANES), F32)
    return pl.pallas_call(
        functools.partial(_inproj_kernel, tm=tm, tk=tk),
        grid=(batch, nt),
        in_specs=[
            pl.BlockSpec((tm, D_MODEL), lambda b, t: (b * nt + t, 0)),
            const((1, D_MODEL)), const((D_MODEL, 2 * D_CONV + D_ATT + LANES)), const((1, LANES)),
            const((2 * D_ATT, D_MODEL)), const((tm, tm)), const((D_ATT, LANES)),
        ],
        out_specs=[
            pl.BlockSpec((tm, D_CONV), lambda b, t: (b * nt + t, 0)),
            pl.BlockSpec((1, N_HEADS, tm // tk, tk, KAUG), lambda b, t: (b, 0, t, 0, 0)),
            pl.BlockSpec((1, N_HEADS, KAUG, tm), lambda b, t: (b, 0, 0, t)),
            pl.BlockSpec((1, N_HEADS, tm // tk, V_ROWS, tk), lambda b, t: (b, 0, t, 0, 0)),
            stat_spec, stat_spec,
        ],
        out_shape=[
            jax.ShapeDtypeStruct((batch * seq, D_CONV), F32),
            jax.ShapeDtypeStruct((batch, N_HEADS, nk, tk, KAUG), BF16),
            jax.ShapeDtypeStruct((batch, N_HEADS, KAUG, seq), BF16),
            jax.ShapeDtypeStruct((batch, N_HEADS, nk, V_ROWS, tk), BF16),
            stat_shape, stat_shape,
        ],
        scratch_shapes=[pltpu.VMEM((1, LANES), F32)],
        compiler_params=pltpu.CompilerParams(
            dimension_semantics=("arbitrary", "arbitrary"), vmem_limit_bytes=VMEM_LIMIT),
        name="inproj",
    )(x2, g1, wrow, bf3, wqvT, ltri, hsel)


def _conv_tile(a_ref, w_ref, b_ref, lng_ref, lnb_ref, og_ref, sh_ref, acc_ref, *, tc, rc):
    n_cb = D_CONV // LANES

    @pl.when(pl.program_id(1) == 0)
    def _():
        for cb in range(n_cb):
            sh_ref[0, cb, 0:CONV_HALO, :] = jnp.zeros((CONV_HALO, LANES), F32)

    @pl.when(pl.program_id(1) > 0)
    def _():
        for cb in range(n_cb):
            sh_ref[0, cb, 0:CONV_HALO, :] = sh_ref[0, cb, tc:tc + CONV_HALO, :]

    n_sh = tc + CONV_HALO - SUBLANES
    for cb in range(n_cb):
        sh_ref[0, cb, CONV_HALO:CONV_HALO + tc, :] = a_ref[:, cb * LANES:(cb + 1) * LANES]
        for f in range(1, SUBLANES):
            sh_ref[f, cb, 0:n_sh, :] = sh_ref[0, cb, f:f + n_sh, :]

    base = CONV_HALO - (CONV_WIDTH - 1)
    for cb in range(n_cb):
        cols = slice(cb * LANES, (cb + 1) * LANES)

        def chunk(c, carry, cb=cb, cols=cols):
            r0 = pl.multiple_of(c * rc, rc)
            acc = jnp.zeros((rc, LANES), F32)
            for j in range(CONV_WIDTH):
                f = (base + j) % SUBLANES
                acc = acc + w_ref[j:j + 1, cols] * sh_ref[f, cb, pl.ds(r0 + (base + j - f), rc), :]
            acc_ref[pl.ds(r0, rc), cols] = acc
            return carry

        lax.fori_loop(0, tc // rc, chunk, 0)

    y = acc_ref[...] + b_ref[...]
    mu = jnp.mean(y, axis=-1, keepdims=True)
    yc = y - mu
    var = jnp.mean(yc * yc, axis=-1, keepdims=True)
    yn = yc * lax.rsqrt(var + EPS) * lng_ref[...] + lnb_ref[...]
    s = yn * jax.nn.sigmoid(yn)
    ms = jnp.mean(s * s, axis=-1, keepdims=True)
    return (s * lax.rsqrt(ms + EPS) * og_ref[...]).astype(BF16)


def _attn_fixed_ref_kernel(g0_ref, bnd_ref, qT_ref, k_ref, v_ref, o_ref, *, tq, tk, unit):
    i = pl.program_id(2)
    qT = qT_ref[0, 0]
    n_sub = tq // tk
    bnd = bnd_ref[0, 0, 0:1, 0:1]
    sb0 = unit * g0_ref[(pl.program_id(0) * N_HEADS + pl.program_id(1)) * pl.num_programs(2) + i]

    def values(sb, n):
        return jnp.concatenate([v_ref[0, 0, sb + d] for d in range(n)], axis=1)

    def probs(sb, n):
        return jnp.concatenate([jnp.exp2(_dot(k_ref[0, 0, sb + d], qT) - bnd).astype(BF16)
                                for d in range(n)], axis=0)

    def accumulate(acc, sb, n):
        for lo in range(0, n, n_sub):
            m = min(n_sub, n - lo)
            acc = acc + _dot(values(sb + lo, m), probs(sb + lo, m))
        return acc

    def loop_trip(p, acc):
        return accumulate(acc, sb0 + GROUPS_PER_TRIP * n_sub * p, GROUPS_PER_TRIP * n_sub)

    key = lax.broadcasted_iota(jnp.int32, (tk, tk), 0)
    qry = lax.broadcasted_iota(jnp.int32, (tk, tk), 1)

    def diagonal(acc):
        rows = []
        for d in range(n_sub):
            s = _dot(k_ref[0, 0, i * n_sub + d], qT[:, d * tk:])
            tiles = [jnp.zeros((tk, d * tk), BF16)] if d else []
            tiles.append(jnp.exp2(jnp.where(key <= qry, s[:, :tk], -jnp.inf) - bnd).astype(BF16))
            if d + 1 < n_sub:
                tiles.append(jnp.exp2(s[:, tk:] - bnd).astype(BF16))
            rows.append(jnp.concatenate(tiles, axis=1))
        acc = acc + _dot(values(i * n_sub, n_sub), jnp.concatenate(rows, axis=0))
        o_ref[0] = acc[:HEAD_DIM] / acc[HEAD_DIM:HEAD_DIM + 1]

    n_before = i * n_sub - sb0
    per_trip = GROUPS_PER_TRIP * n_sub
    acc = lax.fori_loop(0, n_before // per_trip, loop_trip, jnp.zeros((V_ROWS, tq), F32))
    left = n_before % per_trip

    for n in range(0, per_trip, unit):
        @pl.when(left == n)
        def _(n=n):
            diagonal(accumulate(acc, i * n_sub - n, n))


def _attn_kernel(g0_ref, qT_ref, k_ref, v_ref, o_ref, sa_ref, sb_ref, cma_ref, cmb_ref, *, tq, tk):
    i = pl.program_id(2)
    qT = qT_ref[0, 0]
    n_sub = tq // tk

    def scores(g, s_ref, cm_ref):
        for d in range(n_sub):
            s = _dot(k_ref[0, 0, g * n_sub + d], qT)
            s_ref[d] = s
            cm_ref[d] = jnp.max(s, axis=0, keepdims=True)

    def consume(g, s_ref, cm_ref, carry, nxt=None):
        m, acc = carry
        m_new = m
        for d in range(n_sub):
            m_new = jnp.maximum(m_new, cm_ref[d])
        ps = []
        for d in range(n_sub):
            if nxt is not None:
                g_n, sn_ref, cmn_ref = nxt
                s = _dot(k_ref[0, 0, g_n * n_sub + d], qT)
                sn_ref[d] = s
                cmn_ref[d] = jnp.max(s, axis=0, keepdims=True)
            ps.append(jnp.exp2(s_ref[d] - m_new).astype(BF16))
        p = jnp.concatenate(ps, axis=0)
        v = jnp.concatenate([v_ref[0, 0, g * n_sub + d] for d in range(n_sub)], axis=1)
        return m_new, jnp.exp2(m - m_new) * acc + _dot(v, p)

    key = lax.broadcasted_iota(jnp.int32, (tk, tk), 0)
    qry = lax.broadcasted_iota(jnp.int32, (tk, tk), 1)

    def consume_diag(s_ref, cm_ref, carry):
        m, acc = carry
        lanes = lambda j: slice(j * tk, (j + 1) * tk)
        causal = [jnp.where(key <= qry, s_ref[d, :, lanes(d)], -jnp.inf) for d in range(n_sub)]
        m_tiles = []
        for j in range(n_sub):
            m_j = jnp.maximum(m[:, lanes(j)], jnp.max(causal[j], axis=0, keepdims=True))
            for d in range(j):
                m_j = jnp.maximum(m_j, cm_ref[d, :, lanes(j)])
            m_tiles.append(m_j)
        m_new = jnp.concatenate(m_tiles, axis=1)
        rows = []
        for d in range(n_sub):
            tiles = [jnp.zeros((tk, d * tk), BF16)] if d else []
            tiles.append(jnp.exp2(causal[d] - m_tiles[d]).astype(BF16))
            if d + 1 < n_sub:
                rest = slice((d + 1) * tk, tq)
                tiles.append(jnp.exp2(s_ref[d, :, rest] - m_new[:, rest]).astype(BF16))
            rows.append(jnp.concatenate(tiles, axis=1))
        p = jnp.concatenate(rows, axis=0)
        v = jnp.concatenate([v_ref[0, 0, i * n_sub + d] for d in range(n_sub)], axis=1)
        acc = jnp.exp2(m - m_new) * acc + _dot(v, p)
        o_ref[0] = acc[:HEAD_DIM] / acc[HEAD_DIM:HEAD_DIM + 1]

    g0 = g0_ref[(pl.program_id(0) * N_HEADS + pl.program_id(1)) * pl.num_programs(2) + i]
    n_full = i - g0

    def pair(p, carry):
        g = g0 + 2 * p
        carry = consume(g, sa_ref, cma_ref, carry, nxt=(g + 1, sb_ref, cmb_ref))
        return consume(g + 1, sb_ref, cmb_ref, carry, nxt=(g + 2, sa_ref, cma_ref))

    scores(g0, sa_ref, cma_ref)
    init = (jnp.full((1, tq), -jnp.inf, F32), jnp.zeros((V_ROWS, tq), F32))
    carry = lax.fori_loop(0, n_full // 2, pair, init)

    @pl.when(n_full % 2 == 1)
    def _():
        consume_diag(sb_ref, cmb_ref, consume(i - 1, sa_ref, cma_ref, carry, nxt=(i, sb_ref, cmb_ref)))

    @pl.when(n_full % 2 == 0)
    def _():
        consume_diag(sa_ref, cma_ref, carry)


def _first_groups(kstat, qstat, *, tm, tq):
    per = tq // tm
    bound = jnp.sqrt(jnp.max(kstat[:, :, 2, :N_HEADS], axis=1) * jnp.max(qstat[:, :, :, 0], axis=1))
    bound = bound * NORM_SLACK
    cq = kstat[:, ::per, 0, :N_HEADS]
    ck = kstat[:, :, 1, :N_HEADS]
    d = (cq[:, :, None, :] - ck[:, None, :, :]) * LOG2E
    nq, ng = cq.shape[1], ck.shape[1]
    earlier = per * jnp.arange(nq)[None, :, None, None] > jnp.arange(ng)[None, None, :, None]
    skip = earlier & (2.0 * bound[:, None, None, :] + d < -SKIP_LOG2)
    g0 = jnp.sum(jnp.cumprod(skip.astype(jnp.int32), axis=2), axis=2)
    return jnp.transpose(g0, (0, 2, 1)).reshape(-1).astype(jnp.int32), bound


def _attention(g0, bound, qpT, kp, vT, *, batch, seq, tm, tq, tk):
    nk = seq // tk
    params = pltpu.CompilerParams(
        dimension_semantics=("arbitrary", "arbitrary", "arbitrary"), vmem_limit_bytes=VMEM_LIMIT)
    out_shape = jax.ShapeDtypeStruct((batch, D_ATT, seq), F32)
    qkv_specs = [
        pl.BlockSpec((1, 1, KAUG, tq), lambda b, h, i, g0: (b, h, 0, i)),
        pl.BlockSpec((1, 1, nk, tk, KAUG), lambda b, h, i, g0: (b, h, 0, 0, 0)),
        pl.BlockSpec((1, 1, nk, V_ROWS, tk), lambda b, h, i, g0: (b, h, 0, 0, 0)),
    ]
    out_spec = pl.BlockSpec((1, HEAD_DIM, tq), lambda b, h, i, g0: (b, h, i))

    def fixed_ref(g0, bnd, qpT, kp, vT):
        grid_spec = pltpu.PrefetchScalarGridSpec(
            num_scalar_prefetch=1, grid=(batch, N_HEADS, seq // tq),
            in_specs=[pl.BlockSpec((1, 1, SUBLANES, LANES), lambda b, h, i, g0: (b, h, 0, 0))] + qkv_specs,
            out_specs=out_spec)
        return pl.pallas_call(functools.partial(_attn_fixed_ref_kernel, tq=tq, tk=tk, unit=tm // tk),
                              grid_spec=grid_spec, out_shape=out_shape, compiler_params=params,
                              name="attention_fixed_ref")(g0, bnd, qpT, kp, vT)

    def running_max(g0, bnd, qpT, kp, vT):
        del bnd
        grid_spec = pltpu.PrefetchScalarGridSpec(
            num_scalar_prefetch=1, grid=(batch, N_HEADS, seq // tq), in_specs=qkv_specs, out_specs=out_spec,
            scratch_shapes=[pltpu.VMEM((tq // tk, tk, tq), F32), pltpu.VMEM((tq // tk, tk, tq), F32),
                            pltpu.VMEM((tq // tk, 1, tq), F32), pltpu.VMEM((tq // tk, 1, tq), F32)])
        return pl.pallas_call(functools.partial(_attn_kernel, tq=tq, tk=tk), grid_spec=grid_spec,
                              out_shape=out_shape, compiler_params=params, name="attention",
                              )(g0 // (tq // tm), qpT, kp, vT)

    bnd = jnp.broadcast_to(bound[:, :, None, None], (batch, N_HEADS, SUBLANES, LANES))
    return lax.cond(jnp.max(bound) <= FIXED_REF_MAX_BOUND, fixed_ref, running_max, g0, bnd, qpT, kp, vT)


def _outproj_kernel(x_ref, a_ref, yT_ref, wdw_ref, bdw_ref, lng_ref, lnb_ref, gc_ref, wo_ref, ga_ref,
                    g2_ref, wrT_ref, brT_ref, utri_ref, x1_ref, h2_ref, route_ref, cnt_ref,
                    carry_ref, sh_ref, acc_ref, *, tm, rc):
    first = (pl.program_id(0) == 0) & (pl.program_id(1) == 0)

    @pl.when(first)
    def _():
        carry_ref[...] = jnp.zeros_like(carry_ref)

    mc = _conv_tile(a_ref, wdw_ref, bdw_ref, lng_ref, lnb_ref, gc_ref, sh_ref, acc_ref, tc=tm, rc=rc)
    yT = yT_ref[0]
    msa = jnp.mean(yT * yT, axis=0, keepdims=True)
    yn = (yT * lax.rsqrt(msa + EPS)).T * ga_ref[...]
    x1 = x_ref[...] + _dot(jnp.concatenate([mc, yn.astype(BF16)], axis=1), wo_ref[...])
    x1_ref[...] = x1
    ms = jnp.mean(x1 * x1, axis=-1, keepdims=True)
    h2 = (x1 * lax.rsqrt(ms + EPS)) * g2_ref[...]
    h2_ref[:, :D_MODEL] = h2

    lg = _dot_nt(wrT_ref[...], h2.astype(BF16)) + brT_ref[...]
    neg = -jnp.inf
    sub = lax.broadcasted_iota(jnp.int32, (SUBLANES, tm), 0)
    big = jnp.int32(SUBLANES)
    first_of = lambda hit: jnp.min(jnp.where(hit, sub, big), axis=0, keepdims=True)

    lg1 = lg[0:SUBLANES]
    m1 = jnp.max(lg1, axis=0, keepdims=True)
    p1_sel = 1.0 / jnp.sum(jnp.exp(lg1 - m1), axis=0, keepdims=True)
    grp = first_of(lg1 == m1)

    slab = lambda g: lg[SUBLANES * (g + 1):SUBLANES * (g + 2)]
    v = slab(N_GROUPS - 1)
    for g in range(N_GROUPS - 2, -1, -1):
        v = jnp.where(grp == g, slab(g), v)
    v1 = jnp.max(v, axis=0, keepdims=True)
    j1 = first_of(v == v1)
    vv = jnp.where(sub == j1, neg, v)
    v2 = jnp.max(vv, axis=0, keepdims=True)
    j2 = first_of(vv == v2)
    e21 = jnp.exp(v2 - v1)
    w0 = p1_sel / (1.0 + e21)
    w1 = p1_sel * e21 / (1.0 + e21)

    swap = j2 < j1
    al = jnp.where(swap, j2, j1)
    bl = jnp.where(swap, j1, j2)
    wa = jnp.where(swap, w1, w0)
    wb = jnp.where(swap, w0, w1)
    cid = PAIRS_PER_GROUP * grp + ((al * (2 * EXPERTS_PER_GROUP - 1 - al)) >> 1) + (bl - al - 1)

    cls = lax.broadcasted_iota(jnp.int32, (LANES, tm), 0)
    oh = cls == cid
    cmat = jnp.where(oh, 1.0, 0.0)
    prefix = _dot(cmat.astype(BF16), utri_ref[...]) + carry_ref[...]
    rank = jnp.sum(jnp.where(oh, prefix, 0.0), axis=0, keepdims=True)
    counts = prefix[:, tm - 1:tm] + cmat[:, tm - 1:tm]
    carry_ref[...] = counts
    cnt_ref[...] = counts

    h2_ref[:, D_MODEL:] = jnp.where(cls == 0, wa, jnp.where(cls == 1, wb, 0.0)).T
    route_ref[...] = jnp.where(sub == 0, cid.astype(F32), jnp.where(sub == 1, rank, 0.0))


def _outproj(x2, a, yT, conv_params, wo, ga, g2, wrT, brT, *, batch, seq, tm, rc):
    nt = seq // tm
    utri = jnp.triu(jnp.ones((tm, tm), F32), 1).astype(BF16)
    const = lambda shape: pl.BlockSpec(shape, lambda b, t: (0,) * len(shape))
    row_spec = lambda w: pl.BlockSpec((tm, w), lambda b, t: (b * nt + t, 0))
    return pl.pallas_call(
        functools.partial(_outproj_kernel, tm=tm, rc=rc),
        grid=(batch, nt),
        in_specs=[row_spec(D_MODEL), row_spec(D_CONV),
                  pl.BlockSpec((1, D_ATT, tm), lambda b, t: (b, 0, t)),
                  const((CONV_WIDTH, D_CONV)), const((1, D_CONV)), const((1, D_CONV)), const((1, D_CONV)),
                  const((1, D_CONV)),
                  const((D_CONV + D_ATT, D_MODEL)), const((1, D_ATT)), const((1, D_MODEL)),
                  const((LANES, D_MODEL)), const((LANES, 1)), const((tm, tm))],
        out_specs=[row_spec(D_MODEL), row_spec(ROW_W),
                   pl.BlockSpec((SUBLANES, tm), lambda b, t: (0, b * nt + t)), const((LANES, 1))],
        out_shape=[jax.ShapeDtypeStruct((batch * seq, D_MODEL), F32),
                   jax.ShapeDtypeStruct((batch * seq, ROW_W), F32),
                   jax.ShapeDtypeStruct((SUBLANES, batch * seq), F32),
                   jax.ShapeDtypeStruct((LANES, 1), F32)],
        scratch_shapes=[pltpu.VMEM((LANES, 1), F32),
                        pltpu.VMEM((SUBLANES, D_CONV // LANES, tm + CONV_HALO, LANES), F32),
                        pltpu.VMEM((tm, D_CONV), F32)],
        compiler_params=pltpu.CompilerParams(
            dimension_semantics=("arbitrary", "arbitrary"), vmem_limit_bytes=VMEM_LIMIT),
        name="outproj",
    )(x2, a, yT, *conv_params, wo, ga, g2, wrT, brT, utri)


def _row_copies(src_at, dst_at, sem):
    return pltpu.make_async_copy(src_at, dst_at, sem)


def _dispatch_kernel(dest_ref, h2_ref, xs_ref, sem, *, tm):
    def body(g, carry):
        for u in range(SUBLANES):
            _row_copies(h2_ref.at[g, pl.ds(u, 1)], xs_ref.at[pl.ds(dest_ref[g * SUBLANES + u], 1)],
                        sem).start(priority=u % 2)
        return carry

    lax.fori_loop(0, tm // SUBLANES, body, 0)
    done = xs_ref.at[pl.ds(0, tm)]
    _row_copies(done, done, sem).wait()


def _dispatch(dest, h2, *, tm):
    t_len = h2.shape[0]
    return pl.pallas_call(
        functools.partial(_dispatch_kernel, tm=tm),
        grid=(t_len // tm,),
        in_specs=[pl.BlockSpec((tm,), lambda i: (i,), memory_space=pltpu.SMEM),
                  pl.BlockSpec((tm // SUBLANES, SUBLANES, ROW_W), lambda i: (i, 0, 0))],
        out_specs=pl.BlockSpec(memory_space=pl.ANY),
        out_shape=jax.ShapeDtypeStruct((t_len, ROW_W), F32),
        scratch_shapes=[pltpu.SemaphoreType.DMA(())],
        compiler_params=pltpu.CompilerParams(dimension_semantics=("arbitrary",), vmem_limit_bytes=VMEM_LIMIT),
        name="dispatch",
    )(dest, h2.reshape(t_len // SUBLANES, SUBLANES, ROW_W))


def _expert_kernel(blk_ref, ea_ref, eb_ref, lo_ref, hi_ref, x_ref, wgua_ref, wda_ref, wgub_ref, wdb_ref, y_ref):
    del blk_ref, ea_ref, eb_ref
    i = pl.program_id(0)
    lo = lo_ref[i]
    hi = hi_ref[i]
    half = ROW_BLOCK // 2

    def mlp(xb, wgu_ref, wd_ref):
        gu = _dot(xb, wgu_ref[0])
        g = gu[:, :D_EXPERT]
        return _dot(((g * jax.nn.sigmoid(g)) * gu[:, D_EXPERT:]).astype(BF16), wd_ref[0])

    def run(rows):
        n = rows.stop - rows.start
        xb = x_ref[rows, :D_MODEL].astype(BF16)
        wts = x_ref[rows, D_MODEL:]
        y = wts[:, 0:1] * mlp(xb, wgua_ref, wda_ref) + wts[:, 1:2] * mlp(xb, wgub_ref, wdb_ref)
        row = rows.start + lax.broadcasted_iota(jnp.int32, (n, 1), 0)
        mine = (row >= lo) & (row < hi)

        @pl.when(lo == 0)
        def _():
            y_ref[rows, :] = jnp.where(mine, y, 0.0)
            if n < ROW_BLOCK:
                y_ref[half:, :] = jnp.zeros((half, D_MODEL), F32)

        @pl.when(lo > 0)
        def _():
            y_ref[rows, :] = jnp.where(mine, y, y_ref[rows, :])

    @pl.when((hi > lo) & (lo < half) & (hi > half))
    def _():
        run(slice(0, ROW_BLOCK))

    @pl.when((hi > lo) & (hi <= half))
    def _():
        run(slice(0, half))

    @pl.when((hi > lo) & (lo >= half))
    def _():
        run(slice(half, ROW_BLOCK))


def _experts(items, xs, wgu, wd):
    n_rows = xs.shape[0]
    n_items = items[0].shape[0]
    wspec = lambda shape, which: pl.BlockSpec(
        shape, lambda i, blk, ea, eb, lo, hi: ((ea, eb)[which][i], 0, 0))
    grid_spec = pltpu.PrefetchScalarGridSpec(
        num_scalar_prefetch=5,
        grid=(n_items,),
        in_specs=[pl.BlockSpec((ROW_BLOCK, ROW_W), lambda i, blk, ea, eb, lo, hi: (blk[i], 0)),
                  wspec((1, D_MODEL, 2 * D_EXPERT), 0), wspec((1, D_EXPERT, D_MODEL), 0),
                  wspec((1, D_MODEL, 2 * D_EXPERT), 1), wspec((1, D_EXPERT, D_MODEL), 1)],
        out_specs=pl.BlockSpec((ROW_BLOCK, D_MODEL), lambda i, blk, ea, eb, lo, hi: (blk[i], 0)),
    )
    return pl.pallas_call(
        _expert_kernel,
        grid_spec=grid_spec,
        out_shape=jax.ShapeDtypeStruct((n_rows, D_MODEL), F32),
        compiler_params=pltpu.CompilerParams(dimension_semantics=("arbitrary",), vmem_limit_bytes=VMEM_LIMIT),
        name="experts",
    )(*items, xs, wgu, wd, wgu, wd)


def _combine_kernel(dcur_ref, dnxt_ref, x1_ref, g_ref, ys_ref, out_ref, buf_ref, sems, *, tm):
    i = pl.program_id(0)
    slot = i % 2

    def issue(d_ref, s):
        def body(g, carry):
            for u in range(SUBLANES):
                _row_copies(ys_ref.at[pl.ds(d_ref[g * SUBLANES + u], 1)], buf_ref.at[s, g, pl.ds(u, 1)],
                            sems.at[s]).start(priority=u % 2)
            return carry

        lax.fori_loop(0, tm // SUBLANES, body, 0)

    @pl.when(i == 0)
    def _():
        issue(dcur_ref, 0)

    @pl.when(i + 1 < pl.num_programs(0))
    def _():
        issue(dnxt_ref, 1 - slot)

    _row_copies(buf_ref.at[slot], buf_ref.at[slot], sems.at[slot]).wait()
    x2 = x1_ref[...] + buf_ref[slot]
    ms = jnp.mean(x2 * x2, axis=-1, keepdims=True)
    out_ref[...] = (x2 * lax.rsqrt(ms + EPS)) * g_ref[...]


def _combine(dest, x1, gf, ys, *, tm):
    t_len = x1.shape[0]
    n = t_len // tm
    return pl.pallas_call(
        functools.partial(_combine_kernel, tm=tm),
        grid=(n,),
        in_specs=[pl.BlockSpec((tm,), lambda i: (i,), memory_space=pltpu.SMEM),
                  pl.BlockSpec((tm,), lambda i: (jnp.minimum(i + 1, n - 1),), memory_space=pltpu.SMEM),
                  pl.BlockSpec((tm // SUBLANES, SUBLANES, D_MODEL), lambda i: (i, 0, 0)),
                  pl.BlockSpec((1, D_MODEL), lambda i: (0, 0)),
                  pl.BlockSpec(memory_space=pl.ANY)],
        out_specs=pl.BlockSpec((tm // SUBLANES, SUBLANES, D_MODEL), lambda i: (i, 0, 0)),
        out_shape=jax.ShapeDtypeStruct((t_len // SUBLANES, SUBLANES, D_MODEL), F32),
        scratch_shapes=[pltpu.VMEM((2, tm // SUBLANES, SUBLANES, D_MODEL), F32), pltpu.SemaphoreType.DMA((2,))],
        compiler_params=pltpu.CompilerParams(dimension_semantics=("arbitrary",), vmem_limit_bytes=VMEM_LIMIT),
        name="combine",
    )(dest, dest, x1.reshape(t_len // SUBLANES, SUBLANES, D_MODEL), gf, ys)


def _layer(x, norm1_g, w_in, b_f, w_dw, b_dw, conv_ln_g, conv_ln_b, out_g_conv, out_g_att, w_out,
           norm2_g, w_r1, b_r1, w_r2, b_r2, w_gate, w_up, w_down):
    batch, seq, d = x.shape
    t_len = batch * seq
    tm = min(512, seq)
    tk = min(256, seq)
    tq = min(1024, seq)
    o2 = 2 * D_CONV
    o3, o4, o5 = o2 + D_ATT, o2 + 2 * D_ATT, o2 + 3 * D_ATT

    assert PIECE_STRIDE == N_HEADS
    pad = LANES - N_PIECES * PIECE_STRIDE
    wf = jnp.pad(jnp.tile(w_in[:, o5:], (1, N_PIECES)), ((0, 0), (0, pad)))
    wrow = jnp.concatenate([w_in[:, :o2], w_in[:, o3:o4], wf], axis=1).astype(BF16)
    wqvT = jnp.concatenate([w_in[:, o2:o3] * (HEAD_DIM ** -0.5), w_in[:, o4:o5]], axis=1).T.astype(BF16)
    bf3 = jnp.pad(jnp.tile(b_f.astype(F32).reshape(1, N_HEADS), (1, N_PIECES)), ((0, 0), (0, pad)))

    x2 = x.reshape(t_len, d)
    a, kp, qpT, vT, kstat, qstat = _inproj(x2, norm1_g.reshape(1, d), wrow, bf3, wqvT,
                                           batch=batch, seq=seq, tm=tm, tk=tk)
    conv_params = (w_dw, b_dw.reshape(1, -1), conv_ln_g.reshape(1, -1), conv_ln_b.reshape(1, -1),
                   out_g_conv.reshape(1, -1))
    g0, bound = _first_groups(kstat, qstat, tm=tm, tq=tq)
    yT = _attention(g0, bound, qpT, kp, vT, batch=batch, seq=seq, tm=tm, tq=tq, tk=tk)

    gpad = SUBLANES - N_GROUPS
    rpad = LANES - SUBLANES - N_EXPERTS
    wrT = jnp.concatenate([w_r1.T, jnp.zeros((gpad, d), F32),
                           jnp.transpose(w_r2, (0, 2, 1)).reshape(N_EXPERTS, d), jnp.zeros((rpad, d), F32)], axis=0)
    brT = jnp.concatenate([b_r1.astype(F32), jnp.full((gpad,), -jnp.inf, F32), b_r2.reshape(-1).astype(F32),
                           jnp.zeros((rpad,), F32)]).reshape(LANES, 1)
    x1, h2, route, cnt = _outproj(x2, a, yT, conv_params, w_out.astype(BF16),
                                  out_g_att.reshape(1, -1), norm2_g.reshape(1, d),
                                  wrT.astype(BF16), brT, batch=batch, seq=seq, tm=tm, rc=min(256, seq))

    i32 = jnp.int32
    lanes = jnp.arange(LANES, dtype=i32)
    pick = lambda table, idx: jnp.sum(jnp.where(idx[:, None] == lanes, table[None, :], 0), axis=1).astype(i32)
    counts = cnt[:, 0].astype(i32)
    ends = jnp.cumsum(counts).astype(i32)
    starts = ends - counts
    dest = pick(starts, route[0].astype(i32)) + route[1].astype(i32)

    n_blocks = t_len // ROW_BLOCK
    b_lo = starts // ROW_BLOCK
    n_it = jnp.where(counts > 0, (ends - 1) // ROW_BLOCK - b_lo + 1, 0)
    it_end = jnp.cumsum(n_it).astype(i32)
    it_start = it_end - n_it
    idx = jnp.arange(n_blocks + N_CLASSES, dtype=i32)
    valid = idx < it_end[-1]
    last_cls = jnp.max(jnp.where(counts > 0, lanes, 0))
    cls = jnp.where(valid, jnp.sum(it_end[None, :] <= idx[:, None], axis=1).astype(i32), last_cls)
    blk = jnp.where(valid, pick(b_lo, cls) + idx - pick(it_start, cls), n_blocks - 1)
    row0 = blk * ROW_BLOCK
    lo = jnp.where(valid, jnp.maximum(pick(starts, cls), row0) - row0, 0)
    hi = jnp.where(valid, jnp.minimum(pick(ends, cls), row0 + ROW_BLOCK) - row0, 0)
    pair_a, pair_b = [], []
    for g in range(N_GROUPS):
        for a_loc in range(EXPERTS_PER_GROUP):
            for b_loc in range(a_loc + 1, EXPERTS_PER_GROUP):
                pair_a.append(g * EXPERTS_PER_GROUP + a_loc)
                pair_b.append(g * EXPERTS_PER_GROUP + b_loc)
    cpad = [0] * (LANES - N_CLASSES)
    items = (blk.astype(i32), pick(jnp.array(pair_a + cpad, i32), cls), pick(jnp.array(pair_b + cpad, i32), cls),
             lo.astype(i32), hi.astype(i32))

    xs = _dispatch(dest, h2, tm=min(4096, t_len))
    ys = _experts(items, xs, jnp.concatenate([w_gate, w_up], axis=-1).astype(BF16), w_down.astype(BF16))
    return dest, x1, ys


def kernel(x, norm1_g, w_in, b_f, w_dw, b_dw, conv_ln_g, conv_ln_b, out_g_conv, out_g_att, w_out, norm2_g,
           w_r1, b_r1, w_r2, b_r2, w_gate, w_up, w_down, final_g):
    assert norm1_g.shape[0] == 1, "single-layer stack"
    batch, seq, d = x.shape
    dest, x1, ys = _layer(
        x, norm1_g[0], w_in[0], b_f[0], w_dw[0], b_dw[0], conv_ln_g[0], conv_ln_b[0], out_g_conv[0],
        out_g_att[0], w_out[0], norm2_g[0], w_r1[0], b_r1[0], w_r2[0], b_r2[0], w_gate[0], w_up[0], w_down[0])
    out = _combine(dest, x1, final_g.reshape(1, d), ys, tm=min(512, batch * seq))
    return out.reshape(batch, seq, d)
```

```python
import functools

import jax
import jax.numpy as jnp
from jax import lax
from jax.experimental import pallas as pl
from jax.experimental.pallas import tpu as pltpu

D_MODEL = 1024
D_CONV = 512
N_HEADS = 8
HEAD_DIM = 64
D_ATT = N_HEADS * HEAD_DIM
CONV_WIDTH = 31
N_GROUPS = 4
EXPERTS_PER_GROUP = 8
N_EXPERTS = N_GROUPS * EXPERTS_PER_GROUP
D_EXPERT = D_MODEL // 4
ROW_BLOCK = 512
ROW_SEGMENT = 128
EPS = 1e-6

LANES = 128
SUBLANES = 8
KAUG = 128
N_PIECES = 3
PIECE_STRIDE = 8
CONV_HALO = 32
PAIRS_PER_GROUP = EXPERTS_PER_GROUP * (EXPERTS_PER_GROUP - 1) // 2
N_CLASSES = N_GROUPS * PAIRS_PER_GROUP
ROW_W = D_MODEL + LANES
SKIP_LOG2 = 160.0
FIXED_REF_MAX_BOUND = 40.0
GROUPS_PER_TRIP = 4
NORM_SLACK = 1.02
V_ROWS = 80
LOG2E = 1.4426950408889634
VMEM_LIMIT = 56 * 1024 * 1024

F32 = jnp.float32
BF16 = jnp.bfloat16


def _dot(a, b):
    return jnp.dot(a, b, preferred_element_type=F32)


def _dot_nt(a, b):
    return lax.dot_general(a, b, (((1,), (1,)), ((), ())), preferred_element_type=F32)


def _split3(x):
    hi = x.astype(BF16)
    r1 = x - hi.astype(F32)
    mid = r1.astype(BF16)
    lo = (r1 - mid.astype(F32)).astype(BF16)
    return hi.astype(F32), mid.astype(F32), lo.astype(F32)


def _piece_lane_mask(lane, h):
    return (lane == h) | (lane == h + PIECE_STRIDE) | (lane == h + 2 * PIECE_STRIDE)


def _inproj_kernel(x_ref, g1_ref, wrow_ref, bf_ref, wqvT_ref, ltri_ref, hsel_ref,
                   a_ref, kp_ref, qpT_ref, vT_ref, kstat_ref, qstat_ref, carry_ref, *, tm, tk):
    @pl.when(pl.program_id(1) == 0)
    def _():
        carry_ref[...] = jnp.zeros_like(carry_ref)

    x = x_ref[...]
    ms = jnp.mean(x * x, axis=-1, keepdims=True)
    hb = ((x * lax.rsqrt(ms + EPS)) * g1_ref[...]).astype(BF16)

    z = _dot(hb, wrow_ref[...])
    a_ref[...] = z[:, :D_CONV] * jax.nn.sigmoid(z[:, D_CONV:2 * D_CONV])

    kk = z[:, 2 * D_CONV:2 * D_CONV + D_ATT]

    zf = z[:, 2 * D_CONV + D_ATT:] + bf_ref[...]
    lf = jnp.minimum(zf, 0.0) - jnp.log1p(jnp.exp(-jnp.abs(zf)))
    lane = lax.broadcasted_iota(jnp.int32, (tm, LANES), 1)
    hi, mid, lo = _split3(lf)
    lf3 = jnp.where(lane < PIECE_STRIDE, hi,
                    jnp.where(lane < 2 * PIECE_STRIDE, mid,
                              jnp.where(lane < 3 * PIECE_STRIDE, lo, 0.0))).astype(BF16)
    cs3 = _dot(ltri_ref[...], lf3)
    c = (cs3 + pltpu.roll(cs3, LANES - PIECE_STRIDE, 1)
         + pltpu.roll(cs3, LANES - 2 * PIECE_STRIDE, 1)) + carry_ref[...]
    carry_ref[...] = c[tm - 1:tm, :]

    nhi, nmid, nlo = _split3(c * (-LOG2E))
    p3 = jnp.where(lane < PIECE_STRIDE, nhi,
                   jnp.where(lane < 2 * PIECE_STRIDE, pltpu.roll(nmid, PIECE_STRIDE, 1),
                             jnp.where(lane < 3 * PIECE_STRIDE, pltpu.roll(nlo, 2 * PIECE_STRIDE, 1), 0.0)))
    p3_hi = pltpu.roll(p3, HEAD_DIM, 1)
    q_pieces = (-p3).T[0:HEAD_DIM]
    q_shift = N_PIECES * PIECE_STRIDE

    qvT = _dot_nt(wqvT_ref[...], hb)
    qT = qvT[:D_ATT] * LOG2E
    vT = qvT[D_ATT:]
    row = lax.broadcasted_iota(jnp.int32, (HEAD_DIM, tm), 0)
    vrow = lax.broadcasted_iota(jnp.int32, (V_ROWS - HEAD_DIM, tk), 0)
    v_tail = jnp.where(vrow == 0, 1.0, 0.0).astype(BF16)

    for h in range(N_HEADS):
        kcol = kk[:, (h // 2) * LANES:(h // 2 + 1) * LANES]
        f_h = jnp.where(_piece_lane_mask(row, h), q_pieces, 0.0)
        aug_q = jnp.where(_piece_lane_mask(row, h), 1.0, 0.0) + jnp.concatenate(
            [jnp.zeros((q_shift, tm), F32), f_h[0:HEAD_DIM - q_shift]], axis=0)
        q_h = qT[h * HEAD_DIM:(h + 1) * HEAD_DIM, :]
        if h % 2 == 0:
            ext = jnp.where(_piece_lane_mask(lane, h + HEAD_DIM), p3_hi,
                            jnp.where(_piece_lane_mask(lane, h + HEAD_DIM + q_shift), 1.0, 0.0))
            kp = jnp.where(lane < HEAD_DIM, kcol, ext)
            qp = jnp.concatenate([q_h, aug_q], axis=0)
        else:
            ext = jnp.where(_piece_lane_mask(lane, h), p3,
                            jnp.where(_piece_lane_mask(lane, h + q_shift), 1.0, 0.0))
            kp = jnp.where(lane >= HEAD_DIM, kcol, ext)
            qp = jnp.concatenate([aug_q, q_h], axis=0)
        qpT_ref[0, h] = qp.astype(BF16)
        for cidx in range(tm // tk):
            kp_ref[0, h, cidx] = kp[cidx * tk:(cidx + 1) * tk, :].astype(BF16)
            v_h = vT[h * HEAD_DIM:(h + 1) * HEAD_DIM, cidx * tk:(cidx + 1) * tk].astype(BF16)
            vT_ref[0, h, cidx] = jnp.concatenate([v_h, v_tail], axis=0)

    kn2 = jnp.max(_dot((kk * kk).astype(BF16), hsel_ref[...]), axis=0, keepdims=True)
    srow = lax.broadcasted_iota(jnp.int32, (SUBLANES, LANES), 0)
    kstat_ref[0, 0] = jnp.where(srow == 0, c[0:1, :], jnp.where(srow == 1, c[tm - 1:tm, :],
                                                                 jnp.where(srow == 2, kn2, 0.0)))
    q2 = qT * qT
    qn2 = jnp.concatenate([jnp.sum(q2[h * HEAD_DIM:(h + 1) * HEAD_DIM, :], axis=0, keepdims=True)
                           for h in range(N_HEADS)], axis=0)
    qstat_ref[0, 0] = jnp.broadcast_to(jnp.max(qn2, axis=1, keepdims=True), (N_HEADS, LANES))


def _inproj(x2, g1, wrow, bf3, wqvT, *, batch, seq, tm, tk):
    nt = seq // tm
    nk = seq // tk
    ltri = jnp.tril(jnp.ones((tm, tm), F32)).astype(BF16)
    hsel = (jnp.arange(D_ATT)[:, None] // HEAD_DIM == jnp.arange(LANES)[None, :]).astype(BF16)
    const = lambda shape: pl.BlockSpec(shape, lambda b, t: (0,) * len(shape))
    stat_spec = pl.BlockSpec((1, 1, SUBLANES, LANES), lambda b, t: (b, t, 0, 0))
    stat_shape = jax.ShapeDtypeStruct((batch, nt, SUBLANES, LANES), F32)
    return pl.pallas_call(
        functools.partial(_inproj_kernel, tm=tm, tk=tk),
        grid=(batch, nt),
        in_specs=[
            pl.BlockSpec((tm, D_MODEL), lambda b, t: (b * nt + t, 0)),
            const((1, D_MODEL)), const((D_MODEL, 2 * D_CONV + D_ATT + LANES)), const((1, LANES)),
            const((2 * D_ATT, D_MODEL)), const((tm, tm)), const((D_ATT, LANES)),
        ],
        out_specs=[
            pl.BlockSpec((tm, D_CONV), lambda b, t: (b * nt + t, 0)),
            pl.BlockSpec((1, N_HEADS, tm // tk, tk, KAUG), lambda b, t: (b, 0, t, 0, 0)),
            pl.BlockSpec((1, N_HEADS, KAUG, tm), lambda b, t: (b, 0, 0, t)),
            pl.BlockSpec((1, N_HEADS, tm // tk, V_ROWS, tk), lambda b, t: (b, 0, t, 0, 0)),
            stat_spec, stat_spec,
        ],
        out_shape=[
            jax.ShapeDtypeStruct((batch * seq, D_CONV), F32),
            jax.ShapeDtypeStruct((batch, N_HEADS, nk, tk, KAUG), BF16),
            jax.ShapeDtypeStruct((batch, N_HEADS, KAUG, seq), BF16),
            jax.ShapeDtypeStruct((batch, N_HEADS, nk, V_ROWS, tk), BF16),
            stat_shape, stat_shape,
        ],
        scratch_shapes=[pltpu.VMEM((1, LANES), F32)],
        compiler_params=pltpu.CompilerParams(
            dimension_semantics=("arbitrary", "arbitrary"), vmem_limit_bytes=VMEM_LIMIT),
        name="inproj",
    )(x2, g1, wrow, bf3, wqvT, ltri, hsel)


def _conv_tile(a_ref, w_ref, b_ref, lng_ref, lnb_ref, og_ref, sh_ref, acc_ref, *, tc, rc):
    n_cb = D_CONV // LANES

    @pl.when(pl.program_id(1) == 0)
    def _():
        for cb in range(n_cb):
            sh_ref[0, cb, 0:CONV_HALO, :] = jnp.zeros((CONV_HALO, LANES), F32)

    @pl.when(pl.program_id(1) > 0)
    def _():
        for cb in range(n_cb):
            sh_ref[0, cb, 0:CONV_HALO, :] = sh_ref[0, cb, tc:tc + CONV_HALO, :]

    n_sh = tc + CONV_HALO - SUBLANES
    for cb in range(n_cb):
        sh_ref[0, cb, CONV_HALO:CONV_HALO + tc, :] = a_ref[:, cb * LANES:(cb + 1) * LANES]
        for f in range(1, SUBLANES):
            sh_ref[f, cb, 0:n_sh, :] = sh_ref[0, cb, f:f + n_sh, :]

    base = CONV_HALO - (CONV_WIDTH - 1)
    for cb in range(n_cb):
        cols = slice(cb * LANES, (cb + 1) * LANES)

        def chunk(c, carry, cb=cb, cols=cols):
            r0 = pl.multiple_of(c * rc, rc)
            acc = jnp.zeros((rc, LANES), F32)
            for j in range(CONV_WIDTH):
                f = (base + j) % SUBLANES
                acc = acc + w_ref[j:j + 1, cols] * sh_ref[f, cb, pl.ds(r0 + (base + j - f), rc), :]
            acc_ref[pl.ds(r0, rc), cols] = acc
            return carry

        lax.fori_loop(0, tc // rc, chunk, 0)

    y = acc_ref[...] + b_ref[...]
    mu = jnp.mean(y, axis=-1, keepdims=True)
    yc = y - mu
    var = jnp.mean(yc * yc, axis=-1, keepdims=True)
    yn = yc * lax.rsqrt(var + EPS) * lng_ref[...] + lnb_ref[...]
    s = yn * jax.nn.sigmoid(yn)
    ms = jnp.mean(s * s, axis=-1, keepdims=True)
    return (s * lax.rsqrt(ms + EPS) * og_ref[...]).astype(BF16)


def _attn_fixed_ref_kernel(g0_ref, bnd_ref, qT_ref, k_ref, v_ref, o_ref, *, tq, tk, unit):
    i = pl.program_id(2)
    qT = qT_ref[0, 0]
    n_sub = tq // tk
    bnd = bnd_ref[0, 0, 0:1, 0:1]
    sb0 = unit * g0_ref[(pl.program_id(0) * N_HEADS + pl.program_id(1)) * pl.num_programs(2) + i]

    def values(sb, n):
        return jnp.concatenate([v_ref[0, 0, sb + d] for d in range(n)], axis=1)

    def probs(sb, n):
        return jnp.concatenate([jnp.exp2(_dot(k_ref[0, 0, sb + d], qT) - bnd).astype(BF16)
                                for d in range(n)], axis=0)

    def accumulate(acc, sb, n):
        for lo in range(0, n, n_sub):
            m = min(n_sub, n - lo)
            acc = acc + _dot(values(sb + lo, m), probs(sb + lo, m))
        return acc

    def loop_trip(p, acc):
        return accumulate(acc, sb0 + GROUPS_PER_TRIP * n_sub * p, GROUPS_PER_TRIP * n_sub)

    key = lax.broadcasted_iota(jnp.int32, (tk, tk), 0)
    qry = lax.broadcasted_iota(jnp.int32, (tk, tk), 1)

    def diagonal(acc):
        rows = []
        for d in range(n_sub):
            s = _dot(k_ref[0, 0, i * n_sub + d], qT[:, d * tk:])
            tiles = [jnp.zeros((tk, d * tk), BF16)] if d else []
            tiles.append(jnp.exp2(jnp.where(key <= qry, s[:, :tk], -jnp.inf) - bnd).astype(BF16))
            if d + 1 < n_sub:
                tiles.append(jnp.exp2(s[:, tk:] - bnd).astype(BF16))
            rows.append(jnp.concatenate(tiles, axis=1))
        acc = acc + _dot(values(i * n_sub, n_sub), jnp.concatenate(rows, axis=0))
        o_ref[0] = acc[:HEAD_DIM] / acc[HEAD_DIM:HEAD_DIM + 1]

    n_before = i * n_sub - sb0
    per_trip = GROUPS_PER_TRIP * n_sub
    acc = lax.fori_loop(0, n_before // per_trip, loop_trip, jnp.zeros((V_ROWS, tq), F32))
    left = n_before % per_trip

    for n in range(0, per_trip, unit):
        @pl.when(left == n)
        def _(n=n):
            diagonal(accumulate(acc, i * n_sub - n, n))


def _attn_kernel(g0_ref, qT_ref, k_ref, v_ref, o_ref, sa_ref, sb_ref, cma_ref, cmb_ref, *, tq, tk):
    i = pl.program_id(2)
    qT = qT_ref[0, 0]
    n_sub = tq // tk

    def scores(g, s_ref, cm_ref):
        for d in range(n_sub):
            s = _dot(k_ref[0, 0, g * n_sub + d], qT)
            s_ref[d] = s
            cm_ref[d] = jnp.max(s, axis=0, keepdims=True)

    def consume(g, s_ref, cm_ref, carry, nxt=None):
        m, acc = carry
        m_new = m
        for d in range(n_sub):
            m_new = jnp.maximum(m_new, cm_ref[d])
        ps = []
        for d in range(n_sub):
            if nxt is not None:
                g_n, sn_ref, cmn_ref = nxt
                s = _dot(k_ref[0, 0, g_n * n_sub + d], qT)
                sn_ref[d] = s
                cmn_ref[d] = jnp.max(s, axis=0, keepdims=True)
            ps.append(jnp.exp2(s_ref[d] - m_new).astype(BF16))
        p = jnp.concatenate(ps, axis=0)
        v = jnp.concatenate([v_ref[0, 0, g * n_sub + d] for d in range(n_sub)], axis=1)
        return m_new, jnp.exp2(m - m_new) * acc + _dot(v, p)

    key = lax.broadcasted_iota(jnp.int32, (tk, tk), 0)
    qry = lax.broadcasted_iota(jnp.int32, (tk, tk), 1)

    def consume_diag(s_ref, cm_ref, carry):
        m, acc = carry
        lanes = lambda j: slice(j * tk, (j + 1) * tk)
        causal = [jnp.where(key <= qry, s_ref[d, :, lanes(d)], -jnp.inf) for d in range(n_sub)]
        m_tiles = []
        for j in range(n_sub):
            m_j = jnp.maximum(m[:, lanes(j)], jnp.max(causal[j], axis=0, keepdims=True))
            for d in range(j):
                m_j = jnp.maximum(m_j, cm_ref[d, :, lanes(j)])
            m_tiles.append(m_j)
        m_new = jnp.concatenate(m_tiles, axis=1)
        rows = []
        for d in range(n_sub):
            tiles = [jnp.zeros((tk, d * tk), BF16)] if d else []
            tiles.append(jnp.exp2(causal[d] - m_tiles[d]).astype(BF16))
            if d + 1 < n_sub:
                rest = slice((d + 1) * tk, tq)
                tiles.append(jnp.exp2(s_ref[d, :, rest] - m_new[:, rest]).astype(BF16))
            rows.append(jnp.concatenate(tiles, axis=1))
        p = jnp.concatenate(rows, axis=0)
        v = jnp.concatenate([v_ref[0, 0, i * n_sub + d] for d in range(n_sub)], axis=1)
        acc = jnp.exp2(m - m_new) * acc + _dot(v, p)
        o_ref[0] = acc[:HEAD_DIM] / acc[HEAD_DIM:HEAD_DIM + 1]

    g0 = g0_ref[(pl.program_id(0) * N_HEADS + pl.program_id(1)) * pl.num_programs(2) + i]
    n_full = i - g0

    def pair(p, carry):
        g = g0 + 2 * p
        carry = consume(g, sa_ref, cma_ref, carry, nxt=(g + 1, sb_ref, cmb_ref))
        return consume(g + 1, sb_ref, cmb_ref, carry, nxt=(g + 2, sa_ref, cma_ref))

    scores(g0, sa_ref, cma_ref)
    init = (jnp.full((1, tq), -jnp.inf, F32), jnp.zeros((V_ROWS, tq), F32))
    carry = lax.fori_loop(0, n_full // 2, pair, init)

    @pl.when(n_full % 2 == 1)
    def _():
        consume_diag(sb_ref, cmb_ref, consume(i - 1, sa_ref, cma_ref, carry, nxt=(i, sb_ref, cmb_ref)))

    @pl.when(n_full % 2 == 0)
    def _():
        consume_diag(sa_ref, cma_ref, carry)


def _first_groups(kstat, qstat, *, tm, tq):
    per = tq // tm
    bound = jnp.sqrt(jnp.max(kstat[:, :, 2, :N_HEADS], axis=1) * jnp.max(qstat[:, :, :, 0], axis=1))
    bound = bound * NORM_SLACK
    cq = kstat[:, ::per, 0, :N_HEADS]
    ck = kstat[:, :, 1, :N_HEADS]
    d = (cq[:, :, None, :] - ck[:, None, :, :]) * LOG2E
    nq, ng = cq.shape[1], ck.shape[1]
    earlier = per * jnp.arange(nq)[None, :, None, None] > jnp.arange(ng)[None, None, :, None]
    skip = earlier & (2.0 * bound[:, None, None, :] + d < -SKIP_LOG2)
    g0 = jnp.sum(jnp.cumprod(skip.astype(jnp.int32), axis=2), axis=2)
    return jnp.transpose(g0, (0, 2, 1)).reshape(-1).astype(jnp.int32), bound


def _attention(g0, bound, qpT, kp, vT, *, batch, seq, tm, tq, tk):
    nk = seq // tk
    params = pltpu.CompilerParams(
        dimension_semantics=("arbitrary", "arbitrary", "arbitrary"), vmem_limit_bytes=VMEM_LIMIT)
    out_shape = jax.ShapeDtypeStruct((batch, D_ATT, seq), F32)
    qkv_specs = [
        pl.BlockSpec((1, 1, KAUG, tq), lambda b, h, i, g0: (b, h, 0, i)),
        pl.BlockSpec((1, 1, nk, tk, KAUG), lambda b, h, i, g0: (b, h, 0, 0, 0)),
        pl.BlockSpec((1, 1, nk, V_ROWS, tk), lambda b, h, i, g0: (b, h, 0, 0, 0)),
    ]
    out_spec = pl.BlockSpec((1, HEAD_DIM, tq), lambda b, h, i, g0: (b, h, i))

    def fixed_ref(g0, bnd, qpT, kp, vT):
        grid_spec = pltpu.PrefetchScalarGridSpec(
            num_scalar_prefetch=1, grid=(batch, N_HEADS, seq // tq),
            in_specs=[pl.BlockSpec((1, 1, SUBLANES, LANES), lambda b, h, i, g0: (b, h, 0, 0))] + qkv_specs,
            out_specs=out_spec)
        return pl.pallas_call(functools.partial(_attn_fixed_ref_kernel, tq=tq, tk=tk, unit=tm // tk),
                              grid_spec=grid_spec, out_shape=out_shape, compiler_params=params,
                              name="attention_fixed_ref")(g0, bnd, qpT, kp, vT)

    def running_max(g0, bnd, qpT, kp, vT):
        del bnd
        grid_spec = pltpu.PrefetchScalarGridSpec(
            num_scalar_prefetch=1, grid=(batch, N_HEADS, seq // tq), in_specs=qkv_specs, out_specs=out_spec,
            scratch_shapes=[pltpu.VMEM((tq // tk, tk, tq), F32), pltpu.VMEM((tq // tk, tk, tq), F32),
                            pltpu.VMEM((tq // tk, 1, tq), F32), pltpu.VMEM((tq // tk, 1, tq), F32)])
        return pl.pallas_call(functools.partial(_attn_kernel, tq=tq, tk=tk), grid_spec=grid_spec,
                              out_shape=out_shape, compiler_params=params, name="attention",
                              )(g0 // (tq // tm), qpT, kp, vT)

    bnd = jnp.broadcast_to(bound[:, :, None, None], (batch, N_HEADS, SUBLANES, LANES))
    return lax.cond(jnp.max(bound) <= FIXED_REF_MAX_BOUND, fixed_ref, running_max, g0, bnd, qpT, kp, vT)


def _outproj_kernel(x_ref, a_ref, yT_ref, wdw_ref, bdw_ref, lng_ref, lnb_ref, gc_ref, wo_ref, ga_ref,
                    g2_ref, wrT_ref, brT_ref, utri_ref, x1_ref, h2_ref, route_ref, cnt_ref,
                    carry_ref, sh_ref, acc_ref, *, tm, rc):
    first = (pl.program_id(0) == 0) & (pl.program_id(1) == 0)

    @pl.when(first)
    def _():
        carry_ref[...] = jnp.zeros_like(carry_ref)

    mc = _conv_tile(a_ref, wdw_ref, bdw_ref, lng_ref, lnb_ref, gc_ref, sh_ref, acc_ref, tc=tm, rc=rc)
    yT = yT_ref[0]
    msa = jnp.mean(yT * yT, axis=0, keepdims=True)
    yn = (yT * lax.rsqrt(msa + EPS)).T * ga_ref[...]
    x1 = x_ref[...] + _dot(jnp.concatenate([mc, yn.astype(BF16)], axis=1), wo_ref[...])
    x1_ref[...] = x1
    ms = jnp.mean(x1 * x1, axis=-1, keepdims=True)
    h2 = (x1 * lax.rsqrt(ms + EPS)) * g2_ref[...]
    h2_ref[:, :D_MODEL] = h2

    lg = _dot_nt(wrT_ref[...], h2.astype(BF16)) + brT_ref[...]
    neg = -jnp.inf
    sub = lax.broadcasted_iota(jnp.int32, (SUBLANES, tm), 0)
    big = jnp.int32(SUBLANES)
    first_of = lambda hit: jnp.min(jnp.where(hit, sub, big), axis=0, keepdims=True)

    lg1 = lg[0:SUBLANES]
    m1 = jnp.max(lg1, axis=0, keepdims=True)
    p1_sel = 1.0 / jnp.sum(jnp.exp(lg1 - m1), axis=0, keepdims=True)
    grp = first_of(lg1 == m1)

    slab = lambda g: lg[SUBLANES * (g + 1):SUBLANES * (g + 2)]
    v = slab(N_GROUPS - 1)
    for g in range(N_GROUPS - 2, -1, -1):
        v = jnp.where(grp == g, slab(g), v)
    v1 = jnp.max(v, axis=0, keepdims=True)
    j1 = first_of(v == v1)
    vv = jnp.where(sub == j1, neg, v)
    v2 = jnp.max(vv, axis=0, keepdims=True)
    j2 = first_of(vv == v2)
    e21 = jnp.exp(v2 - v1)
    w0 = p1_sel / (1.0 + e21)
    w1 = p1_sel * e21 / (1.0 + e21)

    swap = j2 < j1
    al = jnp.where(swap, j2, j1)
    bl = jnp.where(swap, j1, j2)
    wa = jnp.where(swap, w1, w0)
    wb = jnp.where(swap, w0, w1)
    cid = PAIRS_PER_GROUP * grp + ((al * (2 * EXPERTS_PER_GROUP - 1 - al)) >> 1) + (bl - al - 1)

    cls = lax.broadcasted_iota(jnp.int32, (LANES, tm), 0)
    oh = cls == cid
    cmat = jnp.where(oh, 1.0, 0.0)
    prefix = _dot(cmat.astype(BF16), utri_ref[...]) + carry_ref[...]
    rank = jnp.sum(jnp.where(oh, prefix, 0.0), axis=0, keepdims=True)
    counts = prefix[:, tm - 1:tm] + cmat[:, tm - 1:tm]
    carry_ref[...] = counts
    cnt_ref[...] = counts

    h2_ref[:, D_MODEL:] = jnp.where(cls == 0, wa, jnp.where(cls == 1, wb, 0.0)).T
    route_ref[...] = jnp.where(sub == 0, cid.astype(F32), jnp.where(sub == 1, rank, 0.0))


def _outproj(x2, a, yT, conv_params, wo, ga, g2, wrT, brT, *, batch, seq, tm, rc):
    nt = seq // tm
    utri = jnp.triu(jnp.ones((tm, tm), F32), 1).astype(BF16)
    const = lambda shape: pl.BlockSpec(shape, lambda b, t: (0,) * len(shape))
    row_spec = lambda w: pl.BlockSpec((tm, w), lambda b, t: (b * nt + t, 0))
    return pl.pallas_call(
        functools.partial(_outproj_kernel, tm=tm, rc=rc),
        grid=(batch, nt),
        in_specs=[row_spec(D_MODEL), row_spec(D_CONV),
                  pl.BlockSpec((1, D_ATT, tm), lambda b, t: (b, 0, t)),
                  const((CONV_WIDTH, D_CONV)), const((1, D_CONV)), const((1, D_CONV)), const((1, D_CONV)),
                  const((1, D_CONV)),
                  const((D_CONV + D_ATT, D_MODEL)), const((1, D_ATT)), const((1, D_MODEL)),
                  const((LANES, D_MODEL)), const((LANES, 1)), const((tm, tm))],
        out_specs=[row_spec(D_MODEL), row_spec(ROW_W),
                   pl.BlockSpec((SUBLANES, tm), lambda b, t: (0, b * nt + t)), const((LANES, 1))],
        out_shape=[jax.ShapeDtypeStruct((batch * seq, D_MODEL), F32),
                   jax.ShapeDtypeStruct((batch * seq, ROW_W), F32),
                   jax.ShapeDtypeStruct((SUBLANES, batch * seq), F32),
                   jax.ShapeDtypeStruct((LANES, 1), F32)],
        scratch_shapes=[pltpu.VMEM((LANES, 1), F32),
                        pltpu.VMEM((SUBLANES, D_CONV // LANES, tm + CONV_HALO, LANES), F32),
                        pltpu.VMEM((tm, D_CONV), F32)],
        compiler_params=pltpu.CompilerParams(
            dimension_semantics=("arbitrary", "arbitrary"), vmem_limit_bytes=VMEM_LIMIT),
        name="outproj",
    )(x2, a, yT, *conv_params, wo, ga, g2, wrT, brT, utri)


def _row_copies(src_at, dst_at, sem):
    return pltpu.make_async_copy(src_at, dst_at, sem)


def _dispatch_kernel(dest_ref, h2_ref, xs_ref, sem, *, tm):
    def body(g, carry):
        for u in range(SUBLANES):
            _row_copies(h2_ref.at[g, pl.ds(u, 1)], xs_ref.at[pl.ds(dest_ref[g * SUBLANES + u], 1)],
                        sem).start(priority=u % 2)
        return carry

    lax.fori_loop(0, tm // SUBLANES, body, 0)
    done = xs_ref.at[pl.ds(0, tm)]
    _row_copies(done, done, sem).wait()


def _dispatch(dest, h2, *, tm):
    t_len = h2.shape[0]
    return pl.pallas_call(
        functools.partial(_dispatch_kernel, tm=tm),
        grid=(t_len // tm,),
        in_specs=[pl.BlockSpec((tm,), lambda i: (i,), memory_space=pltpu.SMEM),
                  pl.BlockSpec((tm // SUBLANES, SUBLANES, ROW_W), lambda i: (i, 0, 0))],
        out_specs=pl.BlockSpec(memory_space=pl.ANY),
        out_shape=jax.ShapeDtypeStruct((t_len, ROW_W), F32),
        scratch_shapes=[pltpu.SemaphoreType.DMA(())],
        compiler_params=pltpu.CompilerParams(dimension_semantics=("arbitrary",), vmem_limit_bytes=VMEM_LIMIT),
        name="dispatch",
    )(dest, h2.reshape(t_len // SUBLANES, SUBLANES, ROW_W))


def _expert_kernel(blk_ref, ea_ref, eb_ref, lo_ref, hi_ref, x_ref, wgua_ref, wda_ref, wgub_ref, wdb_ref, y_ref):
    del blk_ref, ea_ref, eb_ref
    i = pl.program_id(0)
    lo = lo_ref[i]
    hi = hi_ref[i]
    n_seg = ROW_BLOCK // ROW_SEGMENT

    def mlp(xb, wgu_ref, wd_ref):
        gu = _dot(xb, wgu_ref[0])
        g = gu[:, :D_EXPERT]
        return _dot(((g * jax.nn.sigmoid(g)) * gu[:, D_EXPERT:]).astype(BF16), wd_ref[0])

    def run(rows):
        n = rows.stop - rows.start
        xb = x_ref[rows, :D_MODEL].astype(BF16)
        wts = x_ref[rows, D_MODEL:]
        y = wts[:, 0:1] * mlp(xb, wgua_ref, wda_ref) + wts[:, 1:2] * mlp(xb, wgub_ref, wdb_ref)
        row = rows.start + lax.broadcasted_iota(jnp.int32, (n, 1), 0)
        mine = (row >= lo) & (row < hi)

        @pl.when(lo == 0)
        def _():
            y_ref[rows, :] = jnp.where(mine, y, 0.0)
            if rows.stop < ROW_BLOCK:
                y_ref[rows.stop:, :] = jnp.zeros((ROW_BLOCK - rows.stop, D_MODEL), F32)

        @pl.when(lo > 0)
        def _():
            y_ref[rows, :] = jnp.where(mine, y, y_ref[rows, :])

    first = lo // ROW_SEGMENT
    last = (hi - 1) // ROW_SEGMENT
    for s0 in range(n_seg):
        for s1 in range(s0, n_seg):
            @pl.when((hi > lo) & (first == s0) & (last == s1))
            def _(s0=s0, s1=s1):
                run(slice(s0 * ROW_SEGMENT, (s1 + 1) * ROW_SEGMENT))


def _experts(items, xs, wgu, wd):
    n_rows = xs.shape[0]
    n_items = items[0].shape[0]
    wspec = lambda shape, which: pl.BlockSpec(
        shape, lambda i, blk, ea, eb, lo, hi: ((ea, eb)[which][i], 0, 0))
    grid_spec = pltpu.PrefetchScalarGridSpec(
        num_scalar_prefetch=5,
        grid=(n_items,),
        in_specs=[pl.BlockSpec((ROW_BLOCK, ROW_W), lambda i, blk, ea, eb, lo, hi: (blk[i], 0)),
                  wspec((1, D_MODEL, 2 * D_EXPERT), 0), wspec((1, D_EXPERT, D_MODEL), 0),
                  wspec((1, D_MODEL, 2 * D_EXPERT), 1), wspec((1, D_EXPERT, D_MODEL), 1)],
        out_specs=pl.BlockSpec((ROW_BLOCK, D_MODEL), lambda i, blk, ea, eb, lo, hi: (blk[i], 0)),
    )
    return pl.pallas_call(
        _expert_kernel,
        grid_spec=grid_spec,
        out_shape=jax.ShapeDtypeStruct((n_rows, D_MODEL), F32),
        compiler_params=pltpu.CompilerParams(dimension_semantics=("arbitrary",), vmem_limit_bytes=VMEM_LIMIT),
        name="experts",
    )(*items, xs, wgu, wd, wgu, wd)


def _combine_kernel(dcur_ref, dnxt_ref, x1_ref, g_ref, ys_ref, out_ref, buf_ref, sems, *, tm):
    i = pl.program_id(0)
    slot = i % 2

    def issue(d_ref, s):
        def body(g, carry):
            for u in range(SUBLANES):
                _row_copies(ys_ref.at[pl.ds(d_ref[g * SUBLANES + u], 1)], buf_ref.at[s, g, pl.ds(u, 1)],
                            sems.at[s]).start(priority=u % 2)
            return carry

        lax.fori_loop(0, tm // SUBLANES, body, 0)

    @pl.when(i == 0)
    def _():
        issue(dcur_ref, 0)

    @pl.when(i + 1 < pl.num_programs(0))
    def _():
        issue(dnxt_ref, 1 - slot)

    _row_copies(buf_ref.at[slot], buf_ref.at[slot], sems.at[slot]).wait()
    x2 = x1_ref[...] + buf_ref[slot]
    ms = jnp.mean(x2 * x2, axis=-1, keepdims=True)
    out_ref[...] = (x2 * lax.rsqrt(ms + EPS)) * g_ref[...]


def _combine(dest, x1, gf, ys, *, tm):
    t_len = x1.shape[0]
    n = t_len // tm
    return pl.pallas_call(
        functools.partial(_combine_kernel, tm=tm),
        grid=(n,),
        in_specs=[pl.BlockSpec((tm,), lambda i: (i,), memory_space=pltpu.SMEM),
                  pl.BlockSpec((tm,), lambda i: (jnp.minimum(i + 1, n - 1),), memory_space=pltpu.SMEM),
                  pl.BlockSpec((tm // SUBLANES, SUBLANES, D_MODEL), lambda i: (i, 0, 0)),
                  pl.BlockSpec((1, D_MODEL), lambda i: (0, 0)),
                  pl.BlockSpec(memory_space=pl.ANY)],
        out_specs=pl.BlockSpec((tm // SUBLANES, SUBLANES, D_MODEL), lambda i: (i, 0, 0)),
        out_shape=jax.ShapeDtypeStruct((t_len // SUBLANES, SUBLANES, D_MODEL), F32),
        scratch_shapes=[pltpu.VMEM((2, tm // SUBLANES, SUBLANES, D_MODEL), F32), pltpu.SemaphoreType.DMA((2,))],
        compiler_params=pltpu.CompilerParams(dimension_semantics=("arbitrary",), vmem_limit_bytes=VMEM_LIMIT),
        name="combine",
    )(dest, dest, x1.reshape(t_len // SUBLANES, SUBLANES, D_MODEL), gf, ys)


def _layer(x, norm1_g, w_in, b_f, w_dw, b_dw, conv_ln_g, conv_ln_b, out_g_conv, out_g_att, w_out,
           norm2_g, w_r1, b_r1, w_r2, b_r2, w_gate, w_up, w_down):
    batch, seq, d = x.shape
    t_len = batch * seq
    tm = min(512, seq)
    tk = min(256, seq)
    tq = min(1024, seq)
    o2 = 2 * D_CONV
    o3, o4, o5 = o2 + D_ATT, o2 + 2 * D_ATT, o2 + 3 * D_ATT

    assert PIECE_STRIDE == N_HEADS
    pad = LANES - N_PIECES * PIECE_STRIDE
    wf = jnp.pad(jnp.tile(w_in[:, o5:], (1, N_PIECES)), ((0, 0), (0, pad)))
    wrow = jnp.concatenate([w_in[:, :o2], w_in[:, o3:o4], wf], axis=1).astype(BF16)
    wqvT = jnp.concatenate([w_in[:, o2:o3] * (HEAD_DIM ** -0.5), w_in[:, o4:o5]], axis=1).T.astype(BF16)
    bf3 = jnp.pad(jnp.tile(b_f.astype(F32).reshape(1, N_HEADS), (1, N_PIECES)), ((0, 0), (0, pad)))

    x2 = x.reshape(t_len, d)
    a, kp, qpT, vT, kstat, qstat = _inproj(x2, norm1_g.reshape(1, d), wrow, bf3, wqvT,
                                           batch=batch, seq=seq, tm=tm, tk=tk)
    conv_params = (w_dw, b_dw.reshape(1, -1), conv_ln_g.reshape(1, -1), conv_ln_b.reshape(1, -1),
                   out_g_conv.reshape(1, -1))
    g0, bound = _first_groups(kstat, qstat, tm=tm, tq=tq)
    yT = _attention(g0, bound, qpT, kp, vT, batch=batch, seq=seq, tm=tm, tq=tq, tk=tk)

    gpad = SUBLANES - N_GROUPS
    rpad = LANES - SUBLANES - N_EXPERTS
    wrT = jnp.concatenate([w_r1.T, jnp.zeros((gpad, d), F32),
                           jnp.transpose(w_r2, (0, 2, 1)).reshape(N_EXPERTS, d), jnp.zeros((rpad, d), F32)], axis=0)
    brT = jnp.concatenate([b_r1.astype(F32), jnp.full((gpad,), -jnp.inf, F32), b_r2.reshape(-1).astype(F32),
                           jnp.zeros((rpad,), F32)]).reshape(LANES, 1)
    x1, h2, route, cnt = _outproj(x2, a, yT, conv_params, w_out.astype(BF16),
                                  out_g_att.reshape(1, -1), norm2_g.reshape(1, d),
                                  wrT.astype(BF16), brT, batch=batch, seq=seq, tm=tm, rc=min(256, seq))

    i32 = jnp.int32
    lanes = jnp.arange(LANES, dtype=i32)
    pick = lambda table, idx: jnp.sum(jnp.where(idx[:, None] == lanes, table[None, :], 0), axis=1).astype(i32)
    counts = cnt[:, 0].astype(i32)
    ends = jnp.cumsum(counts).astype(i32)
    starts = ends - counts
    dest = pick(starts, route[0].astype(i32)) + route[1].astype(i32)

    n_blocks = t_len // ROW_BLOCK
    b_lo = starts // ROW_BLOCK
    n_it = jnp.where(counts > 0, (ends - 1) // ROW_BLOCK - b_lo + 1, 0)
    it_end = jnp.cumsum(n_it).astype(i32)
    it_start = it_end - n_it
    idx = jnp.arange(n_blocks + N_CLASSES, dtype=i32)
    valid = idx < it_end[-1]
    last_cls = jnp.max(jnp.where(counts > 0, lanes, 0))
    cls = jnp.where(valid, jnp.sum(it_end[None, :] <= idx[:, None], axis=1).astype(i32), last_cls)
    blk = jnp.where(valid, pick(b_lo, cls) + idx - pick(it_start, cls), n_blocks - 1)
    row0 = blk * ROW_BLOCK
    lo = jnp.where(valid, jnp.maximum(pick(starts, cls), row0) - row0, 0)
    hi = jnp.where(valid, jnp.minimum(pick(ends, cls), row0 + ROW_BLOCK) - row0, 0)
    pair_a, pair_b = [], []
    for g in range(N_GROUPS):
        for a_loc in range(EXPERTS_PER_GROUP):
            for b_loc in range(a_loc + 1, EXPERTS_PER_GROUP):
                pair_a.append(g * EXPERTS_PER_GROUP + a_loc)
                pair_b.append(g * EXPERTS_PER_GROUP + b_loc)
    cpad = [0] * (LANES - N_CLASSES)
    items = (blk.astype(i32), pick(jnp.array(pair_a + cpad, i32), cls), pick(jnp.array(pair_b + cpad, i32), cls),
             lo.astype(i32), hi.astype(i32))

    xs = _dispatch(dest, h2, tm=min(4096, t_len))
    ys = _experts(items, xs, jnp.concatenate([w_gate, w_up], axis=-1).astype(BF16), w_down.astype(BF16))
    return dest, x1, ys


def kernel(x, norm1_g, w_in, b_f, w_dw, b_dw, conv_ln_g, conv_ln_b, out_g_conv, out_g_att, w_out, norm2_g,
           w_r1, b_r1, w_r2, b_r2, w_gate, w_up, w_down, final_g):
    assert norm1_g.shape[0] == 1, "single-layer stack"
    batch, seq, d = x.shape
    dest, x1, ys = _layer(
        x, norm1_g[0], w_in[0], b_f[0], w_dw[0], b_dw[0], conv_ln_g[0], conv_ln_b[0], out_g_conv[0],
        out_g_att[0], w_out[0], norm2_g[0], w_r1[0], b_r1[0], w_r2[0], b_r2[0], w_gate[0], w_up[0], w_down[0])
    out = _combine(dest, x1, final_g.reshape(1, d), ys, tm=min(512, batch * seq))
    return out.reshape(batch, seq, d)
```

```python
import functools

import jax
import jax.numpy as jnp
from jax import lax
from jax.experimental import pallas as pl
from jax.experimental.pallas import tpu as pltpu

D_MODEL = 1024
D_CONV = 512
N_HEADS = 8
HEAD_DIM = 64
D_ATT = N_HEADS * HEAD_DIM
CONV_WIDTH = 31
N_GROUPS = 4
EXPERTS_PER_GROUP = 8
N_EXPERTS = N_GROUPS * EXPERTS_PER_GROUP
D_EXPERT = D_MODEL // 4
ROW_BLOCK = 1024
ROW_SEGMENT = 128
EPS = 1e-6

LANES = 128
SUBLANES = 8
KAUG = 128
N_PIECES = 3
PIECE_STRIDE = 8
CONV_HALO = 32
PAIRS_PER_GROUP = EXPERTS_PER_GROUP * (EXPERTS_PER_GROUP - 1) // 2
N_CLASSES = N_GROUPS * PAIRS_PER_GROUP
ROW_W = D_MODEL + LANES
SKIP_LOG2 = 160.0
FIXED_REF_MAX_BOUND = 40.0
GROUPS_PER_TRIP = 4
NORM_SLACK = 1.02
V_ROWS = 80
LOG2E = 1.4426950408889634
VMEM_LIMIT = 56 * 1024 * 1024

F32 = jnp.float32
BF16 = jnp.bfloat16


def _dot(a, b):
    return jnp.dot(a, b, preferred_element_type=F32)


def _dot_nt(a, b):
    return lax.dot_general(a, b, (((1,), (1,)), ((), ())), preferred_element_type=F32)


def _split3(x):
    hi = x.astype(BF16)
    r1 = x - hi.astype(F32)
    mid = r1.astype(BF16)
    lo = (r1 - mid.astype(F32)).astype(BF16)
    return hi.astype(F32), mid.astype(F32), lo.astype(F32)


def _piece_lane_mask(lane, h):
    return (lane == h) | (lane == h + PIECE_STRIDE) | (lane == h + 2 * PIECE_STRIDE)


def _inproj_kernel(x_ref, g1_ref, wrow_ref, bf_ref, wqvT_ref, ltri_ref, hsel_ref,
                   a_ref, kp_ref, qpT_ref, vT_ref, kstat_ref, qstat_ref, carry_ref, *, tm, tk):
    @pl.when(pl.program_id(1) == 0)
    def _():
        carry_ref[...] = jnp.zeros_like(carry_ref)

    x = x_ref[...]
    ms = jnp.mean(x * x, axis=-1, keepdims=True)
    hb = ((x * lax.rsqrt(ms + EPS)) * g1_ref[...]).astype(BF16)

    z = _dot(hb, wrow_ref[...])
    a_ref[...] = z[:, :D_CONV] * jax.nn.sigmoid(z[:, D_CONV:2 * D_CONV])

    kk = z[:, 2 * D_CONV:2 * D_CONV + D_ATT]

    zf = z[:, 2 * D_CONV + D_ATT:] + bf_ref[...]
    lf = jnp.minimum(zf, 0.0) - jnp.log1p(jnp.exp(-jnp.abs(zf)))
    lane = lax.broadcasted_iota(jnp.int32, (tm, LANES), 1)
    hi, mid, lo = _split3(lf)
    lf3 = jnp.where(lane < PIECE_STRIDE, hi,
                    jnp.where(lane < 2 * PIECE_STRIDE, mid,
                              jnp.where(lane < 3 * PIECE_STRIDE, lo, 0.0))).astype(BF16)
    cs3 = _dot(ltri_ref[...], lf3)
    c = (cs3 + pltpu.roll(cs3, LANES - PIECE_STRIDE, 1)
         + pltpu.roll(cs3, LANES - 2 * PIECE_STRIDE, 1)) + carry_ref[...]
    carry_ref[...] = c[tm - 1:tm, :]

    nhi, nmid, nlo = _split3(c * (-LOG2E))
    p3 = jnp.where(lane < PIECE_STRIDE, nhi,
                   jnp.where(lane < 2 * PIECE_STRIDE, pltpu.roll(nmid, PIECE_STRIDE, 1),
                             jnp.where(lane < 3 * PIECE_STRIDE, pltpu.roll(nlo, 2 * PIECE_STRIDE, 1), 0.0)))
    p3_hi = pltpu.roll(p3, HEAD_DIM, 1)
    q_pieces = (-p3).T[0:HEAD_DIM]
    q_shift = N_PIECES * PIECE_STRIDE

    qvT = _dot_nt(wqvT_ref[...], hb)
    qT = qvT[:D_ATT] * LOG2E
    vT = qvT[D_ATT:]
    row = lax.broadcasted_iota(jnp.int32, (HEAD_DIM, tm), 0)
    vrow = lax.broadcasted_iota(jnp.int32, (V_ROWS - HEAD_DIM, tk), 0)
    v_tail = jnp.where(vrow == 0, 1.0, 0.0).astype(BF16)

    for h in range(N_HEADS):
        kcol = kk[:, (h // 2) * LANES:(h // 2 + 1) * LANES]
        f_h = jnp.where(_piece_lane_mask(row, h), q_pieces, 0.0)
        aug_q = jnp.where(_piece_lane_mask(row, h), 1.0, 0.0) + jnp.concatenate(
            [jnp.zeros((q_shift, tm), F32), f_h[0:HEAD_DIM - q_shift]], axis=0)
        q_h = qT[h * HEAD_DIM:(h + 1) * HEAD_DIM, :]
        if h % 2 == 0:
            ext = jnp.where(_piece_lane_mask(lane, h + HEAD_DIM), p3_hi,
                            jnp.where(_piece_lane_mask(lane, h + HEAD_DIM + q_shift), 1.0, 0.0))
            kp = jnp.where(lane < HEAD_DIM, kcol, ext)
            qp = jnp.concatenate([q_h, aug_q], axis=0)
        else:
            ext = jnp.where(_piece_lane_mask(lane, h), p3,
                            jnp.where(_piece_lane_mask(lane, h + q_shift), 1.0, 0.0))
            kp = jnp.where(lane >= HEAD_DIM, kcol, ext)
            qp = jnp.concatenate([aug_q, q_h], axis=0)
        qpT_ref[0, h] = qp.astype(BF16)
        for cidx in range(tm // tk):
            kp_ref[0, h, cidx] = kp[cidx * tk:(cidx + 1) * tk, :].astype(BF16)
            v_h = vT[h * HEAD_DIM:(h + 1) * HEAD_DIM, cidx * tk:(cidx + 1) * tk].astype(BF16)
            vT_ref[0, h, cidx] = jnp.concatenate([v_h, v_tail], axis=0)

    kn2 = jnp.max(_dot((kk * kk).astype(BF16), hsel_ref[...]), axis=0, keepdims=True)
    srow = lax.broadcasted_iota(jnp.int32, (SUBLANES, LANES), 0)
    kstat_ref[0, 0] = jnp.where(srow == 0, c[0:1, :], jnp.where(srow == 1, c[tm - 1:tm, :],
                                                                 jnp.where(srow == 2, kn2, 0.0)))
    q2 = qT * qT
    qn2 = jnp.concatenate([jnp.sum(q2[h * HEAD_DIM:(h + 1) * HEAD_DIM, :], axis=0, keepdims=True)
                           for h in range(N_HEADS)], axis=0)
    qstat_ref[0, 0] = jnp.broadcast_to(jnp.max(qn2, axis=1, keepdims=True), (N_HEADS, LANES))


def _inproj(x2, g1, wrow, bf3, wqvT, *, batch, seq, tm, tk):
    nt = seq // tm
    nk = seq // tk
    ltri = jnp.tril(jnp.ones((tm, tm), F32)).astype(BF16)
    hsel = (jnp.arange(D_ATT)[:, None] // HEAD_DIM == jnp.arange(LANES)[None, :]).astype(BF16)
    const = lambda shape: pl.BlockSpec(shape, lambda b, t: (0,) * len(shape))
    stat_spec = pl.BlockSpec((1, 1, SUBLANES, LANES), lambda b, t: (b, t, 0, 0))
    stat_shape = jax.ShapeDtypeStruct((batch, nt, SUBLANES, LANES), F32)
    return pl.pallas_call(
        functools.partial(_inproj_kernel, tm=tm, tk=tk),
        grid=(batch, nt),
        in_specs=[
            pl.BlockSpec((tm, D_MODEL), lambda b, t: (b * nt + t, 0)),
            const((1, D_MODEL)), const((D_MODEL, 2 * D_CONV + D_ATT + LANES)), const((1, LANES)),
            const((2 * D_ATT, D_MODEL)), const((tm, tm)), const((D_ATT, LANES)),
        ],
        out_specs=[
            pl.BlockSpec((tm, D_CONV), lambda b, t: (b * nt + t, 0)),
            pl.BlockSpec((1, N_HEADS, tm // tk, tk, KAUG), lambda b, t: (b, 0, t, 0, 0)),
            pl.BlockSpec((1, N_HEADS, KAUG, tm), lambda b, t: (b, 0, 0, t)),
            pl.BlockSpec((1, N_HEADS, tm // tk, V_ROWS, tk), lambda b, t: (b, 0, t, 0, 0)),
            stat_spec, stat_spec,
        ],
        out_shape=[
            jax.ShapeDtypeStruct((batch * seq, D_CONV), F32),
            jax.ShapeDtypeStruct((batch, N_HEADS, nk, tk, KAUG), BF16),
            jax.ShapeDtypeStruct((batch, N_HEADS, KAUG, seq), BF16),
            jax.ShapeDtypeStruct((batch, N_HEADS, nk, V_ROWS, tk), BF16),
            stat_shape, stat_shape,
        ],
        scratch_shapes=[pltpu.VMEM((1, LANES), F32)],
        compiler_params=pltpu.CompilerParams(
            dimension_semantics=("arbitrary", "arbitrary"), vmem_limit_bytes=VMEM_LIMIT),
        name="inproj",
    )(x2, g1, wrow, bf3, wqvT, ltri, hsel)


def _conv_tile(a_ref, w_ref, b_ref, lng_ref, lnb_ref, og_ref, sh_ref, acc_ref, *, tc, rc):
    n_cb = D_CONV // LANES

    @pl.when(pl.program_id(1) == 0)
    def _():
        for cb in range(n_cb):
            sh_ref[0, cb, 0:CONV_HALO, :] = jnp.zeros((CONV_HALO, LANES), F32)

    @pl.when(pl.program_id(1) > 0)
    def _():
        for cb in range(n_cb):
            sh_ref[0, cb, 0:CONV_HALO, :] = sh_ref[0, cb, tc:tc + CONV_HALO, :]

    n_sh = tc + CONV_HALO - SUBLANES
    for cb in range(n_cb):
        sh_ref[0, cb, CONV_HALO:CONV_HALO + tc, :] = a_ref[:, cb * LANES:(cb + 1) * LANES]
        for f in range(1, SUBLANES):
            sh_ref[f, cb, 0:n_sh, :] = sh_ref[0, cb, f:f + n_sh, :]

    base = CONV_HALO - (CONV_WIDTH - 1)
    for cb in range(n_cb):
        cols = slice(cb * LANES, (cb + 1) * LANES)

        def chunk(c, carry, cb=cb, cols=cols):
            r0 = pl.multiple_of(c * rc, rc)
            acc = jnp.zeros((rc, LANES), F32)
            for j in range(CONV_WIDTH):
                f = (base + j) % SUBLANES
                acc = acc + w_ref[j:j + 1, cols] * sh_ref[f, cb, pl.ds(r0 + (base + j - f), rc), :]
            acc_ref[pl.ds(r0, rc), cols] = acc
            return carry

        lax.fori_loop(0, tc // rc, chunk, 0)

    y = acc_ref[...] + b_ref[...]
    mu = jnp.mean(y, axis=-1, keepdims=True)
    yc = y - mu
    var = jnp.mean(yc * yc, axis=-1, keepdims=True)
    yn = yc * lax.rsqrt(var + EPS) * lng_ref[...] + lnb_ref[...]
    s = yn * jax.nn.sigmoid(yn)
    ms = jnp.mean(s * s, axis=-1, keepdims=True)
    return (s * lax.rsqrt(ms + EPS) * og_ref[...]).astype(BF16)


def _attn_fixed_ref_kernel(g0_ref, bnd_ref, qT_ref, k_ref, v_ref, o_ref, *, tq, tk, unit):
    i = pl.program_id(2)
    qT = qT_ref[0, 0]
    n_sub = tq // tk
    bnd = bnd_ref[0, 0, 0:1, 0:1]
    sb0 = unit * g0_ref[(pl.program_id(0) * N_HEADS + pl.program_id(1)) * pl.num_programs(2) + i]

    def values(sb, n):
        return jnp.concatenate([v_ref[0, 0, sb + d] for d in range(n)], axis=1)

    def probs(sb, n):
        return jnp.concatenate([jnp.exp2(_dot(k_ref[0, 0, sb + d], qT) - bnd).astype(BF16)
                                for d in range(n)], axis=0)

    def accumulate(acc, sb, n):
        for lo in range(0, n, n_sub):
            m = min(n_sub, n - lo)
            acc = acc + _dot(values(sb + lo, m), probs(sb + lo, m))
        return acc

    def loop_trip(p, acc):
        return accumulate(acc, sb0 + GROUPS_PER_TRIP * n_sub * p, GROUPS_PER_TRIP * n_sub)

    key = lax.broadcasted_iota(jnp.int32, (tk, tk), 0)
    qry = lax.broadcasted_iota(jnp.int32, (tk, tk), 1)

    def diagonal(acc):
        rows = []
        for d in range(n_sub):
            s = _dot(k_ref[0, 0, i * n_sub + d], qT[:, d * tk:])
            tiles = [jnp.zeros((tk, d * tk), BF16)] if d else []
            tiles.append(jnp.exp2(jnp.where(key <= qry, s[:, :tk], -jnp.inf) - bnd).astype(BF16))
            if d + 1 < n_sub:
                tiles.append(jnp.exp2(s[:, tk:] - bnd).astype(BF16))
            rows.append(jnp.concatenate(tiles, axis=1))
        acc = acc + _dot(values(i * n_sub, n_sub), jnp.concatenate(rows, axis=0))
        o_ref[0] = acc[:HEAD_DIM] / acc[HEAD_DIM:HEAD_DIM + 1]

    n_before = i * n_sub - sb0
    per_trip = GROUPS_PER_TRIP * n_sub
    acc = lax.fori_loop(0, n_before // per_trip, loop_trip, jnp.zeros((V_ROWS, tq), F32))
    left = n_before % per_trip

    for n in range(0, per_trip, unit):
        @pl.when(left == n)
        def _(n=n):
            diagonal(accumulate(acc, i * n_sub - n, n))


def _attn_kernel(g0_ref, qT_ref, k_ref, v_ref, o_ref, sa_ref, sb_ref, cma_ref, cmb_ref, *, tq, tk):
    i = pl.program_id(2)
    qT = qT_ref[0, 0]
    n_sub = tq // tk

    def scores(g, s_ref, cm_ref):
        for d in range(n_sub):
            s = _dot(k_ref[0, 0, g * n_sub + d], qT)
            s_ref[d] = s
            cm_ref[d] = jnp.max(s, axis=0, keepdims=True)

    def consume(g, s_ref, cm_ref, carry, nxt=None):
        m, acc = carry
        m_new = m
        for d in range(n_sub):
            m_new = jnp.maximum(m_new, cm_ref[d])
        ps = []
        for d in range(n_sub):
            if nxt is not None:
                g_n, sn_ref, cmn_ref = nxt
                s = _dot(k_ref[0, 0, g_n * n_sub + d], qT)
                sn_ref[d] = s
                cmn_ref[d] = jnp.max(s, axis=0, keepdims=True)
            ps.append(jnp.exp2(s_ref[d] - m_new).astype(BF16))
        p = jnp.concatenate(ps, axis=0)
        v = jnp.concatenate([v_ref[0, 0, g * n_sub + d] for d in range(n_sub)], axis=1)
        return m_new, jnp.exp2(m - m_new) * acc + _dot(v, p)

    key = lax.broadcasted_iota(jnp.int32, (tk, tk), 0)
    qry = lax.broadcasted_iota(jnp.int32, (tk, tk), 1)

    def consume_diag(s_ref, cm_ref, carry):
        m, acc = carry
        lanes = lambda j: slice(j * tk, (j + 1) * tk)
        causal = [jnp.where(key <= qry, s_ref[d, :, lanes(d)], -jnp.inf) for d in range(n_sub)]
        m_tiles = []
        for j in range(n_sub):
            m_j = jnp.maximum(m[:, lanes(j)], jnp.max(causal[j], axis=0, keepdims=True))
            for d in range(j):
                m_j = jnp.maximum(m_j, cm_ref[d, :, lanes(j)])
            m_tiles.append(m_j)
        m_new = jnp.concatenate(m_tiles, axis=1)
        rows = []
        for d in range(n_sub):
            tiles = [jnp.zeros((tk, d * tk), BF16)] if d else []
            tiles.append(jnp.exp2(causal[d] - m_tiles[d]).astype(BF16))
            if d + 1 < n_sub:
                rest = slice((d + 1) * tk, tq)
                tiles.append(jnp.exp2(s_ref[d, :, rest] - m_new[:, rest]).astype(BF16))
            rows.append(jnp.concatenate(tiles, axis=1))
        p = jnp.concatenate(rows, axis=0)
        v = jnp.concatenate([v_ref[0, 0, i * n_sub + d] for d in range(n_sub)], axis=1)
        acc = jnp.exp2(m - m_new) * acc + _dot(v, p)
        o_ref[0] = acc[:HEAD_DIM] / acc[HEAD_DIM:HEAD_DIM + 1]

    g0 = g0_ref[(pl.program_id(0) * N_HEADS + pl.program_id(1)) * pl.num_programs(2) + i]
    n_full = i - g0

    def pair(p, carry):
        g = g0 + 2 * p
        carry = consume(g, sa_ref, cma_ref, carry, nxt=(g + 1, sb_ref, cmb_ref))
        return consume(g + 1, sb_ref, cmb_ref, carry, nxt=(g + 2, sa_ref, cma_ref))

    scores(g0, sa_ref, cma_ref)
    init = (jnp.full((1, tq), -jnp.inf, F32), jnp.zeros((V_ROWS, tq), F32))
    carry = lax.fori_loop(0, n_full // 2, pair, init)

    @pl.when(n_full % 2 == 1)
    def _():
        consume_diag(sb_ref, cmb_ref, consume(i - 1, sa_ref, cma_ref, carry, nxt=(i, sb_ref, cmb_ref)))

    @pl.when(n_full % 2 == 0)
    def _():
        consume_diag(sa_ref, cma_ref, carry)


def _first_groups(kstat, qstat, *, tm, tq):
    per = tq // tm
    bound = jnp.sqrt(jnp.max(kstat[:, :, 2, :N_HEADS], axis=1) * jnp.max(qstat[:, :, :, 0], axis=1))
    bound = bound * NORM_SLACK
    cq = kstat[:, ::per, 0, :N_HEADS]
    ck = kstat[:, :, 1, :N_HEADS]
    d = (cq[:, :, None, :] - ck[:, None, :, :]) * LOG2E
    nq, ng = cq.shape[1], ck.shape[1]
    earlier = per * jnp.arange(nq)[None, :, None, None] > jnp.arange(ng)[None, None, :, None]
    skip = earlier & (2.0 * bound[:, None, None, :] + d < -SKIP_LOG2)
    g0 = jnp.sum(jnp.cumprod(skip.astype(jnp.int32), axis=2), axis=2)
    return jnp.transpose(g0, (0, 2, 1)).reshape(-1).astype(jnp.int32), bound


def _attention(g0, bound, qpT, kp, vT, *, batch, seq, tm, tq, tk):
    nk = seq // tk
    params = pltpu.CompilerParams(
        dimension_semantics=("arbitrary", "arbitrary", "arbitrary"), vmem_limit_bytes=VMEM_LIMIT)
    out_shape = jax.ShapeDtypeStruct((batch, D_ATT, seq), F32)
    qkv_specs = [
        pl.BlockSpec((1, 1, KAUG, tq), lambda b, h, i, g0: (b, h, 0, i)),
        pl.BlockSpec((1, 1, nk, tk, KAUG), lambda b, h, i, g0: (b, h, 0, 0, 0)),
        pl.BlockSpec((1, 1, nk, V_ROWS, tk), lambda b, h, i, g0: (b, h, 0, 0, 0)),
    ]
    out_spec = pl.BlockSpec((1, HEAD_DIM, tq), lambda b, h, i, g0: (b, h, i))

    def fixed_ref(g0, bnd, qpT, kp, vT):
        grid_spec = pltpu.PrefetchScalarGridSpec(
            num_scalar_prefetch=1, grid=(batch, N_HEADS, seq // tq),
            in_specs=[pl.BlockSpec((1, 1, SUBLANES, LANES), lambda b, h, i, g0: (b, h, 0, 0))] + qkv_specs,
            out_specs=out_spec)
        return pl.pallas_call(functools.partial(_attn_fixed_ref_kernel, tq=tq, tk=tk, unit=tm // tk),
                              grid_spec=grid_spec, out_shape=out_shape, compiler_params=params,
                              name="attention_fixed_ref")(g0, bnd, qpT, kp, vT)

    def running_max(g0, bnd, qpT, kp, vT):
        del bnd
        grid_spec = pltpu.PrefetchScalarGridSpec(
            num_scalar_prefetch=1, grid=(batch, N_HEADS, seq // tq), in_specs=qkv_specs, out_specs=out_spec,
            scratch_shapes=[pltpu.VMEM((tq // tk, tk, tq), F32), pltpu.VMEM((tq // tk, tk, tq), F32),
                            pltpu.VMEM((tq // tk, 1, tq), F32), pltpu.VMEM((tq // tk, 1, tq), F32)])
        return pl.pallas_call(functools.partial(_attn_kernel, tq=tq, tk=tk), grid_spec=grid_spec,
                              out_shape=out_shape, compiler_params=params, name="attention",
                              )(g0 // (tq // tm), qpT, kp, vT)

    bnd = jnp.broadcast_to(bound[:, :, None, None], (batch, N_HEADS, SUBLANES, LANES))
    return lax.cond(jnp.max(bound) <= FIXED_REF_MAX_BOUND, fixed_ref, running_max, g0, bnd, qpT, kp, vT)


def _outproj_kernel(x_ref, a_ref, yT_ref, wdw_ref, bdw_ref, lng_ref, lnb_ref, gc_ref, wo_ref, ga_ref,
                    g2_ref, wrT_ref, brT_ref, utri_ref, x1_ref, h2_ref, route_ref, cnt_ref,
                    carry_ref, sh_ref, acc_ref, *, tm, rc):
    first = (pl.program_id(0) == 0) & (pl.program_id(1) == 0)

    @pl.when(first)
    def _():
        carry_ref[...] = jnp.zeros_like(carry_ref)

    mc = _conv_tile(a_ref, wdw_ref, bdw_ref, lng_ref, lnb_ref, gc_ref, sh_ref, acc_ref, tc=tm, rc=rc)
    yT = yT_ref[0]
    msa = jnp.mean(yT * yT, axis=0, keepdims=True)
    yn = (yT * lax.rsqrt(msa + EPS)).T * ga_ref[...]
    x1 = x_ref[...] + _dot(jnp.concatenate([mc, yn.astype(BF16)], axis=1), wo_ref[...])
    x1_ref[...] = x1
    ms = jnp.mean(x1 * x1, axis=-1, keepdims=True)
    h2 = (x1 * lax.rsqrt(ms + EPS)) * g2_ref[...]
    h2_ref[:, :D_MODEL] = h2

    lg = _dot_nt(wrT_ref[...], h2.astype(BF16)) + brT_ref[...]
    neg = -jnp.inf
    sub = lax.broadcasted_iota(jnp.int32, (SUBLANES, tm), 0)
    big = jnp.int32(SUBLANES)
    first_of = lambda hit: jnp.min(jnp.where(hit, sub, big), axis=0, keepdims=True)

    lg1 = lg[0:SUBLANES]
    m1 = jnp.max(lg1, axis=0, keepdims=True)
    p1_sel = 1.0 / jnp.sum(jnp.exp(lg1 - m1), axis=0, keepdims=True)
    grp = first_of(lg1 == m1)

    slab = lambda g: lg[SUBLANES * (g + 1):SUBLANES * (g + 2)]
    v = slab(N_GROUPS - 1)
    for g in range(N_GROUPS - 2, -1, -1):
        v = jnp.where(grp == g, slab(g), v)
    v1 = jnp.max(v, axis=0, keepdims=True)
    j1 = first_of(v == v1)
    vv = jnp.where(sub == j1, neg, v)
    v2 = jnp.max(vv, axis=0, keepdims=True)
    j2 = first_of(vv == v2)
    e21 = jnp.exp(v2 - v1)
    w0 = p1_sel / (1.0 + e21)
    w1 = p1_sel * e21 / (1.0 + e21)

    swap = j2 < j1
    al = jnp.where(swap, j2, j1)
    bl = jnp.where(swap, j1, j2)
    wa = jnp.where(swap, w1, w0)
    wb = jnp.where(swap, w0, w1)
    cid = PAIRS_PER_GROUP * grp + ((al * (2 * EXPERTS_PER_GROUP - 1 - al)) >> 1) + (bl - al - 1)

    cls = lax.broadcasted_iota(jnp.int32, (LANES, tm), 0)
    oh = cls == cid
    cmat = jnp.where(oh, 1.0, 0.0)
    prefix = _dot(cmat.astype(BF16), utri_ref[...]) + carry_ref[...]
    rank = jnp.sum(jnp.where(oh, prefix, 0.0), axis=0, keepdims=True)
    counts = prefix[:, tm - 1:tm] + cmat[:, tm - 1:tm]
    carry_ref[...] = counts
    cnt_ref[...] = counts

    h2_ref[:, D_MODEL:] = jnp.where(cls == 0, wa, jnp.where(cls == 1, wb, 0.0)).T
    route_ref[...] = jnp.where(sub == 0, cid.astype(F32), jnp.where(sub == 1, rank, 0.0))


def _outproj(x2, a, yT, conv_params, wo, ga, g2, wrT, brT, *, batch, seq, tm, rc):
    nt = seq // tm
    utri = jnp.triu(jnp.ones((tm, tm), F32), 1).astype(BF16)
    const = lambda shape: pl.BlockSpec(shape, lambda b, t: (0,) * len(shape))
    row_spec = lambda w: pl.BlockSpec((tm, w), lambda b, t: (b * nt + t, 0))
    return pl.pallas_call(
        functools.partial(_outproj_kernel, tm=tm, rc=rc),
        grid=(batch, nt),
        in_specs=[row_spec(D_MODEL), row_spec(D_CONV),
                  pl.BlockSpec((1, D_ATT, tm), lambda b, t: (b, 0, t)),
                  const((CONV_WIDTH, D_CONV)), const((1, D_CONV)), const((1, D_CONV)), const((1, D_CONV)),
                  const((1, D_CONV)),
                  const((D_CONV + D_ATT, D_MODEL)), const((1, D_ATT)), const((1, D_MODEL)),
                  const((LANES, D_MODEL)), const((LANES, 1)), const((tm, tm))],
        out_specs=[row_spec(D_MODEL), row_spec(ROW_W),
                   pl.BlockSpec((SUBLANES, tm), lambda b, t: (0, b * nt + t)), const((LANES, 1))],
        out_shape=[jax.ShapeDtypeStruct((batch * seq, D_MODEL), F32),
                   jax.ShapeDtypeStruct((batch * seq, ROW_W), F32),
                   jax.ShapeDtypeStruct((SUBLANES, batch * seq), F32),
                   jax.ShapeDtypeStruct((LANES, 1), F32)],
        scratch_shapes=[pltpu.VMEM((LANES, 1), F32),
                        pltpu.VMEM((SUBLANES, D_CONV // LANES, tm + CONV_HALO, LANES), F32),
                        pltpu.VMEM((tm, D_CONV), F32)],
        compiler_params=pltpu.CompilerParams(
            dimension_semantics=("arbitrary", "arbitrary"), vmem_limit_bytes=VMEM_LIMIT),
        name="outproj",
    )(x2, a, yT, *conv_params, wo, ga, g2, wrT, brT, utri)


def _row_copies(src_at, dst_at, sem):
    return pltpu.make_async_copy(src_at, dst_at, sem)


def _dispatch_kernel(dest_ref, h2_ref, xs_ref, sem, *, tm):
    def body(g, carry):
        for u in range(SUBLANES):
            _row_copies(h2_ref.at[g, pl.ds(u, 1)], xs_ref.at[pl.ds(dest_ref[g * SUBLANES + u], 1)],
                        sem).start(priority=u % 2)
        return carry

    lax.fori_loop(0, tm // SUBLANES, body, 0)
    done = xs_ref.at[pl.ds(0, tm)]
    _row_copies(done, done, sem).wait()


def _dispatch(dest, h2, *, tm):
    t_len = h2.shape[0]
    return pl.pallas_call(
        functools.partial(_dispatch_kernel, tm=tm),
        grid=(t_len // tm,),
        in_specs=[pl.BlockSpec((tm,), lambda i: (i,), memory_space=pltpu.SMEM),
                  pl.BlockSpec((tm // SUBLANES, SUBLANES, ROW_W), lambda i: (i, 0, 0))],
        out_specs=pl.BlockSpec(memory_space=pl.ANY),
        out_shape=jax.ShapeDtypeStruct((t_len, ROW_W), F32),
        scratch_shapes=[pltpu.SemaphoreType.DMA(())],
        compiler_params=pltpu.CompilerParams(dimension_semantics=("arbitrary",), vmem_limit_bytes=VMEM_LIMIT),
        name="dispatch",
    )(dest, h2.reshape(t_len // SUBLANES, SUBLANES, ROW_W))


def _expert_kernel(blk_ref, ea_ref, eb_ref, lo_ref, hi_ref, x_ref, wgua_ref, wda_ref, wgub_ref, wdb_ref, y_ref):
    del blk_ref, ea_ref, eb_ref
    i = pl.program_id(0)
    lo = lo_ref[i]
    hi = hi_ref[i]
    n_seg = ROW_BLOCK // ROW_SEGMENT

    def mlp(xb, wgu_ref, wd_ref):
        gu = _dot(xb, wgu_ref[0])
        g = gu[:, :D_EXPERT]
        return _dot(((g * jax.nn.sigmoid(g)) * gu[:, D_EXPERT:]).astype(BF16), wd_ref[0])

    def run(rows):
        n = rows.stop - rows.start
        xb = x_ref[rows, :D_MODEL].astype(BF16)
        wts = x_ref[rows, D_MODEL:]
        y = wts[:, 0:1] * mlp(xb, wgua_ref, wda_ref) + wts[:, 1:2] * mlp(xb, wgub_ref, wdb_ref)
        row = rows.start + lax.broadcasted_iota(jnp.int32, (n, 1), 0)
        mine = (row >= lo) & (row < hi)

        @pl.when(lo == 0)
        def _():
            y_ref[rows, :] = jnp.where(mine, y, 0.0)
            if rows.stop < ROW_BLOCK:
                y_ref[rows.stop:, :] = jnp.zeros((ROW_BLOCK - rows.stop, D_MODEL), F32)

        @pl.when(lo > 0)
        def _():
            y_ref[rows, :] = jnp.where(mine, y, y_ref[rows, :])

    first = lo // ROW_SEGMENT
    last = (hi - 1) // ROW_SEGMENT
    for s0 in range(n_seg):
        for s1 in range(s0, n_seg):
            @pl.when((hi > lo) & (first == s0) & (last == s1))
            def _(s0=s0, s1=s1):
                run(slice(s0 * ROW_SEGMENT, (s1 + 1) * ROW_SEGMENT))


def _experts(items, xs, wgu, wd):
    n_rows = xs.shape[0]
    n_items = items[0].shape[0]
    wspec = lambda shape, which: pl.BlockSpec(
        shape, lambda i, blk, ea, eb, lo, hi: ((ea, eb)[which][i], 0, 0))
    grid_spec = pltpu.PrefetchScalarGridSpec(
        num_scalar_prefetch=5,
        grid=(n_items,),
        in_specs=[pl.BlockSpec((ROW_BLOCK, ROW_W), lambda i, blk, ea, eb, lo, hi: (blk[i], 0)),
                  wspec((1, D_MODEL, 2 * D_EXPERT), 0), wspec((1, D_EXPERT, D_MODEL), 0),
                  wspec((1, D_MODEL, 2 * D_EXPERT), 1), wspec((1, D_EXPERT, D_MODEL), 1)],
        out_specs=pl.BlockSpec((ROW_BLOCK, D_MODEL), lambda i, blk, ea, eb, lo, hi: (blk[i], 0)),
    )
    return pl.pallas_call(
        _expert_kernel,
        grid_spec=grid_spec,
        out_shape=jax.ShapeDtypeStruct((n_rows, D_MODEL), F32),
        compiler_params=pltpu.CompilerParams(dimension_semantics=("arbitrary",), vmem_limit_bytes=VMEM_LIMIT),
        name="experts",
    )(*items, xs, wgu, wd, wgu, wd)


def _combine_kernel(dcur_ref, dnxt_ref, x1_ref, g_ref, ys_ref, out_ref, buf_ref, sems, *, tm):
    i = pl.program_id(0)
    slot = i % 2

    def issue(d_ref, s):
        def body(g, carry):
            for u in range(SUBLANES):
                _row_copies(ys_ref.at[pl.ds(d_ref[g * SUBLANES + u], 1)], buf_ref.at[s, g, pl.ds(u, 1)],
                            sems.at[s]).start(priority=u % 2)
            return carry

        lax.fori_loop(0, tm // SUBLANES, body, 0)

    @pl.when(i == 0)
    def _():
        issue(dcur_ref, 0)

    @pl.when(i + 1 < pl.num_programs(0))
    def _():
        issue(dnxt_ref, 1 - slot)

    _row_copies(buf_ref.at[slot], buf_ref.at[slot], sems.at[slot]).wait()
    x2 = x1_ref[...] + buf_ref[slot]
    ms = jnp.mean(x2 * x2, axis=-1, keepdims=True)
    out_ref[...] = (x2 * lax.rsqrt(ms + EPS)) * g_ref[...]


def _combine(dest, x1, gf, ys, *, tm):
    t_len = x1.shape[0]
    n = t_len // tm
    return pl.pallas_call(
        functools.partial(_combine_kernel, tm=tm),
        grid=(n,),
        in_specs=[pl.BlockSpec((tm,), lambda i: (i,), memory_space=pltpu.SMEM),
                  pl.BlockSpec((tm,), lambda i: (jnp.minimum(i + 1, n - 1),), memory_space=pltpu.SMEM),
                  pl.BlockSpec((tm // SUBLANES, SUBLANES, D_MODEL), lambda i: (i, 0, 0)),
                  pl.BlockSpec((1, D_MODEL), lambda i: (0, 0)),
                  pl.BlockSpec(memory_space=pl.ANY)],
        out_specs=pl.BlockSpec((tm // SUBLANES, SUBLANES, D_MODEL), lambda i: (i, 0, 0)),
        out_shape=jax.ShapeDtypeStruct((t_len // SUBLANES, SUBLANES, D_MODEL), F32),
        scratch_shapes=[pltpu.VMEM((2, tm // SUBLANES, SUBLANES, D_MODEL), F32), pltpu.SemaphoreType.DMA((2,))],
        compiler_params=pltpu.CompilerParams(dimension_semantics=("arbitrary",), vmem_limit_bytes=VMEM_LIMIT),
        name="combine",
    )(dest, dest, x1.reshape(t_len // SUBLANES, SUBLANES, D_MODEL), gf, ys)


def _layer(x, norm1_g, w_in, b_f, w_dw, b_dw, conv_ln_g, conv_ln_b, out_g_conv, out_g_att, w_out,
           norm2_g, w_r1, b_r1, w_r2, b_r2, w_gate, w_up, w_down):
    batch, seq, d = x.shape
    t_len = batch * seq
    tm = min(512, seq)
    tk = min(256, seq)
    tq = min(1024, seq)
    o2 = 2 * D_CONV
    o3, o4, o5 = o2 + D_ATT, o2 + 2 * D_ATT, o2 + 3 * D_ATT

    assert PIECE_STRIDE == N_HEADS
    pad = LANES - N_PIECES * PIECE_STRIDE
    wf = jnp.pad(jnp.tile(w_in[:, o5:], (1, N_PIECES)), ((0, 0), (0, pad)))
    wrow = jnp.concatenate([w_in[:, :o2], w_in[:, o3:o4], wf], axis=1).astype(BF16)
    wqvT = jnp.concatenate([w_in[:, o2:o3] * (HEAD_DIM ** -0.5), w_in[:, o4:o5]], axis=1).T.astype(BF16)
    bf3 = jnp.pad(jnp.tile(b_f.astype(F32).reshape(1, N_HEADS), (1, N_PIECES)), ((0, 0), (0, pad)))

    x2 = x.reshape(t_len, d)
    a, kp, qpT, vT, kstat, qstat = _inproj(x2, norm1_g.reshape(1, d), wrow, bf3, wqvT,
                                           batch=batch, seq=seq, tm=tm, tk=tk)
    conv_params = (w_dw, b_dw.reshape(1, -1), conv_ln_g.reshape(1, -1), conv_ln_b.reshape(1, -1),
                   out_g_conv.reshape(1, -1))
    g0, bound = _first_groups(kstat, qstat, tm=tm, tq=tq)
    yT = _attention(g0, bound, qpT, kp, vT, batch=batch, seq=seq, tm=tm, tq=tq, tk=tk)

    gpad = SUBLANES - N_GROUPS
    rpad = LANES - SUBLANES - N_EXPERTS
    wrT = jnp.concatenate([w_r1.T, jnp.zeros((gpad, d), F32),
                           jnp.transpose(w_r2, (0, 2, 1)).reshape(N_EXPERTS, d), jnp.zeros((rpad, d), F32)], axis=0)
    brT = jnp.concatenate([b_r1.astype(F32), jnp.full((gpad,), -jnp.inf, F32), b_r2.reshape(-1).astype(F32),
                           jnp.zeros((rpad,), F32)]).reshape(LANES, 1)
    x1, h2, route, cnt = _outproj(x2, a, yT, conv_params, w_out.astype(BF16),
                                  out_g_att.reshape(1, -1), norm2_g.reshape(1, d),
                                  wrT.astype(BF16), brT, batch=batch, seq=seq, tm=tm, rc=min(256, seq))

    i32 = jnp.int32
    lanes = jnp.arange(LANES, dtype=i32)
    pick = lambda table, idx: jnp.sum(jnp.where(idx[:, None] == lanes, table[None, :], 0), axis=1).astype(i32)
    counts = cnt[:, 0].astype(i32)
    ends = jnp.cumsum(counts).astype(i32)
    starts = ends - counts
    dest = pick(starts, route[0].astype(i32)) + route[1].astype(i32)

    n_blocks = t_len // ROW_BLOCK
    b_lo = starts // ROW_BLOCK
    n_it = jnp.where(counts > 0, (ends - 1) // ROW_BLOCK - b_lo + 1, 0)
    it_end = jnp.cumsum(n_it).astype(i32)
    it_start = it_end - n_it
    idx = jnp.arange(n_blocks + N_CLASSES, dtype=i32)
    valid = idx < it_end[-1]
    last_cls = jnp.max(jnp.where(counts > 0, lanes, 0))
    cls = jnp.where(valid, jnp.sum(it_end[None, :] <= idx[:, None], axis=1).astype(i32), last_cls)
    blk = jnp.where(valid, pick(b_lo, cls) + idx - pick(it_start, cls), n_blocks - 1)
    row0 = blk * ROW_BLOCK
    lo = jnp.where(valid, jnp.maximum(pick(starts, cls), row0) - row0, 0)
    hi = jnp.where(valid, jnp.minimum(pick(ends, cls), row0 + ROW_BLOCK) - row0, 0)
    pair_a, pair_b = [], []
    for g in range(N_GROUPS):
        for a_loc in range(EXPERTS_PER_GROUP):
            for b_loc in range(a_loc + 1, EXPERTS_PER_GROUP):
                pair_a.append(g * EXPERTS_PER_GROUP + a_loc)
                pair_b.append(g * EXPERTS_PER_GROUP + b_loc)
    cpad = [0] * (LANES - N_CLASSES)
    items = (blk.astype(i32), pick(jnp.array(pair_a + cpad, i32), cls), pick(jnp.array(pair_b + cpad, i32), cls),
             lo.astype(i32), hi.astype(i32))

    xs = _dispatch(dest, h2, tm=min(4096, t_len))
    ys = _experts(items, xs, jnp.concatenate([w_gate, w_up], axis=-1).astype(BF16), w_down.astype(BF16))
    return dest, x1, ys


def kernel(x, norm1_g, w_in, b_f, w_dw, b_dw, conv_ln_g, conv_ln_b, out_g_conv, out_g_att, w_out, norm2_g,
           w_r1, b_r1, w_r2, b_r2, w_gate, w_up, w_down, final_g):
    assert norm1_g.shape[0] == 1, "single-layer stack"
    batch, seq, d = x.shape
    dest, x1, ys = _layer(
        x, norm1_g[0], w_in[0], b_f[0], w_dw[0], b_dw[0], conv_ln_g[0], conv_ln_b[0], out_g_conv[0],
        out_g_att[0], w_out[0], norm2_g[0], w_r1[0], b_r1[0], w_r2[0], b_r2[0], w_gate[0], w_up[0], w_down[0])
    out = _combine(dest, x1, final_g.reshape(1, d), ys, tm=min(512, batch * seq))
    return out.reshape(batch, seq, d)
```

```python
import functools

import jax
import jax.numpy as jnp
from jax import lax
from jax.experimental import pallas as pl
from jax.experimental.pallas import tpu as pltpu

D_MODEL = 1024
D_CONV = 512
N_HEADS = 8
HEAD_DIM = 64
D_ATT = N_HEADS * HEAD_DIM
CONV_WIDTH = 31
N_GROUPS = 4
EXPERTS_PER_GROUP = 8
N_EXPERTS = N_GROUPS * EXPERTS_PER_GROUP
D_EXPERT = D_MODEL // 4
ROW_BLOCK = 1024
ROW_SEGMENT = 128
EPS = 1e-6

LANES = 128
SUBLANES = 8
KAUG = 128
N_PIECES = 3
PIECE_STRIDE = 8
CONV_HALO = 32
PAIRS_PER_GROUP = EXPERTS_PER_GROUP * (EXPERTS_PER_GROUP - 1) // 2
N_CLASSES = N_GROUPS * PAIRS_PER_GROUP
ROW_W = D_MODEL + LANES
SKIP_LOG2 = 160.0
FIXED_REF_MAX_BOUND = 40.0
GROUPS_PER_TRIP = 4
NORM_SLACK = 1.02
V_ROWS = 80
LOG2E = 1.4426950408889634
VMEM_LIMIT = 56 * 1024 * 1024

F32 = jnp.float32
BF16 = jnp.bfloat16


def _dot(a, b):
    return jnp.dot(a, b, preferred_element_type=F32)


def _dot_nt(a, b):
    return lax.dot_general(a, b, (((1,), (1,)), ((), ())), preferred_element_type=F32)


def _split3(x):
    hi = x.astype(BF16)
    r1 = x - hi.astype(F32)
    mid = r1.astype(BF16)
    lo = (r1 - mid.astype(F32)).astype(BF16)
    return hi.astype(F32), mid.astype(F32), lo.astype(F32)


def _piece_lane_mask(lane, h):
    return (lane == h) | (lane == h + PIECE_STRIDE) | (lane == h + 2 * PIECE_STRIDE)


def _inproj_kernel(x_ref, g1_ref, wrow_ref, bf_ref, wqvT_ref, ltri_ref, hsel_ref,
                   a_ref, kp_ref, qpT_ref, vT_ref, kstat_ref, qstat_ref, carry_ref, *, tm, tk):
    @pl.when(pl.program_id(1) == 0)
    def _():
        carry_ref[...] = jnp.zeros_like(carry_ref)

    x = x_ref[...]
    ms = jnp.mean(x * x, axis=-1, keepdims=True)
    hb = ((x * lax.rsqrt(ms + EPS)) * g1_ref[...]).astype(BF16)

    z = _dot(hb, wrow_ref[...])
    a_ref[...] = z[:, :D_CONV] * jax.nn.sigmoid(z[:, D_CONV:2 * D_CONV])

    kk = z[:, 2 * D_CONV:2 * D_CONV + D_ATT]

    zf = z[:, 2 * D_CONV + D_ATT:] + bf_ref[...]
    lf = jnp.minimum(zf, 0.0) - jnp.log1p(jnp.exp(-jnp.abs(zf)))
    lane = lax.broadcasted_iota(jnp.int32, (tm, LANES), 1)
    hi, mid, lo = _split3(lf)
    lf3 = jnp.where(lane < PIECE_STRIDE, hi,
                    jnp.where(lane < 2 * PIECE_STRIDE, mid,
                              jnp.where(lane < 3 * PIECE_STRIDE, lo, 0.0))).astype(BF16)
    cs3 = _dot(ltri_ref[...], lf3)
    c = (cs3 + pltpu.roll(cs3, LANES - PIECE_STRIDE, 1)
         + pltpu.roll(cs3, LANES - 2 * PIECE_STRIDE, 1)) + carry_ref[...]
    carry_ref[...] = c[tm - 1:tm, :]

    nhi, nmid, nlo = _split3(c * (-LOG2E))
    p3 = jnp.where(lane < PIECE_STRIDE, nhi,
                   jnp.where(lane < 2 * PIECE_STRIDE, pltpu.roll(nmid, PIECE_STRIDE, 1),
                             jnp.where(lane < 3 * PIECE_STRIDE, pltpu.roll(nlo, 2 * PIECE_STRIDE, 1), 0.0)))
    p3_hi = pltpu.roll(p3, HEAD_DIM, 1)
    q_pieces = (-p3).T[0:HEAD_DIM]
    q_shift = N_PIECES * PIECE_STRIDE

    qvT = _dot_nt(wqvT_ref[...], hb)
    qT = qvT[:D_ATT] * LOG2E
    vT = qvT[D_ATT:]
    row = lax.broadcasted_iota(jnp.int32, (HEAD_DIM, tm), 0)
    vrow = lax.broadcasted_iota(jnp.int32, (V_ROWS - HEAD_DIM, tk), 0)
    v_tail = jnp.where(vrow == 0, 1.0, 0.0).astype(BF16)

    for h in range(N_HEADS):
        kcol = kk[:, (h // 2) * LANES:(h // 2 + 1) * LANES]
        f_h = jnp.where(_piece_lane_mask(row, h), q_pieces, 0.0)
        aug_q = jnp.where(_piece_lane_mask(row, h), 1.0, 0.0) + jnp.concatenate(
            [jnp.zeros((q_shift, tm), F32), f_h[0:HEAD_DIM - q_shift]], axis=0)
        q_h = qT[h * HEAD_DIM:(h + 1) * HEAD_DIM, :]
        if h % 2 == 0:
            ext = jnp.where(_piece_lane_mask(lane, h + HEAD_DIM), p3_hi,
                            jnp.where(_piece_lane_mask(lane, h + HEAD_DIM + q_shift), 1.0, 0.0))
            kp = jnp.where(lane < HEAD_DIM, kcol, ext)
            qp = jnp.concatenate([q_h, aug_q], axis=0)
        else:
            ext = jnp.where(_piece_lane_mask(lane, h), p3,
                            jnp.where(_piece_lane_mask(lane, h + q_shift), 1.0, 0.0))
            kp = jnp.where(lane >= HEAD_DIM, kcol, ext)
            qp = jnp.concatenate([aug_q, q_h], axis=0)
        qpT_ref[0, h] = qp.astype(BF16)
        for cidx in range(tm // tk):
            kp_ref[0, h, cidx] = kp[cidx * tk:(cidx + 1) * tk, :].astype(BF16)
            v_h = vT[h * HEAD_DIM:(h + 1) * HEAD_DIM, cidx * tk:(cidx + 1) * tk].astype(BF16)
            vT_ref[0, h, cidx] = jnp.concatenate([v_h, v_tail], axis=0)

    kn2 = jnp.max(_dot((kk * kk).astype(BF16), hsel_ref[...]), axis=0, keepdims=True)
    srow = lax.broadcasted_iota(jnp.int32, (SUBLANES, LANES), 0)
    kstat_ref[0, 0] = jnp.where(srow == 0, c[0:1, :], jnp.where(srow == 1, c[tm - 1:tm, :],
                                                                 jnp.where(srow == 2, kn2, 0.0)))
    q2 = qT * qT
    qn2 = jnp.concatenate([jnp.sum(q2[h * HEAD_DIM:(h + 1) * HEAD_DIM, :], axis=0, keepdims=True)
                           for h in range(N_HEADS)], axis=0)
    qstat_ref[0, 0] = jnp.broadcast_to(jnp.max(qn2, axis=1, keepdims=True), (N_HEADS, LANES))


def _inproj(x2, g1, wrow, bf3, wqvT, *, batch, seq, tm, tk):
    nt = seq // tm
    nk = seq // tk
    ltri = jnp.tril(jnp.ones((tm, tm), F32)).astype(BF16)
    hsel = (jnp.arange(D_ATT)[:, None] // HEAD_DIM == jnp.arange(LANES)[None, :]).astype(BF16)
    const = lambda shape: pl.BlockSpec(shape, lambda b, t: (0,) * len(shape))
    stat_spec = pl.BlockSpec((1, 1, SUBLANES, LANES), lambda b, t: (b, t, 0, 0))
    stat_shape = jax.ShapeDtypeStruct((batch, nt, SUBLANES, LANES), F32)
    return pl.pallas_call(
        functools.partial(_inproj_kernel, tm=tm, tk=tk),
        grid=(batch, nt),
        in_specs=[
            pl.BlockSpec((tm, D_MODEL), lambda b, t: (b * nt + t, 0)),
            const((1, D_MODEL)), const((D_MODEL, 2 * D_CONV + D_ATT + LANES)), const((1, LANES)),
            const((2 * D_ATT, D_MODEL)), const((tm, tm)), const((D_ATT, LANES)),
        ],
        out_specs=[
            pl.BlockSpec((tm, D_CONV), lambda b, t: (b * nt + t, 0)),
            pl.BlockSpec((1, N_HEADS, tm // tk, tk, KAUG), lambda b, t: (b, 0, t, 0, 0)),
            pl.BlockSpec((1, N_HEADS, KAUG, tm), lambda b, t: (b, 0, 0, t)),
            pl.BlockSpec((1, N_HEADS, tm // tk, V_ROWS, tk), lambda b, t: (b, 0, t, 0, 0)),
            stat_spec, stat_spec,
        ],
        out_shape=[
            jax.ShapeDtypeStruct((batch * seq, D_CONV), F32),
            jax.ShapeDtypeStruct((batch, N_HEADS, nk, tk, KAUG), BF16),
            jax.ShapeDtypeStruct((batch, N_HEADS, KAUG, seq), BF16),
            jax.ShapeDtypeStruct((batch, N_HEADS, nk, V_ROWS, tk), BF16),
            stat_shape, stat_shape,
        ],
        scratch_shapes=[pltpu.VMEM((1, LANES), F32)],
        compiler_params=pltpu.CompilerParams(
            dimension_semantics=("arbitrary", "arbitrary"), vmem_limit_bytes=VMEM_LIMIT),
        name="inproj",
    )(x2, g1, wrow, bf3, wqvT, ltri, hsel)


def _conv_tile(a_ref, w_ref, b_ref, lng_ref, lnb_ref, og_ref, sh_ref, acc_ref, *, tc, rc):
    n_cb = D_CONV // LANES

    @pl.when(pl.program_id(1) == 0)
    def _():
        for cb in range(n_cb):
            sh_ref[0, cb, 0:CONV_HALO, :] = jnp.zeros((CONV_HALO, LANES), F32)

    @pl.when(pl.program_id(1) > 0)
    def _():
        for cb in range(n_cb):
            sh_ref[0, cb, 0:CONV_HALO, :] = sh_ref[0, cb, tc:tc + CONV_HALO, :]

    n_sh = tc + CONV_HALO - SUBLANES
    for cb in range(n_cb):
        sh_ref[0, cb, CONV_HALO:CONV_HALO + tc, :] = a_ref[:, cb * LANES:(cb + 1) * LANES]
        for f in range(1, SUBLANES):
            sh_ref[f, cb, 0:n_sh, :] = sh_ref[0, cb, f:f + n_sh, :]

    base = CONV_HALO - (CONV_WIDTH - 1)
    for cb in range(n_cb):
        cols = slice(cb * LANES, (cb + 1) * LANES)

        def chunk(c, carry, cb=cb, cols=cols):
            r0 = pl.multiple_of(c * rc, rc)
            acc = jnp.zeros((rc, LANES), F32)
            for j in range(CONV_WIDTH):
                f = (base + j) % SUBLANES
                acc = acc + w_ref[j:j + 1, cols] * sh_ref[f, cb, pl.ds(r0 + (base + j - f), rc), :]
            acc_ref[pl.ds(r0, rc), cols] = acc
            return carry

        lax.fori_loop(0, tc // rc, chunk, 0)

    y = acc_ref[...] + b_ref[...]
    mu = jnp.mean(y, axis=-1, keepdims=True)
    yc = y - mu
    var = jnp.mean(yc * yc, axis=-1, keepdims=True)
    yn = yc * lax.rsqrt(var + EPS) * lng_ref[...] + lnb_ref[...]
    s = yn * jax.nn.sigmoid(yn)
    ms = jnp.mean(s * s, axis=-1, keepdims=True)
    return (s * lax.rsqrt(ms + EPS) * og_ref[...]).astype(BF16)


def _attn_fixed_ref_kernel(g0_ref, bnd_ref, qT_ref, k_ref, v_ref, o_ref, *, tq, tk, unit):
    i = pl.program_id(2)
    qT = qT_ref[0, 0]
    n_sub = tq // tk
    bnd = bnd_ref[0, 0, 0:1, 0:1]
    sb0 = unit * g0_ref[(pl.program_id(0) * N_HEADS + pl.program_id(1)) * pl.num_programs(2) + i]

    def values(sb, n):
        return jnp.concatenate([v_ref[0, 0, sb + d] for d in range(n)], axis=1)

    def probs(sb, n):
        return jnp.concatenate([jnp.exp2(_dot(k_ref[0, 0, sb + d], qT) - bnd).astype(BF16)
                                for d in range(n)], axis=0)

    def accumulate(acc, sb, n):
        for lo in range(0, n, n_sub):
            m = min(n_sub, n - lo)
            acc = acc + _dot(values(sb + lo, m), probs(sb + lo, m))
        return acc

    def loop_trip(p, acc):
        return accumulate(acc, sb0 + GROUPS_PER_TRIP * n_sub * p, GROUPS_PER_TRIP * n_sub)

    key = lax.broadcasted_iota(jnp.int32, (tk, tk), 0)
    qry = lax.broadcasted_iota(jnp.int32, (tk, tk), 1)

    def diagonal(acc):
        rows = []
        for d in range(n_sub):
            s = _dot(k_ref[0, 0, i * n_sub + d], qT[:, d * tk:])
            tiles = [jnp.zeros((tk, d * tk), BF16)] if d else []
            tiles.append(jnp.exp2(jnp.where(key <= qry, s[:, :tk], -jnp.inf) - bnd).astype(BF16))
            if d + 1 < n_sub:
                tiles.append(jnp.exp2(s[:, tk:] - bnd).astype(BF16))
            rows.append(jnp.concatenate(tiles, axis=1))
        acc = acc + _dot(values(i * n_sub, n_sub), jnp.concatenate(rows, axis=0))
        o_ref[0] = acc[:HEAD_DIM] / acc[HEAD_DIM:HEAD_DIM + 1]

    n_before = i * n_sub - sb0
    per_trip = GROUPS_PER_TRIP * n_sub
    acc = lax.fori_loop(0, n_before // per_trip, loop_trip, jnp.zeros((V_ROWS, tq), F32))
    left = n_before % per_trip

    for n in range(0, per_trip, unit):
        @pl.when(left == n)
        def _(n=n):
            diagonal(accumulate(acc, i * n_sub - n, n))


def _attn_kernel(g0_ref, qT_ref, k_ref, v_ref, o_ref, sa_ref, sb_ref, cma_ref, cmb_ref, *, tq, tk):
    i = pl.program_id(2)
    qT = qT_ref[0, 0]
    n_sub = tq // tk

    def scores(g, s_ref, cm_ref):
        for d in range(n_sub):
            s = _dot(k_ref[0, 0, g * n_sub + d], qT)
            s_ref[d] = s
            cm_ref[d] = jnp.max(s, axis=0, keepdims=True)

    def consume(g, s_ref, cm_ref, carry, nxt=None):
        m, acc = carry
        m_new = m
        for d in range(n_sub):
            m_new = jnp.maximum(m_new, cm_ref[d])
        ps = []
        for d in range(n_sub):
            if nxt is not None:
                g_n, sn_ref, cmn_ref = nxt
                s = _dot(k_ref[0, 0, g_n * n_sub + d], qT)
                sn_ref[d] = s
                cmn_ref[d] = jnp.max(s, axis=0, keepdims=True)
            ps.append(jnp.exp2(s_ref[d] - m_new).astype(BF16))
        p = jnp.concatenate(ps, axis=0)
        v = jnp.concatenate([v_ref[0, 0, g * n_sub + d] for d in range(n_sub)], axis=1)
        return m_new, jnp.exp2(m - m_new) * acc + _dot(v, p)

    key = lax.broadcasted_iota(jnp.int32, (tk, tk), 0)
    qry = lax.broadcasted_iota(jnp.int32, (tk, tk), 1)

    def consume_diag(s_ref, cm_ref, carry):
        m, acc = carry
        lanes = lambda j: slice(j * tk, (j + 1) * tk)
        causal = [jnp.where(key <= qry, s_ref[d, :, lanes(d)], -jnp.inf) for d in range(n_sub)]
        m_tiles = []
        for j in range(n_sub):
            m_j = jnp.maximum(m[:, lanes(j)], jnp.max(causal[j], axis=0, keepdims=True))
            for d in range(j):
                m_j = jnp.maximum(m_j, cm_ref[d, :, lanes(j)])
            m_tiles.append(m_j)
        m_new = jnp.concatenate(m_tiles, axis=1)
        rows = []
        for d in range(n_sub):
            tiles = [jnp.zeros((tk, d * tk), BF16)] if d else []
            tiles.append(jnp.exp2(causal[d] - m_tiles[d]).astype(BF16))
            if d + 1 < n_sub:
                rest = slice((d + 1) * tk, tq)
                tiles.append(jnp.exp2(s_ref[d, :, rest] - m_new[:, rest]).astype(BF16))
            rows.append(jnp.concatenate(tiles, axis=1))
        p = jnp.concatenate(rows, axis=0)
        v = jnp.concatenate([v_ref[0, 0, i * n_sub + d] for d in range(n_sub)], axis=1)
        acc = jnp.exp2(m - m_new) * acc + _dot(v, p)
        o_ref[0] = acc[:HEAD_DIM] / acc[HEAD_DIM:HEAD_DIM + 1]

    g0 = g0_ref[(pl.program_id(0) * N_HEADS + pl.program_id(1)) * pl.num_programs(2) + i]
    n_full = i - g0

    def pair(p, carry):
        g = g0 + 2 * p
        carry = consume(g, sa_ref, cma_ref, carry, nxt=(g + 1, sb_ref, cmb_ref))
        return consume(g + 1, sb_ref, cmb_ref, carry, nxt=(g + 2, sa_ref, cma_ref))

    scores(g0, sa_ref, cma_ref)
    init = (jnp.full((1, tq), -jnp.inf, F32), jnp.zeros((V_ROWS, tq), F32))
    carry = lax.fori_loop(0, n_full // 2, pair, init)

    @pl.when(n_full % 2 == 1)
    def _():
        consume_diag(sb_ref, cmb_ref, consume(i - 1, sa_ref, cma_ref, carry, nxt=(i, sb_ref, cmb_ref)))

    @pl.when(n_full % 2 == 0)
    def _():
        consume_diag(sa_ref, cma_ref, carry)


def _first_groups(kstat, qstat, *, tm, tq):
    per = tq // tm
    bound = jnp.sqrt(jnp.max(kstat[:, :, 2, :N_HEADS], axis=1) * jnp.max(qstat[:, :, :, 0], axis=1))
    bound = bound * NORM_SLACK
    cq = kstat[:, ::per, 0, :N_HEADS]
    ck = kstat[:, :, 1, :N_HEADS]
    d = (cq[:, :, None, :] - ck[:, None, :, :]) * LOG2E
    nq, ng = cq.shape[1], ck.shape[1]
    earlier = per * jnp.arange(nq)[None, :, None, None] > jnp.arange(ng)[None, None, :, None]
    skip = earlier & (2.0 * bound[:, None, None, :] + d < -SKIP_LOG2)
    g0 = jnp.sum(jnp.cumprod(skip.astype(jnp.int32), axis=2), axis=2)
    return jnp.transpose(g0, (0, 2, 1)).reshape(-1).astype(jnp.int32), bound


def _attention(g0, bound, qpT, kp, vT, *, batch, seq, tm, tq, tk):
    nk = seq // tk
    params = pltpu.CompilerParams(
        dimension_semantics=("arbitrary", "arbitrary", "arbitrary"), vmem_limit_bytes=VMEM_LIMIT)
    out_shape = jax.ShapeDtypeStruct((batch, D_ATT, seq), F32)
    qkv_specs = [
        pl.BlockSpec((1, 1, KAUG, tq), lambda b, h, i, g0: (b, h, 0, i)),
        pl.BlockSpec((1, 1, nk, tk, KAUG), lambda b, h, i, g0: (b, h, 0, 0, 0)),
        pl.BlockSpec((1, 1, nk, V_ROWS, tk), lambda b, h, i, g0: (b, h, 0, 0, 0)),
    ]
    out_spec = pl.BlockSpec((1, HEAD_DIM, tq), lambda b, h, i, g0: (b, h, i))

    def fixed_ref(g0, bnd, qpT, kp, vT):
        grid_spec = pltpu.PrefetchScalarGridSpec(
            num_scalar_prefetch=1, grid=(batch, N_HEADS, seq // tq),
            in_specs=[pl.BlockSpec((1, 1, SUBLANES, LANES), lambda b, h, i, g0: (b, h, 0, 0))] + qkv_specs,
            out_specs=out_spec)
        return pl.pallas_call(functools.partial(_attn_fixed_ref_kernel, tq=tq, tk=tk, unit=tm // tk),
                              grid_spec=grid_spec, out_shape=out_shape, compiler_params=params,
                              name="attention_fixed_ref")(g0, bnd, qpT, kp, vT)

    def running_max(g0, bnd, qpT, kp, vT):
        del bnd
        grid_spec = pltpu.PrefetchScalarGridSpec(
            num_scalar_prefetch=1, grid=(batch, N_HEADS, seq // tq), in_specs=qkv_specs, out_specs=out_spec,
            scratch_shapes=[pltpu.VMEM((tq // tk, tk, tq), F32), pltpu.VMEM((tq // tk, tk, tq), F32),
                            pltpu.VMEM((tq // tk, 1, tq), F32), pltpu.VMEM((tq // tk, 1, tq), F32)])
        return pl.pallas_call(functools.partial(_attn_kernel, tq=tq, tk=tk), grid_spec=grid_spec,
                              out_shape=out_shape, compiler_params=params, name="attention",
                              )(g0 // (tq // tm), qpT, kp, vT)

    bnd = jnp.broadcast_to(bound[:, :, None, None], (batch, N_HEADS, SUBLANES, LANES))
    return lax.cond(jnp.max(bound) <= FIXED_REF_MAX_BOUND, fixed_ref, running_max, g0, bnd, qpT, kp, vT)


def _outproj_kernel(x_ref, a_ref, yT_ref, wdw_ref, bdw_ref, lng_ref, lnb_ref, gc_ref, wo_ref, ga_ref,
                    g2_ref, wrT_ref, brT_ref, utri_ref, x1_ref, h2_ref, route_ref, cnt_ref,
                    carry_ref, sh_ref, acc_ref, *, tm, rc):
    first = (pl.program_id(0) == 0) & (pl.program_id(1) == 0)

    @pl.when(first)
    def _():
        carry_ref[...] = jnp.zeros_like(carry_ref)

    mc = _conv_tile(a_ref, wdw_ref, bdw_ref, lng_ref, lnb_ref, gc_ref, sh_ref, acc_ref, tc=tm, rc=rc)
    yT = yT_ref[0]
    msa = jnp.mean(yT * yT, axis=0, keepdims=True)
    yn = (yT * lax.rsqrt(msa + EPS)).T * ga_ref[...]
    x1 = x_ref[...] + _dot(jnp.concatenate([mc, yn.astype(BF16)], axis=1), wo_ref[...])
    x1_ref[...] = x1
    ms = jnp.mean(x1 * x1, axis=-1, keepdims=True)
    h2 = (x1 * lax.rsqrt(ms + EPS)) * g2_ref[...]
    h2_ref[:, :D_MODEL] = h2

    lg = _dot_nt(wrT_ref[...], h2.astype(BF16)) + brT_ref[...]
    neg = -jnp.inf
    sub = lax.broadcasted_iota(jnp.int32, (SUBLANES, tm), 0)
    big = jnp.int32(SUBLANES)
    first_of = lambda hit: jnp.min(jnp.where(hit, sub, big), axis=0, keepdims=True)

    lg1 = lg[0:SUBLANES]
    m1 = jnp.max(lg1, axis=0, keepdims=True)
    p1_sel = 1.0 / jnp.sum(jnp.exp(lg1 - m1), axis=0, keepdims=True)
    grp = first_of(lg1 == m1)

    slab = lambda g: lg[SUBLANES * (g + 1):SUBLANES * (g + 2)]
    v = slab(N_GROUPS - 1)
    for g in range(N_GROUPS - 2, -1, -1):
        v = jnp.where(grp == g, slab(g), v)
    v1 = jnp.max(v, axis=0, keepdims=True)
    j1 = first_of(v == v1)
    vv = jnp.where(sub == j1, neg, v)
    v2 = jnp.max(vv, axis=0, keepdims=True)
    j2 = first_of(vv == v2)
    e21 = jnp.exp(v2 - v1)
    w0 = p1_sel / (1.0 + e21)
    w1 = p1_sel * e21 / (1.0 + e21)

    swap = j2 < j1
    al = jnp.where(swap, j2, j1)
    bl = jnp.where(swap, j1, j2)
    wa = jnp.where(swap, w1, w0)
    wb = jnp.where(swap, w0, w1)
    cid = PAIRS_PER_GROUP * grp + ((al * (2 * EXPERTS_PER_GROUP - 1 - al)) >> 1) + (bl - al - 1)

    cls = lax.broadcasted_iota(jnp.int32, (LANES, tm), 0)
    oh = cls == cid
    cmat = jnp.where(oh, 1.0, 0.0)
    prefix = _dot(cmat.astype(BF16), utri_ref[...]) + carry_ref[...]
    rank = jnp.sum(jnp.where(oh, prefix, 0.0), axis=0, keepdims=True)
    counts = prefix[:, tm - 1:tm] + cmat[:, tm - 1:tm]
    carry_ref[...] = counts
    cnt_ref[...] = counts

    h2_ref[:, D_MODEL:] = jnp.where(cls == 0, wa, jnp.where(cls == 1, wb, 0.0)).T
    route_ref[...] = jnp.where(sub == 0, cid.astype(F32), jnp.where(sub == 1, rank, 0.0))


def _outproj(x2, a, yT, conv_params, wo, ga, g2, wrT, brT, *, batch, seq, tm, rc):
    nt = seq // tm
    utri = jnp.triu(jnp.ones((tm, tm), F32), 1).astype(BF16)
    const = lambda shape: pl.BlockSpec(shape, lambda b, t: (0,) * len(shape))
    row_spec = lambda w: pl.BlockSpec((tm, w), lambda b, t: (b * nt + t, 0))
    return pl.pallas_call(
        functools.partial(_outproj_kernel, tm=tm, rc=rc),
        grid=(batch, nt),
        in_specs=[row_spec(D_MODEL), row_spec(D_CONV),
                  pl.BlockSpec((1, D_ATT, tm), lambda b, t: (b, 0, t)),
                  const((CONV_WIDTH, D_CONV)), const((1, D_CONV)), const((1, D_CONV)), const((1, D_CONV)),
                  const((1, D_CONV)),
                  const((D_CONV + D_ATT, D_MODEL)), const((1, D_ATT)), const((1, D_MODEL)),
                  const((LANES, D_MODEL)), const((LANES, 1)), const((tm, tm))],
        out_specs=[row_spec(D_MODEL), row_spec(ROW_W),
                   pl.BlockSpec((SUBLANES, tm), lambda b, t: (0, b * nt + t)), const((LANES, 1))],
        out_shape=[jax.ShapeDtypeStruct((batch * seq, D_MODEL), F32),
                   jax.ShapeDtypeStruct((batch * seq, ROW_W), F32),
                   jax.ShapeDtypeStruct((SUBLANES, batch * seq), F32),
                   jax.ShapeDtypeStruct((LANES, 1), F32)],
        scratch_shapes=[pltpu.VMEM((LANES, 1), F32),
                        pltpu.VMEM((SUBLANES, D_CONV // LANES, tm + CONV_HALO, LANES), F32),
                        pltpu.VMEM((tm, D_CONV), F32)],
        compiler_params=pltpu.CompilerParams(
            dimension_semantics=("arbitrary", "arbitrary"), vmem_limit_bytes=VMEM_LIMIT),
        name="outproj",
    )(x2, a, yT, *conv_params, wo, ga, g2, wrT, brT, utri)


def _row_copies(src_at, dst_at, sem):
    return pltpu.make_async_copy(src_at, dst_at, sem)


def _dispatch_kernel(dest_ref, h2_ref, xs_ref, sem, *, tm):
    def body(g, carry):
        for u in range(SUBLANES):
            _row_copies(h2_ref.at[g, pl.ds(u, 1)], xs_ref.at[pl.ds(dest_ref[g * SUBLANES + u], 1)],
                        sem).start(priority=u % 2)
        return carry

    lax.fori_loop(0, tm // SUBLANES, body, 0)
    done = xs_ref.at[pl.ds(0, tm)]
    _row_copies(done, done, sem).wait()


def _dispatch(dest, h2, *, tm):
    t_len = h2.shape[0]
    return pl.pallas_call(
        functools.partial(_dispatch_kernel, tm=tm),
        grid=(t_len // tm,),
        in_specs=[pl.BlockSpec((tm,), lambda i: (i,), memory_space=pltpu.SMEM),
                  pl.BlockSpec((tm // SUBLANES, SUBLANES, ROW_W), lambda i: (i, 0, 0))],
        out_specs=pl.BlockSpec(memory_space=pl.ANY),
        out_shape=jax.ShapeDtypeStruct((t_len, ROW_W), F32),
        scratch_shapes=[pltpu.SemaphoreType.DMA(())],
        compiler_params=pltpu.CompilerParams(dimension_semantics=("arbitrary",), vmem_limit_bytes=VMEM_LIMIT),
        name="dispatch",
    )(dest, h2.reshape(t_len // SUBLANES, SUBLANES, ROW_W))


def _expert_kernel(blk_ref, ea_ref, eb_ref, lo_ref, hi_ref, x_ref, wgua_ref, wda_ref, wgub_ref, wdb_ref, y_ref):
    del blk_ref, ea_ref, eb_ref
    i = pl.program_id(0)
    lo = lo_ref[i]
    hi = hi_ref[i]
    first = lo // ROW_SEGMENT
    last = (hi - 1) // ROW_SEGMENT
    start = pl.multiple_of(first * ROW_SEGMENT, ROW_SEGMENT)

    def mlp(xb, wgu_ref, wd_ref):
        gu = _dot(xb, wgu_ref[0])
        g = gu[:, :D_EXPERT]
        return _dot(((g * jax.nn.sigmoid(g)) * gu[:, D_EXPERT:]).astype(BF16), wd_ref[0])

    def run(n_rows):
        rows = pl.ds(start, n_rows)
        xb = x_ref[rows, :D_MODEL].astype(BF16)
        wts = x_ref[rows, D_MODEL:]
        y = wts[:, 0:1] * mlp(xb, wgua_ref, wda_ref) + wts[:, 1:2] * mlp(xb, wgub_ref, wdb_ref)
        row = start + lax.broadcasted_iota(jnp.int32, (n_rows, 1), 0)
        mine = (row >= lo) & (row < hi)
        head = pl.ds(start, ROW_SEGMENT)

        @pl.when(lo == start)
        def _():
            y_ref[head, :] = jnp.where(mine[:ROW_SEGMENT], y[:ROW_SEGMENT], 0.0)

        @pl.when(lo > start)
        def _():
            y_ref[head, :] = jnp.where(mine[:ROW_SEGMENT], y[:ROW_SEGMENT], y_ref[head, :])

        if n_rows > ROW_SEGMENT:
            rest = pl.multiple_of(start + ROW_SEGMENT, ROW_SEGMENT)
            y_ref[pl.ds(rest, n_rows - ROW_SEGMENT), :] = jnp.where(
                mine[ROW_SEGMENT:], y[ROW_SEGMENT:], 0.0)

    for n in range(1, ROW_BLOCK // ROW_SEGMENT + 1):
        @pl.when((hi > lo) & (last - first + 1 == n))
        def _(n=n):
            run(n * ROW_SEGMENT)


def _experts(items, xs, wgu, wd):
    n_rows = xs.shape[0]
    n_items = items[0].shape[0]
    wspec = lambda shape, which: pl.BlockSpec(
        shape, lambda i, blk, ea, eb, lo, hi: ((ea, eb)[which][i], 0, 0))
    grid_spec = pltpu.PrefetchScalarGridSpec(
        num_scalar_prefetch=5,
        grid=(n_items,),
        in_specs=[pl.BlockSpec((ROW_BLOCK, ROW_W), lambda i, blk, ea, eb, lo, hi: (blk[i], 0)),
                  wspec((1, D_MODEL, 2 * D_EXPERT), 0), wspec((1, D_EXPERT, D_MODEL), 0),
                  wspec((1, D_MODEL, 2 * D_EXPERT), 1), wspec((1, D_EXPERT, D_MODEL), 1)],
        out_specs=pl.BlockSpec((ROW_BLOCK, D_MODEL), lambda i, blk, ea, eb, lo, hi: (blk[i], 0)),
    )
    return pl.pallas_call(
        _expert_kernel,
        grid_spec=grid_spec,
        out_shape=jax.ShapeDtypeStruct((n_rows, D_MODEL), F32),
        compiler_params=pltpu.CompilerParams(dimension_semantics=("arbitrary",), vmem_limit_bytes=VMEM_LIMIT),
        name="experts",
    )(*items, xs, wgu, wd, wgu, wd)


def _combine_kernel(dcur_ref, dnxt_ref, x1_ref, g_ref, ys_ref, out_ref, buf_ref, sems, *, tm):
    i = pl.program_id(0)
    slot = i % 2

    def issue(d_ref, s):
        def body(g, carry):
            for u in range(SUBLANES):
                _row_copies(ys_ref.at[pl.ds(d_ref[g * SUBLANES + u], 1)], buf_ref.at[s, g, pl.ds(u, 1)],
                            sems.at[s]).start(priority=u % 2)
            return carry

        lax.fori_loop(0, tm // SUBLANES, body, 0)

    @pl.when(i == 0)
    def _():
        issue(dcur_ref, 0)

    @pl.when(i + 1 < pl.num_programs(0))
    def _():
        issue(dnxt_ref, 1 - slot)

    _row_copies(buf_ref.at[slot], buf_ref.at[slot], sems.at[slot]).wait()
    x2 = x1_ref[...] + buf_ref[slot]
    ms = jnp.mean(x2 * x2, axis=-1, keepdims=True)
    out_ref[...] = (x2 * lax.rsqrt(ms + EPS)) * g_ref[...]


def _combine(dest, x1, gf, ys, *, tm):
    t_len = x1.shape[0]
    n = t_len // tm
    return pl.pallas_call(
        functools.partial(_combine_kernel, tm=tm),
        grid=(n,),
        in_specs=[pl.BlockSpec((tm,), lambda i: (i,), memory_space=pltpu.SMEM),
                  pl.BlockSpec((tm,), lambda i: (jnp.minimum(i + 1, n - 1),), memory_space=pltpu.SMEM),
                  pl.BlockSpec((tm // SUBLANES, SUBLANES, D_MODEL), lambda i: (i, 0, 0)),
                  pl.BlockSpec((1, D_MODEL), lambda i: (0, 0)),
                  pl.BlockSpec(memory_space=pl.ANY)],
        out_specs=pl.BlockSpec((tm // SUBLANES, SUBLANES, D_MODEL), lambda i: (i, 0, 0)),
        out_shape=jax.ShapeDtypeStruct((t_len // SUBLANES, SUBLANES, D_MODEL), F32),
        scratch_shapes=[pltpu.VMEM((2, tm // SUBLANES, SUBLANES, D_MODEL), F32), pltpu.SemaphoreType.DMA((2,))],
        compiler_params=pltpu.CompilerParams(dimension_semantics=("arbitrary",), vmem_limit_bytes=VMEM_LIMIT),
        name="combine",
    )(dest, dest, x1.reshape(t_len // SUBLANES, SUBLANES, D_MODEL), gf, ys)


def _layer(x, norm1_g, w_in, b_f, w_dw, b_dw, conv_ln_g, conv_ln_b, out_g_conv, out_g_att, w_out,
           norm2_g, w_r1, b_r1, w_r2, b_r2, w_gate, w_up, w_down):
    batch, seq, d = x.shape
    t_len = batch * seq
    tm = min(512, seq)
    tk = min(256, seq)
    tq = min(1024, seq)
    o2 = 2 * D_CONV
    o3, o4, o5 = o2 + D_ATT, o2 + 2 * D_ATT, o2 + 3 * D_ATT

    assert PIECE_STRIDE == N_HEADS
    pad = LANES - N_PIECES * PIECE_STRIDE
    wf = jnp.pad(jnp.tile(w_in[:, o5:], (1, N_PIECES)), ((0, 0), (0, pad)))
    wrow = jnp.concatenate([w_in[:, :o2], w_in[:, o3:o4], wf], axis=1).astype(BF16)
    wqvT = jnp.concatenate([w_in[:, o2:o3] * (HEAD_DIM ** -0.5), w_in[:, o4:o5]], axis=1).T.astype(BF16)
    bf3 = jnp.pad(jnp.tile(b_f.astype(F32).reshape(1, N_HEADS), (1, N_PIECES)), ((0, 0), (0, pad)))

    x2 = x.reshape(t_len, d)
    a, kp, qpT, vT, kstat, qstat = _inproj(x2, norm1_g.reshape(1, d), wrow, bf3, wqvT,
                                           batch=batch, seq=seq, tm=tm, tk=tk)
    conv_params = (w_dw, b_dw.reshape(1, -1), conv_ln_g.reshape(1, -1), conv_ln_b.reshape(1, -1),
                   out_g_conv.reshape(1, -1))
    g0, bound = _first_groups(kstat, qstat, tm=tm, tq=tq)
    yT = _attention(g0, bound, qpT, kp, vT, batch=batch, seq=seq, tm=tm, tq=tq, tk=tk)

    gpad = SUBLANES - N_GROUPS
    rpad = LANES - SUBLANES - N_EXPERTS
    wrT = jnp.concatenate([w_r1.T, jnp.zeros((gpad, d), F32),
                           jnp.transpose(w_r2, (0, 2, 1)).reshape(N_EXPERTS, d), jnp.zeros((rpad, d), F32)], axis=0)
    brT = jnp.concatenate([b_r1.astype(F32), jnp.full((gpad,), -jnp.inf, F32), b_r2.reshape(-1).astype(F32),
                           jnp.zeros((rpad,), F32)]).reshape(LANES, 1)
    x1, h2, route, cnt = _outproj(x2, a, yT, conv_params, w_out.astype(BF16),
                                  out_g_att.reshape(1, -1), norm2_g.reshape(1, d),
                                  wrT.astype(BF16), brT, batch=batch, seq=seq, tm=tm, rc=min(256, seq))

    i32 = jnp.int32
    lanes = jnp.arange(LANES, dtype=i32)
    pick = lambda table, idx: jnp.sum(jnp.where(idx[:, None] == lanes, table[None, :], 0), axis=1).astype(i32)
    counts = cnt[:, 0].astype(i32)
    ends = jnp.cumsum(counts).astype(i32)
    starts = ends - counts
    dest = pick(starts, route[0].astype(i32)) + route[1].astype(i32)

    n_blocks = t_len // ROW_BLOCK
    b_lo = starts // ROW_BLOCK
    n_it = jnp.where(counts > 0, (ends - 1) // ROW_BLOCK - b_lo + 1, 0)
    it_end = jnp.cumsum(n_it).astype(i32)
    it_start = it_end - n_it
    idx = jnp.arange(n_blocks + N_CLASSES, dtype=i32)
    valid = idx < it_end[-1]
    last_cls = jnp.max(jnp.where(counts > 0, lanes, 0))
    cls = jnp.where(valid, jnp.sum(it_end[None, :] <= idx[:, None], axis=1).astype(i32), last_cls)
    blk = jnp.where(valid, pick(b_lo, cls) + idx - pick(it_start, cls), n_blocks - 1)
    row0 = blk * ROW_BLOCK
    lo = jnp.where(valid, jnp.maximum(pick(starts, cls), row0) - row0, 0)
    hi = jnp.where(valid, jnp.minimum(pick(ends, cls), row0 + ROW_BLOCK) - row0, 0)
    pair_a, pair_b = [], []
    for g in range(N_GROUPS):
        for a_loc in range(EXPERTS_PER_GROUP):
            for b_loc in range(a_loc + 1, EXPERTS_PER_GROUP):
                pair_a.append(g * EXPERTS_PER_GROUP + a_loc)
                pair_b.append(g * EXPERTS_PER_GROUP + b_loc)
    cpad = [0] * (LANES - N_CLASSES)
    items = (blk.astype(i32), pick(jnp.array(pair_a + cpad, i32), cls), pick(jnp.array(pair_b + cpad, i32), cls),
             lo.astype(i32), hi.astype(i32))

    xs = _dispatch(dest, h2, tm=min(4096, t_len))
    ys = _experts(items, xs, jnp.concatenate([w_gate, w_up], axis=-1).astype(BF16), w_down.astype(BF16))
    return dest, x1, ys


def kernel(x, norm1_g, w_in, b_f, w_dw, b_dw, conv_ln_g, conv_ln_b, out_g_conv, out_g_att, w_out, norm2_g,
           w_r1, b_r1, w_r2, b_r2, w_gate, w_up, w_down, final_g):
    assert norm1_g.shape[0] == 1, "single-layer stack"
    batch, seq, d = x.shape
    dest, x1, ys = _layer(
        x, norm1_g[0], w_in[0], b_f[0], w_dw[0], b_dw[0], conv_ln_g[0], conv_ln_b[0], out_g_conv[0],
        out_g_att[0], w_out[0], norm2_g[0], w_r1[0], b_r1[0], w_r2[0], b_r2[0], w_gate[0], w_up[0], w_down[0])
    out = _combine(dest, x1, final_g.reshape(1, d), ys, tm=min(512, batch * seq))
    return out.reshape(batch, seq, d)
```

```python
import functools

import jax
import jax.numpy as jnp
from jax import lax
from jax.experimental import pallas as pl
from jax.experimental.pallas import tpu as pltpu

D_MODEL = 1024
D_CONV = 512
N_HEADS = 8
HEAD_DIM = 64
D_ATT = N_HEADS * HEAD_DIM
CONV_WIDTH = 31
N_GROUPS = 4
EXPERTS_PER_GROUP = 8
N_EXPERTS = N_GROUPS * EXPERTS_PER_GROUP
D_EXPERT = D_MODEL // 4
ROW_BLOCK = 1024
ROW_SEGMENT = 128
EPS = 1e-6

LANES = 128
SUBLANES = 8
KAUG = 128
N_PIECES = 3
PIECE_STRIDE = 8
CONV_HALO = 32
PAIRS_PER_GROUP = EXPERTS_PER_GROUP * (EXPERTS_PER_GROUP - 1) // 2
N_CLASSES = N_GROUPS * PAIRS_PER_GROUP
ROW_W = D_MODEL + LANES
SKIP_LOG2 = 160.0
FIXED_REF_MAX_BOUND = 40.0
GROUPS_PER_TRIP = 4
NORM_SLACK = 1.02
V_ROWS = 80
LOG2E = 1.4426950408889634
VMEM_LIMIT = 56 * 1024 * 1024

F32 = jnp.float32
BF16 = jnp.bfloat16


def _dot(a, b):
    return jnp.dot(a, b, preferred_element_type=F32)


def _dot_nt(a, b):
    return lax.dot_general(a, b, (((1,), (1,)), ((), ())), preferred_element_type=F32)


def _split3(x):
    hi = x.astype(BF16)
    r1 = x - hi.astype(F32)
    mid = r1.astype(BF16)
    lo = (r1 - mid.astype(F32)).astype(BF16)
    return hi.astype(F32), mid.astype(F32), lo.astype(F32)


def _piece_lane_mask(lane, h):
    return (lane == h) | (lane == h + PIECE_STRIDE) | (lane == h + 2 * PIECE_STRIDE)


def _inproj_kernel(x_ref, g1_ref, wrow_ref, bf_ref, wqvT_ref, ltri_ref, hsel_ref,
                   a_ref, kp_ref, qpT_ref, vT_ref, kstat_ref, qstat_ref, carry_ref, *, tm, tk):
    @pl.when(pl.program_id(1) == 0)
    def _():
        carry_ref[...] = jnp.zeros_like(carry_ref)

    x = x_ref[...]
    ms = jnp.mean(x * x, axis=-1, keepdims=True)
    hb = ((x * lax.rsqrt(ms + EPS)) * g1_ref[...]).astype(BF16)

    z = _dot(hb, wrow_ref[...])
    a_ref[...] = z[:, :D_CONV] * jax.nn.sigmoid(z[:, D_CONV:2 * D_CONV])

    kk = z[:, 2 * D_CONV:2 * D_CONV + D_ATT]

    zf = z[:, 2 * D_CONV + D_ATT:] + bf_ref[...]
    lf = jnp.minimum(zf, 0.0) - jnp.log1p(jnp.exp(-jnp.abs(zf)))
    lane = lax.broadcasted_iota(jnp.int32, (tm, LANES), 1)
    hi, mid, lo = _split3(lf)
    lf3 = jnp.where(lane < PIECE_STRIDE, hi,
                    jnp.where(lane < 2 * PIECE_STRIDE, mid,
                              jnp.where(lane < 3 * PIECE_STRIDE, lo, 0.0))).astype(BF16)
    cs3 = _dot(ltri_ref[...], lf3)
    c = (cs3 + pltpu.roll(cs3, LANES - PIECE_STRIDE, 1)
         + pltpu.roll(cs3, LANES - 2 * PIECE_STRIDE, 1)) + carry_ref[...]
    carry_ref[...] = c[tm - 1:tm, :]

    nhi, nmid, nlo = _split3(c * (-LOG2E))
    p3 = jnp.where(lane < PIECE_STRIDE, nhi,
                   jnp.where(lane < 2 * PIECE_STRIDE, pltpu.roll(nmid, PIECE_STRIDE, 1),
                             jnp.where(lane < 3 * PIECE_STRIDE, pltpu.roll(nlo, 2 * PIECE_STRIDE, 1), 0.0)))
    p3_hi = pltpu.roll(p3, HEAD_DIM, 1)
    q_pieces = (-p3).T[0:HEAD_DIM]
    q_shift = N_PIECES * PIECE_STRIDE

    qvT = _dot_nt(wqvT_ref[...], hb)
    qT = qvT[:D_ATT] * LOG2E
    vT = qvT[D_ATT:]
    row = lax.broadcasted_iota(jnp.int32, (HEAD_DIM, tm), 0)
    vrow = lax.broadcasted_iota(jnp.int32, (V_ROWS - HEAD_DIM, tk), 0)
    v_tail = jnp.where(vrow == 0, 1.0, 0.0).astype(BF16)

    for h in range(N_HEADS):
        kcol = kk[:, (h // 2) * LANES:(h // 2 + 1) * LANES]
        f_h = jnp.where(_piece_lane_mask(row, h), q_pieces, 0.0)
        aug_q = jnp.where(_piece_lane_mask(row, h), 1.0, 0.0) + jnp.concatenate(
            [jnp.zeros((q_shift, tm), F32), f_h[0:HEAD_DIM - q_shift]], axis=0)
        q_h = qT[h * HEAD_DIM:(h + 1) * HEAD_DIM, :]
        if h % 2 == 0:
            ext = jnp.where(_piece_lane_mask(lane, h + HEAD_DIM), p3_hi,
                            jnp.where(_piece_lane_mask(lane, h + HEAD_DIM + q_shift), 1.0, 0.0))
            kp = jnp.where(lane < HEAD_DIM, kcol, ext)
            qp = jnp.concatenate([q_h, aug_q], axis=0)
        else:
            ext = jnp.where(_piece_lane_mask(lane, h), p3,
                            jnp.where(_piece_lane_mask(lane, h + q_shift), 1.0, 0.0))
            kp = jnp.where(lane >= HEAD_DIM, kcol, ext)
            qp = jnp.concatenate([aug_q, q_h], axis=0)
        qpT_ref[0, h] = qp.astype(BF16)
        for cidx in range(tm // tk):
            kp_ref[0, h, cidx] = kp[cidx * tk:(cidx + 1) * tk, :].astype(BF16)
            v_h = vT[h * HEAD_DIM:(h + 1) * HEAD_DIM, cidx * tk:(cidx + 1) * tk].astype(BF16)
            vT_ref[0, h, cidx] = jnp.concatenate([v_h, v_tail], axis=0)

    kn2 = jnp.max(_dot((kk * kk).astype(BF16), hsel_ref[...]), axis=0, keepdims=True)
    srow = lax.broadcasted_iota(jnp.int32, (SUBLANES, LANES), 0)
    kstat_ref[0, 0] = jnp.where(srow == 0, c[0:1, :], jnp.where(srow == 1, c[tm - 1:tm, :],
                                                                 jnp.where(srow == 2, kn2, 0.0)))
    q2 = qT * qT
    qn2 = jnp.concatenate([jnp.sum(q2[h * HEAD_DIM:(h + 1) * HEAD_DIM, :], axis=0, keepdims=True)
                           for h in range(N_HEADS)], axis=0)
    qstat_ref[0, 0] = jnp.broadcast_to(jnp.max(qn2, axis=1, keepdims=True), (N_HEADS, LANES))


def _inproj(x2, g1, wrow, bf3, wqvT, *, batch, seq, tm, tk):
    nt = seq // tm
    nk = seq // tk
    ltri = jnp.tril(jnp.ones((tm, tm), F32)).astype(BF16)
    hsel = (jnp.arange(D_ATT)[:, None] // HEAD_DIM == jnp.arange(LANES)[None, :]).astype(BF16)
    const = lambda shape: pl.BlockSpec(shape, lambda b, t: (0,) * len(shape))
    stat_spec = pl.BlockSpec((1, 1, SUBLANES, LANES), lambda b, t: (b, t, 0, 0))
    stat_shape = jax.ShapeDtypeStruct((batch, nt, SUBLANES, LANES), F32)
    return pl.pallas_call(
        functools.partial(_inproj_kernel, tm=tm, tk=tk),
        grid=(batch, nt),
        in_specs=[
            pl.BlockSpec((tm, D_MODEL), lambda b, t: (b * nt + t, 0)),
            const((1, D_MODEL)), const((D_MODEL, 2 * D_CONV + D_ATT + LANES)), const((1, LANES)),
            const((2 * D_ATT, D_MODEL)), const((tm, tm)), const((D_ATT, LANES)),
        ],
        out_specs=[
            pl.BlockSpec((tm, D_CONV), lambda b, t: (b * nt + t, 0)),
            pl.BlockSpec((1, N_HEADS, tm // tk, tk, KAUG), lambda b, t: (b, 0, t, 0, 0)),
            pl.BlockSpec((1, N_HEADS, KAUG, tm), lambda b, t: (b, 0, 0, t)),
            pl.BlockSpec((1, N_HEADS, tm // tk, V_ROWS, tk), lambda b, t: (b, 0, t, 0, 0)),
            stat_spec, stat_spec,
        ],
        out_shape=[
            jax.ShapeDtypeStruct((batch * seq, D_CONV), F32),
            jax.ShapeDtypeStruct((batch, N_HEADS, nk, tk, KAUG), BF16),
            jax.ShapeDtypeStruct((batch, N_HEADS, KAUG, seq), BF16),
            jax.ShapeDtypeStruct((batch, N_HEADS, nk, V_ROWS, tk), BF16),
            stat_shape, stat_shape,
        ],
        scratch_shapes=[pltpu.VMEM((1, LANES), F32)],
        compiler_params=pltpu.CompilerParams(
            dimension_semantics=("arbitrary", "arbitrary"), vmem_limit_bytes=VMEM_LIMIT),
        name="inproj",
    )(x2, g1, wrow, bf3, wqvT, ltri, hsel)


def _conv_tile(a_ref, w_ref, b_ref, lng_ref, lnb_ref, og_ref, sh_ref, acc_ref, *, tc, rc):
    n_cb = D_CONV // LANES

    @pl.when(pl.program_id(1) == 0)
    def _():
        for cb in range(n_cb):
            sh_ref[0, cb, 0:CONV_HALO, :] = jnp.zeros((CONV_HALO, LANES), F32)

    @pl.when(pl.program_id(1) > 0)
    def _():
        for cb in range(n_cb):
            sh_ref[0, cb, 0:CONV_HALO, :] = sh_ref[0, cb, tc:tc + CONV_HALO, :]

    n_sh = tc + CONV_HALO - SUBLANES
    for cb in range(n_cb):
        sh_ref[0, cb, CONV_HALO:CONV_HALO + tc, :] = a_ref[:, cb * LANES:(cb + 1) * LANES]
        for f in range(1, SUBLANES):
            sh_ref[f, cb, 0:n_sh, :] = sh_ref[0, cb, f:f + n_sh, :]

    base = CONV_HALO - (CONV_WIDTH - 1)
    for cb in range(n_cb):
        cols = slice(cb * LANES, (cb + 1) * LANES)

        def chunk(c, carry, cb=cb, cols=cols):
            r0 = pl.multiple_of(c * rc, rc)
            acc = jnp.zeros((rc, LANES), F32)
            for j in range(CONV_WIDTH):
                f = (base + j) % SUBLANES
                acc = acc + w_ref[j:j + 1, cols] * sh_ref[f, cb, pl.ds(r0 + (base + j - f), rc), :]
            acc_ref[pl.ds(r0, rc), cols] = acc
            return carry

        lax.fori_loop(0, tc // rc, chunk, 0)

    y = acc_ref[...] + b_ref[...]
    mu = jnp.mean(y, axis=-1, keepdims=True)
    yc = y - mu
    var = jnp.mean(yc * yc, axis=-1, keepdims=True)
    yn = yc * lax.rsqrt(var + EPS) * lng_ref[...] + lnb_ref[...]
    s = yn * jax.nn.sigmoid(yn)
    ms = jnp.mean(s * s, axis=-1, keepdims=True)
    return (s * lax.rsqrt(ms + EPS) * og_ref[...]).astype(BF16)


def _attn_fixed_ref_kernel(g0_ref, bnd_ref, qT_ref, k_ref, v_ref, o_ref, *, tq, tk, unit):
    i = pl.program_id(2)
    qT = qT_ref[0, 0]
    n_sub = tq // tk
    bnd = bnd_ref[0, 0, 0:1, 0:1]
    sb0 = unit * g0_ref[(pl.program_id(0) * N_HEADS + pl.program_id(1)) * pl.num_programs(2) + i]

    def values(sb, n):
        return jnp.concatenate([v_ref[0, 0, sb + d] for d in range(n)], axis=1)

    def probs(sb, n):
        return jnp.concatenate([jnp.exp2(_dot(k_ref[0, 0, sb + d], qT) - bnd).astype(BF16)
                                for d in range(n)], axis=0)

    def accumulate(acc, sb, n):
        for lo in range(0, n, n_sub):
            m = min(n_sub, n - lo)
            acc = acc + _dot(values(sb + lo, m), probs(sb + lo, m))
        return acc

    def loop_trip(p, acc):
        return accumulate(acc, sb0 + GROUPS_PER_TRIP * n_sub * p, GROUPS_PER_TRIP * n_sub)

    key = lax.broadcasted_iota(jnp.int32, (tk, tk), 0)
    qry = lax.broadcasted_iota(jnp.int32, (tk, tk), 1)

    def diagonal(acc):
        rows = []
        for d in range(n_sub):
            s = _dot(k_ref[0, 0, i * n_sub + d], qT[:, d * tk:])
            tiles = [jnp.zeros((tk, d * tk), BF16)] if d else []
            tiles.append(jnp.exp2(jnp.where(key <= qry, s[:, :tk], -jnp.inf) - bnd).astype(BF16))
            if d + 1 < n_sub:
                tiles.append(jnp.exp2(s[:, tk:] - bnd).astype(BF16))
            rows.append(jnp.concatenate(tiles, axis=1))
        acc = acc + _dot(values(i * n_sub, n_sub), jnp.concatenate(rows, axis=0))
        o_ref[0] = acc[:HEAD_DIM] / acc[HEAD_DIM:HEAD_DIM + 1]

    n_before = i * n_sub - sb0
    per_trip = GROUPS_PER_TRIP * n_sub
    acc = lax.fori_loop(0, n_before // per_trip, loop_trip, jnp.zeros((V_ROWS, tq), F32))
    left = n_before % per_trip

    for n in range(0, per_trip, unit):
        @pl.when(left == n)
        def _(n=n):
            diagonal(accumulate(acc, i * n_sub - n, n))


def _attn_kernel(g0_ref, qT_ref, k_ref, v_ref, o_ref, sa_ref, sb_ref, cma_ref, cmb_ref, *, tq, tk):
    i = pl.program_id(2)
    qT = qT_ref[0, 0]
    n_sub = tq // tk

    def scores(g, s_ref, cm_ref):
        for d in range(n_sub):
            s = _dot(k_ref[0, 0, g * n_sub + d], qT)
            s_ref[d] = s
            cm_ref[d] = jnp.max(s, axis=0, keepdims=True)

    def consume(g, s_ref, cm_ref, carry, nxt=None):
        m, acc = carry
        m_new = m
        for d in range(n_sub):
            m_new = jnp.maximum(m_new, cm_ref[d])
        ps = []
        for d in range(n_sub):
            if nxt is not None:
                g_n, sn_ref, cmn_ref = nxt
                s = _dot(k_ref[0, 0, g_n * n_sub + d], qT)
                sn_ref[d] = s
                cmn_ref[d] = jnp.max(s, axis=0, keepdims=True)
            ps.append(jnp.exp2(s_ref[d] - m_new).astype(BF16))
        p = jnp.concatenate(ps, axis=0)
        v = jnp.concatenate([v_ref[0, 0, g * n_sub + d] for d in range(n_sub)], axis=1)
        return m_new, jnp.exp2(m - m_new) * acc + _dot(v, p)

    key = lax.broadcasted_iota(jnp.int32, (tk, tk), 0)
    qry = lax.broadcasted_iota(jnp.int32, (tk, tk), 1)

    def consume_diag(s_ref, cm_ref, carry):
        m, acc = carry
        lanes = lambda j: slice(j * tk, (j + 1) * tk)
        causal = [jnp.where(key <= qry, s_ref[d, :, lanes(d)], -jnp.inf) for d in range(n_sub)]
        m_tiles = []
        for j in range(n_sub):
            m_j = jnp.maximum(m[:, lanes(j)], jnp.max(causal[j], axis=0, keepdims=True))
            for d in range(j):
                m_j = jnp.maximum(m_j, cm_ref[d, :, lanes(j)])
            m_tiles.append(m_j)
        m_new = jnp.concatenate(m_tiles, axis=1)
        rows = []
        for d in range(n_sub):
            tiles = [jnp.zeros((tk, d * tk), BF16)] if d else []
            tiles.append(jnp.exp2(causal[d] - m_tiles[d]).astype(BF16))
            if d + 1 < n_sub:
                rest = slice((d + 1) * tk, tq)
                tiles.append(jnp.exp2(s_ref[d, :, rest] - m_new[:, rest]).astype(BF16))
            rows.append(jnp.concatenate(tiles, axis=1))
        p = jnp.concatenate(rows, axis=0)
        v = jnp.concatenate([v_ref[0, 0, i * n_sub + d] for d in range(n_sub)], axis=1)
        acc = jnp.exp2(m - m_new) * acc + _dot(v, p)
        o_ref[0] = acc[:HEAD_DIM] / acc[HEAD_DIM:HEAD_DIM + 1]

    g0 = g0_ref[(pl.program_id(0) * N_HEADS + pl.program_id(1)) * pl.num_programs(2) + i]
    n_full = i - g0

    def pair(p, carry):
        g = g0 + 2 * p
        carry = consume(g, sa_ref, cma_ref, carry, nxt=(g + 1, sb_ref, cmb_ref))
        return consume(g + 1, sb_ref, cmb_ref, carry, nxt=(g + 2, sa_ref, cma_ref))

    scores(g0, sa_ref, cma_ref)
    init = (jnp.full((1, tq), -jnp.inf, F32), jnp.zeros((V_ROWS, tq), F32))
    carry = lax.fori_loop(0, n_full // 2, pair, init)

    @pl.when(n_full % 2 == 1)
    def _():
        consume_diag(sb_ref, cmb_ref, consume(i - 1, sa_ref, cma_ref, carry, nxt=(i, sb_ref, cmb_ref)))

    @pl.when(n_full % 2 == 0)
    def _():
        consume_diag(sa_ref, cma_ref, carry)


def _first_groups(kstat, qstat, *, tm, tq):
    per = tq // tm
    bound = jnp.sqrt(jnp.max(kstat[:, :, 2, :N_HEADS], axis=1) * jnp.max(qstat[:, :, :, 0], axis=1))
    bound = bound * NORM_SLACK
    cq = kstat[:, ::per, 0, :N_HEADS]
    ck = kstat[:, :, 1, :N_HEADS]
    d = (cq[:, :, None, :] - ck[:, None, :, :]) * LOG2E
    nq, ng = cq.shape[1], ck.shape[1]
    earlier = per * jnp.arange(nq)[None, :, None, None] > jnp.arange(ng)[None, None, :, None]
    skip = earlier & (2.0 * bound[:, None, None, :] + d < -SKIP_LOG2)
    g0 = jnp.sum(jnp.cumprod(skip.astype(jnp.int32), axis=2), axis=2)
    return jnp.transpose(g0, (0, 2, 1)).reshape(-1).astype(jnp.int32), bound


def _attention(g0, bound, qpT, kp, vT, *, batch, seq, tm, tq, tk):
    nk = seq // tk
    params = pltpu.CompilerParams(
        dimension_semantics=("arbitrary", "arbitrary", "arbitrary"), vmem_limit_bytes=VMEM_LIMIT)
    out_shape = jax.ShapeDtypeStruct((batch, D_ATT, seq), F32)
    qkv_specs = [
        pl.BlockSpec((1, 1, KAUG, tq), lambda b, h, i, g0: (b, h, 0, i)),
        pl.BlockSpec((1, 1, nk, tk, KAUG), lambda b, h, i, g0: (b, h, 0, 0, 0)),
        pl.BlockSpec((1, 1, nk, V_ROWS, tk), lambda b, h, i, g0: (b, h, 0, 0, 0)),
    ]
    out_spec = pl.BlockSpec((1, HEAD_DIM, tq), lambda b, h, i, g0: (b, h, i))

    def fixed_ref(g0, bnd, qpT, kp, vT):
        grid_spec = pltpu.PrefetchScalarGridSpec(
            num_scalar_prefetch=1, grid=(batch, N_HEADS, seq // tq),
            in_specs=[pl.BlockSpec((1, 1, SUBLANES, LANES), lambda b, h, i, g0: (b, h, 0, 0))] + qkv_specs,
            out_specs=out_spec)
        return pl.pallas_call(functools.partial(_attn_fixed_ref_kernel, tq=tq, tk=tk, unit=tm // tk),
                              grid_spec=grid_spec, out_shape=out_shape, compiler_params=params,
                              name="attention_fixed_ref")(g0, bnd, qpT, kp, vT)

    def running_max(g0, bnd, qpT, kp, vT):
        del bnd
        grid_spec = pltpu.PrefetchScalarGridSpec(
            num_scalar_prefetch=1, grid=(batch, N_HEADS, seq // tq), in_specs=qkv_specs, out_specs=out_spec,
            scratch_shapes=[pltpu.VMEM((tq // tk, tk, tq), F32), pltpu.VMEM((tq // tk, tk, tq), F32),
                            pltpu.VMEM((tq // tk, 1, tq), F32), pltpu.VMEM((tq // tk, 1, tq), F32)])
        return pl.pallas_call(functools.partial(_attn_kernel, tq=tq, tk=tk), grid_spec=grid_spec,
                              out_shape=out_shape, compiler_params=params, name="attention",
                              )(g0 // (tq // tm), qpT, kp, vT)

    bnd = jnp.broadcast_to(bound[:, :, None, None], (batch, N_HEADS, SUBLANES, LANES))
    return lax.cond(jnp.max(bound) <= FIXED_REF_MAX_BOUND, fixed_ref, running_max, g0, bnd, qpT, kp, vT)


def _outproj_kernel(x_ref, a_ref, yT_ref, wdw_ref, bdw_ref, lng_ref, lnb_ref, gc_ref, wo_ref, ga_ref,
                    g2_ref, wrT_ref, brT_ref, utri_ref, x1_ref, h2_ref, route_ref, cnt_ref,
                    carry_ref, sh_ref, acc_ref, *, tm, rc):
    first = (pl.program_id(0) == 0) & (pl.program_id(1) == 0)

    @pl.when(first)
    def _():
        carry_ref[...] = jnp.zeros_like(carry_ref)

    mc = _conv_tile(a_ref, wdw_ref, bdw_ref, lng_ref, lnb_ref, gc_ref, sh_ref, acc_ref, tc=tm, rc=rc)
    yT = yT_ref[0]
    msa = jnp.mean(yT * yT, axis=0, keepdims=True)
    yn = (yT * lax.rsqrt(msa + EPS)).T * ga_ref[...]
    x1 = x_ref[...] + _dot(jnp.concatenate([mc, yn.astype(BF16)], axis=1), wo_ref[...])
    x1_ref[...] = x1
    ms = jnp.mean(x1 * x1, axis=-1, keepdims=True)
    h2 = (x1 * lax.rsqrt(ms + EPS)) * g2_ref[...]
    h2_ref[:, :D_MODEL] = h2

    lg = _dot_nt(wrT_ref[...], h2.astype(BF16)) + brT_ref[...]
    neg = -jnp.inf
    sub = lax.broadcasted_iota(jnp.int32, (SUBLANES, tm), 0)
    big = jnp.int32(SUBLANES)
    first_of = lambda hit: jnp.min(jnp.where(hit, sub, big), axis=0, keepdims=True)

    lg1 = lg[0:SUBLANES]
    m1 = jnp.max(lg1, axis=0, keepdims=True)
    p1_sel = 1.0 / jnp.sum(jnp.exp(lg1 - m1), axis=0, keepdims=True)
    grp = first_of(lg1 == m1)

    slab = lambda g: lg[SUBLANES * (g + 1):SUBLANES * (g + 2)]
    v = slab(N_GROUPS - 1)
    for g in range(N_GROUPS - 2, -1, -1):
        v = jnp.where(grp == g, slab(g), v)
    v1 = jnp.max(v, axis=0, keepdims=True)
    j1 = first_of(v == v1)
    vv = jnp.where(sub == j1, neg, v)
    v2 = jnp.max(vv, axis=0, keepdims=True)
    j2 = first_of(vv == v2)
    e21 = jnp.exp(v2 - v1)
    w0 = p1_sel / (1.0 + e21)
    w1 = p1_sel * e21 / (1.0 + e21)

    swap = j2 < j1
    al = jnp.where(swap, j2, j1)
    bl = jnp.where(swap, j1, j2)
    wa = jnp.where(swap, w1, w0)
    wb = jnp.where(swap, w0, w1)
    cid = PAIRS_PER_GROUP * grp + ((al * (2 * EXPERTS_PER_GROUP - 1 - al)) >> 1) + (bl - al - 1)

    cls = lax.broadcasted_iota(jnp.int32, (LANES, tm), 0)
    oh = cls == cid
    cmat = jnp.where(oh, 1.0, 0.0)
    prefix = _dot(cmat.astype(BF16), utri_ref[...]) + carry_ref[...]
    rank = jnp.sum(jnp.where(oh, prefix, 0.0), axis=0, keepdims=True)
    counts = prefix[:, tm - 1:tm] + cmat[:, tm - 1:tm]
    carry_ref[...] = counts
    cnt_ref[...] = counts

    h2_ref[:, D_MODEL:] = jnp.where(cls == 0, wa, jnp.where(cls == 1, wb, 0.0)).T
    route_ref[...] = jnp.where(sub == 0, cid.astype(F32), jnp.where(sub == 1, rank, 0.0))


def _outproj(x2, a, yT, conv_params, wo, ga, g2, wrT, brT, *, batch, seq, tm, rc):
    nt = seq // tm
    utri = jnp.triu(jnp.ones((tm, tm), F32), 1).astype(BF16)
    const = lambda shape: pl.BlockSpec(shape, lambda b, t: (0,) * len(shape))
    row_spec = lambda w: pl.BlockSpec((tm, w), lambda b, t: (b * nt + t, 0))
    return pl.pallas_call(
        functools.partial(_outproj_kernel, tm=tm, rc=rc),
        grid=(batch, nt),
        in_specs=[row_spec(D_MODEL), row_spec(D_CONV),
                  pl.BlockSpec((1, D_ATT, tm), lambda b, t: (b, 0, t)),
                  const((CONV_WIDTH, D_CONV)), const((1, D_CONV)), const((1, D_CONV)), const((1, D_CONV)),
                  const((1, D_CONV)),
                  const((D_CONV + D_ATT, D_MODEL)), const((1, D_ATT)), const((1, D_MODEL)),
                  const((LANES, D_MODEL)), const((LANES, 1)), const((tm, tm))],
        out_specs=[row_spec(D_MODEL), row_spec(ROW_W),
                   pl.BlockSpec((SUBLANES, tm), lambda b, t: (0, b * nt + t)), const((LANES, 1))],
        out_shape=[jax.ShapeDtypeStruct((batch * seq, D_MODEL), F32),
                   jax.ShapeDtypeStruct((batch * seq, ROW_W), F32),
                   jax.ShapeDtypeStruct((SUBLANES, batch * seq), F32),
                   jax.ShapeDtypeStruct((LANES, 1), F32)],
        scratch_shapes=[pltpu.VMEM((LANES, 1), F32),
                        pltpu.VMEM((SUBLANES, D_CONV // LANES, tm + CONV_HALO, LANES), F32),
                        pltpu.VMEM((tm, D_CONV), F32)],
        compiler_params=pltpu.CompilerParams(
            dimension_semantics=("arbitrary", "arbitrary"), vmem_limit_bytes=VMEM_LIMIT),
        name="outproj",
    )(x2, a, yT, *conv_params, wo, ga, g2, wrT, brT, utri)


def _row_copies(src_at, dst_at, sem):
    return pltpu.make_async_copy(src_at, dst_at, sem)


def _dispatch_kernel(dest_ref, h2_ref, xs_ref, sem, *, tm):
    for g in range(tm // SUBLANES):
        for u in range(SUBLANES):
            _row_copies(h2_ref.at[g, pl.ds(u, 1)], xs_ref.at[pl.ds(dest_ref[g * SUBLANES + u], 1)],
                        sem).start(priority=u % 2)
    done = xs_ref.at[pl.ds(0, tm)]
    _row_copies(done, done, sem).wait()


def _dispatch(dest, h2, *, tm):
    t_len = h2.shape[0]
    return pl.pallas_call(
        functools.partial(_dispatch_kernel, tm=tm),
        grid=(t_len // tm,),
        in_specs=[pl.BlockSpec((tm,), lambda i: (i,), memory_space=pltpu.SMEM),
                  pl.BlockSpec((tm // SUBLANES, SUBLANES, ROW_W), lambda i: (i, 0, 0))],
        out_specs=pl.BlockSpec(memory_space=pl.ANY),
        out_shape=jax.ShapeDtypeStruct((t_len, ROW_W), F32),
        scratch_shapes=[pltpu.SemaphoreType.DMA(())],
        compiler_params=pltpu.CompilerParams(dimension_semantics=("arbitrary",), vmem_limit_bytes=VMEM_LIMIT),
        name="dispatch",
    )(dest, h2.reshape(t_len // SUBLANES, SUBLANES, ROW_W))


def _expert_kernel(blk_ref, ea_ref, eb_ref, lo_ref, hi_ref, x_ref, wgua_ref, wda_ref, wgub_ref, wdb_ref, y_ref):
    del blk_ref, ea_ref, eb_ref
    i = pl.program_id(0)
    lo = lo_ref[i]
    hi = hi_ref[i]
    first = lo // ROW_SEGMENT
    last = (hi - 1) // ROW_SEGMENT
    start = pl.multiple_of(first * ROW_SEGMENT, ROW_SEGMENT)

    def mlp(xb, wgu_ref, wd_ref):
        gu = _dot(xb, wgu_ref[0])
        g = gu[:, :D_EXPERT]
        return _dot(((g * jax.nn.sigmoid(g)) * gu[:, D_EXPERT:]).astype(BF16), wd_ref[0])

    def run(n_rows):
        rows = pl.ds(start, n_rows)
        xb = x_ref[rows, :D_MODEL].astype(BF16)
        wts = x_ref[rows, D_MODEL:]
        y = wts[:, 0:1] * mlp(xb, wgua_ref, wda_ref) + wts[:, 1:2] * mlp(xb, wgub_ref, wdb_ref)
        row = start + lax.broadcasted_iota(jnp.int32, (n_rows, 1), 0)
        mine = (row >= lo) & (row < hi)
        head = pl.ds(start, ROW_SEGMENT)

        @pl.when(lo == start)
        def _():
            y_ref[head, :] = jnp.where(mine[:ROW_SEGMENT], y[:ROW_SEGMENT], 0.0)

        @pl.when(lo > start)
        def _():
            y_ref[head, :] = jnp.where(mine[:ROW_SEGMENT], y[:ROW_SEGMENT], y_ref[head, :])

        if n_rows > ROW_SEGMENT:
            rest = pl.multiple_of(start + ROW_SEGMENT, ROW_SEGMENT)
            y_ref[pl.ds(rest, n_rows - ROW_SEGMENT), :] = jnp.where(
                mine[ROW_SEGMENT:], y[ROW_SEGMENT:], 0.0)

    for n in range(1, ROW_BLOCK // ROW_SEGMENT + 1):
        @pl.when((hi > lo) & (last - first + 1 == n))
        def _(n=n):
            run(n * ROW_SEGMENT)


def _experts(items, xs, wgu, wd):
    n_rows = xs.shape[0]
    n_items = items[0].shape[0]
    wspec = lambda shape, which: pl.BlockSpec(
        shape, lambda i, blk, ea, eb, lo, hi: ((ea, eb)[which][i], 0, 0))
    grid_spec = pltpu.PrefetchScalarGridSpec(
        num_scalar_prefetch=5,
        grid=(n_items,),
        in_specs=[pl.BlockSpec((ROW_BLOCK, ROW_W), lambda i, blk, ea, eb, lo, hi: (blk[i], 0)),
                  wspec((1, D_MODEL, 2 * D_EXPERT), 0), wspec((1, D_EXPERT, D_MODEL), 0),
                  wspec((1, D_MODEL, 2 * D_EXPERT), 1), wspec((1, D_EXPERT, D_MODEL), 1)],
        out_specs=pl.BlockSpec((ROW_BLOCK, D_MODEL), lambda i, blk, ea, eb, lo, hi: (blk[i], 0)),
    )
    return pl.pallas_call(
        _expert_kernel,
        grid_spec=grid_spec,
        out_shape=jax.ShapeDtypeStruct((n_rows, D_MODEL), F32),
        compiler_params=pltpu.CompilerParams(dimension_semantics=("arbitrary",), vmem_limit_bytes=VMEM_LIMIT),
        name="experts",
    )(*items, xs, wgu, wd, wgu, wd)


def _combine_kernel(dcur_ref, dnxt_ref, x1_ref, g_ref, ys_ref, out_ref, buf_ref, sems, *, tm):
    i = pl.program_id(0)
    slot = i % 2

    def issue(d_ref, s):
        for g in range(tm // SUBLANES):
            for u in range(SUBLANES):
                _row_copies(ys_ref.at[pl.ds(d_ref[g * SUBLANES + u], 1)], buf_ref.at[s, g, pl.ds(u, 1)],
                            sems.at[s]).start(priority=u % 2)

    @pl.when(i == 0)
    def _():
        issue(dcur_ref, 0)

    for s in range(2):
        @pl.when((i + 1 < pl.num_programs(0)) & (slot == 1 - s))
        def _(s=s):
            issue(dnxt_ref, s)

    _row_copies(buf_ref.at[slot], buf_ref.at[slot], sems.at[slot]).wait()
    x2 = x1_ref[...] + buf_ref[slot]
    ms = jnp.mean(x2 * x2, axis=-1, keepdims=True)
    out_ref[...] = (x2 * lax.rsqrt(ms + EPS)) * g_ref[...]


def _combine(dest, x1, gf, ys, *, tm):
    t_len = x1.shape[0]
    n = t_len // tm
    return pl.pallas_call(
        functools.partial(_combine_kernel, tm=tm),
        grid=(n,),
        in_specs=[pl.BlockSpec((tm,), lambda i: (i,), memory_space=pltpu.SMEM),
                  pl.BlockSpec((tm,), lambda i: (jnp.minimum(i + 1, n - 1),), memory_space=pltpu.SMEM),
                  pl.BlockSpec((tm // SUBLANES, SUBLANES, D_MODEL), lambda i: (i, 0, 0)),
                  pl.BlockSpec((1, D_MODEL), lambda i: (0, 0)),
                  pl.BlockSpec(memory_space=pl.ANY)],
        out_specs=pl.BlockSpec((tm // SUBLANES, SUBLANES, D_MODEL), lambda i: (i, 0, 0)),
        out_shape=jax.ShapeDtypeStruct((t_len // SUBLANES, SUBLANES, D_MODEL), F32),
        scratch_shapes=[pltpu.VMEM((2, tm // SUBLANES, SUBLANES, D_MODEL), F32), pltpu.SemaphoreType.DMA((2,))],
        compiler_params=pltpu.CompilerParams(dimension_semantics=("arbitrary",), vmem_limit_bytes=VMEM_LIMIT),
        name="combine",
    )(dest, dest, x1.reshape(t_len // SUBLANES, SUBLANES, D_MODEL), gf, ys)


def _layer(x, norm1_g, w_in, b_f, w_dw, b_dw, conv_ln_g, conv_ln_b, out_g_conv, out_g_att, w_out,
           norm2_g, w_r1, b_r1, w_r2, b_r2, w_gate, w_up, w_down):
    batch, seq, d = x.shape
    t_len = batch * seq
    tm = min(512, seq)
    tk = min(256, seq)
    tq = min(1024, seq)
    o2 = 2 * D_CONV
    o3, o4, o5 = o2 + D_ATT, o2 + 2 * D_ATT, o2 + 3 * D_ATT

    assert PIECE_STRIDE == N_HEADS
    pad = LANES - N_PIECES * PIECE_STRIDE
    wf = jnp.pad(jnp.tile(w_in[:, o5:], (1, N_PIECES)), ((0, 0), (0, pad)))
    wrow = jnp.concatenate([w_in[:, :o2], w_in[:, o3:o4], wf], axis=1).astype(BF16)
    wqvT = jnp.concatenate([w_in[:, o2:o3] * (HEAD_DIM ** -0.5), w_in[:, o4:o5]], axis=1).T.astype(BF16)
    bf3 = jnp.pad(jnp.tile(b_f.astype(F32).reshape(1, N_HEADS), (1, N_PIECES)), ((0, 0), (0, pad)))

    x2 = x.reshape(t_len, d)
    a, kp, qpT, vT, kstat, qstat = _inproj(x2, norm1_g.reshape(1, d), wrow, bf3, wqvT,
                                           batch=batch, seq=seq, tm=tm, tk=tk)
    conv_params = (w_dw, b_dw.reshape(1, -1), conv_ln_g.reshape(1, -1), conv_ln_b.reshape(1, -1),
                   out_g_conv.reshape(1, -1))
    g0, bound = _first_groups(kstat, qstat, tm=tm, tq=tq)
    yT = _attention(g0, bound, qpT, kp, vT, batch=batch, seq=seq, tm=tm, tq=tq, tk=tk)

    gpad = SUBLANES - N_GROUPS
    rpad = LANES - SUBLANES - N_EXPERTS
    wrT = jnp.concatenate([w_r1.T, jnp.zeros((gpad, d), F32),
                           jnp.transpose(w_r2, (0, 2, 1)).reshape(N_EXPERTS, d), jnp.zeros((rpad, d), F32)], axis=0)
    brT = jnp.concatenate([b_r1.astype(F32), jnp.full((gpad,), -jnp.inf, F32), b_r2.reshape(-1).astype(F32),
                           jnp.zeros((rpad,), F32)]).reshape(LANES, 1)
    x1, h2, route, cnt = _outproj(x2, a, yT, conv_params, w_out.astype(BF16),
                                  out_g_att.reshape(1, -1), norm2_g.reshape(1, d),
                                  wrT.astype(BF16), brT, batch=batch, seq=seq, tm=tm, rc=min(256, seq))

    i32 = jnp.int32
    lanes = jnp.arange(LANES, dtype=i32)
    pick = lambda table, idx: jnp.sum(jnp.where(idx[:, None] == lanes, table[None, :], 0), axis=1).astype(i32)
    counts = cnt[:, 0].astype(i32)
    ends = jnp.cumsum(counts).astype(i32)
    starts = ends - counts
    dest = pick(starts, route[0].astype(i32)) + route[1].astype(i32)

    n_blocks = t_len // ROW_BLOCK
    b_lo = starts // ROW_BLOCK
    n_it = jnp.where(counts > 0, (ends - 1) // ROW_BLOCK - b_lo + 1, 0)
    it_end = jnp.cumsum(n_it).astype(i32)
    it_start = it_end - n_it
    idx = jnp.arange(n_blocks + N_CLASSES, dtype=i32)
    valid = idx < it_end[-1]
    last_cls = jnp.max(jnp.where(counts > 0, lanes, 0))
    cls = jnp.where(valid, jnp.sum(it_end[None, :] <= idx[:, None], axis=1).astype(i32), last_cls)
    blk = jnp.where(valid, pick(b_lo, cls) + idx - pick(it_start, cls), n_blocks - 1)
    row0 = blk * ROW_BLOCK
    lo = jnp.where(valid, jnp.maximum(pick(starts, cls), row0) - row0, 0)
    hi = jnp.where(valid, jnp.minimum(pick(ends, cls), row0 + ROW_BLOCK) - row0, 0)
    pair_a, pair_b = [], []
    for g in range(N_GROUPS):
        for a_loc in range(EXPERTS_PER_GROUP):
            for b_loc in range(a_loc + 1, EXPERTS_PER_GROUP):
                pair_a.append(g * EXPERTS_PER_GROUP + a_loc)
                pair_b.append(g * EXPERTS_PER_GROUP + b_loc)
    cpad = [0] * (LANES - N_CLASSES)
    items = (blk.astype(i32), pick(jnp.array(pair_a + cpad, i32), cls), pick(jnp.array(pair_b + cpad, i32), cls),
             lo.astype(i32), hi.astype(i32))

    xs = _dispatch(dest, h2, tm=min(1024, t_len))
    ys = _experts(items, xs, jnp.concatenate([w_gate, w_up], axis=-1).astype(BF16), w_down.astype(BF16))
    return dest, x1, ys


def kernel(x, norm1_g, w_in, b_f, w_dw, b_dw, conv_ln_g, conv_ln_b, out_g_conv, out_g_att, w_out, norm2_g,
           w_r1, b_r1, w_r2, b_r2, w_gate, w_up, w_down, final_g):
    assert norm1_g.shape[0] == 1, "single-layer stack"
    batch, seq, d = x.shape
    dest, x1, ys = _layer(
        x, norm1_g[0], w_in[0], b_f[0], w_dw[0], b_dw[0], conv_ln_g[0], conv_ln_b[0], out_g_conv[0],
        out_g_att[0], w_out[0], norm2_g[0], w_r1[0], b_r1[0], w_r2[0], b_r2[0], w_gate[0], w_up[0], w_down[0])
    out = _combine(dest, x1, final_g.reshape(1, d), ys, tm=min(512, batch * seq))
    return out.reshape(batch, seq, d)
```

```python
import functools

import jax
import jax.numpy as jnp
from jax import lax
from jax.experimental import pallas as pl
from jax.experimental.pallas import tpu as pltpu

D_MODEL = 1024
D_CONV = 512
N_HEADS = 8
HEAD_DIM = 64
D_ATT = N_HEADS * HEAD_DIM
CONV_WIDTH = 31
N_GROUPS = 4
EXPERTS_PER_GROUP = 8
N_EXPERTS = N_GROUPS * EXPERTS_PER_GROUP
D_EXPERT = D_MODEL // 4
ROW_BLOCK = 1024
ROW_SEGMENT = 128
EPS = 1e-6

LANES = 128
SUBLANES = 8
KAUG = 128
N_PIECES = 3
PIECE_STRIDE = 8
CONV_HALO = 32
PAIRS_PER_GROUP = EXPERTS_PER_GROUP * (EXPERTS_PER_GROUP - 1) // 2
N_CLASSES = N_GROUPS * PAIRS_PER_GROUP
ROW_W = D_MODEL + LANES
SKIP_LOG2 = 160.0
FIXED_REF_MAX_BOUND = 40.0
GROUPS_PER_TRIP = 4
NORM_SLACK = 1.02
V_ROWS = 80
LOG2E = 1.4426950408889634
VMEM_LIMIT = 56 * 1024 * 1024

F32 = jnp.float32
BF16 = jnp.bfloat16


def _dot(a, b):
    return jnp.dot(a, b, preferred_element_type=F32)


def _dot_nt(a, b):
    return lax.dot_general(a, b, (((1,), (1,)), ((), ())), preferred_element_type=F32)


def _split3(x):
    hi = x.astype(BF16)
    r1 = x - hi.astype(F32)
    mid = r1.astype(BF16)
    lo = (r1 - mid.astype(F32)).astype(BF16)
    return hi.astype(F32), mid.astype(F32), lo.astype(F32)


def _piece_lane_mask(lane, h):
    return (lane == h) | (lane == h + PIECE_STRIDE) | (lane == h + 2 * PIECE_STRIDE)


def _inproj_kernel(x_ref, g1_ref, wrow_ref, bf_ref, wqvT_ref, ltri_ref, hsel_ref,
                   a_ref, kp_ref, qpT_ref, vT_ref, kstat_ref, qstat_ref, carry_ref, *, tm, tk):
    @pl.when(pl.program_id(1) == 0)
    def _():
        carry_ref[...] = jnp.zeros_like(carry_ref)

    x = x_ref[...]
    ms = jnp.mean(x * x, axis=-1, keepdims=True)
    hb = ((x * lax.rsqrt(ms + EPS)) * g1_ref[...]).astype(BF16)

    z = _dot(hb, wrow_ref[...])
    a_ref[...] = z[:, :D_CONV] * jax.nn.sigmoid(z[:, D_CONV:2 * D_CONV])

    kk = z[:, 2 * D_CONV:2 * D_CONV + D_ATT]

    zf = z[:, 2 * D_CONV + D_ATT:] + bf_ref[...]
    lf = jnp.minimum(zf, 0.0) - jnp.log1p(jnp.exp(-jnp.abs(zf)))
    lane = lax.broadcasted_iota(jnp.int32, (tm, LANES), 1)
    hi, mid, lo = _split3(lf)
    lf3 = jnp.where(lane < PIECE_STRIDE, hi,
                    jnp.where(lane < 2 * PIECE_STRIDE, mid,
                              jnp.where(lane < 3 * PIECE_STRIDE, lo, 0.0))).astype(BF16)
    cs3 = _dot(ltri_ref[...], lf3)
    c = (cs3 + pltpu.roll(cs3, LANES - PIECE_STRIDE, 1)
         + pltpu.roll(cs3, LANES - 2 * PIECE_STRIDE, 1)) + carry_ref[...]
    carry_ref[...] = c[tm - 1:tm, :]

    nhi, nmid, nlo = _split3(c * (-LOG2E))
    p3 = jnp.where(lane < PIECE_STRIDE, nhi,
                   jnp.where(lane < 2 * PIECE_STRIDE, pltpu.roll(nmid, PIECE_STRIDE, 1),
                             jnp.where(lane < 3 * PIECE_STRIDE, pltpu.roll(nlo, 2 * PIECE_STRIDE, 1), 0.0)))
    p3_hi = pltpu.roll(p3, HEAD_DIM, 1)
    q_pieces = (-p3).T[0:HEAD_DIM]
    q_shift = N_PIECES * PIECE_STRIDE

    qvT = _dot_nt(wqvT_ref[...], hb)
    qT = qvT[:D_ATT] * LOG2E
    vT = qvT[D_ATT:]
    row = lax.broadcasted_iota(jnp.int32, (HEAD_DIM, tm), 0)
    vrow = lax.broadcasted_iota(jnp.int32, (V_ROWS - HEAD_DIM, tk), 0)
    v_tail = jnp.where(vrow == 0, 1.0, 0.0).astype(BF16)

    for h in range(N_HEADS):
        kcol = kk[:, (h // 2) * LANES:(h // 2 + 1) * LANES]
        f_h = jnp.where(_piece_lane_mask(row, h), q_pieces, 0.0)
        aug_q = jnp.where(_piece_lane_mask(row, h), 1.0, 0.0) + jnp.concatenate(
            [jnp.zeros((q_shift, tm), F32), f_h[0:HEAD_DIM - q_shift]], axis=0)
        q_h = qT[h * HEAD_DIM:(h + 1) * HEAD_DIM, :]
        if h % 2 == 0:
            ext = jnp.where(_piece_lane_mask(lane, h + HEAD_DIM), p3_hi,
                            jnp.where(_piece_lane_mask(lane, h + HEAD_DIM + q_shift), 1.0, 0.0))
            kp = jnp.where(lane < HEAD_DIM, kcol, ext)
            qp = jnp.concatenate([q_h, aug_q], axis=0)
        else:
            ext = jnp.where(_piece_lane_mask(lane, h), p3,
                            jnp.where(_piece_lane_mask(lane, h + q_shift), 1.0, 0.0))
            kp = jnp.where(lane >= HEAD_DIM, kcol, ext)
            qp = jnp.concatenate([aug_q, q_h], axis=0)
        qpT_ref[0, h] = qp.astype(BF16)
        for cidx in range(tm // tk):
            kp_ref[0, h, cidx] = kp[cidx * tk:(cidx + 1) * tk, :].astype(BF16)
            v_h = vT[h * HEAD_DIM:(h + 1) * HEAD_DIM, cidx * tk:(cidx + 1) * tk].astype(BF16)
            vT_ref[0, h, cidx] = jnp.concatenate([v_h, v_tail], axis=0)

    kn2 = jnp.max(_dot((kk * kk).astype(BF16), hsel_ref[...]), axis=0, keepdims=True)
    srow = lax.broadcasted_iota(jnp.int32, (SUBLANES, LANES), 0)
    kstat_ref[0, 0] = jnp.where(srow == 0, c[0:1, :], jnp.where(srow == 1, c[tm - 1:tm, :],
                                                                 jnp.where(srow == 2, kn2, 0.0)))
    q2 = qT * qT
    qn2 = jnp.concatenate([jnp.sum(q2[h * HEAD_DIM:(h + 1) * HEAD_DIM, :], axis=0, keepdims=True)
                           for h in range(N_HEADS)], axis=0)
    qstat_ref[0, 0] = jnp.broadcast_to(jnp.max(qn2, axis=1, keepdims=True), (N_HEADS, LANES))


def _inproj(x2, g1, wrow, bf3, wqvT, *, batch, seq, tm, tk):
    nt = seq // tm
    nk = seq // tk
    ltri = jnp.tril(jnp.ones((tm, tm), F32)).astype(BF16)
    hsel = (jnp.arange(D_ATT)[:, None] // HEAD_DIM == jnp.arange(LANES)[None, :]).astype(BF16)
    const = lambda shape: pl.BlockSpec(shape, lambda b, t: (0,) * len(shape))
    stat_spec = pl.BlockSpec((1, 1, SUBLANES, LANES), lambda b, t: (b, t, 0, 0))
    stat_shape = jax.ShapeDtypeStruct((batch, nt, SUBLANES, LANES), F32)
    return pl.pallas_call(
        functools.partial(_inproj_kernel, tm=tm, tk=tk),
        grid=(batch, nt),
        in_specs=[
            pl.BlockSpec((tm, D_MODEL), lambda b, t: (b * nt + t, 0)),
            const((1, D_MODEL)), const((D_MODEL, 2 * D_CONV + D_ATT + LANES)), const((1, LANES)),
            const((2 * D_ATT, D_MODEL)), const((tm, tm)), const((D_ATT, LANES)),
        ],
        out_specs=[
            pl.BlockSpec((tm, D_CONV), lambda b, t: (b * nt + t, 0)),
            pl.BlockSpec((1, N_HEADS, tm // tk, tk, KAUG), lambda b, t: (b, 0, t, 0, 0)),
            pl.BlockSpec((1, N_HEADS, KAUG, tm), lambda b, t: (b, 0, 0, t)),
            pl.BlockSpec((1, N_HEADS, tm // tk, V_ROWS, tk), lambda b, t: (b, 0, t, 0, 0)),
            stat_spec, stat_spec,
        ],
        out_shape=[
            jax.ShapeDtypeStruct((batch * seq, D_CONV), F32),
            jax.ShapeDtypeStruct((batch, N_HEADS, nk, tk, KAUG), BF16),
            jax.ShapeDtypeStruct((batch, N_HEADS, KAUG, seq), BF16),
            jax.ShapeDtypeStruct((batch, N_HEADS, nk, V_ROWS, tk), BF16),
            stat_shape, stat_shape,
        ],
        scratch_shapes=[pltpu.VMEM((1, LANES), F32)],
        compiler_params=pltpu.CompilerParams(
            dimension_semantics=("arbitrary", "arbitrary"), vmem_limit_bytes=VMEM_LIMIT),
        name="inproj",
    )(x2, g1, wrow, bf3, wqvT, ltri, hsel)


def _conv_tile(a_ref, w_ref, b_ref, lng_ref, lnb_ref, og_ref, sh_ref, acc_ref, *, tc, rc):
    n_cb = D_CONV // LANES

    @pl.when(pl.program_id(1) == 0)
    def _():
        for cb in range(n_cb):
            sh_ref[0, cb, 0:CONV_HALO, :] = jnp.zeros((CONV_HALO, LANES), F32)

    @pl.when(pl.program_id(1) > 0)
    def _():
        for cb in range(n_cb):
            sh_ref[0, cb, 0:CONV_HALO, :] = sh_ref[0, cb, tc:tc + CONV_HALO, :]

    n_sh = tc + CONV_HALO - SUBLANES
    for cb in range(n_cb):
        sh_ref[0, cb, CONV_HALO:CONV_HALO + tc, :] = a_ref[:, cb * LANES:(cb + 1) * LANES]
        for f in range(1, SUBLANES):
            sh_ref[f, cb, 0:n_sh, :] = sh_ref[0, cb, f:f + n_sh, :]

    base = CONV_HALO - (CONV_WIDTH - 1)
    for cb in range(n_cb):
        cols = slice(cb * LANES, (cb + 1) * LANES)

        def chunk(c, carry, cb=cb, cols=cols):
            r0 = pl.multiple_of(c * rc, rc)
            acc = jnp.zeros((rc, LANES), F32)
            for j in range(CONV_WIDTH):
                f = (base + j) % SUBLANES
                acc = acc + w_ref[j:j + 1, cols] * sh_ref[f, cb, pl.ds(r0 + (base + j - f), rc), :]
            acc_ref[pl.ds(r0, rc), cols] = acc
            return carry

        lax.fori_loop(0, tc // rc, chunk, 0)

    y = acc_ref[...] + b_ref[...]
    mu = jnp.mean(y, axis=-1, keepdims=True)
    yc = y - mu
    var = jnp.mean(yc * yc, axis=-1, keepdims=True)
    yn = yc * lax.rsqrt(var + EPS) * lng_ref[...] + lnb_ref[...]
    s = yn * jax.nn.sigmoid(yn)
    ms = jnp.mean(s * s, axis=-1, keepdims=True)
    return (s * lax.rsqrt(ms + EPS) * og_ref[...]).astype(BF16)


def _attn_fixed_ref_kernel(g0_ref, bnd_ref, qT_ref, k_ref, v_ref, o_ref, *, tq, tk, unit):
    i = pl.program_id(2)
    qT = qT_ref[0, 0]
    n_sub = tq // tk
    bnd = bnd_ref[0, 0, 0:1, 0:1]
    sb0 = unit * g0_ref[(pl.program_id(0) * N_HEADS + pl.program_id(1)) * pl.num_programs(2) + i]

    def values(sb, n):
        return jnp.concatenate([v_ref[0, 0, sb + d] for d in range(n)], axis=1)

    def probs(sb, n):
        return jnp.concatenate([jnp.exp2(_dot(k_ref[0, 0, sb + d], qT) - bnd).astype(BF16)
                                for d in range(n)], axis=0)

    def accumulate(acc, sb, n):
        for lo in range(0, n, n_sub):
            m = min(n_sub, n - lo)
            acc = acc + _dot(values(sb + lo, m), probs(sb + lo, m))
        return acc

    def loop_trip(p, acc):
        return accumulate(acc, sb0 + GROUPS_PER_TRIP * n_sub * p, GROUPS_PER_TRIP * n_sub)

    key = lax.broadcasted_iota(jnp.int32, (tk, tk), 0)
    qry = lax.broadcasted_iota(jnp.int32, (tk, tk), 1)

    def diagonal(acc):
        rows = []
        for d in range(n_sub):
            s = _dot(k_ref[0, 0, i * n_sub + d], qT[:, d * tk:])
            tiles = [jnp.zeros((tk, d * tk), BF16)] if d else []
            tiles.append(jnp.exp2(jnp.where(key <= qry, s[:, :tk], -jnp.inf) - bnd).astype(BF16))
            if d + 1 < n_sub:
                tiles.append(jnp.exp2(s[:, tk:] - bnd).astype(BF16))
            rows.append(jnp.concatenate(tiles, axis=1))
        acc = acc + _dot(values(i * n_sub, n_sub), jnp.concatenate(rows, axis=0))
        o_ref[0] = acc[:HEAD_DIM] / acc[HEAD_DIM:HEAD_DIM + 1]

    n_before = i * n_sub - sb0
    per_trip = GROUPS_PER_TRIP * n_sub
    acc = lax.fori_loop(0, n_before // per_trip, loop_trip, jnp.zeros((V_ROWS, tq), F32))
    left = n_before % per_trip

    for n in range(0, per_trip, unit):
        @pl.when(left == n)
        def _(n=n):
            diagonal(accumulate(acc, i * n_sub - n, n))


def _attn_kernel(g0_ref, qT_ref, k_ref, v_ref, o_ref, sa_ref, sb_ref, cma_ref, cmb_ref, *, tq, tk):
    i = pl.program_id(2)
    qT = qT_ref[0, 0]
    n_sub = tq // tk

    def scores(g, s_ref, cm_ref):
        for d in range(n_sub):
            s = _dot(k_ref[0, 0, g * n_sub + d], qT)
            s_ref[d] = s
            cm_ref[d] = jnp.max(s, axis=0, keepdims=True)

    def consume(g, s_ref, cm_ref, carry, nxt=None):
        m, acc = carry
        m_new = m
        for d in range(n_sub):
            m_new = jnp.maximum(m_new, cm_ref[d])
        ps = []
        for d in range(n_sub):
            if nxt is not None:
                g_n, sn_ref, cmn_ref = nxt
                s = _dot(k_ref[0, 0, g_n * n_sub + d], qT)
                sn_ref[d] = s
                cmn_ref[d] = jnp.max(s, axis=0, keepdims=True)
            ps.append(jnp.exp2(s_ref[d] - m_new).astype(BF16))
        p = jnp.concatenate(ps, axis=0)
        v = jnp.concatenate([v_ref[0, 0, g * n_sub + d] for d in range(n_sub)], axis=1)
        return m_new, jnp.exp2(m - m_new) * acc + _dot(v, p)

    key = lax.broadcasted_iota(jnp.int32, (tk, tk), 0)
    qry = lax.broadcasted_iota(jnp.int32, (tk, tk), 1)

    def consume_diag(s_ref, cm_ref, carry):
        m, acc = carry
        lanes = lambda j: slice(j * tk, (j + 1) * tk)
        causal = [jnp.where(key <= qry, s_ref[d, :, lanes(d)], -jnp.inf) for d in range(n_sub)]
        m_tiles = []
        for j in range(n_sub):
            m_j = jnp.maximum(m[:, lanes(j)], jnp.max(causal[j], axis=0, keepdims=True))
            for d in range(j):
                m_j = jnp.maximum(m_j, cm_ref[d, :, lanes(j)])
            m_tiles.append(m_j)
        m_new = jnp.concatenate(m_tiles, axis=1)
        rows = []
        for d in range(n_sub):
            tiles = [jnp.zeros((tk, d * tk), BF16)] if d else []
            tiles.append(jnp.exp2(causal[d] - m_tiles[d]).astype(BF16))
            if d + 1 < n_sub:
                rest = slice((d + 1) * tk, tq)
                tiles.append(jnp.exp2(s_ref[d, :, rest] - m_new[:, rest]).astype(BF16))
            rows.append(jnp.concatenate(tiles, axis=1))
        p = jnp.concatenate(rows, axis=0)
        v = jnp.concatenate([v_ref[0, 0, i * n_sub + d] for d in range(n_sub)], axis=1)
        acc = jnp.exp2(m - m_new) * acc + _dot(v, p)
        o_ref[0] = acc[:HEAD_DIM] / acc[HEAD_DIM:HEAD_DIM + 1]

    g0 = g0_ref[(pl.program_id(0) * N_HEADS + pl.program_id(1)) * pl.num_programs(2) + i]
    n_full = i - g0

    def pair(p, carry):
        g = g0 + 2 * p
        carry = consume(g, sa_ref, cma_ref, carry, nxt=(g + 1, sb_ref, cmb_ref))
        return consume(g + 1, sb_ref, cmb_ref, carry, nxt=(g + 2, sa_ref, cma_ref))

    scores(g0, sa_ref, cma_ref)
    init = (jnp.full((1, tq), -jnp.inf, F32), jnp.zeros((V_ROWS, tq), F32))
    carry = lax.fori_loop(0, n_full // 2, pair, init)

    @pl.when(n_full % 2 == 1)
    def _():
        consume_diag(sb_ref, cmb_ref, consume(i - 1, sa_ref, cma_ref, carry, nxt=(i, sb_ref, cmb_ref)))

    @pl.when(n_full % 2 == 0)
    def _():
        consume_diag(sa_ref, cma_ref, carry)


def _first_groups(kstat, qstat, *, tm, tq):
    per = tq // tm
    bound = jnp.sqrt(jnp.max(kstat[:, :, 2, :N_HEADS], axis=1) * jnp.max(qstat[:, :, :, 0], axis=1))
    bound = bound * NORM_SLACK
    cq = kstat[:, ::per, 0, :N_HEADS]
    ck = kstat[:, :, 1, :N_HEADS]
    d = (cq[:, :, None, :] - ck[:, None, :, :]) * LOG2E
    nq, ng = cq.shape[1], ck.shape[1]
    earlier = per * jnp.arange(nq)[None, :, None, None] > jnp.arange(ng)[None, None, :, None]
    skip = earlier & (2.0 * bound[:, None, None, :] + d < -SKIP_LOG2)
    g0 = jnp.sum(jnp.cumprod(skip.astype(jnp.int32), axis=2), axis=2)
    return jnp.transpose(g0, (0, 2, 1)).reshape(-1).astype(jnp.int32), bound


def _attention(g0, bound, qpT, kp, vT, *, batch, seq, tm, tq, tk):
    nk = seq // tk
    params = pltpu.CompilerParams(
        dimension_semantics=("arbitrary", "arbitrary", "arbitrary"), vmem_limit_bytes=VMEM_LIMIT)
    out_shape = jax.ShapeDtypeStruct((batch, D_ATT, seq), F32)
    qkv_specs = [
        pl.BlockSpec((1, 1, KAUG, tq), lambda b, h, i, g0: (b, h, 0, i)),
        pl.BlockSpec((1, 1, nk, tk, KAUG), lambda b, h, i, g0: (b, h, 0, 0, 0)),
        pl.BlockSpec((1, 1, nk, V_ROWS, tk), lambda b, h, i, g0: (b, h, 0, 0, 0)),
    ]
    out_spec = pl.BlockSpec((1, HEAD_DIM, tq), lambda b, h, i, g0: (b, h, i))

    def fixed_ref(g0, bnd, qpT, kp, vT):
        grid_spec = pltpu.PrefetchScalarGridSpec(
            num_scalar_prefetch=1, grid=(batch, N_HEADS, seq // tq),
            in_specs=[pl.BlockSpec((1, 1, SUBLANES, LANES), lambda b, h, i, g0: (b, h, 0, 0))] + qkv_specs,
            out_specs=out_spec)
        return pl.pallas_call(functools.partial(_attn_fixed_ref_kernel, tq=tq, tk=tk, unit=tm // tk),
                              grid_spec=grid_spec, out_shape=out_shape, compiler_params=params,
                              name="attention_fixed_ref")(g0, bnd, qpT, kp, vT)

    def running_max(g0, bnd, qpT, kp, vT):
        del bnd
        grid_spec = pltpu.PrefetchScalarGridSpec(
            num_scalar_prefetch=1, grid=(batch, N_HEADS, seq // tq), in_specs=qkv_specs, out_specs=out_spec,
            scratch_shapes=[pltpu.VMEM((tq // tk, tk, tq), F32), pltpu.VMEM((tq // tk, tk, tq), F32),
                            pltpu.VMEM((tq // tk, 1, tq), F32), pltpu.VMEM((tq // tk, 1, tq), F32)])
        return pl.pallas_call(functools.partial(_attn_kernel, tq=tq, tk=tk), grid_spec=grid_spec,
                              out_shape=out_shape, compiler_params=params, name="attention",
                              )(g0 // (tq // tm), qpT, kp, vT)

    bnd = jnp.broadcast_to(bound[:, :, None, None], (batch, N_HEADS, SUBLANES, LANES))
    return lax.cond(jnp.max(bound) <= FIXED_REF_MAX_BOUND, fixed_ref, running_max, g0, bnd, qpT, kp, vT)


def _outproj_kernel(x_ref, a_ref, yT_ref, wdw_ref, bdw_ref, lng_ref, lnb_ref, gc_ref, wo_ref, ga_ref,
                    g2_ref, wrT_ref, brT_ref, utri_ref, x1_ref, h2_ref, route_ref, cnt_ref,
                    carry_ref, sh_ref, acc_ref, *, tm, rc):
    first = (pl.program_id(0) == 0) & (pl.program_id(1) == 0)

    @pl.when(first)
    def _():
        carry_ref[...] = jnp.zeros_like(carry_ref)

    mc = _conv_tile(a_ref, wdw_ref, bdw_ref, lng_ref, lnb_ref, gc_ref, sh_ref, acc_ref, tc=tm, rc=rc)
    yT = yT_ref[0]
    msa = jnp.mean(yT * yT, axis=0, keepdims=True)
    yn = (yT * lax.rsqrt(msa + EPS)).T * ga_ref[...]
    x1 = x_ref[...] + _dot(jnp.concatenate([mc, yn.astype(BF16)], axis=1), wo_ref[...])
    x1_ref[...] = x1
    ms = jnp.mean(x1 * x1, axis=-1, keepdims=True)
    h2 = (x1 * lax.rsqrt(ms + EPS)) * g2_ref[...]
    h2_ref[:, :D_MODEL] = h2

    lg = _dot_nt(wrT_ref[...], h2.astype(BF16)) + brT_ref[...]
    neg = -jnp.inf
    sub = lax.broadcasted_iota(jnp.int32, (SUBLANES, tm), 0)
    big = jnp.int32(SUBLANES)
    first_of = lambda hit: jnp.min(jnp.where(hit, sub, big), axis=0, keepdims=True)

    lg1 = lg[0:SUBLANES]
    m1 = jnp.max(lg1, axis=0, keepdims=True)
    p1_sel = 1.0 / jnp.sum(jnp.exp(lg1 - m1), axis=0, keepdims=True)
    grp = first_of(lg1 == m1)

    slab = lambda g: lg[SUBLANES * (g + 1):SUBLANES * (g + 2)]
    v = slab(N_GROUPS - 1)
    for g in range(N_GROUPS - 2, -1, -1):
        v = jnp.where(grp == g, slab(g), v)
    v1 = jnp.max(v, axis=0, keepdims=True)
    j1 = first_of(v == v1)
    vv = jnp.where(sub == j1, neg, v)
    v2 = jnp.max(vv, axis=0, keepdims=True)
    j2 = first_of(vv == v2)
    e21 = jnp.exp(v2 - v1)
    w0 = p1_sel / (1.0 + e21)
    w1 = p1_sel * e21 / (1.0 + e21)

    swap = j2 < j1
    al = jnp.where(swap, j2, j1)
    bl = jnp.where(swap, j1, j2)
    wa = jnp.where(swap, w1, w0)
    wb = jnp.where(swap, w0, w1)
    cid = PAIRS_PER_GROUP * grp + ((al * (2 * EXPERTS_PER_GROUP - 1 - al)) >> 1) + (bl - al - 1)

    cls = lax.broadcasted_iota(jnp.int32, (LANES, tm), 0)
    oh = cls == cid
    cmat = jnp.where(oh, 1.0, 0.0)
    prefix = _dot(cmat.astype(BF16), utri_ref[...]) + carry_ref[...]
    rank = jnp.sum(jnp.where(oh, prefix, 0.0), axis=0, keepdims=True)
    counts = prefix[:, tm - 1:tm] + cmat[:, tm - 1:tm]
    carry_ref[...] = counts
    cnt_ref[...] = counts

    h2_ref[:, D_MODEL:] = jnp.where(cls == 0, wa, jnp.where(cls == 1, wb, 0.0)).T
    route_ref[...] = jnp.where(sub == 0, cid.astype(F32), jnp.where(sub == 1, rank, 0.0))


def _outproj(x2, a, yT, conv_params, wo, ga, g2, wrT, brT, *, batch, seq, tm, rc):
    nt = seq // tm
    utri = jnp.triu(jnp.ones((tm, tm), F32), 1).astype(BF16)
    const = lambda shape: pl.BlockSpec(shape, lambda b, t: (0,) * len(shape))
    row_spec = lambda w: pl.BlockSpec((tm, w), lambda b, t: (b * nt + t, 0))
    return pl.pallas_call(
        functools.partial(_outproj_kernel, tm=tm, rc=rc),
        grid=(batch, nt),
        in_specs=[row_spec(D_MODEL), row_spec(D_CONV),
                  pl.BlockSpec((1, D_ATT, tm), lambda b, t: (b, 0, t)),
                  const((CONV_WIDTH, D_CONV)), const((1, D_CONV)), const((1, D_CONV)), const((1, D_CONV)),
                  const((1, D_CONV)),
                  const((D_CONV + D_ATT, D_MODEL)), const((1, D_ATT)), const((1, D_MODEL)),
                  const((LANES, D_MODEL)), const((LANES, 1)), const((tm, tm))],
        out_specs=[row_spec(D_MODEL), row_spec(ROW_W),
                   pl.BlockSpec((SUBLANES, tm), lambda b, t: (0, b * nt + t)), const((LANES, 1))],
        out_shape=[jax.ShapeDtypeStruct((batch * seq, D_MODEL), F32),
                   jax.ShapeDtypeStruct((batch * seq, ROW_W), F32),
                   jax.ShapeDtypeStruct((SUBLANES, batch * seq), F32),
                   jax.ShapeDtypeStruct((LANES, 1), F32)],
        scratch_shapes=[pltpu.VMEM((LANES, 1), F32),
                        pltpu.VMEM((SUBLANES, D_CONV // LANES, tm + CONV_HALO, LANES), F32),
                        pltpu.VMEM((tm, D_CONV), F32)],
        compiler_params=pltpu.CompilerParams(
            dimension_semantics=("arbitrary", "arbitrary"), vmem_limit_bytes=VMEM_LIMIT),
        name="outproj",
    )(x2, a, yT, *conv_params, wo, ga, g2, wrT, brT, utri)


def _row_copies(src_at, dst_at, sem):
    return pltpu.make_async_copy(src_at, dst_at, sem)


def _dispatch_kernel(dest_ref, h2_ref, xs_ref, sem, *, tm):
    for g in range(tm // SUBLANES):
        for u in range(SUBLANES):
            _row_copies(h2_ref.at[g, pl.ds(u, 1)], xs_ref.at[pl.ds(dest_ref[g * SUBLANES + u], 1)],
                        sem).start(priority=u % 2)
    done = xs_ref.at[pl.ds(0, tm)]
    _row_copies(done, done, sem).wait()


def _dispatch(dest, h2, *, tm):
    t_len = h2.shape[0]
    return pl.pallas_call(
        functools.partial(_dispatch_kernel, tm=tm),
        grid=(t_len // tm,),
        in_specs=[pl.BlockSpec((tm,), lambda i: (i,), memory_space=pltpu.SMEM),
                  pl.BlockSpec((tm // SUBLANES, SUBLANES, ROW_W), lambda i: (i, 0, 0))],
        out_specs=pl.BlockSpec(memory_space=pl.ANY),
        out_shape=jax.ShapeDtypeStruct((t_len, ROW_W), F32),
        scratch_shapes=[pltpu.SemaphoreType.DMA(())],
        compiler_params=pltpu.CompilerParams(dimension_semantics=("arbitrary",), vmem_limit_bytes=VMEM_LIMIT),
        name="dispatch",
    )(dest, h2.reshape(t_len // SUBLANES, SUBLANES, ROW_W))


def _expert_kernel(blk_ref, ea_ref, eb_ref, lo_ref, hi_ref, x_ref, wgua_ref, wda_ref, wgub_ref, wdb_ref, y_ref):
    del blk_ref, ea_ref, eb_ref
    i = pl.program_id(0)
    lo = lo_ref[i]
    hi = hi_ref[i]
    first = lo // ROW_SEGMENT
    last = (hi - 1) // ROW_SEGMENT
    start = pl.multiple_of(first * ROW_SEGMENT, ROW_SEGMENT)

    def mlp(xb, wgu_ref, wd_ref):
        gu = _dot(xb, wgu_ref[0])
        g = gu[:, :D_EXPERT]
        return _dot(((g * jax.nn.sigmoid(g)) * gu[:, D_EXPERT:]).astype(BF16), wd_ref[0])

    def run(n_rows):
        rows = pl.ds(start, n_rows)
        xb = x_ref[rows, :D_MODEL].astype(BF16)
        wts = x_ref[rows, D_MODEL:]
        y = wts[:, 0:1] * mlp(xb, wgua_ref, wda_ref) + wts[:, 1:2] * mlp(xb, wgub_ref, wdb_ref)
        row = start + lax.broadcasted_iota(jnp.int32, (n_rows, 1), 0)
        mine = (row >= lo) & (row < hi)
        head = pl.ds(start, ROW_SEGMENT)

        @pl.when(lo == start)
        def _():
            y_ref[head, :] = jnp.where(mine[:ROW_SEGMENT], y[:ROW_SEGMENT], 0.0)

        @pl.when(lo > start)
        def _():
            y_ref[head, :] = jnp.where(mine[:ROW_SEGMENT], y[:ROW_SEGMENT], y_ref[head, :])

        if n_rows > ROW_SEGMENT:
            rest = pl.multiple_of(start + ROW_SEGMENT, ROW_SEGMENT)
            y_ref[pl.ds(rest, n_rows - ROW_SEGMENT), :] = jnp.where(
                mine[ROW_SEGMENT:], y[ROW_SEGMENT:], 0.0)

    for n in range(1, ROW_BLOCK // ROW_SEGMENT + 1):
        @pl.when((hi > lo) & (last - first + 1 == n))
        def _(n=n):
            run(n * ROW_SEGMENT)


def _experts(items, xs, wgu, wd):
    n_rows = xs.shape[0]
    n_items = items[0].shape[0]
    wspec = lambda shape, which: pl.BlockSpec(
        shape, lambda i, blk, ea, eb, lo, hi: ((ea, eb)[which][i], 0, 0))
    grid_spec = pltpu.PrefetchScalarGridSpec(
        num_scalar_prefetch=5,
        grid=(n_items,),
        in_specs=[pl.BlockSpec((ROW_BLOCK, ROW_W), lambda i, blk, ea, eb, lo, hi: (blk[i], 0)),
                  wspec((1, D_MODEL, 2 * D_EXPERT), 0), wspec((1, D_EXPERT, D_MODEL), 0),
                  wspec((1, D_MODEL, 2 * D_EXPERT), 1), wspec((1, D_EXPERT, D_MODEL), 1)],
        out_specs=pl.BlockSpec((ROW_BLOCK, D_MODEL), lambda i, blk, ea, eb, lo, hi: (blk[i], 0)),
    )
    return pl.pallas_call(
        _expert_kernel,
        grid_spec=grid_spec,
        out_shape=jax.ShapeDtypeStruct((n_rows, D_MODEL), F32),
        compiler_params=pltpu.CompilerParams(dimension_semantics=("arbitrary",), vmem_limit_bytes=VMEM_LIMIT),
        name="experts",
    )(*items, xs, wgu, wd, wgu, wd)


def _combine_kernel(dcur_ref, dnxt_ref, x1_ref, g_ref, ys_ref, out_ref, buf0_ref, buf1_ref, sems, *, tm):
    k = pl.program_id(0)
    bufs = (buf0_ref, buf1_ref)
    groups = tm // SUBLANES

    def issue(d_ref, first, s):
        for g in range(groups):
            for u in range(SUBLANES):
                _row_copies(ys_ref.at[pl.ds(d_ref[first + g * SUBLANES + u], 1)], bufs[s].at[g, pl.ds(u, 1)],
                            sems.at[s]).start(priority=u % 2)

    def gathered(s):
        _row_copies(bufs[s], bufs[s], sems.at[s]).wait()

    def finish(s):
        rows = slice(s * groups, (s + 1) * groups)
        x2 = x1_ref[rows] + bufs[s][...]
        ms = jnp.mean(x2 * x2, axis=-1, keepdims=True)
        out_ref[rows] = (x2 * lax.rsqrt(ms + EPS)) * g_ref[...]

    @pl.when(k == 0)
    def _():
        issue(dcur_ref, 0, 0)

    gathered(0)
    issue(dcur_ref, tm, 1)
    finish(0)
    gathered(1)
    issue(dnxt_ref, 0, 0)
    finish(1)

    @pl.when(k == pl.num_programs(0) - 1)
    def _():
        gathered(0)


def _combine(dest, x1, gf, ys, *, tm):
    t_len = x1.shape[0]
    n = t_len // tm
    assert n % 2 == 0
    tile = pl.BlockSpec((2 * tm // SUBLANES, SUBLANES, D_MODEL), lambda k: (k, 0, 0))
    return pl.pallas_call(
        functools.partial(_combine_kernel, tm=tm),
        grid=(n // 2,),
        in_specs=[pl.BlockSpec((2 * tm,), lambda k: (k,), memory_space=pltpu.SMEM),
                  pl.BlockSpec((tm,), lambda k: (jnp.minimum(2 * k + 2, n - 1),), memory_space=pltpu.SMEM),
                  tile,
                  pl.BlockSpec((1, D_MODEL), lambda k: (0, 0)),
                  pl.BlockSpec(memory_space=pl.ANY)],
        out_specs=tile,
        out_shape=jax.ShapeDtypeStruct((t_len // SUBLANES, SUBLANES, D_MODEL), F32),
        scratch_shapes=[pltpu.VMEM((tm // SUBLANES, SUBLANES, D_MODEL), F32),
                        pltpu.VMEM((tm // SUBLANES, SUBLANES, D_MODEL), F32), pltpu.SemaphoreType.DMA((2,))],
        compiler_params=pltpu.CompilerParams(dimension_semantics=("arbitrary",), vmem_limit_bytes=VMEM_LIMIT),
        name="combine",
    )(dest, dest, x1.reshape(t_len // SUBLANES, SUBLANES, D_MODEL), gf, ys)


def _layer(x, norm1_g, w_in, b_f, w_dw, b_dw, conv_ln_g, conv_ln_b, out_g_conv, out_g_att, w_out,
           norm2_g, w_r1, b_r1, w_r2, b_r2, w_gate, w_up, w_down):
    batch, seq, d = x.shape
    t_len = batch * seq
    tm = min(512, seq)
    tk = min(256, seq)
    tq = min(1024, seq)
    o2 = 2 * D_CONV
    o3, o4, o5 = o2 + D_ATT, o2 + 2 * D_ATT, o2 + 3 * D_ATT

    assert PIECE_STRIDE == N_HEADS
    pad = LANES - N_PIECES * PIECE_STRIDE
    wf = jnp.pad(jnp.tile(w_in[:, o5:], (1, N_PIECES)), ((0, 0), (0, pad)))
    wrow = jnp.concatenate([w_in[:, :o2], w_in[:, o3:o4], wf], axis=1).astype(BF16)
    wqvT = jnp.concatenate([w_in[:, o2:o3] * (HEAD_DIM ** -0.5), w_in[:, o4:o5]], axis=1).T.astype(BF16)
    bf3 = jnp.pad(jnp.tile(b_f.astype(F32).reshape(1, N_HEADS), (1, N_PIECES)), ((0, 0), (0, pad)))

    x2 = x.reshape(t_len, d)
    a, kp, qpT, vT, kstat, qstat = _inproj(x2, norm1_g.reshape(1, d), wrow, bf3, wqvT,
                                           batch=batch, seq=seq, tm=tm, tk=tk)
    conv_params = (w_dw, b_dw.reshape(1, -1), conv_ln_g.reshape(1, -1), conv_ln_b.reshape(1, -1),
                   out_g_conv.reshape(1, -1))
    g0, bound = _first_groups(kstat, qstat, tm=tm, tq=tq)
    yT = _attention(g0, bound, qpT, kp, vT, batch=batch, seq=seq, tm=tm, tq=tq, tk=tk)

    gpad = SUBLANES - N_GROUPS
    rpad = LANES - SUBLANES - N_EXPERTS
    wrT = jnp.concatenate([w_r1.T, jnp.zeros((gpad, d), F32),
                           jnp.transpose(w_r2, (0, 2, 1)).reshape(N_EXPERTS, d), jnp.zeros((rpad, d), F32)], axis=0)
    brT = jnp.concatenate([b_r1.astype(F32), jnp.full((gpad,), -jnp.inf, F32), b_r2.reshape(-1).astype(F32),
                           jnp.zeros((rpad,), F32)]).reshape(LANES, 1)
    x1, h2, route, cnt = _outproj(x2, a, yT, conv_params, w_out.astype(BF16),
                                  out_g_att.reshape(1, -1), norm2_g.reshape(1, d),
                                  wrT.astype(BF16), brT, batch=batch, seq=seq, tm=tm, rc=min(256, seq))

    i32 = jnp.int32
    lanes = jnp.arange(LANES, dtype=i32)
    pick = lambda table, idx: jnp.sum(jnp.where(idx[:, None] == lanes, table[None, :], 0), axis=1).astype(i32)
    counts = cnt[:, 0].astype(i32)
    ends = jnp.cumsum(counts).astype(i32)
    starts = ends - counts
    dest = pick(starts, route[0].astype(i32)) + route[1].astype(i32)

    n_blocks = t_len // ROW_BLOCK
    b_lo = starts // ROW_BLOCK
    n_it = jnp.where(counts > 0, (ends - 1) // ROW_BLOCK - b_lo + 1, 0)
    it_end = jnp.cumsum(n_it).astype(i32)
    it_start = it_end - n_it
    idx = jnp.arange(n_blocks + N_CLASSES, dtype=i32)
    valid = idx < it_end[-1]
    last_cls = jnp.max(jnp.where(counts > 0, lanes, 0))
    cls = jnp.where(valid, jnp.sum(it_end[None, :] <= idx[:, None], axis=1).astype(i32), last_cls)
    blk = jnp.where(valid, pick(b_lo, cls) + idx - pick(it_start, cls), n_blocks - 1)
    row0 = blk * ROW_BLOCK
    lo = jnp.where(valid, jnp.maximum(pick(starts, cls), row0) - row0, 0)
    hi = jnp.where(valid, jnp.minimum(pick(ends, cls), row0 + ROW_BLOCK) - row0, 0)
    pair_a, pair_b = [], []
    for g in range(N_GROUPS):
        for a_loc in range(EXPERTS_PER_GROUP):
            for b_loc in range(a_loc + 1, EXPERTS_PER_GROUP):
                pair_a.append(g * EXPERTS_PER_GROUP + a_loc)
                pair_b.append(g * EXPERTS_PER_GROUP + b_loc)
    cpad = [0] * (LANES - N_CLASSES)
    items = (blk.astype(i32), pick(jnp.array(pair_a + cpad, i32), cls), pick(jnp.array(pair_b + cpad, i32), cls),
             lo.astype(i32), hi.astype(i32))

    xs = _dispatch(dest, h2, tm=min(1024, t_len))
    ys = _experts(items, xs, jnp.concatenate([w_gate, w_up], axis=-1).astype(BF16), w_down.astype(BF16))
    return dest, x1, ys


def kernel(x, norm1_g, w_in, b_f, w_dw, b_dw, conv_ln_g, conv_ln_b, out_g_conv, out_g_att, w_out, norm2_g,
           w_r1, b_r1, w_r2, b_r2, w_gate, w_up, w_down, final_g):
    assert norm1_g.shape[0] == 1, "single-layer stack"
    batch, seq, d = x.shape
    dest, x1, ys = _layer(
        x, norm1_g[0], w_in[0], b_f[0], w_dw[0], b_dw[0], conv_ln_g[0], conv_ln_b[0], out_g_conv[0],
        out_g_att[0], w_out[0], norm2_g[0], w_r1[0], b_r1[0], w_r2[0], b_r2[0], w_gate[0], w_up[0], w_down[0])
    out = _combine(dest, x1, final_g.reshape(1, d), ys, tm=min(512, batch * seq))
    return out.reshape(batch, seq, d)
```

```python
import functools

import jax
import jax.numpy as jnp
from jax import lax
from jax.experimental import pallas as pl
from jax.experimental.pallas import tpu as pltpu

D_MODEL = 1024
D_CONV = 512
N_HEADS = 8
HEAD_DIM = 64
D_ATT = N_HEADS * HEAD_DIM
CONV_WIDTH = 31
N_GROUPS = 4
EXPERTS_PER_GROUP = 8
N_EXPERTS = N_GROUPS * EXPERTS_PER_GROUP
D_EXPERT = D_MODEL // 4
ROW_BLOCK = 1024
ROW_SEGMENT = 128
EPS = 1e-6

LANES = 128
SUBLANES = 8
KAUG = 128
N_PIECES = 3
PIECE_STRIDE = 8
CONV_HALO = 32
PAIRS_PER_GROUP = EXPERTS_PER_GROUP * (EXPERTS_PER_GROUP - 1) // 2
N_CLASSES = N_GROUPS * PAIRS_PER_GROUP
ROW_W = D_MODEL + LANES
SKIP_LOG2 = 160.0
FIXED_REF_MAX_BOUND = 40.0
GROUPS_PER_TRIP = 4
NORM_SLACK = 1.02
V_ROWS = 80
LOG2E = 1.4426950408889634
VMEM_LIMIT = 56 * 1024 * 1024

F32 = jnp.float32
BF16 = jnp.bfloat16


def _dot(a, b):
    return jnp.dot(a, b, preferred_element_type=F32)


def _dot_nt(a, b):
    return lax.dot_general(a, b, (((1,), (1,)), ((), ())), preferred_element_type=F32)


def _split3(x):
    hi = x.astype(BF16)
    r1 = x - hi.astype(F32)
    mid = r1.astype(BF16)
    lo = (r1 - mid.astype(F32)).astype(BF16)
    return hi.astype(F32), mid.astype(F32), lo.astype(F32)


def _piece_lane_mask(lane, h):
    return (lane == h) | (lane == h + PIECE_STRIDE) | (lane == h + 2 * PIECE_STRIDE)


def _inproj_kernel(x_ref, g1_ref, wrow_ref, bf_ref, wqvT_ref, ltri_ref, hsel_ref,
                   a_ref, kp_ref, qpT_ref, vT_ref, kstat_ref, qstat_ref, carry_ref, *, tm, tk):
    @pl.when(pl.program_id(1) == 0)
    def _():
        carry_ref[...] = jnp.zeros_like(carry_ref)

    x = x_ref[...]
    ms = jnp.mean(x * x, axis=-1, keepdims=True)
    hb = ((x * lax.rsqrt(ms + EPS)) * g1_ref[...]).astype(BF16)

    z = _dot(hb, wrow_ref[...])
    a_ref[...] = z[:, :D_CONV] * jax.nn.sigmoid(z[:, D_CONV:2 * D_CONV])

    kk = z[:, 2 * D_CONV:2 * D_CONV + D_ATT]

    zf = z[:, 2 * D_CONV + D_ATT:] + bf_ref[...]
    lf = jnp.minimum(zf, 0.0) - jnp.log1p(jnp.exp(-jnp.abs(zf)))
    lane = lax.broadcasted_iota(jnp.int32, (tm, LANES), 1)
    hi, mid, lo = _split3(lf)
    lf3 = jnp.where(lane < PIECE_STRIDE, hi,
                    jnp.where(lane < 2 * PIECE_STRIDE, mid,
                              jnp.where(lane < 3 * PIECE_STRIDE, lo, 0.0))).astype(BF16)
    cs3 = _dot(ltri_ref[...], lf3)
    c = (cs3 + pltpu.roll(cs3, LANES - PIECE_STRIDE, 1)
         + pltpu.roll(cs3, LANES - 2 * PIECE_STRIDE, 1)) + carry_ref[...]
    carry_ref[...] = c[tm - 1:tm, :]

    nhi, nmid, nlo = _split3(c * (-LOG2E))
    p3 = jnp.where(lane < PIECE_STRIDE, nhi,
                   jnp.where(lane < 2 * PIECE_STRIDE, pltpu.roll(nmid, PIECE_STRIDE, 1),
                             jnp.where(lane < 3 * PIECE_STRIDE, pltpu.roll(nlo, 2 * PIECE_STRIDE, 1), 0.0)))
    p3_hi = pltpu.roll(p3, HEAD_DIM, 1)
    q_pieces = (-p3).T[0:HEAD_DIM]
    q_shift = N_PIECES * PIECE_STRIDE

    qvT = _dot_nt(wqvT_ref[...], hb)
    qT = qvT[:D_ATT] * LOG2E
    vT = qvT[D_ATT:]
    row = lax.broadcasted_iota(jnp.int32, (HEAD_DIM, tm), 0)
    vrow = lax.broadcasted_iota(jnp.int32, (V_ROWS - HEAD_DIM, tk), 0)
    v_tail = jnp.where(vrow == 0, 1.0, 0.0).astype(BF16)

    for h in range(N_HEADS):
        kcol = kk[:, (h // 2) * LANES:(h // 2 + 1) * LANES]
        f_h = jnp.where(_piece_lane_mask(row, h), q_pieces, 0.0)
        aug_q = jnp.where(_piece_lane_mask(row, h), 1.0, 0.0) + jnp.concatenate(
            [jnp.zeros((q_shift, tm), F32), f_h[0:HEAD_DIM - q_shift]], axis=0)
        q_h = qT[h * HEAD_DIM:(h + 1) * HEAD_DIM, :]
        if h % 2 == 0:
            ext = jnp.where(_piece_lane_mask(lane, h + HEAD_DIM), p3_hi,
                            jnp.where(_piece_lane_mask(lane, h + HEAD_DIM + q_shift), 1.0, 0.0))
            kp = jnp.where(lane < HEAD_DIM, kcol, ext)
            qp = jnp.concatenate([q_h, aug_q], axis=0)
        else:
            ext = jnp.where(_piece_lane_mask(lane, h), p3,
                            jnp.where(_piece_lane_mask(lane, h + q_shift), 1.0, 0.0))
            kp = jnp.where(lane >= HEAD_DIM, kcol, ext)
            qp = jnp.concatenate([aug_q, q_h], axis=0)
        qpT_ref[0, h] = qp.astype(BF16)
        for cidx in range(tm // tk):
            kp_ref[0, h, cidx] = kp[cidx * tk:(cidx + 1) * tk, :].astype(BF16)
            v_h = vT[h * HEAD_DIM:(h + 1) * HEAD_DIM, cidx * tk:(cidx + 1) * tk].astype(BF16)
            vT_ref[0, h, cidx] = jnp.concatenate([v_h, v_tail], axis=0)

    kn2 = jnp.max(_dot((kk * kk).astype(BF16), hsel_ref[...]), axis=0, keepdims=True)
    srow = lax.broadcasted_iota(jnp.int32, (SUBLANES, LANES), 0)
    kstat_ref[0, 0] = jnp.where(srow == 0, c[0:1, :], jnp.where(srow == 1, c[tm - 1:tm, :],
                                                                 jnp.where(srow == 2, kn2, 0.0)))
    q2 = qT * qT
    qn2 = jnp.concatenate([jnp.sum(q2[h * HEAD_DIM:(h + 1) * HEAD_DIM, :], axis=0, keepdims=True)
                           for h in range(N_HEADS)], axis=0)
    qstat_ref[0, 0] = jnp.broadcast_to(jnp.max(qn2, axis=1, keepdims=True), (N_HEADS, LANES))


def _inproj(x2, g1, wrow, bf3, wqvT, *, batch, seq, tm, tk):
    nt = seq // tm
    nk = seq // tk
    ltri = jnp.tril(jnp.ones((tm, tm), F32)).astype(BF16)
    hsel = (jnp.arange(D_ATT)[:, None] // HEAD_DIM == jnp.arange(LANES)[None, :]).astype(BF16)
    const = lambda shape: pl.BlockSpec(shape, lambda b, t: (0,) * len(shape))
    stat_spec = pl.BlockSpec((1, 1, SUBLANES, LANES), lambda b, t: (b, t, 0, 0))
    stat_shape = jax.ShapeDtypeStruct((batch, nt, SUBLANES, LANES), F32)
    return pl.pallas_call(
        functools.partial(_inproj_kernel, tm=tm, tk=tk),
        grid=(batch, nt),
        in_specs=[
            pl.BlockSpec((tm, D_MODEL), lambda b, t: (b * nt + t, 0)),
            const((1, D_MODEL)), const((D_MODEL, 2 * D_CONV + D_ATT + LANES)), const((1, LANES)),
            const((2 * D_ATT, D_MODEL)), const((tm, tm)), const((D_ATT, LANES)),
        ],
        out_specs=[
            pl.BlockSpec((tm, D_CONV), lambda b, t: (b * nt + t, 0)),
            pl.BlockSpec((1, N_HEADS, tm // tk, tk, KAUG), lambda b, t: (b, 0, t, 0, 0)),
            pl.BlockSpec((1, N_HEADS, KAUG, tm), lambda b, t: (b, 0, 0, t)),
            pl.BlockSpec((1, N_HEADS, tm // tk, V_ROWS, tk), lambda b, t: (b, 0, t, 0, 0)),
            stat_spec, stat_spec,
        ],
        out_shape=[
            jax.ShapeDtypeStruct((batch * seq, D_CONV), F32),
            jax.ShapeDtypeStruct((batch, N_HEADS, nk, tk, KAUG), BF16),
            jax.ShapeDtypeStruct((batch, N_HEADS, KAUG, seq), BF16),
            jax.ShapeDtypeStruct((batch, N_HEADS, nk, V_ROWS, tk), BF16),
            stat_shape, stat_shape,
        ],
        scratch_shapes=[pltpu.VMEM((1, LANES), F32)],
        compiler_params=pltpu.CompilerParams(
            dimension_semantics=("arbitrary", "arbitrary"), vmem_limit_bytes=VMEM_LIMIT),
        name="inproj",
    )(x2, g1, wrow, bf3, wqvT, ltri, hsel)


def _conv_tile(a_ref, w_ref, b_ref, lng_ref, lnb_ref, og_ref, sh_ref, acc_ref, *, tc, rc):
    n_cb = D_CONV // LANES

    @pl.when(pl.program_id(1) == 0)
    def _():
        for cb in range(n_cb):
            sh_ref[0, cb, 0:CONV_HALO, :] = jnp.zeros((CONV_HALO, LANES), F32)

    @pl.when(pl.program_id(1) > 0)
    def _():
        for cb in range(n_cb):
            sh_ref[0, cb, 0:CONV_HALO, :] = sh_ref[0, cb, tc:tc + CONV_HALO, :]

    n_sh = tc + CONV_HALO - SUBLANES
    for cb in range(n_cb):
        sh_ref[0, cb, CONV_HALO:CONV_HALO + tc, :] = a_ref[:, cb * LANES:(cb + 1) * LANES]
        for f in range(1, SUBLANES):
            sh_ref[f, cb, 0:n_sh, :] = sh_ref[0, cb, f:f + n_sh, :]

    base = CONV_HALO - (CONV_WIDTH - 1)
    for cb in range(n_cb):
        cols = slice(cb * LANES, (cb + 1) * LANES)

        def chunk(c, carry, cb=cb, cols=cols):
            r0 = pl.multiple_of(c * rc, rc)
            acc = jnp.zeros((rc, LANES), F32)
            for j in range(CONV_WIDTH):
                f = (base + j) % SUBLANES
                acc = acc + w_ref[j:j + 1, cols] * sh_ref[f, cb, pl.ds(r0 + (base + j - f), rc), :]
            acc_ref[pl.ds(r0, rc), cols] = acc
            return carry

        lax.fori_loop(0, tc // rc, chunk, 0)

    y = acc_ref[...] + b_ref[...]
    mu = jnp.mean(y, axis=-1, keepdims=True)
    yc = y - mu
    var = jnp.mean(yc * yc, axis=-1, keepdims=True)
    yn = yc * lax.rsqrt(var + EPS) * lng_ref[...] + lnb_ref[...]
    s = yn * jax.nn.sigmoid(yn)
    ms = jnp.mean(s * s, axis=-1, keepdims=True)
    return (s * lax.rsqrt(ms + EPS) * og_ref[...]).astype(BF16)


def _attn_fixed_ref_kernel(g0_ref, bnd_ref, qT_ref, k_ref, v_ref, o_ref, *, tq, tk, unit):
    i = pl.program_id(2)
    qT = qT_ref[0, 0]
    n_sub = tq // tk
    bnd = bnd_ref[0, 0, 0:1, 0:1]
    sb0 = unit * g0_ref[(pl.program_id(0) * N_HEADS + pl.program_id(1)) * pl.num_programs(2) + i]

    def values(sb, n):
        return jnp.concatenate([v_ref[0, 0, sb + d] for d in range(n)], axis=1)

    def probs(sb, n):
        return jnp.concatenate([jnp.exp2(_dot(k_ref[0, 0, sb + d], qT) - bnd).astype(BF16)
                                for d in range(n)], axis=0)

    def accumulate(acc, sb, n):
        for lo in range(0, n, n_sub):
            m = min(n_sub, n - lo)
            acc = acc + _dot(values(sb + lo, m), probs(sb + lo, m))
        return acc

    def loop_trip(p, acc):
        return accumulate(acc, sb0 + GROUPS_PER_TRIP * n_sub * p, GROUPS_PER_TRIP * n_sub)

    key = lax.broadcasted_iota(jnp.int32, (tk, tk), 0)
    qry = lax.broadcasted_iota(jnp.int32, (tk, tk), 1)

    def diagonal(acc):
        rows = []
        for d in range(n_sub):
            s = _dot(k_ref[0, 0, i * n_sub + d], qT[:, d * tk:])
            tiles = [jnp.zeros((tk, d * tk), BF16)] if d else []
            tiles.append(jnp.exp2(jnp.where(key <= qry, s[:, :tk], -jnp.inf) - bnd).astype(BF16))
            if d + 1 < n_sub:
                tiles.append(jnp.exp2(s[:, tk:] - bnd).astype(BF16))
            rows.append(jnp.concatenate(tiles, axis=1))
        acc = acc + _dot(values(i * n_sub, n_sub), jnp.concatenate(rows, axis=0))
        o_ref[0] = acc[:HEAD_DIM] / acc[HEAD_DIM:HEAD_DIM + 1]

    n_before = i * n_sub - sb0
    per_trip = GROUPS_PER_TRIP * n_sub
    acc = lax.fori_loop(0, n_before // per_trip, loop_trip, jnp.zeros((V_ROWS, tq), F32))
    left = n_before % per_trip

    for n in range(0, per_trip, unit):
        @pl.when(left == n)
        def _(n=n):
            diagonal(accumulate(acc, i * n_sub - n, n))


def _attn_kernel(g0_ref, qT_ref, k_ref, v_ref, o_ref, sa_ref, sb_ref, cma_ref, cmb_ref, *, tq, tk):
    i = pl.program_id(2)
    qT = qT_ref[0, 0]
    n_sub = tq // tk

    def scores(g, s_ref, cm_ref):
        for d in range(n_sub):
            s = _dot(k_ref[0, 0, g * n_sub + d], qT)
            s_ref[d] = s
            cm_ref[d] = jnp.max(s, axis=0, keepdims=True)

    def consume(g, s_ref, cm_ref, carry, nxt=None):
        m, acc = carry
        m_new = m
        for d in range(n_sub):
            m_new = jnp.maximum(m_new, cm_ref[d])
        ps = []
        for d in range(n_sub):
            if nxt is not None:
                g_n, sn_ref, cmn_ref = nxt
                s = _dot(k_ref[0, 0, g_n * n_sub + d], qT)
                sn_ref[d] = s
                cmn_ref[d] = jnp.max(s, axis=0, keepdims=True)
            ps.append(jnp.exp2(s_ref[d] - m_new).astype(BF16))
        p = jnp.concatenate(ps, axis=0)
        v = jnp.concatenate([v_ref[0, 0, g * n_sub + d] for d in range(n_sub)], axis=1)
        return m_new, jnp.exp2(m - m_new) * acc + _dot(v, p)

    key = lax.broadcasted_iota(jnp.int32, (tk, tk), 0)
    qry = lax.broadcasted_iota(jnp.int32, (tk, tk), 1)

    def consume_diag(s_ref, cm_ref, carry):
        m, acc = carry
        lanes = lambda j: slice(j * tk, (j + 1) * tk)
        causal = [jnp.where(key <= qry, s_ref[d, :, lanes(d)], -jnp.inf) for d in range(n_sub)]
        m_tiles = []
        for j in range(n_sub):
            m_j = jnp.maximum(m[:, lanes(j)], jnp.max(causal[j], axis=0, keepdims=True))
            for d in range(j):
                m_j = jnp.maximum(m_j, cm_ref[d, :, lanes(j)])
            m_tiles.append(m_j)
        m_new = jnp.concatenate(m_tiles, axis=1)
        rows = []
        for d in range(n_sub):
            tiles = [jnp.zeros((tk, d * tk), BF16)] if d else []
            tiles.append(jnp.exp2(causal[d] - m_tiles[d]).astype(BF16))
            if d + 1 < n_sub:
                rest = slice((d + 1) * tk, tq)
                tiles.append(jnp.exp2(s_ref[d, :, rest] - m_new[:, rest]).astype(BF16))
            rows.append(jnp.concatenate(tiles, axis=1))
        p = jnp.concatenate(rows, axis=0)
        v = jnp.concatenate([v_ref[0, 0, i * n_sub + d] for d in range(n_sub)], axis=1)
        acc = jnp.exp2(m - m_new) * acc + _dot(v, p)
        o_ref[0] = acc[:HEAD_DIM] / acc[HEAD_DIM:HEAD_DIM + 1]

    g0 = g0_ref[(pl.program_id(0) * N_HEADS + pl.program_id(1)) * pl.num_programs(2) + i]
    n_full = i - g0

    def pair(p, carry):
        g = g0 + 2 * p
        carry = consume(g, sa_ref, cma_ref, carry, nxt=(g + 1, sb_ref, cmb_ref))
        return consume(g + 1, sb_ref, cmb_ref, carry, nxt=(g + 2, sa_ref, cma_ref))

    scores(g0, sa_ref, cma_ref)
    init = (jnp.full((1, tq), -jnp.inf, F32), jnp.zeros((V_ROWS, tq), F32))
    carry = lax.fori_loop(0, n_full // 2, pair, init)

    @pl.when(n_full % 2 == 1)
    def _():
        consume_diag(sb_ref, cmb_ref, consume(i - 1, sa_ref, cma_ref, carry, nxt=(i, sb_ref, cmb_ref)))

    @pl.when(n_full % 2 == 0)
    def _():
        consume_diag(sa_ref, cma_ref, carry)


def _first_groups(kstat, qstat, *, tm, tq):
    per = tq // tm
    bound = jnp.sqrt(jnp.max(kstat[:, :, 2, :N_HEADS], axis=1) * jnp.max(qstat[:, :, :, 0], axis=1))
    bound = bound * NORM_SLACK
    cq = kstat[:, ::per, 0, :N_HEADS]
    ck = kstat[:, :, 1, :N_HEADS]
    d = (cq[:, :, None, :] - ck[:, None, :, :]) * LOG2E
    nq, ng = cq.shape[1], ck.shape[1]
    earlier = per * jnp.arange(nq)[None, :, None, None] > jnp.arange(ng)[None, None, :, None]
    skip = earlier & (2.0 * bound[:, None, None, :] + d < -SKIP_LOG2)
    g0 = jnp.sum(jnp.cumprod(skip.astype(jnp.int32), axis=2), axis=2)
    return jnp.transpose(g0, (0, 2, 1)).reshape(-1).astype(jnp.int32), bound


def _attention(g0, bound, qpT, kp, vT, *, batch, seq, tm, tq, tk):
    nk = seq // tk
    params = pltpu.CompilerParams(
        dimension_semantics=("arbitrary", "arbitrary", "arbitrary"), vmem_limit_bytes=VMEM_LIMIT)
    out_shape = jax.ShapeDtypeStruct((batch, D_ATT, seq), F32)
    qkv_specs = [
        pl.BlockSpec((1, 1, KAUG, tq), lambda b, h, i, g0: (b, h, 0, i)),
        pl.BlockSpec((1, 1, nk, tk, KAUG), lambda b, h, i, g0: (b, h, 0, 0, 0)),
        pl.BlockSpec((1, 1, nk, V_ROWS, tk), lambda b, h, i, g0: (b, h, 0, 0, 0)),
    ]
    out_spec = pl.BlockSpec((1, HEAD_DIM, tq), lambda b, h, i, g0: (b, h, i))

    def fixed_ref(g0, bnd, qpT, kp, vT):
        grid_spec = pltpu.PrefetchScalarGridSpec(
            num_scalar_prefetch=1, grid=(batch, N_HEADS, seq // tq),
            in_specs=[pl.BlockSpec((1, 1, SUBLANES, LANES), lambda b, h, i, g0: (b, h, 0, 0))] + qkv_specs,
            out_specs=out_spec)
        return pl.pallas_call(functools.partial(_attn_fixed_ref_kernel, tq=tq, tk=tk, unit=tm // tk),
                              grid_spec=grid_spec, out_shape=out_shape, compiler_params=params,
                              name="attention_fixed_ref")(g0, bnd, qpT, kp, vT)

    def running_max(g0, bnd, qpT, kp, vT):
        del bnd
        grid_spec = pltpu.PrefetchScalarGridSpec(
            num_scalar_prefetch=1, grid=(batch, N_HEADS, seq // tq), in_specs=qkv_specs, out_specs=out_spec,
            scratch_shapes=[pltpu.VMEM((tq // tk, tk, tq), F32), pltpu.VMEM((tq // tk, tk, tq), F32),
                            pltpu.VMEM((tq // tk, 1, tq), F32), pltpu.VMEM((tq // tk, 1, tq), F32)])
        return pl.pallas_call(functools.partial(_attn_kernel, tq=tq, tk=tk), grid_spec=grid_spec,
                              out_shape=out_shape, compiler_params=params, name="attention",
                              )(g0 // (tq // tm), qpT, kp, vT)

    bnd = jnp.broadcast_to(bound[:, :, None, None], (batch, N_HEADS, SUBLANES, LANES))
    return lax.cond(jnp.max(bound) <= FIXED_REF_MAX_BOUND, fixed_ref, running_max, g0, bnd, qpT, kp, vT)


def _outproj_kernel(x_ref, a_ref, yT_ref, wdw_ref, bdw_ref, lng_ref, lnb_ref, gc_ref, wo_ref, ga_ref,
                    g2_ref, wrT_ref, brT_ref, utri_ref, x1_ref, h2_ref, route_ref, cnt_ref,
                    carry_ref, sh_ref, acc_ref, *, tm, rc):
    first = (pl.program_id(0) == 0) & (pl.program_id(1) == 0)

    @pl.when(first)
    def _():
        carry_ref[...] = jnp.zeros_like(carry_ref)

    mc = _conv_tile(a_ref, wdw_ref, bdw_ref, lng_ref, lnb_ref, gc_ref, sh_ref, acc_ref, tc=tm, rc=rc)
    yT = yT_ref[0]
    msa = jnp.mean(yT * yT, axis=0, keepdims=True)
    yn = (yT * lax.rsqrt(msa + EPS)).T * ga_ref[...]
    x1 = x_ref[...] + _dot(jnp.concatenate([mc, yn.astype(BF16)], axis=1), wo_ref[...])
    x1_ref[...] = x1
    ms = jnp.mean(x1 * x1, axis=-1, keepdims=True)
    h2 = (x1 * lax.rsqrt(ms + EPS)) * g2_ref[...]
    h2_ref[:, :D_MODEL] = h2

    lg = _dot_nt(wrT_ref[...], h2.astype(BF16)) + brT_ref[...]
    neg = -jnp.inf
    sub = lax.broadcasted_iota(jnp.int32, (SUBLANES, tm), 0)
    big = jnp.int32(SUBLANES)
    first_of = lambda hit: jnp.min(jnp.where(hit, sub, big), axis=0, keepdims=True)

    lg1 = lg[0:SUBLANES]
    m1 = jnp.max(lg1, axis=0, keepdims=True)
    p1_sel = 1.0 / jnp.sum(jnp.exp(lg1 - m1), axis=0, keepdims=True)
    grp = first_of(lg1 == m1)

    slab = lambda g: lg[SUBLANES * (g + 1):SUBLANES * (g + 2)]
    v = slab(N_GROUPS - 1)
    for g in range(N_GROUPS - 2, -1, -1):
        v = jnp.where(grp == g, slab(g), v)
    v1 = jnp.max(v, axis=0, keepdims=True)
    j1 = first_of(v == v1)
    vv = jnp.where(sub == j1, neg, v)
    v2 = jnp.max(vv, axis=0, keepdims=True)
    j2 = first_of(vv == v2)
    e21 = jnp.exp(v2 - v1)
    w0 = p1_sel / (1.0 + e21)
    w1 = p1_sel * e21 / (1.0 + e21)

    swap = j2 < j1
    al = jnp.where(swap, j2, j1)
    bl = jnp.where(swap, j1, j2)
    wa = jnp.where(swap, w1, w0)
    wb = jnp.where(swap, w0, w1)
    cid = PAIRS_PER_GROUP * grp + ((al * (2 * EXPERTS_PER_GROUP - 1 - al)) >> 1) + (bl - al - 1)

    cls = lax.broadcasted_iota(jnp.int32, (LANES, tm), 0)
    oh = cls == cid
    cmat = jnp.where(oh, 1.0, 0.0)
    prefix = _dot(cmat.astype(BF16), utri_ref[...]) + carry_ref[...]
    rank = jnp.sum(jnp.where(oh, prefix, 0.0), axis=0, keepdims=True)
    counts = prefix[:, tm - 1:tm] + cmat[:, tm - 1:tm]
    carry_ref[...] = counts
    cnt_ref[...] = counts

    h2_ref[:, D_MODEL:] = jnp.where(cls == 0, wa, jnp.where(cls == 1, wb, 0.0)).T
    route_ref[...] = jnp.where(sub == 0, cid.astype(F32), jnp.where(sub == 1, rank, 0.0))


def _outproj(x2, a, yT, conv_params, wo, ga, g2, wrT, brT, *, batch, seq, tm, rc):
    nt = seq // tm
    utri = jnp.triu(jnp.ones((tm, tm), F32), 1).astype(BF16)
    const = lambda shape: pl.BlockSpec(shape, lambda b, t: (0,) * len(shape))
    row_spec = lambda w: pl.BlockSpec((tm, w), lambda b, t: (b * nt + t, 0))
    return pl.pallas_call(
        functools.partial(_outproj_kernel, tm=tm, rc=rc),
        grid=(batch, nt),
        in_specs=[row_spec(D_MODEL), row_spec(D_CONV),
                  pl.BlockSpec((1, D_ATT, tm), lambda b, t: (b, 0, t)),
                  const((CONV_WIDTH, D_CONV)), const((1, D_CONV)), const((1, D_CONV)), const((1, D_CONV)),
                  const((1, D_CONV)),
                  const((D_CONV + D_ATT, D_MODEL)), const((1, D_ATT)), const((1, D_MODEL)),
                  const((LANES, D_MODEL)), const((LANES, 1)), const((tm, tm))],
        out_specs=[row_spec(D_MODEL), row_spec(ROW_W),
                   pl.BlockSpec((SUBLANES, tm), lambda b, t: (0, b * nt + t)), const((LANES, 1))],
        out_shape=[jax.ShapeDtypeStruct((batch * seq, D_MODEL), F32),
                   jax.ShapeDtypeStruct((batch * seq, ROW_W), F32),
                   jax.ShapeDtypeStruct((SUBLANES, batch * seq), F32),
                   jax.ShapeDtypeStruct((LANES, 1), F32)],
        scratch_shapes=[pltpu.VMEM((LANES, 1), F32),
                        pltpu.VMEM((SUBLANES, D_CONV // LANES, tm + CONV_HALO, LANES), F32),
                        pltpu.VMEM((tm, D_CONV), F32)],
        compiler_params=pltpu.CompilerParams(
            dimension_semantics=("arbitrary", "arbitrary"), vmem_limit_bytes=VMEM_LIMIT),
        name="outproj",
    )(x2, a, yT, *conv_params, wo, ga, g2, wrT, brT, utri)


def _row_copies(src_at, dst_at, sem):
    return pltpu.make_async_copy(src_at, dst_at, sem)


def _dispatch_kernel(dest_ref, h2_ref, xs_ref, sem, *, tm):
    for g in range(tm // SUBLANES):
        for u in range(SUBLANES):
            _row_copies(h2_ref.at[g, pl.ds(u, 1)], xs_ref.at[pl.ds(dest_ref[g * SUBLANES + u], 1)],
                        sem).start(priority=u % 2)
    done = xs_ref.at[pl.ds(0, tm)]
    _row_copies(done, done, sem).wait()


def _dispatch(dest, h2, *, tm):
    t_len = h2.shape[0]
    return pl.pallas_call(
        functools.partial(_dispatch_kernel, tm=tm),
        grid=(t_len // tm,),
        in_specs=[pl.BlockSpec((tm,), lambda i: (i,), memory_space=pltpu.SMEM),
                  pl.BlockSpec((tm // SUBLANES, SUBLANES, ROW_W), lambda i: (i, 0, 0))],
        out_specs=pl.BlockSpec(memory_space=pl.ANY),
        out_shape=jax.ShapeDtypeStruct((t_len, ROW_W), F32),
        scratch_shapes=[pltpu.SemaphoreType.DMA(())],
        compiler_params=pltpu.CompilerParams(dimension_semantics=("arbitrary",), vmem_limit_bytes=VMEM_LIMIT),
        name="dispatch",
    )(dest, h2.reshape(t_len // SUBLANES, SUBLANES, ROW_W))


def _expert_kernel(blk_ref, ea_ref, eb_ref, lo_ref, hi_ref, x_ref, wgua_ref, wda_ref, wgub_ref, wdb_ref, y_ref):
    del blk_ref, ea_ref, eb_ref
    i = pl.program_id(0)
    lo = lo_ref[i]
    hi = hi_ref[i]
    first = lo // ROW_SEGMENT
    last = (hi - 1) // ROW_SEGMENT
    start = pl.multiple_of(first * ROW_SEGMENT, ROW_SEGMENT)

    def mlp(xb, wgu_ref, wd_ref):
        gu = _dot(xb, wgu_ref[0])
        g = gu[:, :D_EXPERT]
        return _dot(((g * jax.nn.sigmoid(g)) * gu[:, D_EXPERT:]).astype(BF16), wd_ref[0])

    def run(n_rows):
        rows = pl.ds(start, n_rows)
        xb = x_ref[rows, :D_MODEL].astype(BF16)
        wts = x_ref[rows, D_MODEL:]
        y = wts[:, 0:1] * mlp(xb, wgua_ref, wda_ref) + wts[:, 1:2] * mlp(xb, wgub_ref, wdb_ref)
        row = start + lax.broadcasted_iota(jnp.int32, (n_rows, 1), 0)
        mine = (row >= lo) & (row < hi)
        head = pl.ds(start, ROW_SEGMENT)

        @pl.when(lo == start)
        def _():
            y_ref[head, :] = jnp.where(mine[:ROW_SEGMENT], y[:ROW_SEGMENT], 0.0)

        @pl.when(lo > start)
        def _():
            y_ref[head, :] = jnp.where(mine[:ROW_SEGMENT], y[:ROW_SEGMENT], y_ref[head, :])

        if n_rows > ROW_SEGMENT:
            rest = pl.multiple_of(start + ROW_SEGMENT, ROW_SEGMENT)
            y_ref[pl.ds(rest, n_rows - ROW_SEGMENT), :] = jnp.where(
                mine[ROW_SEGMENT:], y[ROW_SEGMENT:], 0.0)

    for n in range(1, ROW_BLOCK // ROW_SEGMENT + 1):
        @pl.when((hi > lo) & (last - first + 1 == n))
        def _(n=n):
            run(n * ROW_SEGMENT)


def _experts(items, xs, wgu, wd):
    n_rows = xs.shape[0]
    n_items = items[0].shape[0]
    wspec = lambda shape, which: pl.BlockSpec(
        shape, lambda i, blk, ea, eb, lo, hi: ((ea, eb)[which][i], 0, 0))
    grid_spec = pltpu.PrefetchScalarGridSpec(
        num_scalar_prefetch=5,
        grid=(n_items,),
        in_specs=[pl.BlockSpec((ROW_BLOCK, ROW_W), lambda i, blk, ea, eb, lo, hi: (blk[i], 0)),
                  wspec((1, D_MODEL, 2 * D_EXPERT), 0), wspec((1, D_EXPERT, D_MODEL), 0),
                  wspec((1, D_MODEL, 2 * D_EXPERT), 1), wspec((1, D_EXPERT, D_MODEL), 1)],
        out_specs=pl.BlockSpec((ROW_BLOCK, D_MODEL), lambda i, blk, ea, eb, lo, hi: (blk[i], 0)),
    )
    return pl.pallas_call(
        _expert_kernel,
        grid_spec=grid_spec,
        out_shape=jax.ShapeDtypeStruct((n_rows, D_MODEL), F32),
        compiler_params=pltpu.CompilerParams(dimension_semantics=("arbitrary",), vmem_limit_bytes=VMEM_LIMIT),
        name="experts",
    )(*items, xs, wgu, wd, wgu, wd)


COMBINE_TILES = 4
COMBINE_AHEAD = 2


def _combine_kernel(dcur_ref, dnxt_ref, x1_ref, g_ref, ys_ref, out_ref, *scratch, tm):
    k = pl.program_id(0)
    bufs, sems = scratch[:COMBINE_TILES], scratch[COMBINE_TILES]
    groups = tm // SUBLANES

    def issue(d_ref, first, s):
        for g in range(groups):
            for u in range(SUBLANES):
                _row_copies(ys_ref.at[pl.ds(d_ref[first + g * SUBLANES + u], 1)], bufs[s].at[g, pl.ds(u, 1)],
                            sems.at[s]).start(priority=u % 2)

    def gathered(s):
        _row_copies(bufs[s], bufs[s], sems.at[s]).wait()

    def finish(s):
        rows = slice(s * groups, (s + 1) * groups)
        x2 = x1_ref[rows] + bufs[s][...]
        ms = jnp.mean(x2 * x2, axis=-1, keepdims=True)
        out_ref[rows] = (x2 * lax.rsqrt(ms + EPS)) * g_ref[...]

    @pl.when(k == 0)
    def _():
        for m in range(COMBINE_AHEAD):
            issue(dcur_ref, m * tm, m)

    for m in range(COMBINE_TILES):
        gathered(m)
        ahead = m + COMBINE_AHEAD
        if ahead < COMBINE_TILES:
            issue(dcur_ref, ahead * tm, ahead)
        else:
            issue(dnxt_ref, (ahead - COMBINE_TILES) * tm, ahead - COMBINE_TILES)
        finish(m)

    @pl.when(k == pl.num_programs(0) - 1)
    def _():
        for m in range(COMBINE_AHEAD):
            gathered(m)


def _combine(dest, x1, gf, ys, *, tm):
    t_len = x1.shape[0]
    steps = t_len // (COMBINE_TILES * tm)
    assert steps * COMBINE_TILES * tm == t_len and COMBINE_TILES % COMBINE_AHEAD == 0
    per = COMBINE_TILES // COMBINE_AHEAD
    tile = pl.BlockSpec((COMBINE_TILES * tm // SUBLANES, SUBLANES, D_MODEL), lambda k: (k, 0, 0))
    buf = pltpu.VMEM((tm // SUBLANES, SUBLANES, D_MODEL), F32)
    return pl.pallas_call(
        functools.partial(_combine_kernel, tm=tm),
        grid=(steps,),
        in_specs=[pl.BlockSpec((COMBINE_TILES * tm,), lambda k: (k,), memory_space=pltpu.SMEM),
                  pl.BlockSpec((COMBINE_AHEAD * tm,), lambda k: (jnp.minimum(per * (k + 1), per * steps - 1),),
                               memory_space=pltpu.SMEM),
                  tile,
                  pl.BlockSpec((1, D_MODEL), lambda k: (0, 0)),
                  pl.BlockSpec(memory_space=pl.ANY)],
        out_specs=tile,
        out_shape=jax.ShapeDtypeStruct((t_len // SUBLANES, SUBLANES, D_MODEL), F32),
        scratch_shapes=[buf] * COMBINE_TILES + [pltpu.SemaphoreType.DMA((COMBINE_TILES,))],
        compiler_params=pltpu.CompilerParams(dimension_semantics=("arbitrary",), vmem_limit_bytes=VMEM_LIMIT),
        name="combine",
    )(dest, dest, x1.reshape(t_len // SUBLANES, SUBLANES, D_MODEL), gf, ys)


def _layer(x, norm1_g, w_in, b_f, w_dw, b_dw, conv_ln_g, conv_ln_b, out_g_conv, out_g_att, w_out,
           norm2_g, w_r1, b_r1, w_r2, b_r2, w_gate, w_up, w_down):
    batch, seq, d = x.shape
    t_len = batch * seq
    tm = min(512, seq)
    tk = min(256, seq)
    tq = min(1024, seq)
    o2 = 2 * D_CONV
    o3, o4, o5 = o2 + D_ATT, o2 + 2 * D_ATT, o2 + 3 * D_ATT

    assert PIECE_STRIDE == N_HEADS
    pad = LANES - N_PIECES * PIECE_STRIDE
    wf = jnp.pad(jnp.tile(w_in[:, o5:], (1, N_PIECES)), ((0, 0), (0, pad)))
    wrow = jnp.concatenate([w_in[:, :o2], w_in[:, o3:o4], wf], axis=1).astype(BF16)
    wqvT = jnp.concatenate([w_in[:, o2:o3] * (HEAD_DIM ** -0.5), w_in[:, o4:o5]], axis=1).T.astype(BF16)
    bf3 = jnp.pad(jnp.tile(b_f.astype(F32).reshape(1, N_HEADS), (1, N_PIECES)), ((0, 0), (0, pad)))

    x2 = x.reshape(t_len, d)
    a, kp, qpT, vT, kstat, qstat = _inproj(x2, norm1_g.reshape(1, d), wrow, bf3, wqvT,
                                           batch=batch, seq=seq, tm=tm, tk=tk)
    conv_params = (w_dw, b_dw.reshape(1, -1), conv_ln_g.reshape(1, -1), conv_ln_b.reshape(1, -1),
                   out_g_conv.reshape(1, -1))
    g0, bound = _first_groups(kstat, qstat, tm=tm, tq=tq)
    yT = _attention(g0, bound, qpT, kp, vT, batch=batch, seq=seq, tm=tm, tq=tq, tk=tk)

    gpad = SUBLANES - N_GROUPS
    rpad = LANES - SUBLANES - N_EXPERTS
    wrT = jnp.concatenate([w_r1.T, jnp.zeros((gpad, d), F32),
                           jnp.transpose(w_r2, (0, 2, 1)).reshape(N_EXPERTS, d), jnp.zeros((rpad, d), F32)], axis=0)
    brT = jnp.concatenate([b_r1.astype(F32), jnp.full((gpad,), -jnp.inf, F32), b_r2.reshape(-1).astype(F32),
                           jnp.zeros((rpad,), F32)]).reshape(LANES, 1)
    x1, h2, route, cnt = _outproj(x2, a, yT, conv_params, w_out.astype(BF16),
                                  out_g_att.reshape(1, -1), norm2_g.reshape(1, d),
                                  wrT.astype(BF16), brT, batch=batch, seq=seq, tm=tm, rc=min(256, seq))

    i32 = jnp.int32
    lanes = jnp.arange(LANES, dtype=i32)
    pick = lambda table, idx: jnp.sum(jnp.where(idx[:, None] == lanes, table[None, :], 0), axis=1).astype(i32)
    counts = cnt[:, 0].astype(i32)
    ends = jnp.cumsum(counts).astype(i32)
    starts = ends - counts
    dest = pick(starts, route[0].astype(i32)) + route[1].astype(i32)

    n_blocks = t_len // ROW_BLOCK
    b_lo = starts // ROW_BLOCK
    n_it = jnp.where(counts > 0, (ends - 1) // ROW_BLOCK - b_lo + 1, 0)
    it_end = jnp.cumsum(n_it).astype(i32)
    it_start = it_end - n_it
    idx = jnp.arange(n_blocks + N_CLASSES, dtype=i32)
    valid = idx < it_end[-1]
    last_cls = jnp.max(jnp.where(counts > 0, lanes, 0))
    cls = jnp.where(valid, jnp.sum(it_end[None, :] <= idx[:, None], axis=1).astype(i32), last_cls)
    blk = jnp.where(valid, pick(b_lo, cls) + idx - pick(it_start, cls), n_blocks - 1)
    row0 = blk * ROW_BLOCK
    lo = jnp.where(valid, jnp.maximum(pick(starts, cls), row0) - row0, 0)
    hi = jnp.where(valid, jnp.minimum(pick(ends, cls), row0 + ROW_BLOCK) - row0, 0)
    pair_a, pair_b = [], []
    for g in range(N_GROUPS):
        for a_loc in range(EXPERTS_PER_GROUP):
            for b_loc in range(a_loc + 1, EXPERTS_PER_GROUP):
                pair_a.append(g * EXPERTS_PER_GROUP + a_loc)
                pair_b.append(g * EXPERTS_PER_GROUP + b_loc)
    cpad = [0] * (LANES - N_CLASSES)
    items = (blk.astype(i32), pick(jnp.array(pair_a + cpad, i32), cls), pick(jnp.array(pair_b + cpad, i32), cls),
             lo.astype(i32), hi.astype(i32))

    xs = _dispatch(dest, h2, tm=min(1024, t_len))
    ys = _experts(items, xs, jnp.concatenate([w_gate, w_up], axis=-1).astype(BF16), w_down.astype(BF16))
    return dest, x1, ys


def kernel(x, norm1_g, w_in, b_f, w_dw, b_dw, conv_ln_g, conv_ln_b, out_g_conv, out_g_att, w_out, norm2_g,
           w_r1, b_r1, w_r2, b_r2, w_gate, w_up, w_down, final_g):
    assert norm1_g.shape[0] == 1, "single-layer stack"
    batch, seq, d = x.shape
    dest, x1, ys = _layer(
        x, norm1_g[0], w_in[0], b_f[0], w_dw[0], b_dw[0], conv_ln_g[0], conv_ln_b[0], out_g_conv[0],
        out_g_att[0], w_out[0], norm2_g[0], w_r1[0], b_r1[0], w_r2[0], b_r2[0], w_gate[0], w_up[0], w_down[0])
    out = _combine(dest, x1, final_g.reshape(1, d), ys, tm=min(256, batch * seq // COMBINE_TILES))
    return out.reshape(batch, seq, d)
```

```python
import functools

import jax
import jax.numpy as jnp
from jax import lax
from jax.experimental import pallas as pl
from jax.experimental.pallas import tpu as pltpu

D_MODEL = 1024
D_CONV = 512
N_HEADS = 8
HEAD_DIM = 64
D_ATT = N_HEADS * HEAD_DIM
CONV_WIDTH = 31
N_GROUPS = 4
EXPERTS_PER_GROUP = 8
N_EXPERTS = N_GROUPS * EXPERTS_PER_GROUP
D_EXPERT = D_MODEL // 4
ROW_BLOCK = 512
ROW_SEGMENT = 128
EPS = 1e-6

LANES = 128
SUBLANES = 8
KAUG = 128
N_PIECES = 3
PIECE_STRIDE = 8
CONV_HALO = 32
PAIRS_PER_GROUP = EXPERTS_PER_GROUP * (EXPERTS_PER_GROUP - 1) // 2
N_CLASSES = N_GROUPS * PAIRS_PER_GROUP
ROW_W = D_MODEL + LANES
SKIP_LOG2 = 160.0
FIXED_REF_MAX_BOUND = 40.0
GROUPS_PER_TRIP = 4
NORM_SLACK = 1.02
V_ROWS = 80
LOG2E = 1.4426950408889634
VMEM_LIMIT = 56 * 1024 * 1024

F32 = jnp.float32
BF16 = jnp.bfloat16


def _dot(a, b):
    return jnp.dot(a, b, preferred_element_type=F32)


def _dot_nt(a, b):
    return lax.dot_general(a, b, (((1,), (1,)), ((), ())), preferred_element_type=F32)


def _split3(x):
    hi = x.astype(BF16)
    r1 = x - hi.astype(F32)
    mid = r1.astype(BF16)
    lo = (r1 - mid.astype(F32)).astype(BF16)
    return hi.astype(F32), mid.astype(F32), lo.astype(F32)


def _piece_lane_mask(lane, h):
    return (lane == h) | (lane == h + PIECE_STRIDE) | (lane == h + 2 * PIECE_STRIDE)


def _inproj_kernel(x_ref, g1_ref, wrow_ref, bf_ref, wqvT_ref, ltri_ref, hsel_ref,
                   a_ref, kp_ref, qpT_ref, vT_ref, kstat_ref, qstat_ref, carry_ref, *, tm, tk):
    @pl.when(pl.program_id(1) == 0)
    def _():
        carry_ref[...] = jnp.zeros_like(carry_ref)

    x = x_ref[...]
    ms = jnp.mean(x * x, axis=-1, keepdims=True)
    hb = ((x * lax.rsqrt(ms + EPS)) * g1_ref[...]).astype(BF16)

    z = _dot(hb, wrow_ref[...])
    a_ref[...] = z[:, :D_CONV] * jax.nn.sigmoid(z[:, D_CONV:2 * D_CONV])

    kk = z[:, 2 * D_CONV:2 * D_CONV + D_ATT]

    zf = z[:, 2 * D_CONV + D_ATT:] + bf_ref[...]
    lf = jnp.minimum(zf, 0.0) - jnp.log1p(jnp.exp(-jnp.abs(zf)))
    lane = lax.broadcasted_iota(jnp.int32, (tm, LANES), 1)
    hi, mid, lo = _split3(lf)
    lf3 = jnp.where(lane < PIECE_STRIDE, hi,
                    jnp.where(lane < 2 * PIECE_STRIDE, mid,
                              jnp.where(lane < 3 * PIECE_STRIDE, lo, 0.0))).astype(BF16)
    cs3 = _dot(ltri_ref[...], lf3)
    c = (cs3 + pltpu.roll(cs3, LANES - PIECE_STRIDE, 1)
         + pltpu.roll(cs3, LANES - 2 * PIECE_STRIDE, 1)) + carry_ref[...]
    carry_ref[...] = c[tm - 1:tm, :]

    nhi, nmid, nlo = _split3(c * (-LOG2E))
    p3 = jnp.where(lane < PIECE_STRIDE, nhi,
                   jnp.where(lane < 2 * PIECE_STRIDE, pltpu.roll(nmid, PIECE_STRIDE, 1),
                             jnp.where(lane < 3 * PIECE_STRIDE, pltpu.roll(nlo, 2 * PIECE_STRIDE, 1), 0.0)))
    p3_hi = pltpu.roll(p3, HEAD_DIM, 1)
    q_pieces = (-p3).T[0:HEAD_DIM]
    q_shift = N_PIECES * PIECE_STRIDE

    qvT = _dot_nt(wqvT_ref[...], hb)
    qT = qvT[:D_ATT] * LOG2E
    vT = qvT[D_ATT:]
    row = lax.broadcasted_iota(jnp.int32, (HEAD_DIM, tm), 0)
    vrow = lax.broadcasted_iota(jnp.int32, (V_ROWS - HEAD_DIM, tk), 0)
    v_tail = jnp.where(vrow == 0, 1.0, 0.0).astype(BF16)

    for h in range(N_HEADS):
        kcol = kk[:, (h // 2) * LANES:(h // 2 + 1) * LANES]
        f_h = jnp.where(_piece_lane_mask(row, h), q_pieces, 0.0)
        aug_q = jnp.where(_piece_lane_mask(row, h), 1.0, 0.0) + jnp.concatenate(
            [jnp.zeros((q_shift, tm), F32), f_h[0:HEAD_DIM - q_shift]], axis=0)
        q_h = qT[h * HEAD_DIM:(h + 1) * HEAD_DIM, :]
        if h % 2 == 0:
            ext = jnp.where(_piece_lane_mask(lane, h + HEAD_DIM), p3_hi,
                            jnp.where(_piece_lane_mask(lane, h + HEAD_DIM + q_shift), 1.0, 0.0))
            kp = jnp.where(lane < HEAD_DIM, kcol, ext)
            qp = jnp.concatenate([q_h, aug_q], axis=0)
        else:
            ext = jnp.where(_piece_lane_mask(lane, h), p3,
                            jnp.where(_piece_lane_mask(lane, h + q_shift), 1.0, 0.0))
            kp = jnp.where(lane >= HEAD_DIM, kcol, ext)
            qp = jnp.concatenate([aug_q, q_h], axis=0)
        qpT_ref[0, h] = qp.astype(BF16)
        for cidx in range(tm // tk):
            kp_ref[0, h, cidx] = kp[cidx * tk:(cidx + 1) * tk, :].astype(BF16)
            v_h = vT[h * HEAD_DIM:(h + 1) * HEAD_DIM, cidx * tk:(cidx + 1) * tk].astype(BF16)
            vT_ref[0, h, cidx] = jnp.concatenate([v_h, v_tail], axis=0)

    kn2 = jnp.max(_dot((kk * kk).astype(BF16), hsel_ref[...]), axis=0, keepdims=True)
    srow = lax.broadcasted_iota(jnp.int32, (SUBLANES, LANES), 0)
    kstat_ref[0, 0] = jnp.where(srow == 0, c[0:1, :], jnp.where(srow == 1, c[tm - 1:tm, :],
                                                                 jnp.where(srow == 2, kn2, 0.0)))
    q2 = qT * qT
    qn2 = jnp.concatenate([jnp.sum(q2[h * HEAD_DIM:(h + 1) * HEAD_DIM, :], axis=0, keepdims=True)
                           for h in range(N_HEADS)], axis=0)
    qstat_ref[0, 0] = jnp.broadcast_to(jnp.max(qn2, axis=1, keepdims=True), (N_HEADS, LANES))


def _inproj(x2, g1, wrow, bf3, wqvT, *, batch, seq, tm, tk):
    nt = seq // tm
    nk = seq // tk
    ltri = jnp.tril(jnp.ones((tm, tm), F32)).astype(BF16)
    hsel = (jnp.arange(D_ATT)[:, None] // HEAD_DIM == jnp.arange(LANES)[None, :]).astype(BF16)
    const = lambda shape: pl.BlockSpec(shape, lambda b, t: (0,) * len(shape))
    stat_spec = pl.BlockSpec((1, 1, SUBLANES, LANES), lambda b, t: (b, t, 0, 0))
    stat_shape = jax.ShapeDtypeStruct((batch, nt, SUBLANES, LANES), F32)
    return pl.pallas_call(
        functools.partial(_inproj_kernel, tm=tm, tk=tk),
        grid=(batch, nt),
        in_specs=[
            pl.BlockSpec((tm, D_MODEL), lambda b, t: (b * nt + t, 0)),
            const((1, D_MODEL)), const((D_MODEL, 2 * D_CONV + D_ATT + LANES)), const((1, LANES)),
            const((2 * D_ATT, D_MODEL)), const((tm, tm)), const((D_ATT, LANES)),
        ],
        out_specs=[
            pl.BlockSpec((tm, D_CONV), lambda b, t: (b * nt + t, 0)),
            pl.BlockSpec((1, N_HEADS, tm // tk, tk, KAUG), lambda b, t: (b, 0, t, 0, 0)),
            pl.BlockSpec((1, N_HEADS, KAUG, tm), lambda b, t: (b, 0, 0, t)),
            pl.BlockSpec((1, N_HEADS, tm // tk, V_ROWS, tk), lambda b, t: (b, 0, t, 0, 0)),
            stat_spec, stat_spec,
        ],
        out_shape=[
            jax.ShapeDtypeStruct((batch * seq, D_CONV), F32),
            jax.ShapeDtypeStruct((batch, N_HEADS, nk, tk, KAUG), BF16),
            jax.ShapeDtypeStruct((batch, N_HEADS, KAUG, seq), BF16),
            jax.ShapeDtypeStruct((batch, N_HEADS, nk, V_ROWS, tk), BF16),
            stat_shape, stat_shape,
        ],
        scratch_shapes=[pltpu.VMEM((1, LANES), F32)],
        compiler_params=pltpu.CompilerParams(
            dimension_semantics=("arbitrary", "arbitrary"), vmem_limit_bytes=VMEM_LIMIT),
        name="inproj",
    )(x2, g1, wrow, bf3, wqvT, ltri, hsel)


def _conv_tile(a_ref, w_ref, b_ref, lng_ref, lnb_ref, og_ref, sh_ref, acc_ref, *, tc, rc):
    n_cb = D_CONV // LANES

    @pl.when(pl.program_id(1) == 0)
    def _():
        for cb in range(n_cb):
            sh_ref[0, cb, 0:CONV_HALO, :] = jnp.zeros((CONV_HALO, LANES), F32)

    @pl.when(pl.program_id(1) > 0)
    def _():
        for cb in range(n_cb):
            sh_ref[0, cb, 0:CONV_HALO, :] = sh_ref[0, cb, tc:tc + CONV_HALO, :]

    n_sh = tc + CONV_HALO - SUBLANES
    for cb in range(n_cb):
        sh_ref[0, cb, CONV_HALO:CONV_HALO + tc, :] = a_ref[:, cb * LANES:(cb + 1) * LANES]
        for f in range(1, SUBLANES):
            sh_ref[f, cb, 0:n_sh, :] = sh_ref[0, cb, f:f + n_sh, :]

    base = CONV_HALO - (CONV_WIDTH - 1)
    for cb in range(n_cb):
        cols = slice(cb * LANES, (cb + 1) * LANES)

        def chunk(c, carry, cb=cb, cols=cols):
            r0 = pl.multiple_of(c * rc, rc)
            acc = jnp.zeros((rc, LANES), F32)
            for j in range(CONV_WIDTH):
                f = (base + j) % SUBLANES
                acc = acc + w_ref[j:j + 1, cols] * sh_ref[f, cb, pl.ds(r0 + (base + j - f), rc), :]
            acc_ref[pl.ds(r0, rc), cols] = acc
            return carry

        lax.fori_loop(0, tc // rc, chunk, 0)

    y = acc_ref[...] + b_ref[...]
    mu = jnp.mean(y, axis=-1, keepdims=True)
    yc = y - mu
    var = jnp.mean(yc * yc, axis=-1, keepdims=True)
    yn = yc * lax.rsqrt(var + EPS) * lng_ref[...] + lnb_ref[...]
    s = yn * jax.nn.sigmoid(yn)
    ms = jnp.mean(s * s, axis=-1, keepdims=True)
    return (s * lax.rsqrt(ms + EPS) * og_ref[...]).astype(BF16)


def _attn_fixed_ref_kernel(g0_ref, bnd_ref, qT_ref, k_ref, v_ref, o_ref, *, tq, tk, unit):
    i = pl.program_id(2)
    qT = qT_ref[0, 0]
    n_sub = tq // tk
    bnd = bnd_ref[0, 0, 0:1, 0:1]
    sb0 = unit * g0_ref[(pl.program_id(0) * N_HEADS + pl.program_id(1)) * pl.num_programs(2) + i]

    def values(sb, n):
        return jnp.concatenate([v_ref[0, 0, sb + d] for d in range(n)], axis=1)

    def probs(sb, n):
        return jnp.concatenate([jnp.exp2(_dot(k_ref[0, 0, sb + d], qT) - bnd).astype(BF16)
                                for d in range(n)], axis=0)

    def accumulate(acc, sb, n):
        for lo in range(0, n, n_sub):
            m = min(n_sub, n - lo)
            acc = acc + _dot(values(sb + lo, m), probs(sb + lo, m))
        return acc

    def loop_trip(p, acc):
        return accumulate(acc, sb0 + GROUPS_PER_TRIP * n_sub * p, GROUPS_PER_TRIP * n_sub)

    key = lax.broadcasted_iota(jnp.int32, (tk, tk), 0)
    qry = lax.broadcasted_iota(jnp.int32, (tk, tk), 1)

    def diagonal(acc):
        rows = []
        for d in range(n_sub):
            s = _dot(k_ref[0, 0, i * n_sub + d], qT[:, d * tk:])
            tiles = [jnp.zeros((tk, d * tk), BF16)] if d else []
            tiles.append(jnp.exp2(jnp.where(key <= qry, s[:, :tk], -jnp.inf) - bnd).astype(BF16))
            if d + 1 < n_sub:
                tiles.append(jnp.exp2(s[:, tk:] - bnd).astype(BF16))
            rows.append(jnp.concatenate(tiles, axis=1))
        acc = acc + _dot(values(i * n_sub, n_sub), jnp.concatenate(rows, axis=0))
        o_ref[0] = acc[:HEAD_DIM] / acc[HEAD_DIM:HEAD_DIM + 1]

    n_before = i * n_sub - sb0
    per_trip = GROUPS_PER_TRIP * n_sub
    acc = lax.fori_loop(0, n_before // per_trip, loop_trip, jnp.zeros((V_ROWS, tq), F32))
    left = n_before % per_trip

    for n in range(0, per_trip, unit):
        @pl.when(left == n)
        def _(n=n):
            diagonal(accumulate(acc, i * n_sub - n, n))


def _attn_kernel(g0_ref, qT_ref, k_ref, v_ref, o_ref, sa_ref, sb_ref, cma_ref, cmb_ref, *, tq, tk):
    i = pl.program_id(2)
    qT = qT_ref[0, 0]
    n_sub = tq // tk

    def scores(g, s_ref, cm_ref):
        for d in range(n_sub):
            s = _dot(k_ref[0, 0, g * n_sub + d], qT)
            s_ref[d] = s
            cm_ref[d] = jnp.max(s, axis=0, keepdims=True)

    def consume(g, s_ref, cm_ref, carry, nxt=None):
        m, acc = carry
        m_new = m
        for d in range(n_sub):
            m_new = jnp.maximum(m_new, cm_ref[d])
        ps = []
        for d in range(n_sub):
            if nxt is not None:
                g_n, sn_ref, cmn_ref = nxt
                s = _dot(k_ref[0, 0, g_n * n_sub + d], qT)
                sn_ref[d] = s
                cmn_ref[d] = jnp.max(s, axis=0, keepdims=True)
            ps.append(jnp.exp2(s_ref[d] - m_new).astype(BF16))
        p = jnp.concatenate(ps, axis=0)
        v = jnp.concatenate([v_ref[0, 0, g * n_sub + d] for d in range(n_sub)], axis=1)
        return m_new, jnp.exp2(m - m_new) * acc + _dot(v, p)

    key = lax.broadcasted_iota(jnp.int32, (tk, tk), 0)
    qry = lax.broadcasted_iota(jnp.int32, (tk, tk), 1)

    def consume_diag(s_ref, cm_ref, carry):
        m, acc = carry
        lanes = lambda j: slice(j * tk, (j + 1) * tk)
        causal = [jnp.where(key <= qry, s_ref[d, :, lanes(d)], -jnp.inf) for d in range(n_sub)]
        m_tiles = []
        for j in range(n_sub):
            m_j = jnp.maximum(m[:, lanes(j)], jnp.max(causal[j], axis=0, keepdims=True))
            for d in range(j):
                m_j = jnp.maximum(m_j, cm_ref[d, :, lanes(j)])
            m_tiles.append(m_j)
        m_new = jnp.concatenate(m_tiles, axis=1)
        rows = []
        for d in range(n_sub):
            tiles = [jnp.zeros((tk, d * tk), BF16)] if d else []
            tiles.append(jnp.exp2(causal[d] - m_tiles[d]).astype(BF16))
            if d + 1 < n_sub:
                rest = slice((d + 1) * tk, tq)
                tiles.append(jnp.exp2(s_ref[d, :, rest] - m_new[:, rest]).astype(BF16))
            rows.append(jnp.concatenate(tiles, axis=1))
        p = jnp.concatenate(rows, axis=0)
        v = jnp.concatenate([v_ref[0, 0, i * n_sub + d] for d in range(n_sub)], axis=1)
        acc = jnp.exp2(m - m_new) * acc + _dot(v, p)
        o_ref[0] = acc[:HEAD_DIM] / acc[HEAD_DIM:HEAD_DIM + 1]

    g0 = g0_ref[(pl.program_id(0) * N_HEADS + pl.program_id(1)) * pl.num_programs(2) + i]
    n_full = i - g0

    def pair(p, carry):
        g = g0 + 2 * p
        carry = consume(g, sa_ref, cma_ref, carry, nxt=(g + 1, sb_ref, cmb_ref))
        return consume(g + 1, sb_ref, cmb_ref, carry, nxt=(g + 2, sa_ref, cma_ref))

    scores(g0, sa_ref, cma_ref)
    init = (jnp.full((1, tq), -jnp.inf, F32), jnp.zeros((V_ROWS, tq), F32))
    carry = lax.fori_loop(0, n_full // 2, pair, init)

    @pl.when(n_full % 2 == 1)
    def _():
        consume_diag(sb_ref, cmb_ref, consume(i - 1, sa_ref, cma_ref, carry, nxt=(i, sb_ref, cmb_ref)))

    @pl.when(n_full % 2 == 0)
    def _():
        consume_diag(sa_ref, cma_ref, carry)


def _first_groups(kstat, qstat, *, tm, tq):
    per = tq // tm
    bound = jnp.sqrt(jnp.max(kstat[:, :, 2, :N_HEADS], axis=1) * jnp.max(qstat[:, :, :, 0], axis=1))
    bound = bound * NORM_SLACK
    cq = kstat[:, ::per, 0, :N_HEADS]
    ck = kstat[:, :, 1, :N_HEADS]
    d = (cq[:, :, None, :] - ck[:, None, :, :]) * LOG2E
    nq, ng = cq.shape[1], ck.shape[1]
    earlier = per * jnp.arange(nq)[None, :, None, None] > jnp.arange(ng)[None, None, :, None]
    skip = earlier & (2.0 * bound[:, None, None, :] + d < -SKIP_LOG2)
    g0 = jnp.sum(jnp.cumprod(skip.astype(jnp.int32), axis=2), axis=2)
    return jnp.transpose(g0, (0, 2, 1)).reshape(-1).astype(jnp.int32), bound


def _attention(g0, bound, qpT, kp, vT, *, batch, seq, tm, tq, tk):
    nk = seq // tk
    params = pltpu.CompilerParams(
        dimension_semantics=("arbitrary", "arbitrary", "arbitrary"), vmem_limit_bytes=VMEM_LIMIT)
    out_shape = jax.ShapeDtypeStruct((batch, D_ATT, seq), F32)
    qkv_specs = [
        pl.BlockSpec((1, 1, KAUG, tq), lambda b, h, i, g0: (b, h, 0, i)),
        pl.BlockSpec((1, 1, nk, tk, KAUG), lambda b, h, i, g0: (b, h, 0, 0, 0)),
        pl.BlockSpec((1, 1, nk, V_ROWS, tk), lambda b, h, i, g0: (b, h, 0, 0, 0)),
    ]
    out_spec = pl.BlockSpec((1, HEAD_DIM, tq), lambda b, h, i, g0: (b, h, i))

    def fixed_ref(g0, bnd, qpT, kp, vT):
        grid_spec = pltpu.PrefetchScalarGridSpec(
            num_scalar_prefetch=1, grid=(batch, N_HEADS, seq // tq),
            in_specs=[pl.BlockSpec((1, 1, SUBLANES, LANES), lambda b, h, i, g0: (b, h, 0, 0))] + qkv_specs,
            out_specs=out_spec)
        return pl.pallas_call(functools.partial(_attn_fixed_ref_kernel, tq=tq, tk=tk, unit=tm // tk),
                              grid_spec=grid_spec, out_shape=out_shape, compiler_params=params,
                              name="attention_fixed_ref")(g0, bnd, qpT, kp, vT)

    def running_max(g0, bnd, qpT, kp, vT):
        del bnd
        grid_spec = pltpu.PrefetchScalarGridSpec(
            num_scalar_prefetch=1, grid=(batch, N_HEADS, seq // tq), in_specs=qkv_specs, out_specs=out_spec,
            scratch_shapes=[pltpu.VMEM((tq // tk, tk, tq), F32), pltpu.VMEM((tq // tk, tk, tq), F32),
                            pltpu.VMEM((tq // tk, 1, tq), F32), pltpu.VMEM((tq // tk, 1, tq), F32)])
        return pl.pallas_call(functools.partial(_attn_kernel, tq=tq, tk=tk), grid_spec=grid_spec,
                              out_shape=out_shape, compiler_params=params, name="attention",
                              )(g0 // (tq // tm), qpT, kp, vT)

    bnd = jnp.broadcast_to(bound[:, :, None, None], (batch, N_HEADS, SUBLANES, LANES))
    return lax.cond(jnp.max(bound) <= FIXED_REF_MAX_BOUND, fixed_ref, running_max, g0, bnd, qpT, kp, vT)


def _outproj_kernel(x_ref, a_ref, yT_ref, wdw_ref, bdw_ref, lng_ref, lnb_ref, gc_ref, wo_ref, ga_ref,
                    g2_ref, wrT_ref, brT_ref, utri_ref, x1_ref, h2_ref, route_ref, cnt_ref,
                    carry_ref, sh_ref, acc_ref, *, tm, rc):
    first = (pl.program_id(0) == 0) & (pl.program_id(1) == 0)

    @pl.when(first)
    def _():
        carry_ref[...] = jnp.zeros_like(carry_ref)

    mc = _conv_tile(a_ref, wdw_ref, bdw_ref, lng_ref, lnb_ref, gc_ref, sh_ref, acc_ref, tc=tm, rc=rc)
    yT = yT_ref[0]
    msa = jnp.mean(yT * yT, axis=0, keepdims=True)
    yn = (yT * lax.rsqrt(msa + EPS)).T * ga_ref[...]
    x1 = x_ref[...] + _dot(jnp.concatenate([mc, yn.astype(BF16)], axis=1), wo_ref[...])
    x1_ref[...] = x1
    ms = jnp.mean(x1 * x1, axis=-1, keepdims=True)
    h2 = (x1 * lax.rsqrt(ms + EPS)) * g2_ref[...]
    h2_ref[:, :D_MODEL] = h2

    lg = _dot_nt(wrT_ref[...], h2.astype(BF16)) + brT_ref[...]
    neg = -jnp.inf
    sub = lax.broadcasted_iota(jnp.int32, (SUBLANES, tm), 0)
    big = jnp.int32(SUBLANES)
    first_of = lambda hit: jnp.min(jnp.where(hit, sub, big), axis=0, keepdims=True)

    lg1 = lg[0:SUBLANES]
    m1 = jnp.max(lg1, axis=0, keepdims=True)
    p1_sel = 1.0 / jnp.sum(jnp.exp(lg1 - m1), axis=0, keepdims=True)
    grp = first_of(lg1 == m1)

    slab = lambda g: lg[SUBLANES * (g + 1):SUBLANES * (g + 2)]
    v = slab(N_GROUPS - 1)
    for g in range(N_GROUPS - 2, -1, -1):
        v = jnp.where(grp == g, slab(g), v)
    v1 = jnp.max(v, axis=0, keepdims=True)
    j1 = first_of(v == v1)
    vv = jnp.where(sub == j1, neg, v)
    v2 = jnp.max(vv, axis=0, keepdims=True)
    j2 = first_of(vv == v2)
    e21 = jnp.exp(v2 - v1)
    w0 = p1_sel / (1.0 + e21)
    w1 = p1_sel * e21 / (1.0 + e21)

    swap = j2 < j1
    al = jnp.where(swap, j2, j1)
    bl = jnp.where(swap, j1, j2)
    wa = jnp.where(swap, w1, w0)
    wb = jnp.where(swap, w0, w1)
    cid = PAIRS_PER_GROUP * grp + ((al * (2 * EXPERTS_PER_GROUP - 1 - al)) >> 1) + (bl - al - 1)

    cls = lax.broadcasted_iota(jnp.int32, (LANES, tm), 0)
    oh = cls == cid
    cmat = jnp.where(oh, 1.0, 0.0)
    prefix = _dot(cmat.astype(BF16), utri_ref[...]) + carry_ref[...]
    rank = jnp.sum(jnp.where(oh, prefix, 0.0), axis=0, keepdims=True)
    counts = prefix[:, tm - 1:tm] + cmat[:, tm - 1:tm]
    carry_ref[...] = counts
    cnt_ref[...] = counts

    h2_ref[:, D_MODEL:] = jnp.where(cls == 0, wa, jnp.where(cls == 1, wb, 0.0)).T
    route_ref[...] = jnp.where(sub == 0, cid.astype(F32), jnp.where(sub == 1, rank, 0.0))


def _outproj(x2, a, yT, conv_params, wo, ga, g2, wrT, brT, *, batch, seq, tm, rc):
    nt = seq // tm
    utri = jnp.triu(jnp.ones((tm, tm), F32), 1).astype(BF16)
    const = lambda shape: pl.BlockSpec(shape, lambda b, t: (0,) * len(shape))
    row_spec = lambda w: pl.BlockSpec((tm, w), lambda b, t: (b * nt + t, 0))
    return pl.pallas_call(
        functools.partial(_outproj_kernel, tm=tm, rc=rc),
        grid=(batch, nt),
        in_specs=[row_spec(D_MODEL), row_spec(D_CONV),
                  pl.BlockSpec((1, D_ATT, tm), lambda b, t: (b, 0, t)),
                  const((CONV_WIDTH, D_CONV)), const((1, D_CONV)), const((1, D_CONV)), const((1, D_CONV)),
                  const((1, D_CONV)),
                  const((D_CONV + D_ATT, D_MODEL)), const((1, D_ATT)), const((1, D_MODEL)),
                  const((LANES, D_MODEL)), const((LANES, 1)), const((tm, tm))],
        out_specs=[row_spec(D_MODEL), row_spec(ROW_W),
                   pl.BlockSpec((SUBLANES, tm), lambda b, t: (0, b * nt + t)), const((LANES, 1))],
        out_shape=[jax.ShapeDtypeStruct((batch * seq, D_MODEL), F32),
                   jax.ShapeDtypeStruct((batch * seq, ROW_W), F32),
                   jax.ShapeDtypeStruct((SUBLANES, batch * seq), F32),
                   jax.ShapeDtypeStruct((LANES, 1), F32)],
        scratch_shapes=[pltpu.VMEM((LANES, 1), F32),
                        pltpu.VMEM((SUBLANES, D_CONV // LANES, tm + CONV_HALO, LANES), F32),
                        pltpu.VMEM((tm, D_CONV), F32)],
        compiler_params=pltpu.CompilerParams(
            dimension_semantics=("arbitrary", "arbitrary"), vmem_limit_bytes=VMEM_LIMIT),
        name="outproj",
    )(x2, a, yT, *conv_params, wo, ga, g2, wrT, brT, utri)


def _row_copies(src_at, dst_at, sem):
    return pltpu.make_async_copy(src_at, dst_at, sem)


def _dispatch_kernel(dest_ref, h2_ref, xs_ref, sem, *, tm):
    for g in range(tm // SUBLANES):
        for u in range(SUBLANES):
            _row_copies(h2_ref.at[g, pl.ds(u, 1)], xs_ref.at[pl.ds(dest_ref[g * SUBLANES + u], 1)],
                        sem).start(priority=u % 2)
    done = xs_ref.at[pl.ds(0, tm)]
    _row_copies(done, done, sem).wait()


def _dispatch(dest, h2, *, tm):
    t_len = h2.shape[0]
    return pl.pallas_call(
        functools.partial(_dispatch_kernel, tm=tm),
        grid=(t_len // tm,),
        in_specs=[pl.BlockSpec((tm,), lambda i: (i,), memory_space=pltpu.SMEM),
                  pl.BlockSpec((tm // SUBLANES, SUBLANES, ROW_W), lambda i: (i, 0, 0))],
        out_specs=pl.BlockSpec(memory_space=pl.ANY),
        out_shape=jax.ShapeDtypeStruct((t_len, ROW_W), F32),
        scratch_shapes=[pltpu.SemaphoreType.DMA(())],
        compiler_params=pltpu.CompilerParams(dimension_semantics=("arbitrary",), vmem_limit_bytes=VMEM_LIMIT),
        name="dispatch",
    )(dest, h2.reshape(t_len // SUBLANES, SUBLANES, ROW_W))


def _expert_kernel(blk_ref, ea_ref, eb_ref, lo_ref, hi_ref, x_ref, wgua_ref, wda_ref, wgub_ref, wdb_ref, y_ref):
    del blk_ref, ea_ref, eb_ref
    i = pl.program_id(0)
    lo = lo_ref[i]
    hi = hi_ref[i]
    first = lo // ROW_SEGMENT
    last = (hi - 1) // ROW_SEGMENT
    start = pl.multiple_of(first * ROW_SEGMENT, ROW_SEGMENT)

    def mlp(xb, wgu_ref, wd_ref):
        gu = _dot(xb, wgu_ref[0])
        g = gu[:, :D_EXPERT]
        return _dot(((g * jax.nn.sigmoid(g)) * gu[:, D_EXPERT:]).astype(BF16), wd_ref[0])

    def run(n_rows):
        rows = pl.ds(start, n_rows)
        xb = x_ref[rows, :D_MODEL].astype(BF16)
        wts = x_ref[rows, D_MODEL:]
        y = wts[:, 0:1] * mlp(xb, wgua_ref, wda_ref) + wts[:, 1:2] * mlp(xb, wgub_ref, wdb_ref)
        row = start + lax.broadcasted_iota(jnp.int32, (n_rows, 1), 0)
        mine = (row >= lo) & (row < hi)
        head = pl.ds(start, ROW_SEGMENT)

        @pl.when(lo == start)
        def _():
            y_ref[head, :] = jnp.where(mine[:ROW_SEGMENT], y[:ROW_SEGMENT], 0.0)

        @pl.when(lo > start)
        def _():
            y_ref[head, :] = jnp.where(mine[:ROW_SEGMENT], y[:ROW_SEGMENT], y_ref[head, :])

        if n_rows > ROW_SEGMENT:
            rest = pl.multiple_of(start + ROW_SEGMENT, ROW_SEGMENT)
            y_ref[pl.ds(rest, n_rows - ROW_SEGMENT), :] = jnp.where(
                mine[ROW_SEGMENT:], y[ROW_SEGMENT:], 0.0)

    for n in range(1, ROW_BLOCK // ROW_SEGMENT + 1):
        @pl.when((hi > lo) & (last - first + 1 == n))
        def _(n=n):
            run(n * ROW_SEGMENT)


def _experts(items, xs, wgu, wd):
    n_rows = xs.shape[0]
    n_items = items[0].shape[0]
    wspec = lambda shape, which: pl.BlockSpec(
        shape, lambda i, blk, ea, eb, lo, hi: ((ea, eb)[which][i], 0, 0))
    grid_spec = pltpu.PrefetchScalarGridSpec(
        num_scalar_prefetch=5,
        grid=(n_items,),
        in_specs=[pl.BlockSpec((ROW_BLOCK, ROW_W), lambda i, blk, ea, eb, lo, hi: (blk[i], 0)),
                  wspec((1, D_MODEL, 2 * D_EXPERT), 0), wspec((1, D_EXPERT, D_MODEL), 0),
                  wspec((1, D_MODEL, 2 * D_EXPERT), 1), wspec((1, D_EXPERT, D_MODEL), 1)],
        out_specs=pl.BlockSpec((ROW_BLOCK, D_MODEL), lambda i, blk, ea, eb, lo, hi: (blk[i], 0)),
    )
    return pl.pallas_call(
        _expert_kernel,
        grid_spec=grid_spec,
        out_shape=jax.ShapeDtypeStruct((n_rows, D_MODEL), F32),
        compiler_params=pltpu.CompilerParams(dimension_semantics=("arbitrary",), vmem_limit_bytes=VMEM_LIMIT),
        name="experts",
    )(*items, xs, wgu, wd, wgu, wd)


def _combine_kernel(dcur_ref, dnxt_ref, x1_ref, g_ref, ys_ref, out_ref, buf_ref, sems, *, tm):
    i = pl.program_id(0)
    slot = i % 2

    def issue(d_ref, s):
        for g in range(tm // SUBLANES):
            for u in range(SUBLANES):
                _row_copies(ys_ref.at[pl.ds(d_ref[g * SUBLANES + u], 1)], buf_ref.at[s, g, pl.ds(u, 1)],
                            sems.at[s]).start(priority=u % 2)

    @pl.when(i == 0)
    def _():
        issue(dcur_ref, 0)

    for s in range(2):
        @pl.when((i + 1 < pl.num_programs(0)) & (slot == 1 - s))
        def _(s=s):
            issue(dnxt_ref, s)

    _row_copies(buf_ref.at[slot], buf_ref.at[slot], sems.at[slot]).wait()
    x2 = x1_ref[...] + buf_ref[slot]
    ms = jnp.mean(x2 * x2, axis=-1, keepdims=True)
    out_ref[...] = (x2 * lax.rsqrt(ms + EPS)) * g_ref[...]


def _combine(dest, x1, gf, ys, *, tm):
    t_len = x1.shape[0]
    n = t_len // tm
    return pl.pallas_call(
        functools.partial(_combine_kernel, tm=tm),
        grid=(n,),
        in_specs=[pl.BlockSpec((tm,), lambda i: (i,), memory_space=pltpu.SMEM),
                  pl.BlockSpec((tm,), lambda i: (jnp.minimum(i + 1, n - 1),), memory_space=pltpu.SMEM),
                  pl.BlockSpec((tm // SUBLANES, SUBLANES, D_MODEL), lambda i: (i, 0, 0)),
                  pl.BlockSpec((1, D_MODEL), lambda i: (0, 0)),
                  pl.BlockSpec(memory_space=pl.ANY)],
        out_specs=pl.BlockSpec((tm // SUBLANES, SUBLANES, D_MODEL), lambda i: (i, 0, 0)),
        out_shape=jax.ShapeDtypeStruct((t_len // SUBLANES, SUBLANES, D_MODEL), F32),
        scratch_shapes=[pltpu.VMEM((2, tm // SUBLANES, SUBLANES, D_MODEL), F32), pltpu.SemaphoreType.DMA((2,))],
        compiler_params=pltpu.CompilerParams(dimension_semantics=("arbitrary",), vmem_limit_bytes=VMEM_LIMIT),
        name="combine",
    )(dest, dest, x1.reshape(t_len // SUBLANES, SUBLANES, D_MODEL), gf, ys)


def _layer(x, norm1_g, w_in, b_f, w_dw, b_dw, conv_ln_g, conv_ln_b, out_g_conv, out_g_att, w_out,
           norm2_g, w_r1, b_r1, w_r2, b_r2, w_gate, w_up, w_down):
    batch, seq, d = x.shape
    t_len = batch * seq
    tm = min(512, seq)
    tk = min(256, seq)
    tq = min(1024, seq)
    o2 = 2 * D_CONV
    o3, o4, o5 = o2 + D_ATT, o2 + 2 * D_ATT, o2 + 3 * D_ATT

    assert PIECE_STRIDE == N_HEADS
    pad = LANES - N_PIECES * PIECE_STRIDE
    wf = jnp.pad(jnp.tile(w_in[:, o5:], (1, N_PIECES)), ((0, 0), (0, pad)))
    wrow = jnp.concatenate([w_in[:, :o2], w_in[:, o3:o4], wf], axis=1).astype(BF16)
    wqvT = jnp.concatenate([w_in[:, o2:o3] * (HEAD_DIM ** -0.5), w_in[:, o4:o5]], axis=1).T.astype(BF16)
    bf3 = jnp.pad(jnp.tile(b_f.astype(F32).reshape(1, N_HEADS), (1, N_PIECES)), ((0, 0), (0, pad)))

    x2 = x.reshape(t_len, d)
    a, kp, qpT, vT, kstat, qstat = _inproj(x2, norm1_g.reshape(1, d), wrow, bf3, wqvT,
                                           batch=batch, seq=seq, tm=tm, tk=tk)
    conv_params = (w_dw, b_dw.reshape(1, -1), conv_ln_g.reshape(1, -1), conv_ln_b.reshape(1, -1),
                   out_g_conv.reshape(1, -1))
    g0, bound = _first_groups(kstat, qstat, tm=tm, tq=tq)
    yT = _attention(g0, bound, qpT, kp, vT, batch=batch, seq=seq, tm=tm, tq=tq, tk=tk)

    gpad = SUBLANES - N_GROUPS
    rpad = LANES - SUBLANES - N_EXPERTS
    wrT = jnp.concatenate([w_r1.T, jnp.zeros((gpad, d), F32),
                           jnp.transpose(w_r2, (0, 2, 1)).reshape(N_EXPERTS, d), jnp.zeros((rpad, d), F32)], axis=0)
    brT = jnp.concatenate([b_r1.astype(F32), jnp.full((gpad,), -jnp.inf, F32), b_r2.reshape(-1).astype(F32),
                           jnp.zeros((rpad,), F32)]).reshape(LANES, 1)
    x1, h2, route, cnt = _outproj(x2, a, yT, conv_params, w_out.astype(BF16),
                                  out_g_att.reshape(1, -1), norm2_g.reshape(1, d),
                                  wrT.astype(BF16), brT, batch=batch, seq=seq, tm=tm, rc=min(256, seq))

    i32 = jnp.int32
    lanes = jnp.arange(LANES, dtype=i32)
    pick = lambda table, idx: jnp.sum(jnp.where(idx[:, None] == lanes, table[None, :], 0), axis=1).astype(i32)
    counts = cnt[:, 0].astype(i32)
    ends = jnp.cumsum(counts).astype(i32)
    starts = ends - counts
    dest = pick(starts, route[0].astype(i32)) + route[1].astype(i32)

    n_blocks = t_len // ROW_BLOCK
    b_lo = starts // ROW_BLOCK
    n_it = jnp.where(counts > 0, (ends - 1) // ROW_BLOCK - b_lo + 1, 0)
    it_end = jnp.cumsum(n_it).astype(i32)
    it_start = it_end - n_it
    idx = jnp.arange(n_blocks + N_CLASSES, dtype=i32)
    valid = idx < it_end[-1]
    last_cls = jnp.max(jnp.where(counts > 0, lanes, 0))
    cls = jnp.where(valid, jnp.sum(it_end[None, :] <= idx[:, None], axis=1).astype(i32), last_cls)
    blk = jnp.where(valid, pick(b_lo, cls) + idx - pick(it_start, cls), n_blocks - 1)
    row0 = blk * ROW_BLOCK
    lo = jnp.where(valid, jnp.maximum(pick(starts, cls), row0) - row0, 0)
    hi = jnp.where(valid, jnp.minimum(pick(ends, cls), row0 + ROW_BLOCK) - row0, 0)
    pair_a, pair_b = [], []
    for g in range(N_GROUPS):
        for a_loc in range(EXPERTS_PER_GROUP):
            for b_loc in range(a_loc + 1, EXPERTS_PER_GROUP):
                pair_a.append(g * EXPERTS_PER_GROUP + a_loc)
                pair_b.append(g * EXPERTS_PER_GROUP + b_loc)
    cpad = [0] * (LANES - N_CLASSES)
    items = (blk.astype(i32), pick(jnp.array(pair_a + cpad, i32), cls), pick(jnp.array(pair_b + cpad, i32), cls),
             lo.astype(i32), hi.astype(i32))

    xs = _dispatch(dest, h2, tm=min(2048, t_len))
    ys = _experts(items, xs, jnp.concatenate([w_gate, w_up], axis=-1).astype(BF16), w_down.astype(BF16))
    return dest, x1, ys


def kernel(x, norm1_g, w_in, b_f, w_dw, b_dw, conv_ln_g, conv_ln_b, out_g_conv, out_g_att, w_out, norm2_g,
           w_r1, b_r1, w_r2, b_r2, w_gate, w_up, w_down, final_g):
    assert norm1_g.shape[0] == 1, "single-layer stack"
    batch, seq, d = x.shape
    dest, x1, ys = _layer(
        x, norm1_g[0], w_in[0], b_f[0], w_dw[0], b_dw[0], conv_ln_g[0], conv_ln_b[0], out_g_conv[0],
        out_g_att[0], w_out[0], norm2_g[0], w_r1[0], b_r1[0], w_r2[0], b_r2[0], w_gate[0], w_up[0], w_down[0])
    out = _combine(dest, x1, final_g.reshape(1, d), ys, tm=min(512, batch * seq))
    return out.reshape(batch, seq, d)
```

```python
import functools

import jax
import jax.numpy as jnp
from jax import lax
from jax.experimental import pallas as pl
from jax.experimental.pallas import tpu as pltpu

D_MODEL = 1024
D_CONV = 512
N_HEADS = 8
HEAD_DIM = 64
D_ATT = N_HEADS * HEAD_DIM
CONV_WIDTH = 31
N_GROUPS = 4
EXPERTS_PER_GROUP = 8
N_EXPERTS = N_GROUPS * EXPERTS_PER_GROUP
D_EXPERT = D_MODEL // 4
ROW_BLOCK = 1024
ROW_SEGMENT = 128
EPS = 1e-6

LANES = 128
SUBLANES = 8
KAUG = 128
N_PIECES = 3
PIECE_STRIDE = 8
CONV_HALO = 32
PAIRS_PER_GROUP = EXPERTS_PER_GROUP * (EXPERTS_PER_GROUP - 1) // 2
N_CLASSES = N_GROUPS * PAIRS_PER_GROUP
ROW_W = D_MODEL + LANES
SKIP_LOG2 = 160.0
FIXED_REF_MAX_BOUND = 40.0
GROUPS_PER_TRIP = 4
NORM_SLACK = 1.02
V_ROWS = 80
LOG2E = 1.4426950408889634
VMEM_LIMIT = 56 * 1024 * 1024

F32 = jnp.float32
BF16 = jnp.bfloat16


def _dot(a, b):
    return jnp.dot(a, b, preferred_element_type=F32)


def _dot_nt(a, b):
    return lax.dot_general(a, b, (((1,), (1,)), ((), ())), preferred_element_type=F32)


def _split3(x):
    hi = x.astype(BF16)
    r1 = x - hi.astype(F32)
    mid = r1.astype(BF16)
    lo = (r1 - mid.astype(F32)).astype(BF16)
    return hi.astype(F32), mid.astype(F32), lo.astype(F32)


def _piece_lane_mask(lane, h):
    return (lane == h) | (lane == h + PIECE_STRIDE) | (lane == h + 2 * PIECE_STRIDE)


def _inproj_kernel(x_ref, g1_ref, wrow_ref, bf_ref, wqvT_ref, ltri_ref, hsel_ref,
                   a_ref, kp_ref, qpT_ref, vT_ref, kstat_ref, qstat_ref, carry_ref, *, tm, tk):
    @pl.when(pl.program_id(1) == 0)
    def _():
        carry_ref[...] = jnp.zeros_like(carry_ref)

    x = x_ref[...]
    ms = jnp.mean(x * x, axis=-1, keepdims=True)
    hb = ((x * lax.rsqrt(ms + EPS)) * g1_ref[...]).astype(BF16)

    z = _dot(hb, wrow_ref[...])
    a_ref[...] = z[:, :D_CONV] * jax.nn.sigmoid(z[:, D_CONV:2 * D_CONV])

    kk = z[:, 2 * D_CONV:2 * D_CONV + D_ATT]

    zf = z[:, 2 * D_CONV + D_ATT:] + bf_ref[...]
    lf = jnp.minimum(zf, 0.0) - jnp.log1p(jnp.exp(-jnp.abs(zf)))
    lane = lax.broadcasted_iota(jnp.int32, (tm, LANES), 1)
    hi, mid, lo = _split3(lf)
    lf3 = jnp.where(lane < PIECE_STRIDE, hi,
                    jnp.where(lane < 2 * PIECE_STRIDE, mid,
                              jnp.where(lane < 3 * PIECE_STRIDE, lo, 0.0))).astype(BF16)
    cs3 = _dot(ltri_ref[...], lf3)
    c = (cs3 + pltpu.roll(cs3, LANES - PIECE_STRIDE, 1)
         + pltpu.roll(cs3, LANES - 2 * PIECE_STRIDE, 1)) + carry_ref[...]
    carry_ref[...] = c[tm - 1:tm, :]

    nhi, nmid, nlo = _split3(c * (-LOG2E))
    p3 = jnp.where(lane < PIECE_STRIDE, nhi,
                   jnp.where(lane < 2 * PIECE_STRIDE, pltpu.roll(nmid, PIECE_STRIDE, 1),
                             jnp.where(lane < 3 * PIECE_STRIDE, pltpu.roll(nlo, 2 * PIECE_STRIDE, 1), 0.0)))
    p3_hi = pltpu.roll(p3, HEAD_DIM, 1)
    q_pieces = (-p3).T[0:HEAD_DIM]
    q_shift = N_PIECES * PIECE_STRIDE

    qvT = _dot_nt(wqvT_ref[...], hb)
    qT = qvT[:D_ATT] * LOG2E
    vT = qvT[D_ATT:]
    row = lax.broadcasted_iota(jnp.int32, (HEAD_DIM, tm), 0)
    vrow = lax.broadcasted_iota(jnp.int32, (V_ROWS - HEAD_DIM, tk), 0)
    v_tail = jnp.where(vrow == 0, 1.0, 0.0).astype(BF16)

    for h in range(N_HEADS):
        kcol = kk[:, (h // 2) * LANES:(h // 2 + 1) * LANES]
        f_h = jnp.where(_piece_lane_mask(row, h), q_pieces, 0.0)
        aug_q = jnp.where(_piece_lane_mask(row, h), 1.0, 0.0) + jnp.concatenate(
            [jnp.zeros((q_shift, tm), F32), f_h[0:HEAD_DIM - q_shift]], axis=0)
        q_h = qT[h * HEAD_DIM:(h + 1) * HEAD_DIM, :]
        if h % 2 == 0:
            ext = jnp.where(_piece_lane_mask(lane, h + HEAD_DIM), p3_hi,
                            jnp.where(_piece_lane_mask(lane, h + HEAD_DIM + q_shift), 1.0, 0.0))
            kp = jnp.where(lane < HEAD_DIM, kcol, ext)
            qp = jnp.concatenate([q_h, aug_q], axis=0)
        else:
            ext = jnp.where(_piece_lane_mask(lane, h), p3,
                            jnp.where(_piece_lane_mask(lane, h + q_shift), 1.0, 0.0))
            kp = jnp.where(lane >= HEAD_DIM, kcol, ext)
            qp = jnp.concatenate([aug_q, q_h], axis=0)
        qpT_ref[0, h] = qp.astype(BF16)
        for cidx in range(tm // tk):
            kp_ref[0, h, cidx] = kp[cidx * tk:(cidx + 1) * tk, :].astype(BF16)
            v_h = vT[h * HEAD_DIM:(h + 1) * HEAD_DIM, cidx * tk:(cidx + 1) * tk].astype(BF16)
            vT_ref[0, h, cidx] = jnp.concatenate([v_h, v_tail], axis=0)

    kn2 = jnp.max(_dot((kk * kk).astype(BF16), hsel_ref[...]), axis=0, keepdims=True)
    srow = lax.broadcasted_iota(jnp.int32, (SUBLANES, LANES), 0)
    kstat_ref[0, 0] = jnp.where(srow == 0, c[0:1, :], jnp.where(srow == 1, c[tm - 1:tm, :],
                                                                 jnp.where(srow == 2, kn2, 0.0)))
    q2 = qT * qT
    qn2 = jnp.concatenate([jnp.sum(q2[h * HEAD_DIM:(h + 1) * HEAD_DIM, :], axis=0, keepdims=True)
                           for h in range(N_HEADS)], axis=0)
    qstat_ref[0, 0] = jnp.broadcast_to(jnp.max(qn2, axis=1, keepdims=True), (N_HEADS, LANES))


def _inproj(x2, g1, wrow, bf3, wqvT, *, batch, seq, tm, tk):
    nt = seq // tm
    nk = seq // tk
    ltri = jnp.tril(jnp.ones((tm, tm), F32)).astype(BF16)
    hsel = (jnp.arange(D_ATT)[:, None] // HEAD_DIM == jnp.arange(LANES)[None, :]).astype(BF16)
    const = lambda shape: pl.BlockSpec(shape, lambda b, t: (0,) * len(shape))
    stat_spec = pl.BlockSpec((1, 1, SUBLANES, LANES), lambda b, t: (b, t, 0, 0))
    stat_shape = jax.ShapeDtypeStruct((batch, nt, SUBLANES, LANES), F32)
    return pl.pallas_call(
        functools.partial(_inproj_kernel, tm=tm, tk=tk),
        grid=(batch, nt),
        in_specs=[
            pl.BlockSpec((tm, D_MODEL), lambda b, t: (b * nt + t, 0)),
            const((1, D_MODEL)), const((D_MODEL, 2 * D_CONV + D_ATT + LANES)), const((1, LANES)),
            const((2 * D_ATT, D_MODEL)), const((tm, tm)), const((D_ATT, LANES)),
        ],
        out_specs=[
            pl.BlockSpec((tm, D_CONV), lambda b, t: (b * nt + t, 0)),
            pl.BlockSpec((1, N_HEADS, tm // tk, tk, KAUG), lambda b, t: (b, 0, t, 0, 0)),
            pl.BlockSpec((1, N_HEADS, KAUG, tm), lambda b, t: (b, 0, 0, t)),
            pl.BlockSpec((1, N_HEADS, tm // tk, V_ROWS, tk), lambda b, t: (b, 0, t, 0, 0)),
            stat_spec, stat_spec,
        ],
        out_shape=[
            jax.ShapeDtypeStruct((batch * seq, D_CONV), F32),
            jax.ShapeDtypeStruct((batch, N_HEADS, nk, tk, KAUG), BF16),
            jax.ShapeDtypeStruct((batch, N_HEADS, KAUG, seq), BF16),
            jax.ShapeDtypeStruct((batch, N_HEADS, nk, V_ROWS, tk), BF16),
            stat_shape, stat_shape,
        ],
        scratch_shapes=[pltpu.VMEM((1, LANES), F32)],
        compiler_params=pltpu.CompilerParams(
            dimension_semantics=("arbitrary", "arbitrary"), vmem_limit_bytes=VMEM_LIMIT),
        name="inproj",
    )(x2, g1, wrow, bf3, wqvT, ltri, hsel)


def _conv_tile(a_ref, w_ref, b_ref, lng_ref, lnb_ref, og_ref, sh_ref, acc_ref, *, tc, rc):
    n_cb = D_CONV // LANES

    @pl.when(pl.program_id(1) == 0)
    def _():
        for cb in range(n_cb):
            sh_ref[0, cb, 0:CONV_HALO, :] = jnp.zeros((CONV_HALO, LANES), F32)

    @pl.when(pl.program_id(1) > 0)
    def _():
        for cb in range(n_cb):
            sh_ref[0, cb, 0:CONV_HALO, :] = sh_ref[0, cb, tc:tc + CONV_HALO, :]

    n_sh = tc + CONV_HALO - SUBLANES
    for cb in range(n_cb):
        sh_ref[0, cb, CONV_HALO:CONV_HALO + tc, :] = a_ref[:, cb * LANES:(cb + 1) * LANES]
        for f in range(1, SUBLANES):
            sh_ref[f, cb, 0:n_sh, :] = sh_ref[0, cb, f:f + n_sh, :]

    base = CONV_HALO - (CONV_WIDTH - 1)
    for cb in range(n_cb):
        cols = slice(cb * LANES, (cb + 1) * LANES)

        def chunk(c, carry, cb=cb, cols=cols):
            r0 = pl.multiple_of(c * rc, rc)
            acc = jnp.zeros((rc, LANES), F32)
            for j in range(CONV_WIDTH):
                f = (base + j) % SUBLANES
                acc = acc + w_ref[j:j + 1, cols] * sh_ref[f, cb, pl.ds(r0 + (base + j - f), rc), :]
            acc_ref[pl.ds(r0, rc), cols] = acc
            return carry

        lax.fori_loop(0, tc // rc, chunk, 0)

    y = acc_ref[...] + b_ref[...]
    mu = jnp.mean(y, axis=-1, keepdims=True)
    yc = y - mu
    var = jnp.mean(yc * yc, axis=-1, keepdims=True)
    yn = yc * lax.rsqrt(var + EPS) * lng_ref[...] + lnb_ref[...]
    s = yn * jax.nn.sigmoid(yn)
    ms = jnp.mean(s * s, axis=-1, keepdims=True)
    return (s * lax.rsqrt(ms + EPS) * og_ref[...]).astype(BF16)


def _attn_fixed_ref_kernel(g0_ref, bnd_ref, qT_ref, k_ref, v_ref, o_ref, *, tq, tk, unit):
    i = pl.program_id(2)
    qT = qT_ref[0, 0]
    n_sub = tq // tk
    bnd = bnd_ref[0, 0, 0:1, 0:1]
    sb0 = unit * g0_ref[(pl.program_id(0) * N_HEADS + pl.program_id(1)) * pl.num_programs(2) + i]

    def values(sb, n):
        return jnp.concatenate([v_ref[0, 0, sb + d] for d in range(n)], axis=1)

    def probs(sb, n):
        return jnp.concatenate([jnp.exp2(_dot(k_ref[0, 0, sb + d], qT) - bnd).astype(BF16)
                                for d in range(n)], axis=0)

    def accumulate(acc, sb, n):
        for lo in range(0, n, n_sub):
            m = min(n_sub, n - lo)
            acc = acc + _dot(values(sb + lo, m), probs(sb + lo, m))
        return acc

    def loop_trip(p, acc):
        return accumulate(acc, sb0 + GROUPS_PER_TRIP * n_sub * p, GROUPS_PER_TRIP * n_sub)

    key = lax.broadcasted_iota(jnp.int32, (tk, tk), 0)
    qry = lax.broadcasted_iota(jnp.int32, (tk, tk), 1)

    def diagonal(acc):
        rows = []
        for d in range(n_sub):
            s = _dot(k_ref[0, 0, i * n_sub + d], qT[:, d * tk:])
            tiles = [jnp.zeros((tk, d * tk), BF16)] if d else []
            tiles.append(jnp.exp2(jnp.where(key <= qry, s[:, :tk], -jnp.inf) - bnd).astype(BF16))
            if d + 1 < n_sub:
                tiles.append(jnp.exp2(s[:, tk:] - bnd).astype(BF16))
            rows.append(jnp.concatenate(tiles, axis=1))
        acc = acc + _dot(values(i * n_sub, n_sub), jnp.concatenate(rows, axis=0))
        o_ref[0] = acc[:HEAD_DIM] / acc[HEAD_DIM:HEAD_DIM + 1]

    n_before = i * n_sub - sb0
    per_trip = GROUPS_PER_TRIP * n_sub
    acc = lax.fori_loop(0, n_before // per_trip, loop_trip, jnp.zeros((V_ROWS, tq), F32))
    left = n_before % per_trip

    for n in range(0, per_trip, unit):
        @pl.when(left == n)
        def _(n=n):
            diagonal(accumulate(acc, i * n_sub - n, n))


def _attn_kernel(g0_ref, qT_ref, k_ref, v_ref, o_ref, sa_ref, sb_ref, cma_ref, cmb_ref, *, tq, tk):
    i = pl.program_id(2)
    qT = qT_ref[0, 0]
    n_sub = tq // tk

    def scores(g, s_ref, cm_ref):
        for d in range(n_sub):
            s = _dot(k_ref[0, 0, g * n_sub + d], qT)
            s_ref[d] = s
            cm_ref[d] = jnp.max(s, axis=0, keepdims=True)

    def consume(g, s_ref, cm_ref, carry, nxt=None):
        m, acc = carry
        m_new = m
        for d in range(n_sub):
            m_new = jnp.maximum(m_new, cm_ref[d])
        ps = []
        for d in range(n_sub):
            if nxt is not None:
                g_n, sn_ref, cmn_ref = nxt
                s = _dot(k_ref[0, 0, g_n * n_sub + d], qT)
                sn_ref[d] = s
                cmn_ref[d] = jnp.max(s, axis=0, keepdims=True)
            ps.append(jnp.exp2(s_ref[d] - m_new).astype(BF16))
        p = jnp.concatenate(ps, axis=0)
        v = jnp.concatenate([v_ref[0, 0, g * n_sub + d] for d in range(n_sub)], axis=1)
        return m_new, jnp.exp2(m - m_new) * acc + _dot(v, p)

    key = lax.broadcasted_iota(jnp.int32, (tk, tk), 0)
    qry = lax.broadcasted_iota(jnp.int32, (tk, tk), 1)

    def consume_diag(s_ref, cm_ref, carry):
        m, acc = carry
        lanes = lambda j: slice(j * tk, (j + 1) * tk)
        causal = [jnp.where(key <= qry, s_ref[d, :, lanes(d)], -jnp.inf) for d in range(n_sub)]
        m_tiles = []
        for j in range(n_sub):
            m_j = jnp.maximum(m[:, lanes(j)], jnp.max(causal[j], axis=0, keepdims=True))
            for d in range(j):
                m_j = jnp.maximum(m_j, cm_ref[d, :, lanes(j)])
            m_tiles.append(m_j)
        m_new = jnp.concatenate(m_tiles, axis=1)
        rows = []
        for d in range(n_sub):
            tiles = [jnp.zeros((tk, d * tk), BF16)] if d else []
            tiles.append(jnp.exp2(causal[d] - m_tiles[d]).astype(BF16))
            if d + 1 < n_sub:
                rest = slice((d + 1) * tk, tq)
                tiles.append(jnp.exp2(s_ref[d, :, rest] - m_new[:, rest]).astype(BF16))
            rows.append(jnp.concatenate(tiles, axis=1))
        p = jnp.concatenate(rows, axis=0)
        v = jnp.concatenate([v_ref[0, 0, i * n_sub + d] for d in range(n_sub)], axis=1)
        acc = jnp.exp2(m - m_new) * acc + _dot(v, p)
        o_ref[0] = acc[:HEAD_DIM] / acc[HEAD_DIM:HEAD_DIM + 1]

    g0 = g0_ref[(pl.program_id(0) * N_HEADS + pl.program_id(1)) * pl.num_programs(2) + i]
    n_full = i - g0

    def pair(p, carry):
        g = g0 + 2 * p
        carry = consume(g, sa_ref, cma_ref, carry, nxt=(g + 1, sb_ref, cmb_ref))
        return consume(g + 1, sb_ref, cmb_ref, carry, nxt=(g + 2, sa_ref, cma_ref))

    scores(g0, sa_ref, cma_ref)
    init = (jnp.full((1, tq), -jnp.inf, F32), jnp.zeros((V_ROWS, tq), F32))
    carry = lax.fori_loop(0, n_full // 2, pair, init)

    @pl.when(n_full % 2 == 1)
    def _():
        consume_diag(sb_ref, cmb_ref, consume(i - 1, sa_ref, cma_ref, carry, nxt=(i, sb_ref, cmb_ref)))

    @pl.when(n_full % 2 == 0)
    def _():
        consume_diag(sa_ref, cma_ref, carry)


def _first_groups(kstat, qstat, *, tm, tq):
    per = tq // tm
    bound = jnp.sqrt(jnp.max(kstat[:, :, 2, :N_HEADS], axis=1) * jnp.max(qstat[:, :, :, 0], axis=1))
    bound = bound * NORM_SLACK
    cq = kstat[:, ::per, 0, :N_HEADS]
    ck = kstat[:, :, 1, :N_HEADS]
    d = (cq[:, :, None, :] - ck[:, None, :, :]) * LOG2E
    nq, ng = cq.shape[1], ck.shape[1]
    earlier = per * jnp.arange(nq)[None, :, None, None] > jnp.arange(ng)[None, None, :, None]
    skip = earlier & (2.0 * bound[:, None, None, :] + d < -SKIP_LOG2)
    g0 = jnp.sum(jnp.cumprod(skip.astype(jnp.int32), axis=2), axis=2)
    return jnp.transpose(g0, (0, 2, 1)).reshape(-1).astype(jnp.int32), bound


def _attention(g0, bound, qpT, kp, vT, *, batch, seq, tm, tq, tk):
    nk = seq // tk
    params = pltpu.CompilerParams(
        dimension_semantics=("arbitrary", "arbitrary", "arbitrary"), vmem_limit_bytes=VMEM_LIMIT)
    out_shape = jax.ShapeDtypeStruct((batch, D_ATT, seq), F32)
    qkv_specs = [
        pl.BlockSpec((1, 1, KAUG, tq), lambda b, h, i, g0: (b, h, 0, i)),
        pl.BlockSpec((1, 1, nk, tk, KAUG), lambda b, h, i, g0: (b, h, 0, 0, 0)),
        pl.BlockSpec((1, 1, nk, V_ROWS, tk), lambda b, h, i, g0: (b, h, 0, 0, 0)),
    ]
    out_spec = pl.BlockSpec((1, HEAD_DIM, tq), lambda b, h, i, g0: (b, h, i))

    def fixed_ref(g0, bnd, qpT, kp, vT):
        grid_spec = pltpu.PrefetchScalarGridSpec(
            num_scalar_prefetch=1, grid=(batch, N_HEADS, seq // tq),
            in_specs=[pl.BlockSpec((1, 1, SUBLANES, LANES), lambda b, h, i, g0: (b, h, 0, 0))] + qkv_specs,
            out_specs=out_spec)
        return pl.pallas_call(functools.partial(_attn_fixed_ref_kernel, tq=tq, tk=tk, unit=tm // tk),
                              grid_spec=grid_spec, out_shape=out_shape, compiler_params=params,
                              name="attention_fixed_ref")(g0, bnd, qpT, kp, vT)

    def running_max(g0, bnd, qpT, kp, vT):
        del bnd
        grid_spec = pltpu.PrefetchScalarGridSpec(
            num_scalar_prefetch=1, grid=(batch, N_HEADS, seq // tq), in_specs=qkv_specs, out_specs=out_spec,
            scratch_shapes=[pltpu.VMEM((tq // tk, tk, tq), F32), pltpu.VMEM((tq // tk, tk, tq), F32),
                            pltpu.VMEM((tq // tk, 1, tq), F32), pltpu.VMEM((tq // tk, 1, tq), F32)])
        return pl.pallas_call(functools.partial(_attn_kernel, tq=tq, tk=tk), grid_spec=grid_spec,
                              out_shape=out_shape, compiler_params=params, name="attention",
                              )(g0 // (tq // tm), qpT, kp, vT)

    bnd = jnp.broadcast_to(bound[:, :, None, None], (batch, N_HEADS, SUBLANES, LANES))
    return lax.cond(jnp.max(bound) <= FIXED_REF_MAX_BOUND, fixed_ref, running_max, g0, bnd, qpT, kp, vT)


def _outproj_kernel(x_ref, a_ref, yT_ref, wdw_ref, bdw_ref, lng_ref, lnb_ref, gc_ref, wo_ref, ga_ref,
                    g2_ref, wrT_ref, brT_ref, utri_ref, x1_ref, h2_ref, route_ref, cnt_ref,
                    carry_ref, sh_ref, acc_ref, *, tm, rc):
    first = (pl.program_id(0) == 0) & (pl.program_id(1) == 0)

    @pl.when(first)
    def _():
        carry_ref[...] = jnp.zeros_like(carry_ref)

    mc = _conv_tile(a_ref, wdw_ref, bdw_ref, lng_ref, lnb_ref, gc_ref, sh_ref, acc_ref, tc=tm, rc=rc)
    yT = yT_ref[0]
    msa = jnp.mean(yT * yT, axis=0, keepdims=True)
    yn = (yT * lax.rsqrt(msa + EPS)).T * ga_ref[...]
    x1 = x_ref[...] + _dot(jnp.concatenate([mc, yn.astype(BF16)], axis=1), wo_ref[...])
    x1_ref[...] = x1
    ms = jnp.mean(x1 * x1, axis=-1, keepdims=True)
    h2 = (x1 * lax.rsqrt(ms + EPS)) * g2_ref[...]
    h2_ref[:, :D_MODEL] = h2

    lg = _dot_nt(wrT_ref[...], h2.astype(BF16)) + brT_ref[...]
    neg = -jnp.inf
    sub = lax.broadcasted_iota(jnp.int32, (SUBLANES, tm), 0)
    big = jnp.int32(SUBLANES)
    first_of = lambda hit: jnp.min(jnp.where(hit, sub, big), axis=0, keepdims=True)

    lg1 = lg[0:SUBLANES]
    m1 = jnp.max(lg1, axis=0, keepdims=True)
    p1_sel = 1.0 / jnp.sum(jnp.exp(lg1 - m1), axis=0, keepdims=True)
    grp = first_of(lg1 == m1)

    slab = lambda g: lg[SUBLANES * (g + 1):SUBLANES * (g + 2)]
    v = slab(N_GROUPS - 1)
    for g in range(N_GROUPS - 2, -1, -1):
        v = jnp.where(grp == g, slab(g), v)
    v1 = jnp.max(v, axis=0, keepdims=True)
    j1 = first_of(v == v1)
    vv = jnp.where(sub == j1, neg, v)
    v2 = jnp.max(vv, axis=0, keepdims=True)
    j2 = first_of(vv == v2)
    e21 = jnp.exp(v2 - v1)
    w0 = p1_sel / (1.0 + e21)
    w1 = p1_sel * e21 / (1.0 + e21)

    swap = j2 < j1
    al = jnp.where(swap, j2, j1)
    bl = jnp.where(swap, j1, j2)
    wa = jnp.where(swap, w1, w0)
    wb = jnp.where(swap, w0, w1)
    cid = PAIRS_PER_GROUP * grp + ((al * (2 * EXPERTS_PER_GROUP - 1 - al)) >> 1) + (bl - al - 1)

    cls = lax.broadcasted_iota(jnp.int32, (LANES, tm), 0)
    oh = cls == cid
    cmat = jnp.where(oh, 1.0, 0.0)
    prefix = _dot(cmat.astype(BF16), utri_ref[...]) + carry_ref[...]
    rank = jnp.sum(jnp.where(oh, prefix, 0.0), axis=0, keepdims=True)
    counts = prefix[:, tm - 1:tm] + cmat[:, tm - 1:tm]
    carry_ref[...] = counts
    cnt_ref[...] = counts

    h2_ref[:, D_MODEL:] = jnp.where(cls == 0, wa, jnp.where(cls == 1, wb, 0.0)).T
    route_ref[...] = jnp.where(sub == 0, cid.astype(F32), jnp.where(sub == 1, rank, 0.0))


def _outproj(x2, a, yT, conv_params, wo, ga, g2, wrT, brT, *, batch, seq, tm, rc):
    nt = seq // tm
    utri = jnp.triu(jnp.ones((tm, tm), F32), 1).astype(BF16)
    const = lambda shape: pl.BlockSpec(shape, lambda b, t: (0,) * len(shape))
    row_spec = lambda w: pl.BlockSpec((tm, w), lambda b, t: (b * nt + t, 0))
    return pl.pallas_call(
        functools.partial(_outproj_kernel, tm=tm, rc=rc),
        grid=(batch, nt),
        in_specs=[row_spec(D_MODEL), row_spec(D_CONV),
                  pl.BlockSpec((1, D_ATT, tm), lambda b, t: (b, 0, t)),
                  const((CONV_WIDTH, D_CONV)), const((1, D_CONV)), const((1, D_CONV)), const((1, D_CONV)),
                  const((1, D_CONV)),
                  const((D_CONV + D_ATT, D_MODEL)), const((1, D_ATT)), const((1, D_MODEL)),
                  const((LANES, D_MODEL)), const((LANES, 1)), const((tm, tm))],
        out_specs=[row_spec(D_MODEL), row_spec(ROW_W),
                   pl.BlockSpec((SUBLANES, tm), lambda b, t: (0, b * nt + t)), const((LANES, 1))],
        out_shape=[jax.ShapeDtypeStruct((batch * seq, D_MODEL), F32),
                   jax.ShapeDtypeStruct((batch * seq, ROW_W), F32),
                   jax.ShapeDtypeStruct((SUBLANES, batch * seq), F32),
                   jax.ShapeDtypeStruct((LANES, 1), F32)],
        scratch_shapes=[pltpu.VMEM((LANES, 1), F32),
                        pltpu.VMEM((SUBLANES, D_CONV // LANES, tm + CONV_HALO, LANES), F32),
                        pltpu.VMEM((tm, D_CONV), F32)],
        compiler_params=pltpu.CompilerParams(
            dimension_semantics=("arbitrary", "arbitrary"), vmem_limit_bytes=VMEM_LIMIT),
        name="outproj",
    )(x2, a, yT, *conv_params, wo, ga, g2, wrT, brT, utri)


def _row_copies(src_at, dst_at, sem):
    return pltpu.make_async_copy(src_at, dst_at, sem)


def _dispatch_kernel(dest_ref, h2_ref, xs_ref, sem, *, tm):
    for g in range(tm // SUBLANES):
        for u in range(SUBLANES):
            _row_copies(h2_ref.at[g, pl.ds(u, 1)], xs_ref.at[pl.ds(dest_ref[g * SUBLANES + u], 1)],
                        sem).start(priority=u % 2)
    done = xs_ref.at[pl.ds(0, tm)]
    _row_copies(done, done, sem).wait()


def _dispatch(dest, h2, *, tm):
    t_len = h2.shape[0]
    return pl.pallas_call(
        functools.partial(_dispatch_kernel, tm=tm),
        grid=(t_len // tm,),
        in_specs=[pl.BlockSpec((tm,), lambda i: (i,), memory_space=pltpu.SMEM),
                  pl.BlockSpec((tm // SUBLANES, SUBLANES, ROW_W), lambda i: (i, 0, 0))],
        out_specs=pl.BlockSpec(memory_space=pl.ANY),
        out_shape=jax.ShapeDtypeStruct((t_len, ROW_W), F32),
        scratch_shapes=[pltpu.SemaphoreType.DMA(())],
        compiler_params=pltpu.CompilerParams(dimension_semantics=("arbitrary",), vmem_limit_bytes=VMEM_LIMIT),
        name="dispatch",
    )(dest, h2.reshape(t_len // SUBLANES, SUBLANES, ROW_W))


def _expert_kernel(blk_ref, ea_ref, eb_ref, lo_ref, hi_ref, x_ref, wgua_ref, wda_ref, wgub_ref, wdb_ref, y_ref):
    del blk_ref, ea_ref, eb_ref
    i = pl.program_id(0)
    lo = lo_ref[i]
    hi = hi_ref[i]
    first = lo // ROW_SEGMENT
    last = (hi - 1) // ROW_SEGMENT
    start = pl.multiple_of(first * ROW_SEGMENT, ROW_SEGMENT)

    def mlp(xb, wgu_ref, wd_ref):
        gu = _dot(xb, wgu_ref[0])
        g = gu[:, :D_EXPERT]
        return _dot(((g * jax.nn.sigmoid(g)) * gu[:, D_EXPERT:]).astype(BF16), wd_ref[0])

    def run(n_rows):
        rows = pl.ds(start, n_rows)
        xb = x_ref[rows, :D_MODEL].astype(BF16)
        wts = x_ref[rows, D_MODEL:]
        y = wts[:, 0:1] * mlp(xb, wgua_ref, wda_ref) + wts[:, 1:2] * mlp(xb, wgub_ref, wdb_ref)
        row = start + lax.broadcasted_iota(jnp.int32, (n_rows, 1), 0)
        mine = (row >= lo) & (row < hi)
        head = pl.ds(start, ROW_SEGMENT)

        @pl.when(lo == start)
        def _():
            y_ref[head, :] = jnp.where(mine[:ROW_SEGMENT], y[:ROW_SEGMENT], 0.0)

        @pl.when(lo > start)
        def _():
            y_ref[head, :] = jnp.where(mine[:ROW_SEGMENT], y[:ROW_SEGMENT], y_ref[head, :])

        if n_rows > ROW_SEGMENT:
            rest = pl.multiple_of(start + ROW_SEGMENT, ROW_SEGMENT)
            y_ref[pl.ds(rest, n_rows - ROW_SEGMENT), :] = jnp.where(
                mine[ROW_SEGMENT:], y[ROW_SEGMENT:], 0.0)

    for n in range(1, ROW_BLOCK // ROW_SEGMENT + 1):
        @pl.when((hi > lo) & (last - first + 1 == n))
        def _(n=n):
            run(n * ROW_SEGMENT)


def _experts(items, xs, wgu, wd):
    n_rows = xs.shape[0]
    n_items = items[0].shape[0]
    wspec = lambda shape, which: pl.BlockSpec(
        shape, lambda i, blk, ea, eb, lo, hi: ((ea, eb)[which][i], 0, 0))
    grid_spec = pltpu.PrefetchScalarGridSpec(
        num_scalar_prefetch=5,
        grid=(n_items,),
        in_specs=[pl.BlockSpec((ROW_BLOCK, ROW_W), lambda i, blk, ea, eb, lo, hi: (blk[i], 0)),
                  wspec((1, D_MODEL, 2 * D_EXPERT), 0), wspec((1, D_EXPERT, D_MODEL), 0),
                  wspec((1, D_MODEL, 2 * D_EXPERT), 1), wspec((1, D_EXPERT, D_MODEL), 1)],
        out_specs=pl.BlockSpec((ROW_BLOCK, D_MODEL), lambda i, blk, ea, eb, lo, hi: (blk[i], 0)),
    )
    return pl.pallas_call(
        _expert_kernel,
        grid_spec=grid_spec,
        out_shape=jax.ShapeDtypeStruct((n_rows, D_MODEL), F32),
        compiler_params=pltpu.CompilerParams(dimension_semantics=("arbitrary",), vmem_limit_bytes=VMEM_LIMIT),
        name="experts",
    )(*items, xs, wgu, wd, wgu, wd)


def _combine_kernel(dcur_ref, dnxt_ref, x1_ref, g_ref, ys_ref, out_ref, buf_ref, sems, *, tm):
    i = pl.program_id(0)
    slot = i % 2

    def issue(d_ref, s):
        for g in range(tm // SUBLANES):
            for u in range(SUBLANES):
                _row_copies(ys_ref.at[pl.ds(d_ref[g * SUBLANES + u], 1)], buf_ref.at[s, g, pl.ds(u, 1)],
                            sems.at[s]).start(priority=u % 2)

    @pl.when(i == 0)
    def _():
        issue(dcur_ref, 0)

    for s in range(2):
        @pl.when((i + 1 < pl.num_programs(0)) & (slot == 1 - s))
        def _(s=s):
            issue(dnxt_ref, s)

    _row_copies(buf_ref.at[slot], buf_ref.at[slot], sems.at[slot]).wait()
    x2 = x1_ref[...] + buf_ref[slot]
    ms = jnp.mean(x2 * x2, axis=-1, keepdims=True)
    out_ref[...] = (x2 * lax.rsqrt(ms + EPS)) * g_ref[...]


def _combine(dest, x1, gf, ys, *, tm):
    t_len = x1.shape[0]
    n = t_len // tm
    return pl.pallas_call(
        functools.partial(_combine_kernel, tm=tm),
        grid=(n,),
        in_specs=[pl.BlockSpec((tm,), lambda i: (i,), memory_space=pltpu.SMEM),
                  pl.BlockSpec((tm,), lambda i: (jnp.minimum(i + 1, n - 1),), memory_space=pltpu.SMEM),
                  pl.BlockSpec((tm // SUBLANES, SUBLANES, D_MODEL), lambda i: (i, 0, 0)),
                  pl.BlockSpec((1, D_MODEL), lambda i: (0, 0)),
                  pl.BlockSpec(memory_space=pl.ANY)],
        out_specs=pl.BlockSpec((tm // SUBLANES, SUBLANES, D_MODEL), lambda i: (i, 0, 0)),
        out_shape=jax.ShapeDtypeStruct((t_len // SUBLANES, SUBLANES, D_MODEL), F32),
        scratch_shapes=[pltpu.VMEM((2, tm // SUBLANES, SUBLANES, D_MODEL), F32), pltpu.SemaphoreType.DMA((2,))],
        compiler_params=pltpu.CompilerParams(dimension_semantics=("arbitrary",), vmem_limit_bytes=VMEM_LIMIT),
        name="combine",
    )(dest, dest, x1.reshape(t_len // SUBLANES, SUBLANES, D_MODEL), gf, ys)


def _layer(x, norm1_g, w_in, b_f, w_dw, b_dw, conv_ln_g, conv_ln_b, out_g_conv, out_g_att, w_out,
           norm2_g, w_r1, b_r1, w_r2, b_r2, w_gate, w_up, w_down):
    batch, seq, d = x.shape
    t_len = batch * seq
    tm = min(512, seq)
    tk = min(256, seq)
    tq = min(1024, seq)
    o2 = 2 * D_CONV
    o3, o4, o5 = o2 + D_ATT, o2 + 2 * D_ATT, o2 + 3 * D_ATT

    assert PIECE_STRIDE == N_HEADS
    pad = LANES - N_PIECES * PIECE_STRIDE
    wf = jnp.pad(jnp.tile(w_in[:, o5:], (1, N_PIECES)), ((0, 0), (0, pad)))
    wrow = jnp.concatenate([w_in[:, :o2], w_in[:, o3:o4], wf], axis=1).astype(BF16)
    wqvT = jnp.concatenate([w_in[:, o2:o3] * (HEAD_DIM ** -0.5), w_in[:, o4:o5]], axis=1).T.astype(BF16)
    bf3 = jnp.pad(jnp.tile(b_f.astype(F32).reshape(1, N_HEADS), (1, N_PIECES)), ((0, 0), (0, pad)))

    x2 = x.reshape(t_len, d)
    a, kp, qpT, vT, kstat, qstat = _inproj(x2, norm1_g.reshape(1, d), wrow, bf3, wqvT,
                                           batch=batch, seq=seq, tm=tm, tk=tk)
    conv_params = (w_dw, b_dw.reshape(1, -1), conv_ln_g.reshape(1, -1), conv_ln_b.reshape(1, -1),
                   out_g_conv.reshape(1, -1))
    g0, bound = _first_groups(kstat, qstat, tm=tm, tq=tq)
    yT = _attention(g0, bound, qpT, kp, vT, batch=batch, seq=seq, tm=tm, tq=tq, tk=tk)

    gpad = SUBLANES - N_GROUPS
    rpad = LANES - SUBLANES - N_EXPERTS
    wrT = jnp.concatenate([w_r1.T, jnp.zeros((gpad, d), F32),
                           jnp.transpose(w_r2, (0, 2, 1)).reshape(N_EXPERTS, d), jnp.zeros((rpad, d), F32)], axis=0)
    brT = jnp.concatenate([b_r1.astype(F32), jnp.full((gpad,), -jnp.inf, F32), b_r2.reshape(-1).astype(F32),
                           jnp.zeros((rpad,), F32)]).reshape(LANES, 1)
    x1, h2, route, cnt = _outproj(x2, a, yT, conv_params, w_out.astype(BF16),
                                  out_g_att.reshape(1, -1), norm2_g.reshape(1, d),
                                  wrT.astype(BF16), brT, batch=batch, seq=seq, tm=tm, rc=min(256, seq))

    i32 = jnp.int32
    lanes = jnp.arange(LANES, dtype=i32)
    pick = lambda table, idx: jnp.sum(jnp.where(idx[:, None] == lanes, table[None, :], 0), axis=1).astype(i32)
    counts = cnt[:, 0].astype(i32)
    ends = jnp.cumsum(counts).astype(i32)
    starts = ends - counts
    dest = pick(starts, route[0].astype(i32)) + route[1].astype(i32)

    n_blocks = t_len // ROW_BLOCK
    b_lo = starts // ROW_BLOCK
    n_it = jnp.where(counts > 0, (ends - 1) // ROW_BLOCK - b_lo + 1, 0)
    it_end = jnp.cumsum(n_it).astype(i32)
    it_start = it_end - n_it
    idx = jnp.arange(n_blocks + N_CLASSES, dtype=i32)
    valid = idx < it_end[-1]
    last_cls = jnp.max(jnp.where(counts > 0, lanes, 0))
    cls = jnp.where(valid, jnp.sum(it_end[None, :] <= idx[:, None], axis=1).astype(i32), last_cls)
    blk = jnp.where(valid, pick(b_lo, cls) + idx - pick(it_start, cls), n_blocks - 1)
    row0 = blk * ROW_BLOCK
    lo = jnp.where(valid, jnp.maximum(pick(starts, cls), row0) - row0, 0)
    hi = jnp.where(valid, jnp.minimum(pick(ends, cls), row0 + ROW_BLOCK) - row0, 0)
    pair_a, pair_b = [], []
    for g in range(N_GROUPS):
        for a_loc in range(EXPERTS_PER_GROUP):
            for b_loc in range(a_loc + 1, EXPERTS_PER_GROUP):
                pair_a.append(g * EXPERTS_PER_GROUP + a_loc)
                pair_b.append(g * EXPERTS_PER_GROUP + b_loc)
    cpad = [0] * (LANES - N_CLASSES)
    items = (blk.astype(i32), pick(jnp.array(pair_a + cpad, i32), cls), pick(jnp.array(pair_b + cpad, i32), cls),
             lo.astype(i32), hi.astype(i32))

    xs = _dispatch(dest, h2, tm=min(4096, t_len))
    ys = _experts(items, xs, jnp.concatenate([w_gate, w_up], axis=-1).astype(BF16), w_down.astype(BF16))
    return dest, x1, ys


def kernel(x, norm1_g, w_in, b_f, w_dw, b_dw, conv_ln_g, conv_ln_b, out_g_conv, out_g_att, w_out, norm2_g,
           w_r1, b_r1, w_r2, b_r2, w_gate, w_up, w_down, final_g):
    assert norm1_g.shape[0] == 1, "single-layer stack"
    batch, seq, d = x.shape
    dest, x1, ys = _layer(
        x, norm1_g[0], w_in[0], b_f[0], w_dw[0], b_dw[0], conv_ln_g[0], conv_ln_b[0], out_g_conv[0],
        out_g_att[0], w_out[0], norm2_g[0], w_r1[0], b_r1[0], w_r2[0], b_r2[0], w_gate[0], w_up[0], w_down[0])
    out = _combine(dest, x1, final_g.reshape(1, d), ys, tm=min(512, batch * seq))
    return out.reshape(batch, seq, d)
```

```python
import functools

import jax
import jax.numpy as jnp
from jax import lax
from jax.experimental import pallas as pl
from jax.experimental.pallas import tpu as pltpu

D_MODEL = 1024
D_CONV = 512
N_HEADS = 8
HEAD_DIM = 64
D_ATT = N_HEADS * HEAD_DIM
CONV_WIDTH = 31
N_GROUPS = 4
EXPERTS_PER_GROUP = 8
N_EXPERTS = N_GROUPS * EXPERTS_PER_GROUP
D_EXPERT = D_MODEL // 4
ROW_BLOCK = 1024
ROW_SEGMENT = 128
EPS = 1e-6

LANES = 128
SUBLANES = 8
KAUG = 128
N_PIECES = 3
PIECE_STRIDE = 8
CONV_HALO = 32
PAIRS_PER_GROUP = EXPERTS_PER_GROUP * (EXPERTS_PER_GROUP - 1) // 2
N_CLASSES = N_GROUPS * PAIRS_PER_GROUP
ROW_W = D_MODEL + LANES
SKIP_LOG2 = 160.0
FIXED_REF_MAX_BOUND = 40.0
GROUPS_PER_TRIP = 4
NORM_SLACK = 1.02
V_ROWS = 80
LOG2E = 1.4426950408889634
VMEM_LIMIT = 56 * 1024 * 1024

F32 = jnp.float32
BF16 = jnp.bfloat16


def _dot(a, b):
    return jnp.dot(a, b, preferred_element_type=F32)


def _dot_nt(a, b):
    return lax.dot_general(a, b, (((1,), (1,)), ((), ())), preferred_element_type=F32)


def _split3(x):
    hi = x.astype(BF16)
    r1 = x - hi.astype(F32)
    mid = r1.astype(BF16)
    lo = (r1 - mid.astype(F32)).astype(BF16)
    return hi.astype(F32), mid.astype(F32), lo.astype(F32)


def _piece_lane_mask(lane, h):
    return (lane == h) | (lane == h + PIECE_STRIDE) | (lane == h + 2 * PIECE_STRIDE)


def _inproj_kernel(x_ref, g1_ref, wrow_ref, bf_ref, wqvT_ref, ltri_ref, hsel_ref,
                   a_ref, kp_ref, qpT_ref, vT_ref, kstat_ref, qstat_ref, carry_ref, *, tm, tk):
    @pl.when(pl.program_id(1) == 0)
    def _():
        carry_ref[...] = jnp.zeros_like(carry_ref)

    x = x_ref[...]
    ms = jnp.mean(x * x, axis=-1, keepdims=True)
    hb = ((x * lax.rsqrt(ms + EPS)) * g1_ref[...]).astype(BF16)

    z = _dot(hb, wrow_ref[...])
    a_ref[...] = z[:, :D_CONV] * jax.nn.sigmoid(z[:, D_CONV:2 * D_CONV])

    kk = z[:, 2 * D_CONV:2 * D_CONV + D_ATT]

    zf = z[:, 2 * D_CONV + D_ATT:] + bf_ref[...]
    lf = jnp.minimum(zf, 0.0) - jnp.log1p(jnp.exp(-jnp.abs(zf)))
    lane = lax.broadcasted_iota(jnp.int32, (tm, LANES), 1)
    hi, mid, lo = _split3(lf)
    lf3 = jnp.where(lane < PIECE_STRIDE, hi,
                    jnp.where(lane < 2 * PIECE_STRIDE, mid,
                              jnp.where(lane < 3 * PIECE_STRIDE, lo, 0.0))).astype(BF16)
    cs3 = _dot(ltri_ref[...], lf3)
    c = (cs3 + pltpu.roll(cs3, LANES - PIECE_STRIDE, 1)
         + pltpu.roll(cs3, LANES - 2 * PIECE_STRIDE, 1)) + carry_ref[...]
    carry_ref[...] = c[tm - 1:tm, :]

    nhi, nmid, nlo = _split3(c * (-LOG2E))
    p3 = jnp.where(lane < PIECE_STRIDE, nhi,
                   jnp.where(lane < 2 * PIECE_STRIDE, pltpu.roll(nmid, PIECE_STRIDE, 1),
                             jnp.where(lane < 3 * PIECE_STRIDE, pltpu.roll(nlo, 2 * PIECE_STRIDE, 1), 0.0)))
    p3_hi = pltpu.roll(p3, HEAD_DIM, 1)
    q_pieces = (-p3).T[0:HEAD_DIM]
    q_shift = N_PIECES * PIECE_STRIDE

    qvT = _dot_nt(wqvT_ref[...], hb)
    qT = qvT[:D_ATT] * LOG2E
    vT = qvT[D_ATT:]
    row = lax.broadcasted_iota(jnp.int32, (HEAD_DIM, tm), 0)
    vrow = lax.broadcasted_iota(jnp.int32, (V_ROWS - HEAD_DIM, tk), 0)
    v_tail = jnp.where(vrow == 0, 1.0, 0.0).astype(BF16)

    for h in range(N_HEADS):
        kcol = kk[:, (h // 2) * LANES:(h // 2 + 1) * LANES]
        f_h = jnp.where(_piece_lane_mask(row, h), q_pieces, 0.0)
        aug_q = jnp.where(_piece_lane_mask(row, h), 1.0, 0.0) + jnp.concatenate(
            [jnp.zeros((q_shift, tm), F32), f_h[0:HEAD_DIM - q_shift]], axis=0)
        q_h = qT[h * HEAD_DIM:(h + 1) * HEAD_DIM, :]
        if h % 2 == 0:
            ext = jnp.where(_piece_lane_mask(lane, h + HEAD_DIM), p3_hi,
                            jnp.where(_piece_lane_mask(lane, h + HEAD_DIM + q_shift), 1.0, 0.0))
            kp = jnp.where(lane < HEAD_DIM, kcol, ext)
            qp = jnp.concatenate([q_h, aug_q], axis=0)
        else:
            ext = jnp.where(_piece_lane_mask(lane, h), p3,
                            jnp.where(_piece_lane_mask(lane, h + q_shift), 1.0, 0.0))
            kp = jnp.where(lane >= HEAD_DIM, kcol, ext)
            qp = jnp.concatenate([aug_q, q_h], axis=0)
        qpT_ref[0, h] = qp.astype(BF16)
        for cidx in range(tm // tk):
            kp_ref[0, h, cidx] = kp[cidx * tk:(cidx + 1) * tk, :].astype(BF16)
            v_h = vT[h * HEAD_DIM:(h + 1) * HEAD_DIM, cidx * tk:(cidx + 1) * tk].astype(BF16)
            vT_ref[0, h, cidx] = jnp.concatenate([v_h, v_tail], axis=0)

    kn2 = jnp.max(_dot((kk * kk).astype(BF16), hsel_ref[...]), axis=0, keepdims=True)
    srow = lax.broadcasted_iota(jnp.int32, (SUBLANES, LANES), 0)
    kstat_ref[0, 0] = jnp.where(srow == 0, c[0:1, :], jnp.where(srow == 1, c[tm - 1:tm, :],
                                                                 jnp.where(srow == 2, kn2, 0.0)))
    q2 = qT * qT
    qn2 = jnp.concatenate([jnp.sum(q2[h * HEAD_DIM:(h + 1) * HEAD_DIM, :], axis=0, keepdims=True)
                           for h in range(N_HEADS)], axis=0)
    qstat_ref[0, 0] = jnp.broadcast_to(jnp.max(qn2, axis=1, keepdims=True), (N_HEADS, LANES))


def _inproj(x2, g1, wrow, bf3, wqvT, *, batch, seq, tm, tk):
    nt = seq // tm
    nk = seq // tk
    ltri = jnp.tril(jnp.ones((tm, tm), F32)).astype(BF16)
    hsel = (jnp.arange(D_ATT)[:, None] // HEAD_DIM == jnp.arange(LANES)[None, :]).astype(BF16)
    const = lambda shape: pl.BlockSpec(shape, lambda b, t: (0,) * len(shape))
    stat_spec = pl.BlockSpec((1, 1, SUBLANES, LANES), lambda b, t: (b, t, 0, 0))
    stat_shape = jax.ShapeDtypeStruct((batch, nt, SUBLANES, LANES), F32)
    return pl.pallas_call(
        functools.partial(_inproj_kernel, tm=tm, tk=tk),
        grid=(batch, nt),
        in_specs=[
            pl.BlockSpec((tm, D_MODEL), lambda b, t: (b * nt + t, 0)),
            const((1, D_MODEL)), const((D_MODEL, 2 * D_CONV + D_ATT + LANES)), const((1, LANES)),
            const((2 * D_ATT, D_MODEL)), const((tm, tm)), const((D_ATT, LANES)),
        ],
        out_specs=[
            pl.BlockSpec((tm, D_CONV), lambda b, t: (b * nt + t, 0)),
            pl.BlockSpec((1, N_HEADS, tm // tk, tk, KAUG), lambda b, t: (b, 0, t, 0, 0)),
            pl.BlockSpec((1, N_HEADS, KAUG, tm), lambda b, t: (b, 0, 0, t)),
            pl.BlockSpec((1, N_HEADS, tm // tk, V_ROWS, tk), lambda b, t: (b, 0, t, 0, 0)),
            stat_spec, stat_spec,
        ],
        out_shape=[
            jax.ShapeDtypeStruct((batch * seq, D_CONV), F32),
            jax.ShapeDtypeStruct((batch, N_HEADS, nk, tk, KAUG), BF16),
            jax.ShapeDtypeStruct((batch, N_HEADS, KAUG, seq), BF16),
            jax.ShapeDtypeStruct((batch, N_HEADS, nk, V_ROWS, tk), BF16),
            stat_shape, stat_shape,
        ],
        scratch_shapes=[pltpu.VMEM((1, LANES), F32)],
        compiler_params=pltpu.CompilerParams(
            dimension_semantics=("arbitrary", "arbitrary"), vmem_limit_bytes=VMEM_LIMIT),
        name="inproj",
    )(x2, g1, wrow, bf3, wqvT, ltri, hsel)


def _conv_tile(a_ref, w_ref, b_ref, lng_ref, lnb_ref, og_ref, sh_ref, acc_ref, *, tc, rc):
    n_cb = D_CONV // LANES

    @pl.when(pl.program_id(1) == 0)
    def _():
        for cb in range(n_cb):
            sh_ref[0, cb, 0:CONV_HALO, :] = jnp.zeros((CONV_HALO, LANES), F32)

    @pl.when(pl.program_id(1) > 0)
    def _():
        for cb in range(n_cb):
            sh_ref[0, cb, 0:CONV_HALO, :] = sh_ref[0, cb, tc:tc + CONV_HALO, :]

    n_sh = tc + CONV_HALO - SUBLANES
    for cb in range(n_cb):
        sh_ref[0, cb, CONV_HALO:CONV_HALO + tc, :] = a_ref[:, cb * LANES:(cb + 1) * LANES]
        for f in range(1, SUBLANES):
            sh_ref[f, cb, 0:n_sh, :] = sh_ref[0, cb, f:f + n_sh, :]

    base = CONV_HALO - (CONV_WIDTH - 1)
    for cb in range(n_cb):
        cols = slice(cb * LANES, (cb + 1) * LANES)

        def chunk(c, carry, cb=cb, cols=cols):
            r0 = pl.multiple_of(c * rc, rc)
            acc = jnp.zeros((rc, LANES), F32)
            for j in range(CONV_WIDTH):
                f = (base + j) % SUBLANES
                acc = acc + w_ref[j:j + 1, cols] * sh_ref[f, cb, pl.ds(r0 + (base + j - f), rc), :]
            acc_ref[pl.ds(r0, rc), cols] = acc
            return carry

        lax.fori_loop(0, tc // rc, chunk, 0)

    y = acc_ref[...] + b_ref[...]
    mu = jnp.mean(y, axis=-1, keepdims=True)
    yc = y - mu
    var = jnp.mean(yc * yc, axis=-1, keepdims=True)
    yn = yc * lax.rsqrt(var + EPS) * lng_ref[...] + lnb_ref[...]
    s = yn * jax.nn.sigmoid(yn)
    ms = jnp.mean(s * s, axis=-1, keepdims=True)
    return (s * lax.rsqrt(ms + EPS) * og_ref[...]).astype(BF16)


def _attn_fixed_ref_kernel(g0_ref, bnd_ref, qT_ref, k_ref, v_ref, o_ref, *, tq, tk, unit):
    i = pl.program_id(2)
    qT = qT_ref[0, 0]
    n_sub = tq // tk
    bnd = bnd_ref[0, 0, 0:1, 0:1]
    sb0 = unit * g0_ref[(pl.program_id(0) * N_HEADS + pl.program_id(1)) * pl.num_programs(2) + i]

    def values(sb, n):
        return jnp.concatenate([v_ref[0, 0, sb + d] for d in range(n)], axis=1)

    def probs(sb, n):
        return jnp.concatenate([jnp.exp2(_dot(k_ref[0, 0, sb + d], qT) - bnd).astype(BF16)
                                for d in range(n)], axis=0)

    def accumulate(acc, sb, n):
        for lo in range(0, n, n_sub):
            m = min(n_sub, n - lo)
            acc = acc + _dot(values(sb + lo, m), probs(sb + lo, m))
        return acc

    def loop_trip(p, acc):
        return accumulate(acc, sb0 + GROUPS_PER_TRIP * n_sub * p, GROUPS_PER_TRIP * n_sub)

    key = lax.broadcasted_iota(jnp.int32, (tk, tk), 0)
    qry = lax.broadcasted_iota(jnp.int32, (tk, tk), 1)

    def diagonal(acc):
        rows = []
        for d in range(n_sub):
            s = _dot(k_ref[0, 0, i * n_sub + d], qT[:, d * tk:])
            tiles = [jnp.zeros((tk, d * tk), BF16)] if d else []
            tiles.append(jnp.exp2(jnp.where(key <= qry, s[:, :tk], -jnp.inf) - bnd).astype(BF16))
            if d + 1 < n_sub:
                tiles.append(jnp.exp2(s[:, tk:] - bnd).astype(BF16))
            rows.append(jnp.concatenate(tiles, axis=1))
        acc = acc + _dot(values(i * n_sub, n_sub), jnp.concatenate(rows, axis=0))
        o_ref[0] = acc[:HEAD_DIM] / acc[HEAD_DIM:HEAD_DIM + 1]

    n_before = i * n_sub - sb0
    per_trip = GROUPS_PER_TRIP * n_sub
    acc = lax.fori_loop(0, n_before // per_trip, loop_trip, jnp.zeros((V_ROWS, tq), F32))
    left = n_before % per_trip

    for n in range(0, per_trip, unit):
        @pl.when(left == n)
        def _(n=n):
            diagonal(accumulate(acc, i * n_sub - n, n))


def _attn_kernel(g0_ref, qT_ref, k_ref, v_ref, o_ref, sa_ref, sb_ref, cma_ref, cmb_ref, *, tq, tk):
    i = pl.program_id(2)
    qT = qT_ref[0, 0]
    n_sub = tq // tk

    def scores(g, s_ref, cm_ref):
        for d in range(n_sub):
            s = _dot(k_ref[0, 0, g * n_sub + d], qT)
            s_ref[d] = s
            cm_ref[d] = jnp.max(s, axis=0, keepdims=True)

    def consume(g, s_ref, cm_ref, carry, nxt=None):
        m, acc = carry
        m_new = m
        for d in range(n_sub):
            m_new = jnp.maximum(m_new, cm_ref[d])
        ps = []
        for d in range(n_sub):
            if nxt is not None:
                g_n, sn_ref, cmn_ref = nxt
                s = _dot(k_ref[0, 0, g_n * n_sub + d], qT)
                sn_ref[d] = s
                cmn_ref[d] = jnp.max(s, axis=0, keepdims=True)
            ps.append(jnp.exp2(s_ref[d] - m_new).astype(BF16))
        p = jnp.concatenate(ps, axis=0)
        v = jnp.concatenate([v_ref[0, 0, g * n_sub + d] for d in range(n_sub)], axis=1)
        return m_new, jnp.exp2(m - m_new) * acc + _dot(v, p)

    key = lax.broadcasted_iota(jnp.int32, (tk, tk), 0)
    qry = lax.broadcasted_iota(jnp.int32, (tk, tk), 1)

    def consume_diag(s_ref, cm_ref, carry):
        m, acc = carry
        lanes = lambda j: slice(j * tk, (j + 1) * tk)
        causal = [jnp.where(key <= qry, s_ref[d, :, lanes(d)], -jnp.inf) for d in range(n_sub)]
        m_tiles = []
        for j in range(n_sub):
            m_j = jnp.maximum(m[:, lanes(j)], jnp.max(causal[j], axis=0, keepdims=True))
            for d in range(j):
                m_j = jnp.maximum(m_j, cm_ref[d, :, lanes(j)])
            m_tiles.append(m_j)
        m_new = jnp.concatenate(m_tiles, axis=1)
        rows = []
        for d in range(n_sub):
            tiles = [jnp.zeros((tk, d * tk), BF16)] if d else []
            tiles.append(jnp.exp2(causal[d] - m_tiles[d]).astype(BF16))
            if d + 1 < n_sub:
                rest = slice((d + 1) * tk, tq)
                tiles.append(jnp.exp2(s_ref[d, :, rest] - m_new[:, rest]).astype(BF16))
            rows.append(jnp.concatenate(tiles, axis=1))
        p = jnp.concatenate(rows, axis=0)
        v = jnp.concatenate([v_ref[0, 0, i * n_sub + d] for d in range(n_sub)], axis=1)
        acc = jnp.exp2(m - m_new) * acc + _dot(v, p)
        o_ref[0] = acc[:HEAD_DIM] / acc[HEAD_DIM:HEAD_DIM + 1]

    g0 = g0_ref[(pl.program_id(0) * N_HEADS + pl.program_id(1)) * pl.num_programs(2) + i]
    n_full = i - g0

    def pair(p, carry):
        g = g0 + 2 * p
        carry = consume(g, sa_ref, cma_ref, carry, nxt=(g + 1, sb_ref, cmb_ref))
        return consume(g + 1, sb_ref, cmb_ref, carry, nxt=(g + 2, sa_ref, cma_ref))

    scores(g0, sa_ref, cma_ref)
    init = (jnp.full((1, tq), -jnp.inf, F32), jnp.zeros((V_ROWS, tq), F32))
    carry = lax.fori_loop(0, n_full // 2, pair, init)

    @pl.when(n_full % 2 == 1)
    def _():
        consume_diag(sb_ref, cmb_ref, consume(i - 1, sa_ref, cma_ref, carry, nxt=(i, sb_ref, cmb_ref)))

    @pl.when(n_full % 2 == 0)
    def _():
        consume_diag(sa_ref, cma_ref, carry)


def _first_groups(kstat, qstat, *, tm, tq):
    per = tq // tm
    bound = jnp.sqrt(jnp.max(kstat[:, :, 2, :N_HEADS], axis=1) * jnp.max(qstat[:, :, :, 0], axis=1))
    bound = bound * NORM_SLACK
    cq = kstat[:, ::per, 0, :N_HEADS]
    ck = kstat[:, :, 1, :N_HEADS]
    d = (cq[:, :, None, :] - ck[:, None, :, :]) * LOG2E
    nq, ng = cq.shape[1], ck.shape[1]
    earlier = per * jnp.arange(nq)[None, :, None, None] > jnp.arange(ng)[None, None, :, None]
    skip = earlier & (2.0 * bound[:, None, None, :] + d < -SKIP_LOG2)
    g0 = jnp.sum(jnp.cumprod(skip.astype(jnp.int32), axis=2), axis=2)
    return jnp.transpose(g0, (0, 2, 1)).reshape(-1).astype(jnp.int32), bound


def _attention(g0, bound, qpT, kp, vT, *, batch, seq, tm, tq, tk):
    nk = seq // tk
    params = pltpu.CompilerParams(
        dimension_semantics=("arbitrary", "arbitrary", "arbitrary"), vmem_limit_bytes=VMEM_LIMIT)
    out_shape = jax.ShapeDtypeStruct((batch, D_ATT, seq), F32)
    qkv_specs = [
        pl.BlockSpec((1, 1, KAUG, tq), lambda b, h, i, g0: (b, h, 0, i)),
        pl.BlockSpec((1, 1, nk, tk, KAUG), lambda b, h, i, g0: (b, h, 0, 0, 0)),
        pl.BlockSpec((1, 1, nk, V_ROWS, tk), lambda b, h, i, g0: (b, h, 0, 0, 0)),
    ]
    out_spec = pl.BlockSpec((1, HEAD_DIM, tq), lambda b, h, i, g0: (b, h, i))

    def fixed_ref(g0, bnd, qpT, kp, vT):
        grid_spec = pltpu.PrefetchScalarGridSpec(
            num_scalar_prefetch=1, grid=(batch, N_HEADS, seq // tq),
            in_specs=[pl.BlockSpec((1, 1, SUBLANES, LANES), lambda b, h, i, g0: (b, h, 0, 0))] + qkv_specs,
            out_specs=out_spec)
        return pl.pallas_call(functools.partial(_attn_fixed_ref_kernel, tq=tq, tk=tk, unit=tm // tk),
                              grid_spec=grid_spec, out_shape=out_shape, compiler_params=params,
                              name="attention_fixed_ref")(g0, bnd, qpT, kp, vT)

    def running_max(g0, bnd, qpT, kp, vT):
        del bnd
        grid_spec = pltpu.PrefetchScalarGridSpec(
            num_scalar_prefetch=1, grid=(batch, N_HEADS, seq // tq), in_specs=qkv_specs, out_specs=out_spec,
            scratch_shapes=[pltpu.VMEM((tq // tk, tk, tq), F32), pltpu.VMEM((tq // tk, tk, tq), F32),
                            pltpu.VMEM((tq // tk, 1, tq), F32), pltpu.VMEM((tq // tk, 1, tq), F32)])
        return pl.pallas_call(functools.partial(_attn_kernel, tq=tq, tk=tk), grid_spec=grid_spec,
                              out_shape=out_shape, compiler_params=params, name="attention",
                              )(g0 // (tq // tm), qpT, kp, vT)

    bnd = jnp.broadcast_to(bound[:, :, None, None], (batch, N_HEADS, SUBLANES, LANES))
    return lax.cond(jnp.max(bound) <= FIXED_REF_MAX_BOUND, fixed_ref, running_max, g0, bnd, qpT, kp, vT)


def _outproj_kernel(x_ref, a_ref, yT_ref, wdw_ref, bdw_ref, lng_ref, lnb_ref, gc_ref, wo_ref, ga_ref,
                    g2_ref, wrT_ref, brT_ref, utri_ref, x1_ref, h2_ref, route_ref, cnt_ref,
                    carry_ref, sh_ref, acc_ref, *, tm, rc):
    first = (pl.program_id(0) == 0) & (pl.program_id(1) == 0)

    @pl.when(first)
    def _():
        carry_ref[...] = jnp.zeros_like(carry_ref)

    mc = _conv_tile(a_ref, wdw_ref, bdw_ref, lng_ref, lnb_ref, gc_ref, sh_ref, acc_ref, tc=tm, rc=rc)
    yT = yT_ref[0]
    msa = jnp.mean(yT * yT, axis=0, keepdims=True)
    yn = (yT * lax.rsqrt(msa + EPS)).T * ga_ref[...]
    x1 = x_ref[...] + _dot(jnp.concatenate([mc, yn.astype(BF16)], axis=1), wo_ref[...])
    x1_ref[...] = x1
    ms = jnp.mean(x1 * x1, axis=-1, keepdims=True)
    h2 = (x1 * lax.rsqrt(ms + EPS)) * g2_ref[...]
    h2_ref[:, :D_MODEL] = h2

    lg = _dot_nt(wrT_ref[...], h2.astype(BF16)) + brT_ref[...]
    neg = -jnp.inf
    sub = lax.broadcasted_iota(jnp.int32, (SUBLANES, tm), 0)
    big = jnp.int32(SUBLANES)
    first_of = lambda hit: jnp.min(jnp.where(hit, sub, big), axis=0, keepdims=True)

    lg1 = lg[0:SUBLANES]
    m1 = jnp.max(lg1, axis=0, keepdims=True)
    p1_sel = 1.0 / jnp.sum(jnp.exp(lg1 - m1), axis=0, keepdims=True)
    grp = first_of(lg1 == m1)

    slab = lambda g: lg[SUBLANES * (g + 1):SUBLANES * (g + 2)]
    v = slab(N_GROUPS - 1)
    for g in range(N_GROUPS - 2, -1, -1):
        v = jnp.where(grp == g, slab(g), v)
    v1 = jnp.max(v, axis=0, keepdims=True)
    j1 = first_of(v == v1)
    vv = jnp.where(sub == j1, neg, v)
    v2 = jnp.max(vv, axis=0, keepdims=True)
    j2 = first_of(vv == v2)
    e21 = jnp.exp(v2 - v1)
    w0 = p1_sel / (1.0 + e21)
    w1 = p1_sel * e21 / (1.0 + e21)

    swap = j2 < j1
    al = jnp.where(swap, j2, j1)
    bl = jnp.where(swap, j1, j2)
    wa = jnp.where(swap, w1, w0)
    wb = jnp.where(swap, w0, w1)
    cid = PAIRS_PER_GROUP * grp + ((al * (2 * EXPERTS_PER_GROUP - 1 - al)) >> 1) + (bl - al - 1)

    cls = lax.broadcasted_iota(jnp.int32, (LANES, tm), 0)
    oh = cls == cid
    cmat = jnp.where(oh, 1.0, 0.0)
    prefix = _dot(cmat.astype(BF16), utri_ref[...]) + carry_ref[...]
    rank = jnp.sum(jnp.where(oh, prefix, 0.0), axis=0, keepdims=True)
    counts = prefix[:, tm - 1:tm] + cmat[:, tm - 1:tm]
    carry_ref[...] = counts
    cnt_ref[...] = counts

    h2_ref[:, D_MODEL:] = jnp.where(cls == 0, wa, jnp.where(cls == 1, wb, 0.0)).T
    route_ref[...] = jnp.where(sub == 0, cid.astype(F32), jnp.where(sub == 1, rank, 0.0))


def _outproj(x2, a, yT, conv_params, wo, ga, g2, wrT, brT, *, batch, seq, tm, rc):
    nt = seq // tm
    utri = jnp.triu(jnp.ones((tm, tm), F32), 1).astype(BF16)
    const = lambda shape: pl.BlockSpec(shape, lambda b, t: (0,) * len(shape))
    row_spec = lambda w: pl.BlockSpec((tm, w), lambda b, t: (b * nt + t, 0))
    return pl.pallas_call(
        functools.partial(_outproj_kernel, tm=tm, rc=rc),
        grid=(batch, nt),
        in_specs=[row_spec(D_MODEL), row_spec(D_CONV),
                  pl.BlockSpec((1, D_ATT, tm), lambda b, t: (b, 0, t)),
                  const((CONV_WIDTH, D_CONV)), const((1, D_CONV)), const((1, D_CONV)), const((1, D_CONV)),
                  const((1, D_CONV)),
                  const((D_CONV + D_ATT, D_MODEL)), const((1, D_ATT)), const((1, D_MODEL)),
                  const((LANES, D_MODEL)), const((LANES, 1)), const((tm, tm))],
        out_specs=[row_spec(D_MODEL), row_spec(ROW_W),
                   pl.BlockSpec((SUBLANES, tm), lambda b, t: (0, b * nt + t)), const((LANES, 1))],
        out_shape=[jax.ShapeDtypeStruct((batch * seq, D_MODEL), F32),
                   jax.ShapeDtypeStruct((batch * seq, ROW_W), F32),
                   jax.ShapeDtypeStruct((SUBLANES, batch * seq), F32),
                   jax.ShapeDtypeStruct((LANES, 1), F32)],
        scratch_shapes=[pltpu.VMEM((LANES, 1), F32),
                        pltpu.VMEM((SUBLANES, D_CONV // LANES, tm + CONV_HALO, LANES), F32),
                        pltpu.VMEM((tm, D_CONV), F32)],
        compiler_params=pltpu.CompilerParams(
            dimension_semantics=("arbitrary", "arbitrary"), vmem_limit_bytes=VMEM_LIMIT),
        name="outproj",
    )(x2, a, yT, *conv_params, wo, ga, g2, wrT, brT, utri)


def _row_copies(src_at, dst_at, sem):
    return pltpu.make_async_copy(src_at, dst_at, sem)


def _dispatch_kernel(dest_ref, h2_ref, xs_ref, sem, *, tm):
    for g in range(tm // SUBLANES):
        for u in range(SUBLANES):
            _row_copies(h2_ref.at[g, pl.ds(u, 1)], xs_ref.at[pl.ds(dest_ref[g * SUBLANES + u], 1)],
                        sem).start(priority=u % 2)
    done = xs_ref.at[pl.ds(0, tm)]
    _row_copies(done, done, sem).wait()


def _dispatch(dest, h2, *, tm):
    t_len = h2.shape[0]
    return pl.pallas_call(
        functools.partial(_dispatch_kernel, tm=tm),
        grid=(t_len // tm,),
        in_specs=[pl.BlockSpec((tm,), lambda i: (i,), memory_space=pltpu.SMEM),
                  pl.BlockSpec((tm // SUBLANES, SUBLANES, ROW_W), lambda i: (i, 0, 0))],
        out_specs=pl.BlockSpec(memory_space=pl.ANY),
        out_shape=jax.ShapeDtypeStruct((t_len, ROW_W), F32),
        scratch_shapes=[pltpu.SemaphoreType.DMA(())],
        compiler_params=pltpu.CompilerParams(dimension_semantics=("arbitrary",), vmem_limit_bytes=VMEM_LIMIT),
        name="dispatch",
    )(dest, h2.reshape(t_len // SUBLANES, SUBLANES, ROW_W))


def _expert_kernel(blk_ref, ea_ref, eb_ref, lo_ref, hi_ref, x_ref, wgua_ref, wda_ref, wgub_ref, wdb_ref, y_ref):
    del blk_ref, ea_ref, eb_ref
    i = pl.program_id(0)
    lo = lo_ref[i]
    hi = hi_ref[i]
    first = lo // ROW_SEGMENT
    last = (hi - 1) // ROW_SEGMENT
    start = pl.multiple_of(first * ROW_SEGMENT, ROW_SEGMENT)

    def mlp(xb, wgu_ref, wd_ref):
        gu = _dot(xb, wgu_ref[0])
        g = gu[:, :D_EXPERT]
        return _dot(((g * jax.nn.sigmoid(g)) * gu[:, D_EXPERT:]).astype(BF16), wd_ref[0])

    def run(n_rows):
        rows = pl.ds(start, n_rows)
        xb = x_ref[rows, :D_MODEL].astype(BF16)
        wts = x_ref[rows, D_MODEL:]
        y = wts[:, 0:1] * mlp(xb, wgua_ref, wda_ref) + wts[:, 1:2] * mlp(xb, wgub_ref, wdb_ref)
        row = start + lax.broadcasted_iota(jnp.int32, (n_rows, 1), 0)
        mine = (row >= lo) & (row < hi)
        head = pl.ds(start, ROW_SEGMENT)

        @pl.when(lo == start)
        def _():
            y_ref[head, :] = jnp.where(mine[:ROW_SEGMENT], y[:ROW_SEGMENT], 0.0)

        @pl.when(lo > start)
        def _():
            y_ref[head, :] = jnp.where(mine[:ROW_SEGMENT], y[:ROW_SEGMENT], y_ref[head, :])

        if n_rows > ROW_SEGMENT:
            rest = pl.multiple_of(start + ROW_SEGMENT, ROW_SEGMENT)
            y_ref[pl.ds(rest, n_rows - ROW_SEGMENT), :] = jnp.where(
                mine[ROW_SEGMENT:], y[ROW_SEGMENT:], 0.0)

    for n in range(1, ROW_BLOCK // ROW_SEGMENT + 1):
        @pl.when((hi > lo) & (last - first + 1 == n))
        def _(n=n):
            run(n * ROW_SEGMENT)


def _experts(items, xs, wgu, wd):
    n_rows = xs.shape[0]
    n_items = items[0].shape[0]
    wspec = lambda shape, which: pl.BlockSpec(
        shape, lambda i, blk, ea, eb, lo, hi: ((ea, eb)[which][i], 0, 0))
    grid_spec = pltpu.PrefetchScalarGridSpec(
        num_scalar_prefetch=5,
        grid=(n_items,),
        in_specs=[pl.BlockSpec((ROW_BLOCK, ROW_W), lambda i, blk, ea, eb, lo, hi: (blk[i], 0)),
                  wspec((1, D_MODEL, 2 * D_EXPERT), 0), wspec((1, D_EXPERT, D_MODEL), 0),
                  wspec((1, D_MODEL, 2 * D_EXPERT), 1), wspec((1, D_EXPERT, D_MODEL), 1)],
        out_specs=pl.BlockSpec((ROW_BLOCK, D_MODEL), lambda i, blk, ea, eb, lo, hi: (blk[i], 0)),
    )
    return pl.pallas_call(
        _expert_kernel,
        grid_spec=grid_spec,
        out_shape=jax.ShapeDtypeStruct((n_rows, D_MODEL), F32),
        compiler_params=pltpu.CompilerParams(dimension_semantics=("arbitrary",), vmem_limit_bytes=VMEM_LIMIT),
        name="experts",
    )(*items, xs, wgu, wd, wgu, wd)


def _combine_kernel(dcur_ref, dnxt_ref, x1_ref, g_ref, ys_ref, out_ref, buf_ref, sems, *, tm):
    i = pl.program_id(0)
    slot = i % 2

    def issue(d_ref, s):
        for g in range(tm // SUBLANES):
            for u in range(SUBLANES):
                _row_copies(ys_ref.at[pl.ds(d_ref[g * SUBLANES + u], 1)], buf_ref.at[s, g, pl.ds(u, 1)],
                            sems.at[s]).start(priority=u % 2)

    @pl.when(i == 0)
    def _():
        issue(dcur_ref, 0)

    for s in range(2):
        @pl.when((i + 1 < pl.num_programs(0)) & (slot == 1 - s))
        def _(s=s):
            issue(dnxt_ref, s)

    _row_copies(buf_ref.at[slot], buf_ref.at[slot], sems.at[slot]).wait()
    x2 = x1_ref[...] + buf_ref[slot]
    ms = jnp.mean(x2 * x2, axis=-1, keepdims=True)
    out_ref[...] = (x2 * lax.rsqrt(ms + EPS)) * g_ref[...]


def _combine(dest, x1, gf, ys, *, tm):
    t_len = x1.shape[0]
    n = t_len // tm
    return pl.pallas_call(
        functools.partial(_combine_kernel, tm=tm),
        grid=(n,),
        in_specs=[pl.BlockSpec((tm,), lambda i: (i,), memory_space=pltpu.SMEM),
                  pl.BlockSpec((tm,), lambda i: (jnp.minimum(i + 1, n - 1),), memory_space=pltpu.SMEM),
                  pl.BlockSpec((tm // SUBLANES, SUBLANES, D_MODEL), lambda i: (i, 0, 0)),
                  pl.BlockSpec((1, D_MODEL), lambda i: (0, 0)),
                  pl.BlockSpec(memory_space=pl.ANY)],
        out_specs=pl.BlockSpec((tm // SUBLANES, SUBLANES, D_MODEL), lambda i: (i, 0, 0)),
        out_shape=jax.ShapeDtypeStruct((t_len // SUBLANES, SUBLANES, D_MODEL), F32),
        scratch_shapes=[pltpu.VMEM((2, tm // SUBLANES, SUBLANES, D_MODEL), F32), pltpu.SemaphoreType.DMA((2,))],
        compiler_params=pltpu.CompilerParams(dimension_semantics=("arbitrary",), vmem_limit_bytes=VMEM_LIMIT),
        name="combine",
    )(dest, dest, x1.reshape(t_len // SUBLANES, SUBLANES, D_MODEL), gf, ys)


def _layer(x, norm1_g, w_in, b_f, w_dw, b_dw, conv_ln_g, conv_ln_b, out_g_conv, out_g_att, w_out,
           norm2_g, w_r1, b_r1, w_r2, b_r2, w_gate, w_up, w_down):
    batch, seq, d = x.shape
    t_len = batch * seq
    tm = min(512, seq)
    tk = min(256, seq)
    tq = min(1024, seq)
    o2 = 2 * D_CONV
    o3, o4, o5 = o2 + D_ATT, o2 + 2 * D_ATT, o2 + 3 * D_ATT

    assert PIECE_STRIDE == N_HEADS
    pad = LANES - N_PIECES * PIECE_STRIDE
    wf = jnp.pad(jnp.tile(w_in[:, o5:], (1, N_PIECES)), ((0, 0), (0, pad)))
    wrow = jnp.concatenate([w_in[:, :o2], w_in[:, o3:o4], wf], axis=1).astype(BF16)
    wqvT = jnp.concatenate([w_in[:, o2:o3] * (HEAD_DIM ** -0.5), w_in[:, o4:o5]], axis=1).T.astype(BF16)
    bf3 = jnp.pad(jnp.tile(b_f.astype(F32).reshape(1, N_HEADS), (1, N_PIECES)), ((0, 0), (0, pad)))

    x2 = x.reshape(t_len, d)
    a, kp, qpT, vT, kstat, qstat = _inproj(x2, norm1_g.reshape(1, d), wrow, bf3, wqvT,
                                           batch=batch, seq=seq, tm=tm, tk=tk)
    conv_params = (w_dw, b_dw.reshape(1, -1), conv_ln_g.reshape(1, -1), conv_ln_b.reshape(1, -1),
                   out_g_conv.reshape(1, -1))
    g0, bound = _first_groups(kstat, qstat, tm=tm, tq=tq)
    yT = _attention(g0, bound, qpT, kp, vT, batch=batch, seq=seq, tm=tm, tq=tq, tk=tk)

    gpad = SUBLANES - N_GROUPS
    rpad = LANES - SUBLANES - N_EXPERTS
    wrT = jnp.concatenate([w_r1.T, jnp.zeros((gpad, d), F32),
                           jnp.transpose(w_r2, (0, 2, 1)).reshape(N_EXPERTS, d), jnp.zeros((rpad, d), F32)], axis=0)
    brT = jnp.concatenate([b_r1.astype(F32), jnp.full((gpad,), -jnp.inf, F32), b_r2.reshape(-1).astype(F32),
                           jnp.zeros((rpad,), F32)]).reshape(LANES, 1)
    x1, h2, route, cnt = _outproj(x2, a, yT, conv_params, w_out.astype(BF16),
                                  out_g_att.reshape(1, -1), norm2_g.reshape(1, d),
                                  wrT.astype(BF16), brT, batch=batch, seq=seq, tm=tm, rc=min(256, seq))

    i32 = jnp.int32
    lanes = jnp.arange(LANES, dtype=i32)
    pick = lambda table, idx: jnp.sum(jnp.where(idx[:, None] == lanes, table[None, :], 0), axis=1).astype(i32)
    counts = cnt[:, 0].astype(i32)
    ends = jnp.cumsum(counts).astype(i32)
    starts = ends - counts
    dest = pick(starts, route[0].astype(i32)) + route[1].astype(i32)

    n_blocks = t_len // ROW_BLOCK
    b_lo = starts // ROW_BLOCK
    n_it = jnp.where(counts > 0, (ends - 1) // ROW_BLOCK - b_lo + 1, 0)
    it_end = jnp.cumsum(n_it).astype(i32)
    it_start = it_end - n_it
    idx = jnp.arange(n_blocks + N_CLASSES, dtype=i32)
    valid = idx < it_end[-1]
    last_cls = jnp.max(jnp.where(counts > 0, lanes, 0))
    cls = jnp.where(valid, jnp.sum(it_end[None, :] <= idx[:, None], axis=1).astype(i32), last_cls)
    blk = jnp.where(valid, pick(b_lo, cls) + idx - pick(it_start, cls), n_blocks - 1)
    row0 = blk * ROW_BLOCK
    lo = jnp.where(valid, jnp.maximum(pick(starts, cls), row0) - row0, 0)
    hi = jnp.where(valid, jnp.minimum(pick(ends, cls), row0 + ROW_BLOCK) - row0, 0)
    pair_a, pair_b = [], []
    for g in range(N_GROUPS):
        for a_loc in range(EXPERTS_PER_GROUP):
            for b_loc in range(a_loc + 1, EXPERTS_PER_GROUP):
                pair_a.append(g * EXPERTS_PER_GROUP + a_loc)
                pair_b.append(g * EXPERTS_PER_GROUP + b_loc)
    cpad = [0] * (LANES - N_CLASSES)
    items = (blk.astype(i32), pick(jnp.array(pair_a + cpad, i32), cls), pick(jnp.array(pair_b + cpad, i32), cls),
             lo.astype(i32), hi.astype(i32))

    xs = _dispatch(dest, h2, tm=min(4096, t_len))
    ys = _experts(items, xs, jnp.concatenate([w_gate, w_up], axis=-1).astype(BF16), w_down.astype(BF16))
    return dest, x1, ys


def kernel(x, norm1_g, w_in, b_f, w_dw, b_dw, conv_ln_g, conv_ln_b, out_g_conv, out_g_att, w_out, norm2_g,
           w_r1, b_r1, w_r2, b_r2, w_gate, w_up, w_down, final_g):
    assert norm1_g.shape[0] == 1, "single-layer stack"
    batch, seq, d = x.shape
    dest, x1, ys = _layer(
        x, norm1_g[0], w_in[0], b_f[0], w_dw[0], b_dw[0], conv_ln_g[0], conv_ln_b[0], out_g_conv[0],
        out_g_att[0], w_out[0], norm2_g[0], w_r1[0], b_r1[0], w_r2[0], b_r2[0], w_gate[0], w_up[0], w_down[0])
    out = _combine(dest, x1, final_g.reshape(1, d), ys, tm=min(1024, batch * seq))
    return out.reshape(batch, seq, d)
```

```python
import functools

import jax
import jax.numpy as jnp
from jax import lax
from jax.experimental import pallas as pl
from jax.experimental.pallas import tpu as pltpu

D_MODEL = 1024
D_CONV = 512
N_HEADS = 8
HEAD_DIM = 64
D_ATT = N_HEADS * HEAD_DIM
CONV_WIDTH = 31
N_GROUPS = 4
EXPERTS_PER_GROUP = 8
N_EXPERTS = N_GROUPS * EXPERTS_PER_GROUP
D_EXPERT = D_MODEL // 4
ROW_BLOCK = 1024
ROW_SEGMENT = 128
EPS = 1e-6

LANES = 128
SUBLANES = 8
KAUG = 128
N_PIECES = 3
PIECE_STRIDE = 8
CONV_HALO = 32
PAIRS_PER_GROUP = EXPERTS_PER_GROUP * (EXPERTS_PER_GROUP - 1) // 2
N_CLASSES = N_GROUPS * PAIRS_PER_GROUP
ROW_W = D_MODEL + LANES
SKIP_LOG2 = 160.0
FIXED_REF_MAX_BOUND = 40.0
GROUPS_PER_TRIP = 4
NORM_SLACK = 1.02
V_ROWS = 80
LOG2E = 1.4426950408889634
VMEM_LIMIT = 56 * 1024 * 1024

F32 = jnp.float32
BF16 = jnp.bfloat16


def _dot(a, b):
    return jnp.dot(a, b, preferred_element_type=F32)


def _dot_nt(a, b):
    return lax.dot_general(a, b, (((1,), (1,)), ((), ())), preferred_element_type=F32)


def _split3(x):
    hi = x.astype(BF16)
    r1 = x - hi.astype(F32)
    mid = r1.astype(BF16)
    lo = (r1 - mid.astype(F32)).astype(BF16)
    return hi.astype(F32), mid.astype(F32), lo.astype(F32)


def _piece_lane_mask(lane, h):
    return (lane == h) | (lane == h + PIECE_STRIDE) | (lane == h + 2 * PIECE_STRIDE)


def _inproj_kernel(x_ref, g1_ref, wrow_ref, bf_ref, wqvT_ref, ltri_ref, hsel_ref,
                   a_ref, kp_ref, qpT_ref, vT_ref, kstat_ref, qstat_ref, carry_ref, *, tm, tk):
    @pl.when(pl.program_id(1) == 0)
    def _():
        carry_ref[...] = jnp.zeros_like(carry_ref)

    x = x_ref[...]
    ms = jnp.mean(x * x, axis=-1, keepdims=True)
    hb = ((x * lax.rsqrt(ms + EPS)) * g1_ref[...]).astype(BF16)

    z = _dot(hb, wrow_ref[...])
    a_ref[...] = z[:, :D_CONV] * jax.nn.sigmoid(z[:, D_CONV:2 * D_CONV])

    kk = z[:, 2 * D_CONV:2 * D_CONV + D_ATT]

    zf = z[:, 2 * D_CONV + D_ATT:] + bf_ref[...]
    lf = jnp.minimum(zf, 0.0) - jnp.log1p(jnp.exp(-jnp.abs(zf)))
    lane = lax.broadcasted_iota(jnp.int32, (tm, LANES), 1)
    hi, mid, lo = _split3(lf)
    lf3 = jnp.where(lane < PIECE_STRIDE, hi,
                    jnp.where(lane < 2 * PIECE_STRIDE, mid,
                              jnp.where(lane < 3 * PIECE_STRIDE, lo, 0.0))).astype(BF16)
    cs3 = _dot(ltri_ref[...], lf3)
    c = (cs3 + pltpu.roll(cs3, LANES - PIECE_STRIDE, 1)
         + pltpu.roll(cs3, LANES - 2 * PIECE_STRIDE, 1)) + carry_ref[...]
    carry_ref[...] = c[tm - 1:tm, :]

    nhi, nmid, nlo = _split3(c * (-LOG2E))
    p3 = jnp.where(lane < PIECE_STRIDE, nhi,
                   jnp.where(lane < 2 * PIECE_STRIDE, pltpu.roll(nmid, PIECE_STRIDE, 1),
                             jnp.where(lane < 3 * PIECE_STRIDE, pltpu.roll(nlo, 2 * PIECE_STRIDE, 1), 0.0)))
    p3_hi = pltpu.roll(p3, HEAD_DIM, 1)
    q_pieces = (-p3).T[0:HEAD_DIM]
    q_shift = N_PIECES * PIECE_STRIDE

    qvT = _dot_nt(wqvT_ref[...], hb)
    qT = qvT[:D_ATT] * LOG2E
    vT = qvT[D_ATT:]
    row = lax.broadcasted_iota(jnp.int32, (HEAD_DIM, tm), 0)
    vrow = lax.broadcasted_iota(jnp.int32, (V_ROWS - HEAD_DIM, tk), 0)
    v_tail = jnp.where(vrow == 0, 1.0, 0.0).astype(BF16)

    for h in range(N_HEADS):
        kcol = kk[:, (h // 2) * LANES:(h // 2 + 1) * LANES]
        f_h = jnp.where(_piece_lane_mask(row, h), q_pieces, 0.0)
        aug_q = jnp.where(_piece_lane_mask(row, h), 1.0, 0.0) + jnp.concatenate(
            [jnp.zeros((q_shift, tm), F32), f_h[0:HEAD_DIM - q_shift]], axis=0)
        q_h = qT[h * HEAD_DIM:(h + 1) * HEAD_DIM, :]
        if h % 2 == 0:
            ext = jnp.where(_piece_lane_mask(lane, h + HEAD_DIM), p3_hi,
                            jnp.where(_piece_lane_mask(lane, h + HEAD_DIM + q_shift), 1.0, 0.0))
            kp = jnp.where(lane < HEAD_DIM, kcol, ext)
            qp = jnp.concatenate([q_h, aug_q], axis=0)
        else:
            ext = jnp.where(_piece_lane_mask(lane, h), p3,
                            jnp.where(_piece_lane_mask(lane, h + q_shift), 1.0, 0.0))
            kp = jnp.where(lane >= HEAD_DIM, kcol, ext)
            qp = jnp.concatenate([aug_q, q_h], axis=0)
        qpT_ref[0, h] = qp.astype(BF16)
        for cidx in range(tm // tk):
            kp_ref[0, h, cidx] = kp[cidx * tk:(cidx + 1) * tk, :].astype(BF16)
            v_h = vT[h * HEAD_DIM:(h + 1) * HEAD_DIM, cidx * tk:(cidx + 1) * tk].astype(BF16)
            vT_ref[0, h, cidx] = jnp.concatenate([v_h, v_tail], axis=0)

    kn2 = jnp.max(_dot((kk * kk).astype(BF16), hsel_ref[...]), axis=0, keepdims=True)
    srow = lax.broadcasted_iota(jnp.int32, (SUBLANES, LANES), 0)
    kstat_ref[0, 0] = jnp.where(srow == 0, c[0:1, :], jnp.where(srow == 1, c[tm - 1:tm, :],
                                                                 jnp.where(srow == 2, kn2, 0.0)))
    q2 = qT * qT
    qn2 = jnp.concatenate([jnp.sum(q2[h * HEAD_DIM:(h + 1) * HEAD_DIM, :], axis=0, keepdims=True)
                           for h in range(N_HEADS)], axis=0)
    qstat_ref[0, 0] = jnp.broadcast_to(jnp.max(qn2, axis=1, keepdims=True), (N_HEADS, LANES))


def _inproj(x2, g1, wrow, bf3, wqvT, *, batch, seq, tm, tk):
    nt = seq // tm
    nk = seq // tk
    ltri = jnp.tril(jnp.ones((tm, tm), F32)).astype(BF16)
    hsel = (jnp.arange(D_ATT)[:, None] // HEAD_DIM == jnp.arange(LANES)[None, :]).astype(BF16)
    const = lambda shape: pl.BlockSpec(shape, lambda b, t: (0,) * len(shape))
    stat_spec = pl.BlockSpec((1, 1, SUBLANES, LANES), lambda b, t: (b, t, 0, 0))
    stat_shape = jax.ShapeDtypeStruct((batch, nt, SUBLANES, LANES), F32)
    return pl.pallas_call(
        functools.partial(_inproj_kernel, tm=tm, tk=tk),
        grid=(batch, nt),
        in_specs=[
            pl.BlockSpec((tm, D_MODEL), lambda b, t: (b * nt + t, 0)),
            const((1, D_MODEL)), const((D_MODEL, 2 * D_CONV + D_ATT + LANES)), const((1, LANES)),
            const((2 * D_ATT, D_MODEL)), const((tm, tm)), const((D_ATT, LANES)),
        ],
        out_specs=[
            pl.BlockSpec((tm, D_CONV), lambda b, t: (b * nt + t, 0)),
            pl.BlockSpec((1, N_HEADS, tm // tk, tk, KAUG), lambda b, t: (b, 0, t, 0, 0)),
            pl.BlockSpec((1, N_HEADS, KAUG, tm), lambda b, t: (b, 0, 0, t)),
            pl.BlockSpec((1, N_HEADS, tm // tk, V_ROWS, tk), lambda b, t: (b, 0, t, 0, 0)),
            stat_spec, stat_spec,
        ],
        out_shape=[
            jax.ShapeDtypeStruct((batch * seq, D_CONV), F32),
            jax.ShapeDtypeStruct((batch, N_HEADS, nk, tk, KAUG), BF16),
            jax.ShapeDtypeStruct((batch, N_HEADS, KAUG, seq), BF16),
            jax.ShapeDtypeStruct((batch, N_HEADS, nk, V_ROWS, tk), BF16),
            stat_shape, stat_shape,
        ],
        scratch_shapes=[pltpu.VMEM((1, LANES), F32)],
        compiler_params=pltpu.CompilerParams(
            dimension_semantics=("arbitrary", "arbitrary"), vmem_limit_bytes=VMEM_LIMIT),
        name="inproj",
    )(x2, g1, wrow, bf3, wqvT, ltri, hsel)


def _conv_tile(a_ref, w_ref, b_ref, lng_ref, lnb_ref, og_ref, sh_ref, acc_ref, *, tc, rc):
    n_cb = D_CONV // LANES

    @pl.when(pl.program_id(1) == 0)
    def _():
        for cb in range(n_cb):
            sh_ref[0, cb, 0:CONV_HALO, :] = jnp.zeros((CONV_HALO, LANES), F32)

    @pl.when(pl.program_id(1) > 0)
    def _():
        for cb in range(n_cb):
            sh_ref[0, cb, 0:CONV_HALO, :] = sh_ref[0, cb, tc:tc + CONV_HALO, :]

    n_sh = tc + CONV_HALO - SUBLANES
    for cb in range(n_cb):
        sh_ref[0, cb, CONV_HALO:CONV_HALO + tc, :] = a_ref[:, cb * LANES:(cb + 1) * LANES]
        for f in range(1, SUBLANES):
            sh_ref[f, cb, 0:n_sh, :] = sh_ref[0, cb, f:f + n_sh, :]

    base = CONV_HALO - (CONV_WIDTH - 1)
    for cb in range(n_cb):
        cols = slice(cb * LANES, (cb + 1) * LANES)

        def chunk(c, carry, cb=cb, cols=cols):
            r0 = pl.multiple_of(c * rc, rc)
            acc = jnp.zeros((rc, LANES), F32)
            for j in range(CONV_WIDTH):
                f = (base + j) % SUBLANES
                acc = acc + w_ref[j:j + 1, cols] * sh_ref[f, cb, pl.ds(r0 + (base + j - f), rc), :]
            acc_ref[pl.ds(r0, rc), cols] = acc
            return carry

        lax.fori_loop(0, tc // rc, chunk, 0)

    y = acc_ref[...] + b_ref[...]
    mu = jnp.mean(y, axis=-1, keepdims=True)
    yc = y - mu
    var = jnp.mean(yc * yc, axis=-1, keepdims=True)
    yn = yc * lax.rsqrt(var + EPS) * lng_ref[...] + lnb_ref[...]
    s = yn * jax.nn.sigmoid(yn)
    ms = jnp.mean(s * s, axis=-1, keepdims=True)
    return (s * lax.rsqrt(ms + EPS) * og_ref[...]).astype(BF16)


def _attn_fixed_ref_kernel(g0_ref, bnd_ref, qT_ref, k_ref, v_ref, o_ref, *, tq, tk, unit):
    i = pl.program_id(2)
    qT = qT_ref[0, 0]
    n_sub = tq // tk
    bnd = bnd_ref[0, 0, 0:1, 0:1]
    sb0 = unit * g0_ref[(pl.program_id(0) * N_HEADS + pl.program_id(1)) * pl.num_programs(2) + i]

    def values(sb, n):
        return jnp.concatenate([v_ref[0, 0, sb + d] for d in range(n)], axis=1)

    def probs(sb, n):
        return jnp.concatenate([jnp.exp2(_dot(k_ref[0, 0, sb + d], qT) - bnd).astype(BF16)
                                for d in range(n)], axis=0)

    def accumulate(acc, sb, n):
        for lo in range(0, n, n_sub):
            m = min(n_sub, n - lo)
            acc = acc + _dot(values(sb + lo, m), probs(sb + lo, m))
        return acc

    def loop_trip(p, acc):
        return accumulate(acc, sb0 + GROUPS_PER_TRIP * n_sub * p, GROUPS_PER_TRIP * n_sub)

    key = lax.broadcasted_iota(jnp.int32, (tk, tk), 0)
    qry = lax.broadcasted_iota(jnp.int32, (tk, tk), 1)

    def diagonal(acc):
        rows = []
        for d in range(n_sub):
            s = _dot(k_ref[0, 0, i * n_sub + d], qT[:, d * tk:])
            tiles = [jnp.zeros((tk, d * tk), BF16)] if d else []
            tiles.append(jnp.exp2(jnp.where(key <= qry, s[:, :tk], -jnp.inf) - bnd).astype(BF16))
            if d + 1 < n_sub:
                tiles.append(jnp.exp2(s[:, tk:] - bnd).astype(BF16))
            rows.append(jnp.concatenate(tiles, axis=1))
        acc = acc + _dot(values(i * n_sub, n_sub), jnp.concatenate(rows, axis=0))
        o_ref[0] = acc[:HEAD_DIM] / acc[HEAD_DIM:HEAD_DIM + 1]

    n_before = i * n_sub - sb0
    per_trip = GROUPS_PER_TRIP * n_sub
    acc = lax.fori_loop(0, n_before // per_trip, loop_trip, jnp.zeros((V_ROWS, tq), F32))
    left = n_before % per_trip

    for n in range(0, per_trip, unit):
        @pl.when(left == n)
        def _(n=n):
            diagonal(accumulate(acc, i * n_sub - n, n))


def _attn_kernel(g0_ref, qT_ref, k_ref, v_ref, o_ref, sa_ref, sb_ref, cma_ref, cmb_ref, *, tq, tk):
    i = pl.program_id(2)
    qT = qT_ref[0, 0]
    n_sub = tq // tk

    def scores(g, s_ref, cm_ref):
        for d in range(n_sub):
            s = _dot(k_ref[0, 0, g * n_sub + d], qT)
            s_ref[d] = s
            cm_ref[d] = jnp.max(s, axis=0, keepdims=True)

    def consume(g, s_ref, cm_ref, carry, nxt=None):
        m, acc = carry
        m_new = m
        for d in range(n_sub):
            m_new = jnp.maximum(m_new, cm_ref[d])
        ps = []
        for d in range(n_sub):
            if nxt is not None:
                g_n, sn_ref, cmn_ref = nxt
                s = _dot(k_ref[0, 0, g_n * n_sub + d], qT)
                sn_ref[d] = s
                cmn_ref[d] = jnp.max(s, axis=0, keepdims=True)
            ps.append(jnp.exp2(s_ref[d] - m_new).astype(BF16))
        p = jnp.concatenate(ps, axis=0)
        v = jnp.concatenate([v_ref[0, 0, g * n_sub + d] for d in range(n_sub)], axis=1)
        return m_new, jnp.exp2(m - m_new) * acc + _dot(v, p)

    key = lax.broadcasted_iota(jnp.int32, (tk, tk), 0)
    qry = lax.broadcasted_iota(jnp.int32, (tk, tk), 1)

    def consume_diag(s_ref, cm_ref, carry):
        m, acc = carry
        lanes = lambda j: slice(j * tk, (j + 1) * tk)
        causal = [jnp.where(key <= qry, s_ref[d, :, lanes(d)], -jnp.inf) for d in range(n_sub)]
        m_tiles = []
        for j in range(n_sub):
            m_j = jnp.maximum(m[:, lanes(j)], jnp.max(causal[j], axis=0, keepdims=True))
            for d in range(j):
                m_j = jnp.maximum(m_j, cm_ref[d, :, lanes(j)])
            m_tiles.append(m_j)
        m_new = jnp.concatenate(m_tiles, axis=1)
        rows = []
        for d in range(n_sub):
            tiles = [jnp.zeros((tk, d * tk), BF16)] if d else []
            tiles.append(jnp.exp2(causal[d] - m_tiles[d]).astype(BF16))
            if d + 1 < n_sub:
                rest = slice((d + 1) * tk, tq)
                tiles.append(jnp.exp2(s_ref[d, :, rest] - m_new[:, rest]).astype(BF16))
            rows.append(jnp.concatenate(tiles, axis=1))
        p = jnp.concatenate(rows, axis=0)
        v = jnp.concatenate([v_ref[0, 0, i * n_sub + d] for d in range(n_sub)], axis=1)
        acc = jnp.exp2(m - m_new) * acc + _dot(v, p)
        o_ref[0] = acc[:HEAD_DIM] / acc[HEAD_DIM:HEAD_DIM + 1]

    g0 = g0_ref[(pl.program_id(0) * N_HEADS + pl.program_id(1)) * pl.num_programs(2) + i]
    n_full = i - g0

    def pair(p, carry):
        g = g0 + 2 * p
        carry = consume(g, sa_ref, cma_ref, carry, nxt=(g + 1, sb_ref, cmb_ref))
        return consume(g + 1, sb_ref, cmb_ref, carry, nxt=(g + 2, sa_ref, cma_ref))

    scores(g0, sa_ref, cma_ref)
    init = (jnp.full((1, tq), -jnp.inf, F32), jnp.zeros((V_ROWS, tq), F32))
    carry = lax.fori_loop(0, n_full // 2, pair, init)

    @pl.when(n_full % 2 == 1)
    def _():
        consume_diag(sb_ref, cmb_ref, consume(i - 1, sa_ref, cma_ref, carry, nxt=(i, sb_ref, cmb_ref)))

    @pl.when(n_full % 2 == 0)
    def _():
        consume_diag(sa_ref, cma_ref, carry)


def _first_groups(kstat, qstat, *, tm, tq):
    per = tq // tm
    bound = jnp.sqrt(jnp.max(kstat[:, :, 2, :N_HEADS], axis=1) * jnp.max(qstat[:, :, :, 0], axis=1))
    bound = bound * NORM_SLACK
    cq = kstat[:, ::per, 0, :N_HEADS]
    ck = kstat[:, :, 1, :N_HEADS]
    d = (cq[:, :, None, :] - ck[:, None, :, :]) * LOG2E
    nq, ng = cq.shape[1], ck.shape[1]
    earlier = per * jnp.arange(nq)[None, :, None, None] > jnp.arange(ng)[None, None, :, None]
    skip = earlier & (2.0 * bound[:, None, None, :] + d < -SKIP_LOG2)
    g0 = jnp.sum(jnp.cumprod(skip.astype(jnp.int32), axis=2), axis=2)
    return jnp.transpose(g0, (0, 2, 1)).reshape(-1).astype(jnp.int32), bound


def _attention(g0, bound, qpT, kp, vT, *, batch, seq, tm, tq, tk):
    nk = seq // tk
    params = pltpu.CompilerParams(
        dimension_semantics=("arbitrary", "arbitrary", "arbitrary"), vmem_limit_bytes=VMEM_LIMIT)
    out_shape = jax.ShapeDtypeStruct((batch, D_ATT, seq), F32)
    qkv_specs = [
        pl.BlockSpec((1, 1, KAUG, tq), lambda b, h, i, g0: (b, h, 0, i)),
        pl.BlockSpec((1, 1, nk, tk, KAUG), lambda b, h, i, g0: (b, h, 0, 0, 0)),
        pl.BlockSpec((1, 1, nk, V_ROWS, tk), lambda b, h, i, g0: (b, h, 0, 0, 0)),
    ]
    out_spec = pl.BlockSpec((1, HEAD_DIM, tq), lambda b, h, i, g0: (b, h, i))

    def fixed_ref(g0, bnd, qpT, kp, vT):
        grid_spec = pltpu.PrefetchScalarGridSpec(
            num_scalar_prefetch=1, grid=(batch, N_HEADS, seq // tq),
            in_specs=[pl.BlockSpec((1, 1, SUBLANES, LANES), lambda b, h, i, g0: (b, h, 0, 0))] + qkv_specs,
            out_specs=out_spec)
        return pl.pallas_call(functools.partial(_attn_fixed_ref_kernel, tq=tq, tk=tk, unit=tm // tk),
                              grid_spec=grid_spec, out_shape=out_shape, compiler_params=params,
                              name="attention_fixed_ref")(g0, bnd, qpT, kp, vT)

    def running_max(g0, bnd, qpT, kp, vT):
        del bnd
        grid_spec = pltpu.PrefetchScalarGridSpec(
            num_scalar_prefetch=1, grid=(batch, N_HEADS, seq // tq), in_specs=qkv_specs, out_specs=out_spec,
            scratch_shapes=[pltpu.VMEM((tq // tk, tk, tq), F32), pltpu.VMEM((tq // tk, tk, tq), F32),
                            pltpu.VMEM((tq // tk, 1, tq), F32), pltpu.VMEM((tq // tk, 1, tq), F32)])
        return pl.pallas_call(functools.partial(_attn_kernel, tq=tq, tk=tk), grid_spec=grid_spec,
                              out_shape=out_shape, compiler_params=params, name="attention",
                              )(g0 // (tq // tm), qpT, kp, vT)

    bnd = jnp.broadcast_to(bound[:, :, None, None], (batch, N_HEADS, SUBLANES, LANES))
    return lax.cond(jnp.max(bound) <= FIXED_REF_MAX_BOUND, fixed_ref, running_max, g0, bnd, qpT, kp, vT)


def _outproj_kernel(x_ref, a_ref, yT_ref, wdw_ref, bdw_ref, lng_ref, lnb_ref, gc_ref, wo_ref, ga_ref,
                    g2_ref, wrT_ref, brT_ref, utri_ref, x1_ref, h2_ref, route_ref, cnt_ref,
                    carry_ref, sh_ref, acc_ref, *, tm, rc):
    first = (pl.program_id(0) == 0) & (pl.program_id(1) == 0)

    @pl.when(first)
    def _():
        carry_ref[...] = jnp.zeros_like(carry_ref)

    mc = _conv_tile(a_ref, wdw_ref, bdw_ref, lng_ref, lnb_ref, gc_ref, sh_ref, acc_ref, tc=tm, rc=rc)
    yT = yT_ref[0]
    msa = jnp.mean(yT * yT, axis=0, keepdims=True)
    yn = (yT * lax.rsqrt(msa + EPS)).T * ga_ref[...]
    x1 = x_ref[...] + _dot(jnp.concatenate([mc, yn.astype(BF16)], axis=1), wo_ref[...])
    x1_ref[...] = x1
    ms = jnp.mean(x1 * x1, axis=-1, keepdims=True)
    h2 = (x1 * lax.rsqrt(ms + EPS)) * g2_ref[...]
    h2_ref[:, :D_MODEL] = h2

    lg = _dot_nt(wrT_ref[...], h2.astype(BF16)) + brT_ref[...]
    neg = -jnp.inf
    sub = lax.broadcasted_iota(jnp.int32, (SUBLANES, tm), 0)
    big = jnp.int32(SUBLANES)
    first_of = lambda hit: jnp.min(jnp.where(hit, sub, big), axis=0, keepdims=True)

    lg1 = lg[0:SUBLANES]
    m1 = jnp.max(lg1, axis=0, keepdims=True)
    p1_sel = 1.0 / jnp.sum(jnp.exp(lg1 - m1), axis=0, keepdims=True)
    grp = first_of(lg1 == m1)

    slab = lambda g: lg[SUBLANES * (g + 1):SUBLANES * (g + 2)]
    v = slab(N_GROUPS - 1)
    for g in range(N_GROUPS - 2, -1, -1):
        v = jnp.where(grp == g, slab(g), v)
    v1 = jnp.max(v, axis=0, keepdims=True)
    j1 = first_of(v == v1)
    vv = jnp.where(sub == j1, neg, v)
    v2 = jnp.max(vv, axis=0, keepdims=True)
    j2 = first_of(vv == v2)
    e21 = jnp.exp(v2 - v1)
    w0 = p1_sel / (1.0 + e21)
    w1 = p1_sel * e21 / (1.0 + e21)

    swap = j2 < j1
    al = jnp.where(swap, j2, j1)
    bl = jnp.where(swap, j1, j2)
    wa = jnp.where(swap, w1, w0)
    wb = jnp.where(swap, w0, w1)
    cid = PAIRS_PER_GROUP * grp + ((al * (2 * EXPERTS_PER_GROUP - 1 - al)) >> 1) + (bl - al - 1)

    cls = lax.broadcasted_iota(jnp.int32, (LANES, tm), 0)
    oh = cls == cid
    cmat = jnp.where(oh, 1.0, 0.0)
    prefix = _dot(cmat.astype(BF16), utri_ref[...]) + carry_ref[...]
    rank = jnp.sum(jnp.where(oh, prefix, 0.0), axis=0, keepdims=True)
    counts = prefix[:, tm - 1:tm] + cmat[:, tm - 1:tm]
    carry_ref[...] = counts
    cnt_ref[...] = counts

    h2_ref[:, D_MODEL:] = jnp.where(cls == 0, wa, jnp.where(cls == 1, wb, 0.0)).T
    route_ref[...] = jnp.where(sub == 0, cid.astype(F32), jnp.where(sub == 1, rank, 0.0))


def _outproj(x2, a, yT, conv_params, wo, ga, g2, wrT, brT, *, batch, seq, tm, rc):
    nt = seq // tm
    utri = jnp.triu(jnp.ones((tm, tm), F32), 1).astype(BF16)
    const = lambda shape: pl.BlockSpec(shape, lambda b, t: (0,) * len(shape))
    row_spec = lambda w: pl.BlockSpec((tm, w), lambda b, t: (b * nt + t, 0))
    return pl.pallas_call(
        functools.partial(_outproj_kernel, tm=tm, rc=rc),
        grid=(batch, nt),
        in_specs=[row_spec(D_MODEL), row_spec(D_CONV),
                  pl.BlockSpec((1, D_ATT, tm), lambda b, t: (b, 0, t)),
                  const((CONV_WIDTH, D_CONV)), const((1, D_CONV)), const((1, D_CONV)), const((1, D_CONV)),
                  const((1, D_CONV)),
                  const((D_CONV + D_ATT, D_MODEL)), const((1, D_ATT)), const((1, D_MODEL)),
                  const((LANES, D_MODEL)), const((LANES, 1)), const((tm, tm))],
        out_specs=[row_spec(D_MODEL), row_spec(ROW_W),
                   pl.BlockSpec((SUBLANES, tm), lambda b, t: (0, b * nt + t)), const((LANES, 1))],
        out_shape=[jax.ShapeDtypeStruct((batch * seq, D_MODEL), F32),
                   jax.ShapeDtypeStruct((batch * seq, ROW_W), F32),
                   jax.ShapeDtypeStruct((SUBLANES, batch * seq), F32),
                   jax.ShapeDtypeStruct((LANES, 1), F32)],
        scratch_shapes=[pltpu.VMEM((LANES, 1), F32),
                        pltpu.VMEM((SUBLANES, D_CONV // LANES, tm + CONV_HALO, LANES), F32),
                        pltpu.VMEM((tm, D_CONV), F32)],
        compiler_params=pltpu.CompilerParams(
            dimension_semantics=("arbitrary", "arbitrary"), vmem_limit_bytes=VMEM_LIMIT),
        name="outproj",
    )(x2, a, yT, *conv_params, wo, ga, g2, wrT, brT, utri)


def _row_copies(src_at, dst_at, sem):
    return pltpu.make_async_copy(src_at, dst_at, sem)


def _dispatch_kernel(dest_ref, h2_ref, xs_ref, sem, *, tm):
    g0 = pl.program_id(0) * (tm // SUBLANES)
    for g in range(tm // SUBLANES):
        for u in range(SUBLANES):
            _row_copies(h2_ref.at[g0 + g, pl.ds(u, 1)], xs_ref.at[pl.ds(dest_ref[g * SUBLANES + u], 1)],
                        sem).start(priority=u % 2)
    done = xs_ref.at[pl.ds(0, tm)]
    _row_copies(done, done, sem).wait()


def _dispatch(dest, h2, *, tm):
    t_len = h2.shape[0]
    return pl.pallas_call(
        functools.partial(_dispatch_kernel, tm=tm),
        grid=(t_len // tm,),
        in_specs=[pl.BlockSpec((tm,), lambda i: (i,), memory_space=pltpu.SMEM),
                  pl.BlockSpec(memory_space=pl.ANY)],
        out_specs=pl.BlockSpec(memory_space=pl.ANY),
        out_shape=jax.ShapeDtypeStruct((t_len, ROW_W), F32),
        scratch_shapes=[pltpu.SemaphoreType.DMA(())],
        compiler_params=pltpu.CompilerParams(dimension_semantics=("arbitrary",), vmem_limit_bytes=VMEM_LIMIT),
        name="dispatch",
    )(dest, h2.reshape(t_len // SUBLANES, SUBLANES, ROW_W))


def _expert_kernel(blk_ref, ea_ref, eb_ref, lo_ref, hi_ref, x_ref, wgua_ref, wda_ref, wgub_ref, wdb_ref, y_ref):
    del blk_ref, ea_ref, eb_ref
    i = pl.program_id(0)
    lo = lo_ref[i]
    hi = hi_ref[i]
    first = lo // ROW_SEGMENT
    last = (hi - 1) // ROW_SEGMENT
    start = pl.multiple_of(first * ROW_SEGMENT, ROW_SEGMENT)

    def mlp(xb, wgu_ref, wd_ref):
        gu = _dot(xb, wgu_ref[0])
        g = gu[:, :D_EXPERT]
        return _dot(((g * jax.nn.sigmoid(g)) * gu[:, D_EXPERT:]).astype(BF16), wd_ref[0])

    def run(n_rows):
        rows = pl.ds(start, n_rows)
        xb = x_ref[rows, :D_MODEL].astype(BF16)
        wts = x_ref[rows, D_MODEL:]
        y = wts[:, 0:1] * mlp(xb, wgua_ref, wda_ref) + wts[:, 1:2] * mlp(xb, wgub_ref, wdb_ref)
        row = start + lax.broadcasted_iota(jnp.int32, (n_rows, 1), 0)
        mine = (row >= lo) & (row < hi)
        head = pl.ds(start, ROW_SEGMENT)

        @pl.when(lo == start)
        def _():
            y_ref[head, :] = jnp.where(mine[:ROW_SEGMENT], y[:ROW_SEGMENT], 0.0)

        @pl.when(lo > start)
        def _():
            y_ref[head, :] = jnp.where(mine[:ROW_SEGMENT], y[:ROW_SEGMENT], y_ref[head, :])

        if n_rows > ROW_SEGMENT:
            rest = pl.multiple_of(start + ROW_SEGMENT, ROW_SEGMENT)
            y_ref[pl.ds(rest, n_rows - ROW_SEGMENT), :] = jnp.where(
                mine[ROW_SEGMENT:], y[ROW_SEGMENT:], 0.0)

    for n in range(1, ROW_BLOCK // ROW_SEGMENT + 1):
        @pl.when((hi > lo) & (last - first + 1 == n))
        def _(n=n):
            run(n * ROW_SEGMENT)


def _experts(items, xs, wgu, wd):
    n_rows = xs.shape[0]
    n_items = items[0].shape[0]
    wspec = lambda shape, which: pl.BlockSpec(
        shape, lambda i, blk, ea, eb, lo, hi: ((ea, eb)[which][i], 0, 0))
    grid_spec = pltpu.PrefetchScalarGridSpec(
        num_scalar_prefetch=5,
        grid=(n_items,),
        in_specs=[pl.BlockSpec((ROW_BLOCK, ROW_W), lambda i, blk, ea, eb, lo, hi: (blk[i], 0)),
                  wspec((1, D_MODEL, 2 * D_EXPERT), 0), wspec((1, D_EXPERT, D_MODEL), 0),
                  wspec((1, D_MODEL, 2 * D_EXPERT), 1), wspec((1, D_EXPERT, D_MODEL), 1)],
        out_specs=pl.BlockSpec((ROW_BLOCK, D_MODEL), lambda i, blk, ea, eb, lo, hi: (blk[i], 0)),
    )
    return pl.pallas_call(
        _expert_kernel,
        grid_spec=grid_spec,
        out_shape=jax.ShapeDtypeStruct((n_rows, D_MODEL), F32),
        compiler_params=pltpu.CompilerParams(dimension_semantics=("arbitrary",), vmem_limit_bytes=VMEM_LIMIT),
        name="experts",
    )(*items, xs, wgu, wd, wgu, wd)


def _combine_kernel(dcur_ref, dnxt_ref, x1_ref, g_ref, ys_ref, out_ref, buf_ref, sems, *, tm):
    i = pl.program_id(0)
    slot = i % 2

    def issue(d_ref, s):
        for g in range(tm // SUBLANES):
            for u in range(SUBLANES):
                _row_copies(ys_ref.at[pl.ds(d_ref[g * SUBLANES + u], 1)], buf_ref.at[s, g, pl.ds(u, 1)],
                            sems.at[s]).start(priority=u % 2)

    @pl.when(i == 0)
    def _():
        issue(dcur_ref, 0)

    for s in range(2):
        @pl.when((i + 1 < pl.num_programs(0)) & (slot == 1 - s))
        def _(s=s):
            issue(dnxt_ref, s)

    _row_copies(buf_ref.at[slot], buf_ref.at[slot], sems.at[slot]).wait()
    x2 = x1_ref[...] + buf_ref[slot]
    ms = jnp.mean(x2 * x2, axis=-1, keepdims=True)
    out_ref[...] = (x2 * lax.rsqrt(ms + EPS)) * g_ref[...]


def _combine(dest, x1, gf, ys, *, tm):
    t_len = x1.shape[0]
    n = t_len // tm
    return pl.pallas_call(
        functools.partial(_combine_kernel, tm=tm),
        grid=(n,),
        in_specs=[pl.BlockSpec((tm,), lambda i: (i,), memory_space=pltpu.SMEM),
                  pl.BlockSpec((tm,), lambda i: (jnp.minimum(i + 1, n - 1),), memory_space=pltpu.SMEM),
                  pl.BlockSpec((tm // SUBLANES, SUBLANES, D_MODEL), lambda i: (i, 0, 0)),
                  pl.BlockSpec((1, D_MODEL), lambda i: (0, 0)),
                  pl.BlockSpec(memory_space=pl.ANY)],
        out_specs=pl.BlockSpec((tm // SUBLANES, SUBLANES, D_MODEL), lambda i: (i, 0, 0)),
        out_shape=jax.ShapeDtypeStruct((t_len // SUBLANES, SUBLANES, D_MODEL), F32),
        scratch_shapes=[pltpu.VMEM((2, tm // SUBLANES, SUBLANES, D_MODEL), F32), pltpu.SemaphoreType.DMA((2,))],
        compiler_params=pltpu.CompilerParams(dimension_semantics=("arbitrary",), vmem_limit_bytes=VMEM_LIMIT),
        name="combine",
    )(dest, dest, x1.reshape(t_len // SUBLANES, SUBLANES, D_MODEL), gf, ys)


def _layer(x, norm1_g, w_in, b_f, w_dw, b_dw, conv_ln_g, conv_ln_b, out_g_conv, out_g_att, w_out,
           norm2_g, w_r1, b_r1, w_r2, b_r2, w_gate, w_up, w_down):
    batch, seq, d = x.shape
    t_len = batch * seq
    tm = min(512, seq)
    tk = min(256, seq)
    tq = min(1024, seq)
    o2 = 2 * D_CONV
    o3, o4, o5 = o2 + D_ATT, o2 + 2 * D_ATT, o2 + 3 * D_ATT

    assert PIECE_STRIDE == N_HEADS
    pad = LANES - N_PIECES * PIECE_STRIDE
    wf = jnp.pad(jnp.tile(w_in[:, o5:], (1, N_PIECES)), ((0, 0), (0, pad)))
    wrow = jnp.concatenate([w_in[:, :o2], w_in[:, o3:o4], wf], axis=1).astype(BF16)
    wqvT = jnp.concatenate([w_in[:, o2:o3] * (HEAD_DIM ** -0.5), w_in[:, o4:o5]], axis=1).T.astype(BF16)
    bf3 = jnp.pad(jnp.tile(b_f.astype(F32).reshape(1, N_HEADS), (1, N_PIECES)), ((0, 0), (0, pad)))

    x2 = x.reshape(t_len, d)
    a, kp, qpT, vT, kstat, qstat = _inproj(x2, norm1_g.reshape(1, d), wrow, bf3, wqvT,
                                           batch=batch, seq=seq, tm=tm, tk=tk)
    conv_params = (w_dw, b_dw.reshape(1, -1), conv_ln_g.reshape(1, -1), conv_ln_b.reshape(1, -1),
                   out_g_conv.reshape(1, -1))
    g0, bound = _first_groups(kstat, qstat, tm=tm, tq=tq)
    yT = _attention(g0, bound, qpT, kp, vT, batch=batch, seq=seq, tm=tm, tq=tq, tk=tk)

    gpad = SUBLANES - N_GROUPS
    rpad = LANES - SUBLANES - N_EXPERTS
    wrT = jnp.concatenate([w_r1.T, jnp.zeros((gpad, d), F32),
                           jnp.transpose(w_r2, (0, 2, 1)).reshape(N_EXPERTS, d), jnp.zeros((rpad, d), F32)], axis=0)
    brT = jnp.concatenate([b_r1.astype(F32), jnp.full((gpad,), -jnp.inf, F32), b_r2.reshape(-1).astype(F32),
                           jnp.zeros((rpad,), F32)]).reshape(LANES, 1)
    x1, h2, route, cnt = _outproj(x2, a, yT, conv_params, w_out.astype(BF16),
                                  out_g_att.reshape(1, -1), norm2_g.reshape(1, d),
                                  wrT.astype(BF16), brT, batch=batch, seq=seq, tm=tm, rc=min(256, seq))

    i32 = jnp.int32
    lanes = jnp.arange(LANES, dtype=i32)
    pick = lambda table, idx: jnp.sum(jnp.where(idx[:, None] == lanes, table[None, :], 0), axis=1).astype(i32)
    counts = cnt[:, 0].astype(i32)
    ends = jnp.cumsum(counts).astype(i32)
    starts = ends - counts
    dest = pick(starts, route[0].astype(i32)) + route[1].astype(i32)

    n_blocks = t_len // ROW_BLOCK
    b_lo = starts // ROW_BLOCK
    n_it = jnp.where(counts > 0, (ends - 1) // ROW_BLOCK - b_lo + 1, 0)
    it_end = jnp.cumsum(n_it).astype(i32)
    it_start = it_end - n_it
    idx = jnp.arange(n_blocks + N_CLASSES, dtype=i32)
    valid = idx < it_end[-1]
    last_cls = jnp.max(jnp.where(counts > 0, lanes, 0))
    cls = jnp.where(valid, jnp.sum(it_end[None, :] <= idx[:, None], axis=1).astype(i32), last_cls)
    blk = jnp.where(valid, pick(b_lo, cls) + idx - pick(it_start, cls), n_blocks - 1)
    row0 = blk * ROW_BLOCK
    lo = jnp.where(valid, jnp.maximum(pick(starts, cls), row0) - row0, 0)
    hi = jnp.where(valid, jnp.minimum(pick(ends, cls), row0 + ROW_BLOCK) - row0, 0)
    pair_a, pair_b = [], []
    for g in range(N_GROUPS):
        for a_loc in range(EXPERTS_PER_GROUP):
            for b_loc in range(a_loc + 1, EXPERTS_PER_GROUP):
                pair_a.append(g * EXPERTS_PER_GROUP + a_loc)
                pair_b.append(g * EXPERTS_PER_GROUP + b_loc)
    cpad = [0] * (LANES - N_CLASSES)
    items = (blk.astype(i32), pick(jnp.array(pair_a + cpad, i32), cls), pick(jnp.array(pair_b + cpad, i32), cls),
             lo.astype(i32), hi.astype(i32))

    xs = _dispatch(dest, h2, tm=min(4096, t_len))
    ys = _experts(items, xs, jnp.concatenate([w_gate, w_up], axis=-1).astype(BF16), w_down.astype(BF16))
    return dest, x1, ys


def kernel(x, norm1_g, w_in, b_f, w_dw, b_dw, conv_ln_g, conv_ln_b, out_g_conv, out_g_att, w_out, norm2_g,
           w_r1, b_r1, w_r2, b_r2, w_gate, w_up, w_down, final_g):
    assert norm1_g.shape[0] == 1, "single-layer stack"
    batch, seq, d = x.shape
    dest, x1, ys = _layer(
        x, norm1_g[0], w_in[0], b_f[0], w_dw[0], b_dw[0], conv_ln_g[0], conv_ln_b[0], out_g_conv[0],
        out_g_att[0], w_out[0], norm2_g[0], w_r1[0], b_r1[0], w_r2[0], b_r2[0], w_gate[0], w_up[0], w_down[0])
    out = _combine(dest, x1, final_g.reshape(1, d), ys, tm=min(512, batch * seq))
    return out.reshape(batch, seq, d)
```
